```python
import math
import jax, jax.numpy as jnp
from jax import lax
import numpy as np

D_MODEL = 1024
BATCH = 4
SEQ = 8192
DEPTH = 1

N_META = 16
CONV_WIDTH = D_MODEL // 2
CONV_K = 3
N_HEADS = 4
HEAD_DIM = 64
V_DIM = 2 * HEAD_DIM
QK_WIDTH = N_HEADS * 2 * HEAD_DIM
ATTN_WIDTH = N_HEADS * V_DIM
MIX_WIDTH = CONV_WIDTH + ATTN_WIDTH
IN_COLS = 3 * CONV_WIDTH + 2 * QK_WIDTH + ATTN_WIDTH
ROPE_DIM = HEAD_DIM // 4
ROPE_THETA = 500000.0
Q_BLOCK = 128
N_GROUPS = 4
EXPERTS_PER_GROUP = 8
N_EXPERTS = N_GROUPS * EXPERTS_PER_GROUP
TOP_K = 2
D_FF_EXPERT = D_MODEL // 2
MOE_BLOCK = 256
EPS = 1e-6

kernel_name = "hymba_conv_diffattn_hiermoe_layer"


def rms_norm(x, g):
    xf = x.astype(jnp.float32)
    y = xf * lax.rsqrt(jnp.mean(xf * xf, axis=-1, keepdims=True) + EPS)
    return y.astype(x.dtype) * g.astype(x.dtype)


def partial_rope(t, cos, sin):
    half = ROPE_DIM // 2
    r1 = t[..., :half]
    r2 = t[..., half:ROPE_DIM]
    rest = t[..., ROPE_DIM:]
    return jnp.concatenate([r1 * cos - r2 * sin, r2 * cos + r1 * sin, rest], axis=-1)


def causal_depthwise_conv(z, w):
    return lax.conv_general_dilated(
        z, w[:, None, :].astype(z.dtype), window_strides=(1,),
        padding=[(CONV_K - 1, 0)], dimension_numbers=("NWC", "WIO", "NWC"),
        feature_group_count=z.shape[-1])


def diff_attention(q, k, v, lam):
    _, b, h, lp, _ = q.shape
    nb = lp // Q_BLOCK
    scale = HEAD_DIM ** -0.5
    qb = q.reshape(2, b, h, nb, Q_BLOCK, HEAD_DIM).transpose(3, 0, 1, 2, 4, 5)
    kpos = jnp.arange(lp)

    def one_block(args):
        qblk, i = args
        qpos = i * Q_BLOCK + jnp.arange(Q_BLOCK)
        mask = qpos[:, None] >= kpos[None, :]
        s = jnp.einsum("nbhqd,nbhkd->nbhqk", qblk, k).astype(jnp.float32) * scale
        p = jax.nn.softmax(jnp.where(mask, s, -jnp.inf), axis=-1)
        w = p[0] - lam * p[1]
        return jnp.einsum("bhqk,bhkd->bhqd", w.astype(v.dtype), v)

    out = lax.map(one_block, (qb, jnp.arange(nb)))
    return out.transpose(1, 2, 0, 3, 4).reshape(b, h, lp, V_DIM)


def hier_moe(t, w_rg, b_rg, w_re, b_re, w_gate, w_up, w_down):
    n = t.shape[0]
    tf = t.astype(jnp.float32)
    g_prob = jax.nn.softmax(tf @ w_rg.astype(jnp.float32) + b_rg.astype(jnp.float32), axis=-1)
    g_val, g_idx = lax.top_k(g_prob, 1)
    e_logits = (tf @ w_re.astype(jnp.float32) + b_re.astype(jnp.float32)).reshape(n, N_GROUPS, EXPERTS_PER_GROUP)
    e_sel = jnp.take_along_axis(e_logits, g_idx[:, :, None], axis=1)[:, 0]
    e_val, e_idx = lax.top_k(jax.nn.softmax(e_sel, axis=-1), TOP_K)
    gates = (g_val * e_val / jnp.sum(e_val, axis=-1, keepdims=True)).reshape(-1).astype(t.dtype)
    expert_ids = (g_idx * EXPERTS_PER_GROUP + e_idx).reshape(-1).astype(jnp.int32)
    token_ids = jnp.repeat(jnp.arange(n, dtype=jnp.int32), TOP_K)

    a = n * TOP_K
    n_blocks = -(-a // MOE_BLOCK) + N_EXPERTS
    p_rows = n_blocks * MOE_BLOCK
    order = jnp.argsort(expert_ids, stable=True)
    sorted_e = expert_ids[order]
    counts = jnp.bincount(expert_ids, length=N_EXPERTS)
    padded = (counts + MOE_BLOCK - 1) // MOE_BLOCK * MOE_BLOCK
    starts = jnp.cumsum(counts) - counts
    pends = jnp.cumsum(padded)
    pstarts = pends - padded
    dest = pstarts[sorted_e] + (jnp.arange(a) - starts[sorted_e])
    buf_tok = jnp.full((p_rows,), n, jnp.int32).at[dest].set(token_ids[order])
    buf_gate = jnp.zeros((p_rows,), t.dtype).at[dest].set(gates[order])
    block_e = jnp.minimum(jnp.searchsorted(pends, jnp.arange(n_blocks) * MOE_BLOCK, side="right"),
                          N_EXPERTS - 1)
    t_pad = jnp.concatenate([t, jnp.zeros((1, t.shape[1]), t.dtype)], axis=0)
    xb = t_pad[buf_tok].reshape(n_blocks, MOE_BLOCK, t.shape[1])

    def expert_block(args):
        xe, e = args
        hid = jax.nn.silu(xe @ w_gate[e]) * (xe @ w_up[e])
        return hid @ w_down[e]

    yb = lax.map(expert_block, (xb, block_e)).reshape(p_rows, t.shape[1])
    out = jnp.zeros((n + 1, t.shape[1]), t.dtype).at[buf_tok].add(yb * buf_gate[:, None])
    return out[:n]


def setup_inputs(seed: int = 0) -> dict:
    key = jax.random.key(seed)
    ks = jax.random.split(key, 24)
    f32 = jnp.float32
    nrm = lambda k, shape, s: jax.random.normal(k, shape, f32) * s
    return {
        "x": nrm(ks[0], (BATCH, SEQ, D_MODEL), 1.0),
        "meta_tokens": nrm(ks[1], (N_META, D_MODEL), 1.0),
        "norm1_g": 1.0 + nrm(ks[2], (DEPTH, D_MODEL), 0.02),
        "w_in": nrm(ks[3], (DEPTH, D_MODEL, IN_COLS), D_MODEL ** -0.5),
        "conv_w": nrm(ks[4], (DEPTH, CONV_K, CONV_WIDTH), CONV_K ** -0.5),
        "q_norm_g": 1.0 + nrm(ks[5], (DEPTH, HEAD_DIM), 0.02),
        "k_norm_g": 1.0 + nrm(ks[6], (DEPTH, HEAD_DIM), 0.02),
        "lambda_q1": nrm(ks[7], (DEPTH, HEAD_DIM), 0.1),
        "lambda_k1": nrm(ks[8], (DEPTH, HEAD_DIM), 0.1),
        "lambda_q2": nrm(ks[9], (DEPTH, HEAD_DIM), 0.1),
        "lambda_k2": nrm(ks[10], (DEPTH, HEAD_DIM), 0.1),
        "subln_g": 1.0 + nrm(ks[11], (DEPTH, V_DIM), 0.02),
        "w_out": nrm(ks[12], (DEPTH, MIX_WIDTH, D_MODEL), MIX_WIDTH ** -0.5),
        "norm2_g": 1.0 + nrm(ks[13], (DEPTH, D_MODEL), 0.02),
        "w_router_group": nrm(ks[14], (DEPTH, D_MODEL, N_GROUPS), D_MODEL ** -0.5),
        "b_router_group": nrm(ks[15], (DEPTH, N_GROUPS), 0.01),
        "w_router_expert": nrm(ks[16], (DEPTH, D_MODEL, N_EXPERTS), D_MODEL ** -0.5),
        "b_router_expert": nrm(ks[17], (DEPTH, N_EXPERTS), 0.01),
        "w_gate": nrm(ks[18], (DEPTH, N_EXPERTS, D_MODEL, D_FF_EXPERT), D_MODEL ** -0.5),
        "w_up": nrm(ks[19], (DEPTH, N_EXPERTS, D_MODEL, D_FF_EXPERT), D_MODEL ** -0.5),
        "w_down": nrm(ks[20], (DEPTH, N_EXPERTS, D_FF_EXPERT, D_MODEL), D_FF_EXPERT ** -0.5),
    }


def reference(x, meta_tokens, norm1_g, w_in, conv_w, q_norm_g, k_norm_g, lambda_q1, lambda_k1,
              lambda_q2, lambda_k2, subln_g, w_out, norm2_g, w_router_group, b_router_group,
              w_router_expert, b_router_expert, w_gate, w_up, w_down):
    b, s, d = x.shape
    h = jnp.concatenate([jnp.broadcast_to(meta_tokens[None].astype(x.dtype), (b, N_META, d)), x], axis=1)
    length = s + N_META
    lp = -(-length // Q_BLOCK) * Q_BLOCK

    pos = jnp.arange(length, dtype=jnp.float32)
    inv_freq = ROPE_THETA ** (-jnp.arange(0, ROPE_DIM, 2, dtype=jnp.float32) / ROPE_DIM)
    ang = pos[:, None] * inv_freq[None, :]
    cos = jnp.cos(ang).astype(x.dtype)
    sin = jnp.sin(ang).astype(x.dtype)

    for l in range(DEPTH):
        lam_init = 0.8 - 0.6 * math.exp(-0.3 * l)
        xn = rms_norm(h, norm1_g[l])
        u = xn @ w_in[l]
        c_b, c_c, c_x, q, k, v = jnp.split(
            u, [CONV_WIDTH, 2 * CONV_WIDTH, 3 * CONV_WIDTH, 3 * CONV_WIDTH + QK_WIDTH,
                3 * CONV_WIDTH + 2 * QK_WIDTH], axis=-1)
        conv_y = c_b * causal_depthwise_conv(c_c * c_x, conv_w[l])

        q = q.reshape(b, length, N_HEADS, 2, HEAD_DIM).transpose(3, 0, 2, 1, 4)
        k = k.reshape(b, length, N_HEADS, 2, HEAD_DIM).transpose(3, 0, 2, 1, 4)
        q = partial_rope(rms_norm(q, q_norm_g[l]), cos, sin)
        k = partial_rope(rms_norm(k, k_norm_g[l]), cos, sin)
        v = v.reshape(b, length, N_HEADS, V_DIM).transpose(0, 2, 1, 3)
        pad = lp - length
        q = jnp.pad(q, ((0, 0), (0, 0), (0, 0), (0, pad), (0, 0)))
        k = jnp.pad(k, ((0, 0), (0, 0), (0, 0), (0, pad), (0, 0)))
        v = jnp.pad(v, ((0, 0), (0, 0), (0, pad), (0, 0)))
        lam = (jnp.exp(jnp.sum(lambda_q1[l].astype(jnp.float32) * lambda_k1[l].astype(jnp.float32)))
               - jnp.exp(jnp.sum(lambda_q2[l].astype(jnp.float32) * lambda_k2[l].astype(jnp.float32)))
               + lam_init)
        o = diff_attention(q, k, v, lam)[:, :, :length]
        o = rms_norm(o, subln_g[l]) * (1.0 - lam_init)
        o = o.transpose(0, 2, 1, 3).reshape(b, length, ATTN_WIDTH)
        h = h + jnp.concatenate([conv_y, o], axis=-1) @ w_out[l]

        xn2 = rms_norm(h, norm2_g[l]).reshape(b * length, d)
        h = h + hier_moe(xn2, w_router_group[l], b_router_group[l], w_router_expert[l],
                         b_router_expert[l], w_gate[l], w_up[l], w_down[l]).reshape(b, length, d)

    return h[:, N_META:]
```

```python
import functools
import math

import jax
import jax.numpy as jnp
from jax import lax
from jax.experimental import pallas as pl
from jax.experimental.pallas import tpu as pltpu

F32 = jnp.float32
BF16 = jnp.bfloat16

N_META = 16
CONV_K = 3
N_HEADS = 4
HEAD_DIM = 64
V_DIM = 2 * HEAD_DIM
ROPE_DIM = HEAD_DIM // 4
ROPE_THETA = 500000.0
N_GROUPS = 4
EXPERTS_PER_GROUP = 8
N_EXPERTS = N_GROUPS * EXPERTS_PER_GROUP
TOP_K = 2
EPS = 1e-6
LOG2E = 1.4426950408889634

LANES = 128
TOKEN_TILE = 640
MOE_BLOCK = 256
ROUTE_COLS = 8
EXPERT_LANE0 = N_GROUPS
NEG_BIG = -1e30
VMEM_LIMIT = 56 * 1024 * 1024


def _largest_tile(n, cap, mult):
    for t in range(min(cap, n), 0, -1):
        if n % t == 0 and t % mult == 0:
            return t
    raise ValueError(f"no tile for {n}")


def _cparams(n_axes):
    return pltpu.CompilerParams(dimension_semantics=("arbitrary",) * n_axes,
                                vmem_limit_bytes=VMEM_LIMIT)


def _inproj_kernel(x_ref, g1_ref, win_ref, convw_ref, gq_ref, gk_ref, bd_ref, rc_ref, rs1_ref, rs2_ref,
                   convy_ref, q_ref, k_ref, v_ref, carry_ref, *, tiles_per_seq, cw, qw):
    i = pl.program_id(0)
    tm = x_ref.shape[0]
    x = x_ref[...]
    ms = jnp.mean(x * x, axis=-1, keepdims=True)
    xn = (x * lax.rsqrt(ms + EPS) * g1_ref[...]).astype(BF16)
    u = jnp.dot(xn, win_ref[...], preferred_element_type=F32)

    z = u[:, cw:2 * cw] * u[:, 2 * cw:3 * cw]

    @pl.when(i % tiles_per_seq == 0)
    def _():
        carry_ref[...] = jnp.zeros_like(carry_ref)

    prev = carry_ref[...]
    p1 = prev[7:8]
    p2 = prev[6:7]
    row = lax.broadcasted_iota(jnp.int32, z.shape, 0)
    z1 = jnp.where(row == 0, p1, pltpu.roll(z, 1, axis=0))
    z2 = jnp.where(row == 0, p2, jnp.where(row == 1, p1, pltpu.roll(z, 2, axis=0)))
    carry_ref[...] = z[tm - 8:tm]
    w = convw_ref[...]
    conv = w[0:1] * z2 + w[1:2] * z1 + w[2:3] * z
    convy_ref[...] = (u[:, 0:cw] * conv).astype(BF16)

    rc = rc_ref[...]
    rs1 = rs1_ref[...]
    rs2 = rs2_ref[...]

    def norm_rope(t, g_ref):
        ss = jnp.dot((t * t).astype(BF16), bd_ref[...], preferred_element_type=F32)
        tn = t * lax.rsqrt(ss * (1.0 / HEAD_DIM) + EPS) * g_ref[...]
        outs = []
        for c in range(qw // LANES):
            ch = tn[:, c * LANES:(c + 1) * LANES]
            outs.append(ch * rc + pltpu.roll(ch, ROPE_DIM // 2, axis=1) * rs1
                        + pltpu.roll(ch, LANES - ROPE_DIM // 2, axis=1) * rs2)
        return jnp.concatenate(outs, axis=1).astype(BF16)

    q0 = 3 * cw
    q_ref[...] = norm_rope(u[:, q0:q0 + qw], gq_ref)
    k_ref[...] = norm_rope(u[:, q0 + qw:q0 + 2 * qw], gk_ref)
    v_ref[...] = u[:, q0 + 2 * qw:].astype(BF16)


def _inproj(hp, g1, w_in, conv_w, gq, gk, bd, rc, rs1, rs2, *, tiles_per_seq, tm):
    n, d = hp.shape
    cw = conv_w.shape[1]
    qw = gq.shape[1]
    aw = w_in.shape[1] - 3 * cw - 2 * qw
    const = lambda i: (0, 0)
    tile = lambda i: (i, 0)
    pos = lambda i: (i % tiles_per_seq, 0)
    kern = functools.partial(_inproj_kernel, tiles_per_seq=tiles_per_seq, cw=cw, qw=qw)
    return pl.pallas_call(
        kern,
        grid=(n // tm,),
        in_specs=[
            pl.BlockSpec((tm, d), tile),
            pl.BlockSpec((1, d), const),
            pl.BlockSpec(w_in.shape, const),
            pl.BlockSpec(conv_w.shape, const),
            pl.BlockSpec((1, qw), const),
            pl.BlockSpec((1, qw), const),
            pl.BlockSpec(bd.shape, const),
            pl.BlockSpec((tm, LANES), pos),
            pl.BlockSpec((tm, LANES), pos),
            pl.BlockSpec((tm, LANES), pos),
        ],
        out_specs=[
            pl.BlockSpec((tm, cw), tile),
            pl.BlockSpec((tm, qw), tile),
            pl.BlockSpec((tm, qw), tile),
            pl.BlockSpec((tm, aw), tile),
        ],
        out_shape=[
            jax.ShapeDtypeStruct((n, cw), BF16),
            jax.ShapeDtypeStruct((n, qw), BF16),
            jax.ShapeDtypeStruct((n, qw), BF16),
            jax.ShapeDtypeStruct((n, aw), BF16),
        ],
        scratch_shapes=[pltpu.VMEM((8, cw), F32)],
        compiler_params=_cparams(1),
        name="inproj",
    )(hp, g1, w_in, conv_w, gq, gk, bd, rc, rs1, rs2)


def _attn_kernel(q_ref, k_ref, v_ref, lamp_ref, sg_ref, o_ref, qs_ref, m_ref, l_ref, acc_ref, *, lam_init):
    qi = pl.program_id(2)
    tq = q_ref.shape[0]
    q = q_ref[...]
    lane = lax.broadcasted_iota(jnp.int32, q.shape, 1)
    zero = jnp.zeros_like(q)
    qs_ref[0:tq, :] = jnp.where(lane < HEAD_DIM, q, zero)
    qs_ref[tq:2 * tq, :] = jnp.where(lane >= HEAD_DIM, q, zero)
    m_ref[...] = jnp.full_like(m_ref, NEG_BIG)
    l_ref[...] = jnp.zeros_like(l_ref)
    acc_ref[...] = jnp.zeros_like(acc_ref)

    def chunk(j, masked):
        off = pl.multiple_of(j * tq, tq)
        kc = k_ref[pl.ds(off, tq), :]
        vc = v_ref[pl.ds(off, tq), :]
        s = lax.dot_general(qs_ref[...], kc, (((1,), (1,)), ((), ())), preferred_element_type=F32)
        if masked:
            r = lax.broadcasted_iota(jnp.int32, s.shape, 0)
            r = jnp.where(r >= tq, r - tq, r)
            c = lax.broadcasted_iota(jnp.int32, s.shape, 1)
            s = jnp.where(c <= r, s, NEG_BIG)
        m_prev = m_ref[...]
        m_new = jnp.maximum(m_prev, jnp.max(s, axis=-1, keepdims=True))
        alpha = jnp.exp2(m_prev - m_new)
        p = jnp.exp2(s - jnp.tile(m_new, (1, tq // LANES)))
        l_ref[...] = alpha * l_ref[...] + jnp.sum(p, axis=-1, keepdims=True)
        acc_ref[...] = alpha * acc_ref[...] + jnp.dot(p.astype(BF16), vc, preferred_element_type=F32)
        m_ref[...] = m_new

    def body(j, carry):
        chunk(j, False)
        return carry

    lax.fori_loop(0, qi, body, 0)
    chunk(qi, True)

    lp = lamp_ref[...]
    lam = (jnp.exp(jnp.sum(lp[0:1] * lp[1:2], axis=-1, keepdims=True))
           - jnp.exp(jnp.sum(lp[2:3] * lp[3:4], axis=-1, keepdims=True)) + lam_init)
    o_all = acc_ref[...] / l_ref[...]
    o = o_all[0:tq] - lam * o_all[tq:2 * tq]
    ms = jnp.mean(o * o, axis=-1, keepdims=True)
    o_ref[...] = (o * lax.rsqrt(ms + EPS) * sg_ref[...] * (1.0 - lam_init)).astype(BF16)


def _attention(q, k, v, lamp, sg, *, batch, lp_len, tq, lam_init):
    n, qw = q.shape
    nq = lp_len // tq
    heads = qw // LANES
    kern = functools.partial(_attn_kernel, lam_init=lam_init)
    return pl.pallas_call(
        kern,
        grid=(batch, heads, nq),
        in_specs=[
            pl.BlockSpec((tq, LANES), lambda b, h, i: (b * nq + i, h)),
            pl.BlockSpec((lp_len, LANES), lambda b, h, i: (b, h)),
            pl.BlockSpec((lp_len, LANES), lambda b, h, i: (b, h)),
            pl.BlockSpec(lamp.shape, lambda b, h, i: (0, 0)),
            pl.BlockSpec(sg.shape, lambda b, h, i: (0, 0)),
        ],
        out_specs=pl.BlockSpec((tq, LANES), lambda b, h, i: (b * nq + i, h)),
        out_shape=jax.ShapeDtypeStruct((n, v.shape[1]), BF16),
        scratch_shapes=[
            pltpu.VMEM((2 * tq, LANES), BF16),
            pltpu.VMEM((2 * tq, LANES), F32),
            pltpu.VMEM((2 * tq, LANES), F32),
            pltpu.VMEM((2 * tq, LANES), F32),
        ],
        compiler_params=_cparams(3),
        name="diffattn",
    )(q, k, v, lamp, sg)


def _outproj_kernel(hp_ref, cy_ref, o_ref, wout_ref, g2_ref, wrh_ref, wrl_ref, br_ref, tri_ref,
                    h1_ref, xn_ref, route_ref, cnt_ref, run_ref, *, tiles_per_seq, seq_len):
    i = pl.program_id(0)
    tm = hp_ref.shape[0]
    cw = cy_ref.shape[1]
    h1 = (hp_ref[...]
          + jnp.dot(cy_ref[...], wout_ref[0:cw, :], preferred_element_type=F32)
          + jnp.dot(o_ref[...], wout_ref[cw:, :], preferred_element_type=F32))
    h1_ref[...] = h1
    ms = jnp.mean(h1 * h1, axis=-1, keepdims=True)
    xn = h1 * lax.rsqrt(ms + EPS) * g2_ref[...]
    xn_ref[...] = xn

    x_hi = xn.astype(BF16)
    x_lo = (xn - x_hi.astype(F32)).astype(BF16)
    logits = (jnp.dot(x_hi, wrh_ref[...], preferred_element_type=F32)
              + jnp.dot(x_hi, wrl_ref[...], preferred_element_type=F32)
              + jnp.dot(x_lo, wrh_ref[...], preferred_element_type=F32)
              + br_ref[...])
    lane = lax.broadcasted_iota(jnp.int32, logits.shape, 1)
    big = jnp.int32(4 * LANES)

    def first_argmax(vals, vmax):
        return jnp.min(jnp.where(vals == vmax, lane, big), axis=-1, keepdims=True)

    gl = jnp.where(lane < N_GROUPS, logits, NEG_BIG)
    gmax = jnp.max(gl, axis=-1, keepdims=True)
    g_val = 1.0 / jnp.sum(jnp.exp(gl - gmax), axis=-1, keepdims=True)
    g_idx = first_argmax(gl, gmax)
    lo = EXPERT_LANE0 + EXPERTS_PER_GROUP * g_idx
    el = jnp.where((lane >= lo) & (lane < lo + EXPERTS_PER_GROUP), logits, NEG_BIG)
    m1 = jnp.max(el, axis=-1, keepdims=True)
    i1 = first_argmax(el, m1)
    el2 = jnp.where(lane == i1, NEG_BIG, el)
    m2 = jnp.max(el2, axis=-1, keepdims=True)
    i2 = first_argmax(el2, m2)
    r = jnp.exp(m2 - m1)
    gate1 = g_val / (1.0 + r)
    gate2 = g_val * r / (1.0 + r)

    prow = (i % tiles_per_seq) * tm + lax.broadcasted_iota(jnp.int32, logits.shape, 0)
    valid = prow < seq_len
    oh1 = jnp.where(valid & (lane == i1), 1.0, 0.0)
    oh2 = jnp.where(valid & (lane == i2), 1.0, 0.0)

    @pl.when(i == 0)
    def _():
        run_ref[...] = jnp.zeros_like(run_ref)

    run = run_ref[0:1, :]
    tri = tri_ref[...]
    pre1 = jnp.dot(tri, oh1.astype(BF16), preferred_element_type=F32)
    pre2 = jnp.dot(tri, oh2.astype(BF16), preferred_element_type=F32)
    tot1 = jnp.sum(oh1, axis=0, keepdims=True)
    tot2 = jnp.sum(oh2, axis=0, keepdims=True)
    rank1 = jnp.sum(oh1 * (pre1 + run), axis=-1, keepdims=True)
    rank2 = jnp.sum(oh2 * (pre2 + run + tot1), axis=-1, keepdims=True)
    new_run = run + tot1 + tot2
    run_ref[...] = jnp.broadcast_to(new_run, run_ref.shape)
    cnt_ref[...] = jnp.broadcast_to(new_run, cnt_ref.shape)

    e1 = (i1 - EXPERT_LANE0).astype(F32)
    e2 = (i2 - EXPERT_LANE0).astype(F32)
    packed = jnp.where(lane == 0, e1, jnp.where(lane == 1, e2, jnp.where(lane == 2, gate1, jnp.where(
        lane == 3, gate2, jnp.where(lane == 4, rank1, jnp.where(lane == 5, rank2, 0.0))))))
    route_ref[...] = packed[:, 0:ROUTE_COLS]


def _outproj(hp, convy, o, w_out, g2, wr_hi, wr_lo, br, tri, *, tiles_per_seq, seq_len, tm):
    n, d = hp.shape
    const = lambda i: (0, 0)
    tile = lambda i: (i, 0)
    kern = functools.partial(_outproj_kernel, tiles_per_seq=tiles_per_seq, seq_len=seq_len)
    return pl.pallas_call(
        kern,
        grid=(n // tm,),
        in_specs=[
            pl.BlockSpec((tm, d), tile),
            pl.BlockSpec((tm, convy.shape[1]), tile),
            pl.BlockSpec((tm, o.shape[1]), tile),
            pl.BlockSpec(w_out.shape, const),
            pl.BlockSpec((1, d), const),
            pl.BlockSpec(wr_hi.shape, const),
            pl.BlockSpec(wr_lo.shape, const),
            pl.BlockSpec((1, LANES), const),
            pl.BlockSpec(tri.shape, const),
        ],
        out_specs=[
            pl.BlockSpec((tm, d), tile),
            pl.BlockSpec((tm, d), tile),
            pl.BlockSpec((tm, ROUTE_COLS), tile),
            pl.BlockSpec((8, LANES), const),
        ],
        out_shape=[
            jax.ShapeDtypeStruct((n, d), F32),
            jax.ShapeDtypeStruct((n, d), F32),
            jax.ShapeDtypeStruct((n, ROUTE_COLS), F32),
            jax.ShapeDtypeStruct((8, LANES), F32),
        ],
        scratch_shapes=[pltpu.VMEM((8, LANES), F32)],
        compiler_params=_cparams(1),
        name="outproj_router",
    )(hp, convy, o, w_out, g2, wr_hi, wr_lo, br, tri)


def _row_copy(src_hbm, src_row, dst_ref, dst_row, sem):
    return pltpu.make_async_copy(src_hbm.at[pl.ds(src_row, 1), :], dst_ref.at[pl.ds(dst_row, 1), :], sem)


def _dispatch_kernel(dest_ref, xn_hbm, xs_hbm, sem, *, tiles_per_seq, lp_len, tile):
    t = pl.program_id(0)
    base = (t // tiles_per_seq) * lp_len + (t % tiles_per_seq) * tile

    def issue(r, carry):
        for kk in range(TOP_K):
            _row_copy(xn_hbm, base + r, xs_hbm, dest_ref[0, 0, TOP_K * r + kk], sem).start()
        return carry

    lax.fori_loop(0, tile, issue, 0)

    def drain(r, carry):
        for kk in range(TOP_K):
            _row_copy(xn_hbm, base + r, xs_hbm, dest_ref[0, 0, TOP_K * r + kk], sem).wait()
        return carry

    lax.fori_loop(0, tile, drain, 0)


def _dispatch(dest_tiles, xn, *, p_rows, tiles_per_seq, lp_len, tile):
    n_tiles = dest_tiles.shape[0]
    kern = functools.partial(_dispatch_kernel, tiles_per_seq=tiles_per_seq, lp_len=lp_len, tile=tile)
    return pl.pallas_call(
        kern,
        grid=(n_tiles,),
        in_specs=[
            pl.BlockSpec((1, 1, TOP_K * tile), lambda t: (t, 0, 0), memory_space=pltpu.SMEM),
            pl.BlockSpec(memory_space=pl.ANY),
        ],
        out_specs=pl.BlockSpec(memory_space=pl.ANY),
        out_shape=jax.ShapeDtypeStruct((p_rows, xn.shape[1]), xn.dtype),
        scratch_shapes=[pltpu.SemaphoreType.DMA(())],
        compiler_params=_cparams(1),
        name="moe_dispatch",
    )(dest_tiles, xn)


def _experts_kernel(be_ref, nv_ref, xs_ref, wg_ref, wu_ref, wd_ref, y_ref, wgb_ref, wub_ref, wdb_ref):
    i = pl.program_id(0)
    e = be_ref[i]
    prev = be_ref[jnp.maximum(i - 1, 0)]

    @pl.when((i == 0) | (e != prev))
    def _():
        wgb_ref[...] = wg_ref[0].astype(BF16)
        wub_ref[...] = wu_ref[0].astype(BF16)
        wdb_ref[...] = wd_ref[0].astype(BF16)

    nv = nv_ref[i]

    @pl.when(nv > 0)
    def _():
        xs = xs_ref[...]
        row = lax.broadcasted_iota(jnp.int32, xs.shape, 0)
        x = jnp.where(row < nv, xs, 0.0).astype(BF16)
        hg = jnp.dot(x, wgb_ref[...], preferred_element_type=F32)
        hu = jnp.dot(x, wub_ref[...], preferred_element_type=F32)
        hid = (hg / (1.0 + jnp.exp(-hg)) * hu).astype(BF16)
        y_ref[...] = jnp.dot(hid, wdb_ref[...], preferred_element_type=F32)

    @pl.when(nv == 0)
    def _():
        y_ref[...] = jnp.zeros_like(y_ref)


def _experts(block_e, nvalid, xs, w_gate, w_up, w_down):
    p_rows, d = xs.shape
    ff = w_gate.shape[2]
    n_blocks = p_rows // MOE_BLOCK
    grid_spec = pltpu.PrefetchScalarGridSpec(
        num_scalar_prefetch=2,
        grid=(n_blocks,),
        in_specs=[
            pl.BlockSpec((MOE_BLOCK, d), lambda i, be, nv: (i, 0)),
            pl.BlockSpec((1, d, ff), lambda i, be, nv: (be[i], 0, 0)),
            pl.BlockSpec((1, d, ff), lambda i, be, nv: (be[i], 0, 0)),
            pl.BlockSpec((1, ff, d), lambda i, be, nv: (be[i], 0, 0)),
        ],
        out_specs=pl.BlockSpec((MOE_BLOCK, d), lambda i, be, nv: (i, 0)),
        scratch_shapes=[
            pltpu.VMEM((d, ff), BF16),
            pltpu.VMEM((d, ff), BF16),
            pltpu.VMEM((ff, d), BF16),
        ],
    )
    return pl.pallas_call(
        _experts_kernel,
        grid_spec=grid_spec,
        out_shape=jax.ShapeDtypeStruct((p_rows, d), F32),
        compiler_params=_cparams(1),
        name="moe_experts",
    )(block_e, nvalid, xs, w_gate, w_up, w_down)


def _combine_kernel(dest_ref, gates_ref, h1_hbm, y_hbm, out_ref, hbuf, ya, yb, sem_h, sem_a, sem_b,
                    *, lp_len, tile):
    b = pl.program_id(0)
    i = pl.program_id(1)
    start = b * lp_len + N_META + i * tile
    h_copy = pltpu.make_async_copy(h1_hbm.at[pl.ds(start, tile), :], hbuf, sem_h)
    h_copy.start()

    def issue(r, carry):
        _row_copy(y_hbm, dest_ref[0, 0, TOP_K * r], ya, r, sem_a).start()
        _row_copy(y_hbm, dest_ref[0, 0, TOP_K * r + 1], yb, r, sem_b).start()
        return carry

    lax.fori_loop(0, tile, issue, 0)

    def drain(r, carry):
        _row_copy(y_hbm, dest_ref[0, 0, TOP_K * r], ya, r, sem_a).wait()
        _row_copy(y_hbm, dest_ref[0, 0, TOP_K * r + 1], yb, r, sem_b).wait()
        return carry

    lax.fori_loop(0, tile, drain, 0)
    h_copy.wait()
    g = gates_ref[0]
    out_ref[0] = hbuf[...] + g[:, 0:1] * ya[...] + g[:, 1:2] * yb[...]


def _combine(dest_tiles, gates, h1, y, *, batch, seq, lp_len, tile):
    d = h1.shape[1]
    nt = seq // tile
    kern = functools.partial(_combine_kernel, lp_len=lp_len, tile=tile)
    return pl.pallas_call(
        kern,
        grid=(batch, nt),
        in_specs=[
            pl.BlockSpec((1, 1, TOP_K * tile), lambda b, i: (b * nt + i, 0, 0), memory_space=pltpu.SMEM),
            pl.BlockSpec((1, tile, TOP_K), lambda b, i: (b, i, 0)),
            pl.BlockSpec(memory_space=pl.ANY),
            pl.BlockSpec(memory_space=pl.ANY),
        ],
        out_specs=pl.BlockSpec((1, tile, d), lambda b, i: (b, i, 0)),
        out_shape=jax.ShapeDtypeStruct((batch, seq, d), F32),
        scratch_shapes=[
            pltpu.VMEM((tile, d), F32),
            pltpu.VMEM((tile, d), F32),
            pltpu.VMEM((tile, d), F32),
            pltpu.SemaphoreType.DMA(()),
            pltpu.SemaphoreType.DMA(()),
            pltpu.SemaphoreType.DMA(()),
        ],
        compiler_params=_cparams(2),
        name="moe_combine",
    )(dest_tiles, gates, h1, y)


def _rope_tables(length, lp_len):
    half = ROPE_DIM // 2
    pos = jnp.arange(length, dtype=F32)
    inv_freq = ROPE_THETA ** (-jnp.arange(0, ROPE_DIM, 2, dtype=F32) / ROPE_DIM)
    ang = pos[:, None] * inv_freq[None, :]
    cos = jnp.cos(ang)
    sin = jnp.sin(ang)
    ones = jnp.ones((length, HEAD_DIM - ROPE_DIM), F32)
    zeros_h = jnp.zeros((length, half), F32)
    zeros_r = jnp.zeros((length, HEAD_DIM - ROPE_DIM), F32)
    c = jnp.concatenate([cos, cos, ones], axis=1)
    s1 = jnp.concatenate([zeros_h, sin, zeros_r], axis=1)
    s2 = jnp.concatenate([-sin, zeros_h, zeros_r], axis=1)
    pad = ((0, lp_len - length), (0, 0))
    rep = LANES // HEAD_DIM
    return tuple(jnp.pad(jnp.tile(t, (1, rep)), pad) for t in (c, s1, s2))


def _layer(hp, l, batch, length, lp_len, tm, norm1_g, w_in, conv_w, q_norm_g, k_norm_g, lambda_q1, lambda_k1,
           lambda_q2, lambda_k2, subln_g, w_out, norm2_g, w_router_group, b_router_group, w_router_expert,
           b_router_expert, w_gate, w_up, w_down, rope, last):
    n, d = hp.shape
    tiles_per_seq = lp_len // tm
    cw = conv_w.shape[2]
    qw = N_HEADS * 2 * HEAD_DIM
    lam_init = 0.8 - 0.6 * math.exp(-0.3 * l)

    reps = qw // HEAD_DIM
    gq = jnp.tile(q_norm_g[l] * (HEAD_DIM ** -0.5 * LOG2E), reps)[None, :]
    gk = jnp.tile(k_norm_g[l], reps)[None, :]
    seg = jnp.arange(qw) // HEAD_DIM
    bd = (seg[:, None] == seg[None, :]).astype(BF16)
    convy, q, k, v = _inproj(hp, norm1_g[l][None, :], w_in[l].astype(BF16), conv_w[l], gq, gk, bd, *rope,
                             tiles_per_seq=tiles_per_seq, tm=tm)

    lamp = jnp.stack([lambda_q1[l], lambda_k1[l], lambda_q2[l], lambda_k2[l]]).astype(F32)
    o = _attention(q, k, v, lamp, subln_g[l][None, :], batch=batch, lp_len=lp_len, tq=tm, lam_init=lam_init)

    wr = jnp.zeros((d, LANES), F32)
    wr = wr.at[:, 0:N_GROUPS].set(w_router_group[l]).at[:, EXPERT_LANE0:EXPERT_LANE0 + N_EXPERTS].set(
        w_router_expert[l])
    wr_hi = wr.astype(BF16)
    wr_lo = (wr - wr_hi.astype(F32)).astype(BF16)
    br = jnp.zeros((1, LANES), F32)
    br = br.at[0, 0:N_GROUPS].set(b_router_group[l]).at[0, EXPERT_LANE0:EXPERT_LANE0 + N_EXPERTS].set(
        b_router_expert[l])
    ridx = jnp.arange(tm)
    tri = (ridx[None, :] < ridx[:, None]).astype(BF16)
    h1, xn2, route, cnt = _outproj(hp, convy, o, w_out[l].astype(BF16), norm2_g[l][None, :], wr_hi, wr_lo, br,
                                   tri, tiles_per_seq=tiles_per_seq, seq_len=length, tm=tm)

    route = route.reshape(batch, lp_len, ROUTE_COLS)[:, :length]
    eid = route[..., 0:TOP_K].astype(jnp.int32)
    gates = route[..., TOP_K:2 * TOP_K]
    rank = route[..., 2 * TOP_K:3 * TOP_K].astype(jnp.int32)
    counts = cnt[0, EXPERT_LANE0:EXPERT_LANE0 + N_EXPERTS].astype(jnp.int32)
    a = batch * length * TOP_K
    n_blocks = -(-a // MOE_BLOCK) + N_EXPERTS
    padded = (counts + MOE_BLOCK - 1) // MOE_BLOCK * MOE_BLOCK
    pends = jnp.cumsum(padded)
    pstarts = pends - padded
    dest = pstarts[eid] + rank
    blk0 = jnp.arange(n_blocks, dtype=jnp.int32) * MOE_BLOCK
    block_e = jnp.minimum(jnp.searchsorted(pends, blk0, side="right"), N_EXPERTS - 1).astype(jnp.int32)
    nvalid = jnp.clip(counts[block_e] - (blk0 - pstarts[block_e]), 0, MOE_BLOCK).astype(jnp.int32)

    dtile = _largest_tile(length, 512, 8)
    xs = _dispatch(dest.reshape(batch * (length // dtile), 1, TOP_K * dtile), xn2,
                   p_rows=n_blocks * MOE_BLOCK, tiles_per_seq=length // dtile, lp_len=lp_len, tile=dtile)
    y = _experts(block_e, nvalid, xs, w_gate[l], w_up[l], w_down[l])

    if not last:
        raise NotImplementedError("only the final layer's combine (which drops the meta tokens) is implemented")
    seq = length - N_META
    ctile = _largest_tile(seq, 512, 8)
    dest_x = dest[:, N_META:].reshape(batch * (seq // ctile), 1, TOP_K * ctile)
    return _combine(dest_x, gates[:, N_META:], h1, y, batch=batch, seq=seq, lp_len=lp_len, tile=ctile)


def kernel(x, meta_tokens, norm1_g, w_in, conv_w, q_norm_g, k_norm_g, lambda_q1, lambda_k1, lambda_q2, lambda_k2,
           subln_g, w_out, norm2_g, w_router_group, b_router_group, w_router_expert, b_router_expert, w_gate,
           w_up, w_down):
    b, s, d = x.shape
    depth = w_in.shape[0]
    assert depth == 1, "a single layer is supported"
    length = s + N_META
    tm = TOKEN_TILE if length >= TOKEN_TILE else LANES
    lp_len = -(-length // tm) * tm
    hp = jnp.concatenate([jnp.broadcast_to(meta_tokens[None].astype(x.dtype), (b, N_META, d)), x,
                          jnp.zeros((b, lp_len - length, d), x.dtype)], axis=1).reshape(b * lp_len, d)
    rope = _rope_tables(length, lp_len)
    return _layer(hp, 0, b, length, lp_len, tm, norm1_g, w_in, conv_w, q_norm_g, k_norm_g, lambda_q1, lambda_k1,
                  lambda_q2, lambda_k2, subln_g, w_out, norm2_g, w_router_group, b_router_group,
                  w_router_expert, b_router_expert, w_gate, w_up, w_down, rope, last=True)
```

```python
import functools
import math

import jax
import jax.numpy as jnp
from jax import lax
from jax.experimental import pallas as pl
from jax.experimental.pallas import tpu as pltpu

F32 = jnp.float32
BF16 = jnp.bfloat16

N_META = 16
CONV_K = 3
N_HEADS = 4
HEAD_DIM = 64
V_DIM = 2 * HEAD_DIM
ROPE_DIM = HEAD_DIM // 4
ROPE_THETA = 500000.0
N_GROUPS = 4
EXPERTS_PER_GROUP = 8
N_EXPERTS = N_GROUPS * EXPERTS_PER_GROUP
TOP_K = 2
EPS = 1e-6
LOG2E = 1.4426950408889634

LANES = 128
TOKEN_TILE = 640
ATTN_ROW_BLOCK = 640
MOE_BLOCK = 256
ROUTE_COLS = 8
EXPERT_LANE0 = N_GROUPS
NEG_BIG = -1e30
VMEM_LIMIT = 56 * 1024 * 1024


def _largest_tile(n, cap, mult):
    for t in range(min(cap, n), 0, -1):
        if n % t == 0 and t % mult == 0:
            return t
    raise ValueError(f"no tile for {n}")


def _cparams(n_axes):
    return pltpu.CompilerParams(dimension_semantics=("arbitrary",) * n_axes,
                                vmem_limit_bytes=VMEM_LIMIT)


def _inproj_kernel(x_ref, g1_ref, win_ref, convw_ref, gq_ref, gk_ref, bd_ref, rc_ref, rs1_ref, rs2_ref,
                   convy_ref, q_ref, k_ref, v_ref, carry_ref, *, tiles_per_seq, cw, qw):
    i = pl.program_id(0)
    tm = x_ref.shape[0]
    x = x_ref[...]
    ms = jnp.mean(x * x, axis=-1, keepdims=True)
    xn = (x * lax.rsqrt(ms + EPS) * g1_ref[...]).astype(BF16)
    u = jnp.dot(xn, win_ref[...], preferred_element_type=F32)

    z = u[:, cw:2 * cw] * u[:, 2 * cw:3 * cw]

    @pl.when(i % tiles_per_seq == 0)
    def _():
        carry_ref[...] = jnp.zeros_like(carry_ref)

    prev = carry_ref[...]
    p1 = prev[7:8]
    p2 = prev[6:7]
    row = lax.broadcasted_iota(jnp.int32, z.shape, 0)
    z1 = jnp.where(row == 0, p1, pltpu.roll(z, 1, axis=0))
    z2 = jnp.where(row == 0, p2, jnp.where(row == 1, p1, pltpu.roll(z, 2, axis=0)))
    carry_ref[...] = z[tm - 8:tm]
    w = convw_ref[...]
    conv = w[0:1] * z2 + w[1:2] * z1 + w[2:3] * z
    convy_ref[...] = (u[:, 0:cw] * conv).astype(BF16)

    rc = rc_ref[...]
    rs1 = rs1_ref[...]
    rs2 = rs2_ref[...]

    def norm_rope(t, g_ref):
        ss = jnp.dot((t * t).astype(BF16), bd_ref[...], preferred_element_type=F32)
        tn = t * lax.rsqrt(ss * (1.0 / HEAD_DIM) + EPS) * g_ref[...]
        outs = []
        for c in range(qw // LANES):
            ch = tn[:, c * LANES:(c + 1) * LANES]
            outs.append(ch * rc + pltpu.roll(ch, ROPE_DIM // 2, axis=1) * rs1
                        + pltpu.roll(ch, LANES - ROPE_DIM // 2, axis=1) * rs2)
        return jnp.concatenate(outs, axis=1).astype(BF16)

    q0 = 3 * cw
    q_ref[...] = norm_rope(u[:, q0:q0 + qw], gq_ref)
    k_ref[...] = norm_rope(u[:, q0 + qw:q0 + 2 * qw], gk_ref)
    v_ref[...] = u[:, q0 + 2 * qw:].astype(BF16)


def _inproj(hp, g1, w_in, conv_w, gq, gk, bd, rc, rs1, rs2, *, tiles_per_seq, tm):
    n, d = hp.shape
    cw = conv_w.shape[1]
    qw = gq.shape[1]
    aw = w_in.shape[1] - 3 * cw - 2 * qw
    const = lambda i: (0, 0)
    tile = lambda i: (i, 0)
    pos = lambda i: (i % tiles_per_seq, 0)
    kern = functools.partial(_inproj_kernel, tiles_per_seq=tiles_per_seq, cw=cw, qw=qw)
    return pl.pallas_call(
        kern,
        grid=(n // tm,),
        in_specs=[
            pl.BlockSpec((tm, d), tile),
            pl.BlockSpec((1, d), const),
            pl.BlockSpec(w_in.shape, const),
            pl.BlockSpec(conv_w.shape, const),
            pl.BlockSpec((1, qw), const),
            pl.BlockSpec((1, qw), const),
            pl.BlockSpec(bd.shape, const),
            pl.BlockSpec((tm, LANES), pos),
            pl.BlockSpec((tm, LANES), pos),
            pl.BlockSpec((tm, LANES), pos),
        ],
        out_specs=[
            pl.BlockSpec((tm, cw), tile),
            pl.BlockSpec((tm, qw), tile),
            pl.BlockSpec((tm, qw), tile),
            pl.BlockSpec((tm, aw), tile),
        ],
        out_shape=[
            jax.ShapeDtypeStruct((n, cw), BF16),
            jax.ShapeDtypeStruct((n, qw), BF16),
            jax.ShapeDtypeStruct((n, qw), BF16),
            jax.ShapeDtypeStruct((n, aw), BF16),
        ],
        scratch_shapes=[pltpu.VMEM((8, cw), F32)],
        compiler_params=_cparams(1),
        name="inproj",
    )(hp, g1, w_in, conv_w, gq, gk, bd, rc, rs1, rs2)


def _attn_kernel(q_ref, k_ref, v_ref, lamp_ref, sg_ref, o_ref, qs_ref, m_ref, l_ref, acc_ref, *, lam_init):
    qi = pl.program_id(2)
    tq = q_ref.shape[0]
    row_block = ATTN_ROW_BLOCK if tq % ATTN_ROW_BLOCK == 0 else tq
    q = q_ref[...]
    lane = lax.broadcasted_iota(jnp.int32, q.shape, 1)
    zero = jnp.zeros_like(q)
    qs_ref[0:tq, :] = jnp.where(lane < HEAD_DIM, q, zero)
    qs_ref[tq:2 * tq, :] = jnp.where(lane >= HEAD_DIM, q, zero)
    m_ref[...] = jnp.full_like(m_ref, NEG_BIG)
    l_ref[...] = jnp.zeros_like(l_ref)
    acc_ref[...] = jnp.zeros_like(acc_ref)

    def chunk(j, masked):
        off = pl.multiple_of(j * tq, tq)
        kc = k_ref[pl.ds(off, tq), :]
        vc = jnp.concatenate([v_ref[pl.ds(off, tq), :], jnp.ones((tq, LANES), BF16)], axis=1)
        n_rb = 2 * tq // row_block

        def scores(rb):
            return lax.dot_general(qs_ref[pl.ds(rb * row_block, row_block), :], kc, (((1,), (1,)), ((), ())),
                                   preferred_element_type=F32)

        s_next = scores(0)
        for rb in range(n_rb):
            rows = pl.ds(rb * row_block, row_block)
            s = s_next
            if rb + 1 < n_rb:
                s_next = scores(rb + 1)
            if masked:
                r = (rb * row_block) % tq + lax.broadcasted_iota(jnp.int32, s.shape, 0)
                c = lax.broadcasted_iota(jnp.int32, s.shape, 1)
                s = jnp.where(c <= r, s, NEG_BIG)
            m_prev = m_ref[rows, :]
            m_new = jnp.maximum(m_prev, jnp.max(s, axis=-1, keepdims=True))
            alpha = jnp.exp2(m_prev - m_new)
            p = jnp.exp2((s - jnp.tile(m_new, (1, tq // LANES))).astype(BF16))
            pv = jnp.dot(p, vc, preferred_element_type=F32)
            l_ref[rows, :] = alpha * l_ref[rows, :] + pv[:, LANES:]
            acc_ref[rows, :] = alpha * acc_ref[rows, :] + pv[:, :LANES]
            m_ref[rows, :] = m_new

    def body(j, carry):
        chunk(j, False)
        return carry

    lax.fori_loop(0, qi, body, 0)
    chunk(qi, True)

    lp = lamp_ref[...]
    lam = (jnp.exp(jnp.sum(lp[0:1] * lp[1:2], axis=-1, keepdims=True))
           - jnp.exp(jnp.sum(lp[2:3] * lp[3:4], axis=-1, keepdims=True)) + lam_init)
    o_all = acc_ref[...] / l_ref[...]
    o = o_all[0:tq] - lam * o_all[tq:2 * tq]
    ms = jnp.mean(o * o, axis=-1, keepdims=True)
    o_ref[...] = (o * lax.rsqrt(ms + EPS) * sg_ref[...] * (1.0 - lam_init)).astype(BF16)


def _attention(q, k, v, lamp, sg, *, batch, lp_len, tq, lam_init):
    n, qw = q.shape
    nq = lp_len // tq
    heads = qw // LANES
    kern = functools.partial(_attn_kernel, lam_init=lam_init)
    return pl.pallas_call(
        kern,
        grid=(batch, heads, nq),
        in_specs=[
            pl.BlockSpec((tq, LANES), lambda b, h, i: (b * nq + i, h)),
            pl.BlockSpec((lp_len, LANES), lambda b, h, i: (b, h)),
            pl.BlockSpec((lp_len, LANES), lambda b, h, i: (b, h)),
            pl.BlockSpec(lamp.shape, lambda b, h, i: (0, 0)),
            pl.BlockSpec(sg.shape, lambda b, h, i: (0, 0)),
        ],
        out_specs=pl.BlockSpec((tq, LANES), lambda b, h, i: (b * nq + i, h)),
        out_shape=jax.ShapeDtypeStruct((n, v.shape[1]), BF16),
        scratch_shapes=[
            pltpu.VMEM((2 * tq, LANES), BF16),
            pltpu.VMEM((2 * tq, LANES), F32),
            pltpu.VMEM((2 * tq, LANES), F32),
            pltpu.VMEM((2 * tq, LANES), F32),
        ],
        compiler_params=_cparams(3),
        name="diffattn",
    )(q, k, v, lamp, sg)


def _outproj_kernel(hp_ref, cy_ref, o_ref, wout_ref, g2_ref, wrh_ref, wrl_ref, br_ref, tri_ref,
                    h1_ref, xn_ref, route_ref, cnt_ref, run_ref, *, tiles_per_seq, seq_len):
    i = pl.program_id(0)
    tm = hp_ref.shape[0]
    cw = cy_ref.shape[1]
    h1 = (hp_ref[...]
          + jnp.dot(cy_ref[...], wout_ref[0:cw, :], preferred_element_type=F32)
          + jnp.dot(o_ref[...], wout_ref[cw:, :], preferred_element_type=F32))
    h1_ref[...] = h1
    ms = jnp.mean(h1 * h1, axis=-1, keepdims=True)
    xn = h1 * lax.rsqrt(ms + EPS) * g2_ref[...]
    xn_ref[...] = xn

    x_hi = xn.astype(BF16)
    x_lo = (xn - x_hi.astype(F32)).astype(BF16)
    logits = (jnp.dot(x_hi, wrh_ref[...], preferred_element_type=F32)
              + jnp.dot(x_hi, wrl_ref[...], preferred_element_type=F32)
              + jnp.dot(x_lo, wrh_ref[...], preferred_element_type=F32)
              + br_ref[...])
    lane = lax.broadcasted_iota(jnp.int32, logits.shape, 1)
    big = jnp.int32(4 * LANES)

    def first_argmax(vals, vmax):
        return jnp.min(jnp.where(vals == vmax, lane, big), axis=-1, keepdims=True)

    gl = jnp.where(lane < N_GROUPS, logits, NEG_BIG)
    gmax = jnp.max(gl, axis=-1, keepdims=True)
    g_val = 1.0 / jnp.sum(jnp.exp(gl - gmax), axis=-1, keepdims=True)
    g_idx = first_argmax(gl, gmax)
    lo = EXPERT_LANE0 + EXPERTS_PER_GROUP * g_idx
    el = jnp.where((lane >= lo) & (lane < lo + EXPERTS_PER_GROUP), logits, NEG_BIG)
    m1 = jnp.max(el, axis=-1, keepdims=True)
    i1 = first_argmax(el, m1)
    el2 = jnp.where(lane == i1, NEG_BIG, el)
    m2 = jnp.max(el2, axis=-1, keepdims=True)
    i2 = first_argmax(el2, m2)
    r = jnp.exp(m2 - m1)
    gate1 = g_val / (1.0 + r)
    gate2 = g_val * r / (1.0 + r)

    prow = (i % tiles_per_seq) * tm + lax.broadcasted_iota(jnp.int32, logits.shape, 0)
    valid = prow < seq_len
    oh1 = jnp.where(valid & (lane == i1), 1.0, 0.0)
    oh2 = jnp.where(valid & (lane == i2), 1.0, 0.0)

    @pl.when(i == 0)
    def _():
        run_ref[...] = jnp.zeros_like(run_ref)

    run = run_ref[0:1, :]
    tri = tri_ref[...]
    pre1 = jnp.dot(tri, oh1.astype(BF16), preferred_element_type=F32)
    pre2 = jnp.dot(tri, oh2.astype(BF16), preferred_element_type=F32)
    tot1 = jnp.sum(oh1, axis=0, keepdims=True)
    tot2 = jnp.sum(oh2, axis=0, keepdims=True)
    rank1 = jnp.sum(oh1 * (pre1 + run), axis=-1, keepdims=True)
    rank2 = jnp.sum(oh2 * (pre2 + run + tot1), axis=-1, keepdims=True)
    new_run = run + tot1 + tot2
    run_ref[...] = jnp.broadcast_to(new_run, run_ref.shape)
    cnt_ref[...] = jnp.broadcast_to(new_run, cnt_ref.shape)

    e1 = (i1 - EXPERT_LANE0).astype(F32)
    e2 = (i2 - EXPERT_LANE0).astype(F32)
    packed = jnp.where(lane == 0, e1, jnp.where(lane == 1, e2, jnp.where(lane == 2, gate1, jnp.where(
        lane == 3, gate2, jnp.where(lane == 4, rank1, jnp.where(lane == 5, rank2, 0.0))))))
    route_ref[...] = packed[:, 0:ROUTE_COLS]


def _outproj(hp, convy, o, w_out, g2, wr_hi, wr_lo, br, tri, *, tiles_per_seq, seq_len, tm):
    n, d = hp.shape
    const = lambda i: (0, 0)
    tile = lambda i: (i, 0)
    kern = functools.partial(_outproj_kernel, tiles_per_seq=tiles_per_seq, seq_len=seq_len)
    return pl.pallas_call(
        kern,
        grid=(n // tm,),
        in_specs=[
            pl.BlockSpec((tm, d), tile),
            pl.BlockSpec((tm, convy.shape[1]), tile),
            pl.BlockSpec((tm, o.shape[1]), tile),
            pl.BlockSpec(w_out.shape, const),
            pl.BlockSpec((1, d), const),
            pl.BlockSpec(wr_hi.shape, const),
            pl.BlockSpec(wr_lo.shape, const),
            pl.BlockSpec((1, LANES), const),
            pl.BlockSpec(tri.shape, const),
        ],
        out_specs=[
            pl.BlockSpec((tm, d), tile),
            pl.BlockSpec((tm, d), tile),
            pl.BlockSpec((tm, ROUTE_COLS), tile),
            pl.BlockSpec((8, LANES), const),
        ],
        out_shape=[
            jax.ShapeDtypeStruct((n, d), F32),
            jax.ShapeDtypeStruct((n, d), F32),
            jax.ShapeDtypeStruct((n, ROUTE_COLS), F32),
            jax.ShapeDtypeStruct((8, LANES), F32),
        ],
        scratch_shapes=[pltpu.VMEM((8, LANES), F32)],
        compiler_params=_cparams(1),
        name="outproj_router",
    )(hp, convy, o, w_out, g2, wr_hi, wr_lo, br, tri)


def _row_copy(src_hbm, src_row, dst_ref, dst_row, sem):
    return pltpu.make_async_copy(src_hbm.at[pl.ds(src_row, 1), :], dst_ref.at[pl.ds(dst_row, 1), :], sem)


def _dispatch_kernel(dest_ref, xn_hbm, xs_hbm, xbuf, sem_in, sem, *, tiles_per_seq, lp_len, tile):
    t = pl.program_id(0)
    base = (t // tiles_per_seq) * lp_len + (t % tiles_per_seq) * tile
    tile_copy = pltpu.make_async_copy(xn_hbm.at[pl.ds(base, tile), :], xbuf, sem_in)
    tile_copy.start()
    tile_copy.wait()

    def issue(r, carry):
        for kk in range(TOP_K):
            _row_copy(xbuf, r, xs_hbm, dest_ref[0, 0, TOP_K * r + kk], sem).start()
        return carry

    lax.fori_loop(0, tile, issue, 0)

    def drain(r, carry):
        for kk in range(TOP_K):
            _row_copy(xbuf, r, xs_hbm, dest_ref[0, 0, TOP_K * r + kk], sem).wait()
        return carry

    lax.fori_loop(0, tile, drain, 0)


def _dispatch(dest_tiles, xn, *, p_rows, tiles_per_seq, lp_len, tile):
    n_tiles = dest_tiles.shape[0]
    kern = functools.partial(_dispatch_kernel, tiles_per_seq=tiles_per_seq, lp_len=lp_len, tile=tile)
    return pl.pallas_call(
        kern,
        grid=(n_tiles,),
        in_specs=[
            pl.BlockSpec((1, 1, TOP_K * tile), lambda t: (t, 0, 0), memory_space=pltpu.SMEM),
            pl.BlockSpec(memory_space=pl.ANY),
        ],
        out_specs=pl.BlockSpec(memory_space=pl.ANY),
        out_shape=jax.ShapeDtypeStruct((p_rows, xn.shape[1]), xn.dtype),
        scratch_shapes=[pltpu.VMEM((tile, xn.shape[1]), xn.dtype), pltpu.SemaphoreType.DMA(()),
                        pltpu.SemaphoreType.DMA(())],
        compiler_params=_cparams(1),
        name="moe_dispatch",
    )(dest_tiles, xn)


def _experts_kernel(be_ref, nv_ref, xs_ref, wg_ref, wu_ref, wd_ref, y_ref, wgb_ref, wub_ref, wdb_ref):
    i = pl.program_id(0)
    e = be_ref[i]
    prev = be_ref[jnp.maximum(i - 1, 0)]

    @pl.when((i == 0) | (e != prev))
    def _():
        wgb_ref[...] = wg_ref[0].astype(BF16)
        wub_ref[...] = wu_ref[0].astype(BF16)
        wdb_ref[...] = wd_ref[0].astype(BF16)

    nv = nv_ref[i]

    @pl.when(nv > 0)
    def _():
        xs = xs_ref[...]
        row = lax.broadcasted_iota(jnp.int32, xs.shape, 0)
        x = jnp.where(row < nv, xs, 0.0).astype(BF16)
        hg = jnp.dot(x, wgb_ref[...], preferred_element_type=F32)
        hu = jnp.dot(x, wub_ref[...], preferred_element_type=F32)
        hid = (hg / (1.0 + jnp.exp(-hg)) * hu).astype(BF16)
        y_ref[...] = jnp.dot(hid, wdb_ref[...], preferred_element_type=F32)

    @pl.when(nv == 0)
    def _():
        y_ref[...] = jnp.zeros_like(y_ref)


def _experts(block_e, nvalid, xs, w_gate, w_up, w_down):
    p_rows, d = xs.shape
    ff = w_gate.shape[2]
    n_blocks = p_rows // MOE_BLOCK
    grid_spec = pltpu.PrefetchScalarGridSpec(
        num_scalar_prefetch=2,
        grid=(n_blocks,),
        in_specs=[
            pl.BlockSpec((MOE_BLOCK, d), lambda i, be, nv: (i, 0)),
            pl.BlockSpec((1, d, ff), lambda i, be, nv: (be[i], 0, 0)),
            pl.BlockSpec((1, d, ff), lambda i, be, nv: (be[i], 0, 0)),
            pl.BlockSpec((1, ff, d), lambda i, be, nv: (be[i], 0, 0)),
        ],
        out_specs=pl.BlockSpec((MOE_BLOCK, d), lambda i, be, nv: (i, 0)),
        scratch_shapes=[
            pltpu.VMEM((d, ff), BF16),
            pltpu.VMEM((d, ff), BF16),
            pltpu.VMEM((ff, d), BF16),
        ],
    )
    return pl.pallas_call(
        _experts_kernel,
        grid_spec=grid_spec,
        out_shape=jax.ShapeDtypeStruct((p_rows, d), F32),
        compiler_params=_cparams(1),
        name="moe_experts",
    )(block_e, nvalid, xs, w_gate, w_up, w_down)


def _combine_kernel(dest_ref, gates_ref, h1_hbm, y_hbm, out_ref, hbuf, ya, yb, sem_h, sem_a, sem_b,
                    *, lp_len, tile):
    b = pl.program_id(0)
    i = pl.program_id(1)
    start = b * lp_len + N_META + i * tile
    h_copy = pltpu.make_async_copy(h1_hbm.at[pl.ds(start, tile), :], hbuf, sem_h)
    h_copy.start()

    def issue(r, carry):
        _row_copy(y_hbm, dest_ref[0, 0, TOP_K * r], ya, r, sem_a).start()
        _row_copy(y_hbm, dest_ref[0, 0, TOP_K * r + 1], yb, r, sem_b).start()
        return carry

    lax.fori_loop(0, tile, issue, 0)

    def drain(r, carry):
        _row_copy(y_hbm, dest_ref[0, 0, TOP_K * r], ya, r, sem_a).wait()
        _row_copy(y_hbm, dest_ref[0, 0, TOP_K * r + 1], yb, r, sem_b).wait()
        return carry

    lax.fori_loop(0, tile, drain, 0)
    h_copy.wait()
    g = gates_ref[0]
    out_ref[0] = hbuf[...] + g[:, 0:1] * ya[...] + g[:, 1:2] * yb[...]


def _combine(dest_tiles, gates, h1, y, *, batch, seq, lp_len, tile):
    d = h1.shape[1]
    nt = seq // tile
    kern = functools.partial(_combine_kernel, lp_len=lp_len, tile=tile)
    return pl.pallas_call(
        kern,
        grid=(batch, nt),
        in_specs=[
            pl.BlockSpec((1, 1, TOP_K * tile), lambda b, i: (b * nt + i, 0, 0), memory_space=pltpu.SMEM),
            pl.BlockSpec((1, tile, TOP_K), lambda b, i: (b, i, 0)),
            pl.BlockSpec(memory_space=pl.ANY),
            pl.BlockSpec(memory_space=pl.ANY),
        ],
        out_specs=pl.BlockSpec((1, tile, d), lambda b, i: (b, i, 0)),
        out_shape=jax.ShapeDtypeStruct((batch, seq, d), F32),
        scratch_shapes=[
            pltpu.VMEM((tile, d), F32),
            pltpu.VMEM((tile, d), F32),
            pltpu.VMEM((tile, d), F32),
            pltpu.SemaphoreType.DMA(()),
            pltpu.SemaphoreType.DMA(()),
            pltpu.SemaphoreType.DMA(()),
        ],
        compiler_params=_cparams(2),
        name="moe_combine",
    )(dest_tiles, gates, h1, y)


def _rope_tables(length, lp_len):
    half = ROPE_DIM // 2
    pos = jnp.arange(length, dtype=F32)
    inv_freq = ROPE_THETA ** (-jnp.arange(0, ROPE_DIM, 2, dtype=F32) / ROPE_DIM)
    ang = pos[:, None] * inv_freq[None, :]
    cos = jnp.cos(ang)
    sin = jnp.sin(ang)
    ones = jnp.ones((length, HEAD_DIM - ROPE_DIM), F32)
    zeros_h = jnp.zeros((length, half), F32)
    zeros_r = jnp.zeros((length, HEAD_DIM - ROPE_DIM), F32)
    c = jnp.concatenate([cos, cos, ones], axis=1)
    s1 = jnp.concatenate([zeros_h, sin, zeros_r], axis=1)
    s2 = jnp.concatenate([-sin, zeros_h, zeros_r], axis=1)
    pad = ((0, lp_len - length), (0, 0))
    rep = LANES // HEAD_DIM
    return tuple(jnp.pad(jnp.tile(t, (1, rep)), pad) for t in (c, s1, s2))


def _layer(hp, l, batch, length, lp_len, tm, norm1_g, w_in, conv_w, q_norm_g, k_norm_g, lambda_q1, lambda_k1,
           lambda_q2, lambda_k2, subln_g, w_out, norm2_g, w_router_group, b_router_group, w_router_expert,
           b_router_expert, w_gate, w_up, w_down, rope, last):
    n, d = hp.shape
    tiles_per_seq = lp_len // tm
    cw = conv_w.shape[2]
    qw = N_HEADS * 2 * HEAD_DIM
    lam_init = 0.8 - 0.6 * math.exp(-0.3 * l)

    reps = qw // HEAD_DIM
    gq = jnp.tile(q_norm_g[l] * (HEAD_DIM ** -0.5 * LOG2E), reps)[None, :]
    gk = jnp.tile(k_norm_g[l], reps)[None, :]
    seg = jnp.arange(qw) // HEAD_DIM
    bd = (seg[:, None] == seg[None, :]).astype(BF16)
    convy, q, k, v = _inproj(hp, norm1_g[l][None, :], w_in[l].astype(BF16), conv_w[l], gq, gk, bd, *rope,
                             tiles_per_seq=tiles_per_seq, tm=tm)

    lamp = jnp.stack([lambda_q1[l], lambda_k1[l], lambda_q2[l], lambda_k2[l]]).astype(F32)
    o = _attention(q, k, v, lamp, subln_g[l][None, :], batch=batch, lp_len=lp_len, tq=tm, lam_init=lam_init)

    lane_pad = LANES - N_GROUPS - N_EXPERTS
    wr = jnp.pad(jnp.concatenate([w_router_group[l], w_router_expert[l]], axis=1), ((0, 0), (0, lane_pad)))
    wr_hi = wr.astype(BF16)
    wr_lo = (wr - wr_hi.astype(F32)).astype(BF16)
    br = jnp.pad(jnp.concatenate([b_router_group[l], b_router_expert[l]]), (0, lane_pad))[None, :]
    ridx = jnp.arange(tm)
    tri = (ridx[None, :] < ridx[:, None]).astype(BF16)
    h1, xn2, route, cnt = _outproj(hp, convy, o, w_out[l].astype(BF16), norm2_g[l][None, :], wr_hi, wr_lo, br,
                                   tri, tiles_per_seq=tiles_per_seq, seq_len=length, tm=tm)

    route = route.reshape(batch, lp_len, ROUTE_COLS)[:, :length]
    eid = route[..., 0:TOP_K].astype(jnp.int32)
    gates = route[..., TOP_K:2 * TOP_K]
    rank = route[..., 2 * TOP_K:3 * TOP_K].astype(jnp.int32)
    counts = cnt[0, EXPERT_LANE0:EXPERT_LANE0 + N_EXPERTS].astype(jnp.int32)
    a = batch * length * TOP_K
    n_blocks = -(-a // MOE_BLOCK) + N_EXPERTS
    padded = (counts + MOE_BLOCK - 1) // MOE_BLOCK * MOE_BLOCK
    pends = jnp.cumsum(padded)
    pstarts = pends - padded
    experts = jnp.arange(N_EXPERTS, dtype=jnp.int32)

    def lookup(table, idx):
        return jnp.sum(jnp.where(idx[..., None] == experts, table, 0), axis=-1)

    dest = lookup(pstarts, eid) + rank
    blk0 = jnp.arange(n_blocks, dtype=jnp.int32) * MOE_BLOCK
    block_e = jnp.minimum(jnp.sum((pends[None, :] <= blk0[:, None]).astype(jnp.int32), axis=1), N_EXPERTS - 1)
    nvalid = jnp.clip(lookup(counts, block_e) - (blk0 - lookup(pstarts, block_e)), 0, MOE_BLOCK)

    dtile = _largest_tile(length, 512, 8)
    xs = _dispatch(dest.reshape(batch * (length // dtile), 1, TOP_K * dtile), xn2,
                   p_rows=n_blocks * MOE_BLOCK, tiles_per_seq=length // dtile, lp_len=lp_len, tile=dtile)
    y = _experts(block_e, nvalid, xs, w_gate[l], w_up[l], w_down[l])

    if not last:
        raise NotImplementedError("only the final layer's combine (which drops the meta tokens) is implemented")
    seq = length - N_META
    ctile = _largest_tile(seq, 512, 8)
    dest_x = dest[:, N_META:].reshape(batch * (seq // ctile), 1, TOP_K * ctile)
    return _combine(dest_x, gates[:, N_META:], h1, y, batch=batch, seq=seq, lp_len=lp_len, tile=ctile)


def kernel(x, meta_tokens, norm1_g, w_in, conv_w, q_norm_g, k_norm_g, lambda_q1, lambda_k1, lambda_q2, lambda_k2,
           subln_g, w_out, norm2_g, w_router_group, b_router_group, w_router_expert, b_router_expert, w_gate,
           w_up, w_down):
    b, s, d = x.shape
    depth = w_in.shape[0]
    assert depth == 1, "a single layer is supported"
    length = s + N_META
    tm = TOKEN_TILE if length >= TOKEN_TILE else LANES
    lp_len = -(-length // tm) * tm
    hp = jnp.concatenate([jnp.broadcast_to(meta_tokens[None].astype(x.dtype), (b, N_META, d)), x,
                          jnp.zeros((b, lp_len - length, d), x.dtype)], axis=1).reshape(b * lp_len, d)
    rope = _rope_tables(length, lp_len)
    return _layer(hp, 0, b, length, lp_len, tm, norm1_g, w_in, conv_w, q_norm_g, k_norm_g, lambda_q1, lambda_k1,
                  lambda_q2, lambda_k2, subln_g, w_out, norm2_g, w_router_group, b_router_group,
                  w_router_expert, b_router_expert, w_gate, w_up, w_down, rope, last=True)
```

```python
import functools
import math

import jax
import jax.numpy as jnp
from jax import lax
from jax.experimental import pallas as pl
from jax.experimental.pallas import tpu as pltpu
from jax.experimental.pallas import tpu_sc as plsc

F32 = jnp.float32
BF16 = jnp.bfloat16

N_META = 16
CONV_K = 3
N_HEADS = 4
HEAD_DIM = 64
V_DIM = 2 * HEAD_DIM
ROPE_DIM = HEAD_DIM // 4
ROPE_THETA = 500000.0
N_GROUPS = 4
EXPERTS_PER_GROUP = 8
N_EXPERTS = N_GROUPS * EXPERTS_PER_GROUP
TOP_K = 2
EPS = 1e-6
LOG2E = 1.4426950408889634

LANES = 128
TOKEN_TILE = 640
MOE_BLOCK = 256
ROUTE_COLS = 8
ROW_PLANES = 4
SC_WINDOW = 128
EXPERT_LANE0 = N_GROUPS
NEG_BIG = -1e30
VMEM_LIMIT = 56 * 1024 * 1024


def _largest_tile(n, cap, mult):
    for t in range(min(cap, n), 0, -1):
        if n % t == 0 and t % mult == 0:
            return t
    raise ValueError(f"no tile for {n}")


def _cparams(n_axes):
    return pltpu.CompilerParams(dimension_semantics=("arbitrary",) * n_axes,
                                vmem_limit_bytes=VMEM_LIMIT)


def _inproj_kernel(x_ref, g1_ref, win_ref, convw_ref, gq_ref, gk_ref, bd_ref, rc_ref, rs1_ref, rs2_ref,
                   convy_ref, q_ref, k_ref, v_ref, carry_ref, *, tiles_per_seq, cw, qw):
    i = pl.program_id(0)
    tm = x_ref.shape[0]
    x = x_ref[...]
    ms = jnp.mean(x * x, axis=-1, keepdims=True)
    xn = (x * lax.rsqrt(ms + EPS) * g1_ref[...]).astype(BF16)
    u = jnp.dot(xn, win_ref[...], preferred_element_type=F32)

    z = u[:, cw:2 * cw] * u[:, 2 * cw:3 * cw]

    @pl.when(i % tiles_per_seq == 0)
    def _():
        carry_ref[...] = jnp.zeros_like(carry_ref)

    prev = carry_ref[...]
    p1 = prev[7:8]
    p2 = prev[6:7]
    row = lax.broadcasted_iota(jnp.int32, z.shape, 0)
    z1 = jnp.where(row == 0, p1, pltpu.roll(z, 1, axis=0))
    z2 = jnp.where(row == 0, p2, jnp.where(row == 1, p1, pltpu.roll(z, 2, axis=0)))
    carry_ref[...] = z[tm - 8:tm]
    w = convw_ref[...]
    conv = w[0:1] * z2 + w[1:2] * z1 + w[2:3] * z
    convy_ref[...] = (u[:, 0:cw] * conv).astype(BF16)

    rc = rc_ref[...]
    rs1 = rs1_ref[...]
    rs2 = rs2_ref[...]

    def norm_rope(t, g_ref):
        ss = jnp.dot((t * t).astype(BF16), bd_ref[...], preferred_element_type=F32)
        tn = t * lax.rsqrt(ss * (1.0 / HEAD_DIM) + EPS) * g_ref[...]
        outs = []
        for c in range(qw // LANES):
            ch = tn[:, c * LANES:(c + 1) * LANES]
            outs.append(ch * rc + pltpu.roll(ch, ROPE_DIM // 2, axis=1) * rs1
                        + pltpu.roll(ch, LANES - ROPE_DIM // 2, axis=1) * rs2)
        return jnp.concatenate(outs, axis=1).astype(BF16)

    q0 = 3 * cw
    q_ref[...] = norm_rope(u[:, q0:q0 + qw], gq_ref)
    k_ref[...] = norm_rope(u[:, q0 + qw:q0 + 2 * qw], gk_ref)
    v_ref[...] = u[:, q0 + 2 * qw:].astype(BF16)


def _inproj(hp, g1, w_in, conv_w, gq, gk, bd, rc, rs1, rs2, *, tiles_per_seq, tm):
    n, d = hp.shape
    cw = conv_w.shape[1]
    qw = gq.shape[1]
    aw = w_in.shape[1] - 3 * cw - 2 * qw
    const = lambda i: (0, 0)
    tile = lambda i: (i, 0)
    pos = lambda i: (i % tiles_per_seq, 0)
    kern = functools.partial(_inproj_kernel, tiles_per_seq=tiles_per_seq, cw=cw, qw=qw)
    return pl.pallas_call(
        kern,
        grid=(n // tm,),
        in_specs=[
            pl.BlockSpec((tm, d), tile),
            pl.BlockSpec((1, d), const),
            pl.BlockSpec(w_in.shape, const),
            pl.BlockSpec(conv_w.shape, const),
            pl.BlockSpec((1, qw), const),
            pl.BlockSpec((1, qw), const),
            pl.BlockSpec(bd.shape, const),
            pl.BlockSpec((tm, LANES), pos),
            pl.BlockSpec((tm, LANES), pos),
            pl.BlockSpec((tm, LANES), pos),
        ],
        out_specs=[
            pl.BlockSpec((tm, cw), tile),
            pl.BlockSpec((tm, qw), tile),
            pl.BlockSpec((tm, qw), tile),
            pl.BlockSpec((tm, aw), tile),
        ],
        out_shape=[
            jax.ShapeDtypeStruct((n, cw), BF16),
            jax.ShapeDtypeStruct((n, qw), BF16),
            jax.ShapeDtypeStruct((n, qw), BF16),
            jax.ShapeDtypeStruct((n, aw), BF16),
        ],
        scratch_shapes=[pltpu.VMEM((8, cw), F32)],
        compiler_params=_cparams(1),
        name="inproj",
    )(hp, g1, w_in, conv_w, gq, gk, bd, rc, rs1, rs2)


def _attn_kernel(q_ref, k_ref, v_ref, lamp_ref, sg_ref, o_ref, qs_ref, m_ref, l_ref, acc_ref, *, lam_init):
    qi = pl.program_id(2)
    tq = q_ref.shape[0]
    q = q_ref[...]
    lane = lax.broadcasted_iota(jnp.int32, q.shape, 1)
    zero = jnp.zeros_like(q)
    qs_ref[0:tq, :] = jnp.where(lane < HEAD_DIM, q, zero)
    qs_ref[tq:2 * tq, :] = jnp.where(lane >= HEAD_DIM, q, zero)
    m_ref[...] = jnp.full_like(m_ref, NEG_BIG)
    l_ref[...] = jnp.zeros_like(l_ref)
    acc_ref[...] = jnp.zeros_like(acc_ref)

    def scores(j, which):
        kc = k_ref[pl.ds(pl.multiple_of(j * tq, tq), tq), :]
        return lax.dot_general(qs_ref[pl.ds(which * tq, tq), :], kc, (((1,), (1,)), ((), ())),
                               preferred_element_type=F32)

    def update(j, which, s, masked):
        vc = jnp.concatenate([v_ref[pl.ds(pl.multiple_of(j * tq, tq), tq), :], jnp.ones((tq, LANES), BF16)],
                             axis=1)
        rows = pl.ds(which * tq, tq)
        if masked:
            r = lax.broadcasted_iota(jnp.int32, s.shape, 0)
            c = lax.broadcasted_iota(jnp.int32, s.shape, 1)
            s = jnp.where(c <= r, s, NEG_BIG)
        m_prev = m_ref[rows, :]
        m_new = jnp.maximum(m_prev, jnp.max(s, axis=-1, keepdims=True))
        alpha = jnp.exp2(m_prev - m_new)
        p = jnp.exp2((s - jnp.tile(m_new, (1, tq // LANES))).astype(BF16))
        pv = jnp.dot(p, vc, preferred_element_type=F32)
        l_ref[rows, :] = alpha * l_ref[rows, :] + pv[:, LANES:]
        acc_ref[rows, :] = alpha * acc_ref[rows, :] + pv[:, :LANES]
        m_ref[rows, :] = m_new

    def chunk(j, masked):
        s1 = scores(j, 0)
        s2 = scores(j, 1)
        update(j, 0, s1, masked)
        update(j, 1, s2, masked)

    def body(j, carry):
        chunk(j, False)
        return carry

    lax.fori_loop(0, qi, body, 0)
    chunk(qi, True)

    lp = lamp_ref[...]
    lam = (jnp.exp(jnp.sum(lp[0:1] * lp[1:2], axis=-1, keepdims=True))
           - jnp.exp(jnp.sum(lp[2:3] * lp[3:4], axis=-1, keepdims=True)) + lam_init)
    o_all = acc_ref[...] / l_ref[...]
    o = o_all[0:tq] - lam * o_all[tq:2 * tq]
    ms = jnp.mean(o * o, axis=-1, keepdims=True)
    o_ref[...] = (o * lax.rsqrt(ms + EPS) * sg_ref[...] * (1.0 - lam_init)).astype(BF16)


def _attention(q, k, v, lamp, sg, *, batch, lp_len, tq, lam_init):
    n, qw = q.shape
    nq = lp_len // tq
    heads = qw // LANES
    kern = functools.partial(_attn_kernel, lam_init=lam_init)
    return pl.pallas_call(
        kern,
        grid=(batch, heads, nq),
        in_specs=[
            pl.BlockSpec((tq, LANES), lambda b, h, i: (b * nq + i, h)),
            pl.BlockSpec((lp_len, LANES), lambda b, h, i: (b, h)),
            pl.BlockSpec((lp_len, LANES), lambda b, h, i: (b, h)),
            pl.BlockSpec(lamp.shape, lambda b, h, i: (0, 0)),
            pl.BlockSpec(sg.shape, lambda b, h, i: (0, 0)),
        ],
        out_specs=pl.BlockSpec((tq, LANES), lambda b, h, i: (b * nq + i, h)),
        out_shape=jax.ShapeDtypeStruct((n, v.shape[1]), BF16),
        scratch_shapes=[
            pltpu.VMEM((2 * tq, LANES), BF16),
            pltpu.VMEM((2 * tq, LANES), F32),
            pltpu.VMEM((2 * tq, LANES), F32),
            pltpu.VMEM((2 * tq, LANES), F32),
        ],
        compiler_params=_cparams(3),
        name="diffattn",
    )(q, k, v, lamp, sg)


def _pack_rows(x):
    w = x.shape[1] // 2
    lo = lax.bitcast_convert_type(x[:, :w].astype(BF16).astype(F32), jnp.uint32)
    hi = lax.bitcast_convert_type(x[:, w:].astype(BF16).astype(F32), jnp.uint32)
    return lax.shift_right_logical(lo, jnp.uint32(16)) | (hi & jnp.uint32(0xFFFF0000))


def _unpack_rows(planes):
    w = jnp.concatenate(planes, axis=1)
    lo = lax.bitcast_convert_type(lax.shift_left(w, jnp.uint32(16)), F32)
    hi = lax.bitcast_convert_type(w & jnp.uint32(0xFFFF0000), F32)
    return jnp.concatenate([lo, hi], axis=1).astype(BF16)


def _outproj_kernel(hp_ref, cy_ref, o_ref, wout_ref, g2_ref, wrh_ref, wrl_ref, br_ref, tri_ref,
                    h1_ref, xp0_ref, xp1_ref, xp2_ref, xp3_ref, route_ref, cnt_ref, run_ref,
                    *, tiles_per_seq, seq_len):
    i = pl.program_id(0)
    tm = hp_ref.shape[0]
    cw = cy_ref.shape[1]
    h1 = (hp_ref[...]
          + jnp.dot(cy_ref[...], wout_ref[0:cw, :], preferred_element_type=F32)
          + jnp.dot(o_ref[...], wout_ref[cw:, :], preferred_element_type=F32))
    h1_ref[...] = h1
    ms = jnp.mean(h1 * h1, axis=-1, keepdims=True)
    xn = h1 * lax.rsqrt(ms + EPS) * g2_ref[...]
    xw = _pack_rows(xn)
    for c, ref in enumerate((xp0_ref, xp1_ref, xp2_ref, xp3_ref)):
        ref[...] = xw[:, c * LANES:(c + 1) * LANES]

    x_hi = xn.astype(BF16)
    x_lo = (xn - x_hi.astype(F32)).astype(BF16)
    logits = (jnp.dot(x_hi, wrh_ref[...], preferred_element_type=F32)
              + jnp.dot(x_hi, wrl_ref[...], preferred_element_type=F32)
              + jnp.dot(x_lo, wrh_ref[...], preferred_element_type=F32)
              + br_ref[...])
    lane = lax.broadcasted_iota(jnp.int32, logits.shape, 1)
    big = jnp.int32(4 * LANES)

    def first_argmax(vals, vmax):
        return jnp.min(jnp.where(vals == vmax, lane, big), axis=-1, keepdims=True)

    gl = jnp.where(lane < N_GROUPS, logits, NEG_BIG)
    gmax = jnp.max(gl, axis=-1, keepdims=True)
    g_val = 1.0 / jnp.sum(jnp.exp(gl - gmax), axis=-1, keepdims=True)
    g_idx = first_argmax(gl, gmax)
    lo = EXPERT_LANE0 + EXPERTS_PER_GROUP * g_idx
    el = jnp.where((lane >= lo) & (lane < lo + EXPERTS_PER_GROUP), logits, NEG_BIG)
    m1 = jnp.max(el, axis=-1, keepdims=True)
    i1 = first_argmax(el, m1)
    el2 = jnp.where(lane == i1, NEG_BIG, el)
    m2 = jnp.max(el2, axis=-1, keepdims=True)
    i2 = first_argmax(el2, m2)
    r = jnp.exp(m2 - m1)
    gate1 = g_val / (1.0 + r)
    gate2 = g_val * r / (1.0 + r)

    prow = (i % tiles_per_seq) * tm + lax.broadcasted_iota(jnp.int32, logits.shape, 0)
    valid = prow < seq_len
    oh1 = jnp.where(valid & (lane == i1), 1.0, 0.0)
    oh2 = jnp.where(valid & (lane == i2), 1.0, 0.0)

    @pl.when(i == 0)
    def _():
        run_ref[...] = jnp.zeros_like(run_ref)

    run = run_ref[0:1, :]
    tri = tri_ref[...]
    pre1 = jnp.dot(tri, oh1.astype(BF16), preferred_element_type=F32)
    pre2 = jnp.dot(tri, oh2.astype(BF16), preferred_element_type=F32)
    tot1 = jnp.sum(oh1, axis=0, keepdims=True)
    tot2 = jnp.sum(oh2, axis=0, keepdims=True)
    rank1 = jnp.sum(oh1 * (pre1 + run), axis=-1, keepdims=True)
    rank2 = jnp.sum(oh2 * (pre2 + run + tot1), axis=-1, keepdims=True)
    new_run = run + tot1 + tot2
    run_ref[...] = jnp.broadcast_to(new_run, run_ref.shape)
    cnt_ref[...] = jnp.broadcast_to(new_run, cnt_ref.shape)

    e1 = (i1 - EXPERT_LANE0).astype(F32)
    e2 = (i2 - EXPERT_LANE0).astype(F32)
    packed = jnp.where(lane == 0, e1, jnp.where(lane == 1, e2, jnp.where(lane == 2, gate1, jnp.where(
        lane == 3, gate2, jnp.where(lane == 4, rank1, jnp.where(lane == 5, rank2, 0.0))))))
    route_ref[...] = packed[:, 0:ROUTE_COLS]


def _outproj(hp, convy, o, w_out, g2, wr_hi, wr_lo, br, tri, *, tiles_per_seq, seq_len, tm):
    n, d = hp.shape
    const = lambda i: (0, 0)
    tile = lambda i: (i, 0)
    kern = functools.partial(_outproj_kernel, tiles_per_seq=tiles_per_seq, seq_len=seq_len)
    return pl.pallas_call(
        kern,
        grid=(n // tm,),
        in_specs=[
            pl.BlockSpec((tm, d), tile),
            pl.BlockSpec((tm, convy.shape[1]), tile),
            pl.BlockSpec((tm, o.shape[1]), tile),
            pl.BlockSpec(w_out.shape, const),
            pl.BlockSpec((1, d), const),
            pl.BlockSpec(wr_hi.shape, const),
            pl.BlockSpec(wr_lo.shape, const),
            pl.BlockSpec((1, LANES), const),
            pl.BlockSpec(tri.shape, const),
        ],
        out_specs=[
            pl.BlockSpec((tm, d), tile),
            *[pl.BlockSpec((tm, LANES), tile)] * ROW_PLANES,
            pl.BlockSpec((tm, ROUTE_COLS), tile),
            pl.BlockSpec((8, LANES), const),
        ],
        out_shape=[
            jax.ShapeDtypeStruct((n, d), F32),
            *[jax.ShapeDtypeStruct((n, LANES), jnp.uint32)] * ROW_PLANES,
            jax.ShapeDtypeStruct((n, ROUTE_COLS), F32),
            jax.ShapeDtypeStruct((8, LANES), F32),
        ],
        scratch_shapes=[pltpu.VMEM((8, LANES), F32)],
        compiler_params=_cparams(1),
        name="outproj_router",
    )(hp, convy, o, w_out, g2, wr_hi, wr_lo, br, tri)


def _row_copy(src_hbm, src_row, dst_ref, dst_row, sem):
    return pltpu.make_async_copy(src_hbm.at[pl.ds(src_row, 1), :], dst_ref.at[pl.ds(dst_row, 1), :], sem)


def _dispatch_kernel(dest_ref, xn_hbm, xs_hbm, xbuf, sem_in, sem, *, tiles_per_seq, lp_len, tile):
    t = pl.program_id(0)
    base = (t // tiles_per_seq) * lp_len + (t % tiles_per_seq) * tile
    tile_copy = pltpu.make_async_copy(xn_hbm.at[pl.ds(base, tile), :], xbuf, sem_in)
    tile_copy.start()
    tile_copy.wait()

    def issue(r, carry):
        for kk in range(TOP_K):
            _row_copy(xbuf, r, xs_hbm, dest_ref[0, 0, TOP_K * r + kk], sem).start()
        return carry

    lax.fori_loop(0, tile, issue, 0)

    def drain(r, carry):
        for kk in range(TOP_K):
            _row_copy(xbuf, r, xs_hbm, dest_ref[0, 0, TOP_K * r + kk], sem).wait()
        return carry

    lax.fori_loop(0, tile, drain, 0)


def _dispatch(dest_tiles, xn, *, p_rows, tiles_per_seq, lp_len, tile):
    n_tiles = dest_tiles.shape[0]
    kern = functools.partial(_dispatch_kernel, tiles_per_seq=tiles_per_seq, lp_len=lp_len, tile=tile)
    return pl.pallas_call(
        kern,
        grid=(n_tiles,),
        in_specs=[
            pl.BlockSpec((1, 1, TOP_K * tile), lambda t: (t, 0, 0), memory_space=pltpu.SMEM),
            pl.BlockSpec(memory_space=pl.ANY),
        ],
        out_specs=pl.BlockSpec(memory_space=pl.ANY),
        out_shape=jax.ShapeDtypeStruct((p_rows, xn.shape[1]), xn.dtype),
        scratch_shapes=[pltpu.VMEM((tile, xn.shape[1]), xn.dtype), pltpu.SemaphoreType.DMA(()),
                        pltpu.SemaphoreType.DMA(())],
        compiler_params=_cparams(1),
        name="moe_dispatch",
    )(dest_tiles, xn)


def _experts_kernel(be_ref, nv_ref, xs_ref, wg_ref, wu_ref, wd_ref, y_ref, wgb_ref, wub_ref, wdb_ref):
    i = pl.program_id(0)
    e = be_ref[i]
    prev = be_ref[jnp.maximum(i - 1, 0)]

    @pl.when((i == 0) | (e != prev))
    def _():
        wgb_ref[...] = wg_ref[0].astype(BF16)
        wub_ref[...] = wu_ref[0].astype(BF16)
        wdb_ref[...] = wd_ref[0].astype(BF16)

    nv = nv_ref[i]

    @pl.when(nv > 0)
    def _():
        xs = xs_ref[...]
        row = lax.broadcasted_iota(jnp.int32, xs.shape, 0)
        x = jnp.where(row < nv, xs, 0.0).astype(BF16)
        hg = jnp.dot(x, wgb_ref[...], preferred_element_type=F32)
        hu = jnp.dot(x, wub_ref[...], preferred_element_type=F32)
        hid = (hg / (1.0 + jnp.exp(-hg)) * hu).astype(BF16)
        y_ref[...] = jnp.dot(hid, wdb_ref[...], preferred_element_type=F32)

    @pl.when(nv == 0)
    def _():
        y_ref[...] = jnp.zeros_like(y_ref)


def _experts(block_e, nvalid, xs, w_gate, w_up, w_down):
    p_rows, d = xs.shape
    ff = w_gate.shape[2]
    n_blocks = p_rows // MOE_BLOCK
    grid_spec = pltpu.PrefetchScalarGridSpec(
        num_scalar_prefetch=2,
        grid=(n_blocks,),
        in_specs=[
            pl.BlockSpec((MOE_BLOCK, d), lambda i, be, nv: (i, 0)),
            pl.BlockSpec((1, d, ff), lambda i, be, nv: (be[i], 0, 0)),
            pl.BlockSpec((1, d, ff), lambda i, be, nv: (be[i], 0, 0)),
            pl.BlockSpec((1, ff, d), lambda i, be, nv: (be[i], 0, 0)),
        ],
        out_specs=pl.BlockSpec((MOE_BLOCK, d), lambda i, be, nv: (i, 0)),
        scratch_shapes=[
            pltpu.VMEM((d, ff), BF16),
            pltpu.VMEM((d, ff), BF16),
            pltpu.VMEM((ff, d), BF16),
        ],
    )
    return pl.pallas_call(
        _experts_kernel,
        grid_spec=grid_spec,
        out_shape=jax.ShapeDtypeStruct((p_rows, d), F32),
        compiler_params=_cparams(1),
        name="moe_experts",
    )(block_e, nvalid, xs, w_gate, w_up, w_down)


def _combine_kernel(dest_ref, gates_ref, h1_hbm, y_hbm, out_ref, hbuf, ya, yb, sem_h, sem_a, sem_b,
                    *, lp_len, tile):
    b = pl.program_id(0)
    i = pl.program_id(1)
    start = b * lp_len + N_META + i * tile
    h_copy = pltpu.make_async_copy(h1_hbm.at[pl.ds(start, tile), :], hbuf, sem_h)
    h_copy.start()

    def issue(r, carry):
        _row_copy(y_hbm, dest_ref[0, 0, TOP_K * r], ya, r, sem_a).start()
        _row_copy(y_hbm, dest_ref[0, 0, TOP_K * r + 1], yb, r, sem_b).start()
        return carry

    lax.fori_loop(0, tile, issue, 0)

    def drain(r, carry):
        _row_copy(y_hbm, dest_ref[0, 0, TOP_K * r], ya, r, sem_a).wait()
        _row_copy(y_hbm, dest_ref[0, 0, TOP_K * r + 1], yb, r, sem_b).wait()
        return carry

    lax.fori_loop(0, tile, drain, 0)
    h_copy.wait()
    g = gates_ref[0]
    out_ref[0] = hbuf[...] + g[:, 0:1] * ya[...] + g[:, 1:2] * yb[...]


def _combine(dest_tiles, gates, h1, y, *, batch, seq, lp_len, tile):
    d = h1.shape[1]
    nt = seq // tile
    kern = functools.partial(_combine_kernel, lp_len=lp_len, tile=tile)
    return pl.pallas_call(
        kern,
        grid=(batch, nt),
        in_specs=[
            pl.BlockSpec((1, 1, TOP_K * tile), lambda b, i: (b * nt + i, 0, 0), memory_space=pltpu.SMEM),
            pl.BlockSpec((1, tile, TOP_K), lambda b, i: (b, i, 0)),
            pl.BlockSpec(memory_space=pl.ANY),
            pl.BlockSpec(memory_space=pl.ANY),
        ],
        out_specs=pl.BlockSpec((1, tile, d), lambda b, i: (b, i, 0)),
        out_shape=jax.ShapeDtypeStruct((batch, seq, d), F32),
        scratch_shapes=[
            pltpu.VMEM((tile, d), F32),
            pltpu.VMEM((tile, d), F32),
            pltpu.VMEM((tile, d), F32),
            pltpu.SemaphoreType.DMA(()),
            pltpu.SemaphoreType.DMA(()),
            pltpu.SemaphoreType.DMA(()),
        ],
        compiler_params=_cparams(2),
        name="moe_combine",
    )(dest_tiles, gates, h1, y)


def _sc_workers():
    info = plsc.get_sparse_core_info()
    return info.num_cores, info.num_cores * info.num_subcores


def _sc_scatter_rows(planes, idx_a, idx_b, out_rows):
    n_win = idx_a.shape[0]
    n_cores, n_workers = _sc_workers()
    trips = -(-n_win // n_workers)
    mesh = plsc.VectorSubcoreMesh(core_axis_name="c", subcore_axis_name="s")

    def body(*refs):
        xs = refs[0:ROW_PLANES]
        ia_hbm, ib_hbm = refs[ROW_PLANES:ROW_PLANES + 2]
        outs = refs[ROW_PLANES + 2:2 * ROW_PLANES + 2]
        ia_v, ib_v, buf, sem = refs[2 * ROW_PLANES + 2:]
        wid = lax.axis_index("s") * n_cores + lax.axis_index("c")

        def step(t, carry):
            g = wid + t * n_workers

            @pl.when(g < n_win)
            def _():
                pltpu.sync_copy(ia_hbm.at[g], ia_v)
                pltpu.sync_copy(ib_hbm.at[g], ib_v)
                row0 = pl.multiple_of(g * SC_WINDOW, SC_WINDOW)
                loads = [pltpu.async_copy(xs[c].at[pl.ds(row0, SC_WINDOW)], buf.at[c], sem)
                         for c in range(ROW_PLANES)]
                for cp in loads:
                    cp.wait()
                stores = [pltpu.async_copy(buf.at[c], outs[c].at[iv], sem)
                          for c in range(ROW_PLANES) for iv in (ia_v, ib_v)]
                for cp in stores:
                    cp.wait()

            return carry

        lax.fori_loop(0, trips, step, 0)

    kern = pl.kernel(
        body,
        out_type=[jax.ShapeDtypeStruct((out_rows, LANES), jnp.uint32)] * ROW_PLANES,
        mesh=mesh,
        scratch_types=[
            pltpu.VMEM((SC_WINDOW,), jnp.int32),
            pltpu.VMEM((SC_WINDOW,), jnp.int32),
            pltpu.VMEM((ROW_PLANES, SC_WINDOW, LANES), jnp.uint32),
            pltpu.SemaphoreType.DMA,
        ],
        name="moe_dispatch_sc",
    )
    return kern(*planes, idx_a, idx_b)


def _sc_gather_rows(planes, idx_a, idx_b):
    n_win = idx_a.shape[0]
    n_cores, n_workers = _sc_workers()
    trips = -(-n_win // n_workers)
    mesh = plsc.VectorSubcoreMesh(core_axis_name="c", subcore_axis_name="s")

    def body(*refs):
        ys = refs[0:ROW_PLANES]
        ia_hbm, ib_hbm = refs[ROW_PLANES:ROW_PLANES + 2]
        outs_a = refs[ROW_PLANES + 2:2 * ROW_PLANES + 2]
        outs_b = refs[2 * ROW_PLANES + 2:3 * ROW_PLANES + 2]
        iv, buf, sem = refs[3 * ROW_PLANES + 2:]
        wid = lax.axis_index("s") * n_cores + lax.axis_index("c")

        def step(t, carry):
            g = wid + t * n_workers

            @pl.when(g < n_win)
            def _():
                row0 = pl.multiple_of(g * SC_WINDOW, SC_WINDOW)
                for i_hbm, outs in ((ia_hbm, outs_a), (ib_hbm, outs_b)):
                    pltpu.sync_copy(i_hbm.at[g], iv)
                    loads = [pltpu.async_copy(ys[c].at[iv], buf.at[c], sem) for c in range(ROW_PLANES)]
                    for cp in loads:
                        cp.wait()
                    stores = [pltpu.async_copy(buf.at[c], outs[c].at[pl.ds(row0, SC_WINDOW)], sem)
                              for c in range(ROW_PLANES)]
                    for cp in stores:
                        cp.wait()

            return carry

        lax.fori_loop(0, trips, step, 0)

    n_rows = n_win * SC_WINDOW
    kern = pl.kernel(
        body,
        out_type=[jax.ShapeDtypeStruct((n_rows, LANES), jnp.uint32)] * (2 * ROW_PLANES),
        mesh=mesh,
        scratch_types=[
            pltpu.VMEM((SC_WINDOW,), jnp.int32),
            pltpu.VMEM((ROW_PLANES, SC_WINDOW, LANES), jnp.uint32),
            pltpu.SemaphoreType.DMA,
        ],
        name="moe_gather_sc",
    )
    res = kern(*planes, idx_a, idx_b)
    return res[:ROW_PLANES], res[ROW_PLANES:]


def _experts_kernel(be_ref, nv_ref, x0_ref, x1_ref, x2_ref, x3_ref, wg_ref, wu_ref, wd_ref,
                    y0_ref, y1_ref, y2_ref, y3_ref, wgb_ref, wub_ref, wdb_ref):
    i = pl.program_id(0)
    e = be_ref[i]
    prev = be_ref[jnp.maximum(i - 1, 0)]
    y_refs = (y0_ref, y1_ref, y2_ref, y3_ref)

    @pl.when((i == 0) | (e != prev))
    def _():
        wgb_ref[...] = wg_ref[0].astype(BF16)
        wub_ref[...] = wu_ref[0].astype(BF16)
        wdb_ref[...] = wd_ref[0].astype(BF16)

    nv = nv_ref[i]

    @pl.when(nv > 0)
    def _():
        xs = _unpack_rows([x0_ref[...], x1_ref[...], x2_ref[...], x3_ref[...]])
        row = lax.broadcasted_iota(jnp.int32, xs.shape, 0)
        x = jnp.where(row < nv, xs, jnp.zeros_like(xs))
        hg = jnp.dot(x, wgb_ref[...], preferred_element_type=F32)
        hu = jnp.dot(x, wub_ref[...], preferred_element_type=F32)
        hid = (hg / (1.0 + jnp.exp(-hg)) * hu).astype(BF16)
        yw = _pack_rows(jnp.dot(hid, wdb_ref[...], preferred_element_type=F32))
        for c, ref in enumerate(y_refs):
            ref[...] = yw[:, c * LANES:(c + 1) * LANES]

    @pl.when(nv == 0)
    def _():
        for ref in y_refs:
            ref[...] = jnp.zeros_like(ref)


def _experts(block_e, nvalid, xs_planes, w_gate, w_up, w_down):
    n_blocks = block_e.shape[0]
    _, d, ff = w_gate.shape
    blk = lambda i, be, nv: (i, 0)
    wsel = lambda i, be, nv: (be[i], 0, 0)
    grid_spec = pltpu.PrefetchScalarGridSpec(
        num_scalar_prefetch=2,
        grid=(n_blocks,),
        in_specs=[
            *[pl.BlockSpec((MOE_BLOCK, LANES), blk)] * ROW_PLANES,
            pl.BlockSpec((1, d, ff), wsel),
            pl.BlockSpec((1, d, ff), wsel),
            pl.BlockSpec((1, ff, d), wsel),
        ],
        out_specs=[pl.BlockSpec((MOE_BLOCK, LANES), blk)] * ROW_PLANES,
        scratch_shapes=[
            pltpu.VMEM((d, ff), BF16),
            pltpu.VMEM((d, ff), BF16),
            pltpu.VMEM((ff, d), BF16),
        ],
    )
    return pl.pallas_call(
        _experts_kernel,
        grid_spec=grid_spec,
        out_shape=[jax.ShapeDtypeStruct((n_blocks * MOE_BLOCK, LANES), jnp.uint32)] * ROW_PLANES,
        compiler_params=_cparams(1),
        name="moe_experts",
    )(block_e, nvalid, *xs_planes, w_gate, w_up, w_down)


def _combine_kernel(gates_ref, h1_hbm, *refs, lp_len, tile):
    a_refs = refs[0:ROW_PLANES]
    b_refs = refs[ROW_PLANES:2 * ROW_PLANES]
    out_ref, hbuf, sem_h = refs[2 * ROW_PLANES:]
    b = pl.program_id(0)
    i = pl.program_id(1)
    start = b * lp_len + N_META + i * tile
    h_copy = pltpu.make_async_copy(h1_hbm.at[pl.ds(start, tile), :], hbuf, sem_h)
    h_copy.start()
    ya = _unpack_rows([r[...] for r in a_refs]).astype(F32)
    yb = _unpack_rows([r[...] for r in b_refs]).astype(F32)
    g = gates_ref[0]
    moe = g[:, 0:1] * ya + g[:, 1:2] * yb
    h_copy.wait()
    out_ref[0] = hbuf[...] + moe


def _combine(gates, h1, a_planes, b_planes, *, batch, seq, lp_len, tile):
    d = h1.shape[1]
    nt = seq // tile
    kern = functools.partial(_combine_kernel, lp_len=lp_len, tile=tile)
    rows = lambda b, i: (b * nt + i, 0)
    return pl.pallas_call(
        kern,
        grid=(batch, nt),
        in_specs=[
            pl.BlockSpec((1, tile, TOP_K), lambda b, i: (b, i, 0)),
            pl.BlockSpec(memory_space=pl.ANY),
            *[pl.BlockSpec((tile, LANES), rows)] * (2 * ROW_PLANES),
        ],
        out_specs=pl.BlockSpec((1, tile, d), lambda b, i: (b, i, 0)),
        out_shape=jax.ShapeDtypeStruct((batch, seq, d), F32),
        scratch_shapes=[pltpu.VMEM((tile, d), F32), pltpu.SemaphoreType.DMA(())],
        compiler_params=_cparams(2),
        name="moe_combine",
    )(gates, h1, *a_planes, *b_planes)


def _rope_tables(length, lp_len):
    half = ROPE_DIM // 2
    pos = jnp.arange(length, dtype=F32)
    inv_freq = ROPE_THETA ** (-jnp.arange(0, ROPE_DIM, 2, dtype=F32) / ROPE_DIM)
    ang = pos[:, None] * inv_freq[None, :]
    cos = jnp.cos(ang)
    sin = jnp.sin(ang)
    ones = jnp.ones((length, HEAD_DIM - ROPE_DIM), F32)
    zeros_h = jnp.zeros((length, half), F32)
    zeros_r = jnp.zeros((length, HEAD_DIM - ROPE_DIM), F32)
    c = jnp.concatenate([cos, cos, ones], axis=1)
    s1 = jnp.concatenate([zeros_h, sin, zeros_r], axis=1)
    s2 = jnp.concatenate([-sin, zeros_h, zeros_r], axis=1)
    pad = ((0, lp_len - length), (0, 0))
    rep = LANES // HEAD_DIM
    return tuple(jnp.pad(jnp.tile(t, (1, rep)), pad) for t in (c, s1, s2))


def _layer(hp, l, batch, length, lp_len, tm, norm1_g, w_in, conv_w, q_norm_g, k_norm_g, lambda_q1, lambda_k1,
           lambda_q2, lambda_k2, subln_g, w_out, norm2_g, w_router_group, b_router_group, w_router_expert,
           b_router_expert, w_gate, w_up, w_down, rope, last):
    n, d = hp.shape
    tiles_per_seq = lp_len // tm
    cw = conv_w.shape[2]
    qw = N_HEADS * 2 * HEAD_DIM
    lam_init = 0.8 - 0.6 * math.exp(-0.3 * l)

    reps = qw // HEAD_DIM
    gq = jnp.tile(q_norm_g[l] * (HEAD_DIM ** -0.5 * LOG2E), reps)[None, :]
    gk = jnp.tile(k_norm_g[l], reps)[None, :]
    seg = jnp.arange(qw) // HEAD_DIM
    bd = (seg[:, None] == seg[None, :]).astype(BF16)
    convy, q, k, v = _inproj(hp, norm1_g[l][None, :], w_in[l].astype(BF16), conv_w[l], gq, gk, bd, *rope,
                             tiles_per_seq=tiles_per_seq, tm=tm)

    lamp = jnp.stack([lambda_q1[l], lambda_k1[l], lambda_q2[l], lambda_k2[l]]).astype(F32)
    o = _attention(q, k, v, lamp, subln_g[l][None, :], batch=batch, lp_len=lp_len, tq=tm, lam_init=lam_init)

    lane_pad = LANES - N_GROUPS - N_EXPERTS
    wr = jnp.pad(jnp.concatenate([w_router_group[l], w_router_expert[l]], axis=1), ((0, 0), (0, lane_pad)))
    wr_hi = wr.astype(BF16)
    wr_lo = (wr - wr_hi.astype(F32)).astype(BF16)
    br = jnp.pad(jnp.concatenate([b_router_group[l], b_router_expert[l]]), (0, lane_pad))[None, :]
    ridx = jnp.arange(tm)
    tri = (ridx[None, :] < ridx[:, None]).astype(BF16)
    h1, *rest = _outproj(hp, convy, o, w_out[l].astype(BF16), norm2_g[l][None, :], wr_hi, wr_lo, br,
                         tri, tiles_per_seq=tiles_per_seq, seq_len=length, tm=tm)
    x_planes = rest[:ROW_PLANES]
    route, cnt = rest[ROW_PLANES:]

    route = route.reshape(batch, lp_len, ROUTE_COLS)
    eid = route[..., 0:TOP_K].astype(jnp.int32)
    gates = route[..., TOP_K:2 * TOP_K]
    rank = route[..., 2 * TOP_K:3 * TOP_K].astype(jnp.int32)
    counts = cnt[0, EXPERT_LANE0:EXPERT_LANE0 + N_EXPERTS].astype(jnp.int32)
    a = batch * length * TOP_K
    n_blocks = -(-a // MOE_BLOCK) + N_EXPERTS
    p_rows = n_blocks * MOE_BLOCK
    padded = (counts + MOE_BLOCK - 1) // MOE_BLOCK * MOE_BLOCK
    pends = jnp.cumsum(padded)
    pstarts = pends - padded
    experts = jnp.arange(N_EXPERTS, dtype=jnp.int32)

    def lookup(table, idx):
        return jnp.sum(jnp.where(idx[..., None] == experts, table, 0), axis=-1)

    n_pad = lp_len - length
    pos = jnp.arange(lp_len, dtype=jnp.int32)[None, :, None]
    spare = p_rows + ((jnp.arange(batch, dtype=jnp.int32)[:, None, None] * n_pad + (pos - length)) * TOP_K
                      + jnp.arange(TOP_K, dtype=jnp.int32)[None, None, :])
    dest = jnp.where(pos < length, lookup(pstarts, eid) + rank, spare)
    spare_rows = -(-(batch * n_pad * TOP_K) // MOE_BLOCK) * MOE_BLOCK
    blk0 = jnp.arange(n_blocks, dtype=jnp.int32) * MOE_BLOCK
    block_e = jnp.minimum(jnp.sum((pends[None, :] <= blk0[:, None]).astype(jnp.int32), axis=1), N_EXPERTS - 1)
    nvalid = jnp.clip(lookup(counts, block_e) - (blk0 - lookup(pstarts, block_e)), 0, MOE_BLOCK)

    assert (batch * lp_len) % SC_WINDOW == 0
    xs_planes = _sc_scatter_rows(x_planes, dest[..., 0].reshape(-1, SC_WINDOW),
                                 dest[..., 1].reshape(-1, SC_WINDOW), p_rows + spare_rows)
    y_planes = _experts(block_e, nvalid, xs_planes, w_gate[l], w_up[l], w_down[l])

    if not last:
        raise NotImplementedError("only the final layer's combine (which drops the meta tokens) is implemented")
    seq = length - N_META
    assert (batch * seq) % SC_WINDOW == 0
    dest_x = dest[:, N_META:length]
    a_planes, b_planes = _sc_gather_rows(y_planes, dest_x[..., 0].reshape(-1, SC_WINDOW),
                                         dest_x[..., 1].reshape(-1, SC_WINDOW))
    ctile = _largest_tile(seq, 512, 8)
    return _combine(gates[:, N_META:length], h1, a_planes, b_planes, batch=batch, seq=seq, lp_len=lp_len,
                    tile=ctile)


def kernel(x, meta_tokens, norm1_g, w_in, conv_w, q_norm_g, k_norm_g, lambda_q1, lambda_k1, lambda_q2, lambda_k2,
           subln_g, w_out, norm2_g, w_router_group, b_router_group, w_router_expert, b_router_expert, w_gate,
           w_up, w_down):
    b, s, d = x.shape
    depth = w_in.shape[0]
    assert depth == 1, "a single layer is supported"
    length = s + N_META
    tm = TOKEN_TILE if length >= TOKEN_TILE else LANES
    lp_len = -(-length // tm) * tm
    hp = jnp.concatenate([jnp.broadcast_to(meta_tokens[None].astype(x.dtype), (b, N_META, d)), x,
                          jnp.zeros((b, lp_len - length, d), x.dtype)], axis=1).reshape(b * lp_len, d)
    rope = _rope_tables(length, lp_len)
    return _layer(hp, 0, b, length, lp_len, tm, norm1_g, w_in, conv_w, q_norm_g, k_norm_g, lambda_q1, lambda_k1,
                  lambda_q2, lambda_k2, subln_g, w_out, norm2_g, w_router_group, b_router_group,
                  w_router_expert, b_router_expert, w_gate, w_up, w_down, rope, last=True)
```

```python
import functools
import math

import jax
import jax.numpy as jnp
from jax import lax
from jax.experimental import pallas as pl
from jax.experimental.pallas import tpu as pltpu
from jax.experimental.pallas import tpu_sc as plsc

F32 = jnp.float32
BF16 = jnp.bfloat16

N_META = 16
CONV_K = 3
N_HEADS = 4
HEAD_DIM = 64
V_DIM = 2 * HEAD_DIM
ROPE_DIM = HEAD_DIM // 4
ROPE_THETA = 500000.0
N_GROUPS = 4
EXPERTS_PER_GROUP = 8
N_EXPERTS = N_GROUPS * EXPERTS_PER_GROUP
TOP_K = 2
EPS = 1e-6
LOG2E = 1.4426950408889634

LANES = 128
TOKEN_TILE = 640
ATTN_WIDE_CHUNKS = 2
MOE_BLOCK = 512
ROUTE_COLS = 8
ROW_PLANES = 4
SC_WINDOW = 128
EXPERT_LANE0 = N_GROUPS
NEG_BIG = -1e30
VMEM_LIMIT = 56 * 1024 * 1024


def _largest_tile(n, cap, mult):
    for t in range(min(cap, n), 0, -1):
        if n % t == 0 and t % mult == 0:
            return t
    raise ValueError(f"no tile for {n}")


def _cparams(n_axes):
    return pltpu.CompilerParams(dimension_semantics=("arbitrary",) * n_axes,
                                vmem_limit_bytes=VMEM_LIMIT)


def _inproj_kernel(x_ref, g1_ref, win_ref, convw_ref, gq_ref, gk_ref, bd_ref, rc_ref, rs1_ref, rs2_ref,
                   convy_ref, q_ref, k_ref, v_ref, carry_ref, *, tiles_per_seq, cw, qw):
    i = pl.program_id(0)
    tm = x_ref.shape[0]
    x = x_ref[...]
    ms = jnp.mean(x * x, axis=-1, keepdims=True)
    xn = (x * lax.rsqrt(ms + EPS) * g1_ref[...]).astype(BF16)
    q0 = 3 * cw

    def proj(lo, hi):
        return jnp.dot(xn, win_ref[:, lo:hi], preferred_element_type=F32)

    u_conv = proj(0, q0)
    u_q = proj(q0, q0 + qw)

    z = u_conv[:, cw:2 * cw] * u_conv[:, 2 * cw:3 * cw]

    @pl.when(i % tiles_per_seq == 0)
    def _():
        carry_ref[...] = jnp.zeros_like(carry_ref)

    prev = carry_ref[...]
    p1 = prev[7:8]
    p2 = prev[6:7]
    row = lax.broadcasted_iota(jnp.int32, z.shape, 0)
    z1 = jnp.where(row == 0, p1, pltpu.roll(z, 1, axis=0))
    z2 = jnp.where(row == 0, p2, jnp.where(row == 1, p1, pltpu.roll(z, 2, axis=0)))
    carry_ref[...] = z[tm - 8:tm]
    w = convw_ref[...]
    conv = w[0:1] * z2 + w[1:2] * z1 + w[2:3] * z
    convy_ref[...] = (u_conv[:, 0:cw] * conv).astype(BF16)

    rc = rc_ref[...]
    rs1 = rs1_ref[...]
    rs2 = rs2_ref[...]

    def norm_rope(t, g_ref):
        ss = jnp.dot((t * t).astype(BF16), bd_ref[...], preferred_element_type=F32)
        tn = t * lax.rsqrt(ss * (1.0 / HEAD_DIM) + EPS) * g_ref[...]
        outs = []
        for c in range(qw // LANES):
            ch = tn[:, c * LANES:(c + 1) * LANES]
            outs.append(ch * rc + pltpu.roll(ch, ROPE_DIM // 2, axis=1) * rs1
                        + pltpu.roll(ch, LANES - ROPE_DIM // 2, axis=1) * rs2)
        return jnp.concatenate(outs, axis=1).astype(BF16)

    u_k = proj(q0 + qw, q0 + 2 * qw)
    q_ref[...] = norm_rope(u_q, gq_ref)
    u_v = proj(q0 + 2 * qw, win_ref.shape[1])
    k_ref[...] = norm_rope(u_k, gk_ref)
    v_ref[...] = u_v.astype(BF16)


def _inproj(hp, g1, w_in, conv_w, gq, gk, bd, rc, rs1, rs2, *, tiles_per_seq, tm):
    n, d = hp.shape
    cw = conv_w.shape[1]
    qw = gq.shape[1]
    aw = w_in.shape[1] - 3 * cw - 2 * qw
    const = lambda i: (0, 0)
    tile = lambda i: (i, 0)
    pos = lambda i: (i % tiles_per_seq, 0)
    kern = functools.partial(_inproj_kernel, tiles_per_seq=tiles_per_seq, cw=cw, qw=qw)
    return pl.pallas_call(
        kern,
        grid=(n // tm,),
        in_specs=[
            pl.BlockSpec((tm, d), tile),
            pl.BlockSpec((1, d), const),
            pl.BlockSpec(w_in.shape, const),
            pl.BlockSpec(conv_w.shape, const),
            pl.BlockSpec((1, qw), const),
            pl.BlockSpec((1, qw), const),
            pl.BlockSpec(bd.shape, const),
            pl.BlockSpec((tm, LANES), pos),
            pl.BlockSpec((tm, LANES), pos),
            pl.BlockSpec((tm, LANES), pos),
        ],
        out_specs=[
            pl.BlockSpec((tm, cw), tile),
            pl.BlockSpec((tm, qw), tile),
            pl.BlockSpec((tm, qw), tile),
            pl.BlockSpec((tm, aw), tile),
        ],
        out_shape=[
            jax.ShapeDtypeStruct((n, cw), BF16),
            jax.ShapeDtypeStruct((n, qw), BF16),
            jax.ShapeDtypeStruct((n, qw), BF16),
            jax.ShapeDtypeStruct((n, aw), BF16),
        ],
        scratch_shapes=[pltpu.VMEM((8, cw), F32)],
        compiler_params=_cparams(1),
        name="inproj",
    )(hp, g1, w_in, conv_w, gq, gk, bd, rc, rs1, rs2)


def _attn_kernel(q_ref, k_ref, v_ref, lamp_ref, sg_ref, o_ref, qs_ref, m_ref, l_ref, acc_ref, *, lam_init):
    qi = pl.program_id(2)
    tq = q_ref.shape[0]
    q = q_ref[...]
    lane = lax.broadcasted_iota(jnp.int32, q.shape, 1)
    zero = jnp.zeros_like(q)
    qs_ref[0:tq, :] = jnp.where(lane < HEAD_DIM, q, zero)
    qs_ref[tq:2 * tq, :] = jnp.where(lane >= HEAD_DIM, q, zero)
    m_ref[...] = jnp.full_like(m_ref, NEG_BIG)
    l_ref[...] = jnp.zeros_like(l_ref)
    acc_ref[...] = jnp.zeros_like(acc_ref)

    def scores(off, width, which):
        kc = k_ref[pl.ds(off, width), :]
        return lax.dot_general(qs_ref[pl.ds(which * tq, tq), :], kc, (((1,), (1,)), ((), ())),
                               preferred_element_type=F32)

    def update(off, width, which, s, masked):
        vc = jnp.concatenate([v_ref[pl.ds(off, width), :], jnp.ones((width, LANES), BF16)], axis=1)
        rows = pl.ds(which * tq, tq)
        if masked:
            r = lax.broadcasted_iota(jnp.int32, s.shape, 0)
            c = lax.broadcasted_iota(jnp.int32, s.shape, 1)
            s = jnp.where(c <= r, s, NEG_BIG)
        m_prev = m_ref[rows, :]
        m_new = jnp.maximum(m_prev, jnp.max(s, axis=-1, keepdims=True))
        alpha = jnp.exp2(m_prev - m_new)
        p = jnp.exp2((s - jnp.tile(m_new, (1, width // LANES))).astype(BF16))
        pv = jnp.dot(p, vc, preferred_element_type=F32)
        l_ref[rows, :] = alpha * l_ref[rows, :] + pv[:, LANES:]
        acc_ref[rows, :] = alpha * acc_ref[rows, :] + pv[:, :LANES]
        m_ref[rows, :] = m_new

    def chunk(off, width, masked):
        s1 = scores(off, width, 0)
        s2 = scores(off, width, 1)
        update(off, width, 0, s1, masked)
        update(off, width, 1, s2, masked)

    wide = ATTN_WIDE_CHUNKS * tq

    def body(j, carry):
        chunk(pl.multiple_of(j * wide, wide), wide, False)
        return carry

    n_wide = qi // ATTN_WIDE_CHUNKS
    lax.fori_loop(0, n_wide, body, 0)
    for extra in range(ATTN_WIDE_CHUNKS - 1):
        @pl.when(n_wide * ATTN_WIDE_CHUNKS + extra < qi)
        def _():
            chunk(pl.multiple_of((n_wide * ATTN_WIDE_CHUNKS + extra) * tq, tq), tq, False)

    chunk(pl.multiple_of(qi * tq, tq), tq, True)

    lp = lamp_ref[...]
    lam = (jnp.exp(jnp.sum(lp[0:1] * lp[1:2], axis=-1, keepdims=True))
           - jnp.exp(jnp.sum(lp[2:3] * lp[3:4], axis=-1, keepdims=True)) + lam_init)
    o_all = acc_ref[...] / l_ref[...]
    o = o_all[0:tq] - lam * o_all[tq:2 * tq]
    ms = jnp.mean(o * o, axis=-1, keepdims=True)
    o_ref[...] = (o * lax.rsqrt(ms + EPS) * sg_ref[...] * (1.0 - lam_init)).astype(BF16)


def _attention(q, k, v, lamp, sg, *, batch, lp_len, tq, lam_init):
    n, qw = q.shape
    nq = lp_len // tq
    heads = qw // LANES
    kern = functools.partial(_attn_kernel, lam_init=lam_init)
    return pl.pallas_call(
        kern,
        grid=(batch, heads, nq),
        in_specs=[
            pl.BlockSpec((tq, LANES), lambda b, h, i: (b * nq + i, h)),
            pl.BlockSpec((lp_len, LANES), lambda b, h, i: (b, h)),
            pl.BlockSpec((lp_len, LANES), lambda b, h, i: (b, h)),
            pl.BlockSpec(lamp.shape, lambda b, h, i: (0, 0)),
            pl.BlockSpec(sg.shape, lambda b, h, i: (0, 0)),
        ],
        out_specs=pl.BlockSpec((tq, LANES), lambda b, h, i: (b * nq + i, h)),
        out_shape=jax.ShapeDtypeStruct((n, v.shape[1]), BF16),
        scratch_shapes=[
            pltpu.VMEM((2 * tq, LANES), BF16),
            pltpu.VMEM((2 * tq, LANES), F32),
            pltpu.VMEM((2 * tq, LANES), F32),
            pltpu.VMEM((2 * tq, LANES), F32),
        ],
        compiler_params=_cparams(3),
        name="diffattn",
    )(q, k, v, lamp, sg)


def _pack_rows(x):
    w = x.shape[1] // 2
    lo = lax.bitcast_convert_type(x[:, :w].astype(BF16).astype(F32), jnp.uint32)
    hi = lax.bitcast_convert_type(x[:, w:].astype(BF16).astype(F32), jnp.uint32)
    return lax.shift_right_logical(lo, jnp.uint32(16)) | (hi & jnp.uint32(0xFFFF0000))


def _unpack_rows(planes):
    w = jnp.concatenate(planes, axis=1)
    lo = lax.bitcast_convert_type(lax.shift_left(w, jnp.uint32(16)), F32)
    hi = lax.bitcast_convert_type(w & jnp.uint32(0xFFFF0000), F32)
    return jnp.concatenate([lo, hi], axis=1).astype(BF16)


def _outproj_kernel(hp_ref, cy_ref, o_ref, wout_ref, g2_ref, wrh_ref, wrl_ref, br_ref, tri_ref,
                    h1_ref, xp0_ref, xp1_ref, xp2_ref, xp3_ref, route_ref, cnt_ref, run_ref,
                    *, tiles_per_seq, seq_len):
    i = pl.program_id(0)
    tm = hp_ref.shape[0]
    cw = cy_ref.shape[1]
    mix = jnp.concatenate([cy_ref[...], o_ref[...]], axis=1)
    h1 = hp_ref[...] + jnp.dot(mix, wout_ref[...], preferred_element_type=F32)
    h1_ref[...] = h1
    ms = jnp.mean(h1 * h1, axis=-1, keepdims=True)
    xn = h1 * lax.rsqrt(ms + EPS) * g2_ref[...]
    xw = _pack_rows(xn)
    for c, ref in enumerate((xp0_ref, xp1_ref, xp2_ref, xp3_ref)):
        ref[...] = xw[:, c * LANES:(c + 1) * LANES]

    x_hi = xn.astype(BF16)
    x_lo = (xn - x_hi.astype(F32)).astype(BF16)
    hi_both = jnp.dot(x_hi, jnp.concatenate([wrh_ref[...], wrl_ref[...]], axis=1), preferred_element_type=F32)
    logits = (hi_both[:, :LANES] + hi_both[:, LANES:]
              + jnp.dot(x_lo, wrh_ref[...], preferred_element_type=F32)
              + br_ref[...])
    lane = lax.broadcasted_iota(jnp.int32, logits.shape, 1)
    big = jnp.int32(4 * LANES)

    def first_argmax(vals, vmax):
        return jnp.min(jnp.where(vals == vmax, lane, big), axis=-1, keepdims=True)

    gl = jnp.where(lane < N_GROUPS, logits, NEG_BIG)
    gmax = jnp.max(gl, axis=-1, keepdims=True)
    g_val = 1.0 / jnp.sum(jnp.exp(gl - gmax), axis=-1, keepdims=True)
    g_idx = first_argmax(gl, gmax)
    lo = EXPERT_LANE0 + EXPERTS_PER_GROUP * g_idx
    el = jnp.where((lane >= lo) & (lane < lo + EXPERTS_PER_GROUP), logits, NEG_BIG)
    m1 = jnp.max(el, axis=-1, keepdims=True)
    i1 = first_argmax(el, m1)
    el2 = jnp.where(lane == i1, NEG_BIG, el)
    m2 = jnp.max(el2, axis=-1, keepdims=True)
    i2 = first_argmax(el2, m2)
    r = jnp.exp(m2 - m1)
    gate1 = g_val / (1.0 + r)
    gate2 = g_val * r / (1.0 + r)

    prow = (i % tiles_per_seq) * tm + lax.broadcasted_iota(jnp.int32, logits.shape, 0)
    valid = prow < seq_len
    oh1 = jnp.where(valid & (lane == i1), 1.0, 0.0)
    oh2 = jnp.where(valid & (lane == i2), 1.0, 0.0)

    @pl.when(i == 0)
    def _():
        run_ref[...] = jnp.zeros_like(run_ref)

    run = run_ref[0:1, :]
    tri = tri_ref[...]
    pre = jnp.dot(tri, jnp.concatenate([oh1, oh2], axis=1).astype(BF16), preferred_element_type=F32)
    pre1 = pre[:, :LANES]
    pre2 = pre[:, LANES:]
    tot1 = jnp.sum(oh1, axis=0, keepdims=True)
    tot2 = jnp.sum(oh2, axis=0, keepdims=True)
    rank1 = jnp.sum(oh1 * (pre1 + run), axis=-1, keepdims=True)
    rank2 = jnp.sum(oh2 * (pre2 + run + tot1), axis=-1, keepdims=True)
    new_run = run + tot1 + tot2
    run_ref[...] = jnp.broadcast_to(new_run, run_ref.shape)
    cnt_ref[...] = jnp.broadcast_to(new_run, cnt_ref.shape)

    e1 = (i1 - EXPERT_LANE0).astype(F32)
    e2 = (i2 - EXPERT_LANE0).astype(F32)
    packed = jnp.where(lane == 0, e1, jnp.where(lane == 1, e2, jnp.where(lane == 2, gate1, jnp.where(
        lane == 3, gate2, jnp.where(lane == 4, rank1, jnp.where(lane == 5, rank2, 0.0))))))
    route_ref[...] = packed[:, 0:ROUTE_COLS]


def _outproj(hp, convy, o, w_out, g2, wr_hi, wr_lo, br, tri, *, tiles_per_seq, seq_len, tm):
    n, d = hp.shape
    const = lambda i: (0, 0)
    tile = lambda i: (i, 0)
    kern = functools.partial(_outproj_kernel, tiles_per_seq=tiles_per_seq, seq_len=seq_len)
    return pl.pallas_call(
        kern,
        grid=(n // tm,),
        in_specs=[
            pl.BlockSpec((tm, d), tile),
            pl.BlockSpec((tm, convy.shape[1]), tile),
            pl.BlockSpec((tm, o.shape[1]), tile),
            pl.BlockSpec(w_out.shape, const),
            pl.BlockSpec((1, d), const),
            pl.BlockSpec(wr_hi.shape, const),
            pl.BlockSpec(wr_lo.shape, const),
            pl.BlockSpec((1, LANES), const),
            pl.BlockSpec(tri.shape, const),
        ],
        out_specs=[
            pl.BlockSpec((tm, d), tile),
            *[pl.BlockSpec((tm, LANES), tile)] * ROW_PLANES,
            pl.BlockSpec((tm, ROUTE_COLS), tile),
            pl.BlockSpec((8, LANES), const),
        ],
        out_shape=[
            jax.ShapeDtypeStruct((n, d), F32),
            *[jax.ShapeDtypeStruct((n, LANES), jnp.uint32)] * ROW_PLANES,
            jax.ShapeDtypeStruct((n, ROUTE_COLS), F32),
            jax.ShapeDtypeStruct((8, LANES), F32),
        ],
        scratch_shapes=[pltpu.VMEM((8, LANES), F32)],
        compiler_params=_cparams(1),
        name="outproj_router",
    )(hp, convy, o, w_out, g2, wr_hi, wr_lo, br, tri)


def _row_copy(src_hbm, src_row, dst_ref, dst_row, sem):
    return pltpu.make_async_copy(src_hbm.at[pl.ds(src_row, 1), :], dst_ref.at[pl.ds(dst_row, 1), :], sem)


def _dispatch_kernel(dest_ref, xn_hbm, xs_hbm, xbuf, sem_in, sem, *, tiles_per_seq, lp_len, tile):
    t = pl.program_id(0)
    base = (t // tiles_per_seq) * lp_len + (t % tiles_per_seq) * tile
    tile_copy = pltpu.make_async_copy(xn_hbm.at[pl.ds(base, tile), :], xbuf, sem_in)
    tile_copy.start()
    tile_copy.wait()

    def issue(r, carry):
        for kk in range(TOP_K):
            _row_copy(xbuf, r, xs_hbm, dest_ref[0, 0, TOP_K * r + kk], sem).start()
        return carry

    lax.fori_loop(0, tile, issue, 0)

    def drain(r, carry):
        for kk in range(TOP_K):
            _row_copy(xbuf, r, xs_hbm, dest_ref[0, 0, TOP_K * r + kk], sem).wait()
        return carry

    lax.fori_loop(0, tile, drain, 0)


def _dispatch(dest_tiles, xn, *, p_rows, tiles_per_seq, lp_len, tile):
    n_tiles = dest_tiles.shape[0]
    kern = functools.partial(_dispatch_kernel, tiles_per_seq=tiles_per_seq, lp_len=lp_len, tile=tile)
    return pl.pallas_call(
        kern,
        grid=(n_tiles,),
        in_specs=[
            pl.BlockSpec((1, 1, TOP_K * tile), lambda t: (t, 0, 0), memory_space=pltpu.SMEM),
            pl.BlockSpec(memory_space=pl.ANY),
        ],
        out_specs=pl.BlockSpec(memory_space=pl.ANY),
        out_shape=jax.ShapeDtypeStruct((p_rows, xn.shape[1]), xn.dtype),
        scratch_shapes=[pltpu.VMEM((tile, xn.shape[1]), xn.dtype), pltpu.SemaphoreType.DMA(()),
                        pltpu.SemaphoreType.DMA(())],
        compiler_params=_cparams(1),
        name="moe_dispatch",
    )(dest_tiles, xn)


def _experts_kernel(be_ref, nv_ref, xs_ref, wg_ref, wu_ref, wd_ref, y_ref, wgb_ref, wub_ref, wdb_ref):
    i = pl.program_id(0)
    e = be_ref[i]
    prev = be_ref[jnp.maximum(i - 1, 0)]

    @pl.when((i == 0) | (e != prev))
    def _():
        wgb_ref[...] = wg_ref[0].astype(BF16)
        wub_ref[...] = wu_ref[0].astype(BF16)
        wdb_ref[...] = wd_ref[0].astype(BF16)

    nv = nv_ref[i]

    @pl.when(nv > 0)
    def _():
        xs = xs_ref[...]
        row = lax.broadcasted_iota(jnp.int32, xs.shape, 0)
        x = jnp.where(row < nv, xs, 0.0).astype(BF16)
        hg = jnp.dot(x, wgb_ref[...], preferred_element_type=F32)
        hu = jnp.dot(x, wub_ref[...], preferred_element_type=F32)
        hid = (hg / (1.0 + jnp.exp(-hg)) * hu).astype(BF16)
        y_ref[...] = jnp.dot(hid, wdb_ref[...], preferred_element_type=F32)

    @pl.when(nv == 0)
    def _():
        y_ref[...] = jnp.zeros_like(y_ref)


def _experts(block_e, nvalid, xs, w_gate, w_up, w_down):
    p_rows, d = xs.shape
    ff = w_gate.shape[2]
    n_blocks = p_rows // MOE_BLOCK
    grid_spec = pltpu.PrefetchScalarGridSpec(
        num_scalar_prefetch=2,
        grid=(n_blocks,),
        in_specs=[
            pl.BlockSpec((MOE_BLOCK, d), lambda i, be, nv: (i, 0)),
            pl.BlockSpec((1, d, ff), lambda i, be, nv: (be[i], 0, 0)),
            pl.BlockSpec((1, d, ff), lambda i, be, nv: (be[i], 0, 0)),
            pl.BlockSpec((1, ff, d), lambda i, be, nv: (be[i], 0, 0)),
        ],
        out_specs=pl.BlockSpec((MOE_BLOCK, d), lambda i, be, nv: (i, 0)),
        scratch_shapes=[
            pltpu.VMEM((d, ff), BF16),
            pltpu.VMEM((d, ff), BF16),
            pltpu.VMEM((ff, d), BF16),
        ],
    )
    return pl.pallas_call(
        _experts_kernel,
        grid_spec=grid_spec,
        out_shape=jax.ShapeDtypeStruct((p_rows, d), F32),
        compiler_params=_cparams(1),
        name="moe_experts",
    )(block_e, nvalid, xs, w_gate, w_up, w_down)


def _combine_kernel(dest_ref, gates_ref, h1_hbm, y_hbm, out_ref, hbuf, ya, yb, sem_h, sem_a, sem_b,
                    *, lp_len, tile):
    b = pl.program_id(0)
    i = pl.program_id(1)
    start = b * lp_len + N_META + i * tile
    h_copy = pltpu.make_async_copy(h1_hbm.at[pl.ds(start, tile), :], hbuf, sem_h)
    h_copy.start()

    def issue(r, carry):
        _row_copy(y_hbm, dest_ref[0, 0, TOP_K * r], ya, r, sem_a).start()
        _row_copy(y_hbm, dest_ref[0, 0, TOP_K * r + 1], yb, r, sem_b).start()
        return carry

    lax.fori_loop(0, tile, issue, 0)

    def drain(r, carry):
        _row_copy(y_hbm, dest_ref[0, 0, TOP_K * r], ya, r, sem_a).wait()
        _row_copy(y_hbm, dest_ref[0, 0, TOP_K * r + 1], yb, r, sem_b).wait()
        return carry

    lax.fori_loop(0, tile, drain, 0)
    h_copy.wait()
    g = gates_ref[0]
    out_ref[0] = hbuf[...] + g[:, 0:1] * ya[...] + g[:, 1:2] * yb[...]


def _combine(dest_tiles, gates, h1, y, *, batch, seq, lp_len, tile):
    d = h1.shape[1]
    nt = seq // tile
    kern = functools.partial(_combine_kernel, lp_len=lp_len, tile=tile)
    return pl.pallas_call(
        kern,
        grid=(batch, nt),
        in_specs=[
            pl.BlockSpec((1, 1, TOP_K * tile), lambda b, i: (b * nt + i, 0, 0), memory_space=pltpu.SMEM),
            pl.BlockSpec((1, tile, TOP_K), lambda b, i: (b, i, 0)),
            pl.BlockSpec(memory_space=pl.ANY),
            pl.BlockSpec(memory_space=pl.ANY),
        ],
        out_specs=pl.BlockSpec((1, tile, d), lambda b, i: (b, i, 0)),
        out_shape=jax.ShapeDtypeStruct((batch, seq, d), F32),
        scratch_shapes=[
            pltpu.VMEM((tile, d), F32),
            pltpu.VMEM((tile, d), F32),
            pltpu.VMEM((tile, d), F32),
            pltpu.SemaphoreType.DMA(()),
            pltpu.SemaphoreType.DMA(()),
            pltpu.SemaphoreType.DMA(()),
        ],
        compiler_params=_cparams(2),
        name="moe_combine",
    )(dest_tiles, gates, h1, y)


def _sc_workers():
    info = plsc.get_sparse_core_info()
    return info.num_cores, info.num_cores * info.num_subcores


def _sc_scatter_rows(planes, idx_a, idx_b, out_rows):
    n_win = idx_a.shape[0]
    n_cores, n_workers = _sc_workers()
    trips = -(-n_win // n_workers)
    mesh = plsc.VectorSubcoreMesh(core_axis_name="c", subcore_axis_name="s")

    def body(*refs):
        xs = refs[0:ROW_PLANES]
        ia_hbm, ib_hbm = refs[ROW_PLANES:ROW_PLANES + 2]
        outs = refs[ROW_PLANES + 2:2 * ROW_PLANES + 2]
        ia_v, ib_v, buf, sem = refs[2 * ROW_PLANES + 2:]
        wid = lax.axis_index("s") * n_cores + lax.axis_index("c")

        def step(t, carry):
            g = wid + t * n_workers

            @pl.when(g < n_win)
            def _():
                pltpu.sync_copy(ia_hbm.at[g], ia_v)
                pltpu.sync_copy(ib_hbm.at[g], ib_v)
                row0 = pl.multiple_of(g * SC_WINDOW, SC_WINDOW)
                loads = [pltpu.async_copy(xs[c].at[pl.ds(row0, SC_WINDOW)], buf.at[c], sem)
                         for c in range(ROW_PLANES)]
                for cp in loads:
                    cp.wait()
                stores = [pltpu.async_copy(buf.at[c], outs[c].at[iv], sem)
                          for c in range(ROW_PLANES) for iv in (ia_v, ib_v)]
                for cp in stores:
                    cp.wait()

            return carry

        lax.fori_loop(0, trips, step, 0)

    kern = pl.kernel(
        body,
        out_type=[jax.ShapeDtypeStruct((out_rows, LANES), jnp.uint32)] * ROW_PLANES,
        mesh=mesh,
        scratch_types=[
            pltpu.VMEM((SC_WINDOW,), jnp.int32),
            pltpu.VMEM((SC_WINDOW,), jnp.int32),
            pltpu.VMEM((ROW_PLANES, SC_WINDOW, LANES), jnp.uint32),
            pltpu.SemaphoreType.DMA,
        ],
        name="moe_dispatch_sc",
    )
    return kern(*planes, idx_a, idx_b)


def _sc_gather_rows(planes, idx_a, idx_b):
    n_win = idx_a.shape[0]
    n_cores, n_workers = _sc_workers()
    trips = -(-n_win // n_workers)
    mesh = plsc.VectorSubcoreMesh(core_axis_name="c", subcore_axis_name="s")

    def body(*refs):
        ys = refs[0:ROW_PLANES]
        ia_hbm, ib_hbm = refs[ROW_PLANES:ROW_PLANES + 2]
        outs_a = refs[ROW_PLANES + 2:2 * ROW_PLANES + 2]
        outs_b = refs[2 * ROW_PLANES + 2:3 * ROW_PLANES + 2]
        iv, buf, sem = refs[3 * ROW_PLANES + 2:]
        wid = lax.axis_index("s") * n_cores + lax.axis_index("c")

        def step(t, carry):
            g = wid + t * n_workers

            @pl.when(g < n_win)
            def _():
                row0 = pl.multiple_of(g * SC_WINDOW, SC_WINDOW)
                for i_hbm, outs in ((ia_hbm, outs_a), (ib_hbm, outs_b)):
                    pltpu.sync_copy(i_hbm.at[g], iv)
                    loads = [pltpu.async_copy(ys[c].at[iv], buf.at[c], sem) for c in range(ROW_PLANES)]
                    for cp in loads:
                        cp.wait()
                    stores = [pltpu.async_copy(buf.at[c], outs[c].at[pl.ds(row0, SC_WINDOW)], sem)
                              for c in range(ROW_PLANES)]
                    for cp in stores:
                        cp.wait()

            return carry

        lax.fori_loop(0, trips, step, 0)

    n_rows = n_win * SC_WINDOW
    kern = pl.kernel(
        body,
        out_type=[jax.ShapeDtypeStruct((n_rows, LANES), jnp.uint32)] * (2 * ROW_PLANES),
        mesh=mesh,
        scratch_types=[
            pltpu.VMEM((SC_WINDOW,), jnp.int32),
            pltpu.VMEM((ROW_PLANES, SC_WINDOW, LANES), jnp.uint32),
            pltpu.SemaphoreType.DMA,
        ],
        name="moe_gather_sc",
    )
    res = kern(*planes, idx_a, idx_b)
    return res[:ROW_PLANES], res[ROW_PLANES:]


def _experts_kernel(be_ref, nv_ref, x0_ref, x1_ref, x2_ref, x3_ref, wg_ref, wu_ref, wd_ref,
                    y0_ref, y1_ref, y2_ref, y3_ref, wgb_ref, wub_ref, wdb_ref):
    i = pl.program_id(0)
    e = be_ref[i]
    prev = be_ref[jnp.maximum(i - 1, 0)]
    y_refs = (y0_ref, y1_ref, y2_ref, y3_ref)

    @pl.when((i == 0) | (e != prev))
    def _():
        wgb_ref[...] = wg_ref[0].astype(BF16)
        wub_ref[...] = wu_ref[0].astype(BF16)
        wdb_ref[...] = wd_ref[0].astype(BF16)

    nv = nv_ref[i]

    @pl.when(nv > 0)
    def _():
        xs = _unpack_rows([x0_ref[...], x1_ref[...], x2_ref[...], x3_ref[...]])
        row = lax.broadcasted_iota(jnp.int32, xs.shape, 0)
        x = jnp.where(row < nv, xs, jnp.zeros_like(xs))
        hg = jnp.dot(x, wgb_ref[...], preferred_element_type=F32)
        hu = jnp.dot(x, wub_ref[...], preferred_element_type=F32)
        hid = (hg / (1.0 + jnp.exp(-hg)) * hu).astype(BF16)
        yw = _pack_rows(jnp.dot(hid, wdb_ref[...], preferred_element_type=F32))
        for c, ref in enumerate(y_refs):
            ref[...] = yw[:, c * LANES:(c + 1) * LANES]

    @pl.when(nv == 0)
    def _():
        for ref in y_refs:
            ref[...] = jnp.zeros_like(ref)


def _experts(block_e, nvalid, xs_planes, w_gate, w_up, w_down):
    n_blocks = block_e.shape[0]
    _, d, ff = w_gate.shape
    blk = lambda i, be, nv: (i, 0)
    wsel = lambda i, be, nv: (be[i], 0, 0)
    grid_spec = pltpu.PrefetchScalarGridSpec(
        num_scalar_prefetch=2,
        grid=(n_blocks,),
        in_specs=[
            *[pl.BlockSpec((MOE_BLOCK, LANES), blk)] * ROW_PLANES,
            pl.BlockSpec((1, d, ff), wsel),
            pl.BlockSpec((1, d, ff), wsel),
            pl.BlockSpec((1, ff, d), wsel),
        ],
        out_specs=[pl.BlockSpec((MOE_BLOCK, LANES), blk)] * ROW_PLANES,
        scratch_shapes=[
            pltpu.VMEM((d, ff), BF16),
            pltpu.VMEM((d, ff), BF16),
            pltpu.VMEM((ff, d), BF16),
        ],
    )
    return pl.pallas_call(
        _experts_kernel,
        grid_spec=grid_spec,
        out_shape=[jax.ShapeDtypeStruct((n_blocks * MOE_BLOCK, LANES), jnp.uint32)] * ROW_PLANES,
        compiler_params=_cparams(1),
        name="moe_experts",
    )(block_e, nvalid, *xs_planes, w_gate, w_up, w_down)


def _combine_kernel(gates_ref, h1_hbm, *refs, lp_len, tile):
    a_refs = refs[0:ROW_PLANES]
    b_refs = refs[ROW_PLANES:2 * ROW_PLANES]
    out_ref, hbuf, sem_h = refs[2 * ROW_PLANES:]
    b = pl.program_id(0)
    i = pl.program_id(1)
    start = b * lp_len + N_META + i * tile
    h_copy = pltpu.make_async_copy(h1_hbm.at[pl.ds(start, tile), :], hbuf, sem_h)
    h_copy.start()
    ya = _unpack_rows([r[...] for r in a_refs]).astype(F32)
    yb = _unpack_rows([r[...] for r in b_refs]).astype(F32)
    g = gates_ref[0]
    moe = g[:, 0:1] * ya + g[:, 1:2] * yb
    h_copy.wait()
    out_ref[0] = hbuf[...] + moe


def _combine(gates, h1, a_planes, b_planes, *, batch, seq, lp_len, tile):
    d = h1.shape[1]
    nt = seq // tile
    kern = functools.partial(_combine_kernel, lp_len=lp_len, tile=tile)
    rows = lambda b, i: (b * nt + i, 0)
    return pl.pallas_call(
        kern,
        grid=(batch, nt),
        in_specs=[
            pl.BlockSpec((1, tile, TOP_K), lambda b, i: (b, i, 0)),
            pl.BlockSpec(memory_space=pl.ANY),
            *[pl.BlockSpec((tile, LANES), rows)] * (2 * ROW_PLANES),
        ],
        out_specs=pl.BlockSpec((1, tile, d), lambda b, i: (b, i, 0)),
        out_shape=jax.ShapeDtypeStruct((batch, seq, d), F32),
        scratch_shapes=[pltpu.VMEM((tile, d), F32), pltpu.SemaphoreType.DMA(())],
        compiler_params=_cparams(2),
        name="moe_combine",
    )(gates, h1, *a_planes, *b_planes)


def _rope_tables(length, lp_len):
    half = ROPE_DIM // 2
    pos = jnp.arange(length, dtype=F32)
    inv_freq = ROPE_THETA ** (-jnp.arange(0, ROPE_DIM, 2, dtype=F32) / ROPE_DIM)
    ang = pos[:, None] * inv_freq[None, :]
    cos = jnp.cos(ang)
    sin = jnp.sin(ang)
    ones = jnp.ones((length, HEAD_DIM - ROPE_DIM), F32)
    zeros_h = jnp.zeros((length, half), F32)
    zeros_r = jnp.zeros((length, HEAD_DIM - ROPE_DIM), F32)
    c = jnp.concatenate([cos, cos, ones], axis=1)
    s1 = jnp.concatenate([zeros_h, sin, zeros_r], axis=1)
    s2 = jnp.concatenate([-sin, zeros_h, zeros_r], axis=1)
    pad = ((0, lp_len - length), (0, 0))
    rep = LANES // HEAD_DIM
    return tuple(jnp.pad(jnp.tile(t, (1, rep)), pad) for t in (c, s1, s2))


def _layer(hp, l, batch, length, lp_len, tm, norm1_g, w_in, conv_w, q_norm_g, k_norm_g, lambda_q1, lambda_k1,
           lambda_q2, lambda_k2, subln_g, w_out, norm2_g, w_router_group, b_router_group, w_router_expert,
           b_router_expert, w_gate, w_up, w_down, rope, last):
    n, d = hp.shape
    tiles_per_seq = lp_len // tm
    cw = conv_w.shape[2]
    qw = N_HEADS * 2 * HEAD_DIM
    lam_init = 0.8 - 0.6 * math.exp(-0.3 * l)

    reps = qw // HEAD_DIM
    gq = jnp.tile(q_norm_g[l] * (HEAD_DIM ** -0.5 * LOG2E), reps)[None, :]
    gk = jnp.tile(k_norm_g[l], reps)[None, :]
    seg = jnp.arange(qw) // HEAD_DIM
    bd = (seg[:, None] == seg[None, :]).astype(BF16)
    convy, q, k, v = _inproj(hp, norm1_g[l][None, :], w_in[l].astype(BF16), conv_w[l], gq, gk, bd, *rope,
                             tiles_per_seq=tiles_per_seq, tm=tm)

    lamp = jnp.stack([lambda_q1[l], lambda_k1[l], lambda_q2[l], lambda_k2[l]]).astype(F32)
    o = _attention(q, k, v, lamp, subln_g[l][None, :], batch=batch, lp_len=lp_len, tq=tm, lam_init=lam_init)

    lane_pad = LANES - N_GROUPS - N_EXPERTS
    wr = jnp.pad(jnp.concatenate([w_router_group[l], w_router_expert[l]], axis=1), ((0, 0), (0, lane_pad)))
    wr_hi = wr.astype(BF16)
    wr_lo = (wr - wr_hi.astype(F32)).astype(BF16)
    br = jnp.pad(jnp.concatenate([b_router_group[l], b_router_expert[l]]), (0, lane_pad))[None, :]
    ridx = jnp.arange(tm)
    tri = (ridx[None, :] < ridx[:, None]).astype(BF16)
    h1, *rest = _outproj(hp, convy, o, w_out[l].astype(BF16), norm2_g[l][None, :], wr_hi, wr_lo, br,
                         tri, tiles_per_seq=tiles_per_seq, seq_len=length, tm=tm)
    x_planes = rest[:ROW_PLANES]
    route, cnt = rest[ROW_PLANES:]

    route = route.reshape(batch, lp_len, ROUTE_COLS)
    eid = route[..., 0:TOP_K].astype(jnp.int32)
    gates = route[..., TOP_K:2 * TOP_K]
    rank = route[..., 2 * TOP_K:3 * TOP_K].astype(jnp.int32)
    counts = cnt[0, EXPERT_LANE0:EXPERT_LANE0 + N_EXPERTS].astype(jnp.int32)
    a = batch * length * TOP_K
    n_blocks = -(-a // MOE_BLOCK) + N_EXPERTS
    p_rows = n_blocks * MOE_BLOCK
    padded = (counts + MOE_BLOCK - 1) // MOE_BLOCK * MOE_BLOCK
    pends = jnp.cumsum(padded)
    pstarts = pends - padded
    experts = jnp.arange(N_EXPERTS, dtype=jnp.int32)

    def lookup(table, idx):
        return jnp.sum(jnp.where(idx[..., None] == experts, table, 0), axis=-1)

    n_pad = lp_len - length
    pos = jnp.arange(lp_len, dtype=jnp.int32)[None, :, None]
    spare = p_rows + ((jnp.arange(batch, dtype=jnp.int32)[:, None, None] * n_pad + (pos - length)) * TOP_K
                      + jnp.arange(TOP_K, dtype=jnp.int32)[None, None, :])
    dest = jnp.where(pos < length, lookup(pstarts, eid) + rank, spare)
    spare_rows = -(-(batch * n_pad * TOP_K) // MOE_BLOCK) * MOE_BLOCK
    blk0 = jnp.arange(n_blocks, dtype=jnp.int32) * MOE_BLOCK
    block_e = jnp.minimum(jnp.sum((pends[None, :] <= blk0[:, None]).astype(jnp.int32), axis=1), N_EXPERTS - 1)
    nvalid = jnp.clip(lookup(counts, block_e) - (blk0 - lookup(pstarts, block_e)), 0, MOE_BLOCK)

    assert (batch * lp_len) % SC_WINDOW == 0
    xs_planes = _sc_scatter_rows(x_planes, dest[..., 0].reshape(-1, SC_WINDOW),
                                 dest[..., 1].reshape(-1, SC_WINDOW), p_rows + spare_rows)
    y_planes = _experts(block_e, nvalid, xs_planes, w_gate[l], w_up[l], w_down[l])

    if not last:
        raise NotImplementedError("only the final layer's combine (which drops the meta tokens) is implemented")
    seq = length - N_META
    assert (batch * seq) % SC_WINDOW == 0
    dest_x = dest[:, N_META:length]
    a_planes, b_planes = _sc_gather_rows(y_planes, dest_x[..., 0].reshape(-1, SC_WINDOW),
                                         dest_x[..., 1].reshape(-1, SC_WINDOW))
    ctile = _largest_tile(seq, 512, 8)
    return _combine(gates[:, N_META:length], h1, a_planes, b_planes, batch=batch, seq=seq, lp_len=lp_len,
                    tile=ctile)


def kernel(x, meta_tokens, norm1_g, w_in, conv_w, q_norm_g, k_norm_g, lambda_q1, lambda_k1, lambda_q2, lambda_k2,
           subln_g, w_out, norm2_g, w_router_group, b_router_group, w_router_expert, b_router_expert, w_gate,
           w_up, w_down):
    b, s, d = x.shape
    depth = w_in.shape[0]
    assert depth == 1, "a single layer is supported"
    length = s + N_META
    tm = TOKEN_TILE if length >= TOKEN_TILE else LANES
    lp_len = -(-length // tm) * tm
    hp = jnp.concatenate([jnp.broadcast_to(meta_tokens[None].astype(x.dtype), (b, N_META, d)), x,
                          jnp.zeros((b, lp_len - length, d), x.dtype)], axis=1).reshape(b * lp_len, d)
    rope = _rope_tables(length, lp_len)
    return _layer(hp, 0, b, length, lp_len, tm, norm1_g, w_in, conv_w, q_norm_g, k_norm_g, lambda_q1, lambda_k1,
                  lambda_q2, lambda_k2, subln_g, w_out, norm2_g, w_router_group, b_router_group,
                  w_router_expert, b_router_expert, w_gate, w_up, w_down, rope, last=True)
```

```python
import functools
import math

import jax
import jax.numpy as jnp
from jax import lax
from jax.experimental import pallas as pl
from jax.experimental.pallas import tpu as pltpu
from jax.experimental.pallas import tpu_sc as plsc

F32 = jnp.float32
BF16 = jnp.bfloat16

N_META = 16
CONV_K = 3
N_HEADS = 4
HEAD_DIM = 64
V_DIM = 2 * HEAD_DIM
ROPE_DIM = HEAD_DIM // 4
ROPE_THETA = 500000.0
N_GROUPS = 4
EXPERTS_PER_GROUP = 8
N_EXPERTS = N_GROUPS * EXPERTS_PER_GROUP
TOP_K = 2
EPS = 1e-6
LOG2E = 1.4426950408889634

LANES = 128
TOKEN_TILE = 640
OUTPROJ_CHAINS = 2
ATTN_WIDE_CHUNKS = 2
MOE_BLOCK = 512
ROUTE_COLS = 8
ROW_PLANES = 4
SC_WINDOW = 128
EXPERT_LANE0 = N_GROUPS
NEG_BIG = -1e30
VMEM_LIMIT = 56 * 1024 * 1024


def _largest_tile(n, cap, mult):
    for t in range(min(cap, n), 0, -1):
        if n % t == 0 and t % mult == 0:
            return t
    raise ValueError(f"no tile for {n}")


def _cparams(n_axes, flags=None):
    return pltpu.CompilerParams(dimension_semantics=("arbitrary",) * n_axes,
                                vmem_limit_bytes=VMEM_LIMIT, flags=flags)


def _inproj_kernel(x_ref, g1_ref, win_ref, convw_ref, gq_ref, gk_ref, bd_ref, rc_ref, rs1_ref, rs2_ref,
                   convy_ref, q_ref, k_ref, v_ref, carry_ref, *, tiles_per_seq, cw, qw):
    i = pl.program_id(0)
    tm = x_ref.shape[0]
    x = x_ref[...]
    ms = jnp.mean(x * x, axis=-1, keepdims=True)
    xn = (x * lax.rsqrt(ms + EPS) * g1_ref[...]).astype(BF16)
    q0 = 3 * cw

    def proj(lo, hi):
        return jnp.dot(xn, win_ref[:, lo:hi], preferred_element_type=F32)

    u_conv = proj(0, q0)
    u_q = proj(q0, q0 + qw)

    z = u_conv[:, cw:2 * cw] * u_conv[:, 2 * cw:3 * cw]

    @pl.when(i % tiles_per_seq == 0)
    def _():
        carry_ref[...] = jnp.zeros_like(carry_ref)

    prev = carry_ref[...]
    p1 = prev[7:8]
    p2 = prev[6:7]
    row = lax.broadcasted_iota(jnp.int32, z.shape, 0)
    z1 = jnp.where(row == 0, p1, pltpu.roll(z, 1, axis=0))
    z2 = jnp.where(row == 0, p2, jnp.where(row == 1, p1, pltpu.roll(z, 2, axis=0)))
    carry_ref[...] = z[tm - 8:tm]
    w = convw_ref[...]
    conv = w[0:1] * z2 + w[1:2] * z1 + w[2:3] * z
    convy_ref[...] = (u_conv[:, 0:cw] * conv).astype(BF16)

    rc = rc_ref[...]
    rs1 = rs1_ref[...]
    rs2 = rs2_ref[...]

    def norm_rope(t, g_ref):
        ss = jnp.dot((t * t).astype(BF16), bd_ref[...], preferred_element_type=F32)
        tn = t * lax.rsqrt(ss * (1.0 / HEAD_DIM) + EPS) * g_ref[...]
        outs = []
        for c in range(qw // LANES):
            ch = tn[:, c * LANES:(c + 1) * LANES]
            outs.append(ch * rc + pltpu.roll(ch, ROPE_DIM // 2, axis=1) * rs1
                        + pltpu.roll(ch, LANES - ROPE_DIM // 2, axis=1) * rs2)
        return jnp.concatenate(outs, axis=1).astype(BF16)

    u_k = proj(q0 + qw, q0 + 2 * qw)
    q_ref[...] = norm_rope(u_q, gq_ref)
    u_v = proj(q0 + 2 * qw, win_ref.shape[1])
    k_ref[...] = norm_rope(u_k, gk_ref)
    v_ref[...] = u_v.astype(BF16)


def _inproj(hp, g1, w_in, conv_w, gq, gk, bd, rc, rs1, rs2, *, tiles_per_seq, tm):
    n, d = hp.shape
    cw = conv_w.shape[1]
    qw = gq.shape[1]
    aw = w_in.shape[1] - 3 * cw - 2 * qw
    const = lambda i: (0, 0)
    tile = lambda i: (i, 0)
    pos = lambda i: (i % tiles_per_seq, 0)
    kern = functools.partial(_inproj_kernel, tiles_per_seq=tiles_per_seq, cw=cw, qw=qw)
    return pl.pallas_call(
        kern,
        grid=(n // tm,),
        in_specs=[
            pl.BlockSpec((tm, d), tile),
            pl.BlockSpec((1, d), const),
            pl.BlockSpec(w_in.shape, const),
            pl.BlockSpec(conv_w.shape, const),
            pl.BlockSpec((1, qw), const),
            pl.BlockSpec((1, qw), const),
            pl.BlockSpec(bd.shape, const),
            pl.BlockSpec((tm, LANES), pos),
            pl.BlockSpec((tm, LANES), pos),
            pl.BlockSpec((tm, LANES), pos),
        ],
        out_specs=[
            pl.BlockSpec((tm, cw), tile),
            pl.BlockSpec((tm, qw), tile),
            pl.BlockSpec((tm, qw), tile),
            pl.BlockSpec((tm, aw), tile),
        ],
        out_shape=[
            jax.ShapeDtypeStruct((n, cw), BF16),
            jax.ShapeDtypeStruct((n, qw), BF16),
            jax.ShapeDtypeStruct((n, qw), BF16),
            jax.ShapeDtypeStruct((n, aw), BF16),
        ],
        scratch_shapes=[pltpu.VMEM((8, cw), F32)],
        compiler_params=_cparams(1),
        name="inproj",
    )(hp, g1, w_in, conv_w, gq, gk, bd, rc, rs1, rs2)


def _attn_kernel(q_ref, k_ref, v_ref, lamp_ref, sg_ref, o_ref, qs_ref, m_ref, l_ref, acc_ref, *, lam_init):
    qi = pl.program_id(2)
    tq = q_ref.shape[0]
    q = q_ref[...]
    lane = lax.broadcasted_iota(jnp.int32, q.shape, 1)
    zero = jnp.zeros_like(q)
    qs_ref[0:tq, :] = jnp.where(lane < HEAD_DIM, q, zero)
    qs_ref[tq:2 * tq, :] = jnp.where(lane >= HEAD_DIM, q, zero)
    m_ref[...] = jnp.full_like(m_ref, NEG_BIG)
    l_ref[...] = jnp.zeros_like(l_ref)
    acc_ref[...] = jnp.zeros_like(acc_ref)

    def scores(off, width, which):
        kc = k_ref[pl.ds(off, width), :]
        return lax.dot_general(qs_ref[pl.ds(which * tq, tq), :], kc, (((1,), (1,)), ((), ())),
                               preferred_element_type=F32)

    def update(off, width, which, s, masked):
        vc = jnp.concatenate([v_ref[pl.ds(off, width), :], jnp.ones((width, LANES), BF16)], axis=1)
        rows = pl.ds(which * tq, tq)
        if masked:
            r = lax.broadcasted_iota(jnp.int32, s.shape, 0)
            c = lax.broadcasted_iota(jnp.int32, s.shape, 1)
            s = jnp.where(c <= r, s, NEG_BIG)
        m_prev = m_ref[rows, :]
        m_new = jnp.maximum(m_prev, jnp.max(s, axis=-1, keepdims=True))
        alpha = jnp.exp2(m_prev - m_new)
        p = jnp.exp2((s - jnp.tile(m_new, (1, width // LANES))).astype(BF16))
        pv = jnp.dot(p, vc, preferred_element_type=F32)
        l_ref[rows, :] = alpha * l_ref[rows, :] + pv[:, LANES:]
        acc_ref[rows, :] = alpha * acc_ref[rows, :] + pv[:, :LANES]
        m_ref[rows, :] = m_new

    def chunk(off, width, masked):
        s1 = scores(off, width, 0)
        s2 = scores(off, width, 1)
        update(off, width, 0, s1, masked)
        update(off, width, 1, s2, masked)

    wide = ATTN_WIDE_CHUNKS * tq

    def body(j, carry):
        chunk(pl.multiple_of(j * wide, wide), wide, False)
        return carry

    n_wide = qi // ATTN_WIDE_CHUNKS
    lax.fori_loop(0, n_wide, body, 0)
    for extra in range(ATTN_WIDE_CHUNKS - 1):
        @pl.when(n_wide * ATTN_WIDE_CHUNKS + extra < qi)
        def _():
            chunk(pl.multiple_of((n_wide * ATTN_WIDE_CHUNKS + extra) * tq, tq), tq, False)

    chunk(pl.multiple_of(qi * tq, tq), tq, True)

    lp = lamp_ref[...]
    lam = (jnp.exp(jnp.sum(lp[0:1] * lp[1:2], axis=-1, keepdims=True))
           - jnp.exp(jnp.sum(lp[2:3] * lp[3:4], axis=-1, keepdims=True)) + lam_init)
    o_all = acc_ref[...] / l_ref[...]
    o = o_all[0:tq] - lam * o_all[tq:2 * tq]
    ms = jnp.mean(o * o, axis=-1, keepdims=True)
    o_ref[...] = (o * lax.rsqrt(ms + EPS) * sg_ref[...] * (1.0 - lam_init)).astype(BF16)


def _attention(q, k, v, lamp, sg, *, batch, lp_len, tq, lam_init):
    n, qw = q.shape
    nq = lp_len // tq
    heads = qw // LANES
    kern = functools.partial(_attn_kernel, lam_init=lam_init)
    return pl.pallas_call(
        kern,
        grid=(batch, heads, nq),
        in_specs=[
            pl.BlockSpec((tq, LANES), lambda b, h, i: (b * nq + i, h)),
            pl.BlockSpec((lp_len, LANES), lambda b, h, i: (b, h)),
            pl.BlockSpec((lp_len, LANES), lambda b, h, i: (b, h)),
            pl.BlockSpec(lamp.shape, lambda b, h, i: (0, 0)),
            pl.BlockSpec(sg.shape, lambda b, h, i: (0, 0)),
        ],
        out_specs=pl.BlockSpec((tq, LANES), lambda b, h, i: (b * nq + i, h)),
        out_shape=jax.ShapeDtypeStruct((n, v.shape[1]), BF16),
        scratch_shapes=[
            pltpu.VMEM((2 * tq, LANES), BF16),
            pltpu.VMEM((2 * tq, LANES), F32),
            pltpu.VMEM((2 * tq, LANES), F32),
            pltpu.VMEM((2 * tq, LANES), F32),
        ],
        compiler_params=_cparams(3),
        name="diffattn",
    )(q, k, v, lamp, sg)


def _pack_rows(x):
    w = x.shape[1] // 2
    lo = lax.bitcast_convert_type(x[:, :w].astype(BF16).astype(F32), jnp.uint32)
    hi = lax.bitcast_convert_type(x[:, w:].astype(BF16).astype(F32), jnp.uint32)
    return lax.shift_right_logical(lo, jnp.uint32(16)) | (hi & jnp.uint32(0xFFFF0000))


def _unpack_rows(planes):
    w = jnp.concatenate(planes, axis=1)
    lo = lax.bitcast_convert_type(lax.shift_left(w, jnp.uint32(16)), F32)
    hi = lax.bitcast_convert_type(w & jnp.uint32(0xFFFF0000), F32)
    return jnp.concatenate([lo, hi], axis=1).astype(BF16)


def _outproj_kernel(hp_ref, cy_ref, o_ref, wout_ref, g2_ref, wrh_ref, wrl_ref, br_ref, tri_ref,
                    h1_ref, xp0_ref, xp1_ref, xp2_ref, xp3_ref, route_ref, cnt_ref, run_ref,
                    *, tiles_per_seq, seq_len):
    i = pl.program_id(0)
    tm = hp_ref.shape[0]
    sub = tri_ref.shape[0]
    wr_both = jnp.concatenate([wrh_ref[...], wrl_ref[...]], axis=1)

    @pl.when(i == 0)
    def _():
        run_ref[...] = jnp.zeros_like(run_ref)

    def project(rows):
        mix = jnp.concatenate([cy_ref[rows, :], o_ref[rows, :]], axis=1)
        h1 = hp_ref[rows, :] + jnp.dot(mix, wout_ref[...], preferred_element_type=F32)
        h1_ref[rows, :] = h1
        ms = jnp.mean(h1 * h1, axis=-1, keepdims=True)
        xn = h1 * lax.rsqrt(ms + EPS) * g2_ref[...]
        xw = _pack_rows(xn)
        for c, ref in enumerate((xp0_ref, xp1_ref, xp2_ref, xp3_ref)):
            ref[rows, :] = xw[:, c * LANES:(c + 1) * LANES]
        x_hi = xn.astype(BF16)
        x_lo = (xn - x_hi.astype(F32)).astype(BF16)
        hi_both = jnp.dot(x_hi, wr_both, preferred_element_type=F32)
        return (hi_both[:, :LANES] + hi_both[:, LANES:]
                + jnp.dot(x_lo, wrh_ref[...], preferred_element_type=F32) + br_ref[...])

    chains = [pl.ds(c * sub, sub) for c in range(tm // sub)]
    all_logits = [project(rows) for rows in chains]
    run = run_ref[0:1, :]
    for c, (rows, logits) in enumerate(zip(chains, all_logits)):
        run = _route_rows(i, c, rows, logits, run, tri_ref, route_ref, tiles_per_seq=tiles_per_seq,
                          seq_len=seq_len, tm=tm)
    run_ref[...] = jnp.broadcast_to(run, run_ref.shape)
    cnt_ref[...] = jnp.broadcast_to(run, cnt_ref.shape)


def _route_rows(i, c, rows, logits, run, tri_ref, route_ref, *, tiles_per_seq, seq_len, tm):
    sub = logits.shape[0]
    lane = lax.broadcasted_iota(jnp.int32, logits.shape, 1)
    big = jnp.int32(4 * LANES)

    def first_argmax(vals, vmax):
        return jnp.min(jnp.where(vals == vmax, lane, big), axis=-1, keepdims=True)

    gl = jnp.where(lane < N_GROUPS, logits, NEG_BIG)
    gmax = jnp.max(gl, axis=-1, keepdims=True)
    g_val = 1.0 / jnp.sum(jnp.exp(gl - gmax), axis=-1, keepdims=True)
    g_idx = first_argmax(gl, gmax)
    lo = EXPERT_LANE0 + EXPERTS_PER_GROUP * g_idx
    el = jnp.where((lane >= lo) & (lane < lo + EXPERTS_PER_GROUP), logits, NEG_BIG)
    m1 = jnp.max(el, axis=-1, keepdims=True)
    i1 = first_argmax(el, m1)
    el2 = jnp.where(lane == i1, NEG_BIG, el)
    m2 = jnp.max(el2, axis=-1, keepdims=True)
    i2 = first_argmax(el2, m2)
    r = jnp.exp(m2 - m1)
    gate1 = g_val / (1.0 + r)
    gate2 = g_val * r / (1.0 + r)

    prow = (i % tiles_per_seq) * tm + c * sub + lax.broadcasted_iota(jnp.int32, logits.shape, 0)
    valid = prow < seq_len
    oh1 = jnp.where(valid & (lane == i1), 1.0, 0.0)
    oh2 = jnp.where(valid & (lane == i2), 1.0, 0.0)
    pre = jnp.dot(tri_ref[...], jnp.concatenate([oh1, oh2], axis=1).astype(BF16), preferred_element_type=F32)
    pre1 = pre[:, :LANES]
    pre2 = pre[:, LANES:]
    tot1 = jnp.sum(oh1, axis=0, keepdims=True)
    tot2 = jnp.sum(oh2, axis=0, keepdims=True)
    rank1 = jnp.sum(oh1 * (pre1 + run), axis=-1, keepdims=True)
    rank2 = jnp.sum(oh2 * (pre2 + run + tot1), axis=-1, keepdims=True)

    e1 = (i1 - EXPERT_LANE0).astype(F32)
    e2 = (i2 - EXPERT_LANE0).astype(F32)
    packed = jnp.where(lane == 0, e1, jnp.where(lane == 1, e2, jnp.where(lane == 2, gate1, jnp.where(
        lane == 3, gate2, jnp.where(lane == 4, rank1, jnp.where(lane == 5, rank2, 0.0))))))
    route_ref[rows, :] = packed[:, 0:ROUTE_COLS]
    return run + tot1 + tot2


def _outproj(hp, convy, o, w_out, g2, wr_hi, wr_lo, br, tri, *, tiles_per_seq, seq_len, tm):
    n, d = hp.shape
    const = lambda i: (0, 0)
    tile = lambda i: (i, 0)
    kern = functools.partial(_outproj_kernel, tiles_per_seq=tiles_per_seq, seq_len=seq_len)
    return pl.pallas_call(
        kern,
        grid=(n // tm,),
        in_specs=[
            pl.BlockSpec((tm, d), tile),
            pl.BlockSpec((tm, convy.shape[1]), tile),
            pl.BlockSpec((tm, o.shape[1]), tile),
            pl.BlockSpec(w_out.shape, const),
            pl.BlockSpec((1, d), const),
            pl.BlockSpec(wr_hi.shape, const),
            pl.BlockSpec(wr_lo.shape, const),
            pl.BlockSpec((1, LANES), const),
            pl.BlockSpec(tri.shape, const),
        ],
        out_specs=[
            pl.BlockSpec((tm, d), tile),
            *[pl.BlockSpec((tm, LANES), tile)] * ROW_PLANES,
            pl.BlockSpec((tm, ROUTE_COLS), tile),
            pl.BlockSpec((8, LANES), const),
        ],
        out_shape=[
            jax.ShapeDtypeStruct((n, d), F32),
            *[jax.ShapeDtypeStruct((n, LANES), jnp.uint32)] * ROW_PLANES,
            jax.ShapeDtypeStruct((n, ROUTE_COLS), F32),
            jax.ShapeDtypeStruct((8, LANES), F32),
        ],
        scratch_shapes=[pltpu.VMEM((8, LANES), F32)],
        compiler_params=_cparams(1),
        name="outproj_router",
    )(hp, convy, o, w_out, g2, wr_hi, wr_lo, br, tri)


def _row_copy(src_hbm, src_row, dst_ref, dst_row, sem):
    return pltpu.make_async_copy(src_hbm.at[pl.ds(src_row, 1), :], dst_ref.at[pl.ds(dst_row, 1), :], sem)


def _dispatch_kernel(dest_ref, xn_hbm, xs_hbm, xbuf, sem_in, sem, *, tiles_per_seq, lp_len, tile):
    t = pl.program_id(0)
    base = (t // tiles_per_seq) * lp_len + (t % tiles_per_seq) * tile
    tile_copy = pltpu.make_async_copy(xn_hbm.at[pl.ds(base, tile), :], xbuf, sem_in)
    tile_copy.start()
    tile_copy.wait()

    def issue(r, carry):
        for kk in range(TOP_K):
            _row_copy(xbuf, r, xs_hbm, dest_ref[0, 0, TOP_K * r + kk], sem).start()
        return carry

    lax.fori_loop(0, tile, issue, 0)

    def drain(r, carry):
        for kk in range(TOP_K):
            _row_copy(xbuf, r, xs_hbm, dest_ref[0, 0, TOP_K * r + kk], sem).wait()
        return carry

    lax.fori_loop(0, tile, drain, 0)


def _dispatch(dest_tiles, xn, *, p_rows, tiles_per_seq, lp_len, tile):
    n_tiles = dest_tiles.shape[0]
    kern = functools.partial(_dispatch_kernel, tiles_per_seq=tiles_per_seq, lp_len=lp_len, tile=tile)
    return pl.pallas_call(
        kern,
        grid=(n_tiles,),
        in_specs=[
            pl.BlockSpec((1, 1, TOP_K * tile), lambda t: (t, 0, 0), memory_space=pltpu.SMEM),
            pl.BlockSpec(memory_space=pl.ANY),
        ],
        out_specs=pl.BlockSpec(memory_space=pl.ANY),
        out_shape=jax.ShapeDtypeStruct((p_rows, xn.shape[1]), xn.dtype),
        scratch_shapes=[pltpu.VMEM((tile, xn.shape[1]), xn.dtype), pltpu.SemaphoreType.DMA(()),
                        pltpu.SemaphoreType.DMA(())],
        compiler_params=_cparams(1),
        name="moe_dispatch",
    )(dest_tiles, xn)


def _experts_kernel(be_ref, nv_ref, xs_ref, wg_ref, wu_ref, wd_ref, y_ref, wgb_ref, wub_ref, wdb_ref):
    i = pl.program_id(0)
    e = be_ref[i]
    prev = be_ref[jnp.maximum(i - 1, 0)]

    @pl.when((i == 0) | (e != prev))
    def _():
        wgb_ref[...] = wg_ref[0].astype(BF16)
        wub_ref[...] = wu_ref[0].astype(BF16)
        wdb_ref[...] = wd_ref[0].astype(BF16)

    nv = nv_ref[i]

    @pl.when(nv > 0)
    def _():
        xs = xs_ref[...]
        row = lax.broadcasted_iota(jnp.int32, xs.shape, 0)
        x = jnp.where(row < nv, xs, 0.0).astype(BF16)
        hg = jnp.dot(x, wgb_ref[...], preferred_element_type=F32)
        hu = jnp.dot(x, wub_ref[...], preferred_element_type=F32)
        hid = (hg / (1.0 + jnp.exp(-hg)) * hu).astype(BF16)
        y_ref[...] = jnp.dot(hid, wdb_ref[...], preferred_element_type=F32)

    @pl.when(nv == 0)
    def _():
        y_ref[...] = jnp.zeros_like(y_ref)


def _experts(block_e, nvalid, xs, w_gate, w_up, w_down):
    p_rows, d = xs.shape
    ff = w_gate.shape[2]
    n_blocks = p_rows // MOE_BLOCK
    grid_spec = pltpu.PrefetchScalarGridSpec(
        num_scalar_prefetch=2,
        grid=(n_blocks,),
        in_specs=[
            pl.BlockSpec((MOE_BLOCK, d), lambda i, be, nv: (i, 0)),
            pl.BlockSpec((1, d, ff), lambda i, be, nv: (be[i], 0, 0)),
            pl.BlockSpec((1, d, ff), lambda i, be, nv: (be[i], 0, 0)),
            pl.BlockSpec((1, ff, d), lambda i, be, nv: (be[i], 0, 0)),
        ],
        out_specs=pl.BlockSpec((MOE_BLOCK, d), lambda i, be, nv: (i, 0)),
        scratch_shapes=[
            pltpu.VMEM((d, ff), BF16),
            pltpu.VMEM((d, ff), BF16),
            pltpu.VMEM((ff, d), BF16),
        ],
    )
    return pl.pallas_call(
        _experts_kernel,
        grid_spec=grid_spec,
        out_shape=jax.ShapeDtypeStruct((p_rows, d), F32),
        compiler_params=_cparams(1),
        name="moe_experts",
    )(block_e, nvalid, xs, w_gate, w_up, w_down)


def _combine_kernel(dest_ref, gates_ref, h1_hbm, y_hbm, out_ref, hbuf, ya, yb, sem_h, sem_a, sem_b,
                    *, lp_len, tile):
    b = pl.program_id(0)
    i = pl.program_id(1)
    start = b * lp_len + N_META + i * tile
    h_copy = pltpu.make_async_copy(h1_hbm.at[pl.ds(start, tile), :], hbuf, sem_h)
    h_copy.start()

    def issue(r, carry):
        _row_copy(y_hbm, dest_ref[0, 0, TOP_K * r], ya, r, sem_a).start()
        _row_copy(y_hbm, dest_ref[0, 0, TOP_K * r + 1], yb, r, sem_b).start()
        return carry

    lax.fori_loop(0, tile, issue, 0)

    def drain(r, carry):
        _row_copy(y_hbm, dest_ref[0, 0, TOP_K * r], ya, r, sem_a).wait()
        _row_copy(y_hbm, dest_ref[0, 0, TOP_K * r + 1], yb, r, sem_b).wait()
        return carry

    lax.fori_loop(0, tile, drain, 0)
    h_copy.wait()
    g = gates_ref[0]
    out_ref[0] = hbuf[...] + g[:, 0:1] * ya[...] + g[:, 1:2] * yb[...]


def _combine(dest_tiles, gates, h1, y, *, batch, seq, lp_len, tile):
    d = h1.shape[1]
    nt = seq // tile
    kern = functools.partial(_combine_kernel, lp_len=lp_len, tile=tile)
    return pl.pallas_call(
        kern,
        grid=(batch, nt),
        in_specs=[
            pl.BlockSpec((1, 1, TOP_K * tile), lambda b, i: (b * nt + i, 0, 0), memory_space=pltpu.SMEM),
            pl.BlockSpec((1, tile, TOP_K), lambda b, i: (b, i, 0)),
            pl.BlockSpec(memory_space=pl.ANY),
            pl.BlockSpec(memory_space=pl.ANY),
        ],
        out_specs=pl.BlockSpec((1, tile, d), lambda b, i: (b, i, 0)),
        out_shape=jax.ShapeDtypeStruct((batch, seq, d), F32),
        scratch_shapes=[
            pltpu.VMEM((tile, d), F32),
            pltpu.VMEM((tile, d), F32),
            pltpu.VMEM((tile, d), F32),
            pltpu.SemaphoreType.DMA(()),
            pltpu.SemaphoreType.DMA(()),
            pltpu.SemaphoreType.DMA(()),
        ],
        compiler_params=_cparams(2),
        name="moe_combine",
    )(dest_tiles, gates, h1, y)


def _sc_workers():
    info = plsc.get_sparse_core_info()
    return info.num_cores, info.num_cores * info.num_subcores


def _sc_scatter_rows(planes, idx_a, idx_b, out_rows):
    n_win = idx_a.shape[0]
    n_cores, n_workers = _sc_workers()
    trips = -(-n_win // n_workers)
    mesh = plsc.VectorSubcoreMesh(core_axis_name="c", subcore_axis_name="s")

    def body(*refs):
        xs = refs[0:ROW_PLANES]
        ia_hbm, ib_hbm = refs[ROW_PLANES:ROW_PLANES + 2]
        outs = refs[ROW_PLANES + 2:2 * ROW_PLANES + 2]
        ia_v, ib_v, buf, sem = refs[2 * ROW_PLANES + 2:]
        wid = lax.axis_index("s") * n_cores + lax.axis_index("c")

        def step(t, carry):
            g = wid + t * n_workers

            @pl.when(g < n_win)
            def _():
                pltpu.sync_copy(ia_hbm.at[g], ia_v)
                pltpu.sync_copy(ib_hbm.at[g], ib_v)
                row0 = pl.multiple_of(g * SC_WINDOW, SC_WINDOW)
                loads = [pltpu.async_copy(xs[c].at[pl.ds(row0, SC_WINDOW)], buf.at[c], sem)
                         for c in range(ROW_PLANES)]
                for cp in loads:
                    cp.wait()
                stores = [pltpu.async_copy(buf.at[c], outs[c].at[iv], sem)
                          for c in range(ROW_PLANES) for iv in (ia_v, ib_v)]
                for cp in stores:
                    cp.wait()

            return carry

        lax.fori_loop(0, trips, step, 0)

    kern = pl.kernel(
        body,
        out_type=[jax.ShapeDtypeStruct((out_rows, LANES), jnp.uint32)] * ROW_PLANES,
        mesh=mesh,
        scratch_types=[
            pltpu.VMEM((SC_WINDOW,), jnp.int32),
            pltpu.VMEM((SC_WINDOW,), jnp.int32),
            pltpu.VMEM((ROW_PLANES, SC_WINDOW, LANES), jnp.uint32),
            pltpu.SemaphoreType.DMA,
        ],
        name="moe_dispatch_sc",
    )
    return kern(*planes, idx_a, idx_b)


def _sc_gather_rows(planes, idx_a, idx_b):
    n_win = idx_a.shape[0]
    n_cores, n_workers = _sc_workers()
    trips = -(-n_win // n_workers)
    mesh = plsc.VectorSubcoreMesh(core_axis_name="c", subcore_axis_name="s")

    def body(*refs):
        ys = refs[0:ROW_PLANES]
        ia_hbm, ib_hbm = refs[ROW_PLANES:ROW_PLANES + 2]
        outs_a = refs[ROW_PLANES + 2:2 * ROW_PLANES + 2]
        outs_b = refs[2 * ROW_PLANES + 2:3 * ROW_PLANES + 2]
        iv, buf, sem = refs[3 * ROW_PLANES + 2:]
        wid = lax.axis_index("s") * n_cores + lax.axis_index("c")

        def step(t, carry):
            g = wid + t * n_workers

            @pl.when(g < n_win)
            def _():
                row0 = pl.multiple_of(g * SC_WINDOW, SC_WINDOW)
                for i_hbm, outs in ((ia_hbm, outs_a), (ib_hbm, outs_b)):
                    pltpu.sync_copy(i_hbm.at[g], iv)
                    loads = [pltpu.async_copy(ys[c].at[iv], buf.at[c], sem) for c in range(ROW_PLANES)]
                    for cp in loads:
                        cp.wait()
                    stores = [pltpu.async_copy(buf.at[c], outs[c].at[pl.ds(row0, SC_WINDOW)], sem)
                              for c in range(ROW_PLANES)]
                    for cp in stores:
                        cp.wait()

            return carry

        lax.fori_loop(0, trips, step, 0)

    n_rows = n_win * SC_WINDOW
    kern = pl.kernel(
        body,
        out_type=[jax.ShapeDtypeStruct((n_rows, LANES), jnp.uint32)] * (2 * ROW_PLANES),
        mesh=mesh,
        scratch_types=[
            pltpu.VMEM((SC_WINDOW,), jnp.int32),
            pltpu.VMEM((ROW_PLANES, SC_WINDOW, LANES), jnp.uint32),
            pltpu.SemaphoreType.DMA,
        ],
        name="moe_gather_sc",
    )
    res = kern(*planes, idx_a, idx_b)
    return res[:ROW_PLANES], res[ROW_PLANES:]


def _experts_kernel(be_ref, nv_ref, x0_ref, x1_ref, x2_ref, x3_ref, wg_ref, wu_ref, wd_ref,
                    y0_ref, y1_ref, y2_ref, y3_ref, wgb_ref, wub_ref, wdb_ref):
    i = pl.program_id(0)
    e = be_ref[i]
    prev = be_ref[jnp.maximum(i - 1, 0)]
    y_refs = (y0_ref, y1_ref, y2_ref, y3_ref)

    @pl.when((i == 0) | (e != prev))
    def _():
        wgb_ref[...] = wg_ref[0].astype(BF16)
        wub_ref[...] = wu_ref[0].astype(BF16)
        wdb_ref[...] = wd_ref[0].astype(BF16)

    nv = nv_ref[i]

    @pl.when(nv > 0)
    def _():
        xs = _unpack_rows([x0_ref[...], x1_ref[...], x2_ref[...], x3_ref[...]])
        row = lax.broadcasted_iota(jnp.int32, xs.shape, 0)
        x = jnp.where(row < nv, xs, jnp.zeros_like(xs))
        hg = jnp.dot(x, wgb_ref[...], preferred_element_type=F32)
        hu = jnp.dot(x, wub_ref[...], preferred_element_type=F32)
        hid = (hg / (1.0 + jnp.exp(-hg)) * hu).astype(BF16)
        yw = _pack_rows(jnp.dot(hid, wdb_ref[...], preferred_element_type=F32))
        for c, ref in enumerate(y_refs):
            ref[...] = yw[:, c * LANES:(c + 1) * LANES]

    @pl.when(nv == 0)
    def _():
        for ref in y_refs:
            ref[...] = jnp.zeros_like(ref)


def _experts(block_e, nvalid, xs_planes, w_gate, w_up, w_down):
    n_blocks = block_e.shape[0]
    _, d, ff = w_gate.shape
    blk = lambda i, be, nv: (i, 0)
    wsel = lambda i, be, nv: (be[i], 0, 0)
    grid_spec = pltpu.PrefetchScalarGridSpec(
        num_scalar_prefetch=2,
        grid=(n_blocks,),
        in_specs=[
            *[pl.BlockSpec((MOE_BLOCK, LANES), blk)] * ROW_PLANES,
            pl.BlockSpec((1, d, ff), wsel),
            pl.BlockSpec((1, d, ff), wsel),
            pl.BlockSpec((1, ff, d), wsel),
        ],
        out_specs=[pl.BlockSpec((MOE_BLOCK, LANES), blk)] * ROW_PLANES,
        scratch_shapes=[
            pltpu.VMEM((d, ff), BF16),
            pltpu.VMEM((d, ff), BF16),
            pltpu.VMEM((ff, d), BF16),
        ],
    )
    return pl.pallas_call(
        _experts_kernel,
        grid_spec=grid_spec,
        out_shape=[jax.ShapeDtypeStruct((n_blocks * MOE_BLOCK, LANES), jnp.uint32)] * ROW_PLANES,
        compiler_params=_cparams(1),
        name="moe_experts",
    )(block_e, nvalid, *xs_planes, w_gate, w_up, w_down)


def _combine_kernel(gates_ref, h1_hbm, *refs, lp_len, tile):
    a_refs = refs[0:ROW_PLANES]
    b_refs = refs[ROW_PLANES:2 * ROW_PLANES]
    out_ref, hbuf, sem_h = refs[2 * ROW_PLANES:]
    nt = pl.num_programs(1)
    step = pl.program_id(0) * nt + pl.program_id(1)
    last = pl.num_programs(0) * nt - 1

    def h_copy(s, slot):
        start = (s // nt) * lp_len + N_META + (s % nt) * tile
        return pltpu.make_async_copy(h1_hbm.at[pl.ds(start, tile), :], hbuf.at[slot], sem_h.at[slot])

    slot = step % 2

    @pl.when(step == 0)
    def _():
        h_copy(step, slot).start()

    @pl.when(step < last)
    def _():
        h_copy(step + 1, 1 - slot).start()

    ya = _unpack_rows([r[...] for r in a_refs]).astype(F32)
    yb = _unpack_rows([r[...] for r in b_refs]).astype(F32)
    g = gates_ref[0]
    moe = g[:, 0:1] * ya + g[:, 1:2] * yb
    h_copy(step, slot).wait()
    out_ref[0] = hbuf[slot] + moe


def _combine(gates, h1, a_planes, b_planes, *, batch, seq, lp_len, tile):
    d = h1.shape[1]
    nt = seq // tile
    kern = functools.partial(_combine_kernel, lp_len=lp_len, tile=tile)
    rows = lambda b, i: (b * nt + i, 0)
    return pl.pallas_call(
        kern,
        grid=(batch, nt),
        in_specs=[
            pl.BlockSpec((1, tile, TOP_K), lambda b, i: (b, i, 0)),
            pl.BlockSpec(memory_space=pl.ANY),
            *[pl.BlockSpec((tile, LANES), rows)] * (2 * ROW_PLANES),
        ],
        out_specs=pl.BlockSpec((1, tile, d), lambda b, i: (b, i, 0)),
        out_shape=jax.ShapeDtypeStruct((batch, seq, d), F32),
        scratch_shapes=[pltpu.VMEM((2, tile, d), F32), pltpu.SemaphoreType.DMA((2,))],
        compiler_params=_cparams(2),
        name="moe_combine",
    )(gates, h1, *a_planes, *b_planes)


def _rope_tables(length, lp_len):
    half = ROPE_DIM // 2
    pos = jnp.arange(length, dtype=F32)
    inv_freq = ROPE_THETA ** (-jnp.arange(0, ROPE_DIM, 2, dtype=F32) / ROPE_DIM)
    ang = pos[:, None] * inv_freq[None, :]
    cos = jnp.cos(ang)
    sin = jnp.sin(ang)
    ones = jnp.ones((length, HEAD_DIM - ROPE_DIM), F32)
    zeros_h = jnp.zeros((length, half), F32)
    zeros_r = jnp.zeros((length, HEAD_DIM - ROPE_DIM), F32)
    c = jnp.concatenate([cos, cos, ones], axis=1)
    s1 = jnp.concatenate([zeros_h, sin, zeros_r], axis=1)
    s2 = jnp.concatenate([-sin, zeros_h, zeros_r], axis=1)
    pad = ((0, lp_len - length), (0, 0))
    rep = LANES // HEAD_DIM
    return tuple(jnp.pad(jnp.tile(t, (1, rep)), pad) for t in (c, s1, s2))


def _layer(hp, l, batch, length, lp_len, tm, norm1_g, w_in, conv_w, q_norm_g, k_norm_g, lambda_q1, lambda_k1,
           lambda_q2, lambda_k2, subln_g, w_out, norm2_g, w_router_group, b_router_group, w_router_expert,
           b_router_expert, w_gate, w_up, w_down, rope, last):
    n, d = hp.shape
    tiles_per_seq = lp_len // tm
    cw = conv_w.shape[2]
    qw = N_HEADS * 2 * HEAD_DIM
    lam_init = 0.8 - 0.6 * math.exp(-0.3 * l)

    reps = qw // HEAD_DIM
    gq = jnp.tile(q_norm_g[l] * (HEAD_DIM ** -0.5 * LOG2E), reps)[None, :]
    gk = jnp.tile(k_norm_g[l], reps)[None, :]
    seg = jnp.arange(qw) // HEAD_DIM
    bd = (seg[:, None] == seg[None, :]).astype(BF16)
    convy, q, k, v = _inproj(hp, norm1_g[l][None, :], w_in[l].astype(BF16), conv_w[l], gq, gk, bd, *rope,
                             tiles_per_seq=tiles_per_seq, tm=tm)

    lamp = jnp.stack([lambda_q1[l], lambda_k1[l], lambda_q2[l], lambda_k2[l]]).astype(F32)
    o = _attention(q, k, v, lamp, subln_g[l][None, :], batch=batch, lp_len=lp_len, tq=tm, lam_init=lam_init)

    lane_pad = LANES - N_GROUPS - N_EXPERTS
    wr = jnp.pad(jnp.concatenate([w_router_group[l], w_router_expert[l]], axis=1), ((0, 0), (0, lane_pad)))
    wr_hi = wr.astype(BF16)
    wr_lo = (wr - wr_hi.astype(F32)).astype(BF16)
    br = jnp.pad(jnp.concatenate([b_router_group[l], b_router_expert[l]]), (0, lane_pad))[None, :]
    ridx = jnp.arange(tm // OUTPROJ_CHAINS)
    tri =(ridx[None, :] < ridx[:, None]).astype(BF16)
    h1, *rest = _outproj(hp, convy, o, w_out[l].astype(BF16), norm2_g[l][None, :], wr_hi, wr_lo, br,
                         tri, tiles_per_seq=tiles_per_seq, seq_len=length, tm=tm)
    x_planes = rest[:ROW_PLANES]
    route, cnt = rest[ROW_PLANES:]

    route = route.reshape(batch, lp_len, ROUTE_COLS)
    eid = route[..., 0:TOP_K].astype(jnp.int32)
    gates = route[..., TOP_K:2 * TOP_K]
    rank = route[..., 2 * TOP_K:3 * TOP_K].astype(jnp.int32)
    counts = cnt[0, EXPERT_LANE0:EXPERT_LANE0 + N_EXPERTS].astype(jnp.int32)
    a = batch * length * TOP_K
    n_blocks = -(-a // MOE_BLOCK) + N_EXPERTS
    p_rows = n_blocks * MOE_BLOCK
    padded = (counts + MOE_BLOCK - 1) // MOE_BLOCK * MOE_BLOCK
    pends = jnp.cumsum(padded)
    pstarts = pends - padded
    experts = jnp.arange(N_EXPERTS, dtype=jnp.int32)

    def lookup(table, idx):
        return jnp.sum(jnp.where(idx[..., None] == experts, table, 0), axis=-1)

    n_pad = lp_len - length
    pos = jnp.arange(lp_len, dtype=jnp.int32)[None, :, None]
    spare = p_rows + ((jnp.arange(batch, dtype=jnp.int32)[:, None, None] * n_pad + (pos - length)) * TOP_K
                      + jnp.arange(TOP_K, dtype=jnp.int32)[None, None, :])
    dest = jnp.where(pos < length, lookup(pstarts, eid) + rank, spare)
    spare_rows = -(-(batch * n_pad * TOP_K) // MOE_BLOCK) * MOE_BLOCK
    blk0 = jnp.arange(n_blocks, dtype=jnp.int32) * MOE_BLOCK
    block_e = jnp.minimum(jnp.sum((pends[None, :] <= blk0[:, None]).astype(jnp.int32), axis=1), N_EXPERTS - 1)
    nvalid = jnp.clip(lookup(counts, block_e) - (blk0 - lookup(pstarts, block_e)), 0, MOE_BLOCK)

    assert (batch * lp_len) % SC_WINDOW == 0
    xs_planes = _sc_scatter_rows(x_planes, dest[..., 0].reshape(-1, SC_WINDOW),
                                 dest[..., 1].reshape(-1, SC_WINDOW), p_rows + spare_rows)
    y_planes = _experts(block_e, nvalid, xs_planes, w_gate[l], w_up[l], w_down[l])

    if not last:
        raise NotImplementedError("only the final layer's combine (which drops the meta tokens) is implemented")
    seq = length - N_META
    assert (batch * seq) % SC_WINDOW == 0
    dest_x = dest[:, N_META:length]
    a_planes, b_planes = _sc_gather_rows(y_planes, dest_x[..., 0].reshape(-1, SC_WINDOW),
                                         dest_x[..., 1].reshape(-1, SC_WINDOW))
    ctile = _largest_tile(seq, 512, 8)
    return _combine(gates[:, N_META:length], h1, a_planes, b_planes, batch=batch, seq=seq, lp_len=lp_len,
                    tile=ctile)


def kernel(x, meta_tokens, norm1_g, w_in, conv_w, q_norm_g, k_norm_g, lambda_q1, lambda_k1, lambda_q2, lambda_k2,
           subln_g, w_out, norm2_g, w_router_group, b_router_group, w_router_expert, b_router_expert, w_gate,
           w_up, w_down):
    b, s, d = x.shape
    depth = w_in.shape[0]
    assert depth == 1, "a single layer is supported"
    length = s + N_META
    tm = TOKEN_TILE if length >= TOKEN_TILE else LANES
    lp_len = -(-length // tm) * tm
    hp = jnp.concatenate([jnp.broadcast_to(meta_tokens[None].astype(x.dtype), (b, N_META, d)), x,
                          jnp.zeros((b, lp_len - length, d), x.dtype)], axis=1).reshape(b * lp_len, d)
    rope = _rope_tables(length, lp_len)
    return _layer(hp, 0, b, length, lp_len, tm, norm1_g, w_in, conv_w, q_norm_g, k_norm_g, lambda_q1, lambda_k1,
                  lambda_q2, lambda_k2, subln_g, w_out, norm2_g, w_router_group, b_router_group,
                  w_router_expert, b_router_expert, w_gate, w_up, w_down, rope, last=True)
```

```python
import functools
import math

import jax
import jax.numpy as jnp
from jax import lax
from jax.experimental import pallas as pl
from jax.experimental.pallas import tpu as pltpu
from jax.experimental.pallas import tpu_sc as plsc

F32 = jnp.float32
BF16 = jnp.bfloat16

N_META = 16
CONV_K = 3
N_HEADS = 4
HEAD_DIM = 64
V_DIM = 2 * HEAD_DIM
ROPE_DIM = HEAD_DIM // 4
ROPE_THETA = 500000.0
N_GROUPS = 4
EXPERTS_PER_GROUP = 8
N_EXPERTS = N_GROUPS * EXPERTS_PER_GROUP
TOP_K = 2
EPS = 1e-6
LOG2E = 1.4426950408889634

LANES = 128
TOKEN_TILE = 640
OUTPROJ_CHAINS = 2
ATTN_WIDE_CHUNKS = 2
MOE_BLOCK = 512
ROUTE_COLS = 8
ROUTE_ROWS = 64
ROW_PLANES = 4
SC_WINDOW = 128
EXPERT_LANE0 = N_GROUPS
NEG_BIG = -1e30
VMEM_LIMIT = 56 * 1024 * 1024


def _largest_tile(n, cap, mult):
    for t in range(min(cap, n), 0, -1):
        if n % t == 0 and t % mult == 0:
            return t
    raise ValueError(f"no tile for {n}")


def _cparams(n_axes, flags=None):
    return pltpu.CompilerParams(dimension_semantics=("arbitrary",) * n_axes,
                                vmem_limit_bytes=VMEM_LIMIT, flags=flags)


def _inproj_kernel(x_ref, g1_ref, win_ref, convw_ref, gq_ref, gk_ref, bd_ref, rc_ref, rs1_ref, rs2_ref,
                   convy_ref, q_ref, k_ref, v_ref, carry_ref, *, tiles_per_seq, cw, qw):
    i = pl.program_id(0)
    tm = x_ref.shape[0]
    x = x_ref[...]
    ms = jnp.mean(x * x, axis=-1, keepdims=True)
    xn = (x * lax.rsqrt(ms + EPS) * g1_ref[...]).astype(BF16)
    q0 = 3 * cw

    def proj(lo, hi):
        return jnp.dot(xn, win_ref[:, lo:hi], preferred_element_type=F32)

    u_conv = proj(0, q0)
    u_q = proj(q0, q0 + qw)

    z = u_conv[:, cw:2 * cw] * u_conv[:, 2 * cw:3 * cw]

    @pl.when(i % tiles_per_seq == 0)
    def _():
        carry_ref[...] = jnp.zeros_like(carry_ref)

    prev = carry_ref[...]
    p1 = prev[7:8]
    p2 = prev[6:7]
    row = lax.broadcasted_iota(jnp.int32, z.shape, 0)
    z1 = jnp.where(row == 0, p1, pltpu.roll(z, 1, axis=0))
    z2 = jnp.where(row == 0, p2, jnp.where(row == 1, p1, pltpu.roll(z, 2, axis=0)))
    carry_ref[...] = z[tm - 8:tm]
    w = convw_ref[...]
    conv = w[0:1] * z2 + w[1:2] * z1 + w[2:3] * z
    convy_ref[...] = (u_conv[:, 0:cw] * conv).astype(BF16)

    rc = rc_ref[...]
    rs1 = rs1_ref[...]
    rs2 = rs2_ref[...]

    def norm_rope(t, g_ref):
        ss = jnp.dot((t * t).astype(BF16), bd_ref[...], preferred_element_type=F32)
        tn = t * lax.rsqrt(ss * (1.0 / HEAD_DIM) + EPS) * g_ref[...]
        outs = []
        for c in range(qw // LANES):
            ch = tn[:, c * LANES:(c + 1) * LANES]
            outs.append(ch * rc + pltpu.roll(ch, ROPE_DIM // 2, axis=1) * rs1
                        + pltpu.roll(ch, LANES - ROPE_DIM // 2, axis=1) * rs2)
        return jnp.concatenate(outs, axis=1).astype(BF16)

    u_k = proj(q0 + qw, q0 + 2 * qw)
    q_ref[...] = norm_rope(u_q, gq_ref)
    u_v = proj(q0 + 2 * qw, win_ref.shape[1])
    k_ref[...] = norm_rope(u_k, gk_ref)
    v_ref[...] = u_v.astype(BF16)


def _inproj(hp, g1, w_in, conv_w, gq, gk, bd, rc, rs1, rs2, *, tiles_per_seq, tm):
    n, d = hp.shape
    cw = conv_w.shape[1]
    qw = gq.shape[1]
    aw = w_in.shape[1] - 3 * cw - 2 * qw
    const = lambda i: (0, 0)
    tile = lambda i: (i, 0)
    pos = lambda i: (i % tiles_per_seq, 0)
    kern = functools.partial(_inproj_kernel, tiles_per_seq=tiles_per_seq, cw=cw, qw=qw)
    return pl.pallas_call(
        kern,
        grid=(n // tm,),
        in_specs=[
            pl.BlockSpec((tm, d), tile),
            pl.BlockSpec((1, d), const),
            pl.BlockSpec(w_in.shape, const),
            pl.BlockSpec(conv_w.shape, const),
            pl.BlockSpec((1, qw), const),
            pl.BlockSpec((1, qw), const),
            pl.BlockSpec(bd.shape, const),
            pl.BlockSpec((tm, LANES), pos),
            pl.BlockSpec((tm, LANES), pos),
            pl.BlockSpec((tm, LANES), pos),
        ],
        out_specs=[
            pl.BlockSpec((tm, cw), tile),
            pl.BlockSpec((tm, qw), tile),
            pl.BlockSpec((tm, qw), tile),
            pl.BlockSpec((tm, aw), tile),
        ],
        out_shape=[
            jax.ShapeDtypeStruct((n, cw), BF16),
            jax.ShapeDtypeStruct((n, qw), BF16),
            jax.ShapeDtypeStruct((n, qw), BF16),
            jax.ShapeDtypeStruct((n, aw), BF16),
        ],
        scratch_shapes=[pltpu.VMEM((8, cw), F32)],
        compiler_params=_cparams(1),
        name="inproj",
    )(hp, g1, w_in, conv_w, gq, gk, bd, rc, rs1, rs2)


def _attn_kernel(q_ref, k_ref, v_ref, lamp_ref, sg_ref, o_ref, qs_ref, m_ref, l_ref, acc_ref, *, lam_init):
    qi = pl.program_id(2)
    tq = q_ref.shape[0]
    q = q_ref[...]
    lane = lax.broadcasted_iota(jnp.int32, q.shape, 1)
    zero = jnp.zeros_like(q)
    qs_ref[0:tq, :] = jnp.where(lane < HEAD_DIM, q, zero)
    qs_ref[tq:2 * tq, :] = jnp.where(lane >= HEAD_DIM, q, zero)
    m_ref[...] = jnp.full_like(m_ref, NEG_BIG)
    l_ref[...] = jnp.zeros_like(l_ref)
    acc_ref[...] = jnp.zeros_like(acc_ref)

    def scores(off, width, which):
        kc = k_ref[pl.ds(off, width), :]
        return lax.dot_general(qs_ref[pl.ds(which * tq, tq), :], kc, (((1,), (1,)), ((), ())),
                               preferred_element_type=F32)

    def update(off, width, which, s, masked):
        vc = jnp.concatenate([v_ref[pl.ds(off, width), :], jnp.ones((width, LANES), BF16)], axis=1)
        rows = pl.ds(which * tq, tq)
        if masked:
            r = lax.broadcasted_iota(jnp.int32, s.shape, 0)
            c = lax.broadcasted_iota(jnp.int32, s.shape, 1)
            s = jnp.where(c <= r, s, NEG_BIG)
        m_prev = m_ref[rows, :]
        m_new = jnp.maximum(m_prev, jnp.max(s, axis=-1, keepdims=True))
        alpha = jnp.exp2(m_prev - m_new)
        p = jnp.exp2((s - jnp.tile(m_new, (1, width // LANES))).astype(BF16))
        pv = jnp.dot(p, vc, preferred_element_type=F32)
        l_ref[rows, :] = alpha * l_ref[rows, :] + pv[:, LANES:]
        acc_ref[rows, :] = alpha * acc_ref[rows, :] + pv[:, :LANES]
        m_ref[rows, :] = m_new

    def chunk(off, width, masked):
        s1 = scores(off, width, 0)
        s2 = scores(off, width, 1)
        update(off, width, 0, s1, masked)
        update(off, width, 1, s2, masked)

    wide = ATTN_WIDE_CHUNKS * tq

    def body(j, carry):
        chunk(pl.multiple_of(j * wide, wide), wide, False)
        return carry

    n_wide = qi // ATTN_WIDE_CHUNKS
    lax.fori_loop(0, n_wide, body, 0)
    for extra in range(ATTN_WIDE_CHUNKS - 1):
        @pl.when(n_wide * ATTN_WIDE_CHUNKS + extra < qi)
        def _():
            chunk(pl.multiple_of((n_wide * ATTN_WIDE_CHUNKS + extra) * tq, tq), tq, False)

    chunk(pl.multiple_of(qi * tq, tq), tq, True)

    lp = lamp_ref[...]
    lam = (jnp.exp(jnp.sum(lp[0:1] * lp[1:2], axis=-1, keepdims=True))
           - jnp.exp(jnp.sum(lp[2:3] * lp[3:4], axis=-1, keepdims=True)) + lam_init)
    o_all = acc_ref[...] / l_ref[...]
    o = o_all[0:tq] - lam * o_all[tq:2 * tq]
    ms = jnp.mean(o * o, axis=-1, keepdims=True)
    o_ref[...] = (o * lax.rsqrt(ms + EPS) * sg_ref[...] * (1.0 - lam_init)).astype(BF16)


def _attention(q, k, v, lamp, sg, *, batch, lp_len, tq, lam_init):
    n, qw = q.shape
    nq = lp_len // tq
    heads = qw // LANES
    kern = functools.partial(_attn_kernel, lam_init=lam_init)
    return pl.pallas_call(
        kern,
        grid=(batch, heads, nq),
        in_specs=[
            pl.BlockSpec((tq, LANES), lambda b, h, i: (b * nq + i, h)),
            pl.BlockSpec((lp_len, LANES), lambda b, h, i: (b, h)),
            pl.BlockSpec((lp_len, LANES), lambda b, h, i: (b, h)),
            pl.BlockSpec(lamp.shape, lambda b, h, i: (0, 0)),
            pl.BlockSpec(sg.shape, lambda b, h, i: (0, 0)),
        ],
        out_specs=pl.BlockSpec((tq, LANES), lambda b, h, i: (b * nq + i, h)),
        out_shape=jax.ShapeDtypeStruct((n, v.shape[1]), BF16),
        scratch_shapes=[
            pltpu.VMEM((2 * tq, LANES), BF16),
            pltpu.VMEM((2 * tq, LANES), F32),
            pltpu.VMEM((2 * tq, LANES), F32),
            pltpu.VMEM((2 * tq, LANES), F32),
        ],
        compiler_params=_cparams(3),
        name="diffattn",
    )(q, k, v, lamp, sg)


def _pack_rows(x):
    w = x.shape[1] // 2
    lo = lax.bitcast_convert_type(x[:, :w].astype(BF16).astype(F32), jnp.uint32)
    hi = lax.bitcast_convert_type(x[:, w:].astype(BF16).astype(F32), jnp.uint32)
    return lax.shift_right_logical(lo, jnp.uint32(16)) | (hi & jnp.uint32(0xFFFF0000))


def _unpack_rows(planes):
    w = jnp.concatenate(planes, axis=1)
    lo = lax.bitcast_convert_type(lax.shift_left(w, jnp.uint32(16)), F32)
    hi = lax.bitcast_convert_type(w & jnp.uint32(0xFFFF0000), F32)
    return jnp.concatenate([lo, hi], axis=1).astype(BF16)


def _outproj_kernel(hp_ref, cy_ref, o_ref, wout_ref, g2_ref, wrh_ref, wrl_ref, br_ref, tri_ref,
                    h1_ref, xp0_ref, xp1_ref, xp2_ref, xp3_ref, route_ref, cnt_ref, run_ref,
                    *, tiles_per_seq, seq_len):
    i = pl.program_id(0)
    tm = hp_ref.shape[0]
    sub = tri_ref.shape[0]
    wr_both = jnp.concatenate([wrh_ref[...], wrl_ref[...]], axis=1)

    @pl.when(i == 0)
    def _():
        run_ref[...] = jnp.zeros_like(run_ref)

    def project(rows):
        mix = jnp.concatenate([cy_ref[rows, :], o_ref[rows, :]], axis=1)
        h1 = hp_ref[rows, :] + jnp.dot(mix, wout_ref[...], preferred_element_type=F32)
        h1_ref[rows, :] = h1
        ms = jnp.mean(h1 * h1, axis=-1, keepdims=True)
        xn = h1 * lax.rsqrt(ms + EPS) * g2_ref[...]
        xw = _pack_rows(xn)
        for c, ref in enumerate((xp0_ref, xp1_ref, xp2_ref, xp3_ref)):
            ref[rows, :] = xw[:, c * LANES:(c + 1) * LANES]
        x_hi = xn.astype(BF16)
        x_lo = (xn - x_hi.astype(F32)).astype(BF16)
        hi_both = jnp.dot(x_hi, wr_both, preferred_element_type=F32)
        return (hi_both[:, :LANES] + hi_both[:, LANES:]
                + jnp.dot(x_lo, wrh_ref[...], preferred_element_type=F32) + br_ref[...])

    chains = [pl.ds(c * sub, sub) for c in range(tm // sub)]
    all_logits = [project(rows) for rows in chains]
    run = run_ref[0:1, :]
    for c, (rows, logits) in enumerate(zip(chains, all_logits)):
        run = _route_rows(i, c, rows, logits, run, tri_ref, route_ref, tiles_per_seq=tiles_per_seq,
                          seq_len=seq_len, tm=tm)
    run_ref[...] = jnp.broadcast_to(run, run_ref.shape)
    cnt_ref[...] = jnp.broadcast_to(run, cnt_ref.shape)


def _route_rows(i, c, rows, logits, run, tri_ref, route_ref, *, tiles_per_seq, seq_len, tm):
    sub = logits.shape[0]
    lane = lax.broadcasted_iota(jnp.int32, logits.shape, 1)
    big = jnp.int32(4 * LANES)

    def first_argmax(vals, vmax):
        return jnp.min(jnp.where(vals == vmax, lane, big), axis=-1, keepdims=True)

    gl = jnp.where(lane < N_GROUPS, logits, NEG_BIG)
    gmax = jnp.max(gl, axis=-1, keepdims=True)
    g_val = 1.0 / jnp.sum(jnp.exp(gl - gmax), axis=-1, keepdims=True)
    g_idx = first_argmax(gl, gmax)
    lo = EXPERT_LANE0 + EXPERTS_PER_GROUP * g_idx
    el = jnp.where((lane >= lo) & (lane < lo + EXPERTS_PER_GROUP), logits, NEG_BIG)
    m1 = jnp.max(el, axis=-1, keepdims=True)
    i1 = first_argmax(el, m1)
    el2 = jnp.where(lane == i1, NEG_BIG, el)
    m2 = jnp.max(el2, axis=-1, keepdims=True)
    i2 = first_argmax(el2, m2)
    r = jnp.exp(m2 - m1)
    gate1 = g_val / (1.0 + r)
    gate2 = g_val * r / (1.0 + r)

    prow = (i % tiles_per_seq) * tm + c * sub + lax.broadcasted_iota(jnp.int32, logits.shape, 0)
    valid = prow < seq_len
    oh1 = jnp.where(valid & (lane == i1), 1.0, 0.0)
    oh2 = jnp.where(valid & (lane == i2), 1.0, 0.0)
    pre = jnp.dot(tri_ref[...], jnp.concatenate([oh1, oh2], axis=1).astype(BF16), preferred_element_type=F32)
    pre1 = pre[:, :LANES]
    pre2 = pre[:, LANES:]
    tot1 = jnp.sum(oh1, axis=0, keepdims=True)
    tot2 = jnp.sum(oh2, axis=0, keepdims=True)
    rank1 = jnp.sum(oh1 * (pre1 + run), axis=-1, keepdims=True)
    rank2 = jnp.sum(oh2 * (pre2 + run + tot1), axis=-1, keepdims=True)

    e1 = (i1 - EXPERT_LANE0).astype(F32)
    e2 = (i2 - EXPERT_LANE0).astype(F32)
    packed = jnp.where(lane == 0, e1, jnp.where(lane == 1, e2, jnp.where(lane == 2, gate1, jnp.where(
        lane == 3, gate2, jnp.where(lane == 4, rank1, jnp.where(lane == 5, rank2, 0.0))))))
    route_ref[rows, :] = packed[:, 0:ROUTE_COLS]
    return run + tot1 + tot2


def _outproj(hp, convy, o, w_out, g2, wr_hi, wr_lo, br, tri, *, tiles_per_seq, seq_len, tm):
    n, d = hp.shape
    const = lambda i: (0, 0)
    tile = lambda i: (i, 0)
    kern = functools.partial(_outproj_kernel, tiles_per_seq=tiles_per_seq, seq_len=seq_len)
    return pl.pallas_call(
        kern,
        grid=(n // tm,),
        in_specs=[
            pl.BlockSpec((tm, d), tile),
            pl.BlockSpec((tm, convy.shape[1]), tile),
            pl.BlockSpec((tm, o.shape[1]), tile),
            pl.BlockSpec(w_out.shape, const),
            pl.BlockSpec((1, d), const),
            pl.BlockSpec(wr_hi.shape, const),
            pl.BlockSpec(wr_lo.shape, const),
            pl.BlockSpec((1, LANES), const),
            pl.BlockSpec(tri.shape, const),
        ],
        out_specs=[
            pl.BlockSpec((tm, d), tile),
            *[pl.BlockSpec((tm, LANES), tile)] * ROW_PLANES,
            pl.BlockSpec((tm, ROUTE_COLS), tile),
            pl.BlockSpec((8, LANES), const),
        ],
        out_shape=[
            jax.ShapeDtypeStruct((n, d), F32),
            *[jax.ShapeDtypeStruct((n, LANES), jnp.uint32)] * ROW_PLANES,
            jax.ShapeDtypeStruct((n, ROUTE_COLS), F32),
            jax.ShapeDtypeStruct((8, LANES), F32),
        ],
        scratch_shapes=[pltpu.VMEM((8, LANES), F32)],
        compiler_params=_cparams(1),
        name="outproj_router",
    )(hp, convy, o, w_out, g2, wr_hi, wr_lo, br, tri)


def _outproj_t_kernel(hp_ref, cy_ref, o_ref, wout_ref, g2_ref, wrh_ref, wrl_ref, br_ref, upper_ref,
                      h1_ref, xp0_ref, xp1_ref, xp2_ref, xp3_ref, route_ref, cnt_ref, run_ref,
                      *, tiles_per_seq, seq_len):
    i = pl.program_id(0)
    tm = hp_ref.shape[0]

    @pl.when(i == 0)
    def _():
        run_ref[...] = jnp.zeros_like(run_ref)

    mix = jnp.concatenate([cy_ref[...], o_ref[...]], axis=1)
    h1 = hp_ref[...] + jnp.dot(mix, wout_ref[...], preferred_element_type=F32)
    h1_ref[...] = h1
    ms = jnp.mean(h1 * h1, axis=-1, keepdims=True)
    xn = h1 * lax.rsqrt(ms + EPS) * g2_ref[...]
    xw = _pack_rows(xn)
    for c, ref in enumerate((xp0_ref, xp1_ref, xp2_ref, xp3_ref)):
        ref[...] = xw[:, c * LANES:(c + 1) * LANES]

    x_hi = xn.astype(BF16)
    x_lo = (xn - x_hi.astype(F32)).astype(BF16)
    hi_both = jnp.dot(x_hi, jnp.concatenate([wrh_ref[...], wrl_ref[...]], axis=1), preferred_element_type=F32)
    logits = (hi_both[:, :LANES] + hi_both[:, LANES:]
              + jnp.dot(x_lo, wrh_ref[...], preferred_element_type=F32) + br_ref[...])

    lt = logits.T[0:ROUTE_ROWS, :]
    row = lax.broadcasted_iota(jnp.int32, lt.shape, 0)
    big = jnp.int32(4 * LANES)

    def first_argmax(vals, vmax):
        return jnp.min(jnp.where(vals == vmax, row, big), axis=0, keepdims=True)

    gl = jnp.where(row < N_GROUPS, lt, NEG_BIG)
    gmax = jnp.max(gl, axis=0, keepdims=True)
    g_val = 1.0 / jnp.sum(jnp.exp(gl - gmax), axis=0, keepdims=True)
    g_idx = first_argmax(gl, gmax)
    lo = EXPERT_LANE0 + EXPERTS_PER_GROUP * g_idx
    el = jnp.where((row >= lo) & (row < lo + EXPERTS_PER_GROUP), lt, NEG_BIG)
    m1 = jnp.max(el, axis=0, keepdims=True)
    i1 = first_argmax(el, m1)
    el2 = jnp.where(row == i1, NEG_BIG, el)
    m2 = jnp.max(el2, axis=0, keepdims=True)
    i2 = first_argmax(el2, m2)
    r = jnp.exp(m2 - m1)
    gate1 = g_val / (1.0 + r)
    gate2 = g_val * r / (1.0 + r)

    pos = (i % tiles_per_seq) * tm + lax.broadcasted_iota(jnp.int32, (1, tm), 1)
    valid = pos < seq_len
    oh1 = jnp.where(valid & (row == i1), 1.0, 0.0)
    oh2 = jnp.where(valid & (row == i2), 1.0, 0.0)
    pre = jnp.dot(jnp.concatenate([oh1, oh2], axis=0).astype(BF16), upper_ref[...], preferred_element_type=F32)
    tot1 = jnp.sum(oh1, axis=1, keepdims=True)
    tot2 = jnp.sum(oh2, axis=1, keepdims=True)
    run = run_ref[...]
    run_t = jnp.tile(run, (1, tm // LANES))
    rank1 = jnp.sum(oh1 * (pre[:ROUTE_ROWS] + run_t), axis=0, keepdims=True)
    rank2 = jnp.sum(oh2 * (pre[ROUTE_ROWS:] + run_t + tot1), axis=0, keepdims=True)
    new_run = run + tot1 + tot2
    run_ref[...] = new_run
    cnt_ref[...] = new_run

    e1 = (i1 - EXPERT_LANE0).astype(F32)
    e2 = (i2 - EXPERT_LANE0).astype(F32)
    r8 = lax.broadcasted_iota(jnp.int32, (ROUTE_COLS, tm), 0)
    route_ref[...] = jnp.where(r8 == 0, e1, jnp.where(r8 == 1, e2, jnp.where(r8 == 2, gate1, jnp.where(
        r8 == 3, gate2, jnp.where(r8 == 4, rank1, jnp.where(r8 == 5, rank2, 0.0))))))


def _outproj_t(hp, convy, o, w_out, g2, wr_hi, wr_lo, br, upper, *, tiles_per_seq, seq_len, tm):
    n, d = hp.shape
    const = lambda i: (0, 0)
    tile = lambda i: (i, 0)
    kern = functools.partial(_outproj_t_kernel, tiles_per_seq=tiles_per_seq, seq_len=seq_len)
    return pl.pallas_call(
        kern,
        grid=(n // tm,),
        in_specs=[
            pl.BlockSpec((tm, d), tile),
            pl.BlockSpec((tm, convy.shape[1]), tile),
            pl.BlockSpec((tm, o.shape[1]), tile),
            pl.BlockSpec(w_out.shape, const),
            pl.BlockSpec((1, d), const),
            pl.BlockSpec(wr_hi.shape, const),
            pl.BlockSpec(wr_lo.shape, const),
            pl.BlockSpec((1, LANES), const),
            pl.BlockSpec(upper.shape, const),
        ],
        out_specs=[
            pl.BlockSpec((tm, d), tile),
            *[pl.BlockSpec((tm, LANES), tile)] * ROW_PLANES,
            pl.BlockSpec((ROUTE_COLS, tm), lambda i: (0, i)),
            pl.BlockSpec((ROUTE_ROWS, LANES), const),
        ],
        out_shape=[
            jax.ShapeDtypeStruct((n, d), F32),
            *[jax.ShapeDtypeStruct((n, LANES), jnp.uint32)] * ROW_PLANES,
            jax.ShapeDtypeStruct((ROUTE_COLS, n), F32),
            jax.ShapeDtypeStruct((ROUTE_ROWS, LANES), F32),
        ],
        scratch_shapes=[pltpu.VMEM((ROUTE_ROWS, LANES), F32)],
        compiler_params=_cparams(1),
        name="outproj_router",
    )(hp, convy, o, w_out, g2, wr_hi, wr_lo, br, upper)


def _row_copy(src_hbm, src_row, dst_ref, dst_row, sem):
    return pltpu.make_async_copy(src_hbm.at[pl.ds(src_row, 1), :], dst_ref.at[pl.ds(dst_row, 1), :], sem)


def _dispatch_kernel(dest_ref, xn_hbm, xs_hbm, xbuf, sem_in, sem, *, tiles_per_seq, lp_len, tile):
    t = pl.program_id(0)
    base = (t // tiles_per_seq) * lp_len + (t % tiles_per_seq) * tile
    tile_copy = pltpu.make_async_copy(xn_hbm.at[pl.ds(base, tile), :], xbuf, sem_in)
    tile_copy.start()
    tile_copy.wait()

    def issue(r, carry):
        for kk in range(TOP_K):
            _row_copy(xbuf, r, xs_hbm, dest_ref[0, 0, TOP_K * r + kk], sem).start()
        return carry

    lax.fori_loop(0, tile, issue, 0)

    def drain(r, carry):
        for kk in range(TOP_K):
            _row_copy(xbuf, r, xs_hbm, dest_ref[0, 0, TOP_K * r + kk], sem).wait()
        return carry

    lax.fori_loop(0, tile, drain, 0)


def _dispatch(dest_tiles, xn, *, p_rows, tiles_per_seq, lp_len, tile):
    n_tiles = dest_tiles.shape[0]
    kern = functools.partial(_dispatch_kernel, tiles_per_seq=tiles_per_seq, lp_len=lp_len, tile=tile)
    return pl.pallas_call(
        kern,
        grid=(n_tiles,),
        in_specs=[
            pl.BlockSpec((1, 1, TOP_K * tile), lambda t: (t, 0, 0), memory_space=pltpu.SMEM),
            pl.BlockSpec(memory_space=pl.ANY),
        ],
        out_specs=pl.BlockSpec(memory_space=pl.ANY),
        out_shape=jax.ShapeDtypeStruct((p_rows, xn.shape[1]), xn.dtype),
        scratch_shapes=[pltpu.VMEM((tile, xn.shape[1]), xn.dtype), pltpu.SemaphoreType.DMA(()),
                        pltpu.SemaphoreType.DMA(())],
        compiler_params=_cparams(1),
        name="moe_dispatch",
    )(dest_tiles, xn)


def _experts_kernel(be_ref, nv_ref, xs_ref, wg_ref, wu_ref, wd_ref, y_ref, wgb_ref, wub_ref, wdb_ref):
    i = pl.program_id(0)
    e = be_ref[i]
    prev = be_ref[jnp.maximum(i - 1, 0)]

    @pl.when((i == 0) | (e != prev))
    def _():
        wgb_ref[...] = wg_ref[0].astype(BF16)
        wub_ref[...] = wu_ref[0].astype(BF16)
        wdb_ref[...] = wd_ref[0].astype(BF16)

    nv = nv_ref[i]

    @pl.when(nv > 0)
    def _():
        xs = xs_ref[...]
        row = lax.broadcasted_iota(jnp.int32, xs.shape, 0)
        x = jnp.where(row < nv, xs, 0.0).astype(BF16)
        hg = jnp.dot(x, wgb_ref[...], preferred_element_type=F32)
        hu = jnp.dot(x, wub_ref[...], preferred_element_type=F32)
        hid = (hg / (1.0 + jnp.exp(-hg)) * hu).astype(BF16)
        y_ref[...] = jnp.dot(hid, wdb_ref[...], preferred_element_type=F32)

    @pl.when(nv == 0)
    def _():
        y_ref[...] = jnp.zeros_like(y_ref)


def _experts(block_e, nvalid, xs, w_gate, w_up, w_down):
    p_rows, d = xs.shape
    ff = w_gate.shape[2]
    n_blocks = p_rows // MOE_BLOCK
    grid_spec = pltpu.PrefetchScalarGridSpec(
        num_scalar_prefetch=2,
        grid=(n_blocks,),
        in_specs=[
            pl.BlockSpec((MOE_BLOCK, d), lambda i, be, nv: (i, 0)),
            pl.BlockSpec((1, d, ff), lambda i, be, nv: (be[i], 0, 0)),
            pl.BlockSpec((1, d, ff), lambda i, be, nv: (be[i], 0, 0)),
            pl.BlockSpec((1, ff, d), lambda i, be, nv: (be[i], 0, 0)),
        ],
        out_specs=pl.BlockSpec((MOE_BLOCK, d), lambda i, be, nv: (i, 0)),
        scratch_shapes=[
            pltpu.VMEM((d, ff), BF16),
            pltpu.VMEM((d, ff), BF16),
            pltpu.VMEM((ff, d), BF16),
        ],
    )
    return pl.pallas_call(
        _experts_kernel,
        grid_spec=grid_spec,
        out_shape=jax.ShapeDtypeStruct((p_rows, d), F32),
        compiler_params=_cparams(1),
        name="moe_experts",
    )(block_e, nvalid, xs, w_gate, w_up, w_down)


def _combine_kernel(dest_ref, gates_ref, h1_hbm, y_hbm, out_ref, hbuf, ya, yb, sem_h, sem_a, sem_b,
                    *, lp_len, tile):
    b = pl.program_id(0)
    i = pl.program_id(1)
    start = b * lp_len + N_META + i * tile
    h_copy = pltpu.make_async_copy(h1_hbm.at[pl.ds(start, tile), :], hbuf, sem_h)
    h_copy.start()

    def issue(r, carry):
        _row_copy(y_hbm, dest_ref[0, 0, TOP_K * r], ya, r, sem_a).start()
        _row_copy(y_hbm, dest_ref[0, 0, TOP_K * r + 1], yb, r, sem_b).start()
        return carry

    lax.fori_loop(0, tile, issue, 0)

    def drain(r, carry):
        _row_copy(y_hbm, dest_ref[0, 0, TOP_K * r], ya, r, sem_a).wait()
        _row_copy(y_hbm, dest_ref[0, 0, TOP_K * r + 1], yb, r, sem_b).wait()
        return carry

    lax.fori_loop(0, tile, drain, 0)
    h_copy.wait()
    g = gates_ref[0]
    out_ref[0] = hbuf[...] + g[:, 0:1] * ya[...] + g[:, 1:2] * yb[...]


def _combine(dest_tiles, gates, h1, y, *, batch, seq, lp_len, tile):
    d = h1.shape[1]
    nt = seq // tile
    kern = functools.partial(_combine_kernel, lp_len=lp_len, tile=tile)
    return pl.pallas_call(
        kern,
        grid=(batch, nt),
        in_specs=[
            pl.BlockSpec((1, 1, TOP_K * tile), lambda b, i: (b * nt + i, 0, 0), memory_space=pltpu.SMEM),
            pl.BlockSpec((1, tile, TOP_K), lambda b, i: (b, i, 0)),
            pl.BlockSpec(memory_space=pl.ANY),
            pl.BlockSpec(memory_space=pl.ANY),
        ],
        out_specs=pl.BlockSpec((1, tile, d), lambda b, i: (b, i, 0)),
        out_shape=jax.ShapeDtypeStruct((batch, seq, d), F32),
        scratch_shapes=[
            pltpu.VMEM((tile, d), F32),
            pltpu.VMEM((tile, d), F32),
            pltpu.VMEM((tile, d), F32),
            pltpu.SemaphoreType.DMA(()),
            pltpu.SemaphoreType.DMA(()),
            pltpu.SemaphoreType.DMA(()),
        ],
        compiler_params=_cparams(2),
        name="moe_combine",
    )(dest_tiles, gates, h1, y)


def _sc_workers():
    info = plsc.get_sparse_core_info()
    return info.num_cores, info.num_cores * info.num_subcores


def _sc_scatter_rows(planes, idx_a, idx_b, out_rows):
    n_win = idx_a.shape[0]
    n_cores, n_workers = _sc_workers()
    trips = -(-n_win // n_workers)
    mesh = plsc.VectorSubcoreMesh(core_axis_name="c", subcore_axis_name="s")

    def body(*refs):
        xs = refs[0:ROW_PLANES]
        ia_hbm, ib_hbm = refs[ROW_PLANES:ROW_PLANES + 2]
        outs = refs[ROW_PLANES + 2:2 * ROW_PLANES + 2]
        ia_v, ib_v, buf, sem = refs[2 * ROW_PLANES + 2:]
        wid = lax.axis_index("s") * n_cores + lax.axis_index("c")

        def step(t, carry):
            g = wid + t * n_workers

            @pl.when(g < n_win)
            def _():
                pltpu.sync_copy(ia_hbm.at[g], ia_v)
                pltpu.sync_copy(ib_hbm.at[g], ib_v)
                row0 = pl.multiple_of(g * SC_WINDOW, SC_WINDOW)
                loads = [pltpu.async_copy(xs[c].at[pl.ds(row0, SC_WINDOW)], buf.at[c], sem)
                         for c in range(ROW_PLANES)]
                for cp in loads:
                    cp.wait()
                stores = [pltpu.async_copy(buf.at[c], outs[c].at[iv], sem)
                          for c in range(ROW_PLANES) for iv in (ia_v, ib_v)]
                for cp in stores:
                    cp.wait()

            return carry

        lax.fori_loop(0, trips, step, 0)

    kern = pl.kernel(
        body,
        out_type=[jax.ShapeDtypeStruct((out_rows, LANES), jnp.uint32)] * ROW_PLANES,
        mesh=mesh,
        scratch_types=[
            pltpu.VMEM((SC_WINDOW,), jnp.int32),
            pltpu.VMEM((SC_WINDOW,), jnp.int32),
            pltpu.VMEM((ROW_PLANES, SC_WINDOW, LANES), jnp.uint32),
            pltpu.SemaphoreType.DMA,
        ],
        name="moe_dispatch_sc",
    )
    return kern(*planes, idx_a, idx_b)


def _sc_gather_rows(planes, idx_a, idx_b):
    n_win = idx_a.shape[0]
    n_cores, n_workers = _sc_workers()
    trips = -(-n_win // n_workers)
    mesh = plsc.VectorSubcoreMesh(core_axis_name="c", subcore_axis_name="s")

    def body(*refs):
        ys = refs[0:ROW_PLANES]
        ia_hbm, ib_hbm = refs[ROW_PLANES:ROW_PLANES + 2]
        outs_a = refs[ROW_PLANES + 2:2 * ROW_PLANES + 2]
        outs_b = refs[2 * ROW_PLANES + 2:3 * ROW_PLANES + 2]
        iv, buf, sem = refs[3 * ROW_PLANES + 2:]
        wid = lax.axis_index("s") * n_cores + lax.axis_index("c")

        def step(t, carry):
            g = wid + t * n_workers

            @pl.when(g < n_win)
            def _():
                row0 = pl.multiple_of(g * SC_WINDOW, SC_WINDOW)
                for i_hbm, outs in ((ia_hbm, outs_a), (ib_hbm, outs_b)):
                    pltpu.sync_copy(i_hbm.at[g], iv)
                    loads = [pltpu.async_copy(ys[c].at[iv], buf.at[c], sem) for c in range(ROW_PLANES)]
                    for cp in loads:
                        cp.wait()
                    stores = [pltpu.async_copy(buf.at[c], outs[c].at[pl.ds(row0, SC_WINDOW)], sem)
                              for c in range(ROW_PLANES)]
                    for cp in stores:
                        cp.wait()

            return carry

        lax.fori_loop(0, trips, step, 0)

    n_rows = n_win * SC_WINDOW
    kern = pl.kernel(
        body,
        out_type=[jax.ShapeDtypeStruct((n_rows, LANES), jnp.uint32)] * (2 * ROW_PLANES),
        mesh=mesh,
        scratch_types=[
            pltpu.VMEM((SC_WINDOW,), jnp.int32),
            pltpu.VMEM((ROW_PLANES, SC_WINDOW, LANES), jnp.uint32),
            pltpu.SemaphoreType.DMA,
        ],
        name="moe_gather_sc",
    )
    res = kern(*planes, idx_a, idx_b)
    return res[:ROW_PLANES], res[ROW_PLANES:]


def _experts_kernel(be_ref, nv_ref, x0_ref, x1_ref, x2_ref, x3_ref, wg_ref, wu_ref, wd_ref,
                    y0_ref, y1_ref, y2_ref, y3_ref, wgb_ref, wub_ref, wdb_ref):
    i = pl.program_id(0)
    e = be_ref[i]
    prev = be_ref[jnp.maximum(i - 1, 0)]
    y_refs = (y0_ref, y1_ref, y2_ref, y3_ref)

    @pl.when((i == 0) | (e != prev))
    def _():
        wgb_ref[...] = wg_ref[0].astype(BF16)
        wub_ref[...] = wu_ref[0].astype(BF16)
        wdb_ref[...] = wd_ref[0].astype(BF16)

    nv = nv_ref[i]

    @pl.when(nv > 0)
    def _():
        xs = _unpack_rows([x0_ref[...], x1_ref[...], x2_ref[...], x3_ref[...]])
        row = lax.broadcasted_iota(jnp.int32, xs.shape, 0)
        x = jnp.where(row < nv, xs, jnp.zeros_like(xs))
        hg = jnp.dot(x, wgb_ref[...], preferred_element_type=F32)
        hu = jnp.dot(x, wub_ref[...], preferred_element_type=F32)
        hid = (hg / (1.0 + jnp.exp(-hg)) * hu).astype(BF16)
        yw = _pack_rows(jnp.dot(hid, wdb_ref[...], preferred_element_type=F32))
        for c, ref in enumerate(y_refs):
            ref[...] = yw[:, c * LANES:(c + 1) * LANES]

    @pl.when(nv == 0)
    def _():
        for ref in y_refs:
            ref[...] = jnp.zeros_like(ref)


def _experts(block_e, nvalid, xs_planes, w_gate, w_up, w_down):
    n_blocks = block_e.shape[0]
    _, d, ff = w_gate.shape
    blk = lambda i, be, nv: (i, 0)
    wsel = lambda i, be, nv: (be[i], 0, 0)
    grid_spec = pltpu.PrefetchScalarGridSpec(
        num_scalar_prefetch=2,
        grid=(n_blocks,),
        in_specs=[
            *[pl.BlockSpec((MOE_BLOCK, LANES), blk)] * ROW_PLANES,
            pl.BlockSpec((1, d, ff), wsel),
            pl.BlockSpec((1, d, ff), wsel),
            pl.BlockSpec((1, ff, d), wsel),
        ],
        out_specs=[pl.BlockSpec((MOE_BLOCK, LANES), blk)] * ROW_PLANES,
        scratch_shapes=[
            pltpu.VMEM((d, ff), BF16),
            pltpu.VMEM((d, ff), BF16),
            pltpu.VMEM((ff, d), BF16),
        ],
    )
    return pl.pallas_call(
        _experts_kernel,
        grid_spec=grid_spec,
        out_shape=[jax.ShapeDtypeStruct((n_blocks * MOE_BLOCK, LANES), jnp.uint32)] * ROW_PLANES,
        compiler_params=_cparams(1),
        name="moe_experts",
    )(block_e, nvalid, *xs_planes, w_gate, w_up, w_down)


def _combine_kernel(gates_ref, h1_hbm, *refs, lp_len, tile):
    a_refs = refs[0:ROW_PLANES]
    b_refs = refs[ROW_PLANES:2 * ROW_PLANES]
    out_ref, hbuf, sem_h = refs[2 * ROW_PLANES:]
    nt = pl.num_programs(1)
    step = pl.program_id(0) * nt + pl.program_id(1)
    last = pl.num_programs(0) * nt - 1

    def h_copy(s, slot):
        start = (s // nt) * lp_len + N_META + (s % nt) * tile
        return pltpu.make_async_copy(h1_hbm.at[pl.ds(start, tile), :], hbuf.at[slot], sem_h.at[slot])

    slot = step % 2

    @pl.when(step == 0)
    def _():
        h_copy(step, slot).start()

    @pl.when(step < last)
    def _():
        h_copy(step + 1, 1 - slot).start()

    ya = _unpack_rows([r[...] for r in a_refs]).astype(F32)
    yb = _unpack_rows([r[...] for r in b_refs]).astype(F32)
    g = jnp.concatenate([gates_ref[...]] * (LANES // ROUTE_COLS), axis=0).T
    moe = g[:, 0:1] * ya + g[:, 1:2] * yb
    h_copy(step, slot).wait()
    out_ref[0] = hbuf[slot] + moe


def _combine(gates, h1, a_planes, b_planes, *, batch, seq, lp_len, tile):
    d = h1.shape[1]
    nt = seq // tile
    kern = functools.partial(_combine_kernel, lp_len=lp_len, tile=tile)
    rows = lambda b, i: (b * nt + i, 0)
    return pl.pallas_call(
        kern,
        grid=(batch, nt),
        in_specs=[
            pl.BlockSpec((ROUTE_COLS, tile), lambda b, i: (0, b * nt + i)),
            pl.BlockSpec(memory_space=pl.ANY),
            *[pl.BlockSpec((tile, LANES), rows)] * (2 * ROW_PLANES),
        ],
        out_specs=pl.BlockSpec((1, tile, d), lambda b, i: (b, i, 0)),
        out_shape=jax.ShapeDtypeStruct((batch, seq, d), F32),
        scratch_shapes=[pltpu.VMEM((2, tile, d), F32), pltpu.SemaphoreType.DMA((2,))],
        compiler_params=_cparams(2),
        name="moe_combine",
    )(gates, h1, *a_planes, *b_planes)


def _rope_tables(length, lp_len):
    half = ROPE_DIM // 2
    pos = jnp.arange(length, dtype=F32)
    inv_freq = ROPE_THETA ** (-jnp.arange(0, ROPE_DIM, 2, dtype=F32) / ROPE_DIM)
    ang = pos[:, None] * inv_freq[None, :]
    cos = jnp.cos(ang)
    sin = jnp.sin(ang)
    ones = jnp.ones((length, HEAD_DIM - ROPE_DIM), F32)
    zeros_h = jnp.zeros((length, half), F32)
    zeros_r = jnp.zeros((length, HEAD_DIM - ROPE_DIM), F32)
    c = jnp.concatenate([cos, cos, ones], axis=1)
    s1 = jnp.concatenate([zeros_h, sin, zeros_r], axis=1)
    s2 = jnp.concatenate([-sin, zeros_h, zeros_r], axis=1)
    pad = ((0, lp_len - length), (0, 0))
    rep = LANES // HEAD_DIM
    return tuple(jnp.pad(jnp.tile(t, (1, rep)), pad) for t in (c, s1, s2))


def _layer(hp, l, batch, length, lp_len, tm, norm1_g, w_in, conv_w, q_norm_g, k_norm_g, lambda_q1, lambda_k1,
           lambda_q2, lambda_k2, subln_g, w_out, norm2_g, w_router_group, b_router_group, w_router_expert,
           b_router_expert, w_gate, w_up, w_down, rope, last):
    n, d = hp.shape
    tiles_per_seq = lp_len // tm
    cw = conv_w.shape[2]
    qw = N_HEADS * 2 * HEAD_DIM
    lam_init = 0.8 - 0.6 * math.exp(-0.3 * l)

    reps = qw // HEAD_DIM
    gq = jnp.tile(q_norm_g[l] * (HEAD_DIM ** -0.5 * LOG2E), reps)[None, :]
    gk = jnp.tile(k_norm_g[l], reps)[None, :]
    seg = jnp.arange(qw) // HEAD_DIM
    bd = (seg[:, None] == seg[None, :]).astype(BF16)
    convy, q, k, v = _inproj(hp, norm1_g[l][None, :], w_in[l].astype(BF16), conv_w[l], gq, gk, bd, *rope,
                             tiles_per_seq=tiles_per_seq, tm=tm)

    lamp = jnp.stack([lambda_q1[l], lambda_k1[l], lambda_q2[l], lambda_k2[l]]).astype(F32)
    o = _attention(q, k, v, lamp, subln_g[l][None, :], batch=batch, lp_len=lp_len, tq=tm, lam_init=lam_init)

    lane_pad = LANES - N_GROUPS - N_EXPERTS
    wr = jnp.pad(jnp.concatenate([w_router_group[l], w_router_expert[l]], axis=1), ((0, 0), (0, lane_pad)))
    wr_hi = wr.astype(BF16)
    wr_lo = (wr - wr_hi.astype(F32)).astype(BF16)
    br = jnp.pad(jnp.concatenate([b_router_group[l], b_router_expert[l]]), (0, lane_pad))[None, :]
    ridx = jnp.arange(tm)
    upper = (ridx[:, None] < ridx[None, :]).astype(BF16)
    h1, *rest = _outproj_t(hp, convy, o, w_out[l].astype(BF16), norm2_g[l][None, :], wr_hi, wr_lo, br,
                           upper, tiles_per_seq=tiles_per_seq, seq_len=length, tm=tm)
    x_planes = rest[:ROW_PLANES]
    route, cnt = rest[ROW_PLANES:]

    route = route.reshape(ROUTE_COLS, batch, lp_len)
    eid = route[0:TOP_K].astype(jnp.int32)
    gates = route[TOP_K:2 * TOP_K]
    rank = route[2 * TOP_K:3 * TOP_K].astype(jnp.int32)
    counts = cnt[EXPERT_LANE0:EXPERT_LANE0 + N_EXPERTS, 0].astype(jnp.int32)
    a = batch * length * TOP_K
    n_blocks = -(-a // MOE_BLOCK) + N_EXPERTS
    p_rows = n_blocks * MOE_BLOCK
    padded = (counts + MOE_BLOCK - 1) // MOE_BLOCK * MOE_BLOCK
    pends = jnp.cumsum(padded)
    pstarts = pends - padded

    def lookup(table, idx):
        return sum(jnp.where(idx == e, table[e], 0) for e in range(N_EXPERTS))

    n_pad = lp_len - length
    pos = jnp.arange(lp_len, dtype=jnp.int32)[None, None, :]
    spare = p_rows + ((jnp.arange(batch, dtype=jnp.int32)[None, :, None] * n_pad + (pos - length)) * TOP_K
                      + jnp.arange(TOP_K, dtype=jnp.int32)[:, None, None])
    dest = jnp.where(pos < length, lookup(pstarts, eid) + rank, spare)
    spare_rows = -(-(batch * n_pad * TOP_K) // MOE_BLOCK) * MOE_BLOCK
    blk0 = jnp.arange(n_blocks, dtype=jnp.int32) * MOE_BLOCK
    block_e = jnp.minimum(jnp.sum((pends[None, :] <= blk0[:, None]).astype(jnp.int32), axis=1), N_EXPERTS - 1)
    nvalid = jnp.clip(lookup(counts, block_e) - (blk0 - lookup(pstarts, block_e)), 0, MOE_BLOCK)

    assert (batch * lp_len) % SC_WINDOW == 0
    xs_planes = _sc_scatter_rows(x_planes, dest[0].reshape(-1, SC_WINDOW), dest[1].reshape(-1, SC_WINDOW),
                                 p_rows + spare_rows)
    y_planes = _experts(block_e, nvalid, xs_planes, w_gate[l], w_up[l], w_down[l])

    if not last:
        raise NotImplementedError("only the final layer's combine (which drops the meta tokens) is implemented")
    seq = length - N_META
    assert (batch * seq) % SC_WINDOW == 0
    dest_x = dest[:, :, N_META:length]
    a_planes, b_planes = _sc_gather_rows(y_planes, dest_x[0].reshape(-1, SC_WINDOW),
                                         dest_x[1].reshape(-1, SC_WINDOW))
    ctile = _largest_tile(seq, 512, LANES)
    gates_x = jnp.pad(gates[:, :, N_META:length].reshape(TOP_K, batch * seq), ((0, ROUTE_COLS - TOP_K), (0, 0)))
    return _combine(gates_x, h1, a_planes, b_planes, batch=batch, seq=seq, lp_len=lp_len, tile=ctile)


def kernel(x, meta_tokens, norm1_g, w_in, conv_w, q_norm_g, k_norm_g, lambda_q1, lambda_k1, lambda_q2, lambda_k2,
           subln_g, w_out, norm2_g, w_router_group, b_router_group, w_router_expert, b_router_expert, w_gate,
           w_up, w_down):
    b, s, d = x.shape
    depth = w_in.shape[0]
    assert depth == 1, "a single layer is supported"
    length = s + N_META
    tm = TOKEN_TILE if length >= TOKEN_TILE else LANES
    lp_len = -(-length // tm) * tm
    hp = jnp.concatenate([jnp.broadcast_to(meta_tokens[None].astype(x.dtype), (b, N_META, d)), x,
                          jnp.zeros((b, lp_len - length, d), x.dtype)], axis=1).reshape(b * lp_len, d)
    rope = _rope_tables(length, lp_len)
    return _layer(hp, 0, b, length, lp_len, tm, norm1_g, w_in, conv_w, q_norm_g, k_norm_g, lambda_q1, lambda_k1,
                  lambda_q2, lambda_k2, subln_g, w_out, norm2_g, w_router_group, b_router_group,
                  w_router_expert, b_router_expert, w_gate, w_up, w_down, rope, last=True)
```

```python
import functools
import math

import jax
import jax.numpy as jnp
from jax import lax
from jax.experimental import pallas as pl
from jax.experimental.pallas import tpu as pltpu
from jax.experimental.pallas import tpu_sc as plsc

F32 = jnp.float32
BF16 = jnp.bfloat16

N_META = 16
CONV_K = 3
N_HEADS = 4
HEAD_DIM = 64
V_DIM = 2 * HEAD_DIM
ROPE_DIM = HEAD_DIM // 4
ROPE_THETA = 500000.0
N_GROUPS = 4
EXPERTS_PER_GROUP = 8
N_EXPERTS = N_GROUPS * EXPERTS_PER_GROUP
TOP_K = 2
EPS = 1e-6
LOG2E = 1.4426950408889634

LANES = 128
TOKEN_TILE = 640
OUTPROJ_CHAINS = 2
ATTN_HEADS_PER_STEP = 4
ATTN_WIDE_CHUNKS = 2
MOE_BLOCK = 512
ROUTE_COLS = 8
ROUTE_ROWS = 64
ROW_PLANES = 4
SC_WINDOW = 128
EXPERT_LANE0 = N_GROUPS
NEG_BIG = -1e30
VMEM_LIMIT = 56 * 1024 * 1024


def _largest_tile(n, cap, mult):
    for t in range(min(cap, n), 0, -1):
        if n % t == 0 and t % mult == 0:
            return t
    raise ValueError(f"no tile for {n}")


def _cparams(n_axes, flags=None):
    return pltpu.CompilerParams(dimension_semantics=("arbitrary",) * n_axes,
                                vmem_limit_bytes=VMEM_LIMIT, flags=flags)


def _inproj_kernel(x_ref, g1_ref, win_ref, convw_ref, gq_ref, gk_ref, bd_ref, rc_ref, rs1_ref, rs2_ref,
                   convy_ref, q_ref, k_ref, v_ref, carry_ref, *, tiles_per_seq, cw, qw):
    i = pl.program_id(0)
    tm = x_ref.shape[0]
    x = x_ref[...]
    ms = jnp.mean(x * x, axis=-1, keepdims=True)
    xn = (x * lax.rsqrt(ms + EPS) * g1_ref[...]).astype(BF16)
    q0 = 3 * cw

    def proj(lo, hi):
        return jnp.dot(xn, win_ref[:, lo:hi], preferred_element_type=F32)

    u_conv = proj(0, q0)
    u_q = proj(q0, q0 + qw)

    z = u_conv[:, cw:2 * cw] * u_conv[:, 2 * cw:3 * cw]

    @pl.when(i % tiles_per_seq == 0)
    def _():
        carry_ref[...] = jnp.zeros_like(carry_ref)

    prev = carry_ref[...]
    p1 = prev[7:8]
    p2 = prev[6:7]
    row = lax.broadcasted_iota(jnp.int32, z.shape, 0)
    z1 = jnp.where(row == 0, p1, pltpu.roll(z, 1, axis=0))
    z2 = jnp.where(row == 0, p2, jnp.where(row == 1, p1, pltpu.roll(z, 2, axis=0)))
    carry_ref[...] = z[tm - 8:tm]
    w = convw_ref[...]
    conv = w[0:1] * z2 + w[1:2] * z1 + w[2:3] * z
    convy_ref[...] = (u_conv[:, 0:cw] * conv).astype(BF16)

    rc = rc_ref[...]
    rs1 = rs1_ref[...]
    rs2 = rs2_ref[...]

    def norm_rope(t, g_ref):
        ss = jnp.dot((t * t).astype(BF16), bd_ref[...], preferred_element_type=F32)
        tn = t * lax.rsqrt(ss * (1.0 / HEAD_DIM) + EPS) * g_ref[...]
        outs = []
        for c in range(qw // LANES):
            ch = tn[:, c * LANES:(c + 1) * LANES]
            outs.append(ch * rc + pltpu.roll(ch, ROPE_DIM // 2, axis=1) * rs1
                        + pltpu.roll(ch, LANES - ROPE_DIM // 2, axis=1) * rs2)
        return jnp.concatenate(outs, axis=1).astype(BF16)

    u_k = proj(q0 + qw, q0 + 2 * qw)
    q_ref[...] = norm_rope(u_q, gq_ref)
    u_v = proj(q0 + 2 * qw, win_ref.shape[1])
    k_ref[...] = norm_rope(u_k, gk_ref)
    v_ref[...] = u_v.astype(BF16)


def _inproj(hp, g1, w_in, conv_w, gq, gk, bd, rc, rs1, rs2, *, tiles_per_seq, tm):
    n, d = hp.shape
    cw = conv_w.shape[1]
    qw = gq.shape[1]
    aw = w_in.shape[1] - 3 * cw - 2 * qw
    const = lambda i: (0, 0)
    tile = lambda i: (i, 0)
    pos = lambda i: (i % tiles_per_seq, 0)
    kern = functools.partial(_inproj_kernel, tiles_per_seq=tiles_per_seq, cw=cw, qw=qw)
    return pl.pallas_call(
        kern,
        grid=(n // tm,),
        in_specs=[
            pl.BlockSpec((tm, d), tile),
            pl.BlockSpec((1, d), const),
            pl.BlockSpec(w_in.shape, const),
            pl.BlockSpec(conv_w.shape, const),
            pl.BlockSpec((1, qw), const),
            pl.BlockSpec((1, qw), const),
            pl.BlockSpec(bd.shape, const),
            pl.BlockSpec((tm, LANES), pos),
            pl.BlockSpec((tm, LANES), pos),
            pl.BlockSpec((tm, LANES), pos),
        ],
        out_specs=[
            pl.BlockSpec((tm, cw), tile),
            pl.BlockSpec((tm, qw), tile),
            pl.BlockSpec((tm, qw), tile),
            pl.BlockSpec((tm, aw), tile),
        ],
        out_shape=[
            jax.ShapeDtypeStruct((n, cw), BF16),
            jax.ShapeDtypeStruct((n, qw), BF16),
            jax.ShapeDtypeStruct((n, qw), BF16),
            jax.ShapeDtypeStruct((n, aw), BF16),
        ],
        scratch_shapes=[pltpu.VMEM((8, cw), F32)],
        compiler_params=_cparams(1),
        name="inproj",
    )(hp, g1, w_in, conv_w, gq, gk, bd, rc, rs1, rs2)


def _attn_kernel(q_ref, k_ref, v_ref, lamp_ref, sg_ref, o_ref, qs_ref, m_ref, l_ref, acc_ref, *, lam_init):
    qi = pl.program_id(2)
    tq = q_ref.shape[0]
    n_heads = q_ref.shape[1] // LANES
    n_chains = 2 * n_heads
    lane = lax.broadcasted_iota(jnp.int32, (tq, LANES), 1)
    for h in range(n_heads):
        q = q_ref[:, h * LANES:(h + 1) * LANES]
        zero = jnp.zeros_like(q)
        qs_ref[pl.ds(2 * h * tq, tq), :] = jnp.where(lane < HEAD_DIM, q, zero)
        qs_ref[pl.ds((2 * h + 1) * tq, tq), :] = jnp.where(lane >= HEAD_DIM, q, zero)
    m_ref[...] = jnp.full_like(m_ref, NEG_BIG)
    l_ref[...] = jnp.zeros_like(l_ref)
    acc_ref[...] = jnp.zeros_like(acc_ref)

    def scores(off, width, which):
        h = which // 2
        kc = k_ref[pl.ds(off, width), h * LANES:(h + 1) * LANES]
        return lax.dot_general(qs_ref[pl.ds(which * tq, tq), :], kc, (((1,), (1,)), ((), ())),
                               preferred_element_type=F32)

    def update(off, width, which, s, masked):
        h = which // 2
        vc = jnp.concatenate([v_ref[pl.ds(off, width), h * LANES:(h + 1) * LANES],
                              jnp.ones((width, LANES), BF16)], axis=1)
        rows = pl.ds(which * tq, tq)
        if masked:
            r = lax.broadcasted_iota(jnp.int32, s.shape, 0)
            c = lax.broadcasted_iota(jnp.int32, s.shape, 1)
            s = jnp.where(c <= r, s, NEG_BIG)
        m_prev = m_ref[rows, :]
        m_new = jnp.maximum(m_prev, jnp.max(s, axis=-1, keepdims=True))
        alpha = jnp.exp2(m_prev - m_new)
        p = jnp.exp2((s - jnp.tile(m_new, (1, width // LANES))).astype(BF16))
        pv = jnp.dot(p, vc, preferred_element_type=F32)
        l_ref[rows, :] = alpha * l_ref[rows, :] + pv[:, LANES:]
        acc_ref[rows, :] = alpha * acc_ref[rows, :] + pv[:, :LANES]
        m_ref[rows, :] = m_new

    def chunk(off, width, masked):
        s_next = scores(off, width, 0)
        for c in range(n_chains):
            s = s_next
            if c + 1 < n_chains:
                s_next = scores(off, width, c + 1)
            update(off, width, c, s, masked)

    wide = ATTN_WIDE_CHUNKS * tq

    def body(j, carry):
        chunk(pl.multiple_of(j * wide, wide), wide, False)
        return carry

    n_wide = qi // ATTN_WIDE_CHUNKS
    lax.fori_loop(0, n_wide, body, 0)
    for extra in range(ATTN_WIDE_CHUNKS - 1):
        @pl.when(n_wide * ATTN_WIDE_CHUNKS + extra < qi)
        def _():
            chunk(pl.multiple_of((n_wide * ATTN_WIDE_CHUNKS + extra) * tq, tq), tq, False)

    chunk(pl.multiple_of(qi * tq, tq), tq, True)

    lp = lamp_ref[...]
    lam = (jnp.exp(jnp.sum(lp[0:1] * lp[1:2], axis=-1, keepdims=True))
           - jnp.exp(jnp.sum(lp[2:3] * lp[3:4], axis=-1, keepdims=True)) + lam_init)
    for h in range(n_heads):
        rows = pl.ds(2 * h * tq, 2 * tq)
        o_all = acc_ref[rows, :] / l_ref[rows, :]
        o = o_all[0:tq] - lam * o_all[tq:2 * tq]
        ms = jnp.mean(o * o, axis=-1, keepdims=True)
        o_ref[:, h * LANES:(h + 1) * LANES] = (o * lax.rsqrt(ms + EPS) * sg_ref[...]
                                               * (1.0 - lam_init)).astype(BF16)


def _attention(q, k, v, lamp, sg, *, batch, lp_len, tq, lam_init):
    n, qw = q.shape
    nq = lp_len // tq
    hw = ATTN_HEADS_PER_STEP * LANES
    chains = 2 * ATTN_HEADS_PER_STEP
    kern = functools.partial(_attn_kernel, lam_init=lam_init)
    return pl.pallas_call(
        kern,
        grid=(batch, qw // hw, nq),
        in_specs=[
            pl.BlockSpec((tq, hw), lambda b, h, i: (b * nq + i, h)),
            pl.BlockSpec((lp_len, hw), lambda b, h, i: (b, h)),
            pl.BlockSpec((lp_len, hw), lambda b, h, i: (b, h)),
            pl.BlockSpec(lamp.shape, lambda b, h, i: (0, 0)),
            pl.BlockSpec(sg.shape, lambda b, h, i: (0, 0)),
        ],
        out_specs=pl.BlockSpec((tq, hw), lambda b, h, i: (b * nq + i, h)),
        out_shape=jax.ShapeDtypeStruct((n, v.shape[1]), BF16),
        scratch_shapes=[
            pltpu.VMEM((chains * tq, LANES), BF16),
            pltpu.VMEM((chains * tq, LANES), F32),
            pltpu.VMEM((chains * tq, LANES), F32),
            pltpu.VMEM((chains * tq, LANES), F32),
        ],
        compiler_params=_cparams(3),
        name="diffattn",
    )(q, k, v, lamp, sg)


def _pack_rows(x):
    w = x.shape[1] // 2
    lo = lax.bitcast_convert_type(x[:, :w].astype(BF16).astype(F32), jnp.uint32)
    hi = lax.bitcast_convert_type(x[:, w:].astype(BF16).astype(F32), jnp.uint32)
    return lax.shift_right_logical(lo, jnp.uint32(16)) | (hi & jnp.uint32(0xFFFF0000))


def _unpack_rows(planes):
    w = jnp.concatenate(planes, axis=1)
    lo = lax.bitcast_convert_type(lax.shift_left(w, jnp.uint32(16)), F32)
    hi = lax.bitcast_convert_type(w & jnp.uint32(0xFFFF0000), F32)
    return jnp.concatenate([lo, hi], axis=1).astype(BF16)


def _outproj_kernel(hp_ref, cy_ref, o_ref, wout_ref, g2_ref, wrh_ref, wrl_ref, br_ref, tri_ref,
                    h1_ref, xp0_ref, xp1_ref, xp2_ref, xp3_ref, route_ref, cnt_ref, run_ref,
                    *, tiles_per_seq, seq_len):
    i = pl.program_id(0)
    tm = hp_ref.shape[0]
    sub = tri_ref.shape[0]
    wr_both = jnp.concatenate([wrh_ref[...], wrl_ref[...]], axis=1)

    @pl.when(i == 0)
    def _():
        run_ref[...] = jnp.zeros_like(run_ref)

    def project(rows):
        mix = jnp.concatenate([cy_ref[rows, :], o_ref[rows, :]], axis=1)
        h1 = hp_ref[rows, :] + jnp.dot(mix, wout_ref[...], preferred_element_type=F32)
        h1_ref[rows, :] = h1
        ms = jnp.mean(h1 * h1, axis=-1, keepdims=True)
        xn = h1 * lax.rsqrt(ms + EPS) * g2_ref[...]
        xw = _pack_rows(xn)
        for c, ref in enumerate((xp0_ref, xp1_ref, xp2_ref, xp3_ref)):
            ref[rows, :] = xw[:, c * LANES:(c + 1) * LANES]
        x_hi = xn.astype(BF16)
        x_lo = (xn - x_hi.astype(F32)).astype(BF16)
        hi_both = jnp.dot(x_hi, wr_both, preferred_element_type=F32)
        return (hi_both[:, :LANES] + hi_both[:, LANES:]
                + jnp.dot(x_lo, wrh_ref[...], preferred_element_type=F32) + br_ref[...])

    chains = [pl.ds(c * sub, sub) for c in range(tm // sub)]
    all_logits = [project(rows) for rows in chains]
    run = run_ref[0:1, :]
    for c, (rows, logits) in enumerate(zip(chains, all_logits)):
        run = _route_rows(i, c, rows, logits, run, tri_ref, route_ref, tiles_per_seq=tiles_per_seq,
                          seq_len=seq_len, tm=tm)
    run_ref[...] = jnp.broadcast_to(run, run_ref.shape)
    cnt_ref[...] = jnp.broadcast_to(run, cnt_ref.shape)


def _route_rows(i, c, rows, logits, run, tri_ref, route_ref, *, tiles_per_seq, seq_len, tm):
    sub = logits.shape[0]
    lane = lax.broadcasted_iota(jnp.int32, logits.shape, 1)
    big = jnp.int32(4 * LANES)

    def first_argmax(vals, vmax):
        return jnp.min(jnp.where(vals == vmax, lane, big), axis=-1, keepdims=True)

    gl = jnp.where(lane < N_GROUPS, logits, NEG_BIG)
    gmax = jnp.max(gl, axis=-1, keepdims=True)
    g_val = 1.0 / jnp.sum(jnp.exp(gl - gmax), axis=-1, keepdims=True)
    g_idx = first_argmax(gl, gmax)
    lo = EXPERT_LANE0 + EXPERTS_PER_GROUP * g_idx
    el = jnp.where((lane >= lo) & (lane < lo + EXPERTS_PER_GROUP), logits, NEG_BIG)
    m1 = jnp.max(el, axis=-1, keepdims=True)
    i1 = first_argmax(el, m1)
    el2 = jnp.where(lane == i1, NEG_BIG, el)
    m2 = jnp.max(el2, axis=-1, keepdims=True)
    i2 = first_argmax(el2, m2)
    r = jnp.exp(m2 - m1)
    gate1 = g_val / (1.0 + r)
    gate2 = g_val * r / (1.0 + r)

    prow = (i % tiles_per_seq) * tm + c * sub + lax.broadcasted_iota(jnp.int32, logits.shape, 0)
    valid = prow < seq_len
    oh1 = jnp.where(valid & (lane == i1), 1.0, 0.0)
    oh2 = jnp.where(valid & (lane == i2), 1.0, 0.0)
    pre = jnp.dot(tri_ref[...], jnp.concatenate([oh1, oh2], axis=1).astype(BF16), preferred_element_type=F32)
    pre1 = pre[:, :LANES]
    pre2 = pre[:, LANES:]
    tot1 = jnp.sum(oh1, axis=0, keepdims=True)
    tot2 = jnp.sum(oh2, axis=0, keepdims=True)
    rank1 = jnp.sum(oh1 * (pre1 + run), axis=-1, keepdims=True)
    rank2 = jnp.sum(oh2 * (pre2 + run + tot1), axis=-1, keepdims=True)

    e1 = (i1 - EXPERT_LANE0).astype(F32)
    e2 = (i2 - EXPERT_LANE0).astype(F32)
    packed = jnp.where(lane == 0, e1, jnp.where(lane == 1, e2, jnp.where(lane == 2, gate1, jnp.where(
        lane == 3, gate2, jnp.where(lane == 4, rank1, jnp.where(lane == 5, rank2, 0.0))))))
    route_ref[rows, :] = packed[:, 0:ROUTE_COLS]
    return run + tot1 + tot2


def _outproj(hp, convy, o, w_out, g2, wr_hi, wr_lo, br, tri, *, tiles_per_seq, seq_len, tm):
    n, d = hp.shape
    const = lambda i: (0, 0)
    tile = lambda i: (i, 0)
    kern = functools.partial(_outproj_kernel, tiles_per_seq=tiles_per_seq, seq_len=seq_len)
    return pl.pallas_call(
        kern,
        grid=(n // tm,),
        in_specs=[
            pl.BlockSpec((tm, d), tile),
            pl.BlockSpec((tm, convy.shape[1]), tile),
            pl.BlockSpec((tm, o.shape[1]), tile),
            pl.BlockSpec(w_out.shape, const),
            pl.BlockSpec((1, d), const),
            pl.BlockSpec(wr_hi.shape, const),
            pl.BlockSpec(wr_lo.shape, const),
            pl.BlockSpec((1, LANES), const),
            pl.BlockSpec(tri.shape, const),
        ],
        out_specs=[
            pl.BlockSpec((tm, d), tile),
            *[pl.BlockSpec((tm, LANES), tile)] * ROW_PLANES,
            pl.BlockSpec((tm, ROUTE_COLS), tile),
            pl.BlockSpec((8, LANES), const),
        ],
        out_shape=[
            jax.ShapeDtypeStruct((n, d), F32),
            *[jax.ShapeDtypeStruct((n, LANES), jnp.uint32)] * ROW_PLANES,
            jax.ShapeDtypeStruct((n, ROUTE_COLS), F32),
            jax.ShapeDtypeStruct((8, LANES), F32),
        ],
        scratch_shapes=[pltpu.VMEM((8, LANES), F32)],
        compiler_params=_cparams(1),
        name="outproj_router",
    )(hp, convy, o, w_out, g2, wr_hi, wr_lo, br, tri)


def _outproj_t_kernel(hp_ref, cy_ref, o_ref, wout_ref, g2_ref, wrh_ref, wrl_ref, br_ref, upper_ref,
                      h1_ref, xp0_ref, xp1_ref, xp2_ref, xp3_ref, route_ref, cnt_ref, run_ref,
                      *, tiles_per_seq, seq_len):
    i = pl.program_id(0)
    tm = hp_ref.shape[0]

    @pl.when(i == 0)
    def _():
        run_ref[...] = jnp.zeros_like(run_ref)

    mix = jnp.concatenate([cy_ref[...], o_ref[...]], axis=1)
    h1 = hp_ref[...] + jnp.dot(mix, wout_ref[...], preferred_element_type=F32)
    h1_ref[...] = h1
    ms = jnp.mean(h1 * h1, axis=-1, keepdims=True)
    xn = h1 * lax.rsqrt(ms + EPS) * g2_ref[...]
    xw = _pack_rows(xn)
    for c, ref in enumerate((xp0_ref, xp1_ref, xp2_ref, xp3_ref)):
        ref[...] = xw[:, c * LANES:(c + 1) * LANES]

    x_hi = xn.astype(BF16)
    x_lo = (xn - x_hi.astype(F32)).astype(BF16)
    hi_both = jnp.dot(x_hi, jnp.concatenate([wrh_ref[...], wrl_ref[...]], axis=1), preferred_element_type=F32)
    logits = (hi_both[:, :LANES] + hi_both[:, LANES:]
              + jnp.dot(x_lo, wrh_ref[...], preferred_element_type=F32) + br_ref[...])

    lt = logits.T[0:ROUTE_ROWS, :]
    row = lax.broadcasted_iota(jnp.int32, lt.shape, 0)
    big = jnp.int32(4 * LANES)

    def first_argmax(vals, vmax):
        return jnp.min(jnp.where(vals == vmax, row, big), axis=0, keepdims=True)

    gl = jnp.where(row < N_GROUPS, lt, NEG_BIG)
    gmax = jnp.max(gl, axis=0, keepdims=True)
    g_val = 1.0 / jnp.sum(jnp.exp(gl - gmax), axis=0, keepdims=True)
    g_idx = first_argmax(gl, gmax)
    lo = EXPERT_LANE0 + EXPERTS_PER_GROUP * g_idx
    el = jnp.where((row >= lo) & (row < lo + EXPERTS_PER_GROUP), lt, NEG_BIG)
    m1 = jnp.max(el, axis=0, keepdims=True)
    i1 = first_argmax(el, m1)
    el2 = jnp.where(row == i1, NEG_BIG, el)
    m2 = jnp.max(el2, axis=0, keepdims=True)
    i2 = first_argmax(el2, m2)
    r = jnp.exp(m2 - m1)
    gate1 = g_val / (1.0 + r)
    gate2 = g_val * r / (1.0 + r)

    pos = (i % tiles_per_seq) * tm + lax.broadcasted_iota(jnp.int32, (1, tm), 1)
    valid = pos < seq_len
    oh1 = jnp.where(valid & (row == i1), 1.0, 0.0)
    oh2 = jnp.where(valid & (row == i2), 1.0, 0.0)
    pre = jnp.dot(jnp.concatenate([oh1, oh2], axis=0).astype(BF16), upper_ref[...], preferred_element_type=F32)
    tot1 = jnp.sum(oh1, axis=1, keepdims=True)
    tot2 = jnp.sum(oh2, axis=1, keepdims=True)
    run = run_ref[...]
    run_t = jnp.tile(run, (1, tm // LANES))
    rank1 = jnp.sum(oh1 * (pre[:ROUTE_ROWS] + run_t), axis=0, keepdims=True)
    rank2 = jnp.sum(oh2 * (pre[ROUTE_ROWS:] + run_t + tot1), axis=0, keepdims=True)
    new_run = run + tot1 + tot2
    run_ref[...] = new_run
    cnt_ref[...] = new_run

    e1 = (i1 - EXPERT_LANE0).astype(F32)
    e2 = (i2 - EXPERT_LANE0).astype(F32)
    r8 = lax.broadcasted_iota(jnp.int32, (ROUTE_COLS, tm), 0)
    route_ref[...] = jnp.where(r8 == 0, e1, jnp.where(r8 == 1, e2, jnp.where(r8 == 2, gate1, jnp.where(
        r8 == 3, gate2, jnp.where(r8 == 4, rank1, jnp.where(r8 == 5, rank2, 0.0))))))


def _outproj_t(hp, convy, o, w_out, g2, wr_hi, wr_lo, br, upper, *, tiles_per_seq, seq_len, tm):
    n, d = hp.shape
    const = lambda i: (0, 0)
    tile = lambda i: (i, 0)
    kern = functools.partial(_outproj_t_kernel, tiles_per_seq=tiles_per_seq, seq_len=seq_len)
    return pl.pallas_call(
        kern,
        grid=(n // tm,),
        in_specs=[
            pl.BlockSpec((tm, d), tile),
            pl.BlockSpec((tm, convy.shape[1]), tile),
            pl.BlockSpec((tm, o.shape[1]), tile),
            pl.BlockSpec(w_out.shape, const),
            pl.BlockSpec((1, d), const),
            pl.BlockSpec(wr_hi.shape, const),
            pl.BlockSpec(wr_lo.shape, const),
            pl.BlockSpec((1, LANES), const),
            pl.BlockSpec(upper.shape, const),
        ],
        out_specs=[
            pl.BlockSpec((tm, d), tile),
            *[pl.BlockSpec((tm, LANES), tile)] * ROW_PLANES,
            pl.BlockSpec((ROUTE_COLS, tm), lambda i: (0, i)),
            pl.BlockSpec((ROUTE_ROWS, LANES), const),
        ],
        out_shape=[
            jax.ShapeDtypeStruct((n, d), F32),
            *[jax.ShapeDtypeStruct((n, LANES), jnp.uint32)] * ROW_PLANES,
            jax.ShapeDtypeStruct((ROUTE_COLS, n), F32),
            jax.ShapeDtypeStruct((ROUTE_ROWS, LANES), F32),
        ],
        scratch_shapes=[pltpu.VMEM((ROUTE_ROWS, LANES), F32)],
        compiler_params=_cparams(1),
        name="outproj_router",
    )(hp, convy, o, w_out, g2, wr_hi, wr_lo, br, upper)


def _row_copy(src_hbm, src_row, dst_ref, dst_row, sem):
    return pltpu.make_async_copy(src_hbm.at[pl.ds(src_row, 1), :], dst_ref.at[pl.ds(dst_row, 1), :], sem)


def _dispatch_kernel(dest_ref, xn_hbm, xs_hbm, xbuf, sem_in, sem, *, tiles_per_seq, lp_len, tile):
    t = pl.program_id(0)
    base = (t // tiles_per_seq) * lp_len + (t % tiles_per_seq) * tile
    tile_copy = pltpu.make_async_copy(xn_hbm.at[pl.ds(base, tile), :], xbuf, sem_in)
    tile_copy.start()
    tile_copy.wait()

    def issue(r, carry):
        for kk in range(TOP_K):
            _row_copy(xbuf, r, xs_hbm, dest_ref[0, 0, TOP_K * r + kk], sem).start()
        return carry

    lax.fori_loop(0, tile, issue, 0)

    def drain(r, carry):
        for kk in range(TOP_K):
            _row_copy(xbuf, r, xs_hbm, dest_ref[0, 0, TOP_K * r + kk], sem).wait()
        return carry

    lax.fori_loop(0, tile, drain, 0)


def _dispatch(dest_tiles, xn, *, p_rows, tiles_per_seq, lp_len, tile):
    n_tiles = dest_tiles.shape[0]
    kern = functools.partial(_dispatch_kernel, tiles_per_seq=tiles_per_seq, lp_len=lp_len, tile=tile)
    return pl.pallas_call(
        kern,
        grid=(n_tiles,),
        in_specs=[
            pl.BlockSpec((1, 1, TOP_K * tile), lambda t: (t, 0, 0), memory_space=pltpu.SMEM),
            pl.BlockSpec(memory_space=pl.ANY),
        ],
        out_specs=pl.BlockSpec(memory_space=pl.ANY),
        out_shape=jax.ShapeDtypeStruct((p_rows, xn.shape[1]), xn.dtype),
        scratch_shapes=[pltpu.VMEM((tile, xn.shape[1]), xn.dtype), pltpu.SemaphoreType.DMA(()),
                        pltpu.SemaphoreType.DMA(())],
        compiler_params=_cparams(1),
        name="moe_dispatch",
    )(dest_tiles, xn)


def _experts_kernel(be_ref, nv_ref, xs_ref, wg_ref, wu_ref, wd_ref, y_ref, wgb_ref, wub_ref, wdb_ref):
    i = pl.program_id(0)
    e = be_ref[i]
    prev = be_ref[jnp.maximum(i - 1, 0)]

    @pl.when((i == 0) | (e != prev))
    def _():
        wgb_ref[...] = wg_ref[0].astype(BF16)
        wub_ref[...] = wu_ref[0].astype(BF16)
        wdb_ref[...] = wd_ref[0].astype(BF16)

    nv = nv_ref[i]

    @pl.when(nv > 0)
    def _():
        xs = xs_ref[...]
        row = lax.broadcasted_iota(jnp.int32, xs.shape, 0)
        x = jnp.where(row < nv, xs, 0.0).astype(BF16)
        hg = jnp.dot(x, wgb_ref[...], preferred_element_type=F32)
        hu = jnp.dot(x, wub_ref[...], preferred_element_type=F32)
        hid = (hg / (1.0 + jnp.exp(-hg)) * hu).astype(BF16)
        y_ref[...] = jnp.dot(hid, wdb_ref[...], preferred_element_type=F32)

    @pl.when(nv == 0)
    def _():
        y_ref[...] = jnp.zeros_like(y_ref)


def _experts(block_e, nvalid, xs, w_gate, w_up, w_down):
    p_rows, d = xs.shape
    ff = w_gate.shape[2]
    n_blocks = p_rows // MOE_BLOCK
    grid_spec = pltpu.PrefetchScalarGridSpec(
        num_scalar_prefetch=2,
        grid=(n_blocks,),
        in_specs=[
            pl.BlockSpec((MOE_BLOCK, d), lambda i, be, nv: (i, 0)),
            pl.BlockSpec((1, d, ff), lambda i, be, nv: (be[i], 0, 0)),
            pl.BlockSpec((1, d, ff), lambda i, be, nv: (be[i], 0, 0)),
            pl.BlockSpec((1, ff, d), lambda i, be, nv: (be[i], 0, 0)),
        ],
        out_specs=pl.BlockSpec((MOE_BLOCK, d), lambda i, be, nv: (i, 0)),
        scratch_shapes=[
            pltpu.VMEM((d, ff), BF16),
            pltpu.VMEM((d, ff), BF16),
            pltpu.VMEM((ff, d), BF16),
        ],
    )
    return pl.pallas_call(
        _experts_kernel,
        grid_spec=grid_spec,
        out_shape=jax.ShapeDtypeStruct((p_rows, d), F32),
        compiler_params=_cparams(1),
        name="moe_experts",
    )(block_e, nvalid, xs, w_gate, w_up, w_down)


def _combine_kernel(dest_ref, gates_ref, h1_hbm, y_hbm, out_ref, hbuf, ya, yb, sem_h, sem_a, sem_b,
                    *, lp_len, tile):
    b = pl.program_id(0)
    i = pl.program_id(1)
    start = b * lp_len + N_META + i * tile
    h_copy = pltpu.make_async_copy(h1_hbm.at[pl.ds(start, tile), :], hbuf, sem_h)
    h_copy.start()

    def issue(r, carry):
        _row_copy(y_hbm, dest_ref[0, 0, TOP_K * r], ya, r, sem_a).start()
        _row_copy(y_hbm, dest_ref[0, 0, TOP_K * r + 1], yb, r, sem_b).start()
        return carry

    lax.fori_loop(0, tile, issue, 0)

    def drain(r, carry):
        _row_copy(y_hbm, dest_ref[0, 0, TOP_K * r], ya, r, sem_a).wait()
        _row_copy(y_hbm, dest_ref[0, 0, TOP_K * r + 1], yb, r, sem_b).wait()
        return carry

    lax.fori_loop(0, tile, drain, 0)
    h_copy.wait()
    g = gates_ref[0]
    out_ref[0] = hbuf[...] + g[:, 0:1] * ya[...] + g[:, 1:2] * yb[...]


def _combine(dest_tiles, gates, h1, y, *, batch, seq, lp_len, tile):
    d = h1.shape[1]
    nt = seq // tile
    kern = functools.partial(_combine_kernel, lp_len=lp_len, tile=tile)
    return pl.pallas_call(
        kern,
        grid=(batch, nt),
        in_specs=[
            pl.BlockSpec((1, 1, TOP_K * tile), lambda b, i: (b * nt + i, 0, 0), memory_space=pltpu.SMEM),
            pl.BlockSpec((1, tile, TOP_K), lambda b, i: (b, i, 0)),
            pl.BlockSpec(memory_space=pl.ANY),
            pl.BlockSpec(memory_space=pl.ANY),
        ],
        out_specs=pl.BlockSpec((1, tile, d), lambda b, i: (b, i, 0)),
        out_shape=jax.ShapeDtypeStruct((batch, seq, d), F32),
        scratch_shapes=[
            pltpu.VMEM((tile, d), F32),
            pltpu.VMEM((tile, d), F32),
            pltpu.VMEM((tile, d), F32),
            pltpu.SemaphoreType.DMA(()),
            pltpu.SemaphoreType.DMA(()),
            pltpu.SemaphoreType.DMA(()),
        ],
        compiler_params=_cparams(2),
        name="moe_combine",
    )(dest_tiles, gates, h1, y)


def _sc_workers():
    info = plsc.get_sparse_core_info()
    return info.num_cores, info.num_cores * info.num_subcores


def _sc_scatter_rows(planes, idx_a, idx_b, out_rows):
    n_win = idx_a.shape[0]
    n_cores, n_workers = _sc_workers()
    trips = -(-n_win // n_workers)
    mesh = plsc.VectorSubcoreMesh(core_axis_name="c", subcore_axis_name="s")

    def body(*refs):
        xs = refs[0:ROW_PLANES]
        ia_hbm, ib_hbm = refs[ROW_PLANES:ROW_PLANES + 2]
        outs = refs[ROW_PLANES + 2:2 * ROW_PLANES + 2]
        ia_v, ib_v, buf, sem = refs[2 * ROW_PLANES + 2:]
        wid = lax.axis_index("s") * n_cores + lax.axis_index("c")

        def step(t, carry):
            g = wid + t * n_workers

            @pl.when(g < n_win)
            def _():
                pltpu.sync_copy(ia_hbm.at[g], ia_v)
                pltpu.sync_copy(ib_hbm.at[g], ib_v)
                row0 = pl.multiple_of(g * SC_WINDOW, SC_WINDOW)
                loads = [pltpu.async_copy(xs[c].at[pl.ds(row0, SC_WINDOW)], buf.at[c], sem)
                         for c in range(ROW_PLANES)]
                for cp in loads:
                    cp.wait()
                stores = [pltpu.async_copy(buf.at[c], outs[c].at[iv], sem)
                          for c in range(ROW_PLANES) for iv in (ia_v, ib_v)]
                for cp in stores:
                    cp.wait()

            return carry

        lax.fori_loop(0, trips, step, 0)

    kern = pl.kernel(
        body,
        out_type=[jax.ShapeDtypeStruct((out_rows, LANES), jnp.uint32)] * ROW_PLANES,
        mesh=mesh,
        scratch_types=[
            pltpu.VMEM((SC_WINDOW,), jnp.int32),
            pltpu.VMEM((SC_WINDOW,), jnp.int32),
            pltpu.VMEM((ROW_PLANES, SC_WINDOW, LANES), jnp.uint32),
            pltpu.SemaphoreType.DMA,
        ],
        name="moe_dispatch_sc",
    )
    return kern(*planes, idx_a, idx_b)


def _sc_gather_rows(planes, idx_a, idx_b):
    n_win = idx_a.shape[0]
    n_cores, n_workers = _sc_workers()
    trips = -(-n_win // n_workers)
    mesh = plsc.VectorSubcoreMesh(core_axis_name="c", subcore_axis_name="s")

    def body(*refs):
        ys = refs[0:ROW_PLANES]
        ia_hbm, ib_hbm = refs[ROW_PLANES:ROW_PLANES + 2]
        outs_a = refs[ROW_PLANES + 2:2 * ROW_PLANES + 2]
        outs_b = refs[2 * ROW_PLANES + 2:3 * ROW_PLANES + 2]
        iv, buf, sem = refs[3 * ROW_PLANES + 2:]
        wid = lax.axis_index("s") * n_cores + lax.axis_index("c")

        def step(t, carry):
            g = wid + t * n_workers

            @pl.when(g < n_win)
            def _():
                row0 = pl.multiple_of(g * SC_WINDOW, SC_WINDOW)
                for i_hbm, outs in ((ia_hbm, outs_a), (ib_hbm, outs_b)):
                    pltpu.sync_copy(i_hbm.at[g], iv)
                    loads = [pltpu.async_copy(ys[c].at[iv], buf.at[c], sem) for c in range(ROW_PLANES)]
                    for cp in loads:
                        cp.wait()
                    stores = [pltpu.async_copy(buf.at[c], outs[c].at[pl.ds(row0, SC_WINDOW)], sem)
                              for c in range(ROW_PLANES)]
                    for cp in stores:
                        cp.wait()

            return carry

        lax.fori_loop(0, trips, step, 0)

    n_rows = n_win * SC_WINDOW
    kern = pl.kernel(
        body,
        out_type=[jax.ShapeDtypeStruct((n_rows, LANES), jnp.uint32)] * (2 * ROW_PLANES),
        mesh=mesh,
        scratch_types=[
            pltpu.VMEM((SC_WINDOW,), jnp.int32),
            pltpu.VMEM((ROW_PLANES, SC_WINDOW, LANES), jnp.uint32),
            pltpu.SemaphoreType.DMA,
        ],
        name="moe_gather_sc",
    )
    res = kern(*planes, idx_a, idx_b)
    return res[:ROW_PLANES], res[ROW_PLANES:]


def _experts_kernel(be_ref, nv_ref, x0_ref, x1_ref, x2_ref, x3_ref, wg_ref, wu_ref, wd_ref,
                    y0_ref, y1_ref, y2_ref, y3_ref, wgb_ref, wub_ref, wdb_ref):
    i = pl.program_id(0)
    e = be_ref[i]
    prev = be_ref[jnp.maximum(i - 1, 0)]
    y_refs = (y0_ref, y1_ref, y2_ref, y3_ref)

    @pl.when((i == 0) | (e != prev))
    def _():
        wgb_ref[...] = wg_ref[0].astype(BF16)
        wub_ref[...] = wu_ref[0].astype(BF16)
        wdb_ref[...] = wd_ref[0].astype(BF16)

    nv = nv_ref[i]

    @pl.when(nv > 0)
    def _():
        xs = _unpack_rows([x0_ref[...], x1_ref[...], x2_ref[...], x3_ref[...]])
        row = lax.broadcasted_iota(jnp.int32, xs.shape, 0)
        x = jnp.where(row < nv, xs, jnp.zeros_like(xs))
        hg = jnp.dot(x, wgb_ref[...], preferred_element_type=F32)
        hu = jnp.dot(x, wub_ref[...], preferred_element_type=F32)
        hid = (hg / (1.0 + jnp.exp(-hg)) * hu).astype(BF16)
        yw = _pack_rows(jnp.dot(hid, wdb_ref[...], preferred_element_type=F32))
        for c, ref in enumerate(y_refs):
            ref[...] = yw[:, c * LANES:(c + 1) * LANES]

    @pl.when(nv == 0)
    def _():
        for ref in y_refs:
            ref[...] = jnp.zeros_like(ref)


def _experts(block_e, nvalid, xs_planes, w_gate, w_up, w_down):
    n_blocks = block_e.shape[0]
    _, d, ff = w_gate.shape
    blk = lambda i, be, nv: (i, 0)
    wsel = lambda i, be, nv: (be[i], 0, 0)
    grid_spec = pltpu.PrefetchScalarGridSpec(
        num_scalar_prefetch=2,
        grid=(n_blocks,),
        in_specs=[
            *[pl.BlockSpec((MOE_BLOCK, LANES), blk)] * ROW_PLANES,
            pl.BlockSpec((1, d, ff), wsel),
            pl.BlockSpec((1, d, ff), wsel),
            pl.BlockSpec((1, ff, d), wsel),
        ],
        out_specs=[pl.BlockSpec((MOE_BLOCK, LANES), blk)] * ROW_PLANES,
        scratch_shapes=[
            pltpu.VMEM((d, ff), BF16),
            pltpu.VMEM((d, ff), BF16),
            pltpu.VMEM((ff, d), BF16),
        ],
    )
    return pl.pallas_call(
        _experts_kernel,
        grid_spec=grid_spec,
        out_shape=[jax.ShapeDtypeStruct((n_blocks * MOE_BLOCK, LANES), jnp.uint32)] * ROW_PLANES,
        compiler_params=_cparams(1),
        name="moe_experts",
    )(block_e, nvalid, *xs_planes, w_gate, w_up, w_down)


def _combine_kernel(gates_ref, h1_hbm, *refs, lp_len, tile):
    a_refs = refs[0:ROW_PLANES]
    b_refs = refs[ROW_PLANES:2 * ROW_PLANES]
    out_ref, hbuf, sem_h = refs[2 * ROW_PLANES:]
    nt = pl.num_programs(1)
    step = pl.program_id(0) * nt + pl.program_id(1)
    last = pl.num_programs(0) * nt - 1

    def h_copy(s, slot):
        start = (s // nt) * lp_len + N_META + (s % nt) * tile
        return pltpu.make_async_copy(h1_hbm.at[pl.ds(start, tile), :], hbuf.at[slot], sem_h.at[slot])

    slot = step % 2

    @pl.when(step == 0)
    def _():
        h_copy(step, slot).start()

    @pl.when(step < last)
    def _():
        h_copy(step + 1, 1 - slot).start()

    ya = _unpack_rows([r[...] for r in a_refs]).astype(F32)
    yb = _unpack_rows([r[...] for r in b_refs]).astype(F32)
    g = jnp.concatenate([gates_ref[...]] * (LANES // ROUTE_COLS), axis=0).T
    moe = g[:, 0:1] * ya + g[:, 1:2] * yb
    h_copy(step, slot).wait()
    out_ref[0] = hbuf[slot] + moe


def _combine(gates, h1, a_planes, b_planes, *, batch, seq, lp_len, tile):
    d = h1.shape[1]
    nt = seq // tile
    kern = functools.partial(_combine_kernel, lp_len=lp_len, tile=tile)
    rows = lambda b, i: (b * nt + i, 0)
    return pl.pallas_call(
        kern,
        grid=(batch, nt),
        in_specs=[
            pl.BlockSpec((ROUTE_COLS, tile), lambda b, i: (0, b * nt + i)),
            pl.BlockSpec(memory_space=pl.ANY),
            *[pl.BlockSpec((tile, LANES), rows)] * (2 * ROW_PLANES),
        ],
        out_specs=pl.BlockSpec((1, tile, d), lambda b, i: (b, i, 0)),
        out_shape=jax.ShapeDtypeStruct((batch, seq, d), F32),
        scratch_shapes=[pltpu.VMEM((2, tile, d), F32), pltpu.SemaphoreType.DMA((2,))],
        compiler_params=_cparams(2),
        name="moe_combine",
    )(gates, h1, *a_planes, *b_planes)


def _rope_tables(length, lp_len):
    half = ROPE_DIM // 2
    pos = jnp.arange(length, dtype=F32)
    inv_freq = ROPE_THETA ** (-jnp.arange(0, ROPE_DIM, 2, dtype=F32) / ROPE_DIM)
    ang = pos[:, None] * inv_freq[None, :]
    cos = jnp.cos(ang)
    sin = jnp.sin(ang)
    ones = jnp.ones((length, HEAD_DIM - ROPE_DIM), F32)
    zeros_h = jnp.zeros((length, half), F32)
    zeros_r = jnp.zeros((length, HEAD_DIM - ROPE_DIM), F32)
    c = jnp.concatenate([cos, cos, ones], axis=1)
    s1 = jnp.concatenate([zeros_h, sin, zeros_r], axis=1)
    s2 = jnp.concatenate([-sin, zeros_h, zeros_r], axis=1)
    pad = ((0, lp_len - length), (0, 0))
    rep = LANES // HEAD_DIM
    return tuple(jnp.pad(jnp.tile(t, (1, rep)), pad) for t in (c, s1, s2))


def _layer(hp, l, batch, length, lp_len, tm, norm1_g, w_in, conv_w, q_norm_g, k_norm_g, lambda_q1, lambda_k1,
           lambda_q2, lambda_k2, subln_g, w_out, norm2_g, w_router_group, b_router_group, w_router_expert,
           b_router_expert, w_gate, w_up, w_down, rope, last):
    n, d = hp.shape
    tiles_per_seq = lp_len // tm
    cw = conv_w.shape[2]
    qw = N_HEADS * 2 * HEAD_DIM
    lam_init = 0.8 - 0.6 * math.exp(-0.3 * l)

    reps = qw // HEAD_DIM
    gq = jnp.tile(q_norm_g[l] * (HEAD_DIM ** -0.5 * LOG2E), reps)[None, :]
    gk = jnp.tile(k_norm_g[l], reps)[None, :]
    seg = jnp.arange(qw) // HEAD_DIM
    bd = (seg[:, None] == seg[None, :]).astype(BF16)
    convy, q, k, v = _inproj(hp, norm1_g[l][None, :], w_in[l].astype(BF16), conv_w[l], gq, gk, bd, *rope,
                             tiles_per_seq=tiles_per_seq, tm=tm)

    lamp = jnp.stack([lambda_q1[l], lambda_k1[l], lambda_q2[l], lambda_k2[l]]).astype(F32)
    o = _attention(q, k, v, lamp, subln_g[l][None, :], batch=batch, lp_len=lp_len, tq=tm, lam_init=lam_init)

    lane_pad = LANES - N_GROUPS - N_EXPERTS
    wr = jnp.pad(jnp.concatenate([w_router_group[l], w_router_expert[l]], axis=1), ((0, 0), (0, lane_pad)))
    wr_hi = wr.astype(BF16)
    wr_lo = (wr - wr_hi.astype(F32)).astype(BF16)
    br = jnp.pad(jnp.concatenate([b_router_group[l], b_router_expert[l]]), (0, lane_pad))[None, :]
    ridx = jnp.arange(tm)
    upper = (ridx[:, None] < ridx[None, :]).astype(BF16)
    h1, *rest = _outproj_t(hp, convy, o, w_out[l].astype(BF16), norm2_g[l][None, :], wr_hi, wr_lo, br,
                           upper, tiles_per_seq=tiles_per_seq, seq_len=length, tm=tm)
    x_planes = rest[:ROW_PLANES]
    route, cnt = rest[ROW_PLANES:]

    route = route.reshape(ROUTE_COLS, batch, lp_len)
    eid = route[0:TOP_K].astype(jnp.int32)
    gates = route[TOP_K:2 * TOP_K]
    rank = route[2 * TOP_K:3 * TOP_K].astype(jnp.int32)
    counts = cnt[EXPERT_LANE0:EXPERT_LANE0 + N_EXPERTS, 0].astype(jnp.int32)
    a = batch * length * TOP_K
    n_blocks = -(-a // MOE_BLOCK) + N_EXPERTS
    p_rows = n_blocks * MOE_BLOCK
    padded = (counts + MOE_BLOCK - 1) // MOE_BLOCK * MOE_BLOCK
    pends = jnp.cumsum(padded)
    pstarts = pends - padded

    def lookup(table, idx):
        return sum(jnp.where(idx == e, table[e], 0) for e in range(N_EXPERTS))

    n_pad = lp_len - length
    pos = jnp.arange(lp_len, dtype=jnp.int32)[None, None, :]
    spare = p_rows + ((jnp.arange(batch, dtype=jnp.int32)[None, :, None] * n_pad + (pos - length)) * TOP_K
                      + jnp.arange(TOP_K, dtype=jnp.int32)[:, None, None])
    dest = jnp.where(pos < length, lookup(pstarts, eid) + rank, spare)
    spare_rows = -(-(batch * n_pad * TOP_K) // MOE_BLOCK) * MOE_BLOCK
    blk0 = jnp.arange(n_blocks, dtype=jnp.int32) * MOE_BLOCK
    block_e = jnp.minimum(jnp.sum((pends[None, :] <= blk0[:, None]).astype(jnp.int32), axis=1), N_EXPERTS - 1)
    nvalid = jnp.clip(lookup(counts, block_e) - (blk0 - lookup(pstarts, block_e)), 0, MOE_BLOCK)

    assert (batch * lp_len) % SC_WINDOW == 0
    xs_planes = _sc_scatter_rows(x_planes, dest[0].reshape(-1, SC_WINDOW), dest[1].reshape(-1, SC_WINDOW),
                                 p_rows + spare_rows)
    y_planes = _experts(block_e, nvalid, xs_planes, w_gate[l], w_up[l], w_down[l])

    if not last:
        raise NotImplementedError("only the final layer's combine (which drops the meta tokens) is implemented")
    seq = length - N_META
    assert (batch * seq) % SC_WINDOW == 0
    dest_x = dest[:, :, N_META:length]
    a_planes, b_planes = _sc_gather_rows(y_planes, dest_x[0].reshape(-1, SC_WINDOW),
                                         dest_x[1].reshape(-1, SC_WINDOW))
    ctile = _largest_tile(seq, 512, LANES)
    gates_x = jnp.pad(gates[:, :, N_META:length].reshape(TOP_K, batch * seq), ((0, ROUTE_COLS - TOP_K), (0, 0)))
    return _combine(gates_x, h1, a_planes, b_planes, batch=batch, seq=seq, lp_len=lp_len, tile=ctile)


def kernel(x, meta_tokens, norm1_g, w_in, conv_w, q_norm_g, k_norm_g, lambda_q1, lambda_k1, lambda_q2, lambda_k2,
           subln_g, w_out, norm2_g, w_router_group, b_router_group, w_router_expert, b_router_expert, w_gate,
           w_up, w_down):
    b, s, d = x.shape
    depth = w_in.shape[0]
    assert depth == 1, "a single layer is supported"
    length = s + N_META
    tm = TOKEN_TILE if length >= TOKEN_TILE else LANES
    lp_len = -(-length // tm) * tm
    hp = jnp.concatenate([jnp.broadcast_to(meta_tokens[None].astype(x.dtype), (b, N_META, d)), x,
                          jnp.zeros((b, lp_len - length, d), x.dtype)], axis=1).reshape(b * lp_len, d)
    rope = _rope_tables(length, lp_len)
    return _layer(hp, 0, b, length, lp_len, tm, norm1_g, w_in, conv_w, q_norm_g, k_norm_g, lambda_q1, lambda_k1,
                  lambda_q2, lambda_k2, subln_g, w_out, norm2_g, w_router_group, b_router_group,
                  w_router_expert, b_router_expert, w_gate, w_up, w_down, rope, last=True)
```

```python
import functools
import math

import jax
import jax.numpy as jnp
from jax import lax
from jax.experimental import pallas as pl
from jax.experimental.pallas import tpu as pltpu
from jax.experimental.pallas import tpu_sc as plsc

F32 = jnp.float32
BF16 = jnp.bfloat16

N_META = 16
CONV_K = 3
N_HEADS = 4
HEAD_DIM = 64
V_DIM = 2 * HEAD_DIM
ROPE_DIM = HEAD_DIM // 4
ROPE_THETA = 500000.0
N_GROUPS = 4
EXPERTS_PER_GROUP = 8
N_EXPERTS = N_GROUPS * EXPERTS_PER_GROUP
TOP_K = 2
EPS = 1e-6
LOG2E = 1.4426950408889634

LANES = 128
TOKEN_TILE = 640
INPROJ_CHAINS = 2
ATTN_HEADS_PER_STEP = 4
ATTN_WIDE_CHUNKS = 2
MOE_BLOCK = 512
ROUTE_COLS = 8
ROUTE_ROWS = 64
ROW_PLANES = 4
SC_WINDOW = 128
EXPERT_LANE0 = N_GROUPS
NEG_BIG = -1e30
VMEM_LIMIT = 56 * 1024 * 1024


def _largest_tile(n, cap, mult):
    for t in range(min(cap, n), 0, -1):
        if n % t == 0 and t % mult == 0:
            return t
    raise ValueError(f"no tile for {n}")


def _cparams(n_axes, flags=None):
    return pltpu.CompilerParams(dimension_semantics=("arbitrary",) * n_axes,
                                vmem_limit_bytes=VMEM_LIMIT, flags=flags)


def _inproj_kernel(x_ref, g1_ref, win_ref, convw_ref, gq_ref, gk_ref, bd_ref, rc_ref, rs1_ref, rs2_ref,
                   convy_ref, q_ref, k_ref, v_ref, carry_ref, *, tiles_per_seq, cw, qw):
    i = pl.program_id(0)
    tm = x_ref.shape[0]
    sub = tm // INPROJ_CHAINS
    q0 = 3 * cw
    w = convw_ref[...]

    @pl.when(i % tiles_per_seq == 0)
    def _():
        carry_ref[...] = jnp.zeros_like(carry_ref)

    prev = carry_ref[...]

    for chain in range(INPROJ_CHAINS):
        rows = pl.ds(chain * sub, sub)
        x = x_ref[rows, :]
        ms = jnp.mean(x * x, axis=-1, keepdims=True)
        xn = (x * lax.rsqrt(ms + EPS) * g1_ref[...]).astype(BF16)

        def proj(lo, hi):
            return jnp.dot(xn, win_ref[:, lo:hi], preferred_element_type=F32)

        u_conv = proj(0, q0)
        u_q = proj(q0, q0 + qw)

        z = u_conv[:, cw:2 * cw] * u_conv[:, 2 * cw:3 * cw]
        p1 = prev[7:8]
        p2 = prev[6:7]
        row = lax.broadcasted_iota(jnp.int32, z.shape, 0)
        z1 = jnp.where(row == 0, p1, pltpu.roll(z, 1, axis=0))
        z2 = jnp.where(row == 0, p2, jnp.where(row == 1, p1, pltpu.roll(z, 2, axis=0)))
        prev = z[sub - 8:sub]
        conv = w[0:1] * z2 + w[1:2] * z1 + w[2:3] * z
        convy_ref[rows, :] = (u_conv[:, 0:cw] * conv).astype(BF16)

        rc = rc_ref[rows, :]
        rs1 = rs1_ref[rows, :]
        rs2 = rs2_ref[rows, :]

        def norm_rope(t, g_ref):
            ss = jnp.dot((t * t).astype(BF16), bd_ref[...], preferred_element_type=F32)
            tn = t * lax.rsqrt(ss * (1.0 / HEAD_DIM) + EPS) * g_ref[...]
            outs = []
            for c in range(qw // LANES):
                ch = tn[:, c * LANES:(c + 1) * LANES]
                outs.append(ch * rc + pltpu.roll(ch, ROPE_DIM // 2, axis=1) * rs1
                            + pltpu.roll(ch, LANES - ROPE_DIM // 2, axis=1) * rs2)
            return jnp.concatenate(outs, axis=1).astype(BF16)

        u_k = proj(q0 + qw, q0 + 2 * qw)
        q_ref[rows, :] = norm_rope(u_q, gq_ref)
        u_v = proj(q0 + 2 * qw, win_ref.shape[1])
        k_ref[rows, :] = norm_rope(u_k, gk_ref)
        v_ref[rows, :] = u_v.astype(BF16)

    carry_ref[...] = prev


def _inproj(hp, g1, w_in, conv_w, gq, gk, bd, rc, rs1, rs2, *, tiles_per_seq, tm):
    n, d = hp.shape
    cw = conv_w.shape[1]
    qw = gq.shape[1]
    aw = w_in.shape[1] - 3 * cw - 2 * qw
    const = lambda i: (0, 0)
    tile = lambda i: (i, 0)
    pos = lambda i: (i % tiles_per_seq, 0)
    kern = functools.partial(_inproj_kernel, tiles_per_seq=tiles_per_seq, cw=cw, qw=qw)
    return pl.pallas_call(
        kern,
        grid=(n // tm,),
        in_specs=[
            pl.BlockSpec((tm, d), tile),
            pl.BlockSpec((1, d), const),
            pl.BlockSpec(w_in.shape, const),
            pl.BlockSpec(conv_w.shape, const),
            pl.BlockSpec((1, qw), const),
            pl.BlockSpec((1, qw), const),
            pl.BlockSpec(bd.shape, const),
            pl.BlockSpec((tm, LANES), pos),
            pl.BlockSpec((tm, LANES), pos),
            pl.BlockSpec((tm, LANES), pos),
        ],
        out_specs=[
            pl.BlockSpec((tm, cw), tile),
            pl.BlockSpec((tm, qw), tile),
            pl.BlockSpec((tm, qw), tile),
            pl.BlockSpec((tm, aw), tile),
        ],
        out_shape=[
            jax.ShapeDtypeStruct((n, cw), BF16),
            jax.ShapeDtypeStruct((n, qw), BF16),
            jax.ShapeDtypeStruct((n, qw), BF16),
            jax.ShapeDtypeStruct((n, aw), BF16),
        ],
        scratch_shapes=[pltpu.VMEM((8, cw), F32)],
        compiler_params=_cparams(1),
        name="inproj",
    )(hp, g1, w_in, conv_w, gq, gk, bd, rc, rs1, rs2)


def _attn_kernel(q_ref, k_ref, v_ref, lamp_ref, sg_ref, o_ref, qs_ref, m_ref, l_ref, acc_ref, *, lam_init):
    qi = pl.program_id(2)
    tq = q_ref.shape[0]
    n_heads = q_ref.shape[1] // LANES
    n_chains = 2 * n_heads
    lane = lax.broadcasted_iota(jnp.int32, (tq, LANES), 1)
    for h in range(n_heads):
        q = q_ref[:, h * LANES:(h + 1) * LANES]
        zero = jnp.zeros_like(q)
        qs_ref[pl.ds(2 * h * tq, tq), :] = jnp.where(lane < HEAD_DIM, q, zero)
        qs_ref[pl.ds((2 * h + 1) * tq, tq), :] = jnp.where(lane >= HEAD_DIM, q, zero)
    m_ref[...] = jnp.full_like(m_ref, NEG_BIG)
    l_ref[...] = jnp.zeros_like(l_ref)
    acc_ref[...] = jnp.zeros_like(acc_ref)

    def scores(off, width, which):
        h = which // 2
        kc = k_ref[pl.ds(off, width), h * LANES:(h + 1) * LANES]
        return lax.dot_general(qs_ref[pl.ds(which * tq, tq), :], kc, (((1,), (1,)), ((), ())),
                               preferred_element_type=F32)

    def update(off, width, which, s, masked):
        h = which // 2
        vc = jnp.concatenate([v_ref[pl.ds(off, width), h * LANES:(h + 1) * LANES],
                              jnp.ones((width, LANES), BF16)], axis=1)
        rows = pl.ds(which * tq, tq)
        if masked:
            r = lax.broadcasted_iota(jnp.int32, s.shape, 0)
            c = lax.broadcasted_iota(jnp.int32, s.shape, 1)
            s = jnp.where(c <= r, s, NEG_BIG)
        m_prev = m_ref[rows, :]
        m_new = jnp.maximum(m_prev, jnp.max(s, axis=-1, keepdims=True))
        alpha = jnp.exp2(m_prev - m_new)
        p = jnp.exp2((s - jnp.tile(m_new, (1, width // LANES))).astype(BF16))
        pv = jnp.dot(p, vc, preferred_element_type=F32)
        l_ref[rows, :] = alpha * l_ref[rows, :] + pv[:, LANES:]
        acc_ref[rows, :] = alpha * acc_ref[rows, :] + pv[:, :LANES]
        m_ref[rows, :] = m_new

    def chunk(off, width, masked):
        s_next = scores(off, width, 0)
        for c in range(n_chains):
            s = s_next
            if c + 1 < n_chains:
                s_next = scores(off, width, c + 1)
            update(off, width, c, s, masked)

    wide = ATTN_WIDE_CHUNKS * tq

    def body(j, carry):
        chunk(pl.multiple_of(j * wide, wide), wide, False)
        return carry

    n_wide = qi // ATTN_WIDE_CHUNKS
    lax.fori_loop(0, n_wide, body, 0)
    for extra in range(ATTN_WIDE_CHUNKS - 1):
        @pl.when(n_wide * ATTN_WIDE_CHUNKS + extra < qi)
        def _():
            chunk(pl.multiple_of((n_wide * ATTN_WIDE_CHUNKS + extra) * tq, tq), tq, False)

    chunk(pl.multiple_of(qi * tq, tq), tq, True)

    lp = lamp_ref[...]
    lam = (jnp.exp(jnp.sum(lp[0:1] * lp[1:2], axis=-1, keepdims=True))
           - jnp.exp(jnp.sum(lp[2:3] * lp[3:4], axis=-1, keepdims=True)) + lam_init)
    for h in range(n_heads):
        rows = pl.ds(2 * h * tq, 2 * tq)
        o_all = acc_ref[rows, :] / l_ref[rows, :]
        o = o_all[0:tq] - lam * o_all[tq:2 * tq]
        ms = jnp.mean(o * o, axis=-1, keepdims=True)
        o_ref[:, h * LANES:(h + 1) * LANES] = (o * lax.rsqrt(ms + EPS) * sg_ref[...]
                                               * (1.0 - lam_init)).astype(BF16)


def _attention(q, k, v, lamp, sg, *, batch, lp_len, tq, lam_init):
    n, qw = q.shape
    nq = lp_len // tq
    hw = ATTN_HEADS_PER_STEP * LANES
    chains = 2 * ATTN_HEADS_PER_STEP
    kern = functools.partial(_attn_kernel, lam_init=lam_init)
    return pl.pallas_call(
        kern,
        grid=(batch, qw // hw, nq),
        in_specs=[
            pl.BlockSpec((tq, hw), lambda b, h, i: (b * nq + i, h)),
            pl.BlockSpec((lp_len, hw), lambda b, h, i: (b, h)),
            pl.BlockSpec((lp_len, hw), lambda b, h, i: (b, h)),
            pl.BlockSpec(lamp.shape, lambda b, h, i: (0, 0)),
            pl.BlockSpec(sg.shape, lambda b, h, i: (0, 0)),
        ],
        out_specs=pl.BlockSpec((tq, hw), lambda b, h, i: (b * nq + i, h)),
        out_shape=jax.ShapeDtypeStruct((n, v.shape[1]), BF16),
        scratch_shapes=[
            pltpu.VMEM((chains * tq, LANES), BF16),
            pltpu.VMEM((chains * tq, LANES), F32),
            pltpu.VMEM((chains * tq, LANES), F32),
            pltpu.VMEM((chains * tq, LANES), F32),
        ],
        compiler_params=_cparams(3),
        name="diffattn",
    )(q, k, v, lamp, sg)


def _pack_rows(x):
    w = x.shape[1] // 2
    lo = lax.bitcast_convert_type(x[:, :w].astype(BF16).astype(F32), jnp.uint32)
    hi = lax.bitcast_convert_type(x[:, w:].astype(BF16).astype(F32), jnp.uint32)
    return lax.shift_right_logical(lo, jnp.uint32(16)) | (hi & jnp.uint32(0xFFFF0000))


def _unpack_rows(planes):
    w = jnp.concatenate(planes, axis=1)
    lo = lax.bitcast_convert_type(lax.shift_left(w, jnp.uint32(16)), F32)
    hi = lax.bitcast_convert_type(w & jnp.uint32(0xFFFF0000), F32)
    return jnp.concatenate([lo, hi], axis=1).astype(BF16)


def _outproj_kernel(hp_ref, cy_ref, o_ref, wout_ref, g2_ref, wrh_ref, wrl_ref, br_ref, tri_ref,
                    h1_ref, xp0_ref, xp1_ref, xp2_ref, xp3_ref, route_ref, cnt_ref, run_ref,
                    *, tiles_per_seq, seq_len):
    i = pl.program_id(0)
    tm = hp_ref.shape[0]
    sub = tri_ref.shape[0]
    wr_both = jnp.concatenate([wrh_ref[...], wrl_ref[...]], axis=1)

    @pl.when(i == 0)
    def _():
        run_ref[...] = jnp.zeros_like(run_ref)

    def project(rows):
        mix = jnp.concatenate([cy_ref[rows, :], o_ref[rows, :]], axis=1)
        h1 = hp_ref[rows, :] + jnp.dot(mix, wout_ref[...], preferred_element_type=F32)
        h1_ref[rows, :] = h1
        ms = jnp.mean(h1 * h1, axis=-1, keepdims=True)
        xn = h1 * lax.rsqrt(ms + EPS) * g2_ref[...]
        xw = _pack_rows(xn)
        for c, ref in enumerate((xp0_ref, xp1_ref, xp2_ref, xp3_ref)):
            ref[rows, :] = xw[:, c * LANES:(c + 1) * LANES]
        x_hi = xn.astype(BF16)
        x_lo = (xn - x_hi.astype(F32)).astype(BF16)
        hi_both = jnp.dot(x_hi, wr_both, preferred_element_type=F32)
        return (hi_both[:, :LANES] + hi_both[:, LANES:]
                + jnp.dot(x_lo, wrh_ref[...], preferred_element_type=F32) + br_ref[...])

    chains = [pl.ds(c * sub, sub) for c in range(tm // sub)]
    all_logits = [project(rows) for rows in chains]
    run = run_ref[0:1, :]
    for c, (rows, logits) in enumerate(zip(chains, all_logits)):
        run = _route_rows(i, c, rows, logits, run, tri_ref, route_ref, tiles_per_seq=tiles_per_seq,
                          seq_len=seq_len, tm=tm)
    run_ref[...] = jnp.broadcast_to(run, run_ref.shape)
    cnt_ref[...] = jnp.broadcast_to(run, cnt_ref.shape)


def _route_rows(i, c, rows, logits, run, tri_ref, route_ref, *, tiles_per_seq, seq_len, tm):
    sub = logits.shape[0]
    lane = lax.broadcasted_iota(jnp.int32, logits.shape, 1)
    big = jnp.int32(4 * LANES)

    def first_argmax(vals, vmax):
        return jnp.min(jnp.where(vals == vmax, lane, big), axis=-1, keepdims=True)

    gl = jnp.where(lane < N_GROUPS, logits, NEG_BIG)
    gmax = jnp.max(gl, axis=-1, keepdims=True)
    g_val = 1.0 / jnp.sum(jnp.exp(gl - gmax), axis=-1, keepdims=True)
    g_idx = first_argmax(gl, gmax)
    lo = EXPERT_LANE0 + EXPERTS_PER_GROUP * g_idx
    el = jnp.where((lane >= lo) & (lane < lo + EXPERTS_PER_GROUP), logits, NEG_BIG)
    m1 = jnp.max(el, axis=-1, keepdims=True)
    i1 = first_argmax(el, m1)
    el2 = jnp.where(lane == i1, NEG_BIG, el)
    m2 = jnp.max(el2, axis=-1, keepdims=True)
    i2 = first_argmax(el2, m2)
    r = jnp.exp(m2 - m1)
    gate1 = g_val / (1.0 + r)
    gate2 = g_val * r / (1.0 + r)

    prow = (i % tiles_per_seq) * tm + c * sub + lax.broadcasted_iota(jnp.int32, logits.shape, 0)
    valid = prow < seq_len
    oh1 = jnp.where(valid & (lane == i1), 1.0, 0.0)
    oh2 = jnp.where(valid & (lane == i2), 1.0, 0.0)
    pre = jnp.dot(tri_ref[...], jnp.concatenate([oh1, oh2], axis=1).astype(BF16), preferred_element_type=F32)
    pre1 = pre[:, :LANES]
    pre2 = pre[:, LANES:]
    tot1 = jnp.sum(oh1, axis=0, keepdims=True)
    tot2 = jnp.sum(oh2, axis=0, keepdims=True)
    rank1 = jnp.sum(oh1 * (pre1 + run), axis=-1, keepdims=True)
    rank2 = jnp.sum(oh2 * (pre2 + run + tot1), axis=-1, keepdims=True)

    e1 = (i1 - EXPERT_LANE0).astype(F32)
    e2 = (i2 - EXPERT_LANE0).astype(F32)
    packed = jnp.where(lane == 0, e1, jnp.where(lane == 1, e2, jnp.where(lane == 2, gate1, jnp.where(
        lane == 3, gate2, jnp.where(lane == 4, rank1, jnp.where(lane == 5, rank2, 0.0))))))
    route_ref[rows, :] = packed[:, 0:ROUTE_COLS]
    return run + tot1 + tot2


def _outproj(hp, convy, o, w_out, g2, wr_hi, wr_lo, br, tri, *, tiles_per_seq, seq_len, tm):
    n, d = hp.shape
    const = lambda i: (0, 0)
    tile = lambda i: (i, 0)
    kern = functools.partial(_outproj_kernel, tiles_per_seq=tiles_per_seq, seq_len=seq_len)
    return pl.pallas_call(
        kern,
        grid=(n // tm,),
        in_specs=[
            pl.BlockSpec((tm, d), tile),
            pl.BlockSpec((tm, convy.shape[1]), tile),
            pl.BlockSpec((tm, o.shape[1]), tile),
            pl.BlockSpec(w_out.shape, const),
            pl.BlockSpec((1, d), const),
            pl.BlockSpec(wr_hi.shape, const),
            pl.BlockSpec(wr_lo.shape, const),
            pl.BlockSpec((1, LANES), const),
            pl.BlockSpec(tri.shape, const),
        ],
        out_specs=[
            pl.BlockSpec((tm, d), tile),
            *[pl.BlockSpec((tm, LANES), tile)] * ROW_PLANES,
            pl.BlockSpec((tm, ROUTE_COLS), tile),
            pl.BlockSpec((8, LANES), const),
        ],
        out_shape=[
            jax.ShapeDtypeStruct((n, d), F32),
            *[jax.ShapeDtypeStruct((n, LANES), jnp.uint32)] * ROW_PLANES,
            jax.ShapeDtypeStruct((n, ROUTE_COLS), F32),
            jax.ShapeDtypeStruct((8, LANES), F32),
        ],
        scratch_shapes=[pltpu.VMEM((8, LANES), F32)],
        compiler_params=_cparams(1),
        name="outproj_router",
    )(hp, convy, o, w_out, g2, wr_hi, wr_lo, br, tri)


def _outproj_t_kernel(hp_ref, cy_ref, o_ref, wout_ref, g2_ref, wrh_ref, wrl_ref, br_ref, upper_ref,
                      h1_ref, xp0_ref, xp1_ref, xp2_ref, xp3_ref, route_ref, cnt_ref, run_ref,
                      *, tiles_per_seq, seq_len):
    i = pl.program_id(0)
    tm = hp_ref.shape[0]

    @pl.when(i == 0)
    def _():
        run_ref[...] = jnp.zeros_like(run_ref)

    mix = jnp.concatenate([cy_ref[...], o_ref[...]], axis=1)
    h1 = hp_ref[...] + jnp.dot(mix, wout_ref[...], preferred_element_type=F32)
    h1_ref[...] = h1
    ms = jnp.mean(h1 * h1, axis=-1, keepdims=True)
    xn = h1 * lax.rsqrt(ms + EPS) * g2_ref[...]
    xw = _pack_rows(xn)
    for c, ref in enumerate((xp0_ref, xp1_ref, xp2_ref, xp3_ref)):
        ref[...] = xw[:, c * LANES:(c + 1) * LANES]

    x_hi = xn.astype(BF16)
    x_lo = (xn - x_hi.astype(F32)).astype(BF16)
    hi_both = jnp.dot(x_hi, jnp.concatenate([wrh_ref[...], wrl_ref[...]], axis=1), preferred_element_type=F32)
    logits = (hi_both[:, :LANES] + hi_both[:, LANES:]
              + jnp.dot(x_lo, wrh_ref[...], preferred_element_type=F32) + br_ref[...])

    lt = logits.T[0:ROUTE_ROWS, :]
    row = lax.broadcasted_iota(jnp.int32, lt.shape, 0)
    big = jnp.int32(4 * LANES)

    def first_argmax(vals, vmax):
        return jnp.min(jnp.where(vals == vmax, row, big), axis=0, keepdims=True)

    gl = jnp.where(row < N_GROUPS, lt, NEG_BIG)
    gmax = jnp.max(gl, axis=0, keepdims=True)
    g_val = 1.0 / jnp.sum(jnp.exp(gl - gmax), axis=0, keepdims=True)
    g_idx = first_argmax(gl, gmax)
    lo = EXPERT_LANE0 + EXPERTS_PER_GROUP * g_idx
    el = jnp.where((row >= lo) & (row < lo + EXPERTS_PER_GROUP), lt, NEG_BIG)
    m1 = jnp.max(el, axis=0, keepdims=True)
    i1 = first_argmax(el, m1)
    el2 = jnp.where(row == i1, NEG_BIG, el)
    m2 = jnp.max(el2, axis=0, keepdims=True)
    i2 = first_argmax(el2, m2)
    r = jnp.exp(m2 - m1)
    gate1 = g_val / (1.0 + r)
    gate2 = g_val * r / (1.0 + r)

    pos = (i % tiles_per_seq) * tm + lax.broadcasted_iota(jnp.int32, (1, tm), 1)
    valid = pos < seq_len
    oh1 = jnp.where(valid & (row == i1), 1.0, 0.0)
    oh2 = jnp.where(valid & (row == i2), 1.0, 0.0)
    pre = jnp.dot(jnp.concatenate([oh1, oh2], axis=0).astype(BF16), upper_ref[...], preferred_element_type=F32)
    tot1 = jnp.sum(oh1, axis=1, keepdims=True)
    tot2 = jnp.sum(oh2, axis=1, keepdims=True)
    run = run_ref[...]
    run_t = jnp.tile(run, (1, tm // LANES))
    rank1 = jnp.sum(oh1 * (pre[:ROUTE_ROWS] + run_t), axis=0, keepdims=True)
    rank2 = jnp.sum(oh2 * (pre[ROUTE_ROWS:] + run_t + tot1), axis=0, keepdims=True)
    new_run = run + tot1 + tot2
    run_ref[...] = new_run
    cnt_ref[...] = new_run

    e1 = (i1 - EXPERT_LANE0).astype(F32)
    e2 = (i2 - EXPERT_LANE0).astype(F32)
    r8 = lax.broadcasted_iota(jnp.int32, (ROUTE_COLS, tm), 0)
    route_ref[...] = jnp.where(r8 == 0, e1, jnp.where(r8 == 1, e2, jnp.where(r8 == 2, gate1, jnp.where(
        r8 == 3, gate2, jnp.where(r8 == 4, rank1, jnp.where(r8 == 5, rank2, 0.0))))))


def _outproj_t(hp, convy, o, w_out, g2, wr_hi, wr_lo, br, upper, *, tiles_per_seq, seq_len, tm):
    n, d = hp.shape
    const = lambda i: (0, 0)
    tile = lambda i: (i, 0)
    kern = functools.partial(_outproj_t_kernel, tiles_per_seq=tiles_per_seq, seq_len=seq_len)
    return pl.pallas_call(
        kern,
        grid=(n // tm,),
        in_specs=[
            pl.BlockSpec((tm, d), tile),
            pl.BlockSpec((tm, convy.shape[1]), tile),
            pl.BlockSpec((tm, o.shape[1]), tile),
            pl.BlockSpec(w_out.shape, const),
            pl.BlockSpec((1, d), const),
            pl.BlockSpec(wr_hi.shape, const),
            pl.BlockSpec(wr_lo.shape, const),
            pl.BlockSpec((1, LANES), const),
            pl.BlockSpec(upper.shape, const),
        ],
        out_specs=[
            pl.BlockSpec((tm, d), tile),
            *[pl.BlockSpec((tm, LANES), tile)] * ROW_PLANES,
            pl.BlockSpec((ROUTE_COLS, tm), lambda i: (0, i)),
            pl.BlockSpec((ROUTE_ROWS, LANES), const),
        ],
        out_shape=[
            jax.ShapeDtypeStruct((n, d), F32),
            *[jax.ShapeDtypeStruct((n, LANES), jnp.uint32)] * ROW_PLANES,
            jax.ShapeDtypeStruct((ROUTE_COLS, n), F32),
            jax.ShapeDtypeStruct((ROUTE_ROWS, LANES), F32),
        ],
        scratch_shapes=[pltpu.VMEM((ROUTE_ROWS, LANES), F32)],
        compiler_params=_cparams(1),
        name="outproj_router",
    )(hp, convy, o, w_out, g2, wr_hi, wr_lo, br, upper)


def _row_copy(src_hbm, src_row, dst_ref, dst_row, sem):
    return pltpu.make_async_copy(src_hbm.at[pl.ds(src_row, 1), :], dst_ref.at[pl.ds(dst_row, 1), :], sem)


def _dispatch_kernel(dest_ref, xn_hbm, xs_hbm, xbuf, sem_in, sem, *, tiles_per_seq, lp_len, tile):
    t = pl.program_id(0)
    base = (t // tiles_per_seq) * lp_len + (t % tiles_per_seq) * tile
    tile_copy = pltpu.make_async_copy(xn_hbm.at[pl.ds(base, tile), :], xbuf, sem_in)
    tile_copy.start()
    tile_copy.wait()

    def issue(r, carry):
        for kk in range(TOP_K):
            _row_copy(xbuf, r, xs_hbm, dest_ref[0, 0, TOP_K * r + kk], sem).start()
        return carry

    lax.fori_loop(0, tile, issue, 0)

    def drain(r, carry):
        for kk in range(TOP_K):
            _row_copy(xbuf, r, xs_hbm, dest_ref[0, 0, TOP_K * r + kk], sem).wait()
        return carry

    lax.fori_loop(0, tile, drain, 0)


def _dispatch(dest_tiles, xn, *, p_rows, tiles_per_seq, lp_len, tile):
    n_tiles = dest_tiles.shape[0]
    kern = functools.partial(_dispatch_kernel, tiles_per_seq=tiles_per_seq, lp_len=lp_len, tile=tile)
    return pl.pallas_call(
        kern,
        grid=(n_tiles,),
        in_specs=[
            pl.BlockSpec((1, 1, TOP_K * tile), lambda t: (t, 0, 0), memory_space=pltpu.SMEM),
            pl.BlockSpec(memory_space=pl.ANY),
        ],
        out_specs=pl.BlockSpec(memory_space=pl.ANY),
        out_shape=jax.ShapeDtypeStruct((p_rows, xn.shape[1]), xn.dtype),
        scratch_shapes=[pltpu.VMEM((tile, xn.shape[1]), xn.dtype), pltpu.SemaphoreType.DMA(()),
                        pltpu.SemaphoreType.DMA(())],
        compiler_params=_cparams(1),
        name="moe_dispatch",
    )(dest_tiles, xn)


def _experts_kernel(be_ref, nv_ref, xs_ref, wg_ref, wu_ref, wd_ref, y_ref, wgb_ref, wub_ref, wdb_ref):
    i = pl.program_id(0)
    e = be_ref[i]
    prev = be_ref[jnp.maximum(i - 1, 0)]

    @pl.when((i == 0) | (e != prev))
    def _():
        wgb_ref[...] = wg_ref[0].astype(BF16)
        wub_ref[...] = wu_ref[0].astype(BF16)
        wdb_ref[...] = wd_ref[0].astype(BF16)

    nv = nv_ref[i]

    @pl.when(nv > 0)
    def _():
        xs = xs_ref[...]
        row = lax.broadcasted_iota(jnp.int32, xs.shape, 0)
        x = jnp.where(row < nv, xs, 0.0).astype(BF16)
        hg = jnp.dot(x, wgb_ref[...], preferred_element_type=F32)
        hu = jnp.dot(x, wub_ref[...], preferred_element_type=F32)
        hid = (hg / (1.0 + jnp.exp(-hg)) * hu).astype(BF16)
        y_ref[...] = jnp.dot(hid, wdb_ref[...], preferred_element_type=F32)

    @pl.when(nv == 0)
    def _():
        y_ref[...] = jnp.zeros_like(y_ref)


def _experts(block_e, nvalid, xs, w_gate, w_up, w_down):
    p_rows, d = xs.shape
    ff = w_gate.shape[2]
    n_blocks = p_rows // MOE_BLOCK
    grid_spec = pltpu.PrefetchScalarGridSpec(
        num_scalar_prefetch=2,
        grid=(n_blocks,),
        in_specs=[
            pl.BlockSpec((MOE_BLOCK, d), lambda i, be, nv: (i, 0)),
            pl.BlockSpec((1, d, ff), lambda i, be, nv: (be[i], 0, 0)),
            pl.BlockSpec((1, d, ff), lambda i, be, nv: (be[i], 0, 0)),
            pl.BlockSpec((1, ff, d), lambda i, be, nv: (be[i], 0, 0)),
        ],
        out_specs=pl.BlockSpec((MOE_BLOCK, d), lambda i, be, nv: (i, 0)),
        scratch_shapes=[
            pltpu.VMEM((d, ff), BF16),
            pltpu.VMEM((d, ff), BF16),
            pltpu.VMEM((ff, d), BF16),
        ],
    )
    return pl.pallas_call(
        _experts_kernel,
        grid_spec=grid_spec,
        out_shape=jax.ShapeDtypeStruct((p_rows, d), F32),
        compiler_params=_cparams(1),
        name="moe_experts",
    )(block_e, nvalid, xs, w_gate, w_up, w_down)


def _combine_kernel(dest_ref, gates_ref, h1_hbm, y_hbm, out_ref, hbuf, ya, yb, sem_h, sem_a, sem_b,
                    *, lp_len, tile):
    b = pl.program_id(0)
    i = pl.program_id(1)
    start = b * lp_len + N_META + i * tile
    h_copy = pltpu.make_async_copy(h1_hbm.at[pl.ds(start, tile), :], hbuf, sem_h)
    h_copy.start()

    def issue(r, carry):
        _row_copy(y_hbm, dest_ref[0, 0, TOP_K * r], ya, r, sem_a).start()
        _row_copy(y_hbm, dest_ref[0, 0, TOP_K * r + 1], yb, r, sem_b).start()
        return carry

    lax.fori_loop(0, tile, issue, 0)

    def drain(r, carry):
        _row_copy(y_hbm, dest_ref[0, 0, TOP_K * r], ya, r, sem_a).wait()
        _row_copy(y_hbm, dest_ref[0, 0, TOP_K * r + 1], yb, r, sem_b).wait()
        return carry

    lax.fori_loop(0, tile, drain, 0)
    h_copy.wait()
    g = gates_ref[0]
    out_ref[0] = hbuf[...] + g[:, 0:1] * ya[...] + g[:, 1:2] * yb[...]


def _combine(dest_tiles, gates, h1, y, *, batch, seq, lp_len, tile):
    d = h1.shape[1]
    nt = seq // tile
    kern = functools.partial(_combine_kernel, lp_len=lp_len, tile=tile)
    return pl.pallas_call(
        kern,
        grid=(batch, nt),
        in_specs=[
            pl.BlockSpec((1, 1, TOP_K * tile), lambda b, i: (b * nt + i, 0, 0), memory_space=pltpu.SMEM),
            pl.BlockSpec((1, tile, TOP_K), lambda b, i: (b, i, 0)),
            pl.BlockSpec(memory_space=pl.ANY),
            pl.BlockSpec(memory_space=pl.ANY),
        ],
        out_specs=pl.BlockSpec((1, tile, d), lambda b, i: (b, i, 0)),
        out_shape=jax.ShapeDtypeStruct((batch, seq, d), F32),
        scratch_shapes=[
            pltpu.VMEM((tile, d), F32),
            pltpu.VMEM((tile, d), F32),
            pltpu.VMEM((tile, d), F32),
            pltpu.SemaphoreType.DMA(()),
            pltpu.SemaphoreType.DMA(()),
            pltpu.SemaphoreType.DMA(()),
        ],
        compiler_params=_cparams(2),
        name="moe_combine",
    )(dest_tiles, gates, h1, y)


def _sc_workers():
    info = plsc.get_sparse_core_info()
    return info.num_cores, info.num_cores * info.num_subcores


def _sc_scatter_rows(planes, idx_a, idx_b, out_rows):
    n_win = idx_a.shape[0]
    n_cores, n_workers = _sc_workers()
    trips = -(-n_win // n_workers)
    mesh = plsc.VectorSubcoreMesh(core_axis_name="c", subcore_axis_name="s")

    def body(*refs):
        xs = refs[0:ROW_PLANES]
        ia_hbm, ib_hbm = refs[ROW_PLANES:ROW_PLANES + 2]
        outs = refs[ROW_PLANES + 2:2 * ROW_PLANES + 2]
        ia_v, ib_v, buf, sem = refs[2 * ROW_PLANES + 2:]
        wid = lax.axis_index("s") * n_cores + lax.axis_index("c")

        def step(t, carry):
            g = wid + t * n_workers

            @pl.when(g < n_win)
            def _():
                pltpu.sync_copy(ia_hbm.at[g], ia_v)
                pltpu.sync_copy(ib_hbm.at[g], ib_v)
                row0 = pl.multiple_of(g * SC_WINDOW, SC_WINDOW)
                loads = [pltpu.async_copy(xs[c].at[pl.ds(row0, SC_WINDOW)], buf.at[c], sem)
                         for c in range(ROW_PLANES)]
                for cp in loads:
                    cp.wait()
                stores = [pltpu.async_copy(buf.at[c], outs[c].at[iv], sem)
                          for c in range(ROW_PLANES) for iv in (ia_v, ib_v)]
                for cp in stores:
                    cp.wait()

            return carry

        lax.fori_loop(0, trips, step, 0)

    kern = pl.kernel(
        body,
        out_type=[jax.ShapeDtypeStruct((out_rows, LANES), jnp.uint32)] * ROW_PLANES,
        mesh=mesh,
        scratch_types=[
            pltpu.VMEM((SC_WINDOW,), jnp.int32),
            pltpu.VMEM((SC_WINDOW,), jnp.int32),
            pltpu.VMEM((ROW_PLANES, SC_WINDOW, LANES), jnp.uint32),
            pltpu.SemaphoreType.DMA,
        ],
        name="moe_dispatch_sc",
    )
    return kern(*planes, idx_a, idx_b)


def _sc_gather_rows(planes, idx_a, idx_b):
    n_win = idx_a.shape[0]
    n_cores, n_workers = _sc_workers()
    trips = -(-n_win // n_workers)
    mesh = plsc.VectorSubcoreMesh(core_axis_name="c", subcore_axis_name="s")

    def body(*refs):
        ys = refs[0:ROW_PLANES]
        ia_hbm, ib_hbm = refs[ROW_PLANES:ROW_PLANES + 2]
        outs_a = refs[ROW_PLANES + 2:2 * ROW_PLANES + 2]
        outs_b = refs[2 * ROW_PLANES + 2:3 * ROW_PLANES + 2]
        iv, buf, sem = refs[3 * ROW_PLANES + 2:]
        wid = lax.axis_index("s") * n_cores + lax.axis_index("c")

        def step(t, carry):
            g = wid + t * n_workers

            @pl.when(g < n_win)
            def _():
                row0 = pl.multiple_of(g * SC_WINDOW, SC_WINDOW)
                for i_hbm, outs in ((ia_hbm, outs_a), (ib_hbm, outs_b)):
                    pltpu.sync_copy(i_hbm.at[g], iv)
                    loads = [pltpu.async_copy(ys[c].at[iv], buf.at[c], sem) for c in range(ROW_PLANES)]
                    for cp in loads:
                        cp.wait()
                    stores = [pltpu.async_copy(buf.at[c], outs[c].at[pl.ds(row0, SC_WINDOW)], sem)
                              for c in range(ROW_PLANES)]
                    for cp in stores:
                        cp.wait()

            return carry

        lax.fori_loop(0, trips, step, 0)

    n_rows = n_win * SC_WINDOW
    kern = pl.kernel(
        body,
        out_type=[jax.ShapeDtypeStruct((n_rows, LANES), jnp.uint32)] * (2 * ROW_PLANES),
        mesh=mesh,
        scratch_types=[
            pltpu.VMEM((SC_WINDOW,), jnp.int32),
            pltpu.VMEM((ROW_PLANES, SC_WINDOW, LANES), jnp.uint32),
            pltpu.SemaphoreType.DMA,
        ],
        name="moe_gather_sc",
    )
    res = kern(*planes, idx_a, idx_b)
    return res[:ROW_PLANES], res[ROW_PLANES:]


def _experts_kernel(be_ref, nv_ref, x0_ref, x1_ref, x2_ref, x3_ref, wg_ref, wu_ref, wd_ref,
                    y0_ref, y1_ref, y2_ref, y3_ref, wgb_ref, wub_ref, wdb_ref):
    i = pl.program_id(0)
    e = be_ref[i]
    prev = be_ref[jnp.maximum(i - 1, 0)]
    y_refs = (y0_ref, y1_ref, y2_ref, y3_ref)

    @pl.when((i == 0) | (e != prev))
    def _():
        wgb_ref[...] = wg_ref[0].astype(BF16)
        wub_ref[...] = wu_ref[0].astype(BF16)
        wdb_ref[...] = wd_ref[0].astype(BF16)

    nv = nv_ref[i]
    half = MOE_BLOCK // 2

    def mlp(rows):
        xs = _unpack_rows([r[rows, :] for r in (x0_ref, x1_ref, x2_ref, x3_ref)])
        row = lax.broadcasted_iota(jnp.int32, xs.shape, 0)
        x = jnp.where(row < nv, xs, jnp.zeros_like(xs))
        hg = jnp.dot(x, wgb_ref[...], preferred_element_type=F32)
        hu = jnp.dot(x, wub_ref[...], preferred_element_type=F32)
        hid = (hg / (1.0 + jnp.exp(-hg)) * hu).astype(BF16)
        yw = _pack_rows(jnp.dot(hid, wdb_ref[...], preferred_element_type=F32))
        for c, ref in enumerate(y_refs):
            ref[rows, :] = yw[:, c * LANES:(c + 1) * LANES]

    def clear(rows):
        for ref in y_refs:
            ref[rows, :] = jnp.zeros((rows.size, LANES), ref.dtype)

    @pl.when(nv > half)
    def _():
        mlp(pl.ds(0, MOE_BLOCK))

    @pl.when((nv > 0) & (nv <= half))
    def _():
        mlp(pl.ds(0, half))
        clear(pl.ds(half, half))

    @pl.when(nv == 0)
    def _():
        clear(pl.ds(0, MOE_BLOCK))


def _experts(block_e, nvalid, xs_planes, w_gate, w_up, w_down):
    n_blocks = block_e.shape[0]
    _, d, ff = w_gate.shape
    blk = lambda i, be, nv: (i, 0)
    wsel = lambda i, be, nv: (be[i], 0, 0)
    grid_spec = pltpu.PrefetchScalarGridSpec(
        num_scalar_prefetch=2,
        grid=(n_blocks,),
        in_specs=[
            *[pl.BlockSpec((MOE_BLOCK, LANES), blk)] * ROW_PLANES,
            pl.BlockSpec((1, d, ff), wsel),
            pl.BlockSpec((1, d, ff), wsel),
            pl.BlockSpec((1, ff, d), wsel),
        ],
        out_specs=[pl.BlockSpec((MOE_BLOCK, LANES), blk)] * ROW_PLANES,
        scratch_shapes=[
            pltpu.VMEM((d, ff), BF16),
            pltpu.VMEM((d, ff), BF16),
            pltpu.VMEM((ff, d), BF16),
        ],
    )
    return pl.pallas_call(
        _experts_kernel,
        grid_spec=grid_spec,
        out_shape=[jax.ShapeDtypeStruct((n_blocks * MOE_BLOCK, LANES), jnp.uint32)] * ROW_PLANES,
        compiler_params=_cparams(1),
        name="moe_experts",
    )(block_e, nvalid, *xs_planes, w_gate, w_up, w_down)


def _combine_kernel(gates_ref, h1_hbm, *refs, lp_len, tile):
    a_refs = refs[0:ROW_PLANES]
    b_refs = refs[ROW_PLANES:2 * ROW_PLANES]
    out_ref, hbuf, sem_h = refs[2 * ROW_PLANES:]
    nt = pl.num_programs(1)
    step = pl.program_id(0) * nt + pl.program_id(1)
    last = pl.num_programs(0) * nt - 1

    def h_copy(s, slot):
        start = (s // nt) * lp_len + N_META + (s % nt) * tile
        return pltpu.make_async_copy(h1_hbm.at[pl.ds(start, tile), :], hbuf.at[slot], sem_h.at[slot])

    slot = step % 2

    @pl.when(step == 0)
    def _():
        h_copy(step, slot).start()

    @pl.when(step < last)
    def _():
        h_copy(step + 1, 1 - slot).start()

    ya = _unpack_rows([r[...] for r in a_refs]).astype(F32)
    yb = _unpack_rows([r[...] for r in b_refs]).astype(F32)
    g = jnp.concatenate([gates_ref[...]] * (LANES // ROUTE_COLS), axis=0).T
    moe = g[:, 0:1] * ya + g[:, 1:2] * yb
    h_copy(step, slot).wait()
    out_ref[0] = hbuf[slot] + moe


def _combine(gates, h1, a_planes, b_planes, *, batch, seq, lp_len, tile):
    d = h1.shape[1]
    nt = seq // tile
    kern = functools.partial(_combine_kernel, lp_len=lp_len, tile=tile)
    rows = lambda b, i: (b * nt + i, 0)
    return pl.pallas_call(
        kern,
        grid=(batch, nt),
        in_specs=[
            pl.BlockSpec((ROUTE_COLS, tile), lambda b, i: (0, b * nt + i)),
            pl.BlockSpec(memory_space=pl.ANY),
            *[pl.BlockSpec((tile, LANES), rows)] * (2 * ROW_PLANES),
        ],
        out_specs=pl.BlockSpec((1, tile, d), lambda b, i: (b, i, 0)),
        out_shape=jax.ShapeDtypeStruct((batch, seq, d), F32),
        scratch_shapes=[pltpu.VMEM((2, tile, d), F32), pltpu.SemaphoreType.DMA((2,))],
        compiler_params=_cparams(2),
        name="moe_combine",
    )(gates, h1, *a_planes, *b_planes)


def _rope_tables(length, lp_len):
    half = ROPE_DIM // 2
    pos = jnp.arange(length, dtype=F32)
    inv_freq = ROPE_THETA ** (-jnp.arange(0, ROPE_DIM, 2, dtype=F32) / ROPE_DIM)
    ang = pos[:, None] * inv_freq[None, :]
    cos = jnp.cos(ang)
    sin = jnp.sin(ang)
    ones = jnp.ones((length, HEAD_DIM - ROPE_DIM), F32)
    zeros_h = jnp.zeros((length, half), F32)
    zeros_r = jnp.zeros((length, HEAD_DIM - ROPE_DIM), F32)
    c = jnp.concatenate([cos, cos, ones], axis=1)
    s1 = jnp.concatenate([zeros_h, sin, zeros_r], axis=1)
    s2 = jnp.concatenate([-sin, zeros_h, zeros_r], axis=1)
    pad = ((0, lp_len - length), (0, 0))
    rep = LANES // HEAD_DIM
    return tuple(jnp.pad(jnp.tile(t, (1, rep)), pad) for t in (c, s1, s2))


def _layer(hp, l, batch, length, lp_len, tm, norm1_g, w_in, conv_w, q_norm_g, k_norm_g, lambda_q1, lambda_k1,
           lambda_q2, lambda_k2, subln_g, w_out, norm2_g, w_router_group, b_router_group, w_router_expert,
           b_router_expert, w_gate, w_up, w_down, rope, last):
    n, d = hp.shape
    tiles_per_seq = lp_len // tm
    cw = conv_w.shape[2]
    qw = N_HEADS * 2 * HEAD_DIM
    lam_init = 0.8 - 0.6 * math.exp(-0.3 * l)

    reps = qw // HEAD_DIM
    gq = jnp.tile(q_norm_g[l] * (HEAD_DIM ** -0.5 * LOG2E), reps)[None, :]
    gk = jnp.tile(k_norm_g[l], reps)[None, :]
    seg = jnp.arange(qw) // HEAD_DIM
    bd = (seg[:, None] == seg[None, :]).astype(BF16)
    convy, q, k, v = _inproj(hp, norm1_g[l][None, :], w_in[l].astype(BF16), conv_w[l], gq, gk, bd, *rope,
                             tiles_per_seq=tiles_per_seq, tm=tm)

    lamp = jnp.stack([lambda_q1[l], lambda_k1[l], lambda_q2[l], lambda_k2[l]]).astype(F32)
    o = _attention(q, k, v, lamp, subln_g[l][None, :], batch=batch, lp_len=lp_len, tq=tm, lam_init=lam_init)

    lane_pad = LANES - N_GROUPS - N_EXPERTS
    wr = jnp.pad(jnp.concatenate([w_router_group[l], w_router_expert[l]], axis=1), ((0, 0), (0, lane_pad)))
    wr_hi = wr.astype(BF16)
    wr_lo = (wr - wr_hi.astype(F32)).astype(BF16)
    br = jnp.pad(jnp.concatenate([b_router_group[l], b_router_expert[l]]), (0, lane_pad))[None, :]
    ridx = jnp.arange(tm)
    upper = (ridx[:, None] < ridx[None, :]).astype(BF16)
    h1, *rest = _outproj_t(hp, convy, o, w_out[l].astype(BF16), norm2_g[l][None, :], wr_hi, wr_lo, br,
                           upper, tiles_per_seq=tiles_per_seq, seq_len=length, tm=tm)
    x_planes = rest[:ROW_PLANES]
    route, cnt = rest[ROW_PLANES:]

    route = route.reshape(ROUTE_COLS, batch, lp_len)
    eid = route[0:TOP_K].astype(jnp.int32)
    gates = route[TOP_K:2 * TOP_K]
    rank = route[2 * TOP_K:3 * TOP_K].astype(jnp.int32)
    counts = cnt[EXPERT_LANE0:EXPERT_LANE0 + N_EXPERTS, 0].astype(jnp.int32)
    a = batch * length * TOP_K
    n_blocks = -(-a // MOE_BLOCK) + N_EXPERTS
    p_rows = n_blocks * MOE_BLOCK
    padded = (counts + MOE_BLOCK - 1) // MOE_BLOCK * MOE_BLOCK
    pends = jnp.cumsum(padded)
    pstarts = pends - padded

    def lookup(table, idx):
        return sum(jnp.where(idx == e, table[e], 0) for e in range(N_EXPERTS))

    n_pad = lp_len - length
    pos = jnp.arange(lp_len, dtype=jnp.int32)[None, None, :]
    spare = p_rows + ((jnp.arange(batch, dtype=jnp.int32)[None, :, None] * n_pad + (pos - length)) * TOP_K
                      + jnp.arange(TOP_K, dtype=jnp.int32)[:, None, None])
    dest = jnp.where(pos < length, lookup(pstarts, eid) + rank, spare)
    spare_rows = -(-(batch * n_pad * TOP_K) // MOE_BLOCK) * MOE_BLOCK
    blk0 = jnp.arange(n_blocks, dtype=jnp.int32) * MOE_BLOCK
    block_e = jnp.minimum(jnp.sum((pends[None, :] <= blk0[:, None]).astype(jnp.int32), axis=1), N_EXPERTS - 1)
    nvalid = jnp.clip(lookup(counts, block_e) - (blk0 - lookup(pstarts, block_e)), 0, MOE_BLOCK)

    assert (batch * lp_len) % SC_WINDOW == 0
    xs_planes = _sc_scatter_rows(x_planes, dest[0].reshape(-1, SC_WINDOW), dest[1].reshape(-1, SC_WINDOW),
                                 p_rows + spare_rows)
    y_planes = _experts(block_e, nvalid, xs_planes, w_gate[l], w_up[l], w_down[l])

    if not last:
        raise NotImplementedError("only the final layer's combine (which drops the meta tokens) is implemented")
    seq = length - N_META
    assert (batch * seq) % SC_WINDOW == 0
    dest_x = dest[:, :, N_META:length]
    a_planes, b_planes = _sc_gather_rows(y_planes, dest_x[0].reshape(-1, SC_WINDOW),
                                         dest_x[1].reshape(-1, SC_WINDOW))
    ctile = _largest_tile(seq, 512, LANES)
    gates_x = jnp.pad(gates[:, :, N_META:length].reshape(TOP_K, batch * seq), ((0, ROUTE_COLS - TOP_K), (0, 0)))
    return _combine(gates_x, h1, a_planes, b_planes, batch=batch, seq=seq, lp_len=lp_len, tile=ctile)


def kernel(x, meta_tokens, norm1_g, w_in, conv_w, q_norm_g, k_norm_g, lambda_q1, lambda_k1, lambda_q2, lambda_k2,
           subln_g, w_out, norm2_g, w_router_group, b_router_group, w_router_expert, b_router_expert, w_gate,
           w_up, w_down):
    b, s, d = x.shape
    depth = w_in.shape[0]
    assert depth == 1, "a single layer is supported"
    length = s + N_META
    tm = TOKEN_TILE if length >= TOKEN_TILE else LANES
    lp_len = -(-length // tm) * tm
    hp = jnp.concatenate([jnp.broadcast_to(meta_tokens[None].astype(x.dtype), (b, N_META, d)), x,
                          jnp.zeros((b, lp_len - length, d), x.dtype)], axis=1).reshape(b * lp_len, d)
    rope = _rope_tables(length, lp_len)
    return _layer(hp, 0, b, length, lp_len, tm, norm1_g, w_in, conv_w, q_norm_g, k_norm_g, lambda_q1, lambda_k1,
                  lambda_q2, lambda_k2, subln_g, w_out, norm2_g, w_router_group, b_router_group,
                  w_router_expert, b_router_expert, w_gate, w_up, w_down, rope, last=True)
```

```python
import functools
import math

import jax
import jax.numpy as jnp
from jax import lax
from jax.experimental import pallas as pl
from jax.experimental.pallas import tpu as pltpu
from jax.experimental.pallas import tpu_sc as plsc

F32 = jnp.float32
BF16 = jnp.bfloat16

N_META = 16
CONV_K = 3
N_HEADS = 4
HEAD_DIM = 64
V_DIM = 2 * HEAD_DIM
ROPE_DIM = HEAD_DIM // 4
ROPE_THETA = 500000.0
N_GROUPS = 4
EXPERTS_PER_GROUP = 8
N_EXPERTS = N_GROUPS * EXPERTS_PER_GROUP
TOP_K = 2
EPS = 1e-6
LOG2E = 1.4426950408889634

LANES = 128
TOKEN_TILE = 640
INPROJ_CHAINS = 2
ATTN_HEADS_PER_STEP = 4
ATTN_WIDE_CHUNKS = 2
MOE_BLOCK = 512
ROUTE_COLS = 8
ROUTE_ROWS = 64
ROW_PLANES = 4
SC_WINDOW = 128
EXPERT_LANE0 = N_GROUPS
NEG_BIG = -1e30
VMEM_LIMIT = 56 * 1024 * 1024


def _largest_tile(n, cap, mult):
    for t in range(min(cap, n), 0, -1):
        if n % t == 0 and t % mult == 0:
            return t
    raise ValueError(f"no tile for {n}")


def _cparams(n_axes, flags=None):
    return pltpu.CompilerParams(dimension_semantics=("arbitrary",) * n_axes,
                                vmem_limit_bytes=VMEM_LIMIT, flags=flags)


def _inproj_kernel(x_ref, g1_ref, win_ref, convw_ref, gq_ref, gk_ref, bd_ref, rc_ref, rs1_ref, rs2_ref,
                   convy_ref, q_ref, k_ref, v_ref, carry_ref, *, tiles_per_seq, cw, qw):
    i = pl.program_id(0)
    tm = x_ref.shape[0]
    sub = tm // INPROJ_CHAINS
    q0 = 3 * cw
    w = convw_ref[...]

    @pl.when(i % tiles_per_seq == 0)
    def _():
        carry_ref[...] = jnp.zeros_like(carry_ref)

    prev = carry_ref[...]

    for chain in range(INPROJ_CHAINS):
        rows = pl.ds(chain * sub, sub)
        x = x_ref[rows, :]
        ms = jnp.mean(x * x, axis=-1, keepdims=True)
        xn = (x * lax.rsqrt(ms + EPS) * g1_ref[...]).astype(BF16)

        def proj(lo, hi):
            return jnp.dot(xn, win_ref[:, lo:hi], preferred_element_type=F32)

        u_conv = proj(0, q0)
        u_q = proj(q0, q0 + qw)

        z = u_conv[:, cw:2 * cw] * u_conv[:, 2 * cw:3 * cw]
        p1 = prev[7:8]
        p2 = prev[6:7]
        row = lax.broadcasted_iota(jnp.int32, z.shape, 0)
        z1 = jnp.where(row == 0, p1, pltpu.roll(z, 1, axis=0))
        z2 = jnp.where(row == 0, p2, jnp.where(row == 1, p1, pltpu.roll(z, 2, axis=0)))
        prev = z[sub - 8:sub]
        conv = w[0:1] * z2 + w[1:2] * z1 + w[2:3] * z
        convy_ref[rows, :] = (u_conv[:, 0:cw] * conv).astype(BF16)

        rc = rc_ref[rows, :]
        rs1 = rs1_ref[rows, :]
        rs2 = rs2_ref[rows, :]

        def norm_rope(t, g_ref):
            ss = jnp.dot((t * t).astype(BF16), bd_ref[...], preferred_element_type=F32)
            tn = t * lax.rsqrt(ss * (1.0 / HEAD_DIM) + EPS) * g_ref[...]
            outs = []
            for c in range(qw // LANES):
                ch = tn[:, c * LANES:(c + 1) * LANES]
                outs.append(ch * rc + pltpu.roll(ch, ROPE_DIM // 2, axis=1) * rs1
                            + pltpu.roll(ch, LANES - ROPE_DIM // 2, axis=1) * rs2)
            return jnp.concatenate(outs, axis=1).astype(BF16)

        u_k = proj(q0 + qw, q0 + 2 * qw)
        q_ref[rows, :] = norm_rope(u_q, gq_ref)
        u_v = proj(q0 + 2 * qw, win_ref.shape[1])
        k_ref[rows, :] = norm_rope(u_k, gk_ref)
        v_ref[rows, :] = u_v.astype(BF16)

    carry_ref[...] = prev


def _inproj(hp, g1, w_in, conv_w, gq, gk, bd, rc, rs1, rs2, *, tiles_per_seq, tm):
    n, d = hp.shape
    cw = conv_w.shape[1]
    qw = gq.shape[1]
    aw = w_in.shape[1] - 3 * cw - 2 * qw
    const = lambda i: (0, 0)
    tile = lambda i: (i, 0)
    pos = lambda i: (i % tiles_per_seq, 0)
    kern = functools.partial(_inproj_kernel, tiles_per_seq=tiles_per_seq, cw=cw, qw=qw)
    return pl.pallas_call(
        kern,
        grid=(n // tm,),
        in_specs=[
            pl.BlockSpec((tm, d), tile),
            pl.BlockSpec((1, d), const),
            pl.BlockSpec(w_in.shape, const),
            pl.BlockSpec(conv_w.shape, const),
            pl.BlockSpec((1, qw), const),
            pl.BlockSpec((1, qw), const),
            pl.BlockSpec(bd.shape, const),
            pl.BlockSpec((tm, LANES), pos),
            pl.BlockSpec((tm, LANES), pos),
            pl.BlockSpec((tm, LANES), pos),
        ],
        out_specs=[
            pl.BlockSpec((tm, cw), tile),
            pl.BlockSpec((tm, qw), tile),
            pl.BlockSpec((tm, qw), tile),
            pl.BlockSpec((tm, aw), tile),
        ],
        out_shape=[
            jax.ShapeDtypeStruct((n, cw), BF16),
            jax.ShapeDtypeStruct((n, qw), BF16),
            jax.ShapeDtypeStruct((n, qw), BF16),
            jax.ShapeDtypeStruct((n, aw), BF16),
        ],
        scratch_shapes=[pltpu.VMEM((8, cw), F32)],
        compiler_params=_cparams(1),
        name="inproj",
    )(hp, g1, w_in, conv_w, gq, gk, bd, rc, rs1, rs2)


def _attn_kernel(q_ref, k_ref, v_ref, lamp_ref, sg_ref, o_ref, qs_ref, m_ref, l_ref, acc_ref, *, lam_init):
    qi = pl.program_id(2)
    tq = q_ref.shape[0]
    n_heads = q_ref.shape[1] // LANES
    n_chains = 2 * n_heads
    lane = lax.broadcasted_iota(jnp.int32, (tq, LANES), 1)
    for h in range(n_heads):
        q = q_ref[:, h * LANES:(h + 1) * LANES]
        zero = jnp.zeros_like(q)
        qs_ref[pl.ds(2 * h * tq, tq), :] = jnp.where(lane < HEAD_DIM, q, zero)
        qs_ref[pl.ds((2 * h + 1) * tq, tq), :] = jnp.where(lane >= HEAD_DIM, q, zero)
    m_ref[...] = jnp.full_like(m_ref, NEG_BIG)
    l_ref[...] = jnp.zeros_like(l_ref)
    acc_ref[...] = jnp.zeros_like(acc_ref)

    def scores(off, width, which):
        h = which // 2
        kc = k_ref[pl.ds(off, width), h * LANES:(h + 1) * LANES]
        return lax.dot_general(qs_ref[pl.ds(which * tq, tq), :], kc, (((1,), (1,)), ((), ())),
                               preferred_element_type=F32)

    def update(off, width, which, s, masked):
        h = which // 2
        vc = jnp.concatenate([v_ref[pl.ds(off, width), h * LANES:(h + 1) * LANES],
                              jnp.ones((width, LANES), BF16)], axis=1)
        rows = pl.ds(which * tq, tq)
        if masked:
            r = lax.broadcasted_iota(jnp.int32, s.shape, 0)
            c = lax.broadcasted_iota(jnp.int32, s.shape, 1)
            s = jnp.where(c <= r, s, NEG_BIG)
        m_prev = m_ref[rows, :]
        m_new = jnp.maximum(m_prev, jnp.max(s, axis=-1, keepdims=True))
        alpha = jnp.exp2(m_prev - m_new)
        p = jnp.exp2((s - jnp.tile(m_new, (1, width // LANES))).astype(BF16))
        pv = jnp.dot(p, vc, preferred_element_type=F32)
        l_ref[rows, :] = alpha * l_ref[rows, :] + pv[:, LANES:]
        acc_ref[rows, :] = alpha * acc_ref[rows, :] + pv[:, :LANES]
        m_ref[rows, :] = m_new

    def chunk(off, width, masked):
        s_next = scores(off, width, 0)
        for c in range(n_chains):
            s = s_next
            if c + 1 < n_chains:
                s_next = scores(off, width, c + 1)
            update(off, width, c, s, masked)

    wide = ATTN_WIDE_CHUNKS * tq

    def body(j, carry):
        chunk(pl.multiple_of(j * wide, wide), wide, False)
        return carry

    n_wide = qi // ATTN_WIDE_CHUNKS
    lax.fori_loop(0, n_wide, body, 0)
    for extra in range(ATTN_WIDE_CHUNKS - 1):
        @pl.when(n_wide * ATTN_WIDE_CHUNKS + extra < qi)
        def _():
            chunk(pl.multiple_of((n_wide * ATTN_WIDE_CHUNKS + extra) * tq, tq), tq, False)

    chunk(pl.multiple_of(qi * tq, tq), tq, True)

    lp = lamp_ref[...]
    lam = (jnp.exp(jnp.sum(lp[0:1] * lp[1:2], axis=-1, keepdims=True))
           - jnp.exp(jnp.sum(lp[2:3] * lp[3:4], axis=-1, keepdims=True)) + lam_init)
    for h in range(n_heads):
        rows = pl.ds(2 * h * tq, 2 * tq)
        o_all = acc_ref[rows, :] / l_ref[rows, :]
        o = o_all[0:tq] - lam * o_all[tq:2 * tq]
        ms = jnp.mean(o * o, axis=-1, keepdims=True)
        o_ref[:, h * LANES:(h + 1) * LANES] = (o * lax.rsqrt(ms + EPS) * sg_ref[...]
                                               * (1.0 - lam_init)).astype(BF16)


def _attention(q, k, v, lamp, sg, *, batch, lp_len, tq, lam_init):
    n, qw = q.shape
    nq = lp_len // tq
    hw = ATTN_HEADS_PER_STEP * LANES
    chains = 2 * ATTN_HEADS_PER_STEP
    kern = functools.partial(_attn_kernel, lam_init=lam_init)
    return pl.pallas_call(
        kern,
        grid=(batch, qw // hw, nq),
        in_specs=[
            pl.BlockSpec((tq, hw), lambda b, h, i: (b * nq + i, h)),
            pl.BlockSpec((lp_len, hw), lambda b, h, i: (b, h)),
            pl.BlockSpec((lp_len, hw), lambda b, h, i: (b, h)),
            pl.BlockSpec(lamp.shape, lambda b, h, i: (0, 0)),
            pl.BlockSpec(sg.shape, lambda b, h, i: (0, 0)),
        ],
        out_specs=pl.BlockSpec((tq, hw), lambda b, h, i: (b * nq + i, h)),
        out_shape=jax.ShapeDtypeStruct((n, v.shape[1]), BF16),
        scratch_shapes=[
            pltpu.VMEM((chains * tq, LANES), BF16),
            pltpu.VMEM((chains * tq, LANES), F32),
            pltpu.VMEM((chains * tq, LANES), F32),
            pltpu.VMEM((chains * tq, LANES), F32),
        ],
        compiler_params=_cparams(3),
        name="diffattn",
    )(q, k, v, lamp, sg)


def _pack_rows(x):
    w = x.shape[1] // 2
    lo = lax.bitcast_convert_type(x[:, :w].astype(BF16).astype(F32), jnp.uint32)
    hi = lax.bitcast_convert_type(x[:, w:].astype(BF16).astype(F32), jnp.uint32)
    return lax.shift_right_logical(lo, jnp.uint32(16)) | (hi & jnp.uint32(0xFFFF0000))


def _unpack_rows(planes):
    w = jnp.concatenate(planes, axis=1)
    lo = lax.bitcast_convert_type(lax.shift_left(w, jnp.uint32(16)), F32)
    hi = lax.bitcast_convert_type(w & jnp.uint32(0xFFFF0000), F32)
    return jnp.concatenate([lo, hi], axis=1).astype(BF16)


def _outproj_kernel(hp_ref, cy_ref, o_ref, wout_ref, g2_ref, wrh_ref, wrl_ref, br_ref, tri_ref,
                    h1_ref, xp0_ref, xp1_ref, xp2_ref, xp3_ref, route_ref, cnt_ref, run_ref,
                    *, tiles_per_seq, seq_len):
    i = pl.program_id(0)
    tm = hp_ref.shape[0]
    sub = tri_ref.shape[0]
    wr_both = jnp.concatenate([wrh_ref[...], wrl_ref[...]], axis=1)

    @pl.when(i == 0)
    def _():
        run_ref[...] = jnp.zeros_like(run_ref)

    def project(rows):
        mix = jnp.concatenate([cy_ref[rows, :], o_ref[rows, :]], axis=1)
        h1 = hp_ref[rows, :] + jnp.dot(mix, wout_ref[...], preferred_element_type=F32)
        h1_ref[rows, :] = h1
        ms = jnp.mean(h1 * h1, axis=-1, keepdims=True)
        xn = h1 * lax.rsqrt(ms + EPS) * g2_ref[...]
        xw = _pack_rows(xn)
        for c, ref in enumerate((xp0_ref, xp1_ref, xp2_ref, xp3_ref)):
            ref[rows, :] = xw[:, c * LANES:(c + 1) * LANES]
        x_hi = xn.astype(BF16)
        x_lo = (xn - x_hi.astype(F32)).astype(BF16)
        hi_both = jnp.dot(x_hi, wr_both, preferred_element_type=F32)
        return (hi_both[:, :LANES] + hi_both[:, LANES:]
                + jnp.dot(x_lo, wrh_ref[...], preferred_element_type=F32) + br_ref[...])

    chains = [pl.ds(c * sub, sub) for c in range(tm // sub)]
    all_logits = [project(rows) for rows in chains]
    run = run_ref[0:1, :]
    for c, (rows, logits) in enumerate(zip(chains, all_logits)):
        run = _route_rows(i, c, rows, logits, run, tri_ref, route_ref, tiles_per_seq=tiles_per_seq,
                          seq_len=seq_len, tm=tm)
    run_ref[...] = jnp.broadcast_to(run, run_ref.shape)
    cnt_ref[...] = jnp.broadcast_to(run, cnt_ref.shape)


def _route_rows(i, c, rows, logits, run, tri_ref, route_ref, *, tiles_per_seq, seq_len, tm):
    sub = logits.shape[0]
    lane = lax.broadcasted_iota(jnp.int32, logits.shape, 1)
    big = jnp.int32(4 * LANES)

    def first_argmax(vals, vmax):
        return jnp.min(jnp.where(vals == vmax, lane, big), axis=-1, keepdims=True)

    gl = jnp.where(lane < N_GROUPS, logits, NEG_BIG)
    gmax = jnp.max(gl, axis=-1, keepdims=True)
    g_val = 1.0 / jnp.sum(jnp.exp(gl - gmax), axis=-1, keepdims=True)
    g_idx = first_argmax(gl, gmax)
    lo = EXPERT_LANE0 + EXPERTS_PER_GROUP * g_idx
    el = jnp.where((lane >= lo) & (lane < lo + EXPERTS_PER_GROUP), logits, NEG_BIG)
    m1 = jnp.max(el, axis=-1, keepdims=True)
    i1 = first_argmax(el, m1)
    el2 = jnp.where(lane == i1, NEG_BIG, el)
    m2 = jnp.max(el2, axis=-1, keepdims=True)
    i2 = first_argmax(el2, m2)
    r = jnp.exp(m2 - m1)
    gate1 = g_val / (1.0 + r)
    gate2 = g_val * r / (1.0 + r)

    prow = (i % tiles_per_seq) * tm + c * sub + lax.broadcasted_iota(jnp.int32, logits.shape, 0)
    valid = prow < seq_len
    oh1 = jnp.where(valid & (lane == i1), 1.0, 0.0)
    oh2 = jnp.where(valid & (lane == i2), 1.0, 0.0)
    pre = jnp.dot(tri_ref[...], jnp.concatenate([oh1, oh2], axis=1).astype(BF16), preferred_element_type=F32)
    pre1 = pre[:, :LANES]
    pre2 = pre[:, LANES:]
    tot1 = jnp.sum(oh1, axis=0, keepdims=True)
    tot2 = jnp.sum(oh2, axis=0, keepdims=True)
    rank1 = jnp.sum(oh1 * (pre1 + run), axis=-1, keepdims=True)
    rank2 = jnp.sum(oh2 * (pre2 + run + tot1), axis=-1, keepdims=True)

    e1 = (i1 - EXPERT_LANE0).astype(F32)
    e2 = (i2 - EXPERT_LANE0).astype(F32)
    packed = jnp.where(lane == 0, e1, jnp.where(lane == 1, e2, jnp.where(lane == 2, gate1, jnp.where(
        lane == 3, gate2, jnp.where(lane == 4, rank1, jnp.where(lane == 5, rank2, 0.0))))))
    route_ref[rows, :] = packed[:, 0:ROUTE_COLS]
    return run + tot1 + tot2


def _outproj(hp, convy, o, w_out, g2, wr_hi, wr_lo, br, tri, *, tiles_per_seq, seq_len, tm):
    n, d = hp.shape
    const = lambda i: (0, 0)
    tile = lambda i: (i, 0)
    kern = functools.partial(_outproj_kernel, tiles_per_seq=tiles_per_seq, seq_len=seq_len)
    return pl.pallas_call(
        kern,
        grid=(n // tm,),
        in_specs=[
            pl.BlockSpec((tm, d), tile),
            pl.BlockSpec((tm, convy.shape[1]), tile),
            pl.BlockSpec((tm, o.shape[1]), tile),
            pl.BlockSpec(w_out.shape, const),
            pl.BlockSpec((1, d), const),
            pl.BlockSpec(wr_hi.shape, const),
            pl.BlockSpec(wr_lo.shape, const),
            pl.BlockSpec((1, LANES), const),
            pl.BlockSpec(tri.shape, const),
        ],
        out_specs=[
            pl.BlockSpec((tm, d), tile),
            *[pl.BlockSpec((tm, LANES), tile)] * ROW_PLANES,
            pl.BlockSpec((tm, ROUTE_COLS), tile),
            pl.BlockSpec((8, LANES), const),
        ],
        out_shape=[
            jax.ShapeDtypeStruct((n, d), F32),
            *[jax.ShapeDtypeStruct((n, LANES), jnp.uint32)] * ROW_PLANES,
            jax.ShapeDtypeStruct((n, ROUTE_COLS), F32),
            jax.ShapeDtypeStruct((8, LANES), F32),
        ],
        scratch_shapes=[pltpu.VMEM((8, LANES), F32)],
        compiler_params=_cparams(1),
        name="outproj_router",
    )(hp, convy, o, w_out, g2, wr_hi, wr_lo, br, tri)


def _outproj_t_kernel(hp_ref, cy_ref, o_ref, wout_ref, g2_ref, wrh_ref, wrl_ref, br_ref, upper_ref,
                      h1_ref, xp0_ref, xp1_ref, xp2_ref, xp3_ref, route_ref, cnt_ref, run_ref,
                      *, tiles_per_seq, seq_len):
    i = pl.program_id(0)
    tm = hp_ref.shape[0]

    @pl.when(i == 0)
    def _():
        run_ref[...] = jnp.zeros_like(run_ref)

    mix = jnp.concatenate([cy_ref[...], o_ref[...]], axis=1)
    h1 = hp_ref[...] + jnp.dot(mix, wout_ref[...], preferred_element_type=F32)
    h1_ref[...] = h1
    ms = jnp.mean(h1 * h1, axis=-1, keepdims=True)
    xn = h1 * lax.rsqrt(ms + EPS) * g2_ref[...]
    xw = _pack_rows(xn)
    for c, ref in enumerate((xp0_ref, xp1_ref, xp2_ref, xp3_ref)):
        ref[...] = xw[:, c * LANES:(c + 1) * LANES]

    x_hi = xn.astype(BF16)
    x_lo = (xn - x_hi.astype(F32)).astype(BF16)
    hi_both = jnp.dot(x_hi, jnp.concatenate([wrh_ref[...], wrl_ref[...]], axis=1), preferred_element_type=F32)
    logits = (hi_both[:, :LANES] + hi_both[:, LANES:]
              + jnp.dot(x_lo, wrh_ref[...], preferred_element_type=F32) + br_ref[...])

    lt = logits.T[0:ROUTE_ROWS, :]
    row = lax.broadcasted_iota(jnp.int32, lt.shape, 0)
    big = jnp.int32(4 * LANES)

    def first_argmax(vals, vmax):
        return jnp.min(jnp.where(vals == vmax, row, big), axis=0, keepdims=True)

    gl = jnp.where(row < N_GROUPS, lt, NEG_BIG)
    gmax = jnp.max(gl, axis=0, keepdims=True)
    g_val = 1.0 / jnp.sum(jnp.exp(gl - gmax), axis=0, keepdims=True)
    g_idx = first_argmax(gl, gmax)
    lo = EXPERT_LANE0 + EXPERTS_PER_GROUP * g_idx
    el = jnp.where((row >= lo) & (row < lo + EXPERTS_PER_GROUP), lt, NEG_BIG)
    m1 = jnp.max(el, axis=0, keepdims=True)
    i1 = first_argmax(el, m1)
    el2 = jnp.where(row == i1, NEG_BIG, el)
    m2 = jnp.max(el2, axis=0, keepdims=True)
    i2 = first_argmax(el2, m2)
    r = jnp.exp(m2 - m1)
    gate1 = g_val / (1.0 + r)
    gate2 = g_val * r / (1.0 + r)

    pos = (i % tiles_per_seq) * tm + lax.broadcasted_iota(jnp.int32, (1, tm), 1)
    valid = pos < seq_len
    oh1 = jnp.where(valid & (row == i1), 1.0, 0.0)
    oh2 = jnp.where(valid & (row == i2), 1.0, 0.0)
    pre = jnp.dot(jnp.concatenate([oh1, oh2], axis=0).astype(BF16), upper_ref[...], preferred_element_type=F32)
    tot1 = jnp.sum(oh1, axis=1, keepdims=True)
    tot2 = jnp.sum(oh2, axis=1, keepdims=True)
    run = run_ref[...]
    run_t = jnp.tile(run, (1, tm // LANES))
    rank1 = jnp.sum(oh1 * (pre[:ROUTE_ROWS] + run_t), axis=0, keepdims=True)
    rank2 = jnp.sum(oh2 * (pre[ROUTE_ROWS:] + run_t + tot1), axis=0, keepdims=True)
    new_run = run + tot1 + tot2
    run_ref[...] = new_run
    cnt_ref[...] = new_run

    e1 = (i1 - EXPERT_LANE0).astype(F32)
    e2 = (i2 - EXPERT_LANE0).astype(F32)
    r8 = lax.broadcasted_iota(jnp.int32, (ROUTE_COLS, tm), 0)
    route_ref[...] = jnp.where(r8 == 0, e1, jnp.where(r8 == 1, e2, jnp.where(r8 == 2, gate1, jnp.where(
        r8 == 3, gate2, jnp.where(r8 == 4, rank1, jnp.where(r8 == 5, rank2, 0.0))))))


def _outproj_t(hp, convy, o, w_out, g2, wr_hi, wr_lo, br, upper, *, tiles_per_seq, seq_len, tm):
    n, d = hp.shape
    const = lambda i: (0, 0)
    tile = lambda i: (i, 0)
    kern = functools.partial(_outproj_t_kernel, tiles_per_seq=tiles_per_seq, seq_len=seq_len)
    return pl.pallas_call(
        kern,
        grid=(n // tm,),
        in_specs=[
            pl.BlockSpec((tm, d), tile),
            pl.BlockSpec((tm, convy.shape[1]), tile),
            pl.BlockSpec((tm, o.shape[1]), tile),
            pl.BlockSpec(w_out.shape, const),
            pl.BlockSpec((1, d), const),
            pl.BlockSpec(wr_hi.shape, const),
            pl.BlockSpec(wr_lo.shape, const),
            pl.BlockSpec((1, LANES), const),
            pl.BlockSpec(upper.shape, const),
        ],
        out_specs=[
            pl.BlockSpec((tm, d), tile),
            *[pl.BlockSpec((tm, LANES), tile)] * ROW_PLANES,
            pl.BlockSpec((ROUTE_COLS, tm), lambda i: (0, i)),
            pl.BlockSpec((ROUTE_ROWS, LANES), const),
        ],
        out_shape=[
            jax.ShapeDtypeStruct((n, d), F32),
            *[jax.ShapeDtypeStruct((n, LANES), jnp.uint32)] * ROW_PLANES,
            jax.ShapeDtypeStruct((ROUTE_COLS, n), F32),
            jax.ShapeDtypeStruct((ROUTE_ROWS, LANES), F32),
        ],
        scratch_shapes=[pltpu.VMEM((ROUTE_ROWS, LANES), F32)],
        compiler_params=_cparams(1),
        name="outproj_router",
    )(hp, convy, o, w_out, g2, wr_hi, wr_lo, br, upper)


def _slots_kernel(pstart_ref, route_ref, dest_ref, *, seq_len, p_rows):
    b = pl.program_id(0)
    route = route_ref[...]
    eid = route.astype(jnp.int32)
    start = jnp.zeros_like(eid)
    for e in range(N_EXPERTS):
        start = jnp.where(eid == e, pstart_ref[e], start)
    rank = pltpu.roll(route, ROUTE_COLS - 2 * TOP_K, axis=0).astype(jnp.int32)
    k = lax.broadcasted_iota(jnp.int32, route.shape, 0)
    pos = lax.broadcasted_iota(jnp.int32, route.shape, 1)
    n_pad = route.shape[1] - seq_len
    spare = p_rows + (b * n_pad + (pos - seq_len)) * TOP_K + k
    dest_ref[...] = jnp.where(pos < seq_len, start + rank, spare)


def _slots(pstarts, route, *, batch, lp_len, seq_len, p_rows):
    kern = functools.partial(_slots_kernel, seq_len=seq_len, p_rows=p_rows)
    grid_spec = pltpu.PrefetchScalarGridSpec(
        num_scalar_prefetch=1,
        grid=(batch,),
        in_specs=[pl.BlockSpec((ROUTE_COLS, lp_len), lambda b, ps: (0, b))],
        out_specs=pl.BlockSpec((ROUTE_COLS, lp_len), lambda b, ps: (0, b)),
    )
    return pl.pallas_call(
        kern,
        grid_spec=grid_spec,
        out_shape=jax.ShapeDtypeStruct(route.shape, jnp.int32),
        compiler_params=_cparams(1),
        name="moe_slots",
    )(pstarts, route)


def _row_copy(src_hbm, src_row, dst_ref, dst_row, sem):
    return pltpu.make_async_copy(src_hbm.at[pl.ds(src_row, 1), :], dst_ref.at[pl.ds(dst_row, 1), :], sem)


def _dispatch_kernel(dest_ref, xn_hbm, xs_hbm, xbuf, sem_in, sem, *, tiles_per_seq, lp_len, tile):
    t = pl.program_id(0)
    base = (t // tiles_per_seq) * lp_len + (t % tiles_per_seq) * tile
    tile_copy = pltpu.make_async_copy(xn_hbm.at[pl.ds(base, tile), :], xbuf, sem_in)
    tile_copy.start()
    tile_copy.wait()

    def issue(r, carry):
        for kk in range(TOP_K):
            _row_copy(xbuf, r, xs_hbm, dest_ref[0, 0, TOP_K * r + kk], sem).start()
        return carry

    lax.fori_loop(0, tile, issue, 0)

    def drain(r, carry):
        for kk in range(TOP_K):
            _row_copy(xbuf, r, xs_hbm, dest_ref[0, 0, TOP_K * r + kk], sem).wait()
        return carry

    lax.fori_loop(0, tile, drain, 0)


def _dispatch(dest_tiles, xn, *, p_rows, tiles_per_seq, lp_len, tile):
    n_tiles = dest_tiles.shape[0]
    kern = functools.partial(_dispatch_kernel, tiles_per_seq=tiles_per_seq, lp_len=lp_len, tile=tile)
    return pl.pallas_call(
        kern,
        grid=(n_tiles,),
        in_specs=[
            pl.BlockSpec((1, 1, TOP_K * tile), lambda t: (t, 0, 0), memory_space=pltpu.SMEM),
            pl.BlockSpec(memory_space=pl.ANY),
        ],
        out_specs=pl.BlockSpec(memory_space=pl.ANY),
        out_shape=jax.ShapeDtypeStruct((p_rows, xn.shape[1]), xn.dtype),
        scratch_shapes=[pltpu.VMEM((tile, xn.shape[1]), xn.dtype), pltpu.SemaphoreType.DMA(()),
                        pltpu.SemaphoreType.DMA(())],
        compiler_params=_cparams(1),
        name="moe_dispatch",
    )(dest_tiles, xn)


def _experts_kernel(be_ref, nv_ref, xs_ref, wg_ref, wu_ref, wd_ref, y_ref, wgb_ref, wub_ref, wdb_ref):
    i = pl.program_id(0)
    e = be_ref[i]
    prev = be_ref[jnp.maximum(i - 1, 0)]

    @pl.when((i == 0) | (e != prev))
    def _():
        wgb_ref[...] = wg_ref[0].astype(BF16)
        wub_ref[...] = wu_ref[0].astype(BF16)
        wdb_ref[...] = wd_ref[0].astype(BF16)

    nv = nv_ref[i]

    @pl.when(nv > 0)
    def _():
        xs = xs_ref[...]
        row = lax.broadcasted_iota(jnp.int32, xs.shape, 0)
        x = jnp.where(row < nv, xs, 0.0).astype(BF16)
        hg = jnp.dot(x, wgb_ref[...], preferred_element_type=F32)
        hu = jnp.dot(x, wub_ref[...], preferred_element_type=F32)
        hid = (hg / (1.0 + jnp.exp(-hg)) * hu).astype(BF16)
        y_ref[...] = jnp.dot(hid, wdb_ref[...], preferred_element_type=F32)

    @pl.when(nv == 0)
    def _():
        y_ref[...] = jnp.zeros_like(y_ref)


def _experts(block_e, nvalid, xs, w_gate, w_up, w_down):
    p_rows, d = xs.shape
    ff = w_gate.shape[2]
    n_blocks = p_rows // MOE_BLOCK
    grid_spec = pltpu.PrefetchScalarGridSpec(
        num_scalar_prefetch=2,
        grid=(n_blocks,),
        in_specs=[
            pl.BlockSpec((MOE_BLOCK, d), lambda i, be, nv: (i, 0)),
            pl.BlockSpec((1, d, ff), lambda i, be, nv: (be[i], 0, 0)),
            pl.BlockSpec((1, d, ff), lambda i, be, nv: (be[i], 0, 0)),
            pl.BlockSpec((1, ff, d), lambda i, be, nv: (be[i], 0, 0)),
        ],
        out_specs=pl.BlockSpec((MOE_BLOCK, d), lambda i, be, nv: (i, 0)),
        scratch_shapes=[
            pltpu.VMEM((d, ff), BF16),
            pltpu.VMEM((d, ff), BF16),
            pltpu.VMEM((ff, d), BF16),
        ],
    )
    return pl.pallas_call(
        _experts_kernel,
        grid_spec=grid_spec,
        out_shape=jax.ShapeDtypeStruct((p_rows, d), F32),
        compiler_params=_cparams(1),
        name="moe_experts",
    )(block_e, nvalid, xs, w_gate, w_up, w_down)


def _combine_kernel(dest_ref, gates_ref, h1_hbm, y_hbm, out_ref, hbuf, ya, yb, sem_h, sem_a, sem_b,
                    *, lp_len, tile):
    b = pl.program_id(0)
    i = pl.program_id(1)
    start = b * lp_len + N_META + i * tile
    h_copy = pltpu.make_async_copy(h1_hbm.at[pl.ds(start, tile), :], hbuf, sem_h)
    h_copy.start()

    def issue(r, carry):
        _row_copy(y_hbm, dest_ref[0, 0, TOP_K * r], ya, r, sem_a).start()
        _row_copy(y_hbm, dest_ref[0, 0, TOP_K * r + 1], yb, r, sem_b).start()
        return carry

    lax.fori_loop(0, tile, issue, 0)

    def drain(r, carry):
        _row_copy(y_hbm, dest_ref[0, 0, TOP_K * r], ya, r, sem_a).wait()
        _row_copy(y_hbm, dest_ref[0, 0, TOP_K * r + 1], yb, r, sem_b).wait()
        return carry

    lax.fori_loop(0, tile, drain, 0)
    h_copy.wait()
    g = gates_ref[0]
    out_ref[0] = hbuf[...] + g[:, 0:1] * ya[...] + g[:, 1:2] * yb[...]


def _combine(dest_tiles, gates, h1, y, *, batch, seq, lp_len, tile):
    d = h1.shape[1]
    nt = seq // tile
    kern = functools.partial(_combine_kernel, lp_len=lp_len, tile=tile)
    return pl.pallas_call(
        kern,
        grid=(batch, nt),
        in_specs=[
            pl.BlockSpec((1, 1, TOP_K * tile), lambda b, i: (b * nt + i, 0, 0), memory_space=pltpu.SMEM),
            pl.BlockSpec((1, tile, TOP_K), lambda b, i: (b, i, 0)),
            pl.BlockSpec(memory_space=pl.ANY),
            pl.BlockSpec(memory_space=pl.ANY),
        ],
        out_specs=pl.BlockSpec((1, tile, d), lambda b, i: (b, i, 0)),
        out_shape=jax.ShapeDtypeStruct((batch, seq, d), F32),
        scratch_shapes=[
            pltpu.VMEM((tile, d), F32),
            pltpu.VMEM((tile, d), F32),
            pltpu.VMEM((tile, d), F32),
            pltpu.SemaphoreType.DMA(()),
            pltpu.SemaphoreType.DMA(()),
            pltpu.SemaphoreType.DMA(()),
        ],
        compiler_params=_cparams(2),
        name="moe_combine",
    )(dest_tiles, gates, h1, y)


def _sc_workers():
    info = plsc.get_sparse_core_info()
    return info.num_cores, info.num_cores * info.num_subcores


def _sc_scatter_rows(planes, idx_a, idx_b, out_rows):
    n_win = idx_a.shape[0]
    n_cores, n_workers = _sc_workers()
    trips = -(-n_win // n_workers)
    mesh = plsc.VectorSubcoreMesh(core_axis_name="c", subcore_axis_name="s")

    def body(*refs):
        xs = refs[0:ROW_PLANES]
        ia_hbm, ib_hbm = refs[ROW_PLANES:ROW_PLANES + 2]
        outs = refs[ROW_PLANES + 2:2 * ROW_PLANES + 2]
        ia_v, ib_v, buf, sem = refs[2 * ROW_PLANES + 2:]
        wid = lax.axis_index("s") * n_cores + lax.axis_index("c")

        def step(t, carry):
            g = wid + t * n_workers

            @pl.when(g < n_win)
            def _():
                pltpu.sync_copy(ia_hbm.at[g], ia_v)
                pltpu.sync_copy(ib_hbm.at[g], ib_v)
                row0 = pl.multiple_of(g * SC_WINDOW, SC_WINDOW)
                loads = [pltpu.async_copy(xs[c].at[pl.ds(row0, SC_WINDOW)], buf.at[c], sem)
                         for c in range(ROW_PLANES)]
                for cp in loads:
                    cp.wait()
                stores = [pltpu.async_copy(buf.at[c], outs[c].at[iv], sem)
                          for c in range(ROW_PLANES) for iv in (ia_v, ib_v)]
                for cp in stores:
                    cp.wait()

            return carry

        lax.fori_loop(0, trips, step, 0)

    kern = pl.kernel(
        body,
        out_type=[jax.ShapeDtypeStruct((out_rows, LANES), jnp.uint32)] * ROW_PLANES,
        mesh=mesh,
        scratch_types=[
            pltpu.VMEM((SC_WINDOW,), jnp.int32),
            pltpu.VMEM((SC_WINDOW,), jnp.int32),
            pltpu.VMEM((ROW_PLANES, SC_WINDOW, LANES), jnp.uint32),
            pltpu.SemaphoreType.DMA,
        ],
        name="moe_dispatch_sc",
    )
    return kern(*planes, idx_a, idx_b)


def _sc_gather_rows(planes, idx_a, idx_b):
    n_win = idx_a.shape[0]
    n_cores, n_workers = _sc_workers()
    trips = -(-n_win // n_workers)
    mesh = plsc.VectorSubcoreMesh(core_axis_name="c", subcore_axis_name="s")

    def body(*refs):
        ys = refs[0:ROW_PLANES]
        ia_hbm, ib_hbm = refs[ROW_PLANES:ROW_PLANES + 2]
        outs_a = refs[ROW_PLANES + 2:2 * ROW_PLANES + 2]
        outs_b = refs[2 * ROW_PLANES + 2:3 * ROW_PLANES + 2]
        iv, buf, sem = refs[3 * ROW_PLANES + 2:]
        wid = lax.axis_index("s") * n_cores + lax.axis_index("c")

        def step(t, carry):
            g = wid + t * n_workers

            @pl.when(g < n_win)
            def _():
                row0 = pl.multiple_of(g * SC_WINDOW, SC_WINDOW)
                for i_hbm, outs in ((ia_hbm, outs_a), (ib_hbm, outs_b)):
                    pltpu.sync_copy(i_hbm.at[g], iv)
                    loads = [pltpu.async_copy(ys[c].at[iv], buf.at[c], sem) for c in range(ROW_PLANES)]
                    for cp in loads:
                        cp.wait()
                    stores = [pltpu.async_copy(buf.at[c], outs[c].at[pl.ds(row0, SC_WINDOW)], sem)
                              for c in range(ROW_PLANES)]
                    for cp in stores:
                        cp.wait()

            return carry

        lax.fori_loop(0, trips, step, 0)

    n_rows = n_win * SC_WINDOW
    kern = pl.kernel(
        body,
        out_type=[jax.ShapeDtypeStruct((n_rows, LANES), jnp.uint32)] * (2 * ROW_PLANES),
        mesh=mesh,
        scratch_types=[
            pltpu.VMEM((SC_WINDOW,), jnp.int32),
            pltpu.VMEM((ROW_PLANES, SC_WINDOW, LANES), jnp.uint32),
            pltpu.SemaphoreType.DMA,
        ],
        name="moe_gather_sc",
    )
    res = kern(*planes, idx_a, idx_b)
    return res[:ROW_PLANES], res[ROW_PLANES:]


def _experts_kernel(be_ref, nv_ref, first_ref, slot_ref, nxt_ref, x0_ref, x1_ref, x2_ref, x3_ref,
                    wg_hbm, wu_hbm, wd_hbm, y0_ref, y1_ref, y2_ref, y3_ref,
                    wgf_ref, wuf_ref, wdf_ref, wgb_ref, wub_ref, wdb_ref, sem):
    i = pl.program_id(0)
    e = be_ref[i]
    slot = slot_ref[i]
    y_refs = (y0_ref, y1_ref, y2_ref, y3_ref)

    def weight_copies(expert, s):
        return [pltpu.make_async_copy(hbm.at[expert], stage.at[s], sem.at[s, j])
                for j, (hbm, stage) in enumerate(((wg_hbm, wgf_ref), (wu_hbm, wuf_ref), (wd_hbm, wdf_ref)))]

    @pl.when(i == 0)
    def _():
        for cp in weight_copies(e, slot):
            cp.start()

    @pl.when(first_ref[i] == 1)
    def _():
        for cp in weight_copies(e, slot):
            cp.wait()
        nxt = nxt_ref[i]

        @pl.when(nxt >= 0)
        def _():
            for cp in weight_copies(nxt, 1 - slot):
                cp.start()

        wgb_ref[...] = wgf_ref[slot].astype(BF16)
        wub_ref[...] = wuf_ref[slot].astype(BF16)
        wdb_ref[...] = wdf_ref[slot].astype(BF16)

    nv = nv_ref[i]
    half = MOE_BLOCK // 2

    def mlp(rows):
        xs = _unpack_rows([r[rows, :] for r in (x0_ref, x1_ref, x2_ref, x3_ref)])
        row = lax.broadcasted_iota(jnp.int32, xs.shape, 0)
        x = jnp.where(row < nv, xs, jnp.zeros_like(xs))
        hg = jnp.dot(x, wgb_ref[...], preferred_element_type=F32)
        hu = jnp.dot(x, wub_ref[...], preferred_element_type=F32)
        hid = (hg / (1.0 + jnp.exp(-hg)) * hu).astype(BF16)
        yw = _pack_rows(jnp.dot(hid, wdb_ref[...], preferred_element_type=F32))
        for c, ref in enumerate(y_refs):
            ref[rows, :] = yw[:, c * LANES:(c + 1) * LANES]

    def clear(rows):
        for ref in y_refs:
            ref[rows, :] = jnp.zeros((rows.size, LANES), ref.dtype)

    @pl.when(nv > half)
    def _():
        mlp(pl.ds(0, MOE_BLOCK))

    @pl.when((nv > 0) & (nv <= half))
    def _():
        mlp(pl.ds(0, half))
        clear(pl.ds(half, half))

    @pl.when(nv == 0)
    def _():
        clear(pl.ds(0, MOE_BLOCK))


def _experts(block_e, nvalid, xs_planes, w_gate, w_up, w_down):
    n_blocks = block_e.shape[0]
    _, d, ff = w_gate.shape
    first = jnp.concatenate([jnp.ones((1,), jnp.int32), (block_e[1:] != block_e[:-1]).astype(jnp.int32)])
    slot = (jnp.cumsum(first) - 1) % 2
    later = jnp.where(block_e[None, :] > block_e[:, None], block_e[None, :], N_EXPERTS)
    nxt = jnp.min(later, axis=1)
    nxt = jnp.where(nxt == N_EXPERTS, -1, nxt).astype(jnp.int32)
    blk = lambda i, *_: (i, 0)
    grid_spec = pltpu.PrefetchScalarGridSpec(
        num_scalar_prefetch=5,
        grid=(n_blocks,),
        in_specs=[
            *[pl.BlockSpec((MOE_BLOCK, LANES), blk)] * ROW_PLANES,
            pl.BlockSpec(memory_space=pl.ANY),
            pl.BlockSpec(memory_space=pl.ANY),
            pl.BlockSpec(memory_space=pl.ANY),
        ],
        out_specs=[pl.BlockSpec((MOE_BLOCK, LANES), blk)] * ROW_PLANES,
        scratch_shapes=[
            pltpu.VMEM((2, d, ff), F32),
            pltpu.VMEM((2, d, ff), F32),
            pltpu.VMEM((2, ff, d), F32),
            pltpu.VMEM((d, ff), BF16),
            pltpu.VMEM((d, ff), BF16),
            pltpu.VMEM((ff, d), BF16),
            pltpu.SemaphoreType.DMA((2, 3)),
        ],
    )
    return pl.pallas_call(
        _experts_kernel,
        grid_spec=grid_spec,
        out_shape=[jax.ShapeDtypeStruct((n_blocks * MOE_BLOCK, LANES), jnp.uint32)] * ROW_PLANES,
        compiler_params=_cparams(1),
        name="moe_experts",
    )(block_e, nvalid, first, slot.astype(jnp.int32), nxt, *xs_planes, w_gate, w_up, w_down)


def _combine_kernel(gates_ref, h1_hbm, *refs, lp_len, tile):
    a_refs = refs[0:ROW_PLANES]
    b_refs = refs[ROW_PLANES:2 * ROW_PLANES]
    out_ref, hbuf, sem_h = refs[2 * ROW_PLANES:]
    nt = pl.num_programs(1)
    step = pl.program_id(0) * nt + pl.program_id(1)
    last = pl.num_programs(0) * nt - 1

    def h_copy(s, slot):
        start = (s // nt) * lp_len + N_META + (s % nt) * tile
        return pltpu.make_async_copy(h1_hbm.at[pl.ds(start, tile), :], hbuf.at[slot], sem_h.at[slot])

    slot = step % 2

    @pl.when(step == 0)
    def _():
        h_copy(step, slot).start()

    @pl.when(step < last)
    def _():
        h_copy(step + 1, 1 - slot).start()

    ya = _unpack_rows([r[...] for r in a_refs]).astype(F32)
    yb = _unpack_rows([r[...] for r in b_refs]).astype(F32)
    g = jnp.concatenate([gates_ref[...]] * (LANES // ROUTE_COLS), axis=0).T
    moe = g[:, 0:1] * ya + g[:, 1:2] * yb
    h_copy(step, slot).wait()
    out_ref[0] = hbuf[slot] + moe


def _combine(gates, h1, a_planes, b_planes, *, batch, seq, lp_len, tile):
    d = h1.shape[1]
    nt = seq // tile
    kern = functools.partial(_combine_kernel, lp_len=lp_len, tile=tile)
    rows = lambda b, i: (b * nt + i, 0)
    return pl.pallas_call(
        kern,
        grid=(batch, nt),
        in_specs=[
            pl.BlockSpec((ROUTE_COLS, tile), lambda b, i: (0, b * nt + i)),
            pl.BlockSpec(memory_space=pl.ANY),
            *[pl.BlockSpec((tile, LANES), rows)] * (2 * ROW_PLANES),
        ],
        out_specs=pl.BlockSpec((1, tile, d), lambda b, i: (b, i, 0)),
        out_shape=jax.ShapeDtypeStruct((batch, seq, d), F32),
        scratch_shapes=[pltpu.VMEM((2, tile, d), F32), pltpu.SemaphoreType.DMA((2,))],
        compiler_params=_cparams(2),
        name="moe_combine",
    )(gates, h1, *a_planes, *b_planes)


def _rope_tables(length, lp_len):
    half = ROPE_DIM // 2
    pos = jnp.arange(length, dtype=F32)
    inv_freq = ROPE_THETA ** (-jnp.arange(0, ROPE_DIM, 2, dtype=F32) / ROPE_DIM)
    ang = pos[:, None] * inv_freq[None, :]
    cos = jnp.cos(ang)
    sin = jnp.sin(ang)
    ones = jnp.ones((length, HEAD_DIM - ROPE_DIM), F32)
    zeros_h = jnp.zeros((length, half), F32)
    zeros_r = jnp.zeros((length, HEAD_DIM - ROPE_DIM), F32)
    c = jnp.concatenate([cos, cos, ones], axis=1)
    s1 = jnp.concatenate([zeros_h, sin, zeros_r], axis=1)
    s2 = jnp.concatenate([-sin, zeros_h, zeros_r], axis=1)
    pad = ((0, lp_len - length), (0, 0))
    rep = LANES // HEAD_DIM
    return tuple(jnp.pad(jnp.tile(t, (1, rep)), pad) for t in (c, s1, s2))


def _layer(hp, l, batch, length, lp_len, tm, norm1_g, w_in, conv_w, q_norm_g, k_norm_g, lambda_q1, lambda_k1,
           lambda_q2, lambda_k2, subln_g, w_out, norm2_g, w_router_group, b_router_group, w_router_expert,
           b_router_expert, w_gate, w_up, w_down, rope, last):
    n, d = hp.shape
    tiles_per_seq = lp_len // tm
    cw = conv_w.shape[2]
    qw = N_HEADS * 2 * HEAD_DIM
    lam_init = 0.8 - 0.6 * math.exp(-0.3 * l)

    reps = qw // HEAD_DIM
    gq = jnp.tile(q_norm_g[l] * (HEAD_DIM ** -0.5 * LOG2E), reps)[None, :]
    gk = jnp.tile(k_norm_g[l], reps)[None, :]
    seg = jnp.arange(qw) // HEAD_DIM
    bd = (seg[:, None] == seg[None, :]).astype(BF16)
    convy, q, k, v = _inproj(hp, norm1_g[l][None, :], w_in[l].astype(BF16), conv_w[l], gq, gk, bd, *rope,
                             tiles_per_seq=tiles_per_seq, tm=tm)

    lamp = jnp.stack([lambda_q1[l], lambda_k1[l], lambda_q2[l], lambda_k2[l]]).astype(F32)
    o = _attention(q, k, v, lamp, subln_g[l][None, :], batch=batch, lp_len=lp_len, tq=tm, lam_init=lam_init)

    lane_pad = LANES - N_GROUPS - N_EXPERTS
    wr = jnp.pad(jnp.concatenate([w_router_group[l], w_router_expert[l]], axis=1), ((0, 0), (0, lane_pad)))
    wr_hi = wr.astype(BF16)
    wr_lo = (wr - wr_hi.astype(F32)).astype(BF16)
    br = jnp.pad(jnp.concatenate([b_router_group[l], b_router_expert[l]]), (0, lane_pad))[None, :]
    ridx = jnp.arange(tm)
    upper = (ridx[:, None] < ridx[None, :]).astype(BF16)
    h1, *rest = _outproj_t(hp, convy, o, w_out[l].astype(BF16), norm2_g[l][None, :], wr_hi, wr_lo, br,
                           upper, tiles_per_seq=tiles_per_seq, seq_len=length, tm=tm)
    x_planes = rest[:ROW_PLANES]
    route, cnt = rest[ROW_PLANES:]

    counts = cnt[EXPERT_LANE0:EXPERT_LANE0 + N_EXPERTS, 0].astype(jnp.int32)
    a = batch * length * TOP_K
    n_blocks = -(-a // MOE_BLOCK) + N_EXPERTS
    p_rows = n_blocks * MOE_BLOCK
    padded = (counts + MOE_BLOCK - 1) // MOE_BLOCK * MOE_BLOCK
    pends = jnp.cumsum(padded)
    pstarts = pends - padded

    def lookup(table, idx):
        return sum(jnp.where(idx == e, table[e], 0) for e in range(N_EXPERTS))

    n_pad = lp_len - length
    spare_rows = -(-(batch * n_pad * TOP_K) // MOE_BLOCK) * MOE_BLOCK
    dest = _slots(pstarts.astype(jnp.int32), route, batch=batch, lp_len=lp_len, seq_len=length, p_rows=p_rows)
    dest = dest.reshape(ROUTE_COLS, batch, lp_len)
    gates = route.reshape(ROUTE_COLS, batch, lp_len)[TOP_K:2 * TOP_K]
    blk0 = jnp.arange(n_blocks, dtype=jnp.int32) * MOE_BLOCK
    block_e = jnp.minimum(jnp.sum((pends[None, :] <= blk0[:, None]).astype(jnp.int32), axis=1), N_EXPERTS - 1)
    nvalid = jnp.clip(lookup(counts, block_e) - (blk0 - lookup(pstarts, block_e)), 0, MOE_BLOCK)

    assert (batch * lp_len) % SC_WINDOW == 0
    xs_planes = _sc_scatter_rows(x_planes, dest[0].reshape(-1, SC_WINDOW), dest[1].reshape(-1, SC_WINDOW),
                                 p_rows + spare_rows)
    y_planes = _experts(block_e, nvalid, xs_planes, w_gate[l], w_up[l], w_down[l])

    if not last:
        raise NotImplementedError("only the final layer's combine (which drops the meta tokens) is implemented")
    seq = length - N_META
    assert (batch * seq) % SC_WINDOW == 0
    dest_x = dest[0:TOP_K, :, N_META:length]
    a_planes, b_planes = _sc_gather_rows(y_planes, dest_x[0].reshape(-1, SC_WINDOW),
                                         dest_x[1].reshape(-1, SC_WINDOW))
    ctile = _largest_tile(seq, 512, LANES)
    gates_x = jnp.pad(gates[:, :, N_META:length].reshape(TOP_K, batch * seq), ((0, ROUTE_COLS - TOP_K), (0, 0)))
    return _combine(gates_x, h1, a_planes, b_planes, batch=batch, seq=seq, lp_len=lp_len, tile=ctile)


def kernel(x, meta_tokens, norm1_g, w_in, conv_w, q_norm_g, k_norm_g, lambda_q1, lambda_k1, lambda_q2, lambda_k2,
           subln_g, w_out, norm2_g, w_router_group, b_router_group, w_router_expert, b_router_expert, w_gate,
           w_up, w_down):
    b, s, d = x.shape
    depth = w_in.shape[0]
    assert depth == 1, "a single layer is supported"
    length = s + N_META
    tm = TOKEN_TILE if length >= TOKEN_TILE else LANES
    lp_len = -(-length // tm) * tm
    hp = jnp.concatenate([jnp.broadcast_to(meta_tokens[None].astype(x.dtype), (b, N_META, d)), x,
                          jnp.zeros((b, lp_len - length, d), x.dtype)], axis=1).reshape(b * lp_len, d)
    rope = _rope_tables(length, lp_len)
    return _layer(hp, 0, b, length, lp_len, tm, norm1_g, w_in, conv_w, q_norm_g, k_norm_g, lambda_q1, lambda_k1,
                  lambda_q2, lambda_k2, subln_g, w_out, norm2_g, w_router_group, b_router_group,
                  w_router_expert, b_router_expert, w_gate, w_up, w_down, rope, last=True)
```

```python
import functools
import math

import jax
import jax.numpy as jnp
from jax import lax
from jax.experimental import pallas as pl
from jax.experimental.pallas import tpu as pltpu
from jax.experimental.pallas import tpu_sc as plsc

F32 = jnp.float32
BF16 = jnp.bfloat16

N_META = 16
CONV_K = 3
N_HEADS = 4
HEAD_DIM = 64
V_DIM = 2 * HEAD_DIM
ROPE_DIM = HEAD_DIM // 4
ROPE_THETA = 500000.0
N_GROUPS = 4
EXPERTS_PER_GROUP = 8
N_EXPERTS = N_GROUPS * EXPERTS_PER_GROUP
TOP_K = 2
EPS = 1e-6
LOG2E = 1.4426950408889634

LANES = 128
TOKEN_TILE = 640
INPROJ_CHAINS = 2
ATTN_HEADS_PER_STEP = 4
ATTN_WIDE_CHUNKS = 2
MOE_BLOCK = 512
ROUTE_COLS = 8
ROUTE_ROWS = 64
ROW_PLANES = 4
SC_WINDOW = 128
EXPERT_LANE0 = N_GROUPS
NEG_BIG = -1e30
VMEM_LIMIT = 56 * 1024 * 1024


def _largest_tile(n, cap, mult):
    for t in range(min(cap, n), 0, -1):
        if n % t == 0 and t % mult == 0:
            return t
    raise ValueError(f"no tile for {n}")


def _cparams(n_axes, flags=None):
    return pltpu.CompilerParams(dimension_semantics=("arbitrary",) * n_axes,
                                vmem_limit_bytes=VMEM_LIMIT, flags=flags)


def _inproj_kernel(x_hbm, meta_hbm, g1_ref, win_ref, convw_ref, gq_ref, gk_ref, bd_ref, rc_ref, rs1_ref, rs2_ref,
                   hp_ref, convy_ref, q_ref, k_ref, v_ref, carry_ref, xbuf, sem, *, tiles_per_seq, seq, cw, qw):
    i = pl.program_id(0)
    tm = hp_ref.shape[0]
    sub = tm // INPROJ_CHAINS
    q0 = 3 * cw
    w = convw_ref[...]
    last_rows = seq + N_META - (tiles_per_seq - 1) * tm

    def fetch(step, slot, start):
        b = step // tiles_per_seq
        t = step % tiles_per_seq

        def go(src, dst):
            cp = pltpu.make_async_copy(src, dst, sem.at[slot])
            if start:
                cp.start()
            else:
                cp.wait()

        @pl.when(t == 0)
        def _():
            go(meta_hbm, xbuf.at[slot, pl.ds(0, N_META)])
            go(x_hbm.at[pl.ds(b * seq, tm - N_META)], xbuf.at[slot, pl.ds(N_META, tm - N_META)])

        @pl.when((t > 0) & (t < tiles_per_seq - 1))
        def _():
            go(x_hbm.at[pl.ds(b * seq + t * tm - N_META, tm)], xbuf.at[slot])

        @pl.when(t == tiles_per_seq - 1)
        def _():
            go(x_hbm.at[pl.ds(b * seq + t * tm - N_META, last_rows)], xbuf.at[slot, pl.ds(0, last_rows)])

    slot = i % 2

    @pl.when(i == 0)
    def _():
        fetch(i, slot, True)

    @pl.when(i + 1 < pl.num_programs(0))
    def _():
        fetch(i + 1, 1 - slot, True)

    fetch(i, slot, False)

    @pl.when(i % tiles_per_seq == tiles_per_seq - 1)
    def _():
        xbuf[slot, pl.ds(last_rows, tm - last_rows), :] = jnp.zeros((tm - last_rows, xbuf.shape[2]), xbuf.dtype)

    hp_ref[...] = xbuf[slot]

    @pl.when(i % tiles_per_seq == 0)
    def _():
        carry_ref[...] = jnp.zeros_like(carry_ref)

    prev = carry_ref[...]

    for chain in range(INPROJ_CHAINS):
        rows = pl.ds(chain * sub, sub)
        x = xbuf[slot, rows, :]
        ms = jnp.mean(x * x, axis=-1, keepdims=True)
        xn = (x * lax.rsqrt(ms + EPS) * g1_ref[...]).astype(BF16)

        def proj(lo, hi):
            return jnp.dot(xn, win_ref[:, lo:hi], preferred_element_type=F32)

        u_conv = proj(0, q0)
        u_q = proj(q0, q0 + qw)

        z = u_conv[:, cw:2 * cw] * u_conv[:, 2 * cw:3 * cw]
        p1 = prev[7:8]
        p2 = prev[6:7]
        row = lax.broadcasted_iota(jnp.int32, z.shape, 0)
        z1 = jnp.where(row == 0, p1, pltpu.roll(z, 1, axis=0))
        z2 = jnp.where(row == 0, p2, jnp.where(row == 1, p1, pltpu.roll(z, 2, axis=0)))
        prev = z[sub - 8:sub]
        conv = w[0:1] * z2 + w[1:2] * z1 + w[2:3] * z
        convy_ref[rows, :] = (u_conv[:, 0:cw] * conv).astype(BF16)

        rc = rc_ref[rows, :]
        rs1 = rs1_ref[rows, :]
        rs2 = rs2_ref[rows, :]

        def norm_rope(t, g_ref):
            ss = jnp.dot((t * t).astype(BF16), bd_ref[...], preferred_element_type=F32)
            tn = t * lax.rsqrt(ss * (1.0 / HEAD_DIM) + EPS) * g_ref[...]
            outs = []
            for c in range(qw // LANES):
                ch = tn[:, c * LANES:(c + 1) * LANES]
                outs.append(ch * rc + pltpu.roll(ch, ROPE_DIM // 2, axis=1) * rs1
                            + pltpu.roll(ch, LANES - ROPE_DIM // 2, axis=1) * rs2)
            return jnp.concatenate(outs, axis=1).astype(BF16)

        u_k = proj(q0 + qw, q0 + 2 * qw)
        q_ref[rows, :] = norm_rope(u_q, gq_ref)
        u_v = proj(q0 + 2 * qw, win_ref.shape[1])
        k_ref[rows, :] = norm_rope(u_k, gk_ref)
        v_ref[rows, :] = u_v.astype(BF16)

    carry_ref[...] = prev


def _inproj(x, meta, g1, w_in, conv_w, gq, gk, bd, rc, rs1, rs2, *, tiles_per_seq, tm):
    batch, seq, d = x.shape
    n = batch * tiles_per_seq * tm
    cw = conv_w.shape[1]
    qw = gq.shape[1]
    aw = w_in.shape[1] - 3 * cw - 2 * qw
    const = lambda i: (0, 0)
    tile = lambda i: (i, 0)
    pos = lambda i: (i % tiles_per_seq, 0)
    kern = functools.partial(_inproj_kernel, tiles_per_seq=tiles_per_seq, seq=seq, cw=cw, qw=qw)
    return pl.pallas_call(
        kern,
        grid=(n // tm,),
        in_specs=[
            pl.BlockSpec(memory_space=pl.ANY),
            pl.BlockSpec(memory_space=pl.ANY),
            pl.BlockSpec((1, d), const),
            pl.BlockSpec(w_in.shape, const),
            pl.BlockSpec(conv_w.shape, const),
            pl.BlockSpec((1, qw), const),
            pl.BlockSpec((1, qw), const),
            pl.BlockSpec(bd.shape, const),
            pl.BlockSpec((tm, LANES), pos),
            pl.BlockSpec((tm, LANES), pos),
            pl.BlockSpec((tm, LANES), pos),
        ],
        out_specs=[
            pl.BlockSpec((tm, d), tile),
            pl.BlockSpec((tm, cw), tile),
            pl.BlockSpec((tm, qw), tile),
            pl.BlockSpec((tm, qw), tile),
            pl.BlockSpec((tm, aw), tile),
        ],
        out_shape=[
            jax.ShapeDtypeStruct((n, d), x.dtype),
            jax.ShapeDtypeStruct((n, cw), BF16),
            jax.ShapeDtypeStruct((n, qw), BF16),
            jax.ShapeDtypeStruct((n, qw), BF16),
            jax.ShapeDtypeStruct((n, aw), BF16),
        ],
        scratch_shapes=[pltpu.VMEM((8, cw), F32), pltpu.VMEM((2, tm, d), x.dtype),
                        pltpu.SemaphoreType.DMA((2,))],
        compiler_params=_cparams(1),
        name="inproj",
    )(x.reshape(batch * seq, d), meta, g1, w_in, conv_w, gq, gk, bd, rc, rs1, rs2)


def _attn_kernel(q_ref, k_ref, v_ref, lamp_ref, sg_ref, o_ref, qs_ref, m_ref, l_ref, acc_ref, *, lam_init):
    qi = pl.program_id(2)
    tq = q_ref.shape[0]
    n_heads = q_ref.shape[1] // LANES
    n_chains = 2 * n_heads
    lane = lax.broadcasted_iota(jnp.int32, (tq, LANES), 1)
    for h in range(n_heads):
        q = q_ref[:, h * LANES:(h + 1) * LANES]
        zero = jnp.zeros_like(q)
        qs_ref[pl.ds(2 * h * tq, tq), :] = jnp.where(lane < HEAD_DIM, q, zero)
        qs_ref[pl.ds((2 * h + 1) * tq, tq), :] = jnp.where(lane >= HEAD_DIM, q, zero)
    m_ref[...] = jnp.full_like(m_ref, NEG_BIG)
    l_ref[...] = jnp.zeros_like(l_ref)
    acc_ref[...] = jnp.zeros_like(acc_ref)

    def scores(off, width, which):
        h = which // 2
        kc = k_ref[pl.ds(off, width), h * LANES:(h + 1) * LANES]
        return lax.dot_general(qs_ref[pl.ds(which * tq, tq), :], kc, (((1,), (1,)), ((), ())),
                               preferred_element_type=F32)

    def update(off, width, which, s, masked):
        h = which // 2
        vc = jnp.concatenate([v_ref[pl.ds(off, width), h * LANES:(h + 1) * LANES],
                              jnp.ones((width, LANES), BF16)], axis=1)
        rows = pl.ds(which * tq, tq)
        if masked:
            r = lax.broadcasted_iota(jnp.int32, s.shape, 0)
            c = lax.broadcasted_iota(jnp.int32, s.shape, 1)
            s = jnp.where(c <= r, s, NEG_BIG)
        m_prev = m_ref[rows, :]
        m_new = jnp.maximum(m_prev, jnp.max(s, axis=-1, keepdims=True))
        alpha = jnp.exp2(m_prev - m_new)
        p = jnp.exp2((s - jnp.tile(m_new, (1, width // LANES))).astype(BF16))
        pv = jnp.dot(p, vc, preferred_element_type=F32)
        l_ref[rows, :] = alpha * l_ref[rows, :] + pv[:, LANES:]
        acc_ref[rows, :] = alpha * acc_ref[rows, :] + pv[:, :LANES]
        m_ref[rows, :] = m_new

    def chunk(off, width, masked):
        s_next = scores(off, width, 0)
        for c in range(n_chains):
            s = s_next
            if c + 1 < n_chains:
                s_next = scores(off, width, c + 1)
            update(off, width, c, s, masked)

    wide = ATTN_WIDE_CHUNKS * tq

    def body(j, carry):
        chunk(pl.multiple_of(j * wide, wide), wide, False)
        return carry

    n_wide = qi // ATTN_WIDE_CHUNKS
    lax.fori_loop(0, n_wide, body, 0)
    for extra in range(ATTN_WIDE_CHUNKS - 1):
        @pl.when(n_wide * ATTN_WIDE_CHUNKS + extra < qi)
        def _():
            chunk(pl.multiple_of((n_wide * ATTN_WIDE_CHUNKS + extra) * tq, tq), tq, False)

    chunk(pl.multiple_of(qi * tq, tq), tq, True)

    lp = lamp_ref[...]
    lam = (jnp.exp(jnp.sum(lp[0:1] * lp[1:2], axis=-1, keepdims=True))
           - jnp.exp(jnp.sum(lp[2:3] * lp[3:4], axis=-1, keepdims=True)) + lam_init)
    for h in range(n_heads):
        rows = pl.ds(2 * h * tq, 2 * tq)
        o_all = acc_ref[rows, :] / l_ref[rows, :]
        o = o_all[0:tq] - lam * o_all[tq:2 * tq]
        ms = jnp.mean(o * o, axis=-1, keepdims=True)
        o_ref[:, h * LANES:(h + 1) * LANES] = (o * lax.rsqrt(ms + EPS) * sg_ref[...]
                                               * (1.0 - lam_init)).astype(BF16)


def _attention(q, k, v, lamp, sg, *, batch, lp_len, tq, lam_init):
    n, qw = q.shape
    nq = lp_len // tq
    hw = ATTN_HEADS_PER_STEP * LANES
    chains = 2 * ATTN_HEADS_PER_STEP
    kern = functools.partial(_attn_kernel, lam_init=lam_init)
    return pl.pallas_call(
        kern,
        grid=(batch, qw // hw, nq),
        in_specs=[
            pl.BlockSpec((tq, hw), lambda b, h, i: (b * nq + i, h)),
            pl.BlockSpec((lp_len, hw), lambda b, h, i: (b, h)),
            pl.BlockSpec((lp_len, hw), lambda b, h, i: (b, h)),
            pl.BlockSpec(lamp.shape, lambda b, h, i: (0, 0)),
            pl.BlockSpec(sg.shape, lambda b, h, i: (0, 0)),
        ],
        out_specs=pl.BlockSpec((tq, hw), lambda b, h, i: (b * nq + i, h)),
        out_shape=jax.ShapeDtypeStruct((n, v.shape[1]), BF16),
        scratch_shapes=[
            pltpu.VMEM((chains * tq, LANES), BF16),
            pltpu.VMEM((chains * tq, LANES), F32),
            pltpu.VMEM((chains * tq, LANES), F32),
            pltpu.VMEM((chains * tq, LANES), F32),
        ],
        compiler_params=_cparams(3),
        name="diffattn",
    )(q, k, v, lamp, sg)


def _pack_rows(x):
    w = x.shape[1] // 2
    lo = lax.bitcast_convert_type(x[:, :w].astype(BF16).astype(F32), jnp.uint32)
    hi = lax.bitcast_convert_type(x[:, w:].astype(BF16).astype(F32), jnp.uint32)
    return lax.shift_right_logical(lo, jnp.uint32(16)) | (hi & jnp.uint32(0xFFFF0000))


def _unpack_rows(planes):
    w = jnp.concatenate(planes, axis=1)
    lo = lax.bitcast_convert_type(lax.shift_left(w, jnp.uint32(16)), F32)
    hi = lax.bitcast_convert_type(w & jnp.uint32(0xFFFF0000), F32)
    return jnp.concatenate([lo, hi], axis=1).astype(BF16)


def _outproj_kernel(hp_ref, cy_ref, o_ref, wout_ref, g2_ref, wrh_ref, wrl_ref, br_ref, tri_ref,
                    h1_ref, xp0_ref, xp1_ref, xp2_ref, xp3_ref, route_ref, cnt_ref, run_ref,
                    *, tiles_per_seq, seq_len):
    i = pl.program_id(0)
    tm = hp_ref.shape[0]
    sub = tri_ref.shape[0]
    wr_both = jnp.concatenate([wrh_ref[...], wrl_ref[...]], axis=1)

    @pl.when(i == 0)
    def _():
        run_ref[...] = jnp.zeros_like(run_ref)

    def project(rows):
        mix = jnp.concatenate([cy_ref[rows, :], o_ref[rows, :]], axis=1)
        h1 = hp_ref[rows, :] + jnp.dot(mix, wout_ref[...], preferred_element_type=F32)
        h1_ref[rows, :] = h1
        ms = jnp.mean(h1 * h1, axis=-1, keepdims=True)
        xn = h1 * lax.rsqrt(ms + EPS) * g2_ref[...]
        xw = _pack_rows(xn)
        for c, ref in enumerate((xp0_ref, xp1_ref, xp2_ref, xp3_ref)):
            ref[rows, :] = xw[:, c * LANES:(c + 1) * LANES]
        x_hi = xn.astype(BF16)
        x_lo = (xn - x_hi.astype(F32)).astype(BF16)
        hi_both = jnp.dot(x_hi, wr_both, preferred_element_type=F32)
        return (hi_both[:, :LANES] + hi_both[:, LANES:]
                + jnp.dot(x_lo, wrh_ref[...], preferred_element_type=F32) + br_ref[...])

    chains = [pl.ds(c * sub, sub) for c in range(tm // sub)]
    all_logits = [project(rows) for rows in chains]
    run = run_ref[0:1, :]
    for c, (rows, logits) in enumerate(zip(chains, all_logits)):
        run = _route_rows(i, c, rows, logits, run, tri_ref, route_ref, tiles_per_seq=tiles_per_seq,
                          seq_len=seq_len, tm=tm)
    run_ref[...] = jnp.broadcast_to(run, run_ref.shape)
    cnt_ref[...] = jnp.broadcast_to(run, cnt_ref.shape)


def _route_rows(i, c, rows, logits, run, tri_ref, route_ref, *, tiles_per_seq, seq_len, tm):
    sub = logits.shape[0]
    lane = lax.broadcasted_iota(jnp.int32, logits.shape, 1)
    big = jnp.int32(4 * LANES)

    def first_argmax(vals, vmax):
        return jnp.min(jnp.where(vals == vmax, lane, big), axis=-1, keepdims=True)

    gl = jnp.where(lane < N_GROUPS, logits, NEG_BIG)
    gmax = jnp.max(gl, axis=-1, keepdims=True)
    g_val = 1.0 / jnp.sum(jnp.exp(gl - gmax), axis=-1, keepdims=True)
    g_idx = first_argmax(gl, gmax)
    lo = EXPERT_LANE0 + EXPERTS_PER_GROUP * g_idx
    el = jnp.where((lane >= lo) & (lane < lo + EXPERTS_PER_GROUP), logits, NEG_BIG)
    m1 = jnp.max(el, axis=-1, keepdims=True)
    i1 = first_argmax(el, m1)
    el2 = jnp.where(lane == i1, NEG_BIG, el)
    m2 = jnp.max(el2, axis=-1, keepdims=True)
    i2 = first_argmax(el2, m2)
    r = jnp.exp(m2 - m1)
    gate1 = g_val / (1.0 + r)
    gate2 = g_val * r / (1.0 + r)

    prow = (i % tiles_per_seq) * tm + c * sub + lax.broadcasted_iota(jnp.int32, logits.shape, 0)
    valid = prow < seq_len
    oh1 = jnp.where(valid & (lane == i1), 1.0, 0.0)
    oh2 = jnp.where(valid & (lane == i2), 1.0, 0.0)
    pre = jnp.dot(tri_ref[...], jnp.concatenate([oh1, oh2], axis=1).astype(BF16), preferred_element_type=F32)
    pre1 = pre[:, :LANES]
    pre2 = pre[:, LANES:]
    tot1 = jnp.sum(oh1, axis=0, keepdims=True)
    tot2 = jnp.sum(oh2, axis=0, keepdims=True)
    rank1 = jnp.sum(oh1 * (pre1 + run), axis=-1, keepdims=True)
    rank2 = jnp.sum(oh2 * (pre2 + run + tot1), axis=-1, keepdims=True)

    e1 = (i1 - EXPERT_LANE0).astype(F32)
    e2 = (i2 - EXPERT_LANE0).astype(F32)
    packed = jnp.where(lane == 0, e1, jnp.where(lane == 1, e2, jnp.where(lane == 2, gate1, jnp.where(
        lane == 3, gate2, jnp.where(lane == 4, rank1, jnp.where(lane == 5, rank2, 0.0))))))
    route_ref[rows, :] = packed[:, 0:ROUTE_COLS]
    return run + tot1 + tot2


def _outproj(hp, convy, o, w_out, g2, wr_hi, wr_lo, br, tri, *, tiles_per_seq, seq_len, tm):
    n, d = hp.shape
    const = lambda i: (0, 0)
    tile = lambda i: (i, 0)
    kern = functools.partial(_outproj_kernel, tiles_per_seq=tiles_per_seq, seq_len=seq_len)
    return pl.pallas_call(
        kern,
        grid=(n // tm,),
        in_specs=[
            pl.BlockSpec((tm, d), tile),
            pl.BlockSpec((tm, convy.shape[1]), tile),
            pl.BlockSpec((tm, o.shape[1]), tile),
            pl.BlockSpec(w_out.shape, const),
            pl.BlockSpec((1, d), const),
            pl.BlockSpec(wr_hi.shape, const),
            pl.BlockSpec(wr_lo.shape, const),
            pl.BlockSpec((1, LANES), const),
            pl.BlockSpec(tri.shape, const),
        ],
        out_specs=[
            pl.BlockSpec((tm, d), tile),
            *[pl.BlockSpec((tm, LANES), tile)] * ROW_PLANES,
            pl.BlockSpec((tm, ROUTE_COLS), tile),
            pl.BlockSpec((8, LANES), const),
        ],
        out_shape=[
            jax.ShapeDtypeStruct((n, d), F32),
            *[jax.ShapeDtypeStruct((n, LANES), jnp.uint32)] * ROW_PLANES,
            jax.ShapeDtypeStruct((n, ROUTE_COLS), F32),
            jax.ShapeDtypeStruct((8, LANES), F32),
        ],
        scratch_shapes=[pltpu.VMEM((8, LANES), F32)],
        compiler_params=_cparams(1),
        name="outproj_router",
    )(hp, convy, o, w_out, g2, wr_hi, wr_lo, br, tri)


def _outproj_t_kernel(hp_ref, cy_ref, o_ref, wout_ref, g2_ref, wrh_ref, wrl_ref, br_ref, upper_ref,
                      h1_ref, xp0_ref, xp1_ref, xp2_ref, xp3_ref, route_ref, cnt_ref, run_ref,
                      *, tiles_per_seq, seq_len):
    i = pl.program_id(0)
    tm = hp_ref.shape[0]

    @pl.when(i == 0)
    def _():
        run_ref[...] = jnp.zeros_like(run_ref)

    mix = jnp.concatenate([cy_ref[...], o_ref[...]], axis=1)
    h1 = hp_ref[...] + jnp.dot(mix, wout_ref[...], preferred_element_type=F32)
    h1_ref[...] = h1
    ms = jnp.mean(h1 * h1, axis=-1, keepdims=True)
    xn = h1 * lax.rsqrt(ms + EPS) * g2_ref[...]
    xw = _pack_rows(xn)
    for c, ref in enumerate((xp0_ref, xp1_ref, xp2_ref, xp3_ref)):
        ref[...] = xw[:, c * LANES:(c + 1) * LANES]

    x_hi = xn.astype(BF16)
    x_lo = (xn - x_hi.astype(F32)).astype(BF16)
    hi_both = jnp.dot(x_hi, jnp.concatenate([wrh_ref[...], wrl_ref[...]], axis=1), preferred_element_type=F32)
    logits = (hi_both[:, :LANES] + hi_both[:, LANES:]
              + jnp.dot(x_lo, wrh_ref[...], preferred_element_type=F32) + br_ref[...])

    lt = logits.T[0:ROUTE_ROWS, :]
    row = lax.broadcasted_iota(jnp.int32, lt.shape, 0)
    big = jnp.int32(4 * LANES)

    def first_argmax(vals, vmax):
        return jnp.min(jnp.where(vals == vmax, row, big), axis=0, keepdims=True)

    gl = jnp.where(row < N_GROUPS, lt, NEG_BIG)
    gmax = jnp.max(gl, axis=0, keepdims=True)
    g_val = 1.0 / jnp.sum(jnp.exp(gl - gmax), axis=0, keepdims=True)
    g_idx = first_argmax(gl, gmax)
    lo = EXPERT_LANE0 + EXPERTS_PER_GROUP * g_idx
    el = jnp.where((row >= lo) & (row < lo + EXPERTS_PER_GROUP), lt, NEG_BIG)
    m1 = jnp.max(el, axis=0, keepdims=True)
    i1 = first_argmax(el, m1)
    el2 = jnp.where(row == i1, NEG_BIG, el)
    m2 = jnp.max(el2, axis=0, keepdims=True)
    i2 = first_argmax(el2, m2)
    r = jnp.exp(m2 - m1)
    gate1 = g_val / (1.0 + r)
    gate2 = g_val * r / (1.0 + r)

    pos = (i % tiles_per_seq) * tm + lax.broadcasted_iota(jnp.int32, (1, tm), 1)
    valid = pos < seq_len
    oh1 = jnp.where(valid & (row == i1), 1.0, 0.0)
    oh2 = jnp.where(valid & (row == i2), 1.0, 0.0)
    pre = jnp.dot(jnp.concatenate([oh1, oh2], axis=0).astype(BF16), upper_ref[...], preferred_element_type=F32)
    tot1 = jnp.sum(oh1, axis=1, keepdims=True)
    tot2 = jnp.sum(oh2, axis=1, keepdims=True)
    run = run_ref[...]
    run_t = jnp.tile(run, (1, tm // LANES))
    rank1 = jnp.sum(oh1 * (pre[:ROUTE_ROWS] + run_t), axis=0, keepdims=True)
    rank2 = jnp.sum(oh2 * (pre[ROUTE_ROWS:] + run_t + tot1), axis=0, keepdims=True)
    new_run = run + tot1 + tot2
    run_ref[...] = new_run
    cnt_ref[...] = new_run

    e1 = (i1 - EXPERT_LANE0).astype(F32)
    e2 = (i2 - EXPERT_LANE0).astype(F32)
    r8 = lax.broadcasted_iota(jnp.int32, (ROUTE_COLS, tm), 0)
    route_ref[...] = jnp.where(r8 == 0, e1, jnp.where(r8 == 1, e2, jnp.where(r8 == 2, gate1, jnp.where(
        r8 == 3, gate2, jnp.where(r8 == 4, rank1, jnp.where(r8 == 5, rank2, 0.0))))))


def _outproj_t(hp, convy, o, w_out, g2, wr_hi, wr_lo, br, upper, *, tiles_per_seq, seq_len, tm):
    n, d = hp.shape
    const = lambda i: (0, 0)
    tile = lambda i: (i, 0)
    kern = functools.partial(_outproj_t_kernel, tiles_per_seq=tiles_per_seq, seq_len=seq_len)
    return pl.pallas_call(
        kern,
        grid=(n // tm,),
        in_specs=[
            pl.BlockSpec((tm, d), tile),
            pl.BlockSpec((tm, convy.shape[1]), tile),
            pl.BlockSpec((tm, o.shape[1]), tile),
            pl.BlockSpec(w_out.shape, const),
            pl.BlockSpec((1, d), const),
            pl.BlockSpec(wr_hi.shape, const),
            pl.BlockSpec(wr_lo.shape, const),
            pl.BlockSpec((1, LANES), const),
            pl.BlockSpec(upper.shape, const),
        ],
        out_specs=[
            pl.BlockSpec((tm, d), tile),
            *[pl.BlockSpec((tm, LANES), tile)] * ROW_PLANES,
            pl.BlockSpec((ROUTE_COLS, tm), lambda i: (0, i)),
            pl.BlockSpec((ROUTE_ROWS, LANES), const),
        ],
        out_shape=[
            jax.ShapeDtypeStruct((n, d), F32),
            *[jax.ShapeDtypeStruct((n, LANES), jnp.uint32)] * ROW_PLANES,
            jax.ShapeDtypeStruct((ROUTE_COLS, n), F32),
            jax.ShapeDtypeStruct((ROUTE_ROWS, LANES), F32),
        ],
        scratch_shapes=[pltpu.VMEM((ROUTE_ROWS, LANES), F32)],
        compiler_params=_cparams(1),
        name="outproj_router",
    )(hp, convy, o, w_out, g2, wr_hi, wr_lo, br, upper)


def _slots_kernel(pstart_ref, route_ref, dest_ref, *, seq_len, p_rows):
    b = pl.program_id(0)
    route = route_ref[...]
    eid = route.astype(jnp.int32)
    start = jnp.zeros_like(eid)
    for e in range(N_EXPERTS):
        start = jnp.where(eid == e, pstart_ref[e], start)
    rank = pltpu.roll(route, ROUTE_COLS - 2 * TOP_K, axis=0).astype(jnp.int32)
    k = lax.broadcasted_iota(jnp.int32, route.shape, 0)
    pos = lax.broadcasted_iota(jnp.int32, route.shape, 1)
    n_pad = route.shape[1] - seq_len
    spare = p_rows + (b * n_pad + (pos - seq_len)) * TOP_K + k
    dest_ref[...] = jnp.where(pos < seq_len, start + rank, spare)


def _slots(pstarts, route, *, batch, lp_len, seq_len, p_rows):
    kern = functools.partial(_slots_kernel, seq_len=seq_len, p_rows=p_rows)
    grid_spec = pltpu.PrefetchScalarGridSpec(
        num_scalar_prefetch=1,
        grid=(batch,),
        in_specs=[pl.BlockSpec((ROUTE_COLS, lp_len), lambda b, ps: (0, b))],
        out_specs=pl.BlockSpec((ROUTE_COLS, lp_len), lambda b, ps: (0, b)),
    )
    return pl.pallas_call(
        kern,
        grid_spec=grid_spec,
        out_shape=jax.ShapeDtypeStruct(route.shape, jnp.int32),
        compiler_params=_cparams(1),
        name="moe_slots",
    )(pstarts, route)


def _row_copy(src_hbm, src_row, dst_ref, dst_row, sem):
    return pltpu.make_async_copy(src_hbm.at[pl.ds(src_row, 1), :], dst_ref.at[pl.ds(dst_row, 1), :], sem)


def _dispatch_kernel(dest_ref, xn_hbm, xs_hbm, xbuf, sem_in, sem, *, tiles_per_seq, lp_len, tile):
    t = pl.program_id(0)
    base = (t // tiles_per_seq) * lp_len + (t % tiles_per_seq) * tile
    tile_copy = pltpu.make_async_copy(xn_hbm.at[pl.ds(base, tile), :], xbuf, sem_in)
    tile_copy.start()
    tile_copy.wait()

    def issue(r, carry):
        for kk in range(TOP_K):
            _row_copy(xbuf, r, xs_hbm, dest_ref[0, 0, TOP_K * r + kk], sem).start()
        return carry

    lax.fori_loop(0, tile, issue, 0)

    def drain(r, carry):
        for kk in range(TOP_K):
            _row_copy(xbuf, r, xs_hbm, dest_ref[0, 0, TOP_K * r + kk], sem).wait()
        return carry

    lax.fori_loop(0, tile, drain, 0)


def _dispatch(dest_tiles, xn, *, p_rows, tiles_per_seq, lp_len, tile):
    n_tiles = dest_tiles.shape[0]
    kern = functools.partial(_dispatch_kernel, tiles_per_seq=tiles_per_seq, lp_len=lp_len, tile=tile)
    return pl.pallas_call(
        kern,
        grid=(n_tiles,),
        in_specs=[
            pl.BlockSpec((1, 1, TOP_K * tile), lambda t: (t, 0, 0), memory_space=pltpu.SMEM),
            pl.BlockSpec(memory_space=pl.ANY),
        ],
        out_specs=pl.BlockSpec(memory_space=pl.ANY),
        out_shape=jax.ShapeDtypeStruct((p_rows, xn.shape[1]), xn.dtype),
        scratch_shapes=[pltpu.VMEM((tile, xn.shape[1]), xn.dtype), pltpu.SemaphoreType.DMA(()),
                        pltpu.SemaphoreType.DMA(())],
        compiler_params=_cparams(1),
        name="moe_dispatch",
    )(dest_tiles, xn)


def _experts_kernel(be_ref, nv_ref, xs_ref, wg_ref, wu_ref, wd_ref, y_ref, wgb_ref, wub_ref, wdb_ref):
    i = pl.program_id(0)
    e = be_ref[i]
    prev = be_ref[jnp.maximum(i - 1, 0)]

    @pl.when((i == 0) | (e != prev))
    def _():
        wgb_ref[...] = wg_ref[0].astype(BF16)
        wub_ref[...] = wu_ref[0].astype(BF16)
        wdb_ref[...] = wd_ref[0].astype(BF16)

    nv = nv_ref[i]

    @pl.when(nv > 0)
    def _():
        xs = xs_ref[...]
        row = lax.broadcasted_iota(jnp.int32, xs.shape, 0)
        x = jnp.where(row < nv, xs, 0.0).astype(BF16)
        hg = jnp.dot(x, wgb_ref[...], preferred_element_type=F32)
        hu = jnp.dot(x, wub_ref[...], preferred_element_type=F32)
        hid = (hg / (1.0 + jnp.exp(-hg)) * hu).astype(BF16)
        y_ref[...] = jnp.dot(hid, wdb_ref[...], preferred_element_type=F32)

    @pl.when(nv == 0)
    def _():
        y_ref[...] = jnp.zeros_like(y_ref)


def _experts(block_e, nvalid, xs, w_gate, w_up, w_down):
    p_rows, d = xs.shape
    ff = w_gate.shape[2]
    n_blocks = p_rows // MOE_BLOCK
    grid_spec = pltpu.PrefetchScalarGridSpec(
        num_scalar_prefetch=2,
        grid=(n_blocks,),
        in_specs=[
            pl.BlockSpec((MOE_BLOCK, d), lambda i, be, nv: (i, 0)),
            pl.BlockSpec((1, d, ff), lambda i, be, nv: (be[i], 0, 0)),
            pl.BlockSpec((1, d, ff), lambda i, be, nv: (be[i], 0, 0)),
            pl.BlockSpec((1, ff, d), lambda i, be, nv: (be[i], 0, 0)),
        ],
        out_specs=pl.BlockSpec((MOE_BLOCK, d), lambda i, be, nv: (i, 0)),
        scratch_shapes=[
            pltpu.VMEM((d, ff), BF16),
            pltpu.VMEM((d, ff), BF16),
            pltpu.VMEM((ff, d), BF16),
        ],
    )
    return pl.pallas_call(
        _experts_kernel,
        grid_spec=grid_spec,
        out_shape=jax.ShapeDtypeStruct((p_rows, d), F32),
        compiler_params=_cparams(1),
        name="moe_experts",
    )(block_e, nvalid, xs, w_gate, w_up, w_down)


def _combine_kernel(dest_ref, gates_ref, h1_hbm, y_hbm, out_ref, hbuf, ya, yb, sem_h, sem_a, sem_b,
                    *, lp_len, tile):
    b = pl.program_id(0)
    i = pl.program_id(1)
    start = b * lp_len + N_META + i * tile
    h_copy = pltpu.make_async_copy(h1_hbm.at[pl.ds(start, tile), :], hbuf, sem_h)
    h_copy.start()

    def issue(r, carry):
        _row_copy(y_hbm, dest_ref[0, 0, TOP_K * r], ya, r, sem_a).start()
        _row_copy(y_hbm, dest_ref[0, 0, TOP_K * r + 1], yb, r, sem_b).start()
        return carry

    lax.fori_loop(0, tile, issue, 0)

    def drain(r, carry):
        _row_copy(y_hbm, dest_ref[0, 0, TOP_K * r], ya, r, sem_a).wait()
        _row_copy(y_hbm, dest_ref[0, 0, TOP_K * r + 1], yb, r, sem_b).wait()
        return carry

    lax.fori_loop(0, tile, drain, 0)
    h_copy.wait()
    g = gates_ref[0]
    out_ref[0] = hbuf[...] + g[:, 0:1] * ya[...] + g[:, 1:2] * yb[...]


def _combine(dest_tiles, gates, h1, y, *, batch, seq, lp_len, tile):
    d = h1.shape[1]
    nt = seq // tile
    kern = functools.partial(_combine_kernel, lp_len=lp_len, tile=tile)
    return pl.pallas_call(
        kern,
        grid=(batch, nt),
        in_specs=[
            pl.BlockSpec((1, 1, TOP_K * tile), lambda b, i: (b * nt + i, 0, 0), memory_space=pltpu.SMEM),
            pl.BlockSpec((1, tile, TOP_K), lambda b, i: (b, i, 0)),
            pl.BlockSpec(memory_space=pl.ANY),
            pl.BlockSpec(memory_space=pl.ANY),
        ],
        out_specs=pl.BlockSpec((1, tile, d), lambda b, i: (b, i, 0)),
        out_shape=jax.ShapeDtypeStruct((batch, seq, d), F32),
        scratch_shapes=[
            pltpu.VMEM((tile, d), F32),
            pltpu.VMEM((tile, d), F32),
            pltpu.VMEM((tile, d), F32),
            pltpu.SemaphoreType.DMA(()),
            pltpu.SemaphoreType.DMA(()),
            pltpu.SemaphoreType.DMA(()),
        ],
        compiler_params=_cparams(2),
        name="moe_combine",
    )(dest_tiles, gates, h1, y)


def _sc_workers():
    info = plsc.get_sparse_core_info()
    return info.num_cores, info.num_cores * info.num_subcores


def _sc_scatter_rows(planes, idx_a, idx_b, out_rows):
    n_win = idx_a.shape[0]
    n_cores, n_workers = _sc_workers()
    trips = -(-n_win // n_workers)
    mesh = plsc.VectorSubcoreMesh(core_axis_name="c", subcore_axis_name="s")

    def body(*refs):
        xs = refs[0:ROW_PLANES]
        ia_hbm, ib_hbm = refs[ROW_PLANES:ROW_PLANES + 2]
        outs = refs[ROW_PLANES + 2:2 * ROW_PLANES + 2]
        ia_v, ib_v, buf, sem = refs[2 * ROW_PLANES + 2:]
        wid = lax.axis_index("s") * n_cores + lax.axis_index("c")

        def step(t, carry):
            g = wid + t * n_workers

            @pl.when(g < n_win)
            def _():
                pltpu.sync_copy(ia_hbm.at[g], ia_v)
                pltpu.sync_copy(ib_hbm.at[g], ib_v)
                row0 = pl.multiple_of(g * SC_WINDOW, SC_WINDOW)
                loads = [pltpu.async_copy(xs[c].at[pl.ds(row0, SC_WINDOW)], buf.at[c], sem)
                         for c in range(ROW_PLANES)]
                for cp in loads:
                    cp.wait()
                stores = [pltpu.async_copy(buf.at[c], outs[c].at[iv], sem)
                          for c in range(ROW_PLANES) for iv in (ia_v, ib_v)]
                for cp in stores:
                    cp.wait()

            return carry

        lax.fori_loop(0, trips, step, 0)

    kern = pl.kernel(
        body,
        out_type=[jax.ShapeDtypeStruct((out_rows, LANES), jnp.uint32)] * ROW_PLANES,
        mesh=mesh,
        scratch_types=[
            pltpu.VMEM((SC_WINDOW,), jnp.int32),
            pltpu.VMEM((SC_WINDOW,), jnp.int32),
            pltpu.VMEM((ROW_PLANES, SC_WINDOW, LANES), jnp.uint32),
            pltpu.SemaphoreType.DMA,
        ],
        name="moe_dispatch_sc",
    )
    return kern(*planes, idx_a, idx_b)


def _sc_gather_rows(planes, idx_a, idx_b):
    n_win = idx_a.shape[0]
    n_cores, n_workers = _sc_workers()
    trips = -(-n_win // n_workers)
    mesh = plsc.VectorSubcoreMesh(core_axis_name="c", subcore_axis_name="s")

    def body(*refs):
        ys = refs[0:ROW_PLANES]
        ia_hbm, ib_hbm = refs[ROW_PLANES:ROW_PLANES + 2]
        outs_a = refs[ROW_PLANES + 2:2 * ROW_PLANES + 2]
        outs_b = refs[2 * ROW_PLANES + 2:3 * ROW_PLANES + 2]
        iv, buf, sem = refs[3 * ROW_PLANES + 2:]
        wid = lax.axis_index("s") * n_cores + lax.axis_index("c")

        def step(t, carry):
            g = wid + t * n_workers

            @pl.when(g < n_win)
            def _():
                row0 = pl.multiple_of(g * SC_WINDOW, SC_WINDOW)
                for i_hbm, outs in ((ia_hbm, outs_a), (ib_hbm, outs_b)):
                    pltpu.sync_copy(i_hbm.at[g], iv)
                    loads = [pltpu.async_copy(ys[c].at[iv], buf.at[c], sem) for c in range(ROW_PLANES)]
                    for cp in loads:
                        cp.wait()
                    stores = [pltpu.async_copy(buf.at[c], outs[c].at[pl.ds(row0, SC_WINDOW)], sem)
                              for c in range(ROW_PLANES)]
                    for cp in stores:
                        cp.wait()

            return carry

        lax.fori_loop(0, trips, step, 0)

    n_rows = n_win * SC_WINDOW
    kern = pl.kernel(
        body,
        out_type=[jax.ShapeDtypeStruct((n_rows, LANES), jnp.uint32)] * (2 * ROW_PLANES),
        mesh=mesh,
        scratch_types=[
            pltpu.VMEM((SC_WINDOW,), jnp.int32),
            pltpu.VMEM((ROW_PLANES, SC_WINDOW, LANES), jnp.uint32),
            pltpu.SemaphoreType.DMA,
        ],
        name="moe_gather_sc",
    )
    res = kern(*planes, idx_a, idx_b)
    return res[:ROW_PLANES], res[ROW_PLANES:]


def _experts_kernel(be_ref, nv_ref, first_ref, slot_ref, nxt_ref, x0_ref, x1_ref, x2_ref, x3_ref,
                    wg_hbm, wu_hbm, wd_hbm, y0_ref, y1_ref, y2_ref, y3_ref,
                    wgf_ref, wuf_ref, wdf_ref, wgb_ref, wub_ref, wdb_ref, sem):
    i = pl.program_id(0)
    e = be_ref[i]
    slot = slot_ref[i]
    y_refs = (y0_ref, y1_ref, y2_ref, y3_ref)

    def weight_copies(expert, s):
        return [pltpu.make_async_copy(hbm.at[expert], stage.at[s], sem.at[s, j])
                for j, (hbm, stage) in enumerate(((wg_hbm, wgf_ref), (wu_hbm, wuf_ref), (wd_hbm, wdf_ref)))]

    @pl.when(i == 0)
    def _():
        for cp in weight_copies(e, slot):
            cp.start()

    @pl.when(first_ref[i] == 1)
    def _():
        for cp in weight_copies(e, slot):
            cp.wait()
        nxt = nxt_ref[i]

        @pl.when(nxt >= 0)
        def _():
            for cp in weight_copies(nxt, 1 - slot):
                cp.start()

        wgb_ref[...] = wgf_ref[slot].astype(BF16)
        wub_ref[...] = wuf_ref[slot].astype(BF16)
        wdb_ref[...] = wdf_ref[slot].astype(BF16)

    nv = nv_ref[i]
    half = MOE_BLOCK // 2

    def mlp(rows):
        xs = _unpack_rows([r[rows, :] for r in (x0_ref, x1_ref, x2_ref, x3_ref)])
        row = lax.broadcasted_iota(jnp.int32, xs.shape, 0)
        x = jnp.where(row < nv, xs, jnp.zeros_like(xs))
        hg = jnp.dot(x, wgb_ref[...], preferred_element_type=F32)
        hu = jnp.dot(x, wub_ref[...], preferred_element_type=F32)
        hid = (hg / (1.0 + jnp.exp(-hg)) * hu).astype(BF16)
        yw = _pack_rows(jnp.dot(hid, wdb_ref[...], preferred_element_type=F32))
        for c, ref in enumerate(y_refs):
            ref[rows, :] = yw[:, c * LANES:(c + 1) * LANES]

    def clear(rows):
        for ref in y_refs:
            ref[rows, :] = jnp.zeros((rows.size, LANES), ref.dtype)

    @pl.when(nv > half)
    def _():
        mlp(pl.ds(0, MOE_BLOCK))

    @pl.when((nv > 0) & (nv <= half))
    def _():
        mlp(pl.ds(0, half))
        clear(pl.ds(half, half))

    @pl.when(nv == 0)
    def _():
        clear(pl.ds(0, MOE_BLOCK))


def _experts(block_e, nvalid, xs_planes, w_gate, w_up, w_down):
    n_blocks = block_e.shape[0]
    _, d, ff = w_gate.shape
    first = jnp.concatenate([jnp.ones((1,), jnp.int32), (block_e[1:] != block_e[:-1]).astype(jnp.int32)])
    slot = (jnp.cumsum(first) - 1) % 2
    later = jnp.where(block_e[None, :] > block_e[:, None], block_e[None, :], N_EXPERTS)
    nxt = jnp.min(later, axis=1)
    nxt = jnp.where(nxt == N_EXPERTS, -1, nxt).astype(jnp.int32)
    blk = lambda i, *_: (i, 0)
    grid_spec = pltpu.PrefetchScalarGridSpec(
        num_scalar_prefetch=5,
        grid=(n_blocks,),
        in_specs=[
            *[pl.BlockSpec((MOE_BLOCK, LANES), blk)] * ROW_PLANES,
            pl.BlockSpec(memory_space=pl.ANY),
            pl.BlockSpec(memory_space=pl.ANY),
            pl.BlockSpec(memory_space=pl.ANY),
        ],
        out_specs=[pl.BlockSpec((MOE_BLOCK, LANES), blk)] * ROW_PLANES,
        scratch_shapes=[
            pltpu.VMEM((2, d, ff), F32),
            pltpu.VMEM((2, d, ff), F32),
            pltpu.VMEM((2, ff, d), F32),
            pltpu.VMEM((d, ff), BF16),
            pltpu.VMEM((d, ff), BF16),
            pltpu.VMEM((ff, d), BF16),
            pltpu.SemaphoreType.DMA((2, 3)),
        ],
    )
    return pl.pallas_call(
        _experts_kernel,
        grid_spec=grid_spec,
        out_shape=[jax.ShapeDtypeStruct((n_blocks * MOE_BLOCK, LANES), jnp.uint32)] * ROW_PLANES,
        compiler_params=_cparams(1),
        name="moe_experts",
    )(block_e, nvalid, first, slot.astype(jnp.int32), nxt, *xs_planes, w_gate, w_up, w_down)


def _combine_kernel(gates_ref, h1_hbm, *refs, lp_len, tile):
    a_refs = refs[0:ROW_PLANES]
    b_refs = refs[ROW_PLANES:2 * ROW_PLANES]
    out_ref, hbuf, sem_h = refs[2 * ROW_PLANES:]
    nt = pl.num_programs(1)
    step = pl.program_id(0) * nt + pl.program_id(1)
    last = pl.num_programs(0) * nt - 1

    def h_copy(s, slot):
        start = (s // nt) * lp_len + N_META + (s % nt) * tile
        return pltpu.make_async_copy(h1_hbm.at[pl.ds(start, tile), :], hbuf.at[slot], sem_h.at[slot])

    slot = step % 2

    @pl.when(step == 0)
    def _():
        h_copy(step, slot).start()

    @pl.when(step < last)
    def _():
        h_copy(step + 1, 1 - slot).start()

    ya = _unpack_rows([r[...] for r in a_refs]).astype(F32)
    yb = _unpack_rows([r[...] for r in b_refs]).astype(F32)
    g = jnp.concatenate([gates_ref[...]] * (LANES // ROUTE_COLS), axis=0).T
    moe = g[:, 0:1] * ya + g[:, 1:2] * yb
    h_copy(step, slot).wait()
    out_ref[0] = hbuf[slot] + moe


def _combine(gates, h1, a_planes, b_planes, *, batch, seq, lp_len, tile):
    d = h1.shape[1]
    nt = seq // tile
    kern = functools.partial(_combine_kernel, lp_len=lp_len, tile=tile)
    rows = lambda b, i: (b * nt + i, 0)
    return pl.pallas_call(
        kern,
        grid=(batch, nt),
        in_specs=[
            pl.BlockSpec((ROUTE_COLS, tile), lambda b, i: (0, b * nt + i)),
            pl.BlockSpec(memory_space=pl.ANY),
            *[pl.BlockSpec((tile, LANES), rows)] * (2 * ROW_PLANES),
        ],
        out_specs=pl.BlockSpec((1, tile, d), lambda b, i: (b, i, 0)),
        out_shape=jax.ShapeDtypeStruct((batch, seq, d), F32),
        scratch_shapes=[pltpu.VMEM((2, tile, d), F32), pltpu.SemaphoreType.DMA((2,))],
        compiler_params=_cparams(2),
        name="moe_combine",
    )(gates, h1, *a_planes, *b_planes)


def _rope_tables(length, lp_len):
    half = ROPE_DIM // 2
    pos = jnp.arange(length, dtype=F32)
    inv_freq = ROPE_THETA ** (-jnp.arange(0, ROPE_DIM, 2, dtype=F32) / ROPE_DIM)
    ang = pos[:, None] * inv_freq[None, :]
    cos = jnp.cos(ang)
    sin = jnp.sin(ang)
    ones = jnp.ones((length, HEAD_DIM - ROPE_DIM), F32)
    zeros_h = jnp.zeros((length, half), F32)
    zeros_r = jnp.zeros((length, HEAD_DIM - ROPE_DIM), F32)
    c = jnp.concatenate([cos, cos, ones], axis=1)
    s1 = jnp.concatenate([zeros_h, sin, zeros_r], axis=1)
    s2 = jnp.concatenate([-sin, zeros_h, zeros_r], axis=1)
    pad = ((0, lp_len - length), (0, 0))
    rep = LANES // HEAD_DIM
    return tuple(jnp.pad(jnp.tile(t, (1, rep)), pad) for t in (c, s1, s2))


def _layer(x, meta, l, batch, length, lp_len, tm, norm1_g, w_in, conv_w, q_norm_g, k_norm_g, lambda_q1, lambda_k1,
           lambda_q2, lambda_k2, subln_g, w_out, norm2_g, w_router_group, b_router_group, w_router_expert,
           b_router_expert, w_gate, w_up, w_down, rope, last):
    d = x.shape[2]
    tiles_per_seq = lp_len // tm
    cw = conv_w.shape[2]
    qw = N_HEADS * 2 * HEAD_DIM
    lam_init = 0.8 - 0.6 * math.exp(-0.3 * l)

    reps = qw // HEAD_DIM
    gq = jnp.tile(q_norm_g[l] * (HEAD_DIM ** -0.5 * LOG2E), reps)[None, :]
    gk = jnp.tile(k_norm_g[l], reps)[None, :]
    seg = jnp.arange(qw) // HEAD_DIM
    bd = (seg[:, None] == seg[None, :]).astype(BF16)
    hp, convy, q, k, v = _inproj(x, meta, norm1_g[l][None, :], w_in[l].astype(BF16), conv_w[l], gq, gk, bd,
                                 *rope, tiles_per_seq=tiles_per_seq, tm=tm)

    lamp = jnp.stack([lambda_q1[l], lambda_k1[l], lambda_q2[l], lambda_k2[l]]).astype(F32)
    o = _attention(q, k, v, lamp, subln_g[l][None, :], batch=batch, lp_len=lp_len, tq=tm, lam_init=lam_init)

    lane_pad = LANES - N_GROUPS - N_EXPERTS
    wr = jnp.pad(jnp.concatenate([w_router_group[l], w_router_expert[l]], axis=1), ((0, 0), (0, lane_pad)))
    wr_hi = wr.astype(BF16)
    wr_lo = (wr - wr_hi.astype(F32)).astype(BF16)
    br = jnp.pad(jnp.concatenate([b_router_group[l], b_router_expert[l]]), (0, lane_pad))[None, :]
    ridx = jnp.arange(tm)
    upper = (ridx[:, None] < ridx[None, :]).astype(BF16)
    h1, *rest = _outproj_t(hp, convy, o, w_out[l].astype(BF16), norm2_g[l][None, :], wr_hi, wr_lo, br,
                           upper, tiles_per_seq=tiles_per_seq, seq_len=length, tm=tm)
    x_planes = rest[:ROW_PLANES]
    route, cnt = rest[ROW_PLANES:]

    counts = cnt[EXPERT_LANE0:EXPERT_LANE0 + N_EXPERTS, 0].astype(jnp.int32)
    a = batch * length * TOP_K
    n_blocks = -(-a // MOE_BLOCK) + N_EXPERTS
    p_rows = n_blocks * MOE_BLOCK
    padded = (counts + MOE_BLOCK - 1) // MOE_BLOCK * MOE_BLOCK
    pends = jnp.cumsum(padded)
    pstarts = pends - padded

    def lookup(table, idx):
        hit = idx[:, None] == jnp.arange(N_EXPERTS, dtype=jnp.int32)[None, :]
        return jnp.sum(jnp.where(hit, table[None, :], 0), axis=1)

    n_pad = lp_len - length
    spare_rows = -(-(batch * n_pad * TOP_K) // MOE_BLOCK) * MOE_BLOCK
    dest = _slots(pstarts.astype(jnp.int32), route, batch=batch, lp_len=lp_len, seq_len=length, p_rows=p_rows)
    dest = dest.reshape(ROUTE_COLS, batch, lp_len)
    gates = route.reshape(ROUTE_COLS, batch, lp_len)[TOP_K:2 * TOP_K]
    blk0 = jnp.arange(n_blocks, dtype=jnp.int32) * MOE_BLOCK
    block_e = jnp.minimum(jnp.sum((pends[None, :] <= blk0[:, None]).astype(jnp.int32), axis=1), N_EXPERTS - 1)
    nvalid = jnp.clip(lookup(counts, block_e) - (blk0 - lookup(pstarts, block_e)), 0, MOE_BLOCK)

    assert (batch * lp_len) % SC_WINDOW == 0
    xs_planes = _sc_scatter_rows(x_planes, dest[0].reshape(-1, SC_WINDOW), dest[1].reshape(-1, SC_WINDOW),
                                 p_rows + spare_rows)
    y_planes = _experts(block_e, nvalid, xs_planes, w_gate[l], w_up[l], w_down[l])

    if not last:
        raise NotImplementedError("only the final layer's combine (which drops the meta tokens) is implemented")
    seq = length - N_META
    assert (batch * seq) % SC_WINDOW == 0
    dest_x = dest[0:TOP_K, :, N_META:length]
    a_planes, b_planes = _sc_gather_rows(y_planes, dest_x[0].reshape(-1, SC_WINDOW),
                                         dest_x[1].reshape(-1, SC_WINDOW))
    ctile = _largest_tile(seq, 512, LANES)
    gates_x = jnp.pad(gates[:, :, N_META:length].reshape(TOP_K, batch * seq), ((0, ROUTE_COLS - TOP_K), (0, 0)))
    return _combine(gates_x, h1, a_planes, b_planes, batch=batch, seq=seq, lp_len=lp_len, tile=ctile)


def kernel(x, meta_tokens, norm1_g, w_in, conv_w, q_norm_g, k_norm_g, lambda_q1, lambda_k1, lambda_q2, lambda_k2,
           subln_g, w_out, norm2_g, w_router_group, b_router_group, w_router_expert, b_router_expert, w_gate,
           w_up, w_down):
    b, s, d = x.shape
    depth = w_in.shape[0]
    assert depth == 1, "a single layer is supported"
    length = s + N_META
    tm = TOKEN_TILE
    lp_len = -(-length // tm) * tm
    assert lp_len // tm >= 2 and (length - (lp_len // tm - 1) * tm) % 8 == 0
    rope = _rope_tables(length, lp_len)
    return _layer(x, meta_tokens.astype(x.dtype), 0, b, length, lp_len, tm, norm1_g, w_in, conv_w, q_norm_g, k_norm_g, lambda_q1, lambda_k1,
                  lambda_q2, lambda_k2, subln_g, w_out, norm2_g, w_router_group, b_router_group,
                  w_router_expert, b_router_expert, w_gate, w_up, w_down, rope, last=True)
```

```python
import functools
import math

import jax
import jax.numpy as jnp
from jax import lax
from jax.experimental import pallas as pl
from jax.experimental.pallas import tpu as pltpu
from jax.experimental.pallas import tpu_sc as plsc

F32 = jnp.float32
BF16 = jnp.bfloat16

N_META = 16
CONV_K = 3
N_HEADS = 4
HEAD_DIM = 64
V_DIM = 2 * HEAD_DIM
ROPE_DIM = HEAD_DIM // 4
ROPE_THETA = 500000.0
N_GROUPS = 4
EXPERTS_PER_GROUP = 8
N_EXPERTS = N_GROUPS * EXPERTS_PER_GROUP
TOP_K = 2
EPS = 1e-6
LOG2E = 1.4426950408889634

LANES = 128
TOKEN_TILE = 640
INPROJ_CHAINS = 2
ATTN_HEADS_PER_STEP = 4
ATTN_WIDE_CHUNKS = 2
MOE_BLOCK = 512
ROUTE_COLS = 8
ROUTE_ROWS = 64
ROW_PLANES = 4
SC_WINDOW = 128
EXPERT_LANE0 = N_GROUPS
NEG_BIG = -1e30
VMEM_LIMIT = 56 * 1024 * 1024


def _largest_tile(n, cap, mult):
    for t in range(min(cap, n), 0, -1):
        if n % t == 0 and t % mult == 0:
            return t
    raise ValueError(f"no tile for {n}")


def _cparams(n_axes, flags=None):
    return pltpu.CompilerParams(dimension_semantics=("arbitrary",) * n_axes,
                                vmem_limit_bytes=VMEM_LIMIT, flags=flags)


def _inproj_kernel(x_hbm, meta_hbm, g1_ref, win_ref, convw_ref, gq_ref, gk_ref, bd_ref, rc_ref, rs1_ref, rs2_ref,
                   hp_ref, convy_ref, q_ref, k_ref, v_ref, carry_ref, xbuf, sem, *, tiles_per_seq, seq, cw, qw):
    i = pl.program_id(0)
    tm = hp_ref.shape[0]
    sub = tm // INPROJ_CHAINS
    q0 = 3 * cw
    w = convw_ref[...]
    last_rows = seq + N_META - (tiles_per_seq - 1) * tm

    def fetch(step, slot, start):
        b = step // tiles_per_seq
        t = step % tiles_per_seq

        def go(src, dst):
            cp = pltpu.make_async_copy(src, dst, sem.at[slot])
            if start:
                cp.start()
            else:
                cp.wait()

        @pl.when(t == 0)
        def _():
            go(meta_hbm, xbuf.at[slot, pl.ds(0, N_META)])
            go(x_hbm.at[pl.ds(b * seq, tm - N_META)], xbuf.at[slot, pl.ds(N_META, tm - N_META)])

        @pl.when((t > 0) & (t < tiles_per_seq - 1))
        def _():
            go(x_hbm.at[pl.ds(b * seq + t * tm - N_META, tm)], xbuf.at[slot])

        @pl.when(t == tiles_per_seq - 1)
        def _():
            go(x_hbm.at[pl.ds(b * seq + t * tm - N_META, last_rows)], xbuf.at[slot, pl.ds(0, last_rows)])

    slot = i % 2

    @pl.when(i == 0)
    def _():
        fetch(i, slot, True)

    @pl.when(i + 1 < pl.num_programs(0))
    def _():
        fetch(i + 1, 1 - slot, True)

    fetch(i, slot, False)

    @pl.when(i % tiles_per_seq == tiles_per_seq - 1)
    def _():
        xbuf[slot, pl.ds(last_rows, tm - last_rows), :] = jnp.zeros((tm - last_rows, xbuf.shape[2]), xbuf.dtype)

    hp_ref[...] = xbuf[slot]

    @pl.when(i % tiles_per_seq == 0)
    def _():
        carry_ref[...] = jnp.zeros_like(carry_ref)

    prev = carry_ref[...]

    for chain in range(INPROJ_CHAINS):
        rows = pl.ds(chain * sub, sub)
        x = xbuf[slot, rows, :]
        ms = jnp.mean(x * x, axis=-1, keepdims=True)
        xn = (x * lax.rsqrt(ms + EPS) * g1_ref[...]).astype(BF16)

        def proj(lo, hi):
            return jnp.dot(xn, win_ref[:, lo:hi], preferred_element_type=F32)

        u_conv = proj(0, q0)
        u_q = proj(q0, q0 + qw)

        z = u_conv[:, cw:2 * cw] * u_conv[:, 2 * cw:3 * cw]
        p1 = prev[7:8]
        p2 = prev[6:7]
        row = lax.broadcasted_iota(jnp.int32, z.shape, 0)
        z1 = jnp.where(row == 0, p1, pltpu.roll(z, 1, axis=0))
        z2 = jnp.where(row == 0, p2, jnp.where(row == 1, p1, pltpu.roll(z, 2, axis=0)))
        prev = z[sub - 8:sub]
        conv = w[0:1] * z2 + w[1:2] * z1 + w[2:3] * z
        convy_ref[rows, :] = (u_conv[:, 0:cw] * conv).astype(BF16)

        rc = rc_ref[rows, :]
        rs1 = rs1_ref[rows, :]
        rs2 = rs2_ref[rows, :]

        def norm_rope(t, g_ref):
            ss = jnp.dot((t * t).astype(BF16), bd_ref[...], preferred_element_type=F32)
            tn = t * lax.rsqrt(ss * (1.0 / HEAD_DIM) + EPS) * g_ref[...]
            outs = []
            for c in range(qw // LANES):
                ch = tn[:, c * LANES:(c + 1) * LANES]
                outs.append(ch * rc + pltpu.roll(ch, ROPE_DIM // 2, axis=1) * rs1
                            + pltpu.roll(ch, LANES - ROPE_DIM // 2, axis=1) * rs2)
            return jnp.concatenate(outs, axis=1).astype(BF16)

        u_k = proj(q0 + qw, q0 + 2 * qw)
        q_ref[rows, :] = norm_rope(u_q, gq_ref)
        u_v = proj(q0 + 2 * qw, win_ref.shape[1])
        k_ref[rows, :] = norm_rope(u_k, gk_ref)
        v_ref[rows, :] = u_v.astype(BF16)

    carry_ref[...] = prev


def _inproj(x, meta, g1, w_in, conv_w, gq, gk, bd, rc, rs1, rs2, *, tiles_per_seq, tm):
    batch, seq, d = x.shape
    n = batch * tiles_per_seq * tm
    cw = conv_w.shape[1]
    qw = gq.shape[1]
    aw = w_in.shape[1] - 3 * cw - 2 * qw
    const = lambda i: (0, 0)
    tile = lambda i: (i, 0)
    pos = lambda i: (i % tiles_per_seq, 0)
    kern = functools.partial(_inproj_kernel, tiles_per_seq=tiles_per_seq, seq=seq, cw=cw, qw=qw)
    return pl.pallas_call(
        kern,
        grid=(n // tm,),
        in_specs=[
            pl.BlockSpec(memory_space=pl.ANY),
            pl.BlockSpec(memory_space=pl.ANY),
            pl.BlockSpec((1, d), const),
            pl.BlockSpec(w_in.shape, const),
            pl.BlockSpec(conv_w.shape, const),
            pl.BlockSpec((1, qw), const),
            pl.BlockSpec((1, qw), const),
            pl.BlockSpec(bd.shape, const),
            pl.BlockSpec((tm, LANES), pos),
            pl.BlockSpec((tm, LANES), pos),
            pl.BlockSpec((tm, LANES), pos),
        ],
        out_specs=[
            pl.BlockSpec((tm, d), tile),
            pl.BlockSpec((tm, cw), tile),
            pl.BlockSpec((tm, qw), tile),
            pl.BlockSpec((tm, qw), tile),
            pl.BlockSpec((tm, aw), tile),
        ],
        out_shape=[
            jax.ShapeDtypeStruct((n, d), x.dtype),
            jax.ShapeDtypeStruct((n, cw), BF16),
            jax.ShapeDtypeStruct((n, qw), BF16),
            jax.ShapeDtypeStruct((n, qw), BF16),
            jax.ShapeDtypeStruct((n, aw), BF16),
        ],
        scratch_shapes=[pltpu.VMEM((8, cw), F32), pltpu.VMEM((2, tm, d), x.dtype),
                        pltpu.SemaphoreType.DMA((2,))],
        compiler_params=_cparams(1),
        name="inproj",
    )(x.reshape(batch * seq, d), meta, g1, w_in, conv_w, gq, gk, bd, rc, rs1, rs2)


def _attn_kernel(q_ref, k_ref, v_ref, lamp_ref, sg_ref, o_ref, qs_ref, m_ref, l_ref, acc_ref, *, lam_init):
    qi = pl.program_id(2)
    tq = q_ref.shape[0]
    n_heads = q_ref.shape[1] // LANES
    n_chains = 2 * n_heads
    lane = lax.broadcasted_iota(jnp.int32, (tq, LANES), 1)
    for h in range(n_heads):
        q = q_ref[:, h * LANES:(h + 1) * LANES]
        zero = jnp.zeros_like(q)
        qs_ref[pl.ds(2 * h * tq, tq), :] = jnp.where(lane < HEAD_DIM, q, zero)
        qs_ref[pl.ds((2 * h + 1) * tq, tq), :] = jnp.where(lane >= HEAD_DIM, q, zero)
    m_ref[...] = jnp.full_like(m_ref, NEG_BIG)
    l_ref[...] = jnp.zeros_like(l_ref)
    acc_ref[...] = jnp.zeros_like(acc_ref)

    def scores(off, width, which):
        h = which // 2
        kc = k_ref[pl.ds(off, width), h * LANES:(h + 1) * LANES]
        return lax.dot_general(qs_ref[pl.ds(which * tq, tq), :], kc, (((1,), (1,)), ((), ())),
                               preferred_element_type=F32)

    def update(off, width, which, s, masked):
        h = which // 2
        vc = jnp.concatenate([v_ref[pl.ds(off, width), h * LANES:(h + 1) * LANES],
                              jnp.ones((width, LANES), BF16)], axis=1)
        rows = pl.ds(which * tq, tq)
        if masked:
            r = lax.broadcasted_iota(jnp.int32, s.shape, 0)
            c = lax.broadcasted_iota(jnp.int32, s.shape, 1)
            s = jnp.where(c <= r, s, NEG_BIG)
        m_prev = m_ref[rows, :]
        m_new = jnp.maximum(m_prev, jnp.max(s, axis=-1, keepdims=True))
        alpha = jnp.exp2(m_prev - m_new)
        p = jnp.exp2((s - jnp.tile(m_new, (1, width // LANES))).astype(BF16))
        pv = jnp.dot(p, vc, preferred_element_type=F32)
        l_ref[rows, :] = alpha * l_ref[rows, :] + pv[:, LANES:]
        acc_ref[rows, :] = alpha * acc_ref[rows, :] + pv[:, :LANES]
        m_ref[rows, :] = m_new

    def chunk(off, width, masked):
        s_next = scores(off, width, 0)
        for c in range(n_chains):
            s = s_next
            if c + 1 < n_chains:
                s_next = scores(off, width, c + 1)
            update(off, width, c, s, masked)

    wide = ATTN_WIDE_CHUNKS * tq

    def body(j, carry):
        chunk(pl.multiple_of(j * wide, wide), wide, False)
        return carry

    n_wide = qi // ATTN_WIDE_CHUNKS
    lax.fori_loop(0, n_wide, body, 0)
    for extra in range(ATTN_WIDE_CHUNKS - 1):
        @pl.when(n_wide * ATTN_WIDE_CHUNKS + extra < qi)
        def _():
            chunk(pl.multiple_of((n_wide * ATTN_WIDE_CHUNKS + extra) * tq, tq), tq, False)

    chunk(pl.multiple_of(qi * tq, tq), tq, True)

    lp = lamp_ref[...]
    lam = (jnp.exp(jnp.sum(lp[0:1] * lp[1:2], axis=-1, keepdims=True))
           - jnp.exp(jnp.sum(lp[2:3] * lp[3:4], axis=-1, keepdims=True)) + lam_init)
    for h in range(n_heads):
        rows = pl.ds(2 * h * tq, 2 * tq)
        o_all = acc_ref[rows, :] / l_ref[rows, :]
        o = o_all[0:tq] - lam * o_all[tq:2 * tq]
        ms = jnp.mean(o * o, axis=-1, keepdims=True)
        o_ref[:, h * LANES:(h + 1) * LANES] = (o * lax.rsqrt(ms + EPS) * sg_ref[...]
                                               * (1.0 - lam_init)).astype(BF16)


def _attention(q, k, v, lamp, sg, *, batch, lp_len, tq, lam_init):
    n, qw = q.shape
    nq = lp_len // tq
    hw = ATTN_HEADS_PER_STEP * LANES
    chains = 2 * ATTN_HEADS_PER_STEP
    kern = functools.partial(_attn_kernel, lam_init=lam_init)
    return pl.pallas_call(
        kern,
        grid=(batch, qw // hw, nq),
        in_specs=[
            pl.BlockSpec((tq, hw), lambda b, h, i: (b * nq + i, h)),
            pl.BlockSpec((lp_len, hw), lambda b, h, i: (b, h)),
            pl.BlockSpec((lp_len, hw), lambda b, h, i: (b, h)),
            pl.BlockSpec(lamp.shape, lambda b, h, i: (0, 0)),
            pl.BlockSpec(sg.shape, lambda b, h, i: (0, 0)),
        ],
        out_specs=pl.BlockSpec((tq, hw), lambda b, h, i: (b * nq + i, h)),
        out_shape=jax.ShapeDtypeStruct((n, v.shape[1]), BF16),
        scratch_shapes=[
            pltpu.VMEM((chains * tq, LANES), BF16),
            pltpu.VMEM((chains * tq, LANES), F32),
            pltpu.VMEM((chains * tq, LANES), F32),
            pltpu.VMEM((chains * tq, LANES), F32),
        ],
        compiler_params=_cparams(3),
        name="diffattn",
    )(q, k, v, lamp, sg)


def _pack_rows(x):
    w = x.shape[1] // 2
    lo = lax.bitcast_convert_type(x[:, :w].astype(BF16).astype(F32), jnp.uint32)
    hi = lax.bitcast_convert_type(x[:, w:].astype(BF16).astype(F32), jnp.uint32)
    return lax.shift_right_logical(lo, jnp.uint32(16)) | (hi & jnp.uint32(0xFFFF0000))


def _unpack_rows(planes):
    w = jnp.concatenate(planes, axis=1)
    lo = lax.bitcast_convert_type(lax.shift_left(w, jnp.uint32(16)), F32)
    hi = lax.bitcast_convert_type(w & jnp.uint32(0xFFFF0000), F32)
    return jnp.concatenate([lo, hi], axis=1).astype(BF16)


def _outproj_kernel(hp_ref, cy_ref, o_ref, wout_ref, g2_ref, wrh_ref, wrl_ref, br_ref, tri_ref,
                    h1_ref, xp0_ref, xp1_ref, xp2_ref, xp3_ref, route_ref, cnt_ref, run_ref,
                    *, tiles_per_seq, seq_len):
    i = pl.program_id(0)
    tm = hp_ref.shape[0]
    sub = tri_ref.shape[0]
    wr_both = jnp.concatenate([wrh_ref[...], wrl_ref[...]], axis=1)

    @pl.when(i == 0)
    def _():
        run_ref[...] = jnp.zeros_like(run_ref)

    def project(rows):
        mix = jnp.concatenate([cy_ref[rows, :], o_ref[rows, :]], axis=1)
        h1 = hp_ref[rows, :] + jnp.dot(mix, wout_ref[...], preferred_element_type=F32)
        h1_ref[rows, :] = h1
        ms = jnp.mean(h1 * h1, axis=-1, keepdims=True)
        xn = h1 * lax.rsqrt(ms + EPS) * g2_ref[...]
        xw = _pack_rows(xn)
        for c, ref in enumerate((xp0_ref, xp1_ref, xp2_ref, xp3_ref)):
            ref[rows, :] = xw[:, c * LANES:(c + 1) * LANES]
        x_hi = xn.astype(BF16)
        x_lo = (xn - x_hi.astype(F32)).astype(BF16)
        hi_both = jnp.dot(x_hi, wr_both, preferred_element_type=F32)
        return (hi_both[:, :LANES] + hi_both[:, LANES:]
                + jnp.dot(x_lo, wrh_ref[...], preferred_element_type=F32) + br_ref[...])

    chains = [pl.ds(c * sub, sub) for c in range(tm // sub)]
    all_logits = [project(rows) for rows in chains]
    run = run_ref[0:1, :]
    for c, (rows, logits) in enumerate(zip(chains, all_logits)):
        run = _route_rows(i, c, rows, logits, run, tri_ref, route_ref, tiles_per_seq=tiles_per_seq,
                          seq_len=seq_len, tm=tm)
    run_ref[...] = jnp.broadcast_to(run, run_ref.shape)
    cnt_ref[...] = jnp.broadcast_to(run, cnt_ref.shape)


def _route_rows(i, c, rows, logits, run, tri_ref, route_ref, *, tiles_per_seq, seq_len, tm):
    sub = logits.shape[0]
    lane = lax.broadcasted_iota(jnp.int32, logits.shape, 1)
    big = jnp.int32(4 * LANES)

    def first_argmax(vals, vmax):
        return jnp.min(jnp.where(vals == vmax, lane, big), axis=-1, keepdims=True)

    gl = jnp.where(lane < N_GROUPS, logits, NEG_BIG)
    gmax = jnp.max(gl, axis=-1, keepdims=True)
    g_val = 1.0 / jnp.sum(jnp.exp(gl - gmax), axis=-1, keepdims=True)
    g_idx = first_argmax(gl, gmax)
    lo = EXPERT_LANE0 + EXPERTS_PER_GROUP * g_idx
    el = jnp.where((lane >= lo) & (lane < lo + EXPERTS_PER_GROUP), logits, NEG_BIG)
    m1 = jnp.max(el, axis=-1, keepdims=True)
    i1 = first_argmax(el, m1)
    el2 = jnp.where(lane == i1, NEG_BIG, el)
    m2 = jnp.max(el2, axis=-1, keepdims=True)
    i2 = first_argmax(el2, m2)
    r = jnp.exp(m2 - m1)
    gate1 = g_val / (1.0 + r)
    gate2 = g_val * r / (1.0 + r)

    prow = (i % tiles_per_seq) * tm + c * sub + lax.broadcasted_iota(jnp.int32, logits.shape, 0)
    valid = prow < seq_len
    oh1 = jnp.where(valid & (lane == i1), 1.0, 0.0)
    oh2 = jnp.where(valid & (lane == i2), 1.0, 0.0)
    pre = jnp.dot(tri_ref[...], jnp.concatenate([oh1, oh2], axis=1).astype(BF16), preferred_element_type=F32)
    pre1 = pre[:, :LANES]
    pre2 = pre[:, LANES:]
    tot1 = jnp.sum(oh1, axis=0, keepdims=True)
    tot2 = jnp.sum(oh2, axis=0, keepdims=True)
    rank1 = jnp.sum(oh1 * (pre1 + run), axis=-1, keepdims=True)
    rank2 = jnp.sum(oh2 * (pre2 + run + tot1), axis=-1, keepdims=True)

    e1 = (i1 - EXPERT_LANE0).astype(F32)
    e2 = (i2 - EXPERT_LANE0).astype(F32)
    packed = jnp.where(lane == 0, e1, jnp.where(lane == 1, e2, jnp.where(lane == 2, gate1, jnp.where(
        lane == 3, gate2, jnp.where(lane == 4, rank1, jnp.where(lane == 5, rank2, 0.0))))))
    route_ref[rows, :] = packed[:, 0:ROUTE_COLS]
    return run + tot1 + tot2


def _outproj(hp, convy, o, w_out, g2, wr_hi, wr_lo, br, tri, *, tiles_per_seq, seq_len, tm):
    n, d = hp.shape
    const = lambda i: (0, 0)
    tile = lambda i: (i, 0)
    kern = functools.partial(_outproj_kernel, tiles_per_seq=tiles_per_seq, seq_len=seq_len)
    return pl.pallas_call(
        kern,
        grid=(n // tm,),
        in_specs=[
            pl.BlockSpec((tm, d), tile),
            pl.BlockSpec((tm, convy.shape[1]), tile),
            pl.BlockSpec((tm, o.shape[1]), tile),
            pl.BlockSpec(w_out.shape, const),
            pl.BlockSpec((1, d), const),
            pl.BlockSpec(wr_hi.shape, const),
            pl.BlockSpec(wr_lo.shape, const),
            pl.BlockSpec((1, LANES), const),
            pl.BlockSpec(tri.shape, const),
        ],
        out_specs=[
            pl.BlockSpec((tm, d), tile),
            *[pl.BlockSpec((tm, LANES), tile)] * ROW_PLANES,
            pl.BlockSpec((tm, ROUTE_COLS), tile),
            pl.BlockSpec((8, LANES), const),
        ],
        out_shape=[
            jax.ShapeDtypeStruct((n, d), F32),
            *[jax.ShapeDtypeStruct((n, LANES), jnp.uint32)] * ROW_PLANES,
            jax.ShapeDtypeStruct((n, ROUTE_COLS), F32),
            jax.ShapeDtypeStruct((8, LANES), F32),
        ],
        scratch_shapes=[pltpu.VMEM((8, LANES), F32)],
        compiler_params=_cparams(1),
        name="outproj_router",
    )(hp, convy, o, w_out, g2, wr_hi, wr_lo, br, tri)


def _outproj_t_kernel(hp_ref, cy_ref, o_ref, wout_ref, g2_ref, wrh_ref, wrl_ref, br_ref, upper_ref,
                      h1_ref, xp0_ref, xp1_ref, xp2_ref, xp3_ref, route_ref, cnt_ref, run_ref,
                      *, tiles_per_seq, seq_len):
    i = pl.program_id(0)
    tm = hp_ref.shape[0]

    @pl.when(i == 0)
    def _():
        run_ref[...] = jnp.zeros_like(run_ref)

    mix = jnp.concatenate([cy_ref[...], o_ref[...]], axis=1)
    h1 = hp_ref[...] + jnp.dot(mix, wout_ref[...], preferred_element_type=F32)
    h1_ref[...] = h1
    ms = jnp.mean(h1 * h1, axis=-1, keepdims=True)
    xn = h1 * lax.rsqrt(ms + EPS) * g2_ref[...]
    xw = _pack_rows(xn)
    for c, ref in enumerate((xp0_ref, xp1_ref, xp2_ref, xp3_ref)):
        ref[...] = xw[:, c * LANES:(c + 1) * LANES]

    x_hi = xn.astype(BF16)
    x_lo = (xn - x_hi.astype(F32)).astype(BF16)
    hi_both = jnp.dot(x_hi, jnp.concatenate([wrh_ref[...], wrl_ref[...]], axis=1), preferred_element_type=F32)
    logits = (hi_both[:, :LANES] + hi_both[:, LANES:]
              + jnp.dot(x_lo, wrh_ref[...], preferred_element_type=F32) + br_ref[...])

    lt = logits.T[0:ROUTE_ROWS, :]
    row = lax.broadcasted_iota(jnp.int32, lt.shape, 0)
    big = jnp.int32(4 * LANES)

    def first_argmax(vals, vmax):
        return jnp.min(jnp.where(vals == vmax, row, big), axis=0, keepdims=True)

    gl = jnp.where(row < N_GROUPS, lt, NEG_BIG)
    gmax = jnp.max(gl, axis=0, keepdims=True)
    g_val = 1.0 / jnp.sum(jnp.exp(gl - gmax), axis=0, keepdims=True)
    g_idx = first_argmax(gl, gmax)
    lo = EXPERT_LANE0 + EXPERTS_PER_GROUP * g_idx
    el = jnp.where((row >= lo) & (row < lo + EXPERTS_PER_GROUP), lt, NEG_BIG)
    m1 = jnp.max(el, axis=0, keepdims=True)
    i1 = first_argmax(el, m1)
    el2 = jnp.where(row == i1, NEG_BIG, el)
    m2 = jnp.max(el2, axis=0, keepdims=True)
    i2 = first_argmax(el2, m2)
    r = jnp.exp(m2 - m1)
    gate1 = g_val / (1.0 + r)
    gate2 = g_val * r / (1.0 + r)

    pos = (i % tiles_per_seq) * tm + lax.broadcasted_iota(jnp.int32, (1, tm), 1)
    valid = pos < seq_len
    oh1 = jnp.where(valid & (row == i1), 1.0, 0.0)
    oh2 = jnp.where(valid & (row == i2), 1.0, 0.0)
    pre = jnp.dot(jnp.concatenate([oh1, oh2], axis=0).astype(BF16), upper_ref[...], preferred_element_type=F32)
    tot1 = jnp.sum(oh1, axis=1, keepdims=True)
    tot2 = jnp.sum(oh2, axis=1, keepdims=True)
    run = run_ref[...]
    run_t = jnp.tile(run, (1, tm // LANES))
    rank1 = jnp.sum(oh1 * (pre[:ROUTE_ROWS] + run_t), axis=0, keepdims=True)
    rank2 = jnp.sum(oh2 * (pre[ROUTE_ROWS:] + run_t + tot1), axis=0, keepdims=True)
    new_run = run + tot1 + tot2
    run_ref[...] = new_run
    cnt_ref[...] = new_run

    e1 = (i1 - EXPERT_LANE0).astype(F32)
    e2 = (i2 - EXPERT_LANE0).astype(F32)
    r8 = lax.broadcasted_iota(jnp.int32, (ROUTE_COLS, tm), 0)
    route_ref[...] = jnp.where(r8 == 0, e1, jnp.where(r8 == 1, e2, jnp.where(r8 == 2, gate1, jnp.where(
        r8 == 3, gate2, jnp.where(r8 == 4, rank1, jnp.where(r8 == 5, rank2, 0.0))))))


def _outproj_t(hp, convy, o, w_out, g2, wr_hi, wr_lo, br, upper, *, tiles_per_seq, seq_len, tm):
    n, d = hp.shape
    const = lambda i: (0, 0)
    tile = lambda i: (i, 0)
    kern = functools.partial(_outproj_t_kernel, tiles_per_seq=tiles_per_seq, seq_len=seq_len)
    return pl.pallas_call(
        kern,
        grid=(n // tm,),
        in_specs=[
            pl.BlockSpec((tm, d), tile),
            pl.BlockSpec((tm, convy.shape[1]), tile),
            pl.BlockSpec((tm, o.shape[1]), tile),
            pl.BlockSpec(w_out.shape, const),
            pl.BlockSpec((1, d), const),
            pl.BlockSpec(wr_hi.shape, const),
            pl.BlockSpec(wr_lo.shape, const),
            pl.BlockSpec((1, LANES), const),
            pl.BlockSpec(upper.shape, const),
        ],
        out_specs=[
            pl.BlockSpec((tm, d), tile),
            *[pl.BlockSpec((tm, LANES), tile)] * ROW_PLANES,
            pl.BlockSpec((ROUTE_COLS, tm), lambda i: (0, i)),
            pl.BlockSpec((ROUTE_ROWS, LANES), const),
        ],
        out_shape=[
            jax.ShapeDtypeStruct((n, d), F32),
            *[jax.ShapeDtypeStruct((n, LANES), jnp.uint32)] * ROW_PLANES,
            jax.ShapeDtypeStruct((ROUTE_COLS, n), F32),
            jax.ShapeDtypeStruct((ROUTE_ROWS, LANES), F32),
        ],
        scratch_shapes=[pltpu.VMEM((ROUTE_ROWS, LANES), F32)],
        compiler_params=_cparams(1),
        name="outproj_router",
    )(hp, convy, o, w_out, g2, wr_hi, wr_lo, br, upper)


def _slots_kernel(pstart_ref, route_ref, dest_ref, *, seq_len, p_rows):
    b = pl.program_id(0)
    route = route_ref[...]
    eid = route.astype(jnp.int32)
    start = jnp.zeros_like(eid)
    for e in range(N_EXPERTS):
        start = jnp.where(eid == e, pstart_ref[e], start)
    rank = pltpu.roll(route, ROUTE_COLS - 2 * TOP_K, axis=0).astype(jnp.int32)
    k = lax.broadcasted_iota(jnp.int32, route.shape, 0)
    pos = lax.broadcasted_iota(jnp.int32, route.shape, 1)
    n_pad = route.shape[1] - seq_len
    spare = p_rows + (b * n_pad + (pos - seq_len)) * TOP_K + k
    dest_ref[...] = jnp.where(pos < seq_len, start + rank, spare)


def _slots(pstarts, route, *, batch, lp_len, seq_len, p_rows):
    kern = functools.partial(_slots_kernel, seq_len=seq_len, p_rows=p_rows)
    grid_spec = pltpu.PrefetchScalarGridSpec(
        num_scalar_prefetch=1,
        grid=(batch,),
        in_specs=[pl.BlockSpec((ROUTE_COLS, lp_len), lambda b, ps: (0, b))],
        out_specs=pl.BlockSpec((ROUTE_COLS, lp_len), lambda b, ps: (0, b)),
    )
    return pl.pallas_call(
        kern,
        grid_spec=grid_spec,
        out_shape=jax.ShapeDtypeStruct(route.shape, jnp.int32),
        compiler_params=_cparams(1),
        name="moe_slots",
    )(pstarts, route)


def _row_copy(src_hbm, src_row, dst_ref, dst_row, sem):
    return pltpu.make_async_copy(src_hbm.at[pl.ds(src_row, 1), :], dst_ref.at[pl.ds(dst_row, 1), :], sem)


def _dispatch_kernel(dest_ref, xn_hbm, xs_hbm, xbuf, sem_in, sem, *, tiles_per_seq, lp_len, tile):
    t = pl.program_id(0)
    base = (t // tiles_per_seq) * lp_len + (t % tiles_per_seq) * tile
    tile_copy = pltpu.make_async_copy(xn_hbm.at[pl.ds(base, tile), :], xbuf, sem_in)
    tile_copy.start()
    tile_copy.wait()

    def issue(r, carry):
        for kk in range(TOP_K):
            _row_copy(xbuf, r, xs_hbm, dest_ref[0, 0, TOP_K * r + kk], sem).start()
        return carry

    lax.fori_loop(0, tile, issue, 0)

    def drain(r, carry):
        for kk in range(TOP_K):
            _row_copy(xbuf, r, xs_hbm, dest_ref[0, 0, TOP_K * r + kk], sem).wait()
        return carry

    lax.fori_loop(0, tile, drain, 0)


def _dispatch(dest_tiles, xn, *, p_rows, tiles_per_seq, lp_len, tile):
    n_tiles = dest_tiles.shape[0]
    kern = functools.partial(_dispatch_kernel, tiles_per_seq=tiles_per_seq, lp_len=lp_len, tile=tile)
    return pl.pallas_call(
        kern,
        grid=(n_tiles,),
        in_specs=[
            pl.BlockSpec((1, 1, TOP_K * tile), lambda t: (t, 0, 0), memory_space=pltpu.SMEM),
            pl.BlockSpec(memory_space=pl.ANY),
        ],
        out_specs=pl.BlockSpec(memory_space=pl.ANY),
        out_shape=jax.ShapeDtypeStruct((p_rows, xn.shape[1]), xn.dtype),
        scratch_shapes=[pltpu.VMEM((tile, xn.shape[1]), xn.dtype), pltpu.SemaphoreType.DMA(()),
                        pltpu.SemaphoreType.DMA(())],
        compiler_params=_cparams(1),
        name="moe_dispatch",
    )(dest_tiles, xn)


def _experts_kernel(be_ref, nv_ref, xs_ref, wg_ref, wu_ref, wd_ref, y_ref, wgb_ref, wub_ref, wdb_ref):
    i = pl.program_id(0)
    e = be_ref[i]
    prev = be_ref[jnp.maximum(i - 1, 0)]

    @pl.when((i == 0) | (e != prev))
    def _():
        wgb_ref[...] = wg_ref[0].astype(BF16)
        wub_ref[...] = wu_ref[0].astype(BF16)
        wdb_ref[...] = wd_ref[0].astype(BF16)

    nv = nv_ref[i]

    @pl.when(nv > 0)
    def _():
        xs = xs_ref[...]
        row = lax.broadcasted_iota(jnp.int32, xs.shape, 0)
        x = jnp.where(row < nv, xs, 0.0).astype(BF16)
        hg = jnp.dot(x, wgb_ref[...], preferred_element_type=F32)
        hu = jnp.dot(x, wub_ref[...], preferred_element_type=F32)
        hid = (hg / (1.0 + jnp.exp(-hg)) * hu).astype(BF16)
        y_ref[...] = jnp.dot(hid, wdb_ref[...], preferred_element_type=F32)

    @pl.when(nv == 0)
    def _():
        y_ref[...] = jnp.zeros_like(y_ref)


def _experts(block_e, nvalid, xs, w_gate, w_up, w_down):
    p_rows, d = xs.shape
    ff = w_gate.shape[2]
    n_blocks = p_rows // MOE_BLOCK
    grid_spec = pltpu.PrefetchScalarGridSpec(
        num_scalar_prefetch=2,
        grid=(n_blocks,),
        in_specs=[
            pl.BlockSpec((MOE_BLOCK, d), lambda i, be, nv: (i, 0)),
            pl.BlockSpec((1, d, ff), lambda i, be, nv: (be[i], 0, 0)),
            pl.BlockSpec((1, d, ff), lambda i, be, nv: (be[i], 0, 0)),
            pl.BlockSpec((1, ff, d), lambda i, be, nv: (be[i], 0, 0)),
        ],
        out_specs=pl.BlockSpec((MOE_BLOCK, d), lambda i, be, nv: (i, 0)),
        scratch_shapes=[
            pltpu.VMEM((d, ff), BF16),
            pltpu.VMEM((d, ff), BF16),
            pltpu.VMEM((ff, d), BF16),
        ],
    )
    return pl.pallas_call(
        _experts_kernel,
        grid_spec=grid_spec,
        out_shape=jax.ShapeDtypeStruct((p_rows, d), F32),
        compiler_params=_cparams(1),
        name="moe_experts",
    )(block_e, nvalid, xs, w_gate, w_up, w_down)


def _combine_kernel(dest_ref, gates_ref, h1_hbm, y_hbm, out_ref, hbuf, ya, yb, sem_h, sem_a, sem_b,
                    *, lp_len, tile):
    b = pl.program_id(0)
    i = pl.program_id(1)
    start = b * lp_len + N_META + i * tile
    h_copy = pltpu.make_async_copy(h1_hbm.at[pl.ds(start, tile), :], hbuf, sem_h)
    h_copy.start()

    def issue(r, carry):
        _row_copy(y_hbm, dest_ref[0, 0, TOP_K * r], ya, r, sem_a).start()
        _row_copy(y_hbm, dest_ref[0, 0, TOP_K * r + 1], yb, r, sem_b).start()
        return carry

    lax.fori_loop(0, tile, issue, 0)

    def drain(r, carry):
        _row_copy(y_hbm, dest_ref[0, 0, TOP_K * r], ya, r, sem_a).wait()
        _row_copy(y_hbm, dest_ref[0, 0, TOP_K * r + 1], yb, r, sem_b).wait()
        return carry

    lax.fori_loop(0, tile, drain, 0)
    h_copy.wait()
    g = gates_ref[0]
    out_ref[0] = hbuf[...] + g[:, 0:1] * ya[...] + g[:, 1:2] * yb[...]


def _combine(dest_tiles, gates, h1, y, *, batch, seq, lp_len, tile):
    d = h1.shape[1]
    nt = seq // tile
    kern = functools.partial(_combine_kernel, lp_len=lp_len, tile=tile)
    return pl.pallas_call(
        kern,
        grid=(batch, nt),
        in_specs=[
            pl.BlockSpec((1, 1, TOP_K * tile), lambda b, i: (b * nt + i, 0, 0), memory_space=pltpu.SMEM),
            pl.BlockSpec((1, tile, TOP_K), lambda b, i: (b, i, 0)),
            pl.BlockSpec(memory_space=pl.ANY),
            pl.BlockSpec(memory_space=pl.ANY),
        ],
        out_specs=pl.BlockSpec((1, tile, d), lambda b, i: (b, i, 0)),
        out_shape=jax.ShapeDtypeStruct((batch, seq, d), F32),
        scratch_shapes=[
            pltpu.VMEM((tile, d), F32),
            pltpu.VMEM((tile, d), F32),
            pltpu.VMEM((tile, d), F32),
            pltpu.SemaphoreType.DMA(()),
            pltpu.SemaphoreType.DMA(()),
            pltpu.SemaphoreType.DMA(()),
        ],
        compiler_params=_cparams(2),
        name="moe_combine",
    )(dest_tiles, gates, h1, y)


def _sc_workers():
    info = plsc.get_sparse_core_info()
    return info.num_cores, info.num_cores * info.num_subcores


def _sc_scatter_rows(planes, idx_a, idx_b, out_rows):
    n_win = idx_a.shape[0]
    n_cores, n_workers = _sc_workers()
    trips = -(-n_win // n_workers)
    mesh = plsc.VectorSubcoreMesh(core_axis_name="c", subcore_axis_name="s")

    def body(*refs):
        xs = refs[0:ROW_PLANES]
        ia_hbm, ib_hbm = refs[ROW_PLANES:ROW_PLANES + 2]
        outs = refs[ROW_PLANES + 2:2 * ROW_PLANES + 2]
        ia_v, ib_v, buf, sem = refs[2 * ROW_PLANES + 2:]
        wid = lax.axis_index("s") * n_cores + lax.axis_index("c")

        def step(t, carry):
            g = wid + t * n_workers

            @pl.when(g < n_win)
            def _():
                pltpu.sync_copy(ia_hbm.at[g], ia_v)
                pltpu.sync_copy(ib_hbm.at[g], ib_v)
                row0 = pl.multiple_of(g * SC_WINDOW, SC_WINDOW)
                loads = [pltpu.async_copy(xs[c].at[pl.ds(row0, SC_WINDOW)], buf.at[c], sem)
                         for c in range(ROW_PLANES)]
                for cp in loads:
                    cp.wait()
                stores = [pltpu.async_copy(buf.at[c], outs[c].at[iv], sem)
                          for c in range(ROW_PLANES) for iv in (ia_v, ib_v)]
                for cp in stores:
                    cp.wait()

            return carry

        lax.fori_loop(0, trips, step, 0)

    kern = pl.kernel(
        body,
        out_type=[jax.ShapeDtypeStruct((out_rows, LANES), jnp.uint32)] * ROW_PLANES,
        mesh=mesh,
        scratch_types=[
            pltpu.VMEM((SC_WINDOW,), jnp.int32),
            pltpu.VMEM((SC_WINDOW,), jnp.int32),
            pltpu.VMEM((ROW_PLANES, SC_WINDOW, LANES), jnp.uint32),
            pltpu.SemaphoreType.DMA,
        ],
        name="moe_dispatch_sc",
    )
    return kern(*planes, idx_a, idx_b)


def _sc_gather_rows(planes, idx_a, idx_b):
    n_win = idx_a.shape[0]
    n_cores, n_workers = _sc_workers()
    trips = -(-n_win // n_workers)
    mesh = plsc.VectorSubcoreMesh(core_axis_name="c", subcore_axis_name="s")

    def body(*refs):
        ys = refs[0:ROW_PLANES]
        ia_hbm, ib_hbm = refs[ROW_PLANES:ROW_PLANES + 2]
        outs_a = refs[ROW_PLANES + 2:2 * ROW_PLANES + 2]
        outs_b = refs[2 * ROW_PLANES + 2:3 * ROW_PLANES + 2]
        iv, buf, sem = refs[3 * ROW_PLANES + 2:]
        wid = lax.axis_index("s") * n_cores + lax.axis_index("c")

        def step(t, carry):
            g = wid + t * n_workers

            @pl.when(g < n_win)
            def _():
                row0 = pl.multiple_of(g * SC_WINDOW, SC_WINDOW)
                for i_hbm, outs in ((ia_hbm, outs_a), (ib_hbm, outs_b)):
                    pltpu.sync_copy(i_hbm.at[g], iv)
                    loads = [pltpu.async_copy(ys[c].at[iv], buf.at[c], sem) for c in range(ROW_PLANES)]
                    for cp in loads:
                        cp.wait()
                    stores = [pltpu.async_copy(buf.at[c], outs[c].at[pl.ds(row0, SC_WINDOW)], sem)
                              for c in range(ROW_PLANES)]
                    for cp in stores:
                        cp.wait()

            return carry

        lax.fori_loop(0, trips, step, 0)

    n_rows = n_win * SC_WINDOW
    kern = pl.kernel(
        body,
        out_type=[jax.ShapeDtypeStruct((n_rows, LANES), jnp.uint32)] * (2 * ROW_PLANES),
        mesh=mesh,
        scratch_types=[
            pltpu.VMEM((SC_WINDOW,), jnp.int32),
            pltpu.VMEM((ROW_PLANES, SC_WINDOW, LANES), jnp.uint32),
            pltpu.SemaphoreType.DMA,
        ],
        name="moe_gather_sc",
    )
    res = kern(*planes, idx_a, idx_b)
    return res[:ROW_PLANES], res[ROW_PLANES:]


def _experts_kernel(be_ref, nv_ref, first_ref, slot_ref, nxt_ref, x0_ref, x1_ref, x2_ref, x3_ref,
                    wg_hbm, wu_hbm, wd_hbm, y0_ref, y1_ref, y2_ref, y3_ref,
                    wgf_ref, wuf_ref, wdf_ref, wgb_ref, wub_ref, wdb_ref, sem):
    i = pl.program_id(0)
    e = be_ref[i]
    slot = slot_ref[i]
    y_refs = (y0_ref, y1_ref, y2_ref, y3_ref)

    def weight_copies(expert, s):
        return [pltpu.make_async_copy(hbm.at[expert], stage.at[s], sem.at[s, j])
                for j, (hbm, stage) in enumerate(((wg_hbm, wgf_ref), (wu_hbm, wuf_ref), (wd_hbm, wdf_ref)))]

    @pl.when(i == 0)
    def _():
        for cp in weight_copies(e, slot):
            cp.start()

    @pl.when(first_ref[i] == 1)
    def _():
        for cp in weight_copies(e, slot):
            cp.wait()
        nxt = nxt_ref[i]

        @pl.when(nxt >= 0)
        def _():
            for cp in weight_copies(nxt, 1 - slot):
                cp.start()

        wgb_ref[...] = wgf_ref[slot].astype(BF16)
        wub_ref[...] = wuf_ref[slot].astype(BF16)
        wdb_ref[...] = wdf_ref[slot].astype(BF16)

    nv = nv_ref[i]
    half = MOE_BLOCK // 2

    def mlp(rows):
        xs = _unpack_rows([r[rows, :] for r in (x0_ref, x1_ref, x2_ref, x3_ref)])
        row = lax.broadcasted_iota(jnp.int32, xs.shape, 0)
        x = jnp.where(row < nv, xs, jnp.zeros_like(xs))
        hg = jnp.dot(x, wgb_ref[...], preferred_element_type=F32)
        hu = jnp.dot(x, wub_ref[...], preferred_element_type=F32)
        hid = (hg / (1.0 + jnp.exp(-hg)) * hu).astype(BF16)
        yw = _pack_rows(jnp.dot(hid, wdb_ref[...], preferred_element_type=F32))
        for c, ref in enumerate(y_refs):
            ref[rows, :] = yw[:, c * LANES:(c + 1) * LANES]

    def clear(rows):
        for ref in y_refs:
            ref[rows, :] = jnp.zeros((rows.size, LANES), ref.dtype)

    @pl.when(nv > half)
    def _():
        mlp(pl.ds(0, MOE_BLOCK))

    @pl.when((nv > 0) & (nv <= half))
    def _():
        mlp(pl.ds(0, half))
        clear(pl.ds(half, half))

    @pl.when(nv == 0)
    def _():
        clear(pl.ds(0, MOE_BLOCK))


def _experts(block_e, nvalid, xs_planes, w_gate, w_up, w_down):
    n_blocks = block_e.shape[0]
    _, d, ff = w_gate.shape
    first = jnp.concatenate([jnp.ones((1,), jnp.int32), (block_e[1:] != block_e[:-1]).astype(jnp.int32)])
    slot = (jnp.cumsum(first) - 1) % 2
    later = jnp.where(block_e[None, :] > block_e[:, None], block_e[None, :], N_EXPERTS)
    nxt = jnp.min(later, axis=1)
    nxt = jnp.where(nxt == N_EXPERTS, -1, nxt).astype(jnp.int32)
    blk = lambda i, *_: (i, 0)
    grid_spec = pltpu.PrefetchScalarGridSpec(
        num_scalar_prefetch=5,
        grid=(n_blocks,),
        in_specs=[
            *[pl.BlockSpec((MOE_BLOCK, LANES), blk)] * ROW_PLANES,
            pl.BlockSpec(memory_space=pl.ANY),
            pl.BlockSpec(memory_space=pl.ANY),
            pl.BlockSpec(memory_space=pl.ANY),
        ],
        out_specs=[pl.BlockSpec((MOE_BLOCK, LANES), blk)] * ROW_PLANES,
        scratch_shapes=[
            pltpu.VMEM((2, d, ff), F32),
            pltpu.VMEM((2, d, ff), F32),
            pltpu.VMEM((2, ff, d), F32),
            pltpu.VMEM((d, ff), BF16),
            pltpu.VMEM((d, ff), BF16),
            pltpu.VMEM((ff, d), BF16),
            pltpu.SemaphoreType.DMA((2, 3)),
        ],
    )
    return pl.pallas_call(
        _experts_kernel,
        grid_spec=grid_spec,
        out_shape=[jax.ShapeDtypeStruct((n_blocks * MOE_BLOCK, LANES), jnp.uint32)] * ROW_PLANES,
        compiler_params=_cparams(1),
        name="moe_experts",
    )(block_e, nvalid, first, slot.astype(jnp.int32), nxt, *xs_planes, w_gate, w_up, w_down)


def _combine_kernel(gates_ref, h1_hbm, *refs, b_index, lp_len, tile, aliased):
    a_refs = refs[0:ROW_PLANES]
    b_refs = refs[ROW_PLANES:2 * ROW_PLANES]
    out_ref, hbuf, sem_h = refs[2 * ROW_PLANES + (1 if aliased else 0):]
    step = pl.program_id(0)
    last = pl.num_programs(0) - 1

    def h_copy(s, slot):
        start = b_index * lp_len + N_META + s * tile
        return pltpu.make_async_copy(h1_hbm.at[pl.ds(start, tile), :], hbuf.at[slot], sem_h.at[slot])

    slot = step % 2

    @pl.when(step == 0)
    def _():
        h_copy(step, slot).start()

    @pl.when(step < last)
    def _():
        h_copy(step + 1, 1 - slot).start()

    ya = _unpack_rows([r[...] for r in a_refs]).astype(F32)
    yb = _unpack_rows([r[...] for r in b_refs]).astype(F32)
    g = jnp.concatenate([gates_ref[...]] * (LANES // ROUTE_COLS), axis=0).T
    moe = g[:, 0:1] * ya + g[:, 1:2] * yb
    h_copy(step, slot).wait()
    out_ref[0] = hbuf[slot] + moe


def _combine(gates, h1, a_planes, b_planes, out_prev, *, b_index, batch, seq, lp_len, tile):
    d = h1.shape[1]
    nt = seq // tile
    aliased = out_prev is not None
    kern = functools.partial(_combine_kernel, b_index=b_index, lp_len=lp_len, tile=tile, aliased=aliased)
    rows = lambda i: (i, 0)
    operands = [gates, h1, *a_planes, *b_planes] + ([out_prev] if aliased else [])
    return pl.pallas_call(
        kern,
        grid=(nt,),
        in_specs=[
            pl.BlockSpec((ROUTE_COLS, tile), lambda i: (0, b_index * nt + i)),
            pl.BlockSpec(memory_space=pl.ANY),
            *[pl.BlockSpec((tile, LANES), rows)] * (2 * ROW_PLANES),
        ] + ([pl.BlockSpec(memory_space=pl.ANY)] if aliased else []),
        out_specs=pl.BlockSpec((1, tile, d), lambda i: (b_index, i, 0)),
        out_shape=jax.ShapeDtypeStruct((batch, seq, d), F32),
        input_output_aliases={len(operands) - 1: 0} if aliased else {},
        scratch_shapes=[pltpu.VMEM((2, tile, d), F32), pltpu.SemaphoreType.DMA((2,))],
        compiler_params=_cparams(1),
        name=f"moe_combine_{b_index}",
    )(*operands)


def _rope_tables(length, lp_len):
    half = ROPE_DIM // 2
    pos = jnp.arange(length, dtype=F32)
    inv_freq = ROPE_THETA ** (-jnp.arange(0, ROPE_DIM, 2, dtype=F32) / ROPE_DIM)
    ang = pos[:, None] * inv_freq[None, :]
    cos = jnp.cos(ang)
    sin = jnp.sin(ang)
    ones = jnp.ones((length, HEAD_DIM - ROPE_DIM), F32)
    zeros_h = jnp.zeros((length, half), F32)
    zeros_r = jnp.zeros((length, HEAD_DIM - ROPE_DIM), F32)
    c = jnp.concatenate([cos, cos, ones], axis=1)
    s1 = jnp.concatenate([zeros_h, sin, zeros_r], axis=1)
    s2 = jnp.concatenate([-sin, zeros_h, zeros_r], axis=1)
    pad = ((0, lp_len - length), (0, 0))
    rep = LANES // HEAD_DIM
    return tuple(jnp.pad(jnp.tile(t, (1, rep)), pad) for t in (c, s1, s2))


def _layer(x, meta, l, batch, length, lp_len, tm, norm1_g, w_in, conv_w, q_norm_g, k_norm_g, lambda_q1, lambda_k1,
           lambda_q2, lambda_k2, subln_g, w_out, norm2_g, w_router_group, b_router_group, w_router_expert,
           b_router_expert, w_gate, w_up, w_down, rope, last):
    d = x.shape[2]
    tiles_per_seq = lp_len // tm
    cw = conv_w.shape[2]
    qw = N_HEADS * 2 * HEAD_DIM
    lam_init = 0.8 - 0.6 * math.exp(-0.3 * l)

    reps = qw // HEAD_DIM
    gq = jnp.tile(q_norm_g[l] * (HEAD_DIM ** -0.5 * LOG2E), reps)[None, :]
    gk = jnp.tile(k_norm_g[l], reps)[None, :]
    seg = jnp.arange(qw) // HEAD_DIM
    bd = (seg[:, None] == seg[None, :]).astype(BF16)
    hp, convy, q, k, v = _inproj(x, meta, norm1_g[l][None, :], w_in[l].astype(BF16), conv_w[l], gq, gk, bd,
                                 *rope, tiles_per_seq=tiles_per_seq, tm=tm)

    lamp = jnp.stack([lambda_q1[l], lambda_k1[l], lambda_q2[l], lambda_k2[l]]).astype(F32)
    o = _attention(q, k, v, lamp, subln_g[l][None, :], batch=batch, lp_len=lp_len, tq=tm, lam_init=lam_init)

    lane_pad = LANES - N_GROUPS - N_EXPERTS
    wr = jnp.pad(jnp.concatenate([w_router_group[l], w_router_expert[l]], axis=1), ((0, 0), (0, lane_pad)))
    wr_hi = wr.astype(BF16)
    wr_lo = (wr - wr_hi.astype(F32)).astype(BF16)
    br = jnp.pad(jnp.concatenate([b_router_group[l], b_router_expert[l]]), (0, lane_pad))[None, :]
    ridx = jnp.arange(tm)
    upper = (ridx[:, None] < ridx[None, :]).astype(BF16)
    h1, *rest = _outproj_t(hp, convy, o, w_out[l].astype(BF16), norm2_g[l][None, :], wr_hi, wr_lo, br,
                           upper, tiles_per_seq=tiles_per_seq, seq_len=length, tm=tm)
    x_planes = rest[:ROW_PLANES]
    route, cnt = rest[ROW_PLANES:]

    counts = cnt[EXPERT_LANE0:EXPERT_LANE0 + N_EXPERTS, 0].astype(jnp.int32)
    a = batch * length * TOP_K
    n_blocks = -(-a // MOE_BLOCK) + N_EXPERTS
    p_rows = n_blocks * MOE_BLOCK
    padded = (counts + MOE_BLOCK - 1) // MOE_BLOCK * MOE_BLOCK
    pends = jnp.cumsum(padded)
    pstarts = pends - padded

    def lookup(table, idx):
        hit = idx[:, None] == jnp.arange(N_EXPERTS, dtype=jnp.int32)[None, :]
        return jnp.sum(jnp.where(hit, table[None, :], 0), axis=1)

    n_pad = lp_len - length
    spare_rows = -(-(batch * n_pad * TOP_K) // MOE_BLOCK) * MOE_BLOCK
    dest = _slots(pstarts.astype(jnp.int32), route, batch=batch, lp_len=lp_len, seq_len=length, p_rows=p_rows)
    dest = dest.reshape(ROUTE_COLS, batch, lp_len)
    gates = route.reshape(ROUTE_COLS, batch, lp_len)[TOP_K:2 * TOP_K]
    blk0 = jnp.arange(n_blocks, dtype=jnp.int32) * MOE_BLOCK
    block_e = jnp.minimum(jnp.sum((pends[None, :] <= blk0[:, None]).astype(jnp.int32), axis=1), N_EXPERTS - 1)
    nvalid = jnp.clip(lookup(counts, block_e) - (blk0 - lookup(pstarts, block_e)), 0, MOE_BLOCK)

    assert (batch * lp_len) % SC_WINDOW == 0
    xs_planes = _sc_scatter_rows(x_planes, dest[0].reshape(-1, SC_WINDOW), dest[1].reshape(-1, SC_WINDOW),
                                 p_rows + spare_rows)
    y_planes = _experts(block_e, nvalid, xs_planes, w_gate[l], w_up[l], w_down[l])

    if not last:
        raise NotImplementedError("only the final layer's combine (which drops the meta tokens) is implemented")
    seq = length - N_META
    assert seq % SC_WINDOW == 0
    dest_x = dest[0:TOP_K, :, N_META:length]
    ctile = _largest_tile(seq, 512, LANES)
    gates_x = jnp.pad(gates[:, :, N_META:length].reshape(TOP_K, batch * seq), ((0, ROUTE_COLS - TOP_K), (0, 0)))
    out = None
    for b in range(batch):
        a_planes, b_planes = _sc_gather_rows(y_planes, dest_x[0, b].reshape(-1, SC_WINDOW),
                                             dest_x[1, b].reshape(-1, SC_WINDOW))
        out = _combine(gates_x, h1, a_planes, b_planes, out, b_index=b, batch=batch, seq=seq, lp_len=lp_len,
                       tile=ctile)
    return out


def kernel(x, meta_tokens, norm1_g, w_in, conv_w, q_norm_g, k_norm_g, lambda_q1, lambda_k1, lambda_q2, lambda_k2,
           subln_g, w_out, norm2_g, w_router_group, b_router_group, w_router_expert, b_router_expert, w_gate,
           w_up, w_down):
    b, s, d = x.shape
    depth = w_in.shape[0]
    assert depth == 1, "a single layer is supported"
    length = s + N_META
    tm = TOKEN_TILE
    lp_len = -(-length // tm) * tm
    assert lp_len // tm >= 2 and (length - (lp_len // tm - 1) * tm) % 8 == 0
    rope = _rope_tables(length, lp_len)
    return _layer(x, meta_tokens.astype(x.dtype), 0, b, length, lp_len, tm, norm1_g, w_in, conv_w, q_norm_g, k_norm_g, lambda_q1, lambda_k1,
                  lambda_q2, lambda_k2, subln_g, w_out, norm2_g, w_router_group, b_router_group,
                  w_router_expert, b_router_expert, w_gate, w_up, w_down, rope, last=True)
```

```python
import functools
import math

import jax
import jax.numpy as jnp
from jax import lax
from jax.experimental import pallas as pl
from jax.experimental.pallas import tpu as pltpu
from jax.experimental.pallas import tpu_sc as plsc

F32 = jnp.float32
BF16 = jnp.bfloat16

N_META = 16
N_HEADS = 4
HEAD_DIM = 64
ROPE_DIM = HEAD_DIM // 4
ROPE_THETA = 500000.0
N_GROUPS = 4
EXPERTS_PER_GROUP = 8
N_EXPERTS = N_GROUPS * EXPERTS_PER_GROUP
TOP_K = 2
EPS = 1e-6
LOG2E = 1.4426950408889634

LANES = 128
TOKEN_TILE = 640
INPROJ_CHAINS = 2
ATTN_HEADS_PER_STEP = 4
MOE_BLOCK = 512
ROUTE_COLS = 8
ROUTE_ROWS = 64
ROW_PLANES = 4
SC_WINDOW = 128
EXPERT_LANE0 = N_GROUPS
NEG_BIG = -1e30
VMEM_LIMIT = 56 * 1024 * 1024


def _largest_tile(n, cap, mult):
    for t in range(min(cap, n), 0, -1):
        if n % t == 0 and t % mult == 0:
            return t
    raise ValueError(f"no tile for {n}")


def _cparams(n_axes):
    return pltpu.CompilerParams(dimension_semantics=("arbitrary",) * n_axes, vmem_limit_bytes=VMEM_LIMIT)


def _pack_rows(x):
    w = x.shape[1] // 2
    lo = lax.bitcast_convert_type(x[:, :w].astype(BF16).astype(F32), jnp.uint32)
    hi = lax.bitcast_convert_type(x[:, w:].astype(BF16).astype(F32), jnp.uint32)
    return lax.shift_right_logical(lo, jnp.uint32(16)) | (hi & jnp.uint32(0xFFFF0000))


def _unpack_rows(planes):
    w = jnp.concatenate(planes, axis=1)
    lo = lax.bitcast_convert_type(lax.shift_left(w, jnp.uint32(16)), F32)
    hi = lax.bitcast_convert_type(w & jnp.uint32(0xFFFF0000), F32)
    return jnp.concatenate([lo, hi], axis=1).astype(BF16)


def _inproj_kernel(x_hbm, meta_hbm, g1_ref, win_ref, convw_ref, gq_ref, gk_ref, bd_ref, rc_ref, rs1_ref, rs2_ref,
                   hp_ref, convy_ref, q_ref, k_ref, v_ref, carry_ref, xbuf, sem, *, tiles_per_seq, seq, cw, qw):
    i = pl.program_id(0)
    tm = hp_ref.shape[0]
    sub = tm // INPROJ_CHAINS
    q0 = 3 * cw
    w = convw_ref[...]
    last_rows = seq + N_META - (tiles_per_seq - 1) * tm

    def fetch(step, slot, start):
        b = step // tiles_per_seq
        t = step % tiles_per_seq

        def go(src, dst):
            cp = pltpu.make_async_copy(src, dst, sem.at[slot])
            if start:
                cp.start()
            else:
                cp.wait()

        @pl.when(t == 0)
        def _():
            go(meta_hbm, xbuf.at[slot, pl.ds(0, N_META)])
            go(x_hbm.at[pl.ds(b * seq, tm - N_META)], xbuf.at[slot, pl.ds(N_META, tm - N_META)])

        @pl.when((t > 0) & (t < tiles_per_seq - 1))
        def _():
            go(x_hbm.at[pl.ds(b * seq + t * tm - N_META, tm)], xbuf.at[slot])

        @pl.when(t == tiles_per_seq - 1)
        def _():
            go(x_hbm.at[pl.ds(b * seq + t * tm - N_META, last_rows)], xbuf.at[slot, pl.ds(0, last_rows)])

    slot = i % 2

    @pl.when(i == 0)
    def _():
        fetch(i, slot, True)

    @pl.when(i + 1 < pl.num_programs(0))
    def _():
        fetch(i + 1, 1 - slot, True)

    fetch(i, slot, False)

    @pl.when(i % tiles_per_seq == tiles_per_seq - 1)
    def _():
        xbuf[slot, pl.ds(last_rows, tm - last_rows), :] = jnp.zeros((tm - last_rows, xbuf.shape[2]), xbuf.dtype)

    hp_ref[...] = xbuf[slot]

    @pl.when(i % tiles_per_seq == 0)
    def _():
        carry_ref[...] = jnp.zeros_like(carry_ref)

    prev = carry_ref[...]

    for chain in range(INPROJ_CHAINS):
        rows = pl.ds(chain * sub, sub)
        x = xbuf[slot, rows, :]
        ms = jnp.mean(x * x, axis=-1, keepdims=True)
        xn = (x * lax.rsqrt(ms + EPS) * g1_ref[...]).astype(BF16)

        def proj(lo, hi):
            return jnp.dot(xn, win_ref[:, lo:hi], preferred_element_type=F32)

        u_conv = proj(0, q0)
        u_q = proj(q0, q0 + qw)

        z = u_conv[:, cw:2 * cw] * u_conv[:, 2 * cw:3 * cw]
        p1 = prev[7:8]
        p2 = prev[6:7]
        row = lax.broadcasted_iota(jnp.int32, z.shape, 0)
        z1 = jnp.where(row == 0, p1, pltpu.roll(z, 1, axis=0))
        z2 = jnp.where(row == 0, p2, jnp.where(row == 1, p1, pltpu.roll(z, 2, axis=0)))
        prev = z[sub - 8:sub]
        conv = w[0:1] * z2 + w[1:2] * z1 + w[2:3] * z
        convy_ref[rows, :] = (u_conv[:, 0:cw] * conv).astype(BF16)

        rc = rc_ref[rows, :]
        rs1 = rs1_ref[rows, :]
        rs2 = rs2_ref[rows, :]

        def norm_rope(t, g_ref):
            ss = jnp.dot((t * t).astype(BF16), bd_ref[...], preferred_element_type=F32)
            tn = t * lax.rsqrt(ss * (1.0 / HEAD_DIM) + EPS) * g_ref[...]
            outs = []
            for c in range(qw // LANES):
                ch = tn[:, c * LANES:(c + 1) * LANES]
                outs.append(ch * rc + pltpu.roll(ch, ROPE_DIM // 2, axis=1) * rs1
                            + pltpu.roll(ch, LANES - ROPE_DIM // 2, axis=1) * rs2)
            return jnp.concatenate(outs, axis=1).astype(BF16)

        u_k = proj(q0 + qw, q0 + 2 * qw)
        q_ref[rows, :] = norm_rope(u_q, gq_ref)
        u_v = proj(q0 + 2 * qw, win_ref.shape[1])
        k_ref[rows, :] = norm_rope(u_k, gk_ref)
        v_ref[rows, :] = u_v.astype(BF16)

    carry_ref[...] = prev


def _inproj(x, meta, g1, w_in, conv_w, gq, gk, bd, rc, rs1, rs2, *, tiles_per_seq, tm):
    batch, seq, d = x.shape
    n = batch * tiles_per_seq * tm
    cw = conv_w.shape[1]
    qw = gq.shape[1]
    aw = w_in.shape[1] - 3 * cw - 2 * qw
    const = lambda i: (0, 0)
    tile = lambda i: (i, 0)
    pos = lambda i: (i % tiles_per_seq, 0)
    kern = functools.partial(_inproj_kernel, tiles_per_seq=tiles_per_seq, seq=seq, cw=cw, qw=qw)
    return pl.pallas_call(
        kern,
        grid=(n // tm,),
        in_specs=[
            pl.BlockSpec(memory_space=pl.ANY),
            pl.BlockSpec(memory_space=pl.ANY),
            pl.BlockSpec((1, d), const),
            pl.BlockSpec(w_in.shape, const),
            pl.BlockSpec(conv_w.shape, const),
            pl.BlockSpec((1, qw), const),
            pl.BlockSpec((1, qw), const),
            pl.BlockSpec(bd.shape, const),
            pl.BlockSpec((tm, LANES), pos),
            pl.BlockSpec((tm, LANES), pos),
            pl.BlockSpec((tm, LANES), pos),
        ],
        out_specs=[
            pl.BlockSpec((tm, d), tile),
            pl.BlockSpec((tm, cw), tile),
            pl.BlockSpec((tm, qw), tile),
            pl.BlockSpec((tm, qw), tile),
            pl.BlockSpec((tm, aw), tile),
        ],
        out_shape=[
            jax.ShapeDtypeStruct((n, d), x.dtype),
            jax.ShapeDtypeStruct((n, cw), BF16),
            jax.ShapeDtypeStruct((n, qw), BF16),
            jax.ShapeDtypeStruct((n, qw), BF16),
            jax.ShapeDtypeStruct((n, aw), BF16),
        ],
        scratch_shapes=[pltpu.VMEM((8, cw), F32), pltpu.VMEM((2, tm, d), x.dtype),
                        pltpu.SemaphoreType.DMA((2,))],
        compiler_params=_cparams(1),
        name="inproj",
    )(x.reshape(batch * seq, d), meta, g1, w_in, conv_w, gq, gk, bd, rc, rs1, rs2)


def _attn_kernel(q_ref, k_ref, v_ref, lamp_ref, sg_ref, o_ref, qs_ref, m_ref, l_ref, acc_ref, *, lam_init):
    qi = pl.program_id(2)
    tq = q_ref.shape[0]
    n_heads = q_ref.shape[1] // LANES
    n_chains = 2 * n_heads
    lane = lax.broadcasted_iota(jnp.int32, (tq, LANES), 1)
    for h in range(n_heads):
        q = q_ref[:, h * LANES:(h + 1) * LANES]
        zero = jnp.zeros_like(q)
        qs_ref[pl.ds(2 * h * tq, tq), :] = jnp.where(lane < HEAD_DIM, q, zero)
        qs_ref[pl.ds((2 * h + 1) * tq, tq), :] = jnp.where(lane >= HEAD_DIM, q, zero)
    m_ref[...] = jnp.full_like(m_ref, NEG_BIG)
    l_ref[...] = jnp.zeros_like(l_ref)
    acc_ref[...] = jnp.zeros_like(acc_ref)

    def scores(off, width, which):
        h = which // 2
        kc = k_ref[pl.ds(off, width), h * LANES:(h + 1) * LANES]
        return lax.dot_general(qs_ref[pl.ds(which * tq, tq), :], kc, (((1,), (1,)), ((), ())),
                               preferred_element_type=F32)

    def update(off, width, which, s, masked):
        h = which // 2
        vc = jnp.concatenate([v_ref[pl.ds(off, width), h * LANES:(h + 1) * LANES],
                              jnp.ones((width, LANES), BF16)], axis=1)
        rows = pl.ds(which * tq, tq)
        if masked:
            r = lax.broadcasted_iota(jnp.int32, s.shape, 0)
            c = lax.broadcasted_iota(jnp.int32, s.shape, 1)
            s = jnp.where(c <= r + (width - tq), s, NEG_BIG)
        m_prev = m_ref[rows, :]
        m_new = jnp.maximum(m_prev, jnp.max(s, axis=-1, keepdims=True))
        alpha = jnp.exp2(m_prev - m_new)
        p = jnp.exp2((s - jnp.tile(m_new, (1, width // LANES))).astype(BF16))
        pv = jnp.dot(p, vc, preferred_element_type=F32)
        l_ref[rows, :] = alpha * l_ref[rows, :] + pv[:, LANES:]
        acc_ref[rows, :] = alpha * acc_ref[rows, :] + pv[:, :LANES]
        m_ref[rows, :] = m_new

    def chunk(off, width, masked):
        s_next = scores(off, width, 0)
        for c in range(n_chains):
            s = s_next
            if c + 1 < n_chains:
                s_next = scores(off, width, c + 1)
            update(off, width, c, s, masked)

    wide = 2 * tq

    def body(j, carry):
        chunk(pl.multiple_of(j * wide, wide), wide, False)
        return carry

    lax.fori_loop(0, qi // 2, body, 0)
    odd = qi % 2 == 1

    @pl.when(odd)
    def _():
        chunk(pl.multiple_of((qi - 1) * tq, tq), wide, True)

    @pl.when(jnp.logical_not(odd))
    def _():
        chunk(pl.multiple_of(qi * tq, tq), tq, True)

    lp = lamp_ref[...]
    lam = (jnp.exp(jnp.sum(lp[0:1] * lp[1:2], axis=-1, keepdims=True))
           - jnp.exp(jnp.sum(lp[2:3] * lp[3:4], axis=-1, keepdims=True)) + lam_init)
    for h in range(n_heads):
        rows = pl.ds(2 * h * tq, 2 * tq)
        o_all = acc_ref[rows, :] / l_ref[rows, :]
        o = o_all[0:tq] - lam * o_all[tq:2 * tq]
        ms = jnp.mean(o * o, axis=-1, keepdims=True)
        o_ref[:, h * LANES:(h + 1) * LANES] = (o * lax.rsqrt(ms + EPS) * sg_ref[...]
                                               * (1.0 - lam_init)).astype(BF16)


def _attention(q, k, v, lamp, sg, *, batch, lp_len, tq, lam_init):
    n, qw = q.shape
    nq = lp_len // tq
    hw = ATTN_HEADS_PER_STEP * LANES
    chains = 2 * ATTN_HEADS_PER_STEP
    kern = functools.partial(_attn_kernel, lam_init=lam_init)
    return pl.pallas_call(
        kern,
        grid=(batch, qw // hw, nq),
        in_specs=[
            pl.BlockSpec((tq, hw), lambda b, h, i: (b * nq + i, h)),
            pl.BlockSpec((lp_len, hw), lambda b, h, i: (b, h)),
            pl.BlockSpec((lp_len, hw), lambda b, h, i: (b, h)),
            pl.BlockSpec(lamp.shape, lambda b, h, i: (0, 0)),
            pl.BlockSpec(sg.shape, lambda b, h, i: (0, 0)),
        ],
        out_specs=pl.BlockSpec((tq, hw), lambda b, h, i: (b * nq + i, h)),
        out_shape=jax.ShapeDtypeStruct((n, v.shape[1]), BF16),
        scratch_shapes=[
            pltpu.VMEM((chains * tq, LANES), BF16),
            pltpu.VMEM((chains * tq, LANES), F32),
            pltpu.VMEM((chains * tq, LANES), F32),
            pltpu.VMEM((chains * tq, LANES), F32),
        ],
        compiler_params=_cparams(3),
        name="diffattn",
    )(q, k, v, lamp, sg)


def _outproj_kernel(hp_ref, cy_ref, o_ref, wout_ref, g2_ref, wrh_ref, wrl_ref, br_ref, upper_ref,
                    h1_ref, xp0_ref, xp1_ref, xp2_ref, xp3_ref, route_ref, cnt_ref, run_ref,
                    *, tiles_per_seq, seq_len):
    i = pl.program_id(0)
    tm = hp_ref.shape[0]

    @pl.when(i == 0)
    def _():
        run_ref[...] = jnp.zeros_like(run_ref)

    mix = jnp.concatenate([cy_ref[...], o_ref[...]], axis=1)
    h1 = hp_ref[...] + jnp.dot(mix, wout_ref[...], preferred_element_type=F32)
    h1_ref[...] = h1
    ms = jnp.mean(h1 * h1, axis=-1, keepdims=True)
    xn = h1 * lax.rsqrt(ms + EPS) * g2_ref[...]
    xw = _pack_rows(xn)
    for c, ref in enumerate((xp0_ref, xp1_ref, xp2_ref, xp3_ref)):
        ref[...] = xw[:, c * LANES:(c + 1) * LANES]

    x_hi = xn.astype(BF16)
    x_lo = (xn - x_hi.astype(F32)).astype(BF16)
    hi_both = jnp.dot(x_hi, jnp.concatenate([wrh_ref[...], wrl_ref[...]], axis=1), preferred_element_type=F32)
    logits = (hi_both[:, :LANES] + hi_both[:, LANES:]
              + jnp.dot(x_lo, wrh_ref[...], preferred_element_type=F32) + br_ref[...])

    lt = logits.T[0:ROUTE_ROWS, :]
    row = lax.broadcasted_iota(jnp.int32, lt.shape, 0)
    big = jnp.int32(4 * LANES)

    def first_argmax(vals, vmax):
        return jnp.min(jnp.where(vals == vmax, row, big), axis=0, keepdims=True)

    gl = jnp.where(row < N_GROUPS, lt, NEG_BIG)
    gmax = jnp.max(gl, axis=0, keepdims=True)
    g_val = 1.0 / jnp.sum(jnp.exp(gl - gmax), axis=0, keepdims=True)
    g_idx = first_argmax(gl, gmax)
    lo = EXPERT_LANE0 + EXPERTS_PER_GROUP * g_idx
    el = jnp.where((row >= lo) & (row < lo + EXPERTS_PER_GROUP), lt, NEG_BIG)
    m1 = jnp.max(el, axis=0, keepdims=True)
    i1 = first_argmax(el, m1)
    el2 = jnp.where(row == i1, NEG_BIG, el)
    m2 = jnp.max(el2, axis=0, keepdims=True)
    i2 = first_argmax(el2, m2)
    r = jnp.exp(m2 - m1)
    gate1 = g_val / (1.0 + r)
    gate2 = g_val * r / (1.0 + r)

    pos = (i % tiles_per_seq) * tm + lax.broadcasted_iota(jnp.int32, (1, tm), 1)
    valid = pos < seq_len
    oh1 = jnp.where(valid & (row == i1), 1.0, 0.0)
    oh2 = jnp.where(valid & (row == i2), 1.0, 0.0)
    pre = jnp.dot(jnp.concatenate([oh1, oh2], axis=0).astype(BF16), upper_ref[...], preferred_element_type=F32)
    tot1 = jnp.sum(oh1, axis=1, keepdims=True)
    tot2 = jnp.sum(oh2, axis=1, keepdims=True)
    run = run_ref[...]
    run_t = jnp.tile(run, (1, tm // LANES))
    rank1 = jnp.sum(oh1 * (pre[:ROUTE_ROWS] + run_t), axis=0, keepdims=True)
    rank2 = jnp.sum(oh2 * (pre[ROUTE_ROWS:] + run_t + tot1), axis=0, keepdims=True)
    new_run = run + tot1 + tot2
    run_ref[...] = new_run
    cnt_ref[...] = new_run

    e1 = (i1 - EXPERT_LANE0).astype(F32)
    e2 = (i2 - EXPERT_LANE0).astype(F32)
    r8 = lax.broadcasted_iota(jnp.int32, (ROUTE_COLS, tm), 0)
    route_ref[...] = jnp.where(r8 == 0, e1, jnp.where(r8 == 1, e2, jnp.where(r8 == 2, gate1, jnp.where(
        r8 == 3, gate2, jnp.where(r8 == 4, rank1, jnp.where(r8 == 5, rank2, 0.0))))))


def _outproj(hp, convy, o, w_out, g2, wr_hi, wr_lo, br, upper, *, tiles_per_seq, seq_len, tm):
    n, d = hp.shape
    const = lambda i: (0, 0)
    tile = lambda i: (i, 0)
    kern = functools.partial(_outproj_kernel, tiles_per_seq=tiles_per_seq, seq_len=seq_len)
    return pl.pallas_call(
        kern,
        grid=(n // tm,),
        in_specs=[
            pl.BlockSpec((tm, d), tile),
            pl.BlockSpec((tm, convy.shape[1]), tile),
            pl.BlockSpec((tm, o.shape[1]), tile),
            pl.BlockSpec(w_out.shape, const),
            pl.BlockSpec((1, d), const),
            pl.BlockSpec(wr_hi.shape, const),
            pl.BlockSpec(wr_lo.shape, const),
            pl.BlockSpec((1, LANES), const),
            pl.BlockSpec(upper.shape, const),
        ],
        out_specs=[
            pl.BlockSpec((tm, d), tile),
            *[pl.BlockSpec((tm, LANES), tile)] * ROW_PLANES,
            pl.BlockSpec((ROUTE_COLS, tm), lambda i: (0, i)),
            pl.BlockSpec((ROUTE_ROWS, LANES), const),
        ],
        out_shape=[
            jax.ShapeDtypeStruct((n, d), F32),
            *[jax.ShapeDtypeStruct((n, LANES), jnp.uint32)] * ROW_PLANES,
            jax.ShapeDtypeStruct((ROUTE_COLS, n), F32),
            jax.ShapeDtypeStruct((ROUTE_ROWS, LANES), F32),
        ],
        scratch_shapes=[pltpu.VMEM((ROUTE_ROWS, LANES), F32)],
        compiler_params=_cparams(1),
        name="outproj_router",
    )(hp, convy, o, w_out, g2, wr_hi, wr_lo, br, upper)


def _slots_kernel(pstart_ref, route_ref, dest_ref, *, seq_len, p_rows):
    b = pl.program_id(0)
    route = route_ref[...]
    eid = route.astype(jnp.int32)
    start = jnp.zeros_like(eid)
    for e in range(N_EXPERTS):
        start = jnp.where(eid == e, pstart_ref[e], start)
    rank = pltpu.roll(route, ROUTE_COLS - 2 * TOP_K, axis=0).astype(jnp.int32)
    k = lax.broadcasted_iota(jnp.int32, route.shape, 0)
    pos = lax.broadcasted_iota(jnp.int32, route.shape, 1)
    n_pad = route.shape[1] - seq_len
    spare = p_rows + (b * n_pad + (pos - seq_len)) * TOP_K + k
    dest_ref[...] = jnp.where(pos < seq_len, start + rank, spare)


def _slots(pstarts, route, *, batch, lp_len, seq_len, p_rows):
    kern = functools.partial(_slots_kernel, seq_len=seq_len, p_rows=p_rows)
    grid_spec = pltpu.PrefetchScalarGridSpec(
        num_scalar_prefetch=1,
        grid=(batch,),
        in_specs=[pl.BlockSpec((ROUTE_COLS, lp_len), lambda b, ps: (0, b))],
        out_specs=pl.BlockSpec((ROUTE_COLS, lp_len), lambda b, ps: (0, b)),
    )
    return pl.pallas_call(
        kern,
        grid_spec=grid_spec,
        out_shape=jax.ShapeDtypeStruct(route.shape, jnp.int32),
        compiler_params=_cparams(1),
        name="moe_slots",
    )(pstarts, route)


def _sc_workers():
    info = plsc.get_sparse_core_info()
    return info.num_cores, info.num_cores * info.num_subcores


def _sc_scatter_rows(planes, idx_a, idx_b, out_rows):
    n_win = idx_a.shape[0]
    n_cores, n_workers = _sc_workers()
    trips = -(-n_win // n_workers)
    mesh = plsc.VectorSubcoreMesh(core_axis_name="c", subcore_axis_name="s")

    def body(*refs):
        xs = refs[0:ROW_PLANES]
        ia_hbm, ib_hbm = refs[ROW_PLANES:ROW_PLANES + 2]
        outs = refs[ROW_PLANES + 2:2 * ROW_PLANES + 2]
        ia_v, ib_v, buf, sem = refs[2 * ROW_PLANES + 2:]
        wid = lax.axis_index("s") * n_cores + lax.axis_index("c")

        def step(t, carry):
            g = wid + t * n_workers

            @pl.when(g < n_win)
            def _():
                pltpu.sync_copy(ia_hbm.at[g], ia_v)
                pltpu.sync_copy(ib_hbm.at[g], ib_v)
                row0 = pl.multiple_of(g * SC_WINDOW, SC_WINDOW)
                loads = [pltpu.async_copy(xs[c].at[pl.ds(row0, SC_WINDOW)], buf.at[c], sem)
                         for c in range(ROW_PLANES)]
                for cp in loads:
                    cp.wait()
                stores = [pltpu.async_copy(buf.at[c], outs[c].at[iv], sem)
                          for c in range(ROW_PLANES) for iv in (ia_v, ib_v)]
                for cp in stores:
                    cp.wait()

            return carry

        lax.fori_loop(0, trips, step, 0)

    kern = pl.kernel(
        body,
        out_type=[jax.ShapeDtypeStruct((out_rows, LANES), jnp.uint32)] * ROW_PLANES,
        mesh=mesh,
        scratch_types=[
            pltpu.VMEM((SC_WINDOW,), jnp.int32),
            pltpu.VMEM((SC_WINDOW,), jnp.int32),
            pltpu.VMEM((ROW_PLANES, SC_WINDOW, LANES), jnp.uint32),
            pltpu.SemaphoreType.DMA,
        ],
        name="moe_dispatch_sc",
    )
    return kern(*planes, idx_a, idx_b)


def _sc_gather_rows(planes, idx_a, idx_b):
    n_win = idx_a.shape[0]
    n_cores, n_workers = _sc_workers()
    trips = -(-n_win // n_workers)
    mesh = plsc.VectorSubcoreMesh(core_axis_name="c", subcore_axis_name="s")

    def body(*refs):
        ys = refs[0:ROW_PLANES]
        ia_hbm, ib_hbm = refs[ROW_PLANES:ROW_PLANES + 2]
        outs_a = refs[ROW_PLANES + 2:2 * ROW_PLANES + 2]
        outs_b = refs[2 * ROW_PLANES + 2:3 * ROW_PLANES + 2]
        iv, buf, sem = refs[3 * ROW_PLANES + 2:]
        wid = lax.axis_index("s") * n_cores + lax.axis_index("c")

        def step(t, carry):
            g = wid + t * n_workers

            @pl.when(g < n_win)
            def _():
                row0 = pl.multiple_of(g * SC_WINDOW, SC_WINDOW)
                for i_hbm, outs in ((ia_hbm, outs_a), (ib_hbm, outs_b)):
                    pltpu.sync_copy(i_hbm.at[g], iv)
                    loads = [pltpu.async_copy(ys[c].at[iv], buf.at[c], sem) for c in range(ROW_PLANES)]
                    for cp in loads:
                        cp.wait()
                    stores = [pltpu.async_copy(buf.at[c], outs[c].at[pl.ds(row0, SC_WINDOW)], sem)
                              for c in range(ROW_PLANES)]
                    for cp in stores:
                        cp.wait()

            return carry

        lax.fori_loop(0, trips, step, 0)

    n_rows = n_win * SC_WINDOW
    kern = pl.kernel(
        body,
        out_type=[jax.ShapeDtypeStruct((n_rows, LANES), jnp.uint32)] * (2 * ROW_PLANES),
        mesh=mesh,
        scratch_types=[
            pltpu.VMEM((SC_WINDOW,), jnp.int32),
            pltpu.VMEM((ROW_PLANES, SC_WINDOW, LANES), jnp.uint32),
            pltpu.SemaphoreType.DMA,
        ],
        name="moe_gather_sc",
    )
    res = kern(*planes, idx_a, idx_b)
    return res[:ROW_PLANES], res[ROW_PLANES:]


def _experts_kernel(be_ref, nv_ref, first_ref, slot_ref, nxt_ref, x0_ref, x1_ref, x2_ref, x3_ref,
                    wg_hbm, wu_hbm, wd_hbm, y0_ref, y1_ref, y2_ref, y3_ref,
                    wgf_ref, wuf_ref, wdf_ref, wgb_ref, wub_ref, wdb_ref, sem):
    i = pl.program_id(0)
    e = be_ref[i]
    slot = slot_ref[i]
    y_refs = (y0_ref, y1_ref, y2_ref, y3_ref)

    def weight_copies(expert, s):
        return [pltpu.make_async_copy(hbm.at[expert], stage.at[s], sem.at[s, j])
                for j, (hbm, stage) in enumerate(((wg_hbm, wgf_ref), (wu_hbm, wuf_ref), (wd_hbm, wdf_ref)))]

    @pl.when(i == 0)
    def _():
        for cp in weight_copies(e, slot):
            cp.start()

    @pl.when(first_ref[i] == 1)
    def _():
        for cp in weight_copies(e, slot):
            cp.wait()
        nxt = nxt_ref[i]

        @pl.when(nxt >= 0)
        def _():
            for cp in weight_copies(nxt, 1 - slot):
                cp.start()

        wgb_ref[...] = wgf_ref[slot].astype(BF16)
        wub_ref[...] = wuf_ref[slot].astype(BF16)
        wdb_ref[...] = wdf_ref[slot].astype(BF16)

    nv = nv_ref[i]
    half = MOE_BLOCK // 2

    def mlp(rows):
        xs = _unpack_rows([r[rows, :] for r in (x0_ref, x1_ref, x2_ref, x3_ref)])
        row = lax.broadcasted_iota(jnp.int32, xs.shape, 0)
        x = jnp.where(row < nv, xs, jnp.zeros_like(xs))
        hg = jnp.dot(x, wgb_ref[...], preferred_element_type=F32)
        hu = jnp.dot(x, wub_ref[...], preferred_element_type=F32)
        hid = (hg / (1.0 + jnp.exp(-hg)) * hu).astype(BF16)
        yw = _pack_rows(jnp.dot(hid, wdb_ref[...], preferred_element_type=F32))
        for c, ref in enumerate(y_refs):
            ref[rows, :] = yw[:, c * LANES:(c + 1) * LANES]

    def clear(rows):
        for ref in y_refs:
            ref[rows, :] = jnp.zeros((rows.size, LANES), ref.dtype)

    @pl.when(nv > half)
    def _():
        mlp(pl.ds(0, MOE_BLOCK))

    @pl.when((nv > 0) & (nv <= half))
    def _():
        mlp(pl.ds(0, half))
        clear(pl.ds(half, half))

    @pl.when(nv == 0)
    def _():
        clear(pl.ds(0, MOE_BLOCK))


def _experts(block_e, nvalid, xs_planes, w_gate, w_up, w_down):
    n_blocks = block_e.shape[0]
    _, d, ff = w_gate.shape
    first = jnp.concatenate([jnp.ones((1,), jnp.int32), (block_e[1:] != block_e[:-1]).astype(jnp.int32)])
    slot = (jnp.cumsum(first) - 1) % 2
    later = jnp.where(block_e[None, :] > block_e[:, None], block_e[None, :], N_EXPERTS)
    nxt = jnp.min(later, axis=1)
    nxt = jnp.where(nxt == N_EXPERTS, -1, nxt).astype(jnp.int32)
    blk = lambda i, *_: (i, 0)
    grid_spec = pltpu.PrefetchScalarGridSpec(
        num_scalar_prefetch=5,
        grid=(n_blocks,),
        in_specs=[
            *[pl.BlockSpec((MOE_BLOCK, LANES), blk)] * ROW_PLANES,
            pl.BlockSpec(memory_space=pl.ANY),
            pl.BlockSpec(memory_space=pl.ANY),
            pl.BlockSpec(memory_space=pl.ANY),
        ],
        out_specs=[pl.BlockSpec((MOE_BLOCK, LANES), blk)] * ROW_PLANES,
        scratch_shapes=[
            pltpu.VMEM((2, d, ff), F32),
            pltpu.VMEM((2, d, ff), F32),
            pltpu.VMEM((2, ff, d), F32),
            pltpu.VMEM((d, ff), BF16),
            pltpu.VMEM((d, ff), BF16),
            pltpu.VMEM((ff, d), BF16),
            pltpu.SemaphoreType.DMA((2, 3)),
        ],
    )
    return pl.pallas_call(
        _experts_kernel,
        grid_spec=grid_spec,
        out_shape=[jax.ShapeDtypeStruct((n_blocks * MOE_BLOCK, LANES), jnp.uint32)] * ROW_PLANES,
        compiler_params=_cparams(1),
        name="moe_experts",
    )(block_e, nvalid, first, slot.astype(jnp.int32), nxt, *xs_planes, w_gate, w_up, w_down)


def _combine_kernel(gates_ref, h1_hbm, *refs, lp_len, tile):
    a_refs = refs[0:ROW_PLANES]
    b_refs = refs[ROW_PLANES:2 * ROW_PLANES]
    out_ref, hbuf, sem_h = refs[2 * ROW_PLANES:]
    nt = pl.num_programs(1)
    step = pl.program_id(0) * nt + pl.program_id(1)
    last = pl.num_programs(0) * nt - 1

    def h_copy(s, slot):
        start = (s // nt) * lp_len + N_META + (s % nt) * tile
        return pltpu.make_async_copy(h1_hbm.at[pl.ds(start, tile), :], hbuf.at[slot], sem_h.at[slot])

    slot = step % 2

    @pl.when(step == 0)
    def _():
        h_copy(step, slot).start()

    @pl.when(step < last)
    def _():
        h_copy(step + 1, 1 - slot).start()

    ya = _unpack_rows([r[...] for r in a_refs]).astype(F32)
    yb = _unpack_rows([r[...] for r in b_refs]).astype(F32)
    g = jnp.concatenate([gates_ref[...]] * (LANES // ROUTE_COLS), axis=0).T
    moe = g[:, 0:1] * ya + g[:, 1:2] * yb
    h_copy(step, slot).wait()
    out_ref[0] = hbuf[slot] + moe


def _combine(gates, h1, a_planes, b_planes, *, batch, seq, lp_len, tile):
    d = h1.shape[1]
    nt = seq // tile
    kern = functools.partial(_combine_kernel, lp_len=lp_len, tile=tile)
    rows = lambda b, i: (b * nt + i, 0)
    return pl.pallas_call(
        kern,
        grid=(batch, nt),
        in_specs=[
            pl.BlockSpec((ROUTE_COLS, tile), lambda b, i: (0, b * nt + i)),
            pl.BlockSpec(memory_space=pl.ANY),
            *[pl.BlockSpec((tile, LANES), rows)] * (2 * ROW_PLANES),
        ],
        out_specs=pl.BlockSpec((1, tile, d), lambda b, i: (b, i, 0)),
        out_shape=jax.ShapeDtypeStruct((batch, seq, d), F32),
        scratch_shapes=[pltpu.VMEM((2, tile, d), F32), pltpu.SemaphoreType.DMA((2,))],
        compiler_params=_cparams(2),
        name="moe_combine",
    )(gates, h1, *a_planes, *b_planes)


def _rope_tables(length, lp_len):
    half = ROPE_DIM // 2
    pos = jnp.arange(length, dtype=F32)
    inv_freq = ROPE_THETA ** (-jnp.arange(0, ROPE_DIM, 2, dtype=F32) / ROPE_DIM)
    ang = pos[:, None] * inv_freq[None, :]
    cos = jnp.cos(ang)
    sin = jnp.sin(ang)
    ones = jnp.ones((length, HEAD_DIM - ROPE_DIM), F32)
    zeros_h = jnp.zeros((length, half), F32)
    zeros_r = jnp.zeros((length, HEAD_DIM - ROPE_DIM), F32)
    c = jnp.concatenate([cos, cos, ones], axis=1)
    s1 = jnp.concatenate([zeros_h, sin, zeros_r], axis=1)
    s2 = jnp.concatenate([-sin, zeros_h, zeros_r], axis=1)
    pad = ((0, lp_len - length), (0, 0))
    rep = LANES // HEAD_DIM
    return tuple(jnp.pad(jnp.tile(t, (1, rep)), pad) for t in (c, s1, s2))


def kernel(x, meta_tokens, norm1_g, w_in, conv_w, q_norm_g, k_norm_g, lambda_q1, lambda_k1, lambda_q2, lambda_k2,
           subln_g, w_out, norm2_g, w_router_group, b_router_group, w_router_expert, b_router_expert, w_gate,
           w_up, w_down):
    batch, seq, _ = x.shape
    assert w_in.shape[0] == 1, "a single layer is supported"
    l = 0
    length = seq + N_META
    tm = TOKEN_TILE
    lp_len = -(-length // tm) * tm
    tiles_per_seq = lp_len // tm
    assert tiles_per_seq >= 2 and (length - (tiles_per_seq - 1) * tm) % 8 == 0
    qw = N_HEADS * 2 * HEAD_DIM
    lam_init = 0.8 - 0.6 * math.exp(-0.3 * l)

    reps = qw // HEAD_DIM
    gq = jnp.tile(q_norm_g[l] * (HEAD_DIM ** -0.5 * LOG2E), reps)[None, :]
    gk = jnp.tile(k_norm_g[l], reps)[None, :]
    seg = jnp.arange(qw) // HEAD_DIM
    bd = (seg[:, None] == seg[None, :]).astype(BF16)
    rope = _rope_tables(length, lp_len)
    hp, convy, q, k, v = _inproj(x, meta_tokens.astype(x.dtype), norm1_g[l][None, :], w_in[l].astype(BF16),
                                 conv_w[l], gq, gk, bd, *rope, tiles_per_seq=tiles_per_seq, tm=tm)

    lamp = jnp.stack([lambda_q1[l], lambda_k1[l], lambda_q2[l], lambda_k2[l]]).astype(F32)
    o = _attention(q, k, v, lamp, subln_g[l][None, :], batch=batch, lp_len=lp_len, tq=tm, lam_init=lam_init)

    lane_pad = LANES - N_GROUPS - N_EXPERTS
    wr = jnp.pad(jnp.concatenate([w_router_group[l], w_router_expert[l]], axis=1), ((0, 0), (0, lane_pad)))
    wr_hi = wr.astype(BF16)
    wr_lo = (wr - wr_hi.astype(F32)).astype(BF16)
    br = jnp.pad(jnp.concatenate([b_router_group[l], b_router_expert[l]]), (0, lane_pad))[None, :]
    ridx = jnp.arange(tm)
    upper = (ridx[:, None] < ridx[None, :]).astype(BF16)
    h1, *rest = _outproj(hp, convy, o, w_out[l].astype(BF16), norm2_g[l][None, :], wr_hi, wr_lo, br, upper,
                         tiles_per_seq=tiles_per_seq, seq_len=length, tm=tm)
    x_planes = rest[:ROW_PLANES]
    route, cnt = rest[ROW_PLANES:]

    counts = cnt[EXPERT_LANE0:EXPERT_LANE0 + N_EXPERTS, 0].astype(jnp.int32)
    n_blocks = -(-(batch * length * TOP_K) // MOE_BLOCK) + N_EXPERTS
    p_rows = n_blocks * MOE_BLOCK
    padded = (counts + MOE_BLOCK - 1) // MOE_BLOCK * MOE_BLOCK
    pends = jnp.cumsum(padded)
    pstarts = pends - padded

    def lookup(table, idx):
        hit = idx[:, None] == jnp.arange(N_EXPERTS, dtype=jnp.int32)[None, :]
        return jnp.sum(jnp.where(hit, table[None, :], 0), axis=1)

    blk0 = jnp.arange(n_blocks, dtype=jnp.int32) * MOE_BLOCK
    block_e = jnp.minimum(jnp.sum((pends[None, :] <= blk0[:, None]).astype(jnp.int32), axis=1), N_EXPERTS - 1)
    nvalid = jnp.clip(lookup(counts, block_e) - (blk0 - lookup(pstarts, block_e)), 0, MOE_BLOCK)

    spare_rows = -(-(batch * (lp_len - length) * TOP_K) // MOE_BLOCK) * MOE_BLOCK
    dest = _slots(pstarts.astype(jnp.int32), route, batch=batch, lp_len=lp_len, seq_len=length, p_rows=p_rows)
    dest = dest.reshape(ROUTE_COLS, batch, lp_len)
    gates = route.reshape(ROUTE_COLS, batch, lp_len)[TOP_K:2 * TOP_K]

    assert (batch * lp_len) % SC_WINDOW == 0 and (batch * seq) % SC_WINDOW == 0
    xs_planes = _sc_scatter_rows(x_planes, dest[0].reshape(-1, SC_WINDOW), dest[1].reshape(-1, SC_WINDOW),
                                 p_rows + spare_rows)
    y_planes = _experts(block_e, nvalid, xs_planes, w_gate[l], w_up[l], w_down[l])

    dest_x = dest[0:TOP_K, :, N_META:length]
    a_planes, b_planes = _sc_gather_rows(y_planes, dest_x[0].reshape(-1, SC_WINDOW),
                                         dest_x[1].reshape(-1, SC_WINDOW))
    gates_x = jnp.pad(gates[:, :, N_META:length].reshape(TOP_K, batch * seq), ((0, ROUTE_COLS - TOP_K), (0, 0)))
    return _combine(gates_x, h1, a_planes, b_planes, batch=batch, seq=seq, lp_len=lp_len,
                    tile=_largest_tile(seq, 512, LANES))
```

```python
import functools
import math

import jax
import jax.numpy as jnp
from jax import lax
from jax.experimental import pallas as pl
from jax.experimental.pallas import tpu as pltpu
from jax.experimental.pallas import tpu_sc as plsc

F32 = jnp.float32
BF16 = jnp.bfloat16

N_META = 16
N_HEADS = 4
HEAD_DIM = 64
ROPE_DIM = HEAD_DIM // 4
ROPE_THETA = 500000.0
N_GROUPS = 4
EXPERTS_PER_GROUP = 8
N_EXPERTS = N_GROUPS * EXPERTS_PER_GROUP
TOP_K = 2
EPS = 1e-6
LOG2E = 1.4426950408889634

LANES = 128
TOKEN_TILE = 640
INPROJ_CHAINS = 2
ATTN_HEADS_PER_STEP = 4
MOE_BLOCK = 512
ROUTE_COLS = 8
ROUTE_ROWS = 64
ROW_PLANES = 4
SC_WINDOW = 128
EXPERT_LANE0 = N_GROUPS
NEG_BIG = -1e30
VMEM_LIMIT = 56 * 1024 * 1024


def _largest_tile(n, cap, mult):
    for t in range(min(cap, n), 0, -1):
        if n % t == 0 and t % mult == 0:
            return t
    raise ValueError(f"no tile for {n}")


def _cparams(n_axes):
    return pltpu.CompilerParams(dimension_semantics=("arbitrary",) * n_axes, vmem_limit_bytes=VMEM_LIMIT)


def _pack_rows(x):
    w = x.shape[1] // 2
    lo = lax.bitcast_convert_type(x[:, :w].astype(BF16).astype(F32), jnp.uint32)
    hi = lax.bitcast_convert_type(x[:, w:].astype(BF16).astype(F32), jnp.uint32)
    return lax.shift_right_logical(lo, jnp.uint32(16)) | (hi & jnp.uint32(0xFFFF0000))


def _unpack_rows(planes):
    w = jnp.concatenate(planes, axis=1)
    lo = lax.bitcast_convert_type(lax.shift_left(w, jnp.uint32(16)), F32)
    hi = lax.bitcast_convert_type(w & jnp.uint32(0xFFFF0000), F32)
    return jnp.concatenate([lo, hi], axis=1).astype(BF16)


def _inproj_kernel(x_hbm, meta_hbm, g1_ref, win_ref, convw_ref, gq_ref, gk_ref, bd_ref, rc_ref, rs1_ref, rs2_ref,
                   hp_hbm, convy_ref, q_ref, k_ref, v_ref, carry_ref, xbuf, sem, sem_out,
                   *, tiles_per_seq, seq, cw, qw):
    i = pl.program_id(0)
    tm = xbuf.shape[1]
    sub = tm // INPROJ_CHAINS
    q0 = 3 * cw
    w = convw_ref[...]
    last_rows = seq + N_META - (tiles_per_seq - 1) * tm

    def fetch(step, slot, start):
        b = step // tiles_per_seq
        t = step % tiles_per_seq

        def go(src, dst):
            cp = pltpu.make_async_copy(src, dst, sem.at[slot])
            if start:
                cp.start()
            else:
                cp.wait()

        @pl.when(t == 0)
        def _():
            go(meta_hbm, xbuf.at[slot, pl.ds(0, N_META)])
            go(x_hbm.at[pl.ds(b * seq, tm - N_META)], xbuf.at[slot, pl.ds(N_META, tm - N_META)])

        @pl.when((t > 0) & (t < tiles_per_seq - 1))
        def _():
            go(x_hbm.at[pl.ds(b * seq + t * tm - N_META, tm)], xbuf.at[slot])

        @pl.when(t == tiles_per_seq - 1)
        def _():
            go(x_hbm.at[pl.ds(b * seq + t * tm - N_META, last_rows)], xbuf.at[slot, pl.ds(0, last_rows)])

    slot = i % 2
    n_steps = pl.num_programs(0)

    def hp_store(step, s):
        return pltpu.make_async_copy(xbuf.at[s], hp_hbm.at[pl.ds(step * tm, tm)], sem_out.at[s])

    @pl.when(i == 0)
    def _():
        fetch(i, slot, True)

    @pl.when(i > 0)
    def _():
        hp_store(i - 1, 1 - slot).wait()

    @pl.when(i + 1 < n_steps)
    def _():
        fetch(i + 1, 1 - slot, True)

    fetch(i, slot, False)

    @pl.when(i % tiles_per_seq == tiles_per_seq - 1)
    def _():
        xbuf[slot, pl.ds(last_rows, tm - last_rows), :] = jnp.zeros((tm - last_rows, xbuf.shape[2]), xbuf.dtype)

    hp_store(i, slot).start()

    @pl.when(i % tiles_per_seq == 0)
    def _():
        carry_ref[...] = jnp.zeros_like(carry_ref)

    prev = carry_ref[...]

    for chain in range(INPROJ_CHAINS):
        rows = pl.ds(chain * sub, sub)
        x = xbuf[slot, rows, :]
        ms = jnp.mean(x * x, axis=-1, keepdims=True)
        xn = (x * lax.rsqrt(ms + EPS) * g1_ref[...]).astype(BF16)

        def proj(lo, hi):
            return jnp.dot(xn, win_ref[:, lo:hi], preferred_element_type=F32)

        u_conv = proj(0, q0)
        u_q = proj(q0, q0 + qw)

        z = u_conv[:, cw:2 * cw] * u_conv[:, 2 * cw:3 * cw]
        p1 = prev[7:8]
        p2 = prev[6:7]
        row = lax.broadcasted_iota(jnp.int32, z.shape, 0)
        z1 = jnp.where(row == 0, p1, pltpu.roll(z, 1, axis=0))
        z2 = jnp.where(row == 0, p2, jnp.where(row == 1, p1, pltpu.roll(z, 2, axis=0)))
        prev = z[sub - 8:sub]
        conv = w[0:1] * z2 + w[1:2] * z1 + w[2:3] * z
        convy_ref[rows, :] = (u_conv[:, 0:cw] * conv).astype(BF16)

        rc = rc_ref[rows, :]
        rs1 = rs1_ref[rows, :]
        rs2 = rs2_ref[rows, :]

        def norm_rope(t, g_ref):
            ss = jnp.dot((t * t).astype(BF16), bd_ref[...], preferred_element_type=F32)
            tn = t * lax.rsqrt(ss * (1.0 / HEAD_DIM) + EPS) * g_ref[...]
            outs = []
            for c in range(qw // LANES):
                ch = tn[:, c * LANES:(c + 1) * LANES]
                outs.append(ch * rc + pltpu.roll(ch, ROPE_DIM // 2, axis=1) * rs1
                            + pltpu.roll(ch, LANES - ROPE_DIM // 2, axis=1) * rs2)
            return jnp.concatenate(outs, axis=1).astype(BF16)

        u_k = proj(q0 + qw, q0 + 2 * qw)
        q_ref[rows, :] = norm_rope(u_q, gq_ref)
        u_v = proj(q0 + 2 * qw, win_ref.shape[1])
        k_ref[rows, :] = norm_rope(u_k, gk_ref)
        v_ref[rows, :] = u_v.astype(BF16)

    carry_ref[...] = prev

    @pl.when(i == n_steps - 1)
    def _():
        hp_store(i, slot).wait()


def _inproj(x, meta, g1, w_in, conv_w, gq, gk, bd, rc, rs1, rs2, *, tiles_per_seq, tm):
    batch, seq, d = x.shape
    n = batch * tiles_per_seq * tm
    cw = conv_w.shape[1]
    qw = gq.shape[1]
    aw = w_in.shape[1] - 3 * cw - 2 * qw
    const = lambda i: (0, 0)
    tile = lambda i: (i, 0)
    pos = lambda i: (i % tiles_per_seq, 0)
    kern = functools.partial(_inproj_kernel, tiles_per_seq=tiles_per_seq, seq=seq, cw=cw, qw=qw)
    return pl.pallas_call(
        kern,
        grid=(n // tm,),
        in_specs=[
            pl.BlockSpec(memory_space=pl.ANY),
            pl.BlockSpec(memory_space=pl.ANY),
            pl.BlockSpec((1, d), const),
            pl.BlockSpec(w_in.shape, const),
            pl.BlockSpec(conv_w.shape, const),
            pl.BlockSpec((1, qw), const),
            pl.BlockSpec((1, qw), const),
            pl.BlockSpec(bd.shape, const),
            pl.BlockSpec((tm, LANES), pos),
            pl.BlockSpec((tm, LANES), pos),
            pl.BlockSpec((tm, LANES), pos),
        ],
        out_specs=[
            pl.BlockSpec(memory_space=pl.ANY),
            pl.BlockSpec((tm, cw), tile),
            pl.BlockSpec((tm, qw), tile),
            pl.BlockSpec((tm, qw), tile),
            pl.BlockSpec((tm, aw), tile),
        ],
        out_shape=[
            jax.ShapeDtypeStruct((n, d), x.dtype),
            jax.ShapeDtypeStruct((n, cw), BF16),
            jax.ShapeDtypeStruct((n, qw), BF16),
            jax.ShapeDtypeStruct((n, qw), BF16),
            jax.ShapeDtypeStruct((n, aw), BF16),
        ],
        scratch_shapes=[pltpu.VMEM((8, cw), F32), pltpu.VMEM((2, tm, d), x.dtype),
                        pltpu.SemaphoreType.DMA((2,)), pltpu.SemaphoreType.DMA((2,))],
        compiler_params=_cparams(1),
        name="inproj",
    )(x.reshape(batch * seq, d), meta, g1, w_in, conv_w, gq, gk, bd, rc, rs1, rs2)


def _attn_kernel(q_ref, k_ref, v_ref, lamp_ref, sg_ref, o_ref, qs_ref, m_ref, l_ref, acc_ref, *, lam_init):
    qi = pl.program_id(2)
    tq = q_ref.shape[0]
    n_heads = q_ref.shape[1] // LANES
    n_chains = 2 * n_heads
    lane = lax.broadcasted_iota(jnp.int32, (tq, LANES), 1)
    for h in range(n_heads):
        q = q_ref[:, h * LANES:(h + 1) * LANES]
        zero = jnp.zeros_like(q)
        qs_ref[pl.ds(2 * h * tq, tq), :] = jnp.where(lane < HEAD_DIM, q, zero)
        qs_ref[pl.ds((2 * h + 1) * tq, tq), :] = jnp.where(lane >= HEAD_DIM, q, zero)
    m_ref[...] = jnp.full_like(m_ref, NEG_BIG)
    l_ref[...] = jnp.zeros_like(l_ref)
    acc_ref[...] = jnp.zeros_like(acc_ref)

    def scores(off, width, which):
        h = which // 2
        kc = k_ref[pl.ds(off, width), h * LANES:(h + 1) * LANES]
        return lax.dot_general(qs_ref[pl.ds(which * tq, tq), :], kc, (((1,), (1,)), ((), ())),
                               preferred_element_type=F32)

    def update(off, width, which, s, masked):
        h = which // 2
        vc = jnp.concatenate([v_ref[pl.ds(off, width), h * LANES:(h + 1) * LANES],
                              jnp.ones((width, LANES), BF16)], axis=1)
        rows = pl.ds(which * tq, tq)
        if masked:
            r = lax.broadcasted_iota(jnp.int32, s.shape, 0)
            c = lax.broadcasted_iota(jnp.int32, s.shape, 1)
            s = jnp.where(c <= r + (width - tq), s, NEG_BIG)
        m_prev = m_ref[rows, :]
        m_new = jnp.maximum(m_prev, jnp.max(s, axis=-1, keepdims=True))
        alpha = jnp.exp2(m_prev - m_new)
        p = jnp.exp2((s - jnp.tile(m_new, (1, width // LANES))).astype(BF16))
        pv = jnp.dot(p, vc, preferred_element_type=F32)
        l_ref[rows, :] = alpha * l_ref[rows, :] + pv[:, LANES:]
        acc_ref[rows, :] = alpha * acc_ref[rows, :] + pv[:, :LANES]
        m_ref[rows, :] = m_new

    def chunk(off, width, masked):
        s_next = scores(off, width, 0)
        for c in range(n_chains):
            s = s_next
            if c + 1 < n_chains:
                s_next = scores(off, width, c + 1)
            update(off, width, c, s, masked)

    wide = 2 * tq

    def body(j, carry):
        chunk(pl.multiple_of(j * wide, wide), wide, False)
        return carry

    lax.fori_loop(0, qi // 2, body, 0)
    odd = qi % 2 == 1

    @pl.when(odd)
    def _():
        chunk(pl.multiple_of((qi - 1) * tq, tq), wide, True)

    @pl.when(jnp.logical_not(odd))
    def _():
        chunk(pl.multiple_of(qi * tq, tq), tq, True)

    lp = lamp_ref[...]
    lam = (jnp.exp(jnp.sum(lp[0:1] * lp[1:2], axis=-1, keepdims=True))
           - jnp.exp(jnp.sum(lp[2:3] * lp[3:4], axis=-1, keepdims=True)) + lam_init)
    for h in range(n_heads):
        rows = pl.ds(2 * h * tq, 2 * tq)
        o_all = acc_ref[rows, :] / l_ref[rows, :]
        o = o_all[0:tq] - lam * o_all[tq:2 * tq]
        ms = jnp.mean(o * o, axis=-1, keepdims=True)
        o_ref[:, h * LANES:(h + 1) * LANES] = (o * lax.rsqrt(ms + EPS) * sg_ref[...]
                                               * (1.0 - lam_init)).astype(BF16)


def _attention(q, k, v, lamp, sg, *, batch, lp_len, tq, lam_init):
    n, qw = q.shape
    nq = lp_len // tq
    hw = ATTN_HEADS_PER_STEP * LANES
    chains = 2 * ATTN_HEADS_PER_STEP
    kern = functools.partial(_attn_kernel, lam_init=lam_init)
    return pl.pallas_call(
        kern,
        grid=(batch, qw // hw, nq),
        in_specs=[
            pl.BlockSpec((tq, hw), lambda b, h, i: (b * nq + i, h)),
            pl.BlockSpec((lp_len, hw), lambda b, h, i: (b, h)),
            pl.BlockSpec((lp_len, hw), lambda b, h, i: (b, h)),
            pl.BlockSpec(lamp.shape, lambda b, h, i: (0, 0)),
            pl.BlockSpec(sg.shape, lambda b, h, i: (0, 0)),
        ],
        out_specs=pl.BlockSpec((tq, hw), lambda b, h, i: (b * nq + i, h)),
        out_shape=jax.ShapeDtypeStruct((n, v.shape[1]), BF16),
        scratch_shapes=[
            pltpu.VMEM((chains * tq, LANES), BF16),
            pltpu.VMEM((chains * tq, LANES), F32),
            pltpu.VMEM((chains * tq, LANES), F32),
            pltpu.VMEM((chains * tq, LANES), F32),
        ],
        compiler_params=_cparams(3),
        name="diffattn",
    )(q, k, v, lamp, sg)


def _outproj_kernel(hp_ref, cy_ref, o_ref, wout_ref, g2_ref, wrh_ref, wrl_ref, br_ref, upper_ref,
                    h1_ref, xp0_ref, xp1_ref, xp2_ref, xp3_ref, route_ref, cnt_ref, run_ref,
                    *, tiles_per_seq, seq_len):
    i = pl.program_id(0)
    tm = hp_ref.shape[0]

    @pl.when(i == 0)
    def _():
        run_ref[...] = jnp.zeros_like(run_ref)

    mix = jnp.concatenate([cy_ref[...], o_ref[...]], axis=1)
    h1 = hp_ref[...] + jnp.dot(mix, wout_ref[...], preferred_element_type=F32)
    h1_ref[...] = h1
    ms = jnp.mean(h1 * h1, axis=-1, keepdims=True)
    xn = h1 * lax.rsqrt(ms + EPS) * g2_ref[...]
    xw = _pack_rows(xn)
    for c, ref in enumerate((xp0_ref, xp1_ref, xp2_ref, xp3_ref)):
        ref[...] = xw[:, c * LANES:(c + 1) * LANES]

    x_hi = xn.astype(BF16)
    x_lo = (xn - x_hi.astype(F32)).astype(BF16)
    hi_both = jnp.dot(x_hi, jnp.concatenate([wrh_ref[...], wrl_ref[...]], axis=1), preferred_element_type=F32)
    logits = (hi_both[:, :LANES] + hi_both[:, LANES:]
              + jnp.dot(x_lo, wrh_ref[...], preferred_element_type=F32) + br_ref[...])

    lt = logits.T[0:ROUTE_ROWS, :]
    row = lax.broadcasted_iota(jnp.int32, lt.shape, 0)
    big = jnp.int32(4 * LANES)

    def first_argmax(vals, vmax):
        return jnp.min(jnp.where(vals == vmax, row, big), axis=0, keepdims=True)

    gl = jnp.where(row < N_GROUPS, lt, NEG_BIG)
    gmax = jnp.max(gl, axis=0, keepdims=True)
    g_val = 1.0 / jnp.sum(jnp.exp(gl - gmax), axis=0, keepdims=True)
    g_idx = first_argmax(gl, gmax)
    lo = EXPERT_LANE0 + EXPERTS_PER_GROUP * g_idx
    el = jnp.where((row >= lo) & (row < lo + EXPERTS_PER_GROUP), lt, NEG_BIG)
    m1 = jnp.max(el, axis=0, keepdims=True)
    i1 = first_argmax(el, m1)
    el2 = jnp.where(row == i1, NEG_BIG, el)
    m2 = jnp.max(el2, axis=0, keepdims=True)
    i2 = first_argmax(el2, m2)
    r = jnp.exp(m2 - m1)
    gate1 = g_val / (1.0 + r)
    gate2 = g_val * r / (1.0 + r)

    pos = (i % tiles_per_seq) * tm + lax.broadcasted_iota(jnp.int32, (1, tm), 1)
    valid = pos < seq_len
    oh1 = jnp.where(valid & (row == i1), 1.0, 0.0)
    oh2 = jnp.where(valid & (row == i2), 1.0, 0.0)
    pre = jnp.dot(jnp.concatenate([oh1, oh2], axis=0).astype(BF16), upper_ref[...], preferred_element_type=F32)
    tot1 = jnp.sum(oh1, axis=1, keepdims=True)
    tot2 = jnp.sum(oh2, axis=1, keepdims=True)
    run = run_ref[...]
    run_t = jnp.tile(run, (1, tm // LANES))
    rank1 = jnp.sum(oh1 * (pre[:ROUTE_ROWS] + run_t), axis=0, keepdims=True)
    rank2 = jnp.sum(oh2 * (pre[ROUTE_ROWS:] + run_t + tot1), axis=0, keepdims=True)
    new_run = run + tot1 + tot2
    run_ref[...] = new_run
    cnt_ref[...] = new_run

    e1 = (i1 - EXPERT_LANE0).astype(F32)
    e2 = (i2 - EXPERT_LANE0).astype(F32)
    r8 = lax.broadcasted_iota(jnp.int32, (ROUTE_COLS, tm), 0)
    route_ref[...] = jnp.where(r8 == 0, e1, jnp.where(r8 == 1, e2, jnp.where(r8 == 2, gate1, jnp.where(
        r8 == 3, gate2, jnp.where(r8 == 4, rank1, jnp.where(r8 == 5, rank2, 0.0))))))


def _outproj(hp, convy, o, w_out, g2, wr_hi, wr_lo, br, upper, *, tiles_per_seq, seq_len, tm):
    n, d = hp.shape
    const = lambda i: (0, 0)
    tile = lambda i: (i, 0)
    kern = functools.partial(_outproj_kernel, tiles_per_seq=tiles_per_seq, seq_len=seq_len)
    return pl.pallas_call(
        kern,
        grid=(n // tm,),
        in_specs=[
            pl.BlockSpec((tm, d), tile),
            pl.BlockSpec((tm, convy.shape[1]), tile),
            pl.BlockSpec((tm, o.shape[1]), tile),
            pl.BlockSpec(w_out.shape, const),
            pl.BlockSpec((1, d), const),
            pl.BlockSpec(wr_hi.shape, const),
            pl.BlockSpec(wr_lo.shape, const),
            pl.BlockSpec((1, LANES), const),
            pl.BlockSpec(upper.shape, const),
        ],
        out_specs=[
            pl.BlockSpec((tm, d), tile),
            *[pl.BlockSpec((tm, LANES), tile)] * ROW_PLANES,
            pl.BlockSpec((ROUTE_COLS, tm), lambda i: (0, i)),
            pl.BlockSpec((ROUTE_ROWS, LANES), const),
        ],
        out_shape=[
            jax.ShapeDtypeStruct((n, d), F32),
            *[jax.ShapeDtypeStruct((n, LANES), jnp.uint32)] * ROW_PLANES,
            jax.ShapeDtypeStruct((ROUTE_COLS, n), F32),
            jax.ShapeDtypeStruct((ROUTE_ROWS, LANES), F32),
        ],
        scratch_shapes=[pltpu.VMEM((ROUTE_ROWS, LANES), F32)],
        compiler_params=_cparams(1),
        name="outproj_router",
    )(hp, convy, o, w_out, g2, wr_hi, wr_lo, br, upper)


def _slots_kernel(pstart_ref, route_ref, dest_ref, *, seq_len, p_rows):
    b = pl.program_id(0)
    route = route_ref[...]
    eid = route.astype(jnp.int32)
    start = jnp.zeros_like(eid)
    for e in range(N_EXPERTS):
        start = jnp.where(eid == e, pstart_ref[e], start)
    rank = pltpu.roll(route, ROUTE_COLS - 2 * TOP_K, axis=0).astype(jnp.int32)
    k = lax.broadcasted_iota(jnp.int32, route.shape, 0)
    pos = lax.broadcasted_iota(jnp.int32, route.shape, 1)
    n_pad = route.shape[1] - seq_len
    spare = p_rows + (b * n_pad + (pos - seq_len)) * TOP_K + k
    dest_ref[...] = jnp.where(pos < seq_len, start + rank, spare)


def _slots(pstarts, route, *, batch, lp_len, seq_len, p_rows):
    kern = functools.partial(_slots_kernel, seq_len=seq_len, p_rows=p_rows)
    grid_spec = pltpu.PrefetchScalarGridSpec(
        num_scalar_prefetch=1,
        grid=(batch,),
        in_specs=[pl.BlockSpec((ROUTE_COLS, lp_len), lambda b, ps: (0, b))],
        out_specs=pl.BlockSpec((ROUTE_COLS, lp_len), lambda b, ps: (0, b)),
    )
    return pl.pallas_call(
        kern,
        grid_spec=grid_spec,
        out_shape=jax.ShapeDtypeStruct(route.shape, jnp.int32),
        compiler_params=_cparams(1),
        name="moe_slots",
    )(pstarts, route)


def _sc_workers():
    info = plsc.get_sparse_core_info()
    return info.num_cores, info.num_cores * info.num_subcores


def _sc_scatter_rows(planes, idx_a, idx_b, out_rows):
    n_win = idx_a.shape[0]
    n_cores, n_workers = _sc_workers()
    trips = -(-n_win // n_workers)
    mesh = plsc.VectorSubcoreMesh(core_axis_name="c", subcore_axis_name="s")

    def body(*refs):
        xs = refs[0:ROW_PLANES]
        ia_hbm, ib_hbm = refs[ROW_PLANES:ROW_PLANES + 2]
        outs = refs[ROW_PLANES + 2:2 * ROW_PLANES + 2]
        ia_v, ib_v, buf, sem = refs[2 * ROW_PLANES + 2:]
        wid = lax.axis_index("s") * n_cores + lax.axis_index("c")

        def step(t, carry):
            g = wid + t * n_workers

            @pl.when(g < n_win)
            def _():
                row0 = pl.multiple_of(g * SC_WINDOW, SC_WINDOW)
                loads = [pltpu.async_copy(ia_hbm.at[g], ia_v, sem), pltpu.async_copy(ib_hbm.at[g], ib_v, sem)]
                loads += [pltpu.async_copy(xs[c].at[pl.ds(row0, SC_WINDOW)], buf.at[c], sem)
                          for c in range(ROW_PLANES)]
                for cp in loads:
                    cp.wait()
                stores = [pltpu.async_copy(buf.at[c], outs[c].at[iv], sem)
                          for c in range(ROW_PLANES) for iv in (ia_v, ib_v)]
                for cp in stores:
                    cp.wait()

            return carry

        lax.fori_loop(0, trips, step, 0)

    kern = pl.kernel(
        body,
        out_type=[jax.ShapeDtypeStruct((out_rows, LANES), jnp.uint32)] * ROW_PLANES,
        mesh=mesh,
        scratch_types=[
            pltpu.VMEM((SC_WINDOW,), jnp.int32),
            pltpu.VMEM((SC_WINDOW,), jnp.int32),
            pltpu.VMEM((ROW_PLANES, SC_WINDOW, LANES), jnp.uint32),
            pltpu.SemaphoreType.DMA,
        ],
        name="moe_dispatch_sc",
    )
    return kern(*planes, idx_a, idx_b)


def _sc_gather_rows(planes, idx_a, idx_b):
    n_win = idx_a.shape[0]
    n_cores, n_workers = _sc_workers()
    trips = -(-n_win // n_workers)
    mesh = plsc.VectorSubcoreMesh(core_axis_name="c", subcore_axis_name="s")

    def body(*refs):
        ys = refs[0:ROW_PLANES]
        ia_hbm, ib_hbm = refs[ROW_PLANES:ROW_PLANES + 2]
        outs_a = refs[ROW_PLANES + 2:2 * ROW_PLANES + 2]
        outs_b = refs[2 * ROW_PLANES + 2:3 * ROW_PLANES + 2]
        ia_v, ib_v, buf, sem = refs[3 * ROW_PLANES + 2:]
        wid = lax.axis_index("s") * n_cores + lax.axis_index("c")

        def step(t, carry):
            g = wid + t * n_workers

            @pl.when(g < n_win)
            def _():
                row0 = pl.multiple_of(g * SC_WINDOW, SC_WINDOW)
                idx_loads = [pltpu.async_copy(ia_hbm.at[g], ia_v, sem), pltpu.async_copy(ib_hbm.at[g], ib_v, sem)]
                for cp in idx_loads:
                    cp.wait()
                for iv, outs in ((ia_v, outs_a), (ib_v, outs_b)):
                    loads = [pltpu.async_copy(ys[c].at[iv], buf.at[c], sem) for c in range(ROW_PLANES)]
                    for cp in loads:
                        cp.wait()
                    stores = [pltpu.async_copy(buf.at[c], outs[c].at[pl.ds(row0, SC_WINDOW)], sem)
                              for c in range(ROW_PLANES)]
                    for cp in stores:
                        cp.wait()

            return carry

        lax.fori_loop(0, trips, step, 0)

    n_rows = n_win * SC_WINDOW
    kern = pl.kernel(
        body,
        out_type=[jax.ShapeDtypeStruct((n_rows, LANES), jnp.uint32)] * (2 * ROW_PLANES),
        mesh=mesh,
        scratch_types=[
            pltpu.VMEM((SC_WINDOW,), jnp.int32),
            pltpu.VMEM((SC_WINDOW,), jnp.int32),
            pltpu.VMEM((ROW_PLANES, SC_WINDOW, LANES), jnp.uint32),
            pltpu.SemaphoreType.DMA,
        ],
        name="moe_gather_sc",
    )
    res = kern(*planes, idx_a, idx_b)
    return res[:ROW_PLANES], res[ROW_PLANES:]


def _experts_kernel(be_ref, nv_ref, first_ref, slot_ref, nxt_ref, x0_ref, x1_ref, x2_ref, x3_ref,
                    wg_hbm, wu_hbm, wd_hbm, y0_ref, y1_ref, y2_ref, y3_ref,
                    wgf_ref, wuf_ref, wdf_ref, wgb_ref, wub_ref, wdb_ref, sem):
    i = pl.program_id(0)
    e = be_ref[i]
    slot = slot_ref[i]
    y_refs = (y0_ref, y1_ref, y2_ref, y3_ref)

    def weight_copies(expert, s):
        return [pltpu.make_async_copy(hbm.at[expert], stage.at[s], sem.at[s, j])
                for j, (hbm, stage) in enumerate(((wg_hbm, wgf_ref), (wu_hbm, wuf_ref), (wd_hbm, wdf_ref)))]

    @pl.when(i == 0)
    def _():
        for cp in weight_copies(e, slot):
            cp.start()

    @pl.when(first_ref[i] == 1)
    def _():
        for cp in weight_copies(e, slot):
            cp.wait()
        nxt = nxt_ref[i]

        @pl.when(nxt >= 0)
        def _():
            for cp in weight_copies(nxt, 1 - slot):
                cp.start()

        wgb_ref[...] = wgf_ref[slot].astype(BF16)
        wub_ref[...] = wuf_ref[slot].astype(BF16)
        wdb_ref[...] = wdf_ref[slot].astype(BF16)

    nv = nv_ref[i]
    half = MOE_BLOCK // 2

    def mlp(rows):
        xs = _unpack_rows([r[rows, :] for r in (x0_ref, x1_ref, x2_ref, x3_ref)])
        row = lax.broadcasted_iota(jnp.int32, xs.shape, 0)
        x = jnp.where(row < nv, xs, jnp.zeros_like(xs))
        hg = jnp.dot(x, wgb_ref[...], preferred_element_type=F32)
        hu = jnp.dot(x, wub_ref[...], preferred_element_type=F32)
        hid = (hg / (1.0 + jnp.exp(-hg)) * hu).astype(BF16)
        yw = _pack_rows(jnp.dot(hid, wdb_ref[...], preferred_element_type=F32))
        for c, ref in enumerate(y_refs):
            ref[rows, :] = yw[:, c * LANES:(c + 1) * LANES]

    def clear(rows):
        for ref in y_refs:
            ref[rows, :] = jnp.zeros((rows.size, LANES), ref.dtype)

    @pl.when(nv > half)
    def _():
        mlp(pl.ds(0, MOE_BLOCK))

    @pl.when((nv > 0) & (nv <= half))
    def _():
        mlp(pl.ds(0, half))
        clear(pl.ds(half, half))

    @pl.when(nv == 0)
    def _():
        clear(pl.ds(0, MOE_BLOCK))


def _experts(block_e, nvalid, xs_planes, w_gate, w_up, w_down):
    n_blocks = block_e.shape[0]
    _, d, ff = w_gate.shape
    first = jnp.concatenate([jnp.ones((1,), jnp.int32), (block_e[1:] != block_e[:-1]).astype(jnp.int32)])
    slot = (jnp.cumsum(first) - 1) % 2
    later = jnp.where(block_e[None, :] > block_e[:, None], block_e[None, :], N_EXPERTS)
    nxt = jnp.min(later, axis=1)
    nxt = jnp.where(nxt == N_EXPERTS, -1, nxt).astype(jnp.int32)
    blk = lambda i, *_: (i, 0)
    grid_spec = pltpu.PrefetchScalarGridSpec(
        num_scalar_prefetch=5,
        grid=(n_blocks,),
        in_specs=[
            *[pl.BlockSpec((MOE_BLOCK, LANES), blk)] * ROW_PLANES,
            pl.BlockSpec(memory_space=pl.ANY),
            pl.BlockSpec(memory_space=pl.ANY),
            pl.BlockSpec(memory_space=pl.ANY),
        ],
        out_specs=[pl.BlockSpec((MOE_BLOCK, LANES), blk)] * ROW_PLANES,
        scratch_shapes=[
            pltpu.VMEM((2, d, ff), F32),
            pltpu.VMEM((2, d, ff), F32),
            pltpu.VMEM((2, ff, d), F32),
            pltpu.VMEM((d, ff), BF16),
            pltpu.VMEM((d, ff), BF16),
            pltpu.VMEM((ff, d), BF16),
            pltpu.SemaphoreType.DMA((2, 3)),
        ],
    )
    return pl.pallas_call(
        _experts_kernel,
        grid_spec=grid_spec,
        out_shape=[jax.ShapeDtypeStruct((n_blocks * MOE_BLOCK, LANES), jnp.uint32)] * ROW_PLANES,
        compiler_params=_cparams(1),
        name="moe_experts",
    )(block_e, nvalid, first, slot.astype(jnp.int32), nxt, *xs_planes, w_gate, w_up, w_down)


def _combine_kernel(gates_ref, h1_hbm, *refs, lp_len, tile):
    a_refs = refs[0:ROW_PLANES]
    b_refs = refs[ROW_PLANES:2 * ROW_PLANES]
    out_ref, hbuf, sem_h = refs[2 * ROW_PLANES:]
    nt = pl.num_programs(1)
    step = pl.program_id(0) * nt + pl.program_id(1)
    last = pl.num_programs(0) * nt - 1

    def h_copy(s, slot):
        start = (s // nt) * lp_len + N_META + (s % nt) * tile
        return pltpu.make_async_copy(h1_hbm.at[pl.ds(start, tile), :], hbuf.at[slot], sem_h.at[slot])

    slot = step % 2

    @pl.when(step == 0)
    def _():
        h_copy(step, slot).start()

    @pl.when(step < last)
    def _():
        h_copy(step + 1, 1 - slot).start()

    ya = _unpack_rows([r[...] for r in a_refs]).astype(F32)
    yb = _unpack_rows([r[...] for r in b_refs]).astype(F32)
    g = jnp.concatenate([gates_ref[...]] * (LANES // ROUTE_COLS), axis=0).T
    moe = g[:, 0:1] * ya + g[:, 1:2] * yb
    h_copy(step, slot).wait()
    out_ref[0] = hbuf[slot] + moe


def _combine(gates, h1, a_planes, b_planes, *, batch, seq, lp_len, tile):
    d = h1.shape[1]
    nt = seq // tile
    kern = functools.partial(_combine_kernel, lp_len=lp_len, tile=tile)
    rows = lambda b, i: (b * nt + i, 0)
    return pl.pallas_call(
        kern,
        grid=(batch, nt),
        in_specs=[
            pl.BlockSpec((ROUTE_COLS, tile), lambda b, i: (0, b * nt + i)),
            pl.BlockSpec(memory_space=pl.ANY),
            *[pl.BlockSpec((tile, LANES), rows)] * (2 * ROW_PLANES),
        ],
        out_specs=pl.BlockSpec((1, tile, d), lambda b, i: (b, i, 0)),
        out_shape=jax.ShapeDtypeStruct((batch, seq, d), F32),
        scratch_shapes=[pltpu.VMEM((2, tile, d), F32), pltpu.SemaphoreType.DMA((2,))],
        compiler_params=_cparams(2),
        name="moe_combine",
    )(gates, h1, *a_planes, *b_planes)


def _rope_tables(length, lp_len):
    half = ROPE_DIM // 2
    pos = jnp.arange(length, dtype=F32)
    inv_freq = ROPE_THETA ** (-jnp.arange(0, ROPE_DIM, 2, dtype=F32) / ROPE_DIM)
    ang = pos[:, None] * inv_freq[None, :]
    cos = jnp.cos(ang)
    sin = jnp.sin(ang)
    ones = jnp.ones((length, HEAD_DIM - ROPE_DIM), F32)
    zeros_h = jnp.zeros((length, half), F32)
    zeros_r = jnp.zeros((length, HEAD_DIM - ROPE_DIM), F32)
    c = jnp.concatenate([cos, cos, ones], axis=1)
    s1 = jnp.concatenate([zeros_h, sin, zeros_r], axis=1)
    s2 = jnp.concatenate([-sin, zeros_h, zeros_r], axis=1)
    pad = ((0, lp_len - length), (0, 0))
    rep = LANES // HEAD_DIM
    return tuple(jnp.pad(jnp.tile(t, (1, rep)), pad) for t in (c, s1, s2))


def kernel(x, meta_tokens, norm1_g, w_in, conv_w, q_norm_g, k_norm_g, lambda_q1, lambda_k1, lambda_q2, lambda_k2,
           subln_g, w_out, norm2_g, w_router_group, b_router_group, w_router_expert, b_router_expert, w_gate,
           w_up, w_down):
    batch, seq, _ = x.shape
    assert w_in.shape[0] == 1, "a single layer is supported"
    l = 0
    length = seq + N_META
    tm = TOKEN_TILE
    lp_len = -(-length // tm) * tm
    tiles_per_seq = lp_len // tm
    assert tiles_per_seq >= 2 and (length - (tiles_per_seq - 1) * tm) % 8 == 0
    qw = N_HEADS * 2 * HEAD_DIM
    lam_init = 0.8 - 0.6 * math.exp(-0.3 * l)

    reps = qw // HEAD_DIM
    gq = jnp.tile(q_norm_g[l] * (HEAD_DIM ** -0.5 * LOG2E), reps)[None, :]
    gk = jnp.tile(k_norm_g[l], reps)[None, :]
    seg = jnp.arange(qw) // HEAD_DIM
    bd = (seg[:, None] == seg[None, :]).astype(BF16)
    rope = _rope_tables(length, lp_len)
    hp, convy, q, k, v = _inproj(x, meta_tokens.astype(x.dtype), norm1_g[l][None, :], w_in[l].astype(BF16),
                                 conv_w[l], gq, gk, bd, *rope, tiles_per_seq=tiles_per_seq, tm=tm)

    lamp = jnp.stack([lambda_q1[l], lambda_k1[l], lambda_q2[l], lambda_k2[l]]).astype(F32)
    o = _attention(q, k, v, lamp, subln_g[l][None, :], batch=batch, lp_len=lp_len, tq=tm, lam_init=lam_init)

    lane_pad = LANES - N_GROUPS - N_EXPERTS
    wr = jnp.pad(jnp.concatenate([w_router_group[l], w_router_expert[l]], axis=1), ((0, 0), (0, lane_pad)))
    wr_hi = wr.astype(BF16)
    wr_lo = (wr - wr_hi.astype(F32)).astype(BF16)
    br = jnp.pad(jnp.concatenate([b_router_group[l], b_router_expert[l]]), (0, lane_pad))[None, :]
    ridx = jnp.arange(tm)
    upper = (ridx[:, None] < ridx[None, :]).astype(BF16)
    h1, *rest = _outproj(hp, convy, o, w_out[l].astype(BF16), norm2_g[l][None, :], wr_hi, wr_lo, br, upper,
                         tiles_per_seq=tiles_per_seq, seq_len=length, tm=tm)
    x_planes = rest[:ROW_PLANES]
    route, cnt = rest[ROW_PLANES:]

    counts = cnt[EXPERT_LANE0:EXPERT_LANE0 + N_EXPERTS, 0].astype(jnp.int32)
    n_blocks = -(-(batch * length * TOP_K) // MOE_BLOCK) + N_EXPERTS
    p_rows = n_blocks * MOE_BLOCK
    padded = (counts + MOE_BLOCK - 1) // MOE_BLOCK * MOE_BLOCK
    pends = jnp.cumsum(padded)
    pstarts = pends - padded

    def lookup(table, idx):
        hit = idx[:, None] == jnp.arange(N_EXPERTS, dtype=jnp.int32)[None, :]
        return jnp.sum(jnp.where(hit, table[None, :], 0), axis=1)

    blk0 = jnp.arange(n_blocks, dtype=jnp.int32) * MOE_BLOCK
    block_e = jnp.minimum(jnp.sum((pends[None, :] <= blk0[:, None]).astype(jnp.int32), axis=1), N_EXPERTS - 1)
    nvalid = jnp.clip(lookup(counts, block_e) - (blk0 - lookup(pstarts, block_e)), 0, MOE_BLOCK)

    spare_rows = -(-(batch * (lp_len - length) * TOP_K) // MOE_BLOCK) * MOE_BLOCK
    dest = _slots(pstarts.astype(jnp.int32), route, batch=batch, lp_len=lp_len, seq_len=length, p_rows=p_rows)
    dest = dest.reshape(ROUTE_COLS, batch, lp_len)
    gates = route.reshape(ROUTE_COLS, batch, lp_len)[TOP_K:2 * TOP_K]

    assert (batch * lp_len) % SC_WINDOW == 0 and (batch * seq) % SC_WINDOW == 0
    xs_planes = _sc_scatter_rows(x_planes, dest[0].reshape(-1, SC_WINDOW), dest[1].reshape(-1, SC_WINDOW),
                                 p_rows + spare_rows)
    y_planes = _experts(block_e, nvalid, xs_planes, w_gate[l], w_up[l], w_down[l])

    dest_x = dest[0:TOP_K, :, N_META:length]
    a_planes, b_planes = _sc_gather_rows(y_planes, dest_x[0].reshape(-1, SC_WINDOW),
                                         dest_x[1].reshape(-1, SC_WINDOW))
    gates_x = jnp.pad(gates[:, :, N_META:length].reshape(TOP_K, batch * seq), ((0, ROUTE_COLS - TOP_K), (0, 0)))
    return _combine(gates_x, h1, a_planes, b_planes, batch=batch, seq=seq, lp_len=lp_len,
                    tile=_largest_tile(seq, 512, LANES))
```

```python
import functools
import math

import jax
import jax.numpy as jnp
from jax import lax
from jax.experimental import pallas as pl
from jax.experimental.pallas import tpu as pltpu
from jax.experimental.pallas import tpu_sc as plsc

F32 = jnp.float32
BF16 = jnp.bfloat16

N_META = 16
N_HEADS = 4
HEAD_DIM = 64
ROPE_DIM = HEAD_DIM // 4
ROPE_THETA = 500000.0
N_GROUPS = 4
EXPERTS_PER_GROUP = 8
N_EXPERTS = N_GROUPS * EXPERTS_PER_GROUP
TOP_K = 2
EPS = 1e-6
LOG2E = 1.4426950408889634

LANES = 128
TOKEN_TILE = 640
INPROJ_CHAINS = 2
ATTN_HEADS_PER_STEP = 4
MOE_BLOCK = 1024
MOE_TAIL_STEP = 256
ROUTE_COLS = 8
ROUTE_ROWS = 64
ROW_PLANES = 4
SC_WINDOW = 128
EXPERT_LANE0 = N_GROUPS
NEG_BIG = -1e30
VMEM_LIMIT = 56 * 1024 * 1024


def _largest_tile(n, cap, mult):
    for t in range(min(cap, n), 0, -1):
        if n % t == 0 and t % mult == 0:
            return t
    raise ValueError(f"no tile for {n}")


def _cparams(n_axes):
    return pltpu.CompilerParams(dimension_semantics=("arbitrary",) * n_axes, vmem_limit_bytes=VMEM_LIMIT)


def _pack_rows(x):
    w = x.shape[1] // 2
    lo = lax.bitcast_convert_type(x[:, :w].astype(BF16).astype(F32), jnp.uint32)
    hi = lax.bitcast_convert_type(x[:, w:].astype(BF16).astype(F32), jnp.uint32)
    return lax.shift_right_logical(lo, jnp.uint32(16)) | (hi & jnp.uint32(0xFFFF0000))


def _unpack_rows(planes):
    w = jnp.concatenate(planes, axis=1)
    lo = lax.bitcast_convert_type(lax.shift_left(w, jnp.uint32(16)), F32)
    hi = lax.bitcast_convert_type(w & jnp.uint32(0xFFFF0000), F32)
    return jnp.concatenate([lo, hi], axis=1).astype(BF16)


def _inproj_kernel(x_hbm, meta_hbm, g1_ref, win_ref, convw_ref, gq_ref, gk_ref, bd_ref, rc_ref, rs1_ref, rs2_ref,
                   hp_hbm, convy_ref, q_ref, k_ref, v_ref, carry_ref, xbuf, sem, sem_out,
                   *, tiles_per_seq, seq, cw, qw):
    i = pl.program_id(0)
    tm = xbuf.shape[1]
    sub = tm // INPROJ_CHAINS
    q0 = 3 * cw
    w = convw_ref[...]
    last_rows = seq + N_META - (tiles_per_seq - 1) * tm

    def fetch(step, slot, start):
        b = step // tiles_per_seq
        t = step % tiles_per_seq

        def go(src, dst):
            cp = pltpu.make_async_copy(src, dst, sem.at[slot])
            if start:
                cp.start()
            else:
                cp.wait()

        @pl.when(t == 0)
        def _():
            go(meta_hbm, xbuf.at[slot, pl.ds(0, N_META)])
            go(x_hbm.at[pl.ds(b * seq, tm - N_META)], xbuf.at[slot, pl.ds(N_META, tm - N_META)])

        @pl.when((t > 0) & (t < tiles_per_seq - 1))
        def _():
            go(x_hbm.at[pl.ds(b * seq + t * tm - N_META, tm)], xbuf.at[slot])

        @pl.when(t == tiles_per_seq - 1)
        def _():
            go(x_hbm.at[pl.ds(b * seq + t * tm - N_META, last_rows)], xbuf.at[slot, pl.ds(0, last_rows)])

    slot = i % 2
    n_steps = pl.num_programs(0)

    def hp_store(step, s):
        return pltpu.make_async_copy(xbuf.at[s], hp_hbm.at[pl.ds(step * tm, tm)], sem_out.at[s])

    @pl.when(i == 0)
    def _():
        fetch(i, slot, True)

    @pl.when(i > 0)
    def _():
        hp_store(i - 1, 1 - slot).wait()

    @pl.when(i + 1 < n_steps)
    def _():
        fetch(i + 1, 1 - slot, True)

    fetch(i, slot, False)

    @pl.when(i % tiles_per_seq == tiles_per_seq - 1)
    def _():
        xbuf[slot, pl.ds(last_rows, tm - last_rows), :] = jnp.zeros((tm - last_rows, xbuf.shape[2]), xbuf.dtype)

    hp_store(i, slot).start()

    @pl.when(i % tiles_per_seq == 0)
    def _():
        carry_ref[...] = jnp.zeros_like(carry_ref)

    prev = carry_ref[...]

    for chain in range(INPROJ_CHAINS):
        rows = pl.ds(chain * sub, sub)
        x = xbuf[slot, rows, :]
        ms = jnp.mean(x * x, axis=-1, keepdims=True)
        xn = (x * lax.rsqrt(ms + EPS) * g1_ref[...]).astype(BF16)

        def proj(lo, hi):
            return jnp.dot(xn, win_ref[:, lo:hi], preferred_element_type=F32)

        u_conv = proj(0, q0)
        u_q = proj(q0, q0 + qw)

        z = u_conv[:, cw:2 * cw] * u_conv[:, 2 * cw:3 * cw]
        p1 = prev[7:8]
        p2 = prev[6:7]
        row = lax.broadcasted_iota(jnp.int32, z.shape, 0)
        z1 = jnp.where(row == 0, p1, pltpu.roll(z, 1, axis=0))
        z2 = jnp.where(row == 0, p2, jnp.where(row == 1, p1, pltpu.roll(z, 2, axis=0)))
        prev = z[sub - 8:sub]
        conv = w[0:1] * z2 + w[1:2] * z1 + w[2:3] * z
        convy_ref[rows, :] = (u_conv[:, 0:cw] * conv).astype(BF16)

        rc = rc_ref[rows, :]
        rs1 = rs1_ref[rows, :]
        rs2 = rs2_ref[rows, :]

        def norm_rope(t, g_ref):
            ss = jnp.dot((t * t).astype(BF16), bd_ref[...], preferred_element_type=F32)
            tn = t * lax.rsqrt(ss * (1.0 / HEAD_DIM) + EPS) * g_ref[...]
            outs = []
            for c in range(qw // LANES):
                ch = tn[:, c * LANES:(c + 1) * LANES]
                outs.append(ch * rc + pltpu.roll(ch, ROPE_DIM // 2, axis=1) * rs1
                            + pltpu.roll(ch, LANES - ROPE_DIM // 2, axis=1) * rs2)
            return jnp.concatenate(outs, axis=1).astype(BF16)

        u_k = proj(q0 + qw, q0 + 2 * qw)
        q_ref[rows, :] = norm_rope(u_q, gq_ref)
        u_v = proj(q0 + 2 * qw, win_ref.shape[1])
        k_ref[rows, :] = norm_rope(u_k, gk_ref)
        v_ref[rows, :] = u_v.astype(BF16)

    carry_ref[...] = prev

    @pl.when(i == n_steps - 1)
    def _():
        hp_store(i, slot).wait()


def _inproj(x, meta, g1, w_in, conv_w, gq, gk, bd, rc, rs1, rs2, *, tiles_per_seq, tm):
    batch, seq, d = x.shape
    n = batch * tiles_per_seq * tm
    cw = conv_w.shape[1]
    qw = gq.shape[1]
    aw = w_in.shape[1] - 3 * cw - 2 * qw
    const = lambda i: (0, 0)
    tile = lambda i: (i, 0)
    pos = lambda i: (i % tiles_per_seq, 0)
    kern = functools.partial(_inproj_kernel, tiles_per_seq=tiles_per_seq, seq=seq, cw=cw, qw=qw)
    return pl.pallas_call(
        kern,
        grid=(n // tm,),
        in_specs=[
            pl.BlockSpec(memory_space=pl.ANY),
            pl.BlockSpec(memory_space=pl.ANY),
            pl.BlockSpec((1, d), const),
            pl.BlockSpec(w_in.shape, const),
            pl.BlockSpec(conv_w.shape, const),
            pl.BlockSpec((1, qw), const),
            pl.BlockSpec((1, qw), const),
            pl.BlockSpec(bd.shape, const),
            pl.BlockSpec((tm, LANES), pos),
            pl.BlockSpec((tm, LANES), pos),
            pl.BlockSpec((tm, LANES), pos),
        ],
        out_specs=[
            pl.BlockSpec(memory_space=pl.ANY),
            pl.BlockSpec((tm, cw), tile),
            pl.BlockSpec((tm, qw), tile),
            pl.BlockSpec((tm, qw), tile),
            pl.BlockSpec((tm, aw), tile),
        ],
        out_shape=[
            jax.ShapeDtypeStruct((n, d), x.dtype),
            jax.ShapeDtypeStruct((n, cw), BF16),
            jax.ShapeDtypeStruct((n, qw), BF16),
            jax.ShapeDtypeStruct((n, qw), BF16),
            jax.ShapeDtypeStruct((n, aw), BF16),
        ],
        scratch_shapes=[pltpu.VMEM((8, cw), F32), pltpu.VMEM((2, tm, d), x.dtype),
                        pltpu.SemaphoreType.DMA((2,)), pltpu.SemaphoreType.DMA((2,))],
        compiler_params=_cparams(1),
        name="inproj",
    )(x.reshape(batch * seq, d), meta, g1, w_in, conv_w, gq, gk, bd, rc, rs1, rs2)


def _attn_kernel(q_ref, k_ref, v_ref, lamp_ref, sg_ref, o_ref, qs_ref, m_ref, l_ref, acc_ref, *, lam_init):
    qi = pl.program_id(2)
    tq = q_ref.shape[0]
    n_heads = q_ref.shape[1] // LANES
    n_chains = 2 * n_heads
    lane = lax.broadcasted_iota(jnp.int32, (tq, LANES), 1)
    for h in range(n_heads):
        q = q_ref[:, h * LANES:(h + 1) * LANES]
        zero = jnp.zeros_like(q)
        qs_ref[pl.ds(2 * h * tq, tq), :] = jnp.where(lane < HEAD_DIM, q, zero)
        qs_ref[pl.ds((2 * h + 1) * tq, tq), :] = jnp.where(lane >= HEAD_DIM, q, zero)
    m_ref[...] = jnp.full_like(m_ref, NEG_BIG)
    l_ref[...] = jnp.zeros_like(l_ref)
    acc_ref[...] = jnp.zeros_like(acc_ref)

    def scores(off, width, which):
        h = which // 2
        kc = k_ref[pl.ds(off, width), h * LANES:(h + 1) * LANES]
        return lax.dot_general(qs_ref[pl.ds(which * tq, tq), :], kc, (((1,), (1,)), ((), ())),
                               preferred_element_type=F32)

    def update(off, width, which, s, masked):
        h = which // 2
        vc = jnp.concatenate([v_ref[pl.ds(off, width), h * LANES:(h + 1) * LANES],
                              jnp.ones((width, LANES), BF16)], axis=1)
        rows = pl.ds(which * tq, tq)
        if masked:
            r = lax.broadcasted_iota(jnp.int32, s.shape, 0)
            c = lax.broadcasted_iota(jnp.int32, s.shape, 1)
            s = jnp.where(c <= r + (width - tq), s, NEG_BIG)
        m_prev = m_ref[rows, :]
        m_new = jnp.maximum(m_prev, jnp.max(s, axis=-1, keepdims=True))
        alpha = jnp.exp2(m_prev - m_new)
        p = jnp.exp2((s - jnp.tile(m_new, (1, width // LANES))).astype(BF16))
        pv = jnp.dot(p, vc, preferred_element_type=F32)
        l_ref[rows, :] = alpha * l_ref[rows, :] + pv[:, LANES:]
        acc_ref[rows, :] = alpha * acc_ref[rows, :] + pv[:, :LANES]
        m_ref[rows, :] = m_new

    def chunk(off, width, masked):
        s_next = scores(off, width, 0)
        for c in range(n_chains):
            s = s_next
            if c + 1 < n_chains:
                s_next = scores(off, width, c + 1)
            update(off, width, c, s, masked)

    wide = 2 * tq

    def body(j, carry):
        chunk(pl.multiple_of(j * wide, wide), wide, False)
        return carry

    lax.fori_loop(0, qi // 2, body, 0)
    odd = qi % 2 == 1

    @pl.when(odd)
    def _():
        chunk(pl.multiple_of((qi - 1) * tq, tq), wide, True)

    @pl.when(jnp.logical_not(odd))
    def _():
        chunk(pl.multiple_of(qi * tq, tq), tq, True)

    lp = lamp_ref[...]
    lam = (jnp.exp(jnp.sum(lp[0:1] * lp[1:2], axis=-1, keepdims=True))
           - jnp.exp(jnp.sum(lp[2:3] * lp[3:4], axis=-1, keepdims=True)) + lam_init)
    for h in range(n_heads):
        rows = pl.ds(2 * h * tq, 2 * tq)
        o_all = acc_ref[rows, :] / l_ref[rows, :]
        o = o_all[0:tq] - lam * o_all[tq:2 * tq]
        ms = jnp.mean(o * o, axis=-1, keepdims=True)
        o_ref[:, h * LANES:(h + 1) * LANES] = (o * lax.rsqrt(ms + EPS) * sg_ref[...]
                                               * (1.0 - lam_init)).astype(BF16)


def _attention(q, k, v, lamp, sg, *, batch, lp_len, tq, lam_init):
    n, qw = q.shape
    nq = lp_len // tq
    hw = ATTN_HEADS_PER_STEP * LANES
    chains = 2 * ATTN_HEADS_PER_STEP
    kern = functools.partial(_attn_kernel, lam_init=lam_init)
    return pl.pallas_call(
        kern,
        grid=(batch, qw // hw, nq),
        in_specs=[
            pl.BlockSpec((tq, hw), lambda b, h, i: (b * nq + i, h)),
            pl.BlockSpec((lp_len, hw), lambda b, h, i: (b, h)),
            pl.BlockSpec((lp_len, hw), lambda b, h, i: (b, h)),
            pl.BlockSpec(lamp.shape, lambda b, h, i: (0, 0)),
            pl.BlockSpec(sg.shape, lambda b, h, i: (0, 0)),
        ],
        out_specs=pl.BlockSpec((tq, hw), lambda b, h, i: (b * nq + i, h)),
        out_shape=jax.ShapeDtypeStruct((n, v.shape[1]), BF16),
        scratch_shapes=[
            pltpu.VMEM((chains * tq, LANES), BF16),
            pltpu.VMEM((chains * tq, LANES), F32),
            pltpu.VMEM((chains * tq, LANES), F32),
            pltpu.VMEM((chains * tq, LANES), F32),
        ],
        compiler_params=_cparams(3),
        name="diffattn",
    )(q, k, v, lamp, sg)


def _outproj_kernel(hp_ref, cy_ref, o_ref, wout_ref, g2_ref, wrh_ref, wrl_ref, br_ref, upper_ref,
                    h1_ref, xp0_ref, xp1_ref, xp2_ref, xp3_ref, route_ref, cnt_ref, run_ref,
                    *, tiles_per_seq, seq_len):
    i = pl.program_id(0)
    tm = hp_ref.shape[0]

    @pl.when(i == 0)
    def _():
        run_ref[...] = jnp.zeros_like(run_ref)

    mix = jnp.concatenate([cy_ref[...], o_ref[...]], axis=1)
    h1 = hp_ref[...] + jnp.dot(mix, wout_ref[...], preferred_element_type=F32)
    h1_ref[...] = h1
    ms = jnp.mean(h1 * h1, axis=-1, keepdims=True)
    xn = h1 * lax.rsqrt(ms + EPS) * g2_ref[...]
    xw = _pack_rows(xn)
    for c, ref in enumerate((xp0_ref, xp1_ref, xp2_ref, xp3_ref)):
        ref[...] = xw[:, c * LANES:(c + 1) * LANES]

    x_hi = xn.astype(BF16)
    x_lo = (xn - x_hi.astype(F32)).astype(BF16)
    hi_both = jnp.dot(x_hi, jnp.concatenate([wrh_ref[...], wrl_ref[...]], axis=1), preferred_element_type=F32)
    logits = (hi_both[:, :LANES] + hi_both[:, LANES:]
              + jnp.dot(x_lo, wrh_ref[...], preferred_element_type=F32) + br_ref[...])

    lt = logits.T[0:ROUTE_ROWS, :]
    row = lax.broadcasted_iota(jnp.int32, lt.shape, 0)
    big = jnp.int32(4 * LANES)

    def first_argmax(vals, vmax):
        return jnp.min(jnp.where(vals == vmax, row, big), axis=0, keepdims=True)

    gl = jnp.where(row < N_GROUPS, lt, NEG_BIG)
    gmax = jnp.max(gl, axis=0, keepdims=True)
    g_val = 1.0 / jnp.sum(jnp.exp(gl - gmax), axis=0, keepdims=True)
    g_idx = first_argmax(gl, gmax)
    lo = EXPERT_LANE0 + EXPERTS_PER_GROUP * g_idx
    el = jnp.where((row >= lo) & (row < lo + EXPERTS_PER_GROUP), lt, NEG_BIG)
    m1 = jnp.max(el, axis=0, keepdims=True)
    i1 = first_argmax(el, m1)
    el2 = jnp.where(row == i1, NEG_BIG, el)
    m2 = jnp.max(el2, axis=0, keepdims=True)
    i2 = first_argmax(el2, m2)
    r = jnp.exp(m2 - m1)
    gate1 = g_val / (1.0 + r)
    gate2 = g_val * r / (1.0 + r)

    pos = (i % tiles_per_seq) * tm + lax.broadcasted_iota(jnp.int32, (1, tm), 1)
    valid = pos < seq_len
    oh1 = jnp.where(valid & (row == i1), 1.0, 0.0)
    oh2 = jnp.where(valid & (row == i2), 1.0, 0.0)
    pre = jnp.dot(jnp.concatenate([oh1, oh2], axis=0).astype(BF16), upper_ref[...], preferred_element_type=F32)
    tot1 = jnp.sum(oh1, axis=1, keepdims=True)
    tot2 = jnp.sum(oh2, axis=1, keepdims=True)
    run = run_ref[...]
    run_t = jnp.tile(run, (1, tm // LANES))
    rank1 = jnp.sum(oh1 * (pre[:ROUTE_ROWS] + run_t), axis=0, keepdims=True)
    rank2 = jnp.sum(oh2 * (pre[ROUTE_ROWS:] + run_t + tot1), axis=0, keepdims=True)
    new_run = run + tot1 + tot2
    run_ref[...] = new_run
    cnt_ref[...] = new_run

    e1 = (i1 - EXPERT_LANE0).astype(F32)
    e2 = (i2 - EXPERT_LANE0).astype(F32)
    r8 = lax.broadcasted_iota(jnp.int32, (ROUTE_COLS, tm), 0)
    route_ref[...] = jnp.where(r8 == 0, e1, jnp.where(r8 == 1, e2, jnp.where(r8 == 2, gate1, jnp.where(
        r8 == 3, gate2, jnp.where(r8 == 4, rank1, jnp.where(r8 == 5, rank2, 0.0))))))


def _outproj(hp, convy, o, w_out, g2, wr_hi, wr_lo, br, upper, *, tiles_per_seq, seq_len, tm):
    n, d = hp.shape
    const = lambda i: (0, 0)
    tile = lambda i: (i, 0)
    kern = functools.partial(_outproj_kernel, tiles_per_seq=tiles_per_seq, seq_len=seq_len)
    return pl.pallas_call(
        kern,
        grid=(n // tm,),
        in_specs=[
            pl.BlockSpec((tm, d), tile),
            pl.BlockSpec((tm, convy.shape[1]), tile),
            pl.BlockSpec((tm, o.shape[1]), tile),
            pl.BlockSpec(w_out.shape, const),
            pl.BlockSpec((1, d), const),
            pl.BlockSpec(wr_hi.shape, const),
            pl.BlockSpec(wr_lo.shape, const),
            pl.BlockSpec((1, LANES), const),
            pl.BlockSpec(upper.shape, const),
        ],
        out_specs=[
            pl.BlockSpec((tm, d), tile),
            *[pl.BlockSpec((tm, LANES), tile)] * ROW_PLANES,
            pl.BlockSpec((ROUTE_COLS, tm), lambda i: (0, i)),
            pl.BlockSpec((ROUTE_ROWS, LANES), const),
        ],
        out_shape=[
            jax.ShapeDtypeStruct((n, d), F32),
            *[jax.ShapeDtypeStruct((n, LANES), jnp.uint32)] * ROW_PLANES,
            jax.ShapeDtypeStruct((ROUTE_COLS, n), F32),
            jax.ShapeDtypeStruct((ROUTE_ROWS, LANES), F32),
        ],
        scratch_shapes=[pltpu.VMEM((ROUTE_ROWS, LANES), F32)],
        compiler_params=_cparams(1),
        name="outproj_router",
    )(hp, convy, o, w_out, g2, wr_hi, wr_lo, br, upper)


def _slots_kernel(pstart_ref, route_ref, dest_ref, *, seq_len, p_rows):
    b = pl.program_id(0)
    route = route_ref[...]
    eid = route.astype(jnp.int32)
    start = jnp.zeros_like(eid)
    for e in range(N_EXPERTS):
        start = jnp.where(eid == e, pstart_ref[e], start)
    rank = pltpu.roll(route, ROUTE_COLS - 2 * TOP_K, axis=0).astype(jnp.int32)
    k = lax.broadcasted_iota(jnp.int32, route.shape, 0)
    pos = lax.broadcasted_iota(jnp.int32, route.shape, 1)
    n_pad = route.shape[1] - seq_len
    spare = p_rows + (b * n_pad + (pos - seq_len)) * TOP_K + k
    dest_ref[...] = jnp.where(pos < seq_len, start + rank, spare)


def _slots(pstarts, route, *, batch, lp_len, seq_len, p_rows):
    kern = functools.partial(_slots_kernel, seq_len=seq_len, p_rows=p_rows)
    grid_spec = pltpu.PrefetchScalarGridSpec(
        num_scalar_prefetch=1,
        grid=(batch,),
        in_specs=[pl.BlockSpec((ROUTE_COLS, lp_len), lambda b, ps: (0, b))],
        out_specs=pl.BlockSpec((ROUTE_COLS, lp_len), lambda b, ps: (0, b)),
    )
    return pl.pallas_call(
        kern,
        grid_spec=grid_spec,
        out_shape=jax.ShapeDtypeStruct(route.shape, jnp.int32),
        compiler_params=_cparams(1),
        name="moe_slots",
    )(pstarts, route)


def _sc_workers():
    info = plsc.get_sparse_core_info()
    return info.num_cores, info.num_cores * info.num_subcores


def _sc_scatter_rows(planes, idx_a, idx_b, out_rows):
    n_win = idx_a.shape[0]
    n_cores, n_workers = _sc_workers()
    trips = -(-n_win // n_workers)
    mesh = plsc.VectorSubcoreMesh(core_axis_name="c", subcore_axis_name="s")

    def body(*refs):
        xs = refs[0:ROW_PLANES]
        ia_hbm, ib_hbm = refs[ROW_PLANES:ROW_PLANES + 2]
        outs = refs[ROW_PLANES + 2:2 * ROW_PLANES + 2]
        ia_v, ib_v, buf, sem = refs[2 * ROW_PLANES + 2:]
        wid = lax.axis_index("s") * n_cores + lax.axis_index("c")

        def step(t, carry):
            g = wid + t * n_workers

            @pl.when(g < n_win)
            def _():
                row0 = pl.multiple_of(g * SC_WINDOW, SC_WINDOW)
                loads = [pltpu.async_copy(ia_hbm.at[g], ia_v, sem), pltpu.async_copy(ib_hbm.at[g], ib_v, sem)]
                loads += [pltpu.async_copy(xs[c].at[pl.ds(row0, SC_WINDOW)], buf.at[c], sem)
                          for c in range(ROW_PLANES)]
                for cp in loads:
                    cp.wait()
                stores = [pltpu.async_copy(buf.at[c], outs[c].at[iv], sem)
                          for c in range(ROW_PLANES) for iv in (ia_v, ib_v)]
                for cp in stores:
                    cp.wait()

            return carry

        lax.fori_loop(0, trips, step, 0)

    kern = pl.kernel(
        body,
        out_type=[jax.ShapeDtypeStruct((out_rows, LANES), jnp.uint32)] * ROW_PLANES,
        mesh=mesh,
        scratch_types=[
            pltpu.VMEM((SC_WINDOW,), jnp.int32),
            pltpu.VMEM((SC_WINDOW,), jnp.int32),
            pltpu.VMEM((ROW_PLANES, SC_WINDOW, LANES), jnp.uint32),
            pltpu.SemaphoreType.DMA,
        ],
        name="moe_dispatch_sc",
    )
    return kern(*planes, idx_a, idx_b)


def _sc_gather_rows(planes, idx_a, idx_b):
    n_win = idx_a.shape[0]
    n_cores, n_workers = _sc_workers()
    trips = -(-n_win // n_workers)
    mesh = plsc.VectorSubcoreMesh(core_axis_name="c", subcore_axis_name="s")

    def body(*refs):
        ys = refs[0:ROW_PLANES]
        ia_hbm, ib_hbm = refs[ROW_PLANES:ROW_PLANES + 2]
        outs_a = refs[ROW_PLANES + 2:2 * ROW_PLANES + 2]
        outs_b = refs[2 * ROW_PLANES + 2:3 * ROW_PLANES + 2]
        ia_v, ib_v, buf, sem = refs[3 * ROW_PLANES + 2:]
        wid = lax.axis_index("s") * n_cores + lax.axis_index("c")

        def step(t, carry):
            g = wid + t * n_workers

            @pl.when(g < n_win)
            def _():
                row0 = pl.multiple_of(g * SC_WINDOW, SC_WINDOW)
                idx_loads = [pltpu.async_copy(ia_hbm.at[g], ia_v, sem), pltpu.async_copy(ib_hbm.at[g], ib_v, sem)]
                for cp in idx_loads:
                    cp.wait()
                for iv, outs in ((ia_v, outs_a), (ib_v, outs_b)):
                    loads = [pltpu.async_copy(ys[c].at[iv], buf.at[c], sem) for c in range(ROW_PLANES)]
                    for cp in loads:
                        cp.wait()
                    stores = [pltpu.async_copy(buf.at[c], outs[c].at[pl.ds(row0, SC_WINDOW)], sem)
                              for c in range(ROW_PLANES)]
                    for cp in stores:
                        cp.wait()

            return carry

        lax.fori_loop(0, trips, step, 0)

    n_rows = n_win * SC_WINDOW
    kern = pl.kernel(
        body,
        out_type=[jax.ShapeDtypeStruct((n_rows, LANES), jnp.uint32)] * (2 * ROW_PLANES),
        mesh=mesh,
        scratch_types=[
            pltpu.VMEM((SC_WINDOW,), jnp.int32),
            pltpu.VMEM((SC_WINDOW,), jnp.int32),
            pltpu.VMEM((ROW_PLANES, SC_WINDOW, LANES), jnp.uint32),
            pltpu.SemaphoreType.DMA,
        ],
        name="moe_gather_sc",
    )
    res = kern(*planes, idx_a, idx_b)
    return res[:ROW_PLANES], res[ROW_PLANES:]


def _experts_kernel(be_ref, nv_ref, first_ref, slot_ref, nxt_ref, x0_ref, x1_ref, x2_ref, x3_ref,
                    wg_hbm, wu_hbm, wd_hbm, y0_ref, y1_ref, y2_ref, y3_ref,
                    wgf_ref, wuf_ref, wdf_ref, wgb_ref, wub_ref, wdb_ref, sem):
    i = pl.program_id(0)
    e = be_ref[i]
    slot = slot_ref[i]
    y_refs = (y0_ref, y1_ref, y2_ref, y3_ref)

    def weight_copies(expert, s):
        return [pltpu.make_async_copy(hbm.at[expert], stage.at[s], sem.at[s, j])
                for j, (hbm, stage) in enumerate(((wg_hbm, wgf_ref), (wu_hbm, wuf_ref), (wd_hbm, wdf_ref)))]

    @pl.when(i == 0)
    def _():
        for cp in weight_copies(e, slot):
            cp.start()

    @pl.when(first_ref[i] == 1)
    def _():
        for cp in weight_copies(e, slot):
            cp.wait()
        nxt = nxt_ref[i]

        @pl.when(nxt >= 0)
        def _():
            for cp in weight_copies(nxt, 1 - slot):
                cp.start()

        wgb_ref[...] = wgf_ref[slot].astype(BF16)
        wub_ref[...] = wuf_ref[slot].astype(BF16)
        wdb_ref[...] = wdf_ref[slot].astype(BF16)

    nv = nv_ref[i]

    def mlp(rows):
        xs = _unpack_rows([r[rows, :] for r in (x0_ref, x1_ref, x2_ref, x3_ref)])
        row = lax.broadcasted_iota(jnp.int32, xs.shape, 0)
        x = jnp.where(row < nv, xs, jnp.zeros_like(xs))
        hg = jnp.dot(x, wgb_ref[...], preferred_element_type=F32)
        hu = jnp.dot(x, wub_ref[...], preferred_element_type=F32)
        hid = (hg / (1.0 + jnp.exp(-hg)) * hu).astype(BF16)
        yw = _pack_rows(jnp.dot(hid, wdb_ref[...], preferred_element_type=F32))
        for c, ref in enumerate(y_refs):
            ref[rows, :] = yw[:, c * LANES:(c + 1) * LANES]

    def clear(rows):
        for ref in y_refs:
            ref[rows, :] = jnp.zeros((rows.size, LANES), ref.dtype)

    for piece in range(MOE_BLOCK // MOE_TAIL_STEP + 1):
        used = piece * MOE_TAIL_STEP

        @pl.when((nv > used - MOE_TAIL_STEP) & (nv <= used))
        def _():
            if used > 0:
                mlp(pl.ds(0, used))
            if used < MOE_BLOCK:
                clear(pl.ds(used, MOE_BLOCK - used))


def _experts(block_e, nvalid, xs_planes, w_gate, w_up, w_down):
    n_blocks = block_e.shape[0]
    _, d, ff = w_gate.shape
    first = jnp.concatenate([jnp.ones((1,), jnp.int32), (block_e[1:] != block_e[:-1]).astype(jnp.int32)])
    slot = (jnp.cumsum(first) - 1) % 2
    later = jnp.where(block_e[None, :] > block_e[:, None], block_e[None, :], N_EXPERTS)
    nxt = jnp.min(later, axis=1)
    nxt = jnp.where(nxt == N_EXPERTS, -1, nxt).astype(jnp.int32)
    blk = lambda i, *_: (i, 0)
    grid_spec = pltpu.PrefetchScalarGridSpec(
        num_scalar_prefetch=5,
        grid=(n_blocks,),
        in_specs=[
            *[pl.BlockSpec((MOE_BLOCK, LANES), blk)] * ROW_PLANES,
            pl.BlockSpec(memory_space=pl.ANY),
            pl.BlockSpec(memory_space=pl.ANY),
            pl.BlockSpec(memory_space=pl.ANY),
        ],
        out_specs=[pl.BlockSpec((MOE_BLOCK, LANES), blk)] * ROW_PLANES,
        scratch_shapes=[
            pltpu.VMEM((2, d, ff), F32),
            pltpu.VMEM((2, d, ff), F32),
            pltpu.VMEM((2, ff, d), F32),
            pltpu.VMEM((d, ff), BF16),
            pltpu.VMEM((d, ff), BF16),
            pltpu.VMEM((ff, d), BF16),
            pltpu.SemaphoreType.DMA((2, 3)),
        ],
    )
    return pl.pallas_call(
        _experts_kernel,
        grid_spec=grid_spec,
        out_shape=[jax.ShapeDtypeStruct((n_blocks * MOE_BLOCK, LANES), jnp.uint32)] * ROW_PLANES,
        compiler_params=_cparams(1),
        name="moe_experts",
    )(block_e, nvalid, first, slot.astype(jnp.int32), nxt, *xs_planes, w_gate, w_up, w_down)


def _combine_kernel(gates_ref, h1_hbm, *refs, lp_len, tile):
    a_refs = refs[0:ROW_PLANES]
    b_refs = refs[ROW_PLANES:2 * ROW_PLANES]
    out_ref, hbuf, sem_h = refs[2 * ROW_PLANES:]
    nt = pl.num_programs(1)
    step = pl.program_id(0) * nt + pl.program_id(1)
    last = pl.num_programs(0) * nt - 1

    def h_copy(s, slot):
        start = (s // nt) * lp_len + N_META + (s % nt) * tile
        return pltpu.make_async_copy(h1_hbm.at[pl.ds(start, tile), :], hbuf.at[slot], sem_h.at[slot])

    slot = step % 2

    @pl.when(step == 0)
    def _():
        h_copy(step, slot).start()

    @pl.when(step < last)
    def _():
        h_copy(step + 1, 1 - slot).start()

    ya = _unpack_rows([r[...] for r in a_refs]).astype(F32)
    yb = _unpack_rows([r[...] for r in b_refs]).astype(F32)
    g = jnp.concatenate([gates_ref[...]] * (LANES // ROUTE_COLS), axis=0).T
    moe = g[:, 0:1] * ya + g[:, 1:2] * yb
    h_copy(step, slot).wait()
    out_ref[0] = hbuf[slot] + moe


def _combine(gates, h1, a_planes, b_planes, *, batch, seq, lp_len, tile):
    d = h1.shape[1]
    nt = seq // tile
    kern = functools.partial(_combine_kernel, lp_len=lp_len, tile=tile)
    rows = lambda b, i: (b * nt + i, 0)
    return pl.pallas_call(
        kern,
        grid=(batch, nt),
        in_specs=[
            pl.BlockSpec((ROUTE_COLS, tile), lambda b, i: (0, b * nt + i)),
            pl.BlockSpec(memory_space=pl.ANY),
            *[pl.BlockSpec((tile, LANES), rows)] * (2 * ROW_PLANES),
        ],
        out_specs=pl.BlockSpec((1, tile, d), lambda b, i: (b, i, 0)),
        out_shape=jax.ShapeDtypeStruct((batch, seq, d), F32),
        scratch_shapes=[pltpu.VMEM((2, tile, d), F32), pltpu.SemaphoreType.DMA((2,))],
        compiler_params=_cparams(2),
        name="moe_combine",
    )(gates, h1, *a_planes, *b_planes)


def _rope_tables(length, lp_len):
    half = ROPE_DIM // 2
    pos = jnp.arange(length, dtype=F32)
    inv_freq = ROPE_THETA ** (-jnp.arange(0, ROPE_DIM, 2, dtype=F32) / ROPE_DIM)
    ang = pos[:, None] * inv_freq[None, :]
    cos = jnp.cos(ang)
    sin = jnp.sin(ang)
    ones = jnp.ones((length, HEAD_DIM - ROPE_DIM), F32)
    zeros_h = jnp.zeros((length, half), F32)
    zeros_r = jnp.zeros((length, HEAD_DIM - ROPE_DIM), F32)
    c = jnp.concatenate([cos, cos, ones], axis=1)
    s1 = jnp.concatenate([zeros_h, sin, zeros_r], axis=1)
    s2 = jnp.concatenate([-sin, zeros_h, zeros_r], axis=1)
    pad = ((0, lp_len - length), (0, 0))
    rep = LANES // HEAD_DIM
    return tuple(jnp.pad(jnp.tile(t, (1, rep)), pad) for t in (c, s1, s2))


def kernel(x, meta_tokens, norm1_g, w_in, conv_w, q_norm_g, k_norm_g, lambda_q1, lambda_k1, lambda_q2, lambda_k2,
           subln_g, w_out, norm2_g, w_router_group, b_router_group, w_router_expert, b_router_expert, w_gate,
           w_up, w_down):
    batch, seq, _ = x.shape
    assert w_in.shape[0] == 1, "a single layer is supported"
    l = 0
    length = seq + N_META
    tm = TOKEN_TILE
    lp_len = -(-length // tm) * tm
    tiles_per_seq = lp_len // tm
    assert tiles_per_seq >= 2 and (length - (tiles_per_seq - 1) * tm) % 8 == 0
    qw = N_HEADS * 2 * HEAD_DIM
    lam_init = 0.8 - 0.6 * math.exp(-0.3 * l)

    reps = qw // HEAD_DIM
    gq = jnp.tile(q_norm_g[l] * (HEAD_DIM ** -0.5 * LOG2E), reps)[None, :]
    gk = jnp.tile(k_norm_g[l], reps)[None, :]
    seg = jnp.arange(qw) // HEAD_DIM
    bd = (seg[:, None] == seg[None, :]).astype(BF16)
    rope = _rope_tables(length, lp_len)
    hp, convy, q, k, v = _inproj(x, meta_tokens.astype(x.dtype), norm1_g[l][None, :], w_in[l].astype(BF16),
                                 conv_w[l], gq, gk, bd, *rope, tiles_per_seq=tiles_per_seq, tm=tm)

    lamp = jnp.stack([lambda_q1[l], lambda_k1[l], lambda_q2[l], lambda_k2[l]]).astype(F32)
    o = _attention(q, k, v, lamp, subln_g[l][None, :], batch=batch, lp_len=lp_len, tq=tm, lam_init=lam_init)

    lane_pad = LANES - N_GROUPS - N_EXPERTS
    wr = jnp.pad(jnp.concatenate([w_router_group[l], w_router_expert[l]], axis=1), ((0, 0), (0, lane_pad)))
    wr_hi = wr.astype(BF16)
    wr_lo = (wr - wr_hi.astype(F32)).astype(BF16)
    br = jnp.pad(jnp.concatenate([b_router_group[l], b_router_expert[l]]), (0, lane_pad))[None, :]
    ridx = jnp.arange(tm)
    upper = (ridx[:, None] < ridx[None, :]).astype(BF16)
    h1, *rest = _outproj(hp, convy, o, w_out[l].astype(BF16), norm2_g[l][None, :], wr_hi, wr_lo, br, upper,
                         tiles_per_seq=tiles_per_seq, seq_len=length, tm=tm)
    x_planes = rest[:ROW_PLANES]
    route, cnt = rest[ROW_PLANES:]

    counts = cnt[EXPERT_LANE0:EXPERT_LANE0 + N_EXPERTS, 0].astype(jnp.int32)
    n_blocks = -(-(batch * length * TOP_K) // MOE_BLOCK) + N_EXPERTS
    p_rows = n_blocks * MOE_BLOCK
    padded = (counts + MOE_BLOCK - 1) // MOE_BLOCK * MOE_BLOCK
    pends = jnp.cumsum(padded)
    pstarts = pends - padded

    def lookup(table, idx):
        hit = idx[:, None] == jnp.arange(N_EXPERTS, dtype=jnp.int32)[None, :]
        return jnp.sum(jnp.where(hit, table[None, :], 0), axis=1)

    blk0 = jnp.arange(n_blocks, dtype=jnp.int32) * MOE_BLOCK
    block_e = jnp.minimum(jnp.sum((pends[None, :] <= blk0[:, None]).astype(jnp.int32), axis=1), N_EXPERTS - 1)
    nvalid = jnp.clip(lookup(counts, block_e) - (blk0 - lookup(pstarts, block_e)), 0, MOE_BLOCK)

    spare_rows = -(-(batch * (lp_len - length) * TOP_K) // MOE_BLOCK) * MOE_BLOCK
    dest = _slots(pstarts.astype(jnp.int32), route, batch=batch, lp_len=lp_len, seq_len=length, p_rows=p_rows)
    dest = dest.reshape(ROUTE_COLS, batch, lp_len)
    gates = route.reshape(ROUTE_COLS, batch, lp_len)[TOP_K:2 * TOP_K]

    assert (batch * lp_len) % SC_WINDOW == 0 and (batch * seq) % SC_WINDOW == 0
    xs_planes = _sc_scatter_rows(x_planes, dest[0].reshape(-1, SC_WINDOW), dest[1].reshape(-1, SC_WINDOW),
                                 p_rows + spare_rows)
    y_planes = _experts(block_e, nvalid, xs_planes, w_gate[l], w_up[l], w_down[l])

    dest_x = dest[0:TOP_K, :, N_META:length]
    a_planes, b_planes = _sc_gather_rows(y_planes, dest_x[0].reshape(-1, SC_WINDOW),
                                         dest_x[1].reshape(-1, SC_WINDOW))
    gates_x = jnp.pad(gates[:, :, N_META:length].reshape(TOP_K, batch * seq), ((0, ROUTE_COLS - TOP_K), (0, 0)))
    return _combine(gates_x, h1, a_planes, b_planes, batch=batch, seq=seq, lp_len=lp_len,
                    tile=_largest_tile(seq, 512, LANES))
```

```python
import functools
import math

import jax
import jax.numpy as jnp
from jax import lax
from jax.experimental import pallas as pl
from jax.experimental.pallas import tpu as pltpu
from jax.experimental.pallas import tpu_sc as plsc

F32 = jnp.float32
BF16 = jnp.bfloat16

N_META = 16
N_HEADS = 4
HEAD_DIM = 64
ROPE_DIM = HEAD_DIM // 4
ROPE_THETA = 500000.0
N_GROUPS = 4
EXPERTS_PER_GROUP = 8
N_EXPERTS = N_GROUPS * EXPERTS_PER_GROUP
TOP_K = 2
EPS = 1e-6
LOG2E = 1.4426950408889634

LANES = 128
TOKEN_TILE = 640
INPROJ_UNITS = 2
OUTPROJ_UNITS = 2
ATTN_HEADS_PER_STEP = 4
MOE_BLOCK = 1024
MOE_TAIL_STEP = 256
ROUTE_COLS = 8
ROUTE_ROWS = 64
ROW_PLANES = 4
SC_WINDOW = 128
EXPERT_LANE0 = N_GROUPS
NEG_BIG = -1e30
VMEM_LIMIT = 56 * 1024 * 1024


def _largest_tile(n, cap, mult):
    for t in range(min(cap, n), 0, -1):
        if n % t == 0 and t % mult == 0:
            return t
    raise ValueError(f"no tile for {n}")


def _cparams(n_axes):
    return pltpu.CompilerParams(dimension_semantics=("arbitrary",) * n_axes, vmem_limit_bytes=VMEM_LIMIT)


def _pack_rows(x):
    w = x.shape[1] // 2
    lo = lax.bitcast_convert_type(x[:, :w].astype(BF16).astype(F32), jnp.uint32)
    hi = lax.bitcast_convert_type(x[:, w:].astype(BF16).astype(F32), jnp.uint32)
    return lax.shift_right_logical(lo, jnp.uint32(16)) | (hi & jnp.uint32(0xFFFF0000))


def _unpack_rows(planes):
    w = jnp.concatenate(planes, axis=1)
    lo = lax.bitcast_convert_type(lax.shift_left(w, jnp.uint32(16)), F32)
    hi = lax.bitcast_convert_type(w & jnp.uint32(0xFFFF0000), F32)
    return jnp.concatenate([lo, hi], axis=1).astype(BF16)


def _inproj_kernel(x_hbm, meta_hbm, g1_ref, win_ref, convw_ref, gq_ref, gk_ref, bd_ref, *refs,
                   tiles_per_seq, seq, cw, qw, units):
    rope_refs = refs[0:3 * units]
    hp_hbm, convy_ref, q_ref, k_ref, v_ref, carry_ref, xbuf, sem, sem_out = refs[3 * units:]
    i = pl.program_id(0)
    n_steps = pl.num_programs(0)
    tm = xbuf.shape[1]
    q0 = 3 * cw
    w = convw_ref[...]
    last_rows = seq + N_META - (tiles_per_seq - 1) * tm

    def fetch(tile, slot, start):
        b = tile // tiles_per_seq
        t = tile % tiles_per_seq

        def go(src, dst):
            cp = pltpu.make_async_copy(src, dst, sem.at[slot])
            if start:
                cp.start()
            else:
                cp.wait()

        @pl.when(t == 0)
        def _():
            go(meta_hbm, xbuf.at[slot, pl.ds(0, N_META)])
            go(x_hbm.at[pl.ds(b * seq, tm - N_META)], xbuf.at[slot, pl.ds(N_META, tm - N_META)])

        @pl.when((t > 0) & (t < tiles_per_seq - 1))
        def _():
            go(x_hbm.at[pl.ds(b * seq + t * tm - N_META, tm)], xbuf.at[slot])

        @pl.when(t == tiles_per_seq - 1)
        def _():
            go(x_hbm.at[pl.ds(b * seq + t * tm - N_META, last_rows)], xbuf.at[slot, pl.ds(0, last_rows)])

    def hp_store(tile, slot):
        return pltpu.make_async_copy(xbuf.at[slot], hp_hbm.at[pl.ds(tile * tm, tm)], sem_out.at[slot])

    mine = (i % 2) * units
    other = units - mine

    @pl.when(i == 0)
    def _():
        for u in range(units):
            fetch(u, mine + u, True)

    @pl.when(i > 0)
    def _():
        for u in range(units):
            hp_store((i - 1) * units + u, other + u).wait()

    @pl.when(i + 1 < n_steps)
    def _():
        for u in range(units):
            fetch((i + 1) * units + u, other + u, True)

    for u in range(units):
        tile = i * units + u
        fetch(tile, mine + u, False)

        @pl.when(tile % tiles_per_seq == tiles_per_seq - 1)
        def _():
            xbuf[mine + u, pl.ds(last_rows, tm - last_rows), :] = jnp.zeros((tm - last_rows, xbuf.shape[2]),
                                                                           xbuf.dtype)

        hp_store(tile, mine + u).start()

    prev = carry_ref[...]

    for u in range(units):
        rows = pl.ds(u * tm, tm)
        x = xbuf[mine + u]
        ms = jnp.mean(x * x, axis=-1, keepdims=True)
        xn = (x * lax.rsqrt(ms + EPS) * g1_ref[...]).astype(BF16)

        def proj(lo, hi):
            return jnp.dot(xn, win_ref[:, lo:hi], preferred_element_type=F32)

        u_conv = proj(0, q0)
        u_q = proj(q0, q0 + qw)

        z = u_conv[:, cw:2 * cw] * u_conv[:, 2 * cw:3 * cw]
        prev = jnp.where((i * units + u) % tiles_per_seq == 0, 0.0, prev)
        p1 = prev[7:8]
        p2 = prev[6:7]
        row = lax.broadcasted_iota(jnp.int32, z.shape, 0)
        z1 = jnp.where(row == 0, p1, pltpu.roll(z, 1, axis=0))
        z2 = jnp.where(row == 0, p2, jnp.where(row == 1, p1, pltpu.roll(z, 2, axis=0)))
        prev = z[tm - 8:tm]
        conv = w[0:1] * z2 + w[1:2] * z1 + w[2:3] * z
        convy_ref[rows, :] = (u_conv[:, 0:cw] * conv).astype(BF16)

        rc, rs1, rs2 = (ref[...] for ref in rope_refs[3 * u:3 * u + 3])

        def norm_rope(t, g_ref):
            ss = jnp.dot((t * t).astype(BF16), bd_ref[...], preferred_element_type=F32)
            tn = t * lax.rsqrt(ss * (1.0 / HEAD_DIM) + EPS) * g_ref[...]
            outs = []
            for c in range(qw // LANES):
                ch = tn[:, c * LANES:(c + 1) * LANES]
                outs.append(ch * rc + pltpu.roll(ch, ROPE_DIM // 2, axis=1) * rs1
                            + pltpu.roll(ch, LANES - ROPE_DIM // 2, axis=1) * rs2)
            return jnp.concatenate(outs, axis=1).astype(BF16)

        u_k = proj(q0 + qw, q0 + 2 * qw)
        q_ref[rows, :] = norm_rope(u_q, gq_ref)
        u_v = proj(q0 + 2 * qw, win_ref.shape[1])
        k_ref[rows, :] = norm_rope(u_k, gk_ref)
        v_ref[rows, :] = u_v.astype(BF16)

    carry_ref[...] = prev

    @pl.when(i == n_steps - 1)
    def _():
        for u in range(units):
            hp_store(i * units + u, mine + u).wait()


def _inproj(x, meta, g1, w_in, conv_w, gq, gk, bd, rc, rs1, rs2, *, tiles_per_seq, tm):
    batch, seq, d = x.shape
    units = INPROJ_UNITS
    n = batch * tiles_per_seq * tm
    assert (batch * tiles_per_seq) % units == 0
    cw = conv_w.shape[1]
    qw = gq.shape[1]
    aw = w_in.shape[1] - 3 * cw - 2 * qw
    const = lambda i: (0, 0)
    step = lambda i: (i, 0)
    rope_specs = [pl.BlockSpec((tm, LANES), lambda i, u=u: ((i * units + u) % tiles_per_seq, 0))
                  for u in range(units) for _ in range(3)]
    kern = functools.partial(_inproj_kernel, tiles_per_seq=tiles_per_seq, seq=seq, cw=cw, qw=qw, units=units)
    return pl.pallas_call(
        kern,
        grid=(n // (units * tm),),
        in_specs=[
            pl.BlockSpec(memory_space=pl.ANY),
            pl.BlockSpec(memory_space=pl.ANY),
            pl.BlockSpec((1, d), const),
            pl.BlockSpec(w_in.shape, const),
            pl.BlockSpec(conv_w.shape, const),
            pl.BlockSpec((1, qw), const),
            pl.BlockSpec((1, qw), const),
            pl.BlockSpec(bd.shape, const),
            *rope_specs,
        ],
        out_specs=[
            pl.BlockSpec(memory_space=pl.ANY),
            pl.BlockSpec((units * tm, cw), step),
            pl.BlockSpec((units * tm, qw), step),
            pl.BlockSpec((units * tm, qw), step),
            pl.BlockSpec((units * tm, aw), step),
        ],
        out_shape=[
            jax.ShapeDtypeStruct((n, d), x.dtype),
            jax.ShapeDtypeStruct((n, cw), BF16),
            jax.ShapeDtypeStruct((n, qw), BF16),
            jax.ShapeDtypeStruct((n, qw), BF16),
            jax.ShapeDtypeStruct((n, aw), BF16),
        ],
        scratch_shapes=[pltpu.VMEM((8, cw), F32), pltpu.VMEM((2 * units, tm, d), x.dtype),
                        pltpu.SemaphoreType.DMA((2 * units,)), pltpu.SemaphoreType.DMA((2 * units,))],
        compiler_params=_cparams(1),
        name="inproj",
    )(x.reshape(batch * seq, d), meta, g1, w_in, conv_w, gq, gk, bd, *([rc, rs1, rs2] * units))


def _attn_kernel(q_ref, k_ref, v_ref, lamp_ref, sg_ref, o_ref, qs_ref, m_ref, l_ref, acc_ref, *, lam_init):
    qi = pl.program_id(2)
    tq = q_ref.shape[0]
    n_heads = q_ref.shape[1] // LANES
    n_chains = 2 * n_heads
    lane = lax.broadcasted_iota(jnp.int32, (tq, LANES), 1)
    for h in range(n_heads):
        q = q_ref[:, h * LANES:(h + 1) * LANES]
        zero = jnp.zeros_like(q)
        qs_ref[pl.ds(2 * h * tq, tq), :] = jnp.where(lane < HEAD_DIM, q, zero)
        qs_ref[pl.ds((2 * h + 1) * tq, tq), :] = jnp.where(lane >= HEAD_DIM, q, zero)
    m_ref[...] = jnp.full_like(m_ref, NEG_BIG)
    l_ref[...] = jnp.zeros_like(l_ref)
    acc_ref[...] = jnp.zeros_like(acc_ref)

    def scores(off, width, which):
        h = which // 2
        kc = k_ref[pl.ds(off, width), h * LANES:(h + 1) * LANES]
        return lax.dot_general(qs_ref[pl.ds(which * tq, tq), :], kc, (((1,), (1,)), ((), ())),
                               preferred_element_type=F32)

    def update(off, width, which, s, masked):
        h = which // 2
        vc = jnp.concatenate([v_ref[pl.ds(off, width), h * LANES:(h + 1) * LANES],
                              jnp.ones((width, LANES), BF16)], axis=1)
        rows = pl.ds(which * tq, tq)
        if masked:
            r = lax.broadcasted_iota(jnp.int32, s.shape, 0)
            c = lax.broadcasted_iota(jnp.int32, s.shape, 1)
            s = jnp.where(c <= r + (width - tq), s, NEG_BIG)
        m_prev = m_ref[rows, :]
        m_new = jnp.maximum(m_prev, jnp.max(s, axis=-1, keepdims=True))
        alpha = jnp.exp2(m_prev - m_new)
        p = jnp.exp2((s - jnp.tile(m_new, (1, width // LANES))).astype(BF16))
        pv = jnp.dot(p, vc, preferred_element_type=F32)
        l_ref[rows, :] = alpha * l_ref[rows, :] + pv[:, LANES:]
        acc_ref[rows, :] = alpha * acc_ref[rows, :] + pv[:, :LANES]
        m_ref[rows, :] = m_new

    def chunk(off, width, masked):
        s_next = scores(off, width, 0)
        for c in range(n_chains):
            s = s_next
            if c + 1 < n_chains:
                s_next = scores(off, width, c + 1)
            update(off, width, c, s, masked)

    wide = 2 * tq

    def body(j, carry):
        chunk(pl.multiple_of(j * wide, wide), wide, False)
        return carry

    lax.fori_loop(0, qi // 2, body, 0)
    odd = qi % 2 == 1

    @pl.when(odd)
    def _():
        chunk(pl.multiple_of((qi - 1) * tq, tq), wide, True)

    @pl.when(jnp.logical_not(odd))
    def _():
        chunk(pl.multiple_of(qi * tq, tq), tq, True)

    lp = lamp_ref[...]
    lam = (jnp.exp(jnp.sum(lp[0:1] * lp[1:2], axis=-1, keepdims=True))
           - jnp.exp(jnp.sum(lp[2:3] * lp[3:4], axis=-1, keepdims=True)) + lam_init)
    for h in range(n_heads):
        rows = pl.ds(2 * h * tq, 2 * tq)
        o_all = acc_ref[rows, :] / l_ref[rows, :]
        o = o_all[0:tq] - lam * o_all[tq:2 * tq]
        ms = jnp.mean(o * o, axis=-1, keepdims=True)
        o_ref[:, h * LANES:(h + 1) * LANES] = (o * lax.rsqrt(ms + EPS) * sg_ref[...]
                                               * (1.0 - lam_init)).astype(BF16)


def _attention(q, k, v, lamp, sg, *, batch, lp_len, tq, lam_init):
    n, qw = q.shape
    nq = lp_len // tq
    hw = ATTN_HEADS_PER_STEP * LANES
    chains = 2 * ATTN_HEADS_PER_STEP
    kern = functools.partial(_attn_kernel, lam_init=lam_init)
    return pl.pallas_call(
        kern,
        grid=(batch, qw // hw, nq),
        in_specs=[
            pl.BlockSpec((tq, hw), lambda b, h, i: (b * nq + i, h)),
            pl.BlockSpec((lp_len, hw), lambda b, h, i: (b, h)),
            pl.BlockSpec((lp_len, hw), lambda b, h, i: (b, h)),
            pl.BlockSpec(lamp.shape, lambda b, h, i: (0, 0)),
            pl.BlockSpec(sg.shape, lambda b, h, i: (0, 0)),
        ],
        out_specs=pl.BlockSpec((tq, hw), lambda b, h, i: (b * nq + i, h)),
        out_shape=jax.ShapeDtypeStruct((n, v.shape[1]), BF16),
        scratch_shapes=[
            pltpu.VMEM((chains * tq, LANES), BF16),
            pltpu.VMEM((chains * tq, LANES), F32),
            pltpu.VMEM((chains * tq, LANES), F32),
            pltpu.VMEM((chains * tq, LANES), F32),
        ],
        compiler_params=_cparams(3),
        name="diffattn",
    )(q, k, v, lamp, sg)


def _outproj_kernel(hp_ref, cy_ref, o_ref, wout_ref, g2_ref, wrh_ref, wrl_ref, br_ref, upper_ref,
                    h1_ref, xp0_ref, xp1_ref, xp2_ref, xp3_ref, route_ref, cnt_ref, run_ref,
                    *, tiles_per_seq, seq_len):
    i = pl.program_id(0)
    tm = upper_ref.shape[0]
    n_units = hp_ref.shape[0] // tm

    @pl.when(i == 0)
    def _():
        run_ref[...] = jnp.zeros_like(run_ref)

    def project(rows):
        mix = jnp.concatenate([cy_ref[rows, :], o_ref[rows, :]], axis=1)
        h1 = hp_ref[rows, :] + jnp.dot(mix, wout_ref[...], preferred_element_type=F32)
        h1_ref[rows, :] = h1
        ms = jnp.mean(h1 * h1, axis=-1, keepdims=True)
        xn = h1 * lax.rsqrt(ms + EPS) * g2_ref[...]
        xw = _pack_rows(xn)
        for c, ref in enumerate((xp0_ref, xp1_ref, xp2_ref, xp3_ref)):
            ref[rows, :] = xw[:, c * LANES:(c + 1) * LANES]
        x_hi = xn.astype(BF16)
        x_lo = (xn - x_hi.astype(F32)).astype(BF16)
        hi_both = jnp.dot(x_hi, jnp.concatenate([wrh_ref[...], wrl_ref[...]], axis=1),
                          preferred_element_type=F32)
        return (hi_both[:, :LANES] + hi_both[:, LANES:]
                + jnp.dot(x_lo, wrh_ref[...], preferred_element_type=F32) + br_ref[...])

    def route(u, logits, run):
        lt = logits.T[0:ROUTE_ROWS, :]
        row = lax.broadcasted_iota(jnp.int32, lt.shape, 0)
        big = jnp.int32(4 * LANES)

        def first_argmax(vals, vmax):
            return jnp.min(jnp.where(vals == vmax, row, big), axis=0, keepdims=True)

        gl = jnp.where(row < N_GROUPS, lt, NEG_BIG)
        gmax = jnp.max(gl, axis=0, keepdims=True)
        g_val = 1.0 / jnp.sum(jnp.exp(gl - gmax), axis=0, keepdims=True)
        g_idx = first_argmax(gl, gmax)
        lo = EXPERT_LANE0 + EXPERTS_PER_GROUP * g_idx
        el = jnp.where((row >= lo) & (row < lo + EXPERTS_PER_GROUP), lt, NEG_BIG)
        m1 = jnp.max(el, axis=0, keepdims=True)
        i1 = first_argmax(el, m1)
        el2 = jnp.where(row == i1, NEG_BIG, el)
        m2 = jnp.max(el2, axis=0, keepdims=True)
        i2 = first_argmax(el2, m2)
        r = jnp.exp(m2 - m1)
        gate1 = g_val / (1.0 + r)
        gate2 = g_val * r / (1.0 + r)

        pos = ((i * n_units + u) % tiles_per_seq) * tm + lax.broadcasted_iota(jnp.int32, (1, tm), 1)
        valid = pos < seq_len
        oh1 = jnp.where(valid & (row == i1), 1.0, 0.0)
        oh2 = jnp.where(valid & (row == i2), 1.0, 0.0)
        pre = jnp.dot(jnp.concatenate([oh1, oh2], axis=0).astype(BF16), upper_ref[...],
                      preferred_element_type=F32)
        tot1 = jnp.sum(oh1, axis=1, keepdims=True)
        tot2 = jnp.sum(oh2, axis=1, keepdims=True)
        run_t = jnp.tile(run, (1, tm // LANES))
        rank1 = jnp.sum(oh1 * (pre[:ROUTE_ROWS] + run_t), axis=0, keepdims=True)
        rank2 = jnp.sum(oh2 * (pre[ROUTE_ROWS:] + run_t + tot1), axis=0, keepdims=True)

        e1 = (i1 - EXPERT_LANE0).astype(F32)
        e2 = (i2 - EXPERT_LANE0).astype(F32)
        r8 = lax.broadcasted_iota(jnp.int32, (ROUTE_COLS, tm), 0)
        route_ref[:, u * tm:(u + 1) * tm] = jnp.where(r8 == 0, e1, jnp.where(r8 == 1, e2, jnp.where(
            r8 == 2, gate1, jnp.where(r8 == 3, gate2, jnp.where(r8 == 4, rank1, jnp.where(r8 == 5, rank2, 0.0))))))
        return run + tot1 + tot2

    all_logits = [project(pl.ds(u * tm, tm)) for u in range(n_units)]
    run = run_ref[...]
    for u, logits in enumerate(all_logits):
        run = route(u, logits, run)
    run_ref[...] = run
    cnt_ref[...] = run


def _outproj(hp, convy, o, w_out, g2, wr_hi, wr_lo, br, upper, *, tiles_per_seq, seq_len):
    n, d = hp.shape
    tm = OUTPROJ_UNITS * upper.shape[0]
    assert n % tm == 0
    const = lambda i: (0, 0)
    tile = lambda i: (i, 0)
    kern = functools.partial(_outproj_kernel, tiles_per_seq=tiles_per_seq, seq_len=seq_len)
    return pl.pallas_call(
        kern,
        grid=(n // tm,),
        in_specs=[
            pl.BlockSpec((tm, d), tile),
            pl.BlockSpec((tm, convy.shape[1]), tile),
            pl.BlockSpec((tm, o.shape[1]), tile),
            pl.BlockSpec(w_out.shape, const),
            pl.BlockSpec((1, d), const),
            pl.BlockSpec(wr_hi.shape, const),
            pl.BlockSpec(wr_lo.shape, const),
            pl.BlockSpec((1, LANES), const),
            pl.BlockSpec(upper.shape, const),
        ],
        out_specs=[
            pl.BlockSpec((tm, d), tile),
            *[pl.BlockSpec((tm, LANES), tile)] * ROW_PLANES,
            pl.BlockSpec((ROUTE_COLS, tm), lambda i: (0, i)),
            pl.BlockSpec((ROUTE_ROWS, LANES), const),
        ],
        out_shape=[
            jax.ShapeDtypeStruct((n, d), F32),
            *[jax.ShapeDtypeStruct((n, LANES), jnp.uint32)] * ROW_PLANES,
            jax.ShapeDtypeStruct((ROUTE_COLS, n), F32),
            jax.ShapeDtypeStruct((ROUTE_ROWS, LANES), F32),
        ],
        scratch_shapes=[pltpu.VMEM((ROUTE_ROWS, LANES), F32)],
        compiler_params=_cparams(1),
        name="outproj_router",
    )(hp, convy, o, w_out, g2, wr_hi, wr_lo, br, upper)


def _slots_kernel(pstart_ref, route_ref, dest_ref, *, seq_len, p_rows):
    b = pl.program_id(0)
    route = route_ref[...]
    eid = route.astype(jnp.int32)
    start = jnp.zeros_like(eid)
    for e in range(N_EXPERTS):
        start = jnp.where(eid == e, pstart_ref[e], start)
    rank = pltpu.roll(route, ROUTE_COLS - 2 * TOP_K, axis=0).astype(jnp.int32)
    k = lax.broadcasted_iota(jnp.int32, route.shape, 0)
    pos = lax.broadcasted_iota(jnp.int32, route.shape, 1)
    n_pad = route.shape[1] - seq_len
    spare = p_rows + (b * n_pad + (pos - seq_len)) * TOP_K + k
    dest_ref[...] = jnp.where(pos < seq_len, start + rank, spare)


def _slots(pstarts, route, *, batch, lp_len, seq_len, p_rows):
    kern = functools.partial(_slots_kernel, seq_len=seq_len, p_rows=p_rows)
    grid_spec = pltpu.PrefetchScalarGridSpec(
        num_scalar_prefetch=1,
        grid=(batch,),
        in_specs=[pl.BlockSpec((ROUTE_COLS, lp_len), lambda b, ps: (0, b))],
        out_specs=pl.BlockSpec((ROUTE_COLS, lp_len), lambda b, ps: (0, b)),
    )
    return pl.pallas_call(
        kern,
        grid_spec=grid_spec,
        out_shape=jax.ShapeDtypeStruct(route.shape, jnp.int32),
        compiler_params=_cparams(1),
        name="moe_slots",
    )(pstarts, route)


def _sc_workers():
    info = plsc.get_sparse_core_info()
    return info.num_cores, info.num_cores * info.num_subcores


def _sc_scatter_rows(planes, idx_a, idx_b, out_rows):
    n_win = idx_a.shape[0]
    n_cores, n_workers = _sc_workers()
    trips = -(-n_win // n_workers)
    mesh = plsc.VectorSubcoreMesh(core_axis_name="c", subcore_axis_name="s")

    def body(*refs):
        xs = refs[0:ROW_PLANES]
        ia_hbm, ib_hbm = refs[ROW_PLANES:ROW_PLANES + 2]
        outs = refs[ROW_PLANES + 2:2 * ROW_PLANES + 2]
        ia_v, ib_v, buf, sem = refs[2 * ROW_PLANES + 2:]
        wid = lax.axis_index("s") * n_cores + lax.axis_index("c")

        def step(t, carry):
            g = wid + t * n_workers

            @pl.when(g < n_win)
            def _():
                row0 = pl.multiple_of(g * SC_WINDOW, SC_WINDOW)
                loads = [pltpu.async_copy(ia_hbm.at[g], ia_v, sem), pltpu.async_copy(ib_hbm.at[g], ib_v, sem)]
                loads += [pltpu.async_copy(xs[c].at[pl.ds(row0, SC_WINDOW)], buf.at[c], sem)
                          for c in range(ROW_PLANES)]
                for cp in loads:
                    cp.wait()
                stores = [pltpu.async_copy(buf.at[c], outs[c].at[iv], sem)
                          for c in range(ROW_PLANES) for iv in (ia_v, ib_v)]
                for cp in stores:
                    cp.wait()

            return carry

        lax.fori_loop(0, trips, step, 0)

    kern = pl.kernel(
        body,
        out_type=[jax.ShapeDtypeStruct((out_rows, LANES), jnp.uint32)] * ROW_PLANES,
        mesh=mesh,
        scratch_types=[
            pltpu.VMEM((SC_WINDOW,), jnp.int32),
            pltpu.VMEM((SC_WINDOW,), jnp.int32),
            pltpu.VMEM((ROW_PLANES, SC_WINDOW, LANES), jnp.uint32),
            pltpu.SemaphoreType.DMA,
        ],
        name="moe_dispatch_sc",
    )
    return kern(*planes, idx_a, idx_b)


def _sc_gather_rows(planes, idx_a, idx_b):
    n_win = idx_a.shape[0]
    n_cores, n_workers = _sc_workers()
    trips = -(-n_win // n_workers)
    mesh = plsc.VectorSubcoreMesh(core_axis_name="c", subcore_axis_name="s")

    def body(*refs):
        ys = refs[0:ROW_PLANES]
        ia_hbm, ib_hbm = refs[ROW_PLANES:ROW_PLANES + 2]
        outs_a = refs[ROW_PLANES + 2:2 * ROW_PLANES + 2]
        outs_b = refs[2 * ROW_PLANES + 2:3 * ROW_PLANES + 2]
        ia_v, ib_v, buf, sem = refs[3 * ROW_PLANES + 2:]
        wid = lax.axis_index("s") * n_cores + lax.axis_index("c")

        def step(t, carry):
            g = wid + t * n_workers

            @pl.when(g < n_win)
            def _():
                row0 = pl.multiple_of(g * SC_WINDOW, SC_WINDOW)
                idx_loads = [pltpu.async_copy(ia_hbm.at[g], ia_v, sem), pltpu.async_copy(ib_hbm.at[g], ib_v, sem)]
                for cp in idx_loads:
                    cp.wait()
                for iv, outs in ((ia_v, outs_a), (ib_v, outs_b)):
                    loads = [pltpu.async_copy(ys[c].at[iv], buf.at[c], sem) for c in range(ROW_PLANES)]
                    for cp in loads:
                        cp.wait()
                    stores = [pltpu.async_copy(buf.at[c], outs[c].at[pl.ds(row0, SC_WINDOW)], sem)
                              for c in range(ROW_PLANES)]
                    for cp in stores:
                        cp.wait()

            return carry

        lax.fori_loop(0, trips, step, 0)

    n_rows = n_win * SC_WINDOW
    kern = pl.kernel(
        body,
        out_type=[jax.ShapeDtypeStruct((n_rows, LANES), jnp.uint32)] * (2 * ROW_PLANES),
        mesh=mesh,
        scratch_types=[
            pltpu.VMEM((SC_WINDOW,), jnp.int32),
            pltpu.VMEM((SC_WINDOW,), jnp.int32),
            pltpu.VMEM((ROW_PLANES, SC_WINDOW, LANES), jnp.uint32),
            pltpu.SemaphoreType.DMA,
        ],
        name="moe_gather_sc",
    )
    res = kern(*planes, idx_a, idx_b)
    return res[:ROW_PLANES], res[ROW_PLANES:]


def _experts_kernel(be_ref, nv_ref, first_ref, slot_ref, nxt_ref, x0_ref, x1_ref, x2_ref, x3_ref,
                    wg_hbm, wu_hbm, wd_hbm, y0_ref, y1_ref, y2_ref, y3_ref,
                    wgf_ref, wuf_ref, wdf_ref, wgb_ref, wub_ref, wdb_ref, sem):
    i = pl.program_id(0)
    e = be_ref[i]
    slot = slot_ref[i]
    y_refs = (y0_ref, y1_ref, y2_ref, y3_ref)

    def weight_copies(expert, s):
        return [pltpu.make_async_copy(hbm.at[expert], stage.at[s], sem.at[s, j])
                for j, (hbm, stage) in enumerate(((wg_hbm, wgf_ref), (wu_hbm, wuf_ref), (wd_hbm, wdf_ref)))]

    @pl.when(i == 0)
    def _():
        for cp in weight_copies(e, slot):
            cp.start()

    @pl.when(first_ref[i] == 1)
    def _():
        for cp in weight_copies(e, slot):
            cp.wait()
        nxt = nxt_ref[i]

        @pl.when(nxt >= 0)
        def _():
            for cp in weight_copies(nxt, 1 - slot):
                cp.start()

        wgb_ref[...] = wgf_ref[slot].astype(BF16)
        wub_ref[...] = wuf_ref[slot].astype(BF16)
        wdb_ref[...] = wdf_ref[slot].astype(BF16)

    nv = nv_ref[i]

    def mlp(rows):
        xs = _unpack_rows([r[rows, :] for r in (x0_ref, x1_ref, x2_ref, x3_ref)])
        row = lax.broadcasted_iota(jnp.int32, xs.shape, 0)
        x = jnp.where(row < nv, xs, jnp.zeros_like(xs))
        hg = jnp.dot(x, wgb_ref[...], preferred_element_type=F32)
        hu = jnp.dot(x, wub_ref[...], preferred_element_type=F32)
        hid = (hg / (1.0 + jnp.exp(-hg)) * hu).astype(BF16)
        yw = _pack_rows(jnp.dot(hid, wdb_ref[...], preferred_element_type=F32))
        for c, ref in enumerate(y_refs):
            ref[rows, :] = yw[:, c * LANES:(c + 1) * LANES]

    def clear(rows):
        for ref in y_refs:
            ref[rows, :] = jnp.zeros((rows.size, LANES), ref.dtype)

    for piece in range(MOE_BLOCK // MOE_TAIL_STEP + 1):
        used = piece * MOE_TAIL_STEP

        @pl.when((nv > used - MOE_TAIL_STEP) & (nv <= used))
        def _():
            if used > 0:
                mlp(pl.ds(0, used))
            if used < MOE_BLOCK:
                clear(pl.ds(used, MOE_BLOCK - used))


def _experts(block_e, nvalid, xs_planes, w_gate, w_up, w_down):
    n_blocks = block_e.shape[0]
    _, d, ff = w_gate.shape
    first = jnp.concatenate([jnp.ones((1,), jnp.int32), (block_e[1:] != block_e[:-1]).astype(jnp.int32)])
    slot = (jnp.cumsum(first) - 1) % 2
    later = jnp.where(block_e[None, :] > block_e[:, None], block_e[None, :], N_EXPERTS)
    nxt = jnp.min(later, axis=1)
    nxt = jnp.where(nxt == N_EXPERTS, -1, nxt).astype(jnp.int32)
    blk = lambda i, *_: (i, 0)
    grid_spec = pltpu.PrefetchScalarGridSpec(
        num_scalar_prefetch=5,
        grid=(n_blocks,),
        in_specs=[
            *[pl.BlockSpec((MOE_BLOCK, LANES), blk)] * ROW_PLANES,
            pl.BlockSpec(memory_space=pl.ANY),
            pl.BlockSpec(memory_space=pl.ANY),
            pl.BlockSpec(memory_space=pl.ANY),
        ],
        out_specs=[pl.BlockSpec((MOE_BLOCK, LANES), blk)] * ROW_PLANES,
        scratch_shapes=[
            pltpu.VMEM((2, d, ff), F32),
            pltpu.VMEM((2, d, ff), F32),
            pltpu.VMEM((2, ff, d), F32),
            pltpu.VMEM((d, ff), BF16),
            pltpu.VMEM((d, ff), BF16),
            pltpu.VMEM((ff, d), BF16),
            pltpu.SemaphoreType.DMA((2, 3)),
        ],
    )
    return pl.pallas_call(
        _experts_kernel,
        grid_spec=grid_spec,
        out_shape=[jax.ShapeDtypeStruct((n_blocks * MOE_BLOCK, LANES), jnp.uint32)] * ROW_PLANES,
        compiler_params=_cparams(1),
        name="moe_experts",
    )(block_e, nvalid, first, slot.astype(jnp.int32), nxt, *xs_planes, w_gate, w_up, w_down)


def _combine_kernel(gates_ref, h1_hbm, *refs, lp_len, tile):
    a_refs = refs[0:ROW_PLANES]
    b_refs = refs[ROW_PLANES:2 * ROW_PLANES]
    out_ref, hbuf, sem_h = refs[2 * ROW_PLANES:]
    nt = pl.num_programs(1)
    step = pl.program_id(0) * nt + pl.program_id(1)
    last = pl.num_programs(0) * nt - 1

    def h_copy(s, slot):
        start = (s // nt) * lp_len + N_META + (s % nt) * tile
        return pltpu.make_async_copy(h1_hbm.at[pl.ds(start, tile), :], hbuf.at[slot], sem_h.at[slot])

    slot = step % 2

    @pl.when(step == 0)
    def _():
        h_copy(step, slot).start()

    @pl.when(step < last)
    def _():
        h_copy(step + 1, 1 - slot).start()

    ya = _unpack_rows([r[...] for r in a_refs]).astype(F32)
    yb = _unpack_rows([r[...] for r in b_refs]).astype(F32)
    g = jnp.concatenate([gates_ref[...]] * (LANES // ROUTE_COLS), axis=0).T
    moe = g[:, 0:1] * ya + g[:, 1:2] * yb
    h_copy(step, slot).wait()
    out_ref[0] = hbuf[slot] + moe


def _combine(gates, h1, a_planes, b_planes, *, batch, seq, lp_len, tile):
    d = h1.shape[1]
    nt = seq // tile
    kern = functools.partial(_combine_kernel, lp_len=lp_len, tile=tile)
    rows = lambda b, i: (b * nt + i, 0)
    return pl.pallas_call(
        kern,
        grid=(batch, nt),
        in_specs=[
            pl.BlockSpec((ROUTE_COLS, tile), lambda b, i: (0, b * nt + i)),
            pl.BlockSpec(memory_space=pl.ANY),
            *[pl.BlockSpec((tile, LANES), rows)] * (2 * ROW_PLANES),
        ],
        out_specs=pl.BlockSpec((1, tile, d), lambda b, i: (b, i, 0)),
        out_shape=jax.ShapeDtypeStruct((batch, seq, d), F32),
        scratch_shapes=[pltpu.VMEM((2, tile, d), F32), pltpu.SemaphoreType.DMA((2,))],
        compiler_params=_cparams(2),
        name="moe_combine",
    )(gates, h1, *a_planes, *b_planes)


def _rope_tables(length, lp_len):
    half = ROPE_DIM // 2
    pos = jnp.arange(length, dtype=F32)
    inv_freq = ROPE_THETA ** (-jnp.arange(0, ROPE_DIM, 2, dtype=F32) / ROPE_DIM)
    ang = pos[:, None] * inv_freq[None, :]
    cos = jnp.cos(ang)
    sin = jnp.sin(ang)
    ones = jnp.ones((length, HEAD_DIM - ROPE_DIM), F32)
    zeros_h = jnp.zeros((length, half), F32)
    zeros_r = jnp.zeros((length, HEAD_DIM - ROPE_DIM), F32)
    c = jnp.concatenate([cos, cos, ones], axis=1)
    s1 = jnp.concatenate([zeros_h, sin, zeros_r], axis=1)
    s2 = jnp.concatenate([-sin, zeros_h, zeros_r], axis=1)
    pad = ((0, lp_len - length), (0, 0))
    rep = LANES // HEAD_DIM
    return tuple(jnp.pad(jnp.tile(t, (1, rep)), pad) for t in (c, s1, s2))


def kernel(x, meta_tokens, norm1_g, w_in, conv_w, q_norm_g, k_norm_g, lambda_q1, lambda_k1, lambda_q2, lambda_k2,
           subln_g, w_out, norm2_g, w_router_group, b_router_group, w_router_expert, b_router_expert, w_gate,
           w_up, w_down):
    batch, seq, _ = x.shape
    assert w_in.shape[0] == 1, "a single layer is supported"
    l = 0
    length = seq + N_META
    tm = TOKEN_TILE
    lp_len = -(-length // tm) * tm
    tiles_per_seq = lp_len // tm
    assert tiles_per_seq >= 2 and (length - (tiles_per_seq - 1) * tm) % 8 == 0
    qw = N_HEADS * 2 * HEAD_DIM
    lam_init = 0.8 - 0.6 * math.exp(-0.3 * l)

    reps = qw // HEAD_DIM
    gq = jnp.tile(q_norm_g[l] * (HEAD_DIM ** -0.5 * LOG2E), reps)[None, :]
    gk = jnp.tile(k_norm_g[l], reps)[None, :]
    seg = jnp.arange(qw) // HEAD_DIM
    bd = (seg[:, None] == seg[None, :]).astype(BF16)
    rope = _rope_tables(length, lp_len)
    hp, convy, q, k, v = _inproj(x, meta_tokens.astype(x.dtype), norm1_g[l][None, :], w_in[l].astype(BF16),
                                 conv_w[l], gq, gk, bd, *rope, tiles_per_seq=tiles_per_seq, tm=tm)

    lamp = jnp.stack([lambda_q1[l], lambda_k1[l], lambda_q2[l], lambda_k2[l]]).astype(F32)
    o = _attention(q, k, v, lamp, subln_g[l][None, :], batch=batch, lp_len=lp_len, tq=tm, lam_init=lam_init)

    lane_pad = LANES - N_GROUPS - N_EXPERTS
    wr = jnp.pad(jnp.concatenate([w_router_group[l], w_router_expert[l]], axis=1), ((0, 0), (0, lane_pad)))
    wr_hi = wr.astype(BF16)
    wr_lo = (wr - wr_hi.astype(F32)).astype(BF16)
    br = jnp.pad(jnp.concatenate([b_router_group[l], b_router_expert[l]]), (0, lane_pad))[None, :]
    ridx = jnp.arange(tm)
    upper = (ridx[:, None] < ridx[None, :]).astype(BF16)
    h1, *rest = _outproj(hp, convy, o, w_out[l].astype(BF16), norm2_g[l][None, :], wr_hi, wr_lo, br, upper,
                         tiles_per_seq=tiles_per_seq, seq_len=length)
    x_planes = rest[:ROW_PLANES]
    route, cnt = rest[ROW_PLANES:]

    counts = cnt[EXPERT_LANE0:EXPERT_LANE0 + N_EXPERTS, 0].astype(jnp.int32)
    n_blocks = -(-(batch * length * TOP_K) // MOE_BLOCK) + N_EXPERTS
    p_rows = n_blocks * MOE_BLOCK
    padded = (counts + MOE_BLOCK - 1) // MOE_BLOCK * MOE_BLOCK
    pends = jnp.cumsum(padded)
    pstarts = pends - padded

    def lookup(table, idx):
        hit = idx[:, None] == jnp.arange(N_EXPERTS, dtype=jnp.int32)[None, :]
        return jnp.sum(jnp.where(hit, table[None, :], 0), axis=1)

    blk0 = jnp.arange(n_blocks, dtype=jnp.int32) * MOE_BLOCK
    block_e = jnp.minimum(jnp.sum((pends[None, :] <= blk0[:, None]).astype(jnp.int32), axis=1), N_EXPERTS - 1)
    nvalid = jnp.clip(lookup(counts, block_e) - (blk0 - lookup(pstarts, block_e)), 0, MOE_BLOCK)

    spare_rows = -(-(batch * (lp_len - length) * TOP_K) // MOE_BLOCK) * MOE_BLOCK
    dest = _slots(pstarts.astype(jnp.int32), route, batch=batch, lp_len=lp_len, seq_len=length, p_rows=p_rows)
    dest = dest.reshape(ROUTE_COLS, batch, lp_len)
    gates = route.reshape(ROUTE_COLS, batch, lp_len)[TOP_K:2 * TOP_K]

    assert (batch * lp_len) % SC_WINDOW == 0 and (batch * seq) % SC_WINDOW == 0
    xs_planes = _sc_scatter_rows(x_planes, dest[0].reshape(-1, SC_WINDOW), dest[1].reshape(-1, SC_WINDOW),
                                 p_rows + spare_rows)
    y_planes = _experts(block_e, nvalid, xs_planes, w_gate[l], w_up[l], w_down[l])

    dest_x = dest[0:TOP_K, :, N_META:length]
    a_planes, b_planes = _sc_gather_rows(y_planes, dest_x[0].reshape(-1, SC_WINDOW),
                                         dest_x[1].reshape(-1, SC_WINDOW))
    gates_x = jnp.pad(gates[:, :, N_META:length].reshape(TOP_K, batch * seq), ((0, ROUTE_COLS - TOP_K), (0, 0)))
    return _combine(gates_x, h1, a_planes, b_planes, batch=batch, seq=seq, lp_len=lp_len,
                    tile=_largest_tile(seq, 512, LANES))
```

```python
import functools
import math

import jax
import jax.numpy as jnp
from jax import lax
from jax.experimental import pallas as pl
from jax.experimental.pallas import tpu as pltpu
from jax.experimental.pallas import tpu_sc as plsc

F32 = jnp.float32
BF16 = jnp.bfloat16

N_META = 16
N_HEADS = 4
HEAD_DIM = 64
ROPE_DIM = HEAD_DIM // 4
ROPE_THETA = 500000.0
N_GROUPS = 4
EXPERTS_PER_GROUP = 8
N_EXPERTS = N_GROUPS * EXPERTS_PER_GROUP
TOP_K = 2
EPS = 1e-6
LOG2E = 1.4426950408889634

LANES = 128
TOKEN_TILE = 640
INPROJ_UNITS = 2
OUTPROJ_UNITS = 2
ATTN_HEADS_PER_STEP = 4
MOE_BLOCK = 1024
MOE_TAIL_STEP = 256
ROUTE_COLS = 8
ROUTE_ROWS = 64
ROW_PLANES = 4
SC_WINDOW = 128
EXPERT_LANE0 = N_GROUPS
NEG_BIG = -1e30
VMEM_LIMIT = 56 * 1024 * 1024


def _largest_tile(n, cap, mult):
    for t in range(min(cap, n), 0, -1):
        if n % t == 0 and t % mult == 0:
            return t
    raise ValueError(f"no tile for {n}")


def _cparams(n_axes):
    return pltpu.CompilerParams(dimension_semantics=("arbitrary",) * n_axes, vmem_limit_bytes=VMEM_LIMIT)


def _pack_rows(x):
    w = x.shape[1] // 2
    lo = lax.bitcast_convert_type(x[:, :w].astype(BF16).astype(F32), jnp.uint32)
    hi = lax.bitcast_convert_type(x[:, w:].astype(BF16).astype(F32), jnp.uint32)
    return lax.shift_right_logical(lo, jnp.uint32(16)) | (hi & jnp.uint32(0xFFFF0000))


def _unpack_rows(planes):
    w = jnp.concatenate(planes, axis=1)
    lo = lax.bitcast_convert_type(lax.shift_left(w, jnp.uint32(16)), F32)
    hi = lax.bitcast_convert_type(w & jnp.uint32(0xFFFF0000), F32)
    return jnp.concatenate([lo, hi], axis=1).astype(BF16)


def _inproj_kernel(x_hbm, meta_hbm, g1_ref, win_ref, convw_ref, gq_ref, gk_ref, bd_ref, *refs,
                   tiles_per_seq, seq, cw, qw, units):
    rope_refs = refs[0:3 * units]
    hp_hbm, convy_ref, q_ref, k_ref, v_ref, carry_ref, xbuf, sem, sem_out = refs[3 * units:]
    i = pl.program_id(0)
    n_steps = pl.num_programs(0)
    tm = xbuf.shape[1]
    q0 = 3 * cw
    w = convw_ref[...]
    last_rows = seq + N_META - (tiles_per_seq - 1) * tm

    def fetch(tile, slot, start):
        b = tile // tiles_per_seq
        t = tile % tiles_per_seq

        def go(src, dst):
            cp = pltpu.make_async_copy(src, dst, sem.at[slot])
            if start:
                cp.start()
            else:
                cp.wait()

        @pl.when(t == 0)
        def _():
            go(meta_hbm, xbuf.at[slot, pl.ds(0, N_META)])
            go(x_hbm.at[pl.ds(b * seq, tm - N_META)], xbuf.at[slot, pl.ds(N_META, tm - N_META)])

        @pl.when((t > 0) & (t < tiles_per_seq - 1))
        def _():
            go(x_hbm.at[pl.ds(b * seq + t * tm - N_META, tm)], xbuf.at[slot])

        @pl.when(t == tiles_per_seq - 1)
        def _():
            go(x_hbm.at[pl.ds(b * seq + t * tm - N_META, last_rows)], xbuf.at[slot, pl.ds(0, last_rows)])

    def hp_store(tile, slot):
        return pltpu.make_async_copy(xbuf.at[slot], hp_hbm.at[pl.ds(tile * tm, tm)], sem_out.at[slot])

    mine = (i % 2) * units
    other = units - mine

    @pl.when(i == 0)
    def _():
        for u in range(units):
            fetch(u, mine + u, True)

    @pl.when(i > 0)
    def _():
        for u in range(units):
            hp_store((i - 1) * units + u, other + u).wait()

    @pl.when(i + 1 < n_steps)
    def _():
        for u in range(units):
            fetch((i + 1) * units + u, other + u, True)

    for u in range(units):
        tile = i * units + u
        fetch(tile, mine + u, False)

        @pl.when(tile % tiles_per_seq == tiles_per_seq - 1)
        def _():
            xbuf[mine + u, pl.ds(last_rows, tm - last_rows), :] = jnp.zeros((tm - last_rows, xbuf.shape[2]),
                                                                           xbuf.dtype)

        hp_store(tile, mine + u).start()

    prev = carry_ref[...]

    for u in range(units):
        rows = pl.ds(u * tm, tm)
        x = xbuf[mine + u]
        ms = jnp.mean(x * x, axis=-1, keepdims=True)
        xn = (x * lax.rsqrt(ms + EPS) * g1_ref[...]).astype(BF16)

        def proj(lo, hi):
            return jnp.dot(xn, win_ref[:, lo:hi], preferred_element_type=F32)

        u_conv = proj(0, q0)
        u_q = proj(q0, q0 + qw)

        z = u_conv[:, cw:2 * cw] * u_conv[:, 2 * cw:3 * cw]
        prev = jnp.where((i * units + u) % tiles_per_seq == 0, 0.0, prev)
        p1 = prev[7:8]
        p2 = prev[6:7]
        row = lax.broadcasted_iota(jnp.int32, z.shape, 0)
        z1 = jnp.where(row == 0, p1, pltpu.roll(z, 1, axis=0))
        z2 = jnp.where(row == 0, p2, jnp.where(row == 1, p1, pltpu.roll(z, 2, axis=0)))
        prev = z[tm - 8:tm]
        conv = w[0:1] * z2 + w[1:2] * z1 + w[2:3] * z
        convy_ref[rows, :] = (u_conv[:, 0:cw] * conv).astype(BF16)

        rc, rs1, rs2 = (ref[...] for ref in rope_refs[3 * u:3 * u + 3])

        def norm_rope(t, g_ref):
            ss = jnp.dot((t * t).astype(BF16), bd_ref[...], preferred_element_type=F32)
            tn = t * lax.rsqrt(ss * (1.0 / HEAD_DIM) + EPS) * g_ref[...]
            outs = []
            for c in range(qw // LANES):
                ch = tn[:, c * LANES:(c + 1) * LANES]
                outs.append(ch * rc + pltpu.roll(ch, ROPE_DIM // 2, axis=1) * rs1
                            + pltpu.roll(ch, LANES - ROPE_DIM // 2, axis=1) * rs2)
            return jnp.concatenate(outs, axis=1).astype(BF16)

        u_k = proj(q0 + qw, q0 + 2 * qw)
        q_ref[rows, :] = norm_rope(u_q, gq_ref)
        u_v = proj(q0 + 2 * qw, win_ref.shape[1])
        k_ref[rows, :] = norm_rope(u_k, gk_ref)
        v_ref[rows, :] = u_v.astype(BF16)

    carry_ref[...] = prev

    @pl.when(i == n_steps - 1)
    def _():
        for u in range(units):
            hp_store(i * units + u, mine + u).wait()


def _inproj(x, meta, g1, w_in, conv_w, gq, gk, bd, rc, rs1, rs2, *, tiles_per_seq, tm):
    batch, seq, d = x.shape
    units = INPROJ_UNITS
    n = batch * tiles_per_seq * tm
    assert (batch * tiles_per_seq) % units == 0
    cw = conv_w.shape[1]
    qw = gq.shape[1]
    aw = w_in.shape[1] - 3 * cw - 2 * qw
    const = lambda i: (0, 0)
    step = lambda i: (i, 0)
    rope_specs = [pl.BlockSpec((tm, LANES), lambda i, u=u: ((i * units + u) % tiles_per_seq, 0))
                  for u in range(units) for _ in range(3)]
    kern = functools.partial(_inproj_kernel, tiles_per_seq=tiles_per_seq, seq=seq, cw=cw, qw=qw, units=units)
    return pl.pallas_call(
        kern,
        grid=(n // (units * tm),),
        in_specs=[
            pl.BlockSpec(memory_space=pl.ANY),
            pl.BlockSpec(memory_space=pl.ANY),
            pl.BlockSpec((1, d), const),
            pl.BlockSpec(w_in.shape, const),
            pl.BlockSpec(conv_w.shape, const),
            pl.BlockSpec((1, qw), const),
            pl.BlockSpec((1, qw), const),
            pl.BlockSpec(bd.shape, const),
            *rope_specs,
        ],
        out_specs=[
            pl.BlockSpec(memory_space=pl.ANY),
            pl.BlockSpec((units * tm, cw), step),
            pl.BlockSpec((units * tm, qw), step),
            pl.BlockSpec((units * tm, qw), step),
            pl.BlockSpec((units * tm, aw), step),
        ],
        out_shape=[
            jax.ShapeDtypeStruct((n, d), x.dtype),
            jax.ShapeDtypeStruct((n, cw), BF16),
            jax.ShapeDtypeStruct((n, qw), BF16),
            jax.ShapeDtypeStruct((n, qw), BF16),
            jax.ShapeDtypeStruct((n, aw), BF16),
        ],
        scratch_shapes=[pltpu.VMEM((8, cw), F32), pltpu.VMEM((2 * units, tm, d), x.dtype),
                        pltpu.SemaphoreType.DMA((2 * units,)), pltpu.SemaphoreType.DMA((2 * units,))],
        compiler_params=_cparams(1),
        name="inproj",
    )(x.reshape(batch * seq, d), meta, g1, w_in, conv_w, gq, gk, bd, *([rc, rs1, rs2] * units))


def _attn_kernel(q_ref, k_ref, v_ref, lamp_ref, sg_ref, o_ref, qs_ref, m_ref, l_ref, acc_ref, *, lam_init):
    qi = pl.program_id(2)
    tq = q_ref.shape[0]
    n_heads = q_ref.shape[1] // LANES
    n_chains = 2 * n_heads
    lane = lax.broadcasted_iota(jnp.int32, (tq, LANES), 1)
    for h in range(n_heads):
        q = q_ref[:, h * LANES:(h + 1) * LANES]
        zero = jnp.zeros_like(q)
        qs_ref[pl.ds(2 * h * tq, tq), :] = jnp.where(lane < HEAD_DIM, q, zero)
        qs_ref[pl.ds((2 * h + 1) * tq, tq), :] = jnp.where(lane >= HEAD_DIM, q, zero)
    m_ref[...] = jnp.full_like(m_ref, NEG_BIG)
    l_ref[...] = jnp.zeros_like(l_ref)
    acc_ref[...] = jnp.zeros_like(acc_ref)

    def scores(off, width, which):
        h = which // 2
        kc = k_ref[pl.ds(off, width), h * LANES:(h + 1) * LANES]
        return lax.dot_general(qs_ref[pl.ds(which * tq, tq), :], kc, (((1,), (1,)), ((), ())),
                               preferred_element_type=F32)

    def update(off, width, which, s, masked):
        h = which // 2
        vc = jnp.concatenate([v_ref[pl.ds(off, width), h * LANES:(h + 1) * LANES],
                              jnp.ones((width, LANES), BF16)], axis=1)
        rows = pl.ds(which * tq, tq)
        if masked:
            r = lax.broadcasted_iota(jnp.int32, s.shape, 0)
            c = lax.broadcasted_iota(jnp.int32, s.shape, 1)
            s = jnp.where(c <= r + (width - tq), s, NEG_BIG)
        m_prev = m_ref[rows, :]
        m_new = jnp.maximum(m_prev, jnp.max(s, axis=-1, keepdims=True))
        alpha = jnp.exp2(m_prev - m_new)
        p = jnp.exp2((s - jnp.tile(m_new, (1, width // LANES))).astype(BF16))
        pv = jnp.dot(p, vc, preferred_element_type=F32)
        l_ref[rows, :] = alpha * l_ref[rows, :] + pv[:, LANES:]
        acc_ref[rows, :] = alpha * acc_ref[rows, :] + pv[:, :LANES]
        m_ref[rows, :] = m_new

    def chunk(off, width, masked):
        s_next = scores(off, width, 0)
        for c in range(n_chains):
            s = s_next
            if c + 1 < n_chains:
                s_next = scores(off, width, c + 1)
            update(off, width, c, s, masked)

    wide = 2 * tq

    def body(j, carry):
        chunk(pl.multiple_of(j * wide, wide), wide, False)
        return carry

    lax.fori_loop(0, qi // 2, body, 0)
    odd = qi % 2 == 1

    @pl.when(odd)
    def _():
        chunk(pl.multiple_of((qi - 1) * tq, tq), wide, True)

    @pl.when(jnp.logical_not(odd))
    def _():
        chunk(pl.multiple_of(qi * tq, tq), tq, True)

    lp = lamp_ref[...]
    lam = (jnp.exp(jnp.sum(lp[0:1] * lp[1:2], axis=-1, keepdims=True))
           - jnp.exp(jnp.sum(lp[2:3] * lp[3:4], axis=-1, keepdims=True)) + lam_init)
    for h in range(n_heads):
        rows = pl.ds(2 * h * tq, 2 * tq)
        o_all = acc_ref[rows, :] / l_ref[rows, :]
        o = o_all[0:tq] - lam * o_all[tq:2 * tq]
        ms = jnp.mean(o * o, axis=-1, keepdims=True)
        o_ref[:, h * LANES:(h + 1) * LANES] = (o * lax.rsqrt(ms + EPS) * sg_ref[...]
                                               * (1.0 - lam_init)).astype(BF16)


def _attention(q, k, v, lamp, sg, *, batch, lp_len, tq, lam_init):
    n, qw = q.shape
    nq = lp_len // tq
    hw = ATTN_HEADS_PER_STEP * LANES
    chains = 2 * ATTN_HEADS_PER_STEP
    kern = functools.partial(_attn_kernel, lam_init=lam_init)
    return pl.pallas_call(
        kern,
        grid=(batch, qw // hw, nq),
        in_specs=[
            pl.BlockSpec((tq, hw), lambda b, h, i: (b * nq + i, h)),
            pl.BlockSpec((lp_len, hw), lambda b, h, i: (b, h)),
            pl.BlockSpec((lp_len, hw), lambda b, h, i: (b, h)),
            pl.BlockSpec(lamp.shape, lambda b, h, i: (0, 0)),
            pl.BlockSpec(sg.shape, lambda b, h, i: (0, 0)),
        ],
        out_specs=pl.BlockSpec((tq, hw), lambda b, h, i: (b * nq + i, h)),
        out_shape=jax.ShapeDtypeStruct((n, v.shape[1]), BF16),
        scratch_shapes=[
            pltpu.VMEM((chains * tq, LANES), BF16),
            pltpu.VMEM((chains * tq, LANES), F32),
            pltpu.VMEM((chains * tq, LANES), F32),
            pltpu.VMEM((chains * tq, LANES), F32),
        ],
        compiler_params=_cparams(3),
        name="diffattn",
    )(q, k, v, lamp, sg)


def _outproj_kernel(hp_ref, cy_ref, o_ref, wout_ref, g2_ref, wrh_ref, wrl_ref, br_ref, upper_ref,
                    h1_ref, xp0_ref, xp1_ref, xp2_ref, xp3_ref, route_ref, cnt_ref, run_ref,
                    *, tiles_per_seq, seq_len):
    i = pl.program_id(0)
    tm = upper_ref.shape[0]
    n_units = hp_ref.shape[0] // tm

    @pl.when(i == 0)
    def _():
        run_ref[...] = jnp.zeros_like(run_ref)

    def project(rows):
        mix = jnp.concatenate([cy_ref[rows, :], o_ref[rows, :]], axis=1)
        h1 = hp_ref[rows, :] + jnp.dot(mix, wout_ref[...], preferred_element_type=F32)
        h1_ref[rows, :] = h1
        ms = jnp.mean(h1 * h1, axis=-1, keepdims=True)
        xn = h1 * lax.rsqrt(ms + EPS) * g2_ref[...]
        xw = _pack_rows(xn)
        for c, ref in enumerate((xp0_ref, xp1_ref, xp2_ref, xp3_ref)):
            ref[rows, :] = xw[:, c * LANES:(c + 1) * LANES]
        x_hi = xn.astype(BF16)
        x_lo = (xn - x_hi.astype(F32)).astype(BF16)
        hi_both = jnp.dot(x_hi, jnp.concatenate([wrh_ref[...], wrl_ref[...]], axis=1),
                          preferred_element_type=F32)
        return (hi_both[:, :LANES] + hi_both[:, LANES:]
                + jnp.dot(x_lo, wrh_ref[...], preferred_element_type=F32) + br_ref[...])

    def route(u, logits, run):
        lt = logits.T[0:ROUTE_ROWS, :]
        row = lax.broadcasted_iota(jnp.int32, lt.shape, 0)
        big = jnp.int32(4 * LANES)

        def first_argmax(vals, vmax):
            return jnp.min(jnp.where(vals == vmax, row, big), axis=0, keepdims=True)

        gl = jnp.where(row < N_GROUPS, lt, NEG_BIG)
        gmax = jnp.max(gl, axis=0, keepdims=True)
        g_val = 1.0 / jnp.sum(jnp.exp(gl - gmax), axis=0, keepdims=True)
        g_idx = first_argmax(gl, gmax)
        lo = EXPERT_LANE0 + EXPERTS_PER_GROUP * g_idx
        el = jnp.where((row >= lo) & (row < lo + EXPERTS_PER_GROUP), lt, NEG_BIG)
        m1 = jnp.max(el, axis=0, keepdims=True)
        i1 = first_argmax(el, m1)
        el2 = jnp.where(row == i1, NEG_BIG, el)
        m2 = jnp.max(el2, axis=0, keepdims=True)
        i2 = first_argmax(el2, m2)
        r = jnp.exp(m2 - m1)
        gate1 = g_val / (1.0 + r)
        gate2 = g_val * r / (1.0 + r)

        pos = ((i * n_units + u) % tiles_per_seq) * tm + lax.broadcasted_iota(jnp.int32, (1, tm), 1)
        valid = pos < seq_len
        oh1 = jnp.where(valid & (row == i1), 1.0, 0.0)
        oh2 = jnp.where(valid & (row == i2), 1.0, 0.0)
        pre = jnp.dot(jnp.concatenate([oh1, oh2], axis=0).astype(BF16), upper_ref[...],
                      preferred_element_type=F32)
        tot1 = jnp.sum(oh1, axis=1, keepdims=True)
        tot2 = jnp.sum(oh2, axis=1, keepdims=True)
        run_t = jnp.tile(run, (1, tm // LANES))
        rank1 = jnp.sum(oh1 * (pre[:ROUTE_ROWS] + run_t), axis=0, keepdims=True)
        rank2 = jnp.sum(oh2 * (pre[ROUTE_ROWS:] + run_t + tot1), axis=0, keepdims=True)

        e1 = (i1 - EXPERT_LANE0).astype(F32)
        e2 = (i2 - EXPERT_LANE0).astype(F32)
        r8 = lax.broadcasted_iota(jnp.int32, (ROUTE_COLS, tm), 0)
        route_ref[:, u * tm:(u + 1) * tm] = jnp.where(r8 == 0, e1, jnp.where(r8 == 1, e2, jnp.where(
            r8 == 2, gate1, jnp.where(r8 == 3, gate2, jnp.where(r8 == 4, rank1, jnp.where(r8 == 5, rank2, 0.0))))))
        return run + tot1 + tot2

    all_logits = [project(pl.ds(u * tm, tm)) for u in range(n_units)]
    run = run_ref[...]
    for u, logits in enumerate(all_logits):
        run = route(u, logits, run)
    run_ref[...] = run
    cnt_ref[...] = run


def _outproj(hp, convy, o, w_out, g2, wr_hi, wr_lo, br, upper, *, tiles_per_seq, seq_len):
    n, d = hp.shape
    tm = OUTPROJ_UNITS * upper.shape[0]
    assert n % tm == 0
    const = lambda i: (0, 0)
    tile = lambda i: (i, 0)
    kern = functools.partial(_outproj_kernel, tiles_per_seq=tiles_per_seq, seq_len=seq_len)
    return pl.pallas_call(
        kern,
        grid=(n // tm,),
        in_specs=[
            pl.BlockSpec((tm, d), tile),
            pl.BlockSpec((tm, convy.shape[1]), tile),
            pl.BlockSpec((tm, o.shape[1]), tile),
            pl.BlockSpec(w_out.shape, const),
            pl.BlockSpec((1, d), const),
            pl.BlockSpec(wr_hi.shape, const),
            pl.BlockSpec(wr_lo.shape, const),
            pl.BlockSpec((1, LANES), const),
            pl.BlockSpec(upper.shape, const),
        ],
        out_specs=[
            pl.BlockSpec((tm, d), tile),
            *[pl.BlockSpec((tm, LANES), tile)] * ROW_PLANES,
            pl.BlockSpec((ROUTE_COLS, tm), lambda i: (0, i)),
            pl.BlockSpec((ROUTE_ROWS, LANES), const),
        ],
        out_shape=[
            jax.ShapeDtypeStruct((n, d), F32),
            *[jax.ShapeDtypeStruct((n, LANES), jnp.uint32)] * ROW_PLANES,
            jax.ShapeDtypeStruct((ROUTE_COLS, n), F32),
            jax.ShapeDtypeStruct((ROUTE_ROWS, LANES), F32),
        ],
        scratch_shapes=[pltpu.VMEM((ROUTE_ROWS, LANES), F32)],
        compiler_params=_cparams(1),
        name="outproj_router",
    )(hp, convy, o, w_out, g2, wr_hi, wr_lo, br, upper)


def _slots_kernel(pstart_ref, route_ref, dest_ref, *, seq_len, p_rows):
    b = pl.program_id(0)
    route = route_ref[...]
    eid = route.astype(jnp.int32)
    start = jnp.zeros_like(eid)
    for e in range(N_EXPERTS):
        start = jnp.where(eid == e, pstart_ref[e], start)
    rank = pltpu.roll(route, ROUTE_COLS - 2 * TOP_K, axis=0).astype(jnp.int32)
    k = lax.broadcasted_iota(jnp.int32, route.shape, 0)
    pos = lax.broadcasted_iota(jnp.int32, route.shape, 1)
    n_pad = route.shape[1] - seq_len
    spare = p_rows + (b * n_pad + (pos - seq_len)) * TOP_K + k
    dest_ref[...] = jnp.where(pos < seq_len, start + rank, spare)


def _slots(pstarts, route, *, batch, lp_len, seq_len, p_rows):
    kern = functools.partial(_slots_kernel, seq_len=seq_len, p_rows=p_rows)
    grid_spec = pltpu.PrefetchScalarGridSpec(
        num_scalar_prefetch=1,
        grid=(batch,),
        in_specs=[pl.BlockSpec((ROUTE_COLS, lp_len), lambda b, ps: (0, b))],
        out_specs=pl.BlockSpec((ROUTE_COLS, lp_len), lambda b, ps: (0, b)),
    )
    return pl.pallas_call(
        kern,
        grid_spec=grid_spec,
        out_shape=jax.ShapeDtypeStruct(route.shape, jnp.int32),
        compiler_params=_cparams(1),
        name="moe_slots",
    )(pstarts, route)


def _sc_workers():
    info = plsc.get_sparse_core_info()
    return info.num_cores, info.num_cores * info.num_subcores


def _sc_scatter_rows(planes, idx_a, idx_b, out_rows):
    n_win = idx_a.shape[0]
    n_cores, n_workers = _sc_workers()
    trips = -(-n_win // n_workers)
    mesh = plsc.VectorSubcoreMesh(core_axis_name="c", subcore_axis_name="s")

    def body(*refs):
        xs = refs[0:ROW_PLANES]
        ia_hbm, ib_hbm = refs[ROW_PLANES:ROW_PLANES + 2]
        outs = refs[ROW_PLANES + 2:2 * ROW_PLANES + 2]
        ia_v, ib_v, buf, sem = refs[2 * ROW_PLANES + 2:]
        wid = lax.axis_index("s") * n_cores + lax.axis_index("c")

        def step(t, carry):
            g = wid + t * n_workers

            @pl.when(g < n_win)
            def _():
                row0 = pl.multiple_of(g * SC_WINDOW, SC_WINDOW)
                loads = [pltpu.async_copy(ia_hbm.at[g], ia_v, sem), pltpu.async_copy(ib_hbm.at[g], ib_v, sem)]
                loads += [pltpu.async_copy(xs[c].at[pl.ds(row0, SC_WINDOW)], buf.at[c], sem)
                          for c in range(ROW_PLANES)]
                for cp in loads:
                    cp.wait()
                stores = [pltpu.async_copy(buf.at[c], outs[c].at[iv], sem)
                          for c in range(ROW_PLANES) for iv in (ia_v, ib_v)]
                for cp in stores:
                    cp.wait()

            return carry

        lax.fori_loop(0, trips, step, 0)

    kern = pl.kernel(
        body,
        out_type=[jax.ShapeDtypeStruct((out_rows, LANES), jnp.uint32)] * ROW_PLANES,
        mesh=mesh,
        scratch_types=[
            pltpu.VMEM((SC_WINDOW,), jnp.int32),
            pltpu.VMEM((SC_WINDOW,), jnp.int32),
            pltpu.VMEM((ROW_PLANES, SC_WINDOW, LANES), jnp.uint32),
            pltpu.SemaphoreType.DMA,
        ],
        name="moe_dispatch_sc",
    )
    return kern(*planes, idx_a, idx_b)


def _sc_gather_rows(planes, idx_a, idx_b):
    n_win = idx_a.shape[0]
    n_cores, n_workers = _sc_workers()
    trips = -(-n_win // n_workers)
    mesh = plsc.VectorSubcoreMesh(core_axis_name="c", subcore_axis_name="s")

    def body(*refs):
        ys = refs[0:ROW_PLANES]
        ia_hbm, ib_hbm = refs[ROW_PLANES:ROW_PLANES + 2]
        outs_a = refs[ROW_PLANES + 2:2 * ROW_PLANES + 2]
        outs_b = refs[2 * ROW_PLANES + 2:3 * ROW_PLANES + 2]
        ia_v, ib_v, buf, sem = refs[3 * ROW_PLANES + 2:]
        wid = lax.axis_index("s") * n_cores + lax.axis_index("c")

        def step(t, carry):
            g = wid + t * n_workers

            @pl.when(g < n_win)
            def _():
                row0 = pl.multiple_of(g * SC_WINDOW, SC_WINDOW)
                idx_loads = [pltpu.async_copy(ia_hbm.at[g], ia_v, sem), pltpu.async_copy(ib_hbm.at[g], ib_v, sem)]
                for cp in idx_loads:
                    cp.wait()
                for iv, outs in ((ia_v, outs_a), (ib_v, outs_b)):
                    loads = [pltpu.async_copy(ys[c].at[iv], buf.at[c], sem) for c in range(ROW_PLANES)]
                    for cp in loads:
                        cp.wait()
                    stores = [pltpu.async_copy(buf.at[c], outs[c].at[pl.ds(row0, SC_WINDOW)], sem)
                              for c in range(ROW_PLANES)]
                    for cp in stores:
                        cp.wait()

            return carry

        lax.fori_loop(0, trips, step, 0)

    n_rows = n_win * SC_WINDOW
    kern = pl.kernel(
        body,
        out_type=[jax.ShapeDtypeStruct((n_rows, LANES), jnp.uint32)] * (2 * ROW_PLANES),
        mesh=mesh,
        scratch_types=[
            pltpu.VMEM((SC_WINDOW,), jnp.int32),
            pltpu.VMEM((SC_WINDOW,), jnp.int32),
            pltpu.VMEM((ROW_PLANES, SC_WINDOW, LANES), jnp.uint32),
            pltpu.SemaphoreType.DMA,
        ],
        name="moe_gather_sc",
    )
    res = kern(*planes, idx_a, idx_b)
    return res[:ROW_PLANES], res[ROW_PLANES:]


def _experts_kernel(be_ref, nv_ref, first_ref, slot_ref, nxt_ref, x0_ref, x1_ref, x2_ref, x3_ref,
                    wg_hbm, wu_hbm, wd_hbm, y0_ref, y1_ref, y2_ref, y3_ref,
                    wgf_ref, wuf_ref, wdf_ref, wgb_ref, wub_ref, wdb_ref, sem):
    i = pl.program_id(0)
    e = be_ref[i]
    slot = slot_ref[i]
    y_refs = (y0_ref, y1_ref, y2_ref, y3_ref)

    def weight_copies(expert, s):
        return [pltpu.make_async_copy(hbm.at[expert], stage.at[s], sem.at[s, j])
                for j, (hbm, stage) in enumerate(((wg_hbm, wgf_ref), (wu_hbm, wuf_ref), (wd_hbm, wdf_ref)))]

    @pl.when(i == 0)
    def _():
        for cp in weight_copies(e, slot):
            cp.start()

    @pl.when(first_ref[i] == 1)
    def _():
        for cp in weight_copies(e, slot):
            cp.wait()
        nxt = nxt_ref[i]

        @pl.when(nxt >= 0)
        def _():
            for cp in weight_copies(nxt, 1 - slot):
                cp.start()

        wgb_ref[...] = wgf_ref[slot].astype(BF16)
        wub_ref[...] = wuf_ref[slot].astype(BF16)
        wdb_ref[...] = wdf_ref[slot].astype(BF16)

    nv = nv_ref[i]

    def mlp(rows):
        xs = _unpack_rows([r[rows, :] for r in (x0_ref, x1_ref, x2_ref, x3_ref)])
        row = lax.broadcasted_iota(jnp.int32, xs.shape, 0)
        x = jnp.where(row < nv, xs, jnp.zeros_like(xs))
        hg = jnp.dot(x, wgb_ref[...], preferred_element_type=F32)
        hu = jnp.dot(x, wub_ref[...], preferred_element_type=F32)
        hid = (hg / (1.0 + jnp.exp(-hg)) * hu).astype(BF16)
        yw = _pack_rows(jnp.dot(hid, wdb_ref[...], preferred_element_type=F32))
        for c, ref in enumerate(y_refs):
            ref[rows, :] = yw[:, c * LANES:(c + 1) * LANES]

    def clear(rows):
        for ref in y_refs:
            ref[rows, :] = jnp.zeros((rows.size, LANES), ref.dtype)

    for piece in range(MOE_BLOCK // MOE_TAIL_STEP + 1):
        used = piece * MOE_TAIL_STEP

        @pl.when((nv > used - MOE_TAIL_STEP) & (nv <= used))
        def _():
            if used > 0:
                mlp(pl.ds(0, used))
            if used < MOE_BLOCK:
                clear(pl.ds(used, MOE_BLOCK - used))


def _experts(block_e, nvalid, xs_planes, w_gate, w_up, w_down):
    n_blocks = block_e.shape[0]
    _, d, ff = w_gate.shape
    first = jnp.concatenate([jnp.ones((1,), jnp.int32), (block_e[1:] != block_e[:-1]).astype(jnp.int32)])
    slot = (jnp.cumsum(first) - 1) % 2
    later = jnp.where(block_e[None, :] > block_e[:, None], block_e[None, :], N_EXPERTS)
    nxt = jnp.min(later, axis=1)
    nxt = jnp.where(nxt == N_EXPERTS, -1, nxt).astype(jnp.int32)
    blk = lambda i, *_: (i, 0)
    grid_spec = pltpu.PrefetchScalarGridSpec(
        num_scalar_prefetch=5,
        grid=(n_blocks,),
        in_specs=[
            *[pl.BlockSpec((MOE_BLOCK, LANES), blk)] * ROW_PLANES,
            pl.BlockSpec(memory_space=pl.ANY),
            pl.BlockSpec(memory_space=pl.ANY),
            pl.BlockSpec(memory_space=pl.ANY),
        ],
        out_specs=[pl.BlockSpec((MOE_BLOCK, LANES), blk)] * ROW_PLANES,
        scratch_shapes=[
            pltpu.VMEM((2, d, ff), F32),
            pltpu.VMEM((2, d, ff), F32),
            pltpu.VMEM((2, ff, d), F32),
            pltpu.VMEM((d, ff), BF16),
            pltpu.VMEM((d, ff), BF16),
            pltpu.VMEM((ff, d), BF16),
            pltpu.SemaphoreType.DMA((2, 3)),
        ],
    )
    return pl.pallas_call(
        _experts_kernel,
        grid_spec=grid_spec,
        out_shape=[jax.ShapeDtypeStruct((n_blocks * MOE_BLOCK, LANES), jnp.uint32)] * ROW_PLANES,
        compiler_params=_cparams(1),
        name="moe_experts",
    )(block_e, nvalid, first, slot.astype(jnp.int32), nxt, *xs_planes, w_gate, w_up, w_down)


def _combine_kernel(route_ref, h1_hbm, *refs, lp_len, tile):
    a_refs = refs[0:ROW_PLANES]
    b_refs = refs[ROW_PLANES:2 * ROW_PLANES]
    out_ref, hbuf, sem_h = refs[2 * ROW_PLANES:]
    nt = pl.num_programs(1)
    step = pl.program_id(0) * nt + pl.program_id(1)
    last = pl.num_programs(0) * nt - 1

    def h_copy(s, slot):
        start = (s // nt) * lp_len + N_META + (s % nt) * tile
        return pltpu.make_async_copy(h1_hbm.at[pl.ds(start, tile), :], hbuf.at[slot], sem_h.at[slot])

    slot = step % 2

    @pl.when(step == 0)
    def _():
        h_copy(step, slot).start()

    @pl.when(step < last)
    def _():
        h_copy(step + 1, 1 - slot).start()

    ya = _unpack_rows([r[...] for r in a_refs]).astype(F32)
    yb = _unpack_rows([r[...] for r in b_refs]).astype(F32)
    g = jnp.concatenate([route_ref[...]] * (LANES // ROUTE_COLS), axis=0).T
    moe = g[:, TOP_K:TOP_K + 1] * ya + g[:, TOP_K + 1:TOP_K + 2] * yb
    h_copy(step, slot).wait()
    out_ref[0] = hbuf[slot] + moe


def _combine(gates, h1, a_planes, b_planes, *, batch, seq, lp_len, tile):
    d = h1.shape[1]
    nt = seq // tile
    kern = functools.partial(_combine_kernel, lp_len=lp_len, tile=tile)
    rows = lambda b, i: (b * nt + i, 0)
    return pl.pallas_call(
        kern,
        grid=(batch, nt),
        in_specs=[
            pl.BlockSpec((ROUTE_COLS, tile), lambda b, i: (0, b * nt + i)),
            pl.BlockSpec(memory_space=pl.ANY),
            *[pl.BlockSpec((tile, LANES), rows)] * (2 * ROW_PLANES),
        ],
        out_specs=pl.BlockSpec((1, tile, d), lambda b, i: (b, i, 0)),
        out_shape=jax.ShapeDtypeStruct((batch, seq, d), F32),
        scratch_shapes=[pltpu.VMEM((2, tile, d), F32), pltpu.SemaphoreType.DMA((2,))],
        compiler_params=_cparams(2),
        name="moe_combine",
    )(gates, h1, *a_planes, *b_planes)


def _rope_tables(lp_len):
    half = ROPE_DIM // 2
    pos = jnp.arange(lp_len, dtype=F32)
    inv_freq = ROPE_THETA ** (-jnp.arange(0, ROPE_DIM, 2, dtype=F32) / ROPE_DIM)
    ang = pos[:, None] * inv_freq[None, :]
    lane = jnp.arange(LANES) % HEAD_DIM
    cos = jnp.tile(jnp.cos(ang), (1, LANES // half))
    sin = jnp.tile(jnp.sin(ang), (1, LANES // half))
    c = jnp.where(lane < ROPE_DIM, cos, 1.0)
    s1 = jnp.where((lane >= half) & (lane < ROPE_DIM), sin, 0.0)
    s2 = jnp.where(lane < half, -sin, 0.0)
    return c, s1, s2


def kernel(x, meta_tokens, norm1_g, w_in, conv_w, q_norm_g, k_norm_g, lambda_q1, lambda_k1, lambda_q2, lambda_k2,
           subln_g, w_out, norm2_g, w_router_group, b_router_group, w_router_expert, b_router_expert, w_gate,
           w_up, w_down):
    batch, seq, _ = x.shape
    assert w_in.shape[0] == 1, "a single layer is supported"
    l = 0
    length = seq + N_META
    tm = TOKEN_TILE
    lp_len = -(-length // tm) * tm
    tiles_per_seq = lp_len // tm
    assert tiles_per_seq >= 2 and (length - (tiles_per_seq - 1) * tm) % 8 == 0
    qw = N_HEADS * 2 * HEAD_DIM
    lam_init = 0.8 - 0.6 * math.exp(-0.3 * l)

    reps = qw // HEAD_DIM
    gq = jnp.tile(q_norm_g[l] * (HEAD_DIM ** -0.5 * LOG2E), reps)[None, :]
    gk = jnp.tile(k_norm_g[l], reps)[None, :]
    seg = jnp.arange(qw) // HEAD_DIM
    bd = (seg[:, None] == seg[None, :]).astype(BF16)
    rope = _rope_tables(lp_len)
    hp, convy, q, k, v = _inproj(x, meta_tokens.astype(x.dtype), norm1_g[l][None, :], w_in[l].astype(BF16),
                                 conv_w[l], gq, gk, bd, *rope, tiles_per_seq=tiles_per_seq, tm=tm)

    lamp = jnp.stack([lambda_q1[l], lambda_k1[l], lambda_q2[l], lambda_k2[l]]).astype(F32)
    o = _attention(q, k, v, lamp, subln_g[l][None, :], batch=batch, lp_len=lp_len, tq=tm, lam_init=lam_init)

    lane_pad = LANES - N_GROUPS - N_EXPERTS
    wr = jnp.pad(jnp.concatenate([w_router_group[l], w_router_expert[l]], axis=1), ((0, 0), (0, lane_pad)))
    wr_hi = wr.astype(BF16)
    wr_lo = (wr - wr_hi.astype(F32)).astype(BF16)
    br = jnp.pad(jnp.concatenate([b_router_group[l], b_router_expert[l]]), (0, lane_pad))[None, :]
    ridx = jnp.arange(tm)
    upper = (ridx[:, None] < ridx[None, :]).astype(BF16)
    h1, *rest = _outproj(hp, convy, o, w_out[l].astype(BF16), norm2_g[l][None, :], wr_hi, wr_lo, br, upper,
                         tiles_per_seq=tiles_per_seq, seq_len=length)
    x_planes = rest[:ROW_PLANES]
    route, cnt = rest[ROW_PLANES:]

    counts = cnt[EXPERT_LANE0:EXPERT_LANE0 + N_EXPERTS, 0].astype(jnp.int32)
    n_blocks = -(-(batch * length * TOP_K) // MOE_BLOCK) + N_EXPERTS
    p_rows = n_blocks * MOE_BLOCK
    padded = (counts + MOE_BLOCK - 1) // MOE_BLOCK * MOE_BLOCK
    pends = jnp.cumsum(padded)
    pstarts = pends - padded

    def lookup(table, idx):
        hit = idx[:, None] == jnp.arange(N_EXPERTS, dtype=jnp.int32)[None, :]
        return jnp.sum(jnp.where(hit, table[None, :], 0), axis=1)

    blk0 = jnp.arange(n_blocks, dtype=jnp.int32) * MOE_BLOCK
    block_e = jnp.minimum(jnp.sum((pends[None, :] <= blk0[:, None]).astype(jnp.int32), axis=1), N_EXPERTS - 1)
    nvalid = jnp.clip(lookup(counts, block_e) - (blk0 - lookup(pstarts, block_e)), 0, MOE_BLOCK)

    spare_rows = -(-(batch * (lp_len - length) * TOP_K) // MOE_BLOCK) * MOE_BLOCK
    dest = _slots(pstarts.astype(jnp.int32), route, batch=batch, lp_len=lp_len, seq_len=length, p_rows=p_rows)
    dest = dest.reshape(ROUTE_COLS, batch, lp_len)

    assert (batch * lp_len) % SC_WINDOW == 0 and (batch * seq) % SC_WINDOW == 0
    xs_planes = _sc_scatter_rows(x_planes, dest[0].reshape(-1, SC_WINDOW), dest[1].reshape(-1, SC_WINDOW),
                                 p_rows + spare_rows)
    y_planes = _experts(block_e, nvalid, xs_planes, w_gate[l], w_up[l], w_down[l])

    dest_x = dest[0:TOP_K, :, N_META:length]
    a_planes, b_planes = _sc_gather_rows(y_planes, dest_x[0].reshape(-1, SC_WINDOW),
                                         dest_x[1].reshape(-1, SC_WINDOW))
    route_x = route.reshape(ROUTE_COLS, batch, lp_len)[:, :, N_META:length].reshape(ROUTE_COLS, batch * seq)
    return _combine(route_x, h1, a_planes, b_planes, batch=batch, seq=seq, lp_len=lp_len,
                    tile=_largest_tile(seq, 512, LANES))
```

```python
import functools
import math

import jax
import jax.numpy as jnp
from jax import lax
from jax.experimental import pallas as pl
from jax.experimental.pallas import tpu as pltpu
from jax.experimental.pallas import tpu_sc as plsc

F32 = jnp.float32
BF16 = jnp.bfloat16

N_META = 16
N_HEADS = 4
HEAD_DIM = 64
ROPE_DIM = HEAD_DIM // 4
ROPE_THETA = 500000.0
N_GROUPS = 4
EXPERTS_PER_GROUP = 8
N_EXPERTS = N_GROUPS * EXPERTS_PER_GROUP
TOP_K = 2
EPS = 1e-6
LOG2E = 1.4426950408889634

LANES = 128
TOKEN_TILE = 640
INPROJ_UNITS = 2
OUTPROJ_UNITS = 2
ATTN_HEADS_PER_STEP = 4
MOE_BLOCK = 1024
MOE_TAIL_STEP = 256
COMBINE_TILE = 1024
ROUTE_COLS = 8
ROUTE_ROWS = 64
ROW_PLANES = 4
SC_WINDOW = 128
EXPERT_LANE0 = N_GROUPS
NEG_BIG = -1e30
VMEM_LIMIT = 56 * 1024 * 1024


def _largest_tile(n, cap, mult):
    for t in range(min(cap, n), 0, -1):
        if n % t == 0 and t % mult == 0:
            return t
    raise ValueError(f"no tile for {n}")


def _cparams(n_axes):
    return pltpu.CompilerParams(dimension_semantics=("arbitrary",) * n_axes, vmem_limit_bytes=VMEM_LIMIT)


def _pack_rows(x):
    w = x.shape[1] // 2
    lo = lax.bitcast_convert_type(x[:, :w].astype(BF16).astype(F32), jnp.uint32)
    hi = lax.bitcast_convert_type(x[:, w:].astype(BF16).astype(F32), jnp.uint32)
    return lax.shift_right_logical(lo, jnp.uint32(16)) | (hi & jnp.uint32(0xFFFF0000))


def _unpack_rows(planes):
    w = jnp.concatenate(planes, axis=1)
    lo = lax.bitcast_convert_type(lax.shift_left(w, jnp.uint32(16)), F32)
    hi = lax.bitcast_convert_type(w & jnp.uint32(0xFFFF0000), F32)
    return jnp.concatenate([lo, hi], axis=1).astype(BF16)


def _inproj_kernel(x_hbm, meta_hbm, g1_ref, win_ref, convw_ref, gq_ref, gk_ref, bd_ref, *refs,
                   tiles_per_seq, seq, cw, qw, units):
    rope_refs = refs[0:3 * units]
    hp_hbm, convy_ref, q_ref, k_ref, v_ref, carry_ref, xbuf, sem, sem_out = refs[3 * units:]
    i = pl.program_id(0)
    n_steps = pl.num_programs(0)
    tm = xbuf.shape[1]
    q0 = 3 * cw
    w = convw_ref[...]
    last_rows = seq + N_META - (tiles_per_seq - 1) * tm

    def fetch(tile, slot, start):
        b = tile // tiles_per_seq
        t = tile % tiles_per_seq

        def go(src, dst):
            cp = pltpu.make_async_copy(src, dst, sem.at[slot])
            if start:
                cp.start()
            else:
                cp.wait()

        @pl.when(t == 0)
        def _():
            go(meta_hbm, xbuf.at[slot, pl.ds(0, N_META)])
            go(x_hbm.at[pl.ds(b * seq, tm - N_META)], xbuf.at[slot, pl.ds(N_META, tm - N_META)])

        @pl.when((t > 0) & (t < tiles_per_seq - 1))
        def _():
            go(x_hbm.at[pl.ds(b * seq + t * tm - N_META, tm)], xbuf.at[slot])

        @pl.when(t == tiles_per_seq - 1)
        def _():
            go(x_hbm.at[pl.ds(b * seq + t * tm - N_META, last_rows)], xbuf.at[slot, pl.ds(0, last_rows)])

    def hp_store(tile, slot):
        return pltpu.make_async_copy(xbuf.at[slot], hp_hbm.at[pl.ds(tile * tm, tm)], sem_out.at[slot])

    mine = (i % 2) * units
    other = units - mine

    @pl.when(i == 0)
    def _():
        for u in range(units):
            fetch(u, mine + u, True)

    @pl.when(i > 0)
    def _():
        for u in range(units):
            hp_store((i - 1) * units + u, other + u).wait()

    @pl.when(i + 1 < n_steps)
    def _():
        for u in range(units):
            fetch((i + 1) * units + u, other + u, True)

    for u in range(units):
        tile = i * units + u
        fetch(tile, mine + u, False)

        @pl.when(tile % tiles_per_seq == tiles_per_seq - 1)
        def _():
            xbuf[mine + u, pl.ds(last_rows, tm - last_rows), :] = jnp.zeros((tm - last_rows, xbuf.shape[2]),
                                                                           xbuf.dtype)

        hp_store(tile, mine + u).start()

    prev = carry_ref[...]

    for u in range(units):
        rows = pl.ds(u * tm, tm)
        x = xbuf[mine + u]
        ms = jnp.mean(x * x, axis=-1, keepdims=True)
        xn = (x * lax.rsqrt(ms + EPS) * g1_ref[...]).astype(BF16)

        def proj(lo, hi):
            return jnp.dot(xn, win_ref[:, lo:hi], preferred_element_type=F32)

        u_conv = proj(0, q0)
        u_q = proj(q0, q0 + qw)

        z = u_conv[:, cw:2 * cw] * u_conv[:, 2 * cw:3 * cw]
        prev = jnp.where((i * units + u) % tiles_per_seq == 0, 0.0, prev)
        p1 = prev[7:8]
        p2 = prev[6:7]
        row = lax.broadcasted_iota(jnp.int32, z.shape, 0)
        z1 = jnp.where(row == 0, p1, pltpu.roll(z, 1, axis=0))
        z2 = jnp.where(row == 0, p2, jnp.where(row == 1, p1, pltpu.roll(z, 2, axis=0)))
        prev = z[tm - 8:tm]
        conv = w[0:1] * z2 + w[1:2] * z1 + w[2:3] * z
        convy_ref[rows, :] = (u_conv[:, 0:cw] * conv).astype(BF16)

        rc, rs1, rs2 = (ref[...] for ref in rope_refs[3 * u:3 * u + 3])

        def norm_rope(t, g_ref):
            ss = jnp.dot((t * t).astype(BF16), bd_ref[...], preferred_element_type=F32)
            tn = t * lax.rsqrt(ss * (1.0 / HEAD_DIM) + EPS) * g_ref[...]
            outs = []
            for c in range(qw // LANES):
                ch = tn[:, c * LANES:(c + 1) * LANES]
                outs.append(ch * rc + pltpu.roll(ch, ROPE_DIM // 2, axis=1) * rs1
                            + pltpu.roll(ch, LANES - ROPE_DIM // 2, axis=1) * rs2)
            return jnp.concatenate(outs, axis=1).astype(BF16)

        u_k = proj(q0 + qw, q0 + 2 * qw)
        q_ref[rows, :] = norm_rope(u_q, gq_ref)
        u_v = proj(q0 + 2 * qw, win_ref.shape[1])
        k_ref[rows, :] = norm_rope(u_k, gk_ref)
        v_ref[rows, :] = u_v.astype(BF16)

    carry_ref[...] = prev

    @pl.when(i == n_steps - 1)
    def _():
        for u in range(units):
            hp_store(i * units + u, mine + u).wait()


def _inproj(x, meta, g1, w_in, conv_w, gq, gk, bd, rc, rs1, rs2, *, tiles_per_seq, tm):
    batch, seq, d = x.shape
    units = INPROJ_UNITS
    n = batch * tiles_per_seq * tm
    assert (batch * tiles_per_seq) % units == 0
    cw = conv_w.shape[1]
    qw = gq.shape[1]
    aw = w_in.shape[1] - 3 * cw - 2 * qw
    const = lambda i: (0, 0)
    step = lambda i: (i, 0)
    rope_specs = [pl.BlockSpec((tm, LANES), lambda i, u=u: ((i * units + u) % tiles_per_seq, 0))
                  for u in range(units) for _ in range(3)]
    kern = functools.partial(_inproj_kernel, tiles_per_seq=tiles_per_seq, seq=seq, cw=cw, qw=qw, units=units)
    return pl.pallas_call(
        kern,
        grid=(n // (units * tm),),
        in_specs=[
            pl.BlockSpec(memory_space=pl.ANY),
            pl.BlockSpec(memory_space=pl.ANY),
            pl.BlockSpec((1, d), const),
            pl.BlockSpec(w_in.shape, const),
            pl.BlockSpec(conv_w.shape, const),
            pl.BlockSpec((1, qw), const),
            pl.BlockSpec((1, qw), const),
            pl.BlockSpec(bd.shape, const),
            *rope_specs,
        ],
        out_specs=[
            pl.BlockSpec(memory_space=pl.ANY),
            pl.BlockSpec((units * tm, cw), step),
            pl.BlockSpec((units * tm, qw), step),
            pl.BlockSpec((units * tm, qw), step),
            pl.BlockSpec((units * tm, aw), step),
        ],
        out_shape=[
            jax.ShapeDtypeStruct((n, d), x.dtype),
            jax.ShapeDtypeStruct((n, cw), BF16),
            jax.ShapeDtypeStruct((n, qw), BF16),
            jax.ShapeDtypeStruct((n, qw), BF16),
            jax.ShapeDtypeStruct((n, aw), BF16),
        ],
        scratch_shapes=[pltpu.VMEM((8, cw), F32), pltpu.VMEM((2 * units, tm, d), x.dtype),
                        pltpu.SemaphoreType.DMA((2 * units,)), pltpu.SemaphoreType.DMA((2 * units,))],
        compiler_params=_cparams(1),
        name="inproj",
    )(x.reshape(batch * seq, d), meta, g1, w_in, conv_w, gq, gk, bd, *([rc, rs1, rs2] * units))


def _attn_kernel(q_ref, k_ref, v_ref, lamp_ref, sg_ref, o_ref, qs_ref, m_ref, l_ref, acc_ref, *, lam_init):
    qi = pl.program_id(2)
    tq = q_ref.shape[0]
    n_heads = q_ref.shape[1] // LANES
    n_chains = 2 * n_heads
    lane = lax.broadcasted_iota(jnp.int32, (tq, LANES), 1)
    for h in range(n_heads):
        q = q_ref[:, h * LANES:(h + 1) * LANES]
        zero = jnp.zeros_like(q)
        qs_ref[pl.ds(2 * h * tq, tq), :] = jnp.where(lane < HEAD_DIM, q, zero)
        qs_ref[pl.ds((2 * h + 1) * tq, tq), :] = jnp.where(lane >= HEAD_DIM, q, zero)
    m_ref[...] = jnp.full_like(m_ref, NEG_BIG)
    l_ref[...] = jnp.zeros_like(l_ref)
    acc_ref[...] = jnp.zeros_like(acc_ref)

    def scores(off, width, which):
        h = which // 2
        kc = k_ref[pl.ds(off, width), h * LANES:(h + 1) * LANES]
        return lax.dot_general(qs_ref[pl.ds(which * tq, tq), :], kc, (((1,), (1,)), ((), ())),
                               preferred_element_type=F32)

    def update(off, width, which, s, masked):
        h = which // 2
        vc = jnp.concatenate([v_ref[pl.ds(off, width), h * LANES:(h + 1) * LANES],
                              jnp.ones((width, LANES), BF16)], axis=1)
        rows = pl.ds(which * tq, tq)
        if masked:
            r = lax.broadcasted_iota(jnp.int32, s.shape, 0)
            c = lax.broadcasted_iota(jnp.int32, s.shape, 1)
            s = jnp.where(c <= r + (width - tq), s, NEG_BIG)
        m_prev = m_ref[rows, :]
        m_new = jnp.maximum(m_prev, jnp.max(s, axis=-1, keepdims=True))
        alpha = jnp.exp2(m_prev - m_new)
        p = jnp.exp2((s - jnp.tile(m_new, (1, width // LANES))).astype(BF16))
        pv = jnp.dot(p, vc, preferred_element_type=F32)
        l_ref[rows, :] = alpha * l_ref[rows, :] + pv[:, LANES:]
        acc_ref[rows, :] = alpha * acc_ref[rows, :] + pv[:, :LANES]
        m_ref[rows, :] = m_new

    def chunk(off, width, masked):
        s_next = scores(off, width, 0)
        for c in range(n_chains):
            s = s_next
            if c + 1 < n_chains:
                s_next = scores(off, width, c + 1)
            update(off, width, c, s, masked)

    wide = 2 * tq

    def body(j, carry):
        chunk(pl.multiple_of(j * wide, wide), wide, False)
        return carry

    lax.fori_loop(0, qi // 2, body, 0)
    odd = qi % 2 == 1

    @pl.when(odd)
    def _():
        chunk(pl.multiple_of((qi - 1) * tq, tq), wide, True)

    @pl.when(jnp.logical_not(odd))
    def _():
        chunk(pl.multiple_of(qi * tq, tq), tq, True)

    lp = lamp_ref[...]
    lam = (jnp.exp(jnp.sum(lp[0:1] * lp[1:2], axis=-1, keepdims=True))
           - jnp.exp(jnp.sum(lp[2:3] * lp[3:4], axis=-1, keepdims=True)) + lam_init)
    for h in range(n_heads):
        rows = pl.ds(2 * h * tq, 2 * tq)
        o_all = acc_ref[rows, :] / l_ref[rows, :]
        o = o_all[0:tq] - lam * o_all[tq:2 * tq]
        ms = jnp.mean(o * o, axis=-1, keepdims=True)
        o_ref[:, h * LANES:(h + 1) * LANES] = (o * lax.rsqrt(ms + EPS) * sg_ref[...]
                                               * (1.0 - lam_init)).astype(BF16)


def _attention(q, k, v, lamp, sg, *, batch, lp_len, tq, lam_init):
    n, qw = q.shape
    nq = lp_len // tq
    hw = ATTN_HEADS_PER_STEP * LANES
    chains = 2 * ATTN_HEADS_PER_STEP
    kern = functools.partial(_attn_kernel, lam_init=lam_init)
    return pl.pallas_call(
        kern,
        grid=(batch, qw // hw, nq),
        in_specs=[
            pl.BlockSpec((tq, hw), lambda b, h, i: (b * nq + i, h)),
            pl.BlockSpec((lp_len, hw), lambda b, h, i: (b, h)),
            pl.BlockSpec((lp_len, hw), lambda b, h, i: (b, h)),
            pl.BlockSpec(lamp.shape, lambda b, h, i: (0, 0)),
            pl.BlockSpec(sg.shape, lambda b, h, i: (0, 0)),
        ],
        out_specs=pl.BlockSpec((tq, hw), lambda b, h, i: (b * nq + i, h)),
        out_shape=jax.ShapeDtypeStruct((n, v.shape[1]), BF16),
        scratch_shapes=[
            pltpu.VMEM((chains * tq, LANES), BF16),
            pltpu.VMEM((chains * tq, LANES), F32),
            pltpu.VMEM((chains * tq, LANES), F32),
            pltpu.VMEM((chains * tq, LANES), F32),
        ],
        compiler_params=_cparams(3),
        name="diffattn",
    )(q, k, v, lamp, sg)


def _outproj_kernel(hp_ref, cy_ref, o_ref, wout_ref, g2_ref, wrh_ref, wrl_ref, br_ref, upper_ref,
                    h1_ref, xp0_ref, xp1_ref, xp2_ref, xp3_ref, route_ref, cnt_ref, run_ref,
                    *, tiles_per_seq, seq_len):
    i = pl.program_id(0)
    tm = upper_ref.shape[0]
    n_units = hp_ref.shape[0] // tm

    @pl.when(i == 0)
    def _():
        run_ref[...] = jnp.zeros_like(run_ref)

    def project(rows):
        mix = jnp.concatenate([cy_ref[rows, :], o_ref[rows, :]], axis=1)
        h1 = hp_ref[rows, :] + jnp.dot(mix, wout_ref[...], preferred_element_type=F32)
        h1_ref[rows, :] = h1
        ms = jnp.mean(h1 * h1, axis=-1, keepdims=True)
        xn = h1 * lax.rsqrt(ms + EPS) * g2_ref[...]
        xw = _pack_rows(xn)
        for c, ref in enumerate((xp0_ref, xp1_ref, xp2_ref, xp3_ref)):
            ref[rows, :] = xw[:, c * LANES:(c + 1) * LANES]
        x_hi = xn.astype(BF16)
        x_lo = (xn - x_hi.astype(F32)).astype(BF16)
        hi_both = jnp.dot(x_hi, jnp.concatenate([wrh_ref[...], wrl_ref[...]], axis=1),
                          preferred_element_type=F32)
        return (hi_both[:, :LANES] + hi_both[:, LANES:]
                + jnp.dot(x_lo, wrh_ref[...], preferred_element_type=F32) + br_ref[...])

    def route(u, logits, run):
        lt = logits.T[0:ROUTE_ROWS, :]
        row = lax.broadcasted_iota(jnp.int32, lt.shape, 0)
        big = jnp.int32(4 * LANES)

        def first_argmax(vals, vmax):
            return jnp.min(jnp.where(vals == vmax, row, big), axis=0, keepdims=True)

        gl = jnp.where(row < N_GROUPS, lt, NEG_BIG)
        gmax = jnp.max(gl, axis=0, keepdims=True)
        g_val = 1.0 / jnp.sum(jnp.exp(gl - gmax), axis=0, keepdims=True)
        g_idx = first_argmax(gl, gmax)
        lo = EXPERT_LANE0 + EXPERTS_PER_GROUP * g_idx
        el = jnp.where((row >= lo) & (row < lo + EXPERTS_PER_GROUP), lt, NEG_BIG)
        m1 = jnp.max(el, axis=0, keepdims=True)
        i1 = first_argmax(el, m1)
        el2 = jnp.where(row == i1, NEG_BIG, el)
        m2 = jnp.max(el2, axis=0, keepdims=True)
        i2 = first_argmax(el2, m2)
        r = jnp.exp(m2 - m1)
        gate1 = g_val / (1.0 + r)
        gate2 = g_val * r / (1.0 + r)

        pos = ((i * n_units + u) % tiles_per_seq) * tm + lax.broadcasted_iota(jnp.int32, (1, tm), 1)
        valid = pos < seq_len
        oh1 = jnp.where(valid & (row == i1), 1.0, 0.0)
        oh2 = jnp.where(valid & (row == i2), 1.0, 0.0)
        pre = jnp.dot(jnp.concatenate([oh1, oh2], axis=0).astype(BF16), upper_ref[...],
                      preferred_element_type=F32)
        tot1 = jnp.sum(oh1, axis=1, keepdims=True)
        tot2 = jnp.sum(oh2, axis=1, keepdims=True)
        run_t = jnp.tile(run, (1, tm // LANES))
        rank1 = jnp.sum(oh1 * (pre[:ROUTE_ROWS] + run_t), axis=0, keepdims=True)
        rank2 = jnp.sum(oh2 * (pre[ROUTE_ROWS:] + run_t + tot1), axis=0, keepdims=True)

        e1 = (i1 - EXPERT_LANE0).astype(F32)
        e2 = (i2 - EXPERT_LANE0).astype(F32)
        r8 = lax.broadcasted_iota(jnp.int32, (ROUTE_COLS, tm), 0)
        route_ref[:, u * tm:(u + 1) * tm] = jnp.where(r8 == 0, e1, jnp.where(r8 == 1, e2, jnp.where(
            r8 == 2, gate1, jnp.where(r8 == 3, gate2, jnp.where(r8 == 4, rank1, jnp.where(r8 == 5, rank2, 0.0))))))
        return run + tot1 + tot2

    all_logits = [project(pl.ds(u * tm, tm)) for u in range(n_units)]
    run = run_ref[...]
    for u, logits in enumerate(all_logits):
        run = route(u, logits, run)
    run_ref[...] = run
    cnt_ref[...] = run


def _outproj(hp, convy, o, w_out, g2, wr_hi, wr_lo, br, upper, *, tiles_per_seq, seq_len):
    n, d = hp.shape
    tm = OUTPROJ_UNITS * upper.shape[0]
    assert n % tm == 0
    const = lambda i: (0, 0)
    tile = lambda i: (i, 0)
    kern = functools.partial(_outproj_kernel, tiles_per_seq=tiles_per_seq, seq_len=seq_len)
    return pl.pallas_call(
        kern,
        grid=(n // tm,),
        in_specs=[
            pl.BlockSpec((tm, d), tile),
            pl.BlockSpec((tm, convy.shape[1]), tile),
            pl.BlockSpec((tm, o.shape[1]), tile),
            pl.BlockSpec(w_out.shape, const),
            pl.BlockSpec((1, d), const),
            pl.BlockSpec(wr_hi.shape, const),
            pl.BlockSpec(wr_lo.shape, const),
            pl.BlockSpec((1, LANES), const),
            pl.BlockSpec(upper.shape, const),
        ],
        out_specs=[
            pl.BlockSpec((tm, d), tile),
            *[pl.BlockSpec((tm, LANES), tile)] * ROW_PLANES,
            pl.BlockSpec((ROUTE_COLS, tm), lambda i: (0, i)),
            pl.BlockSpec((ROUTE_ROWS, LANES), const),
        ],
        out_shape=[
            jax.ShapeDtypeStruct((n, d), F32),
            *[jax.ShapeDtypeStruct((n, LANES), jnp.uint32)] * ROW_PLANES,
            jax.ShapeDtypeStruct((ROUTE_COLS, n), F32),
            jax.ShapeDtypeStruct((ROUTE_ROWS, LANES), F32),
        ],
        scratch_shapes=[pltpu.VMEM((ROUTE_ROWS, LANES), F32)],
        compiler_params=_cparams(1),
        name="outproj_router",
    )(hp, convy, o, w_out, g2, wr_hi, wr_lo, br, upper)


def _slots_kernel(pstart_ref, route_ref, dest_ref, *, seq_len, p_rows):
    b = pl.program_id(0)
    route = route_ref[...]
    eid = route.astype(jnp.int32)
    start = jnp.zeros_like(eid)
    for e in range(N_EXPERTS):
        start = jnp.where(eid == e, pstart_ref[e], start)
    rank = pltpu.roll(route, ROUTE_COLS - 2 * TOP_K, axis=0).astype(jnp.int32)
    k = lax.broadcasted_iota(jnp.int32, route.shape, 0)
    pos = lax.broadcasted_iota(jnp.int32, route.shape, 1)
    n_pad = route.shape[1] - seq_len
    spare = p_rows + (b * n_pad + (pos - seq_len)) * TOP_K + k
    dest_ref[...] = jnp.where(pos < seq_len, start + rank, spare)


def _slots(pstarts, route, *, batch, lp_len, seq_len, p_rows):
    kern = functools.partial(_slots_kernel, seq_len=seq_len, p_rows=p_rows)
    grid_spec = pltpu.PrefetchScalarGridSpec(
        num_scalar_prefetch=1,
        grid=(batch,),
        in_specs=[pl.BlockSpec((ROUTE_COLS, lp_len), lambda b, ps: (0, b))],
        out_specs=pl.BlockSpec((ROUTE_COLS, lp_len), lambda b, ps: (0, b)),
    )
    return pl.pallas_call(
        kern,
        grid_spec=grid_spec,
        out_shape=jax.ShapeDtypeStruct(route.shape, jnp.int32),
        compiler_params=_cparams(1),
        name="moe_slots",
    )(pstarts, route)


def _sc_workers():
    info = plsc.get_sparse_core_info()
    return info.num_cores, info.num_cores * info.num_subcores


def _sc_scatter_rows(planes, idx_a, idx_b, out_rows):
    n_win = idx_a.shape[0]
    n_cores, n_workers = _sc_workers()
    trips = -(-n_win // n_workers)
    mesh = plsc.VectorSubcoreMesh(core_axis_name="c", subcore_axis_name="s")

    def body(*refs):
        xs = refs[0:ROW_PLANES]
        ia_hbm, ib_hbm = refs[ROW_PLANES:ROW_PLANES + 2]
        outs = refs[ROW_PLANES + 2:2 * ROW_PLANES + 2]
        ia_v, ib_v, buf, sem = refs[2 * ROW_PLANES + 2:]
        wid = lax.axis_index("s") * n_cores + lax.axis_index("c")

        def step(t, carry):
            g = wid + t * n_workers

            @pl.when(g < n_win)
            def _():
                row0 = pl.multiple_of(g * SC_WINDOW, SC_WINDOW)
                loads = [pltpu.async_copy(ia_hbm.at[g], ia_v, sem), pltpu.async_copy(ib_hbm.at[g], ib_v, sem)]
                loads += [pltpu.async_copy(xs[c].at[pl.ds(row0, SC_WINDOW)], buf.at[c], sem)
                          for c in range(ROW_PLANES)]
                for cp in loads:
                    cp.wait()
                stores = [pltpu.async_copy(buf.at[c], outs[c].at[iv], sem)
                          for c in range(ROW_PLANES) for iv in (ia_v, ib_v)]
                for cp in stores:
                    cp.wait()

            return carry

        lax.fori_loop(0, trips, step, 0)

    kern = pl.kernel(
        body,
        out_type=[jax.ShapeDtypeStruct((out_rows, LANES), jnp.uint32)] * ROW_PLANES,
        mesh=mesh,
        scratch_types=[
            pltpu.VMEM((SC_WINDOW,), jnp.int32),
            pltpu.VMEM((SC_WINDOW,), jnp.int32),
            pltpu.VMEM((ROW_PLANES, SC_WINDOW, LANES), jnp.uint32),
            pltpu.SemaphoreType.DMA,
        ],
        name="moe_dispatch_sc",
    )
    return kern(*planes, idx_a, idx_b)


def _sc_gather_rows(planes, idx_a, idx_b):
    n_win = idx_a.shape[0]
    n_cores, n_workers = _sc_workers()
    trips = -(-n_win // n_workers)
    mesh = plsc.VectorSubcoreMesh(core_axis_name="c", subcore_axis_name="s")

    def body(*refs):
        ys = refs[0:ROW_PLANES]
        ia_hbm, ib_hbm = refs[ROW_PLANES:ROW_PLANES + 2]
        outs_a = refs[ROW_PLANES + 2:2 * ROW_PLANES + 2]
        outs_b = refs[2 * ROW_PLANES + 2:3 * ROW_PLANES + 2]
        ia_v, ib_v, buf, sem = refs[3 * ROW_PLANES + 2:]
        wid = lax.axis_index("s") * n_cores + lax.axis_index("c")

        def step(t, carry):
            g = wid + t * n_workers

            @pl.when(g < n_win)
            def _():
                row0 = pl.multiple_of(g * SC_WINDOW, SC_WINDOW)
                idx_loads = [pltpu.async_copy(ia_hbm.at[g], ia_v, sem), pltpu.async_copy(ib_hbm.at[g], ib_v, sem)]
                for cp in idx_loads:
                    cp.wait()
                for iv, outs in ((ia_v, outs_a), (ib_v, outs_b)):
                    loads = [pltpu.async_copy(ys[c].at[iv], buf.at[c], sem) for c in range(ROW_PLANES)]
                    for cp in loads:
                        cp.wait()
                    stores = [pltpu.async_copy(buf.at[c], outs[c].at[pl.ds(row0, SC_WINDOW)], sem)
                              for c in range(ROW_PLANES)]
                    for cp in stores:
                        cp.wait()

            return carry

        lax.fori_loop(0, trips, step, 0)

    n_rows = n_win * SC_WINDOW
    kern = pl.kernel(
        body,
        out_type=[jax.ShapeDtypeStruct((n_rows, LANES), jnp.uint32)] * (2 * ROW_PLANES),
        mesh=mesh,
        scratch_types=[
            pltpu.VMEM((SC_WINDOW,), jnp.int32),
            pltpu.VMEM((SC_WINDOW,), jnp.int32),
            pltpu.VMEM((ROW_PLANES, SC_WINDOW, LANES), jnp.uint32),
            pltpu.SemaphoreType.DMA,
        ],
        name="moe_gather_sc",
    )
    res = kern(*planes, idx_a, idx_b)
    return res[:ROW_PLANES], res[ROW_PLANES:]


def _experts_kernel(be_ref, nv_ref, first_ref, slot_ref, nxt_ref, x0_ref, x1_ref, x2_ref, x3_ref,
                    wg_hbm, wu_hbm, wd_hbm, y0_ref, y1_ref, y2_ref, y3_ref,
                    wgf_ref, wuf_ref, wdf_ref, wgb_ref, wub_ref, wdb_ref, sem):
    i = pl.program_id(0)
    e = be_ref[i]
    slot = slot_ref[i]
    y_refs = (y0_ref, y1_ref, y2_ref, y3_ref)

    def weight_copies(expert, s):
        return [pltpu.make_async_copy(hbm.at[expert], stage.at[s], sem.at[s, j])
                for j, (hbm, stage) in enumerate(((wg_hbm, wgf_ref), (wu_hbm, wuf_ref), (wd_hbm, wdf_ref)))]

    @pl.when(i == 0)
    def _():
        for cp in weight_copies(e, slot):
            cp.start()

    @pl.when(first_ref[i] == 1)
    def _():
        for cp in weight_copies(e, slot):
            cp.wait()
        nxt = nxt_ref[i]

        @pl.when(nxt >= 0)
        def _():
            for cp in weight_copies(nxt, 1 - slot):
                cp.start()

        wgb_ref[...] = wgf_ref[slot].astype(BF16)
        wub_ref[...] = wuf_ref[slot].astype(BF16)
        wdb_ref[...] = wdf_ref[slot].astype(BF16)

    nv = nv_ref[i]

    def mlp(rows):
        xs = _unpack_rows([r[rows, :] for r in (x0_ref, x1_ref, x2_ref, x3_ref)])
        row = lax.broadcasted_iota(jnp.int32, xs.shape, 0)
        x = jnp.where(row < nv, xs, jnp.zeros_like(xs))
        hg = jnp.dot(x, wgb_ref[...], preferred_element_type=F32)
        hu = jnp.dot(x, wub_ref[...], preferred_element_type=F32)
        hid = (hg / (1.0 + jnp.exp(-hg)) * hu).astype(BF16)
        yw = _pack_rows(jnp.dot(hid, wdb_ref[...], preferred_element_type=F32))
        for c, ref in enumerate(y_refs):
            ref[rows, :] = yw[:, c * LANES:(c + 1) * LANES]

    def clear(rows):
        for ref in y_refs:
            ref[rows, :] = jnp.zeros((rows.size, LANES), ref.dtype)

    for piece in range(MOE_BLOCK // MOE_TAIL_STEP + 1):
        used = piece * MOE_TAIL_STEP

        @pl.when((nv > used - MOE_TAIL_STEP) & (nv <= used))
        def _():
            if used > 0:
                mlp(pl.ds(0, used))
            if used < MOE_BLOCK:
                clear(pl.ds(used, MOE_BLOCK - used))


def _experts(block_e, nvalid, xs_planes, w_gate, w_up, w_down):
    n_blocks = block_e.shape[0]
    _, d, ff = w_gate.shape
    first = jnp.concatenate([jnp.ones((1,), jnp.int32), (block_e[1:] != block_e[:-1]).astype(jnp.int32)])
    slot = (jnp.cumsum(first) - 1) % 2
    later = jnp.where(block_e[None, :] > block_e[:, None], block_e[None, :], N_EXPERTS)
    nxt = jnp.min(later, axis=1)
    nxt = jnp.where(nxt == N_EXPERTS, -1, nxt).astype(jnp.int32)
    blk = lambda i, *_: (i, 0)
    grid_spec = pltpu.PrefetchScalarGridSpec(
        num_scalar_prefetch=5,
        grid=(n_blocks,),
        in_specs=[
            *[pl.BlockSpec((MOE_BLOCK, LANES), blk)] * ROW_PLANES,
            pl.BlockSpec(memory_space=pl.ANY),
            pl.BlockSpec(memory_space=pl.ANY),
            pl.BlockSpec(memory_space=pl.ANY),
        ],
        out_specs=[pl.BlockSpec((MOE_BLOCK, LANES), blk)] * ROW_PLANES,
        scratch_shapes=[
            pltpu.VMEM((2, d, ff), F32),
            pltpu.VMEM((2, d, ff), F32),
            pltpu.VMEM((2, ff, d), F32),
            pltpu.VMEM((d, ff), BF16),
            pltpu.VMEM((d, ff), BF16),
            pltpu.VMEM((ff, d), BF16),
            pltpu.SemaphoreType.DMA((2, 3)),
        ],
    )
    return pl.pallas_call(
        _experts_kernel,
        grid_spec=grid_spec,
        out_shape=[jax.ShapeDtypeStruct((n_blocks * MOE_BLOCK, LANES), jnp.uint32)] * ROW_PLANES,
        compiler_params=_cparams(1),
        name="moe_experts",
    )(block_e, nvalid, first, slot.astype(jnp.int32), nxt, *xs_planes, w_gate, w_up, w_down)


def _combine_kernel(route_ref, h1_hbm, *refs, lp_len, tile):
    a_refs = refs[0:ROW_PLANES]
    b_refs = refs[ROW_PLANES:2 * ROW_PLANES]
    out_ref, hbuf, sem_h = refs[2 * ROW_PLANES:]
    nt = pl.num_programs(1)
    step = pl.program_id(0) * nt + pl.program_id(1)
    last = pl.num_programs(0) * nt - 1

    def h_copy(s, slot):
        start = (s // nt) * lp_len + N_META + (s % nt) * tile
        return pltpu.make_async_copy(h1_hbm.at[pl.ds(start, tile), :], hbuf.at[slot], sem_h.at[slot])

    slot = step % 2

    @pl.when(step == 0)
    def _():
        h_copy(step, slot).start()

    @pl.when(step < last)
    def _():
        h_copy(step + 1, 1 - slot).start()

    ya = _unpack_rows([r[...] for r in a_refs]).astype(F32)
    yb = _unpack_rows([r[...] for r in b_refs]).astype(F32)
    g = jnp.concatenate([route_ref[...]] * (LANES // ROUTE_COLS), axis=0).T
    moe = g[:, TOP_K:TOP_K + 1] * ya + g[:, TOP_K + 1:TOP_K + 2] * yb
    h_copy(step, slot).wait()
    out_ref[0] = hbuf[slot] + moe


def _combine(gates, h1, a_planes, b_planes, *, batch, seq, lp_len, tile):
    d = h1.shape[1]
    nt = seq // tile
    kern = functools.partial(_combine_kernel, lp_len=lp_len, tile=tile)
    rows = lambda b, i: (b * nt + i, 0)
    return pl.pallas_call(
        kern,
        grid=(batch, nt),
        in_specs=[
            pl.BlockSpec((ROUTE_COLS, tile), lambda b, i: (0, b * nt + i)),
            pl.BlockSpec(memory_space=pl.ANY),
            *[pl.BlockSpec((tile, LANES), rows)] * (2 * ROW_PLANES),
        ],
        out_specs=pl.BlockSpec((1, tile, d), lambda b, i: (b, i, 0)),
        out_shape=jax.ShapeDtypeStruct((batch, seq, d), F32),
        scratch_shapes=[pltpu.VMEM((2, tile, d), F32), pltpu.SemaphoreType.DMA((2,))],
        compiler_params=_cparams(2),
        name="moe_combine",
    )(gates, h1, *a_planes, *b_planes)


def _rope_tables(lp_len):
    half = ROPE_DIM // 2
    pos = jnp.arange(lp_len, dtype=F32)
    inv_freq = ROPE_THETA ** (-jnp.arange(0, ROPE_DIM, 2, dtype=F32) / ROPE_DIM)
    ang = pos[:, None] * inv_freq[None, :]
    lane = jnp.arange(LANES) % HEAD_DIM
    cos = jnp.tile(jnp.cos(ang), (1, LANES // half))
    sin = jnp.tile(jnp.sin(ang), (1, LANES // half))
    c = jnp.where(lane < ROPE_DIM, cos, 1.0)
    s1 = jnp.where((lane >= half) & (lane < ROPE_DIM), sin, 0.0)
    s2 = jnp.where(lane < half, -sin, 0.0)
    return c, s1, s2


def kernel(x, meta_tokens, norm1_g, w_in, conv_w, q_norm_g, k_norm_g, lambda_q1, lambda_k1, lambda_q2, lambda_k2,
           subln_g, w_out, norm2_g, w_router_group, b_router_group, w_router_expert, b_router_expert, w_gate,
           w_up, w_down):
    batch, seq, _ = x.shape
    assert w_in.shape[0] == 1, "a single layer is supported"
    l = 0
    length = seq + N_META
    tm = TOKEN_TILE
    lp_len = -(-length // tm) * tm
    tiles_per_seq = lp_len // tm
    assert tiles_per_seq >= 2 and (length - (tiles_per_seq - 1) * tm) % 8 == 0
    qw = N_HEADS * 2 * HEAD_DIM
    lam_init = 0.8 - 0.6 * math.exp(-0.3 * l)

    reps = qw // HEAD_DIM
    gq = jnp.tile(q_norm_g[l] * (HEAD_DIM ** -0.5 * LOG2E), reps)[None, :]
    gk = jnp.tile(k_norm_g[l], reps)[None, :]
    seg = jnp.arange(qw) // HEAD_DIM
    bd = (seg[:, None] == seg[None, :]).astype(BF16)
    rope = _rope_tables(lp_len)
    hp, convy, q, k, v = _inproj(x, meta_tokens.astype(x.dtype), norm1_g[l][None, :], w_in[l].astype(BF16),
                                 conv_w[l], gq, gk, bd, *rope, tiles_per_seq=tiles_per_seq, tm=tm)

    lamp = jnp.stack([lambda_q1[l], lambda_k1[l], lambda_q2[l], lambda_k2[l]]).astype(F32)
    o = _attention(q, k, v, lamp, subln_g[l][None, :], batch=batch, lp_len=lp_len, tq=tm, lam_init=lam_init)

    lane_pad = LANES - N_GROUPS - N_EXPERTS
    wr = jnp.pad(jnp.concatenate([w_router_group[l], w_router_expert[l]], axis=1), ((0, 0), (0, lane_pad)))
    wr_hi = wr.astype(BF16)
    wr_lo = (wr - wr_hi.astype(F32)).astype(BF16)
    br = jnp.pad(jnp.concatenate([b_router_group[l], b_router_expert[l]]), (0, lane_pad))[None, :]
    ridx = jnp.arange(tm)
    upper = (ridx[:, None] < ridx[None, :]).astype(BF16)
    h1, *rest = _outproj(hp, convy, o, w_out[l].astype(BF16), norm2_g[l][None, :], wr_hi, wr_lo, br, upper,
                         tiles_per_seq=tiles_per_seq, seq_len=length)
    x_planes = rest[:ROW_PLANES]
    route, cnt = rest[ROW_PLANES:]

    counts = cnt[EXPERT_LANE0:EXPERT_LANE0 + N_EXPERTS, 0].astype(jnp.int32)
    n_blocks = -(-(batch * length * TOP_K) // MOE_BLOCK) + N_EXPERTS
    p_rows = n_blocks * MOE_BLOCK
    padded = (counts + MOE_BLOCK - 1) // MOE_BLOCK * MOE_BLOCK
    pends = jnp.cumsum(padded)
    pstarts = pends - padded

    def lookup(table, idx):
        hit = idx[:, None] == jnp.arange(N_EXPERTS, dtype=jnp.int32)[None, :]
        return jnp.sum(jnp.where(hit, table[None, :], 0), axis=1)

    blk0 = jnp.arange(n_blocks, dtype=jnp.int32) * MOE_BLOCK
    block_e = jnp.minimum(jnp.sum((pends[None, :] <= blk0[:, None]).astype(jnp.int32), axis=1), N_EXPERTS - 1)
    nvalid = jnp.clip(lookup(counts, block_e) - (blk0 - lookup(pstarts, block_e)), 0, MOE_BLOCK)

    spare_rows = -(-(batch * (lp_len - length) * TOP_K) // MOE_BLOCK) * MOE_BLOCK
    dest = _slots(pstarts.astype(jnp.int32), route, batch=batch, lp_len=lp_len, seq_len=length, p_rows=p_rows)
    dest = dest.reshape(ROUTE_COLS, batch, lp_len)

    assert (batch * lp_len) % SC_WINDOW == 0 and (batch * seq) % SC_WINDOW == 0
    xs_planes = _sc_scatter_rows(x_planes, dest[0].reshape(-1, SC_WINDOW), dest[1].reshape(-1, SC_WINDOW),
                                 p_rows + spare_rows)
    y_planes = _experts(block_e, nvalid, xs_planes, w_gate[l], w_up[l], w_down[l])

    dest_x = dest[0:TOP_K, :, N_META:length]
    a_planes, b_planes = _sc_gather_rows(y_planes, dest_x[0].reshape(-1, SC_WINDOW),
                                         dest_x[1].reshape(-1, SC_WINDOW))
    route_x = route.reshape(ROUTE_COLS, batch, lp_len)[:, :, N_META:length].reshape(ROUTE_COLS, batch * seq)
    return _combine(route_x, h1, a_planes, b_planes, batch=batch, seq=seq, lp_len=lp_len,
                    tile=_largest_tile(seq, COMBINE_TILE, LANES))
```

```python
import functools
import math

import jax
import jax.numpy as jnp
from jax import lax
from jax.experimental import pallas as pl
from jax.experimental.pallas import tpu as pltpu
from jax.experimental.pallas import tpu_sc as plsc

F32 = jnp.float32
BF16 = jnp.bfloat16

N_META = 16
N_HEADS = 4
HEAD_DIM = 64
ROPE_DIM = HEAD_DIM // 4
ROPE_THETA = 500000.0
N_GROUPS = 4
EXPERTS_PER_GROUP = 8
N_EXPERTS = N_GROUPS * EXPERTS_PER_GROUP
TOP_K = 2
EPS = 1e-6
LOG2E = 1.4426950408889634

LANES = 128
TOKEN_TILE = 640
INPROJ_UNITS = 2
OUTPROJ_UNITS = 2
ATTN_HEADS_PER_STEP = 4
MOE_BLOCK = 1024
MOE_TAIL_STEP = 256
COMBINE_TILE = 1024
ROUTE_COLS = 8
ROUTE_ROWS = 64
ROW_PLANES = 4
SC_WINDOW = 128
EXPERT_LANE0 = N_GROUPS
NEG_BIG = -1e30
VMEM_LIMIT = 56 * 1024 * 1024


def _largest_tile(n, cap, mult):
    for t in range(min(cap, n), 0, -1):
        if n % t == 0 and t % mult == 0:
            return t
    raise ValueError(f"no tile for {n}")


def _cparams(n_axes):
    return pltpu.CompilerParams(dimension_semantics=("arbitrary",) * n_axes, vmem_limit_bytes=VMEM_LIMIT)


def _pack_rows(x):
    w = x.shape[1] // 2
    lo = lax.bitcast_convert_type(x[:, :w].astype(BF16).astype(F32), jnp.uint32)
    hi = lax.bitcast_convert_type(x[:, w:].astype(BF16).astype(F32), jnp.uint32)
    return lax.shift_right_logical(lo, jnp.uint32(16)) | (hi & jnp.uint32(0xFFFF0000))


def _unpack_rows(planes):
    w = jnp.concatenate(planes, axis=1)
    lo = lax.bitcast_convert_type(lax.shift_left(w, jnp.uint32(16)), F32)
    hi = lax.bitcast_convert_type(w & jnp.uint32(0xFFFF0000), F32)
    return jnp.concatenate([lo, hi], axis=1).astype(BF16)


def _inproj_kernel(x_hbm, meta_hbm, g1_ref, win_ref, convw_ref, gq_ref, gk_ref, bd_ref, *refs,
                   tiles_per_seq, seq, cw, qw, units):
    rope_refs = refs[0:3 * units]
    hp_hbm, convy_ref, q_ref, k_ref, v_ref, carry_ref, xbuf, sem, sem_out = refs[3 * units:]
    i = pl.program_id(0)
    n_steps = pl.num_programs(0)
    tm = xbuf.shape[1]
    q0 = 3 * cw
    w = convw_ref[...]
    last_rows = seq + N_META - (tiles_per_seq - 1) * tm

    def fetch(tile, slot, start):
        b = tile // tiles_per_seq
        t = tile % tiles_per_seq

        def go(src, dst):
            cp = pltpu.make_async_copy(src, dst, sem.at[slot])
            if start:
                cp.start()
            else:
                cp.wait()

        @pl.when(t == 0)
        def _():
            go(meta_hbm, xbuf.at[slot, pl.ds(0, N_META)])
            go(x_hbm.at[pl.ds(b * seq, tm - N_META)], xbuf.at[slot, pl.ds(N_META, tm - N_META)])

        @pl.when((t > 0) & (t < tiles_per_seq - 1))
        def _():
            go(x_hbm.at[pl.ds(b * seq + t * tm - N_META, tm)], xbuf.at[slot])

        @pl.when(t == tiles_per_seq - 1)
        def _():
            go(x_hbm.at[pl.ds(b * seq + t * tm - N_META, last_rows)], xbuf.at[slot, pl.ds(0, last_rows)])

    def hp_store(tile, slot):
        return pltpu.make_async_copy(xbuf.at[slot], hp_hbm.at[pl.ds(tile * tm, tm)], sem_out.at[slot])

    mine = (i % 2) * units
    other = units - mine

    @pl.when(i == 0)
    def _():
        for u in range(units):
            fetch(u, mine + u, True)

    @pl.when(i > 0)
    def _():
        for u in range(units):
            hp_store((i - 1) * units + u, other + u).wait()

    @pl.when(i + 1 < n_steps)
    def _():
        for u in range(units):
            fetch((i + 1) * units + u, other + u, True)

    for u in range(units):
        tile = i * units + u
        fetch(tile, mine + u, False)

        @pl.when(tile % tiles_per_seq == tiles_per_seq - 1)
        def _():
            xbuf[mine + u, pl.ds(last_rows, tm - last_rows), :] = jnp.zeros((tm - last_rows, xbuf.shape[2]),
                                                                           xbuf.dtype)

        hp_store(tile, mine + u).start()

    prev = carry_ref[...]

    for u in range(units):
        rows = pl.ds(u * tm, tm)
        x = xbuf[mine + u]
        ms = jnp.mean(x * x, axis=-1, keepdims=True)
        xn = (x * lax.rsqrt(ms + EPS) * g1_ref[...]).astype(BF16)

        def proj(lo, hi):
            return jnp.dot(xn, win_ref[:, lo:hi], preferred_element_type=F32)

        u_conv = proj(0, q0)
        u_q = proj(q0, q0 + qw)

        z = u_conv[:, cw:2 * cw] * u_conv[:, 2 * cw:3 * cw]
        prev = jnp.where((i * units + u) % tiles_per_seq == 0, 0.0, prev)
        p1 = prev[7:8]
        p2 = prev[6:7]
        row = lax.broadcasted_iota(jnp.int32, z.shape, 0)
        z1 = jnp.where(row == 0, p1, pltpu.roll(z, 1, axis=0))
        z2 = jnp.where(row == 0, p2, jnp.where(row == 1, p1, pltpu.roll(z, 2, axis=0)))
        prev = z[tm - 8:tm]
        conv = w[0:1] * z2 + w[1:2] * z1 + w[2:3] * z
        convy_ref[rows, :] = (u_conv[:, 0:cw] * conv).astype(BF16)

        rc, rs1, rs2 = (ref[...] for ref in rope_refs[3 * u:3 * u + 3])

        def norm_rope(t, g_ref):
            ss = jnp.dot((t * t).astype(BF16), bd_ref[...], preferred_element_type=F32)
            tn = t * lax.rsqrt(ss * (1.0 / HEAD_DIM) + EPS) * g_ref[...]
            outs = []
            for c in range(qw // LANES):
                ch = tn[:, c * LANES:(c + 1) * LANES]
                outs.append(ch * rc + pltpu.roll(ch, ROPE_DIM // 2, axis=1) * rs1
                            + pltpu.roll(ch, LANES - ROPE_DIM // 2, axis=1) * rs2)
            return jnp.concatenate(outs, axis=1).astype(BF16)

        u_k = proj(q0 + qw, q0 + 2 * qw)
        q_ref[rows, :] = norm_rope(u_q, gq_ref)
        u_v = proj(q0 + 2 * qw, win_ref.shape[1])
        k_ref[rows, :] = norm_rope(u_k, gk_ref)
        v_ref[rows, :] = u_v.astype(BF16)

    carry_ref[...] = prev

    @pl.when(i == n_steps - 1)
    def _():
        for u in range(units):
            hp_store(i * units + u, mine + u).wait()


def _inproj(x, meta, g1, w_in, conv_w, gq, gk, bd, rc, rs1, rs2, *, tiles_per_seq, tm):
    batch, seq, d = x.shape
    units = INPROJ_UNITS
    n = batch * tiles_per_seq * tm
    assert (batch * tiles_per_seq) % units == 0
    cw = conv_w.shape[1]
    qw = gq.shape[1]
    aw = w_in.shape[1] - 3 * cw - 2 * qw
    const = lambda i: (0, 0)
    step = lambda i: (i, 0)
    rope_specs = [pl.BlockSpec((tm, LANES), lambda i, u=u: ((i * units + u) % tiles_per_seq, 0))
                  for u in range(units) for _ in range(3)]
    kern = functools.partial(_inproj_kernel, tiles_per_seq=tiles_per_seq, seq=seq, cw=cw, qw=qw, units=units)
    return pl.pallas_call(
        kern,
        grid=(n // (units * tm),),
        in_specs=[
            pl.BlockSpec(memory_space=pl.ANY),
            pl.BlockSpec(memory_space=pl.ANY),
            pl.BlockSpec((1, d), const),
            pl.BlockSpec(w_in.shape, const),
            pl.BlockSpec(conv_w.shape, const),
            pl.BlockSpec((1, qw), const),
            pl.BlockSpec((1, qw), const),
            pl.BlockSpec(bd.shape, const),
            *rope_specs,
        ],
        out_specs=[
            pl.BlockSpec(memory_space=pl.ANY),
            pl.BlockSpec((units * tm, cw), step),
            pl.BlockSpec((units * tm, qw), step),
            pl.BlockSpec((units * tm, qw), step),
            pl.BlockSpec((units * tm, aw), step),
        ],
        out_shape=[
            jax.ShapeDtypeStruct((n, d), x.dtype),
            jax.ShapeDtypeStruct((n, cw), BF16),
            jax.ShapeDtypeStruct((n, qw), BF16),
            jax.ShapeDtypeStruct((n, qw), BF16),
            jax.ShapeDtypeStruct((n, aw), BF16),
        ],
        scratch_shapes=[pltpu.VMEM((8, cw), F32), pltpu.VMEM((2 * units, tm, d), x.dtype),
                        pltpu.SemaphoreType.DMA((2 * units,)), pltpu.SemaphoreType.DMA((2 * units,))],
        compiler_params=_cparams(1),
        name="inproj",
    )(x.reshape(batch * seq, d), meta, g1, w_in, conv_w, gq, gk, bd, *([rc, rs1, rs2] * units))


def _attn_kernel(q_ref, k_ref, v_ref, lamp_ref, sg_ref, o_ref, qs_ref, m_ref, l_ref, acc_ref, *, lam_init):
    qi = pl.program_id(2)
    tq = q_ref.shape[0]
    n_heads = q_ref.shape[1] // LANES
    n_chains = 2 * n_heads
    lane = lax.broadcasted_iota(jnp.int32, (tq, LANES), 1)
    for h in range(n_heads):
        q = q_ref[:, h * LANES:(h + 1) * LANES]
        zero = jnp.zeros_like(q)
        qs_ref[pl.ds(2 * h * tq, tq), :] = jnp.where(lane < HEAD_DIM, q, zero)
        qs_ref[pl.ds((2 * h + 1) * tq, tq), :] = jnp.where(lane >= HEAD_DIM, q, zero)

    def scores(off, width, which):
        h = which // 2
        kc = k_ref[pl.ds(off, width), h * LANES:(h + 1) * LANES]
        return lax.dot_general(qs_ref[pl.ds(which * tq, tq), :], kc, (((1,), (1,)), ((), ())),
                               preferred_element_type=F32)

    def update(off, width, which, s, masked, first):
        h = which // 2
        vc = jnp.concatenate([v_ref[pl.ds(off, width), h * LANES:(h + 1) * LANES],
                              jnp.ones((width, LANES), BF16)], axis=1)
        rows = pl.ds(which * tq, tq)
        if masked:
            r = lax.broadcasted_iota(jnp.int32, s.shape, 0)
            c = lax.broadcasted_iota(jnp.int32, s.shape, 1)
            s = jnp.where(c <= r + (width - tq), s, NEG_BIG)
        m_prev = jnp.where(first, NEG_BIG, m_ref[rows, :])
        l_prev = jnp.where(first, 0.0, l_ref[rows, :])
        acc_prev = jnp.where(first, 0.0, acc_ref[rows, :])
        m_new = jnp.maximum(m_prev, jnp.max(s, axis=-1, keepdims=True))
        alpha = jnp.exp2(m_prev - m_new)
        p = jnp.exp2((s - jnp.tile(m_new, (1, width // LANES))).astype(BF16))
        pv = jnp.dot(p, vc, preferred_element_type=F32)
        l_ref[rows, :] = alpha * l_prev + pv[:, LANES:]
        acc_ref[rows, :] = alpha * acc_prev + pv[:, :LANES]
        m_ref[rows, :] = m_new

    def chunk(off, width, masked, first):
        s_next = scores(off, width, 0)
        for c in range(n_chains):
            s = s_next
            if c + 1 < n_chains:
                s_next = scores(off, width, c + 1)
            update(off, width, c, s, masked, first)

    wide = 2 * tq

    def body(j, carry):
        chunk(pl.multiple_of(j * wide, wide), wide, False, j == 0)
        return carry

    lax.fori_loop(0, qi // 2, body, 0)
    odd = qi % 2 == 1

    @pl.when(odd)
    def _():
        chunk(pl.multiple_of((qi - 1) * tq, tq), wide, True, qi < 2)

    @pl.when(jnp.logical_not(odd))
    def _():
        chunk(pl.multiple_of(qi * tq, tq), tq, True, qi < 2)

    lp = lamp_ref[...]
    lam = (jnp.exp(jnp.sum(lp[0:1] * lp[1:2], axis=-1, keepdims=True))
           - jnp.exp(jnp.sum(lp[2:3] * lp[3:4], axis=-1, keepdims=True)) + lam_init)
    for h in range(n_heads):
        rows = pl.ds(2 * h * tq, 2 * tq)
        o_all = acc_ref[rows, :] / l_ref[rows, :]
        o = o_all[0:tq] - lam * o_all[tq:2 * tq]
        ms = jnp.mean(o * o, axis=-1, keepdims=True)
        o_ref[:, h * LANES:(h + 1) * LANES] = (o * lax.rsqrt(ms + EPS) * sg_ref[...]
                                               * (1.0 - lam_init)).astype(BF16)


def _attention(q, k, v, lamp, sg, *, batch, lp_len, tq, lam_init):
    n, qw = q.shape
    nq = lp_len // tq
    hw = ATTN_HEADS_PER_STEP * LANES
    chains = 2 * ATTN_HEADS_PER_STEP
    kern = functools.partial(_attn_kernel, lam_init=lam_init)
    return pl.pallas_call(
        kern,
        grid=(batch, qw // hw, nq),
        in_specs=[
            pl.BlockSpec((tq, hw), lambda b, h, i: (b * nq + i, h)),
            pl.BlockSpec((lp_len, hw), lambda b, h, i: (b, h)),
            pl.BlockSpec((lp_len, hw), lambda b, h, i: (b, h)),
            pl.BlockSpec(lamp.shape, lambda b, h, i: (0, 0)),
            pl.BlockSpec(sg.shape, lambda b, h, i: (0, 0)),
        ],
        out_specs=pl.BlockSpec((tq, hw), lambda b, h, i: (b * nq + i, h)),
        out_shape=jax.ShapeDtypeStruct((n, v.shape[1]), BF16),
        scratch_shapes=[
            pltpu.VMEM((chains * tq, LANES), BF16),
            pltpu.VMEM((chains * tq, LANES), F32),
            pltpu.VMEM((chains * tq, LANES), F32),
            pltpu.VMEM((chains * tq, LANES), F32),
        ],
        compiler_params=_cparams(3),
        name="diffattn",
    )(q, k, v, lamp, sg)


def _outproj_kernel(hp_ref, cy_ref, o_ref, wout_ref, g2_ref, wrh_ref, wrl_ref, br_ref, upper_ref,
                    h1_ref, xp0_ref, xp1_ref, xp2_ref, xp3_ref, route_ref, cnt_ref, run_ref,
                    *, tiles_per_seq, seq_len):
    i = pl.program_id(0)
    tm = upper_ref.shape[0]
    n_units = hp_ref.shape[0] // tm

    @pl.when(i == 0)
    def _():
        run_ref[...] = jnp.zeros_like(run_ref)

    def project(rows):
        mix = jnp.concatenate([cy_ref[rows, :], o_ref[rows, :]], axis=1)
        h1 = hp_ref[rows, :] + jnp.dot(mix, wout_ref[...], preferred_element_type=F32)
        h1_ref[rows, :] = h1
        ms = jnp.mean(h1 * h1, axis=-1, keepdims=True)
        xn = h1 * lax.rsqrt(ms + EPS) * g2_ref[...]
        xw = _pack_rows(xn)
        for c, ref in enumerate((xp0_ref, xp1_ref, xp2_ref, xp3_ref)):
            ref[rows, :] = xw[:, c * LANES:(c + 1) * LANES]
        x_hi = xn.astype(BF16)
        x_lo = (xn - x_hi.astype(F32)).astype(BF16)
        hi_both = jnp.dot(x_hi, jnp.concatenate([wrh_ref[...], wrl_ref[...]], axis=1),
                          preferred_element_type=F32)
        return (hi_both[:, :LANES] + hi_both[:, LANES:]
                + jnp.dot(x_lo, wrh_ref[...], preferred_element_type=F32) + br_ref[...])

    def route(u, logits, run):
        lt = logits.T[0:ROUTE_ROWS, :]
        row = lax.broadcasted_iota(jnp.int32, lt.shape, 0)
        big = jnp.int32(4 * LANES)

        def first_argmax(vals, vmax):
            return jnp.min(jnp.where(vals == vmax, row, big), axis=0, keepdims=True)

        gl = jnp.where(row < N_GROUPS, lt, NEG_BIG)
        gmax = jnp.max(gl, axis=0, keepdims=True)
        g_val = 1.0 / jnp.sum(jnp.exp(gl - gmax), axis=0, keepdims=True)
        g_idx = first_argmax(gl, gmax)
        lo = EXPERT_LANE0 + EXPERTS_PER_GROUP * g_idx
        el = jnp.where((row >= lo) & (row < lo + EXPERTS_PER_GROUP), lt, NEG_BIG)
        m1 = jnp.max(el, axis=0, keepdims=True)
        i1 = first_argmax(el, m1)
        el2 = jnp.where(row == i1, NEG_BIG, el)
        m2 = jnp.max(el2, axis=0, keepdims=True)
        i2 = first_argmax(el2, m2)
        r = jnp.exp(m2 - m1)
        gate1 = g_val / (1.0 + r)
        gate2 = g_val * r / (1.0 + r)

        pos = ((i * n_units + u) % tiles_per_seq) * tm + lax.broadcasted_iota(jnp.int32, (1, tm), 1)
        valid = pos < seq_len
        oh1 = jnp.where(valid & (row == i1), 1.0, 0.0)
        oh2 = jnp.where(valid & (row == i2), 1.0, 0.0)
        pre = jnp.dot(jnp.concatenate([oh1, oh2], axis=0).astype(BF16), upper_ref[...],
                      preferred_element_type=F32)
        tot1 = jnp.sum(oh1, axis=1, keepdims=True)
        tot2 = jnp.sum(oh2, axis=1, keepdims=True)
        run_t = jnp.tile(run, (1, tm // LANES))
        rank1 = jnp.sum(oh1 * (pre[:ROUTE_ROWS] + run_t), axis=0, keepdims=True)
        rank2 = jnp.sum(oh2 * (pre[ROUTE_ROWS:] + run_t + tot1), axis=0, keepdims=True)

        e1 = (i1 - EXPERT_LANE0).astype(F32)
        e2 = (i2 - EXPERT_LANE0).astype(F32)
        r8 = lax.broadcasted_iota(jnp.int32, (ROUTE_COLS, tm), 0)
        route_ref[:, u * tm:(u + 1) * tm] = jnp.where(r8 == 0, e1, jnp.where(r8 == 1, e2, jnp.where(
            r8 == 2, gate1, jnp.where(r8 == 3, gate2, jnp.where(r8 == 4, rank1, jnp.where(r8 == 5, rank2, 0.0))))))
        return run + tot1 + tot2

    all_logits = [project(pl.ds(u * tm, tm)) for u in range(n_units)]
    run = run_ref[...]
    for u, logits in enumerate(all_logits):
        run = route(u, logits, run)
    run_ref[...] = run
    cnt_ref[...] = run


def _outproj(hp, convy, o, w_out, g2, wr_hi, wr_lo, br, upper, *, tiles_per_seq, seq_len):
    n, d = hp.shape
    tm = OUTPROJ_UNITS * upper.shape[0]
    assert n % tm == 0
    const = lambda i: (0, 0)
    tile = lambda i: (i, 0)
    kern = functools.partial(_outproj_kernel, tiles_per_seq=tiles_per_seq, seq_len=seq_len)
    return pl.pallas_call(
        kern,
        grid=(n // tm,),
        in_specs=[
            pl.BlockSpec((tm, d), tile),
            pl.BlockSpec((tm, convy.shape[1]), tile),
            pl.BlockSpec((tm, o.shape[1]), tile),
            pl.BlockSpec(w_out.shape, const),
            pl.BlockSpec((1, d), const),
            pl.BlockSpec(wr_hi.shape, const),
            pl.BlockSpec(wr_lo.shape, const),
            pl.BlockSpec((1, LANES), const),
            pl.BlockSpec(upper.shape, const),
        ],
        out_specs=[
            pl.BlockSpec((tm, d), tile),
            *[pl.BlockSpec((tm, LANES), tile)] * ROW_PLANES,
            pl.BlockSpec((ROUTE_COLS, tm), lambda i: (0, i)),
            pl.BlockSpec((ROUTE_ROWS, LANES), const),
        ],
        out_shape=[
            jax.ShapeDtypeStruct((n, d), F32),
            *[jax.ShapeDtypeStruct((n, LANES), jnp.uint32)] * ROW_PLANES,
            jax.ShapeDtypeStruct((ROUTE_COLS, n), F32),
            jax.ShapeDtypeStruct((ROUTE_ROWS, LANES), F32),
        ],
        scratch_shapes=[pltpu.VMEM((ROUTE_ROWS, LANES), F32)],
        compiler_params=_cparams(1),
        name="outproj_router",
    )(hp, convy, o, w_out, g2, wr_hi, wr_lo, br, upper)


def _slots_kernel(pstart_ref, route_ref, dest_ref, *, seq_len, p_rows):
    b = pl.program_id(0)
    route = route_ref[...]
    eid = route.astype(jnp.int32)
    start = jnp.zeros_like(eid)
    for e in range(N_EXPERTS):
        start = jnp.where(eid == e, pstart_ref[e], start)
    rank = pltpu.roll(route, ROUTE_COLS - 2 * TOP_K, axis=0).astype(jnp.int32)
    k = lax.broadcasted_iota(jnp.int32, route.shape, 0)
    pos = lax.broadcasted_iota(jnp.int32, route.shape, 1)
    n_pad = route.shape[1] - seq_len
    spare = p_rows + (b * n_pad + (pos - seq_len)) * TOP_K + k
    dest_ref[...] = jnp.where(pos < seq_len, start + rank, spare)


def _slots(pstarts, route, *, batch, lp_len, seq_len, p_rows):
    kern = functools.partial(_slots_kernel, seq_len=seq_len, p_rows=p_rows)
    grid_spec = pltpu.PrefetchScalarGridSpec(
        num_scalar_prefetch=1,
        grid=(batch,),
        in_specs=[pl.BlockSpec((ROUTE_COLS, lp_len), lambda b, ps: (0, b))],
        out_specs=pl.BlockSpec((ROUTE_COLS, lp_len), lambda b, ps: (0, b)),
    )
    return pl.pallas_call(
        kern,
        grid_spec=grid_spec,
        out_shape=jax.ShapeDtypeStruct(route.shape, jnp.int32),
        compiler_params=_cparams(1),
        name="moe_slots",
    )(pstarts, route)


def _sc_workers():
    info = plsc.get_sparse_core_info()
    return info.num_cores, info.num_cores * info.num_subcores


def _sc_scatter_rows(planes, idx_a, idx_b, out_rows):
    n_win = idx_a.shape[0]
    n_cores, n_workers = _sc_workers()
    trips = -(-n_win // n_workers)
    mesh = plsc.VectorSubcoreMesh(core_axis_name="c", subcore_axis_name="s")

    def body(*refs):
        xs = refs[0:ROW_PLANES]
        ia_hbm, ib_hbm = refs[ROW_PLANES:ROW_PLANES + 2]
        outs = refs[ROW_PLANES + 2:2 * ROW_PLANES + 2]
        ia_v, ib_v, buf, sem = refs[2 * ROW_PLANES + 2:]
        wid = lax.axis_index("s") * n_cores + lax.axis_index("c")

        def step(t, carry):
            g = wid + t * n_workers

            @pl.when(g < n_win)
            def _():
                row0 = pl.multiple_of(g * SC_WINDOW, SC_WINDOW)
                loads = [pltpu.async_copy(ia_hbm.at[g], ia_v, sem), pltpu.async_copy(ib_hbm.at[g], ib_v, sem)]
                loads += [pltpu.async_copy(xs[c].at[pl.ds(row0, SC_WINDOW)], buf.at[c], sem)
                          for c in range(ROW_PLANES)]
                for cp in loads:
                    cp.wait()
                stores = [pltpu.async_copy(buf.at[c], outs[c].at[iv], sem)
                          for c in range(ROW_PLANES) for iv in (ia_v, ib_v)]
                for cp in stores:
                    cp.wait()

            return carry

        lax.fori_loop(0, trips, step, 0)

    kern = pl.kernel(
        body,
        out_type=[jax.ShapeDtypeStruct((out_rows, LANES), jnp.uint32)] * ROW_PLANES,
        mesh=mesh,
        scratch_types=[
            pltpu.VMEM((SC_WINDOW,), jnp.int32),
            pltpu.VMEM((SC_WINDOW,), jnp.int32),
            pltpu.VMEM((ROW_PLANES, SC_WINDOW, LANES), jnp.uint32),
            pltpu.SemaphoreType.DMA,
        ],
        name="moe_dispatch_sc",
    )
    return kern(*planes, idx_a, idx_b)


def _sc_gather_rows(planes, idx_a, idx_b):
    n_win = idx_a.shape[0]
    n_cores, n_workers = _sc_workers()
    trips = -(-n_win // n_workers)
    mesh = plsc.VectorSubcoreMesh(core_axis_name="c", subcore_axis_name="s")

    def body(*refs):
        ys = refs[0:ROW_PLANES]
        ia_hbm, ib_hbm = refs[ROW_PLANES:ROW_PLANES + 2]
        outs_a = refs[ROW_PLANES + 2:2 * ROW_PLANES + 2]
        outs_b = refs[2 * ROW_PLANES + 2:3 * ROW_PLANES + 2]
        ia_v, ib_v, buf, sem = refs[3 * ROW_PLANES + 2:]
        wid = lax.axis_index("s") * n_cores + lax.axis_index("c")

        def step(t, carry):
            g = wid + t * n_workers

            @pl.when(g < n_win)
            def _():
                row0 = pl.multiple_of(g * SC_WINDOW, SC_WINDOW)
                idx_loads = [pltpu.async_copy(ia_hbm.at[g], ia_v, sem), pltpu.async_copy(ib_hbm.at[g], ib_v, sem)]
                for cp in idx_loads:
                    cp.wait()
                for iv, outs in ((ia_v, outs_a), (ib_v, outs_b)):
                    loads = [pltpu.async_copy(ys[c].at[iv], buf.at[c], sem) for c in range(ROW_PLANES)]
                    for cp in loads:
                        cp.wait()
                    stores = [pltpu.async_copy(buf.at[c], outs[c].at[pl.ds(row0, SC_WINDOW)], sem)
                              for c in range(ROW_PLANES)]
                    for cp in stores:
                        cp.wait()

            return carry

        lax.fori_loop(0, trips, step, 0)

    n_rows = n_win * SC_WINDOW
    kern = pl.kernel(
        body,
        out_type=[jax.ShapeDtypeStruct((n_rows, LANES), jnp.uint32)] * (2 * ROW_PLANES),
        mesh=mesh,
        scratch_types=[
            pltpu.VMEM((SC_WINDOW,), jnp.int32),
            pltpu.VMEM((SC_WINDOW,), jnp.int32),
            pltpu.VMEM((ROW_PLANES, SC_WINDOW, LANES), jnp.uint32),
            pltpu.SemaphoreType.DMA,
        ],
        name="moe_gather_sc",
    )
    res = kern(*planes, idx_a, idx_b)
    return res[:ROW_PLANES], res[ROW_PLANES:]


def _experts_kernel(be_ref, nv_ref, first_ref, slot_ref, nxt_ref, x0_ref, x1_ref, x2_ref, x3_ref,
                    wg_hbm, wu_hbm, wd_hbm, y0_ref, y1_ref, y2_ref, y3_ref,
                    wgf_ref, wuf_ref, wdf_ref, wgb_ref, wub_ref, wdb_ref, sem):
    i = pl.program_id(0)
    e = be_ref[i]
    slot = slot_ref[i]
    y_refs = (y0_ref, y1_ref, y2_ref, y3_ref)

    def weight_copies(expert, s):
        return [pltpu.make_async_copy(hbm.at[expert], stage.at[s], sem.at[s, j])
                for j, (hbm, stage) in enumerate(((wg_hbm, wgf_ref), (wu_hbm, wuf_ref), (wd_hbm, wdf_ref)))]

    @pl.when(i == 0)
    def _():
        for cp in weight_copies(e, slot):
            cp.start()

    @pl.when(first_ref[i] == 1)
    def _():
        for cp in weight_copies(e, slot):
            cp.wait()
        nxt = nxt_ref[i]

        @pl.when(nxt >= 0)
        def _():
            for cp in weight_copies(nxt, 1 - slot):
                cp.start()

        wgb_ref[...] = wgf_ref[slot].astype(BF16)
        wub_ref[...] = wuf_ref[slot].astype(BF16)
        wdb_ref[...] = wdf_ref[slot].astype(BF16)

    nv = nv_ref[i]

    def mlp(rows):
        xs = _unpack_rows([r[rows, :] for r in (x0_ref, x1_ref, x2_ref, x3_ref)])
        row = lax.broadcasted_iota(jnp.int32, xs.shape, 0)
        x = jnp.where(row < nv, xs, jnp.zeros_like(xs))
        hg = jnp.dot(x, wgb_ref[...], preferred_element_type=F32)
        hu = jnp.dot(x, wub_ref[...], preferred_element_type=F32)
        hid = (hg / (1.0 + jnp.exp(-hg)) * hu).astype(BF16)
        yw = _pack_rows(jnp.dot(hid, wdb_ref[...], preferred_element_type=F32))
        for c, ref in enumerate(y_refs):
            ref[rows, :] = yw[:, c * LANES:(c + 1) * LANES]

    def clear(rows):
        for ref in y_refs:
            ref[rows, :] = jnp.zeros((rows.size, LANES), ref.dtype)

    for piece in range(MOE_BLOCK // MOE_TAIL_STEP + 1):
        used = piece * MOE_TAIL_STEP

        @pl.when((nv > used - MOE_TAIL_STEP) & (nv <= used))
        def _():
            if used > 0:
                mlp(pl.ds(0, used))
            if used < MOE_BLOCK:
                clear(pl.ds(used, MOE_BLOCK - used))


def _experts(block_e, nvalid, xs_planes, w_gate, w_up, w_down):
    n_blocks = block_e.shape[0]
    _, d, ff = w_gate.shape
    first = jnp.concatenate([jnp.ones((1,), jnp.int32), (block_e[1:] != block_e[:-1]).astype(jnp.int32)])
    slot = (jnp.cumsum(first) - 1) % 2
    later = jnp.where(block_e[None, :] > block_e[:, None], block_e[None, :], N_EXPERTS)
    nxt = jnp.min(later, axis=1)
    nxt = jnp.where(nxt == N_EXPERTS, -1, nxt).astype(jnp.int32)
    blk = lambda i, *_: (i, 0)
    grid_spec = pltpu.PrefetchScalarGridSpec(
        num_scalar_prefetch=5,
        grid=(n_blocks,),
        in_specs=[
            *[pl.BlockSpec((MOE_BLOCK, LANES), blk)] * ROW_PLANES,
            pl.BlockSpec(memory_space=pl.ANY),
            pl.BlockSpec(memory_space=pl.ANY),
            pl.BlockSpec(memory_space=pl.ANY),
        ],
        out_specs=[pl.BlockSpec((MOE_BLOCK, LANES), blk)] * ROW_PLANES,
        scratch_shapes=[
            pltpu.VMEM((2, d, ff), F32),
            pltpu.VMEM((2, d, ff), F32),
            pltpu.VMEM((2, ff, d), F32),
            pltpu.VMEM((d, ff), BF16),
            pltpu.VMEM((d, ff), BF16),
            pltpu.VMEM((ff, d), BF16),
            pltpu.SemaphoreType.DMA((2, 3)),
        ],
    )
    return pl.pallas_call(
        _experts_kernel,
        grid_spec=grid_spec,
        out_shape=[jax.ShapeDtypeStruct((n_blocks * MOE_BLOCK, LANES), jnp.uint32)] * ROW_PLANES,
        compiler_params=_cparams(1),
        name="moe_experts",
    )(block_e, nvalid, first, slot.astype(jnp.int32), nxt, *xs_planes, w_gate, w_up, w_down)


def _combine_kernel(route_ref, h1_hbm, *refs, lp_len, tile):
    a_refs = refs[0:ROW_PLANES]
    b_refs = refs[ROW_PLANES:2 * ROW_PLANES]
    out_ref, hbuf, sem_h = refs[2 * ROW_PLANES:]
    nt = pl.num_programs(1)
    step = pl.program_id(0) * nt + pl.program_id(1)
    last = pl.num_programs(0) * nt - 1

    def h_copy(s, slot):
        start = (s // nt) * lp_len + N_META + (s % nt) * tile
        return pltpu.make_async_copy(h1_hbm.at[pl.ds(start, tile), :], hbuf.at[slot], sem_h.at[slot])

    slot = step % 2

    @pl.when(step == 0)
    def _():
        h_copy(step, slot).start()

    @pl.when(step < last)
    def _():
        h_copy(step + 1, 1 - slot).start()

    ya = _unpack_rows([r[...] for r in a_refs]).astype(F32)
    yb = _unpack_rows([r[...] for r in b_refs]).astype(F32)
    g = jnp.concatenate([route_ref[...]] * (LANES // ROUTE_COLS), axis=0).T
    moe = g[:, TOP_K:TOP_K + 1] * ya + g[:, TOP_K + 1:TOP_K + 2] * yb
    h_copy(step, slot).wait()
    out_ref[0] = hbuf[slot] + moe


def _combine(gates, h1, a_planes, b_planes, *, batch, seq, lp_len, tile):
    d = h1.shape[1]
    nt = seq // tile
    kern = functools.partial(_combine_kernel, lp_len=lp_len, tile=tile)
    rows = lambda b, i: (b * nt + i, 0)
    return pl.pallas_call(
        kern,
        grid=(batch, nt),
        in_specs=[
            pl.BlockSpec((ROUTE_COLS, tile), lambda b, i: (0, b * nt + i)),
            pl.BlockSpec(memory_space=pl.ANY),
            *[pl.BlockSpec((tile, LANES), rows)] * (2 * ROW_PLANES),
        ],
        out_specs=pl.BlockSpec((1, tile, d), lambda b, i: (b, i, 0)),
        out_shape=jax.ShapeDtypeStruct((batch, seq, d), F32),
        scratch_shapes=[pltpu.VMEM((2, tile, d), F32), pltpu.SemaphoreType.DMA((2,))],
        compiler_params=_cparams(2),
        name="moe_combine",
    )(gates, h1, *a_planes, *b_planes)


def _rope_tables(lp_len):
    half = ROPE_DIM // 2
    pos = jnp.arange(lp_len, dtype=F32)
    inv_freq = ROPE_THETA ** (-jnp.arange(0, ROPE_DIM, 2, dtype=F32) / ROPE_DIM)
    ang = pos[:, None] * inv_freq[None, :]
    lane = jnp.arange(LANES) % HEAD_DIM
    cos = jnp.tile(jnp.cos(ang), (1, LANES // half))
    sin = jnp.tile(jnp.sin(ang), (1, LANES // half))
    c = jnp.where(lane < ROPE_DIM, cos, 1.0)
    s1 = jnp.where((lane >= half) & (lane < ROPE_DIM), sin, 0.0)
    s2 = jnp.where(lane < half, -sin, 0.0)
    return c, s1, s2


def kernel(x, meta_tokens, norm1_g, w_in, conv_w, q_norm_g, k_norm_g, lambda_q1, lambda_k1, lambda_q2, lambda_k2,
           subln_g, w_out, norm2_g, w_router_group, b_router_group, w_router_expert, b_router_expert, w_gate,
           w_up, w_down):
    batch, seq, _ = x.shape
    assert w_in.shape[0] == 1, "a single layer is supported"
    l = 0
    length = seq + N_META
    tm = TOKEN_TILE
    lp_len = -(-length // tm) * tm
    tiles_per_seq = lp_len // tm
    assert tiles_per_seq >= 2 and (length - (tiles_per_seq - 1) * tm) % 8 == 0
    qw = N_HEADS * 2 * HEAD_DIM
    lam_init = 0.8 - 0.6 * math.exp(-0.3 * l)

    reps = qw // HEAD_DIM
    gq = jnp.tile(q_norm_g[l] * (HEAD_DIM ** -0.5 * LOG2E), reps)[None, :]
    gk = jnp.tile(k_norm_g[l], reps)[None, :]
    seg = jnp.arange(qw) // HEAD_DIM
    bd = (seg[:, None] == seg[None, :]).astype(BF16)
    rope = _rope_tables(lp_len)
    hp, convy, q, k, v = _inproj(x, meta_tokens.astype(x.dtype), norm1_g[l][None, :], w_in[l].astype(BF16),
                                 conv_w[l], gq, gk, bd, *rope, tiles_per_seq=tiles_per_seq, tm=tm)

    lamp = jnp.stack([lambda_q1[l], lambda_k1[l], lambda_q2[l], lambda_k2[l]]).astype(F32)
    o = _attention(q, k, v, lamp, subln_g[l][None, :], batch=batch, lp_len=lp_len, tq=tm, lam_init=lam_init)

    lane_pad = LANES - N_GROUPS - N_EXPERTS
    wr = jnp.pad(jnp.concatenate([w_router_group[l], w_router_expert[l]], axis=1), ((0, 0), (0, lane_pad)))
    wr_hi = wr.astype(BF16)
    wr_lo = (wr - wr_hi.astype(F32)).astype(BF16)
    br = jnp.pad(jnp.concatenate([b_router_group[l], b_router_expert[l]]), (0, lane_pad))[None, :]
    ridx = jnp.arange(tm)
    upper = (ridx[:, None] < ridx[None, :]).astype(BF16)
    h1, *rest = _outproj(hp, convy, o, w_out[l].astype(BF16), norm2_g[l][None, :], wr_hi, wr_lo, br, upper,
                         tiles_per_seq=tiles_per_seq, seq_len=length)
    x_planes = rest[:ROW_PLANES]
    route, cnt = rest[ROW_PLANES:]

    counts = cnt[EXPERT_LANE0:EXPERT_LANE0 + N_EXPERTS, 0].astype(jnp.int32)
    n_blocks = -(-(batch * length * TOP_K) // MOE_BLOCK) + N_EXPERTS
    p_rows = n_blocks * MOE_BLOCK
    padded = (counts + MOE_BLOCK - 1) // MOE_BLOCK * MOE_BLOCK
    pends = jnp.cumsum(padded)
    pstarts = pends - padded

    def lookup(table, idx):
        hit = idx[:, None] == jnp.arange(N_EXPERTS, dtype=jnp.int32)[None, :]
        return jnp.sum(jnp.where(hit, table[None, :], 0), axis=1)

    blk0 = jnp.arange(n_blocks, dtype=jnp.int32) * MOE_BLOCK
    block_e = jnp.minimum(jnp.sum((pends[None, :] <= blk0[:, None]).astype(jnp.int32), axis=1), N_EXPERTS - 1)
    nvalid = jnp.clip(lookup(counts, block_e) - (blk0 - lookup(pstarts, block_e)), 0, MOE_BLOCK)

    spare_rows = -(-(batch * (lp_len - length) * TOP_K) // MOE_BLOCK) * MOE_BLOCK
    dest = _slots(pstarts.astype(jnp.int32), route, batch=batch, lp_len=lp_len, seq_len=length, p_rows=p_rows)
    dest = dest.reshape(ROUTE_COLS, batch, lp_len)

    assert (batch * lp_len) % SC_WINDOW == 0 and (batch * seq) % SC_WINDOW == 0
    xs_planes = _sc_scatter_rows(x_planes, dest[0].reshape(-1, SC_WINDOW), dest[1].reshape(-1, SC_WINDOW),
                                 p_rows + spare_rows)
    y_planes = _experts(block_e, nvalid, xs_planes, w_gate[l], w_up[l], w_down[l])

    dest_x = dest[0:TOP_K, :, N_META:length]
    a_planes, b_planes = _sc_gather_rows(y_planes, dest_x[0].reshape(-1, SC_WINDOW),
                                         dest_x[1].reshape(-1, SC_WINDOW))
    route_x = route.reshape(ROUTE_COLS, batch, lp_len)[:, :, N_META:length].reshape(ROUTE_COLS, batch * seq)
    return _combine(route_x, h1, a_planes, b_planes, batch=batch, seq=seq, lp_len=lp_len,
                    tile=_largest_tile(seq, COMBINE_TILE, LANES))
```

```python
import functools
import math

import jax
import jax.numpy as jnp
from jax import lax
from jax.experimental import pallas as pl
from jax.experimental.pallas import tpu as pltpu
from jax.experimental.pallas import tpu_sc as plsc

F32 = jnp.float32
BF16 = jnp.bfloat16

N_META = 16
N_HEADS = 4
HEAD_DIM = 64
ROPE_DIM = HEAD_DIM // 4
ROPE_THETA = 500000.0
N_GROUPS = 4
EXPERTS_PER_GROUP = 8
N_EXPERTS = N_GROUPS * EXPERTS_PER_GROUP
TOP_K = 2
EPS = 1e-6
LOG2E = 1.4426950408889634

LANES = 128
TOKEN_TILE = 640
INPROJ_UNITS = 2
OUTPROJ_UNITS = 2
ATTN_TAIL_SKIP = 384
ATTN_HEADS_PER_STEP = 4
MOE_BLOCK = 1024
MOE_TAIL_STEP = 256
COMBINE_TILE = 1024
ROUTE_COLS = 8
ROUTE_ROWS = 64
ROW_PLANES = 4
SC_WINDOW = 128
EXPERT_LANE0 = N_GROUPS
NEG_BIG = -1e30
VMEM_LIMIT = 56 * 1024 * 1024


def _largest_tile(n, cap, mult):
    for t in range(min(cap, n), 0, -1):
        if n % t == 0 and t % mult == 0:
            return t
    raise ValueError(f"no tile for {n}")


def _cparams(n_axes):
    return pltpu.CompilerParams(dimension_semantics=("arbitrary",) * n_axes, vmem_limit_bytes=VMEM_LIMIT)


def _pack_rows(x):
    w = x.shape[1] // 2
    lo = lax.bitcast_convert_type(x[:, :w].astype(BF16).astype(F32), jnp.uint32)
    hi = lax.bitcast_convert_type(x[:, w:].astype(BF16).astype(F32), jnp.uint32)
    return lax.shift_right_logical(lo, jnp.uint32(16)) | (hi & jnp.uint32(0xFFFF0000))


def _unpack_rows(planes):
    w = jnp.concatenate(planes, axis=1)
    lo = lax.bitcast_convert_type(lax.shift_left(w, jnp.uint32(16)), F32)
    hi = lax.bitcast_convert_type(w & jnp.uint32(0xFFFF0000), F32)
    return jnp.concatenate([lo, hi], axis=1).astype(BF16)


def _inproj_kernel(x_hbm, meta_hbm, g1_ref, win_ref, convw_ref, gq_ref, gk_ref, bd_ref, *refs,
                   tiles_per_seq, seq, cw, qw, units):
    rope_refs = refs[0:3 * units]
    hp_hbm, convy_ref, q_ref, k_ref, v_ref, carry_ref, xbuf, sem, sem_out = refs[3 * units:]
    i = pl.program_id(0)
    n_steps = pl.num_programs(0)
    tm = xbuf.shape[1]
    q0 = 3 * cw
    w = convw_ref[...]
    last_rows = seq + N_META - (tiles_per_seq - 1) * tm

    def fetch(tile, slot, start):
        b = tile // tiles_per_seq
        t = tile % tiles_per_seq

        def go(src, dst):
            cp = pltpu.make_async_copy(src, dst, sem.at[slot])
            if start:
                cp.start()
            else:
                cp.wait()

        @pl.when(t == 0)
        def _():
            go(meta_hbm, xbuf.at[slot, pl.ds(0, N_META)])
            go(x_hbm.at[pl.ds(b * seq, tm - N_META)], xbuf.at[slot, pl.ds(N_META, tm - N_META)])

        @pl.when((t > 0) & (t < tiles_per_seq - 1))
        def _():
            go(x_hbm.at[pl.ds(b * seq + t * tm - N_META, tm)], xbuf.at[slot])

        @pl.when(t == tiles_per_seq - 1)
        def _():
            go(x_hbm.at[pl.ds(b * seq + t * tm - N_META, last_rows)], xbuf.at[slot, pl.ds(0, last_rows)])

    def hp_store(tile, slot):
        return pltpu.make_async_copy(xbuf.at[slot], hp_hbm.at[pl.ds(tile * tm, tm)], sem_out.at[slot])

    mine = (i % 2) * units
    other = units - mine

    @pl.when(i == 0)
    def _():
        for u in range(units):
            fetch(u, mine + u, True)

    @pl.when(i > 0)
    def _():
        for u in range(units):
            hp_store((i - 1) * units + u, other + u).wait()

    @pl.when(i + 1 < n_steps)
    def _():
        for u in range(units):
            fetch((i + 1) * units + u, other + u, True)

    for u in range(units):
        tile = i * units + u
        fetch(tile, mine + u, False)

        @pl.when(tile % tiles_per_seq == tiles_per_seq - 1)
        def _():
            xbuf[mine + u, pl.ds(last_rows, tm - last_rows), :] = jnp.zeros((tm - last_rows, xbuf.shape[2]),
                                                                           xbuf.dtype)

        hp_store(tile, mine + u).start()

    prev = carry_ref[...]

    for u in range(units):
        rows = pl.ds(u * tm, tm)
        x = xbuf[mine + u]
        ms = jnp.mean(x * x, axis=-1, keepdims=True)
        xn = (x * lax.rsqrt(ms + EPS) * g1_ref[...]).astype(BF16)

        def proj(lo, hi):
            return jnp.dot(xn, win_ref[:, lo:hi], preferred_element_type=F32)

        u_conv = proj(0, q0)
        u_q = proj(q0, q0 + qw)

        z = u_conv[:, cw:2 * cw] * u_conv[:, 2 * cw:3 * cw]
        prev = jnp.where((i * units + u) % tiles_per_seq == 0, 0.0, prev)
        p1 = prev[7:8]
        p2 = prev[6:7]
        row = lax.broadcasted_iota(jnp.int32, z.shape, 0)
        z1 = jnp.where(row == 0, p1, pltpu.roll(z, 1, axis=0))
        z2 = jnp.where(row == 0, p2, jnp.where(row == 1, p1, pltpu.roll(z, 2, axis=0)))
        prev = z[tm - 8:tm]
        conv = w[0:1] * z2 + w[1:2] * z1 + w[2:3] * z
        convy_ref[rows, :] = (u_conv[:, 0:cw] * conv).astype(BF16)

        rc, rs1, rs2 = (ref[...] for ref in rope_refs[3 * u:3 * u + 3])

        def norm_rope(t, g_ref):
            ss = jnp.dot((t * t).astype(BF16), bd_ref[...], preferred_element_type=F32)
            tn = t * lax.rsqrt(ss * (1.0 / HEAD_DIM) + EPS) * g_ref[...]
            outs = []
            for c in range(qw // LANES):
                ch = tn[:, c * LANES:(c + 1) * LANES]
                outs.append(ch * rc + pltpu.roll(ch, ROPE_DIM // 2, axis=1) * rs1
                            + pltpu.roll(ch, LANES - ROPE_DIM // 2, axis=1) * rs2)
            return jnp.concatenate(outs, axis=1).astype(BF16)

        u_k = proj(q0 + qw, q0 + 2 * qw)
        q_ref[rows, :] = norm_rope(u_q, gq_ref)
        u_v = proj(q0 + 2 * qw, win_ref.shape[1])
        k_ref[rows, :] = norm_rope(u_k, gk_ref)
        v_ref[rows, :] = u_v.astype(BF16)

    carry_ref[...] = prev

    @pl.when(i == n_steps - 1)
    def _():
        for u in range(units):
            hp_store(i * units + u, mine + u).wait()


def _inproj(x, meta, g1, w_in, conv_w, gq, gk, bd, rc, rs1, rs2, *, tiles_per_seq, tm):
    batch, seq, d = x.shape
    units = INPROJ_UNITS
    n = batch * tiles_per_seq * tm
    assert (batch * tiles_per_seq) % units == 0
    cw = conv_w.shape[1]
    qw = gq.shape[1]
    aw = w_in.shape[1] - 3 * cw - 2 * qw
    const = lambda i: (0, 0)
    step = lambda i: (i, 0)
    rope_specs = [pl.BlockSpec((tm, LANES), lambda i, u=u: ((i * units + u) % tiles_per_seq, 0))
                  for u in range(units) for _ in range(3)]
    kern = functools.partial(_inproj_kernel, tiles_per_seq=tiles_per_seq, seq=seq, cw=cw, qw=qw, units=units)
    return pl.pallas_call(
        kern,
        grid=(n // (units * tm),),
        in_specs=[
            pl.BlockSpec(memory_space=pl.ANY),
            pl.BlockSpec(memory_space=pl.ANY),
            pl.BlockSpec((1, d), const),
            pl.BlockSpec(w_in.shape, const),
            pl.BlockSpec(conv_w.shape, const),
            pl.BlockSpec((1, qw), const),
            pl.BlockSpec((1, qw), const),
            pl.BlockSpec(bd.shape, const),
            *rope_specs,
        ],
        out_specs=[
            pl.BlockSpec(memory_space=pl.ANY),
            pl.BlockSpec((units * tm, cw), step),
            pl.BlockSpec((units * tm, qw), step),
            pl.BlockSpec((units * tm, qw), step),
            pl.BlockSpec((units * tm, aw), step),
        ],
        out_shape=[
            jax.ShapeDtypeStruct((n, d), x.dtype),
            jax.ShapeDtypeStruct((n, cw), BF16),
            jax.ShapeDtypeStruct((n, qw), BF16),
            jax.ShapeDtypeStruct((n, qw), BF16),
            jax.ShapeDtypeStruct((n, aw), BF16),
        ],
        scratch_shapes=[pltpu.VMEM((8, cw), F32), pltpu.VMEM((2 * units, tm, d), x.dtype),
                        pltpu.SemaphoreType.DMA((2 * units,)), pltpu.SemaphoreType.DMA((2 * units,))],
        compiler_params=_cparams(1),
        name="inproj",
    )(x.reshape(batch * seq, d), meta, g1, w_in, conv_w, gq, gk, bd, *([rc, rs1, rs2] * units))


def _attn_kernel(q_ref, k_ref, v_ref, lamp_ref, sg_ref, o_ref, qs_ref, m_ref, l_ref, acc_ref, *, lam_init):
    qi = pl.program_id(2)
    tq = q_ref.shape[0]
    n_heads = q_ref.shape[1] // LANES
    n_chains = 2 * n_heads
    lane = lax.broadcasted_iota(jnp.int32, (tq, LANES), 1)
    for h in range(n_heads):
        q = q_ref[:, h * LANES:(h + 1) * LANES]
        zero = jnp.zeros_like(q)
        qs_ref[pl.ds(2 * h * tq, tq), :] = jnp.where(lane < HEAD_DIM, q, zero)
        qs_ref[pl.ds((2 * h + 1) * tq, tq), :] = jnp.where(lane >= HEAD_DIM, q, zero)

    def scores(off, width, which, r0=0, nr=None):
        h = which // 2
        nr = tq if nr is None else nr
        kc = k_ref[pl.ds(off, width), h * LANES:(h + 1) * LANES]
        return lax.dot_general(qs_ref[pl.ds(which * tq + r0, nr), :], kc, (((1,), (1,)), ((), ())),
                               preferred_element_type=F32)

    def update(off, width, which, s, first, r0=0, nr=None, diag=None):
        h = which // 2
        nr = tq if nr is None else nr
        vc = jnp.concatenate([v_ref[pl.ds(off, width), h * LANES:(h + 1) * LANES],
                              jnp.ones((width, LANES), BF16)], axis=1)
        rows = pl.ds(which * tq + r0, nr)
        if diag is not None:
            r = lax.broadcasted_iota(jnp.int32, s.shape, 0)
            c = lax.broadcasted_iota(jnp.int32, s.shape, 1)
            s = jnp.where(c <= r + (r0 + diag), s, NEG_BIG)
        m_prev = jnp.where(first, NEG_BIG, m_ref[rows, :])
        l_prev = jnp.where(first, 0.0, l_ref[rows, :])
        acc_prev = jnp.where(first, 0.0, acc_ref[rows, :])
        m_new = jnp.maximum(m_prev, jnp.max(s, axis=-1, keepdims=True))
        alpha = jnp.exp2(m_prev - m_new)
        p = jnp.exp2((s - jnp.tile(m_new, (1, width // LANES))).astype(BF16))
        pv = jnp.dot(p, vc, preferred_element_type=F32)
        l_ref[rows, :] = alpha * l_prev + pv[:, LANES:]
        acc_ref[rows, :] = alpha * acc_prev + pv[:, :LANES]
        m_ref[rows, :] = m_new

    def run_pieces(pieces, first):
        s_next = scores(pieces[0][1], pieces[0][2], pieces[0][0], pieces[0][3], pieces[0][4])
        for n, (c, off, width, r0, nr, diag) in enumerate(pieces):
            s = s_next
            if n + 1 < len(pieces):
                c2, off2, width2, r02, nr2, _ = pieces[n + 1]
                s_next = scores(off2, width2, c2, r02, nr2)
            update(off, width, c, s, first, r0, nr, diag)

    def chunk(off, width, first):
        run_pieces([(c, off, width, 0, tq, None) for c in range(n_chains)], first)

    def tail_chunk(off, width, first):
        upper = tq - ATTN_TAIL_SKIP
        pieces = []
        for c in range(n_chains):
            pieces.append((c, off, width - ATTN_TAIL_SKIP, 0, upper, width - tq))
            pieces.append((c, off, width, upper, tq - upper, width - tq))
        run_pieces(pieces, first)

    wide = 2 * tq

    def body(j, carry):
        chunk(pl.multiple_of(j * wide, wide), wide, j == 0)
        return carry

    lax.fori_loop(0, qi // 2, body, 0)
    odd = qi % 2 == 1

    @pl.when(odd)
    def _():
        tail_chunk(pl.multiple_of((qi - 1) * tq, tq), wide, qi < 2)

    @pl.when(jnp.logical_not(odd))
    def _():
        tail_chunk(pl.multiple_of(qi * tq, tq), tq, qi < 2)

    lp = lamp_ref[...]
    lam = (jnp.exp(jnp.sum(lp[0:1] * lp[1:2], axis=-1, keepdims=True))
           - jnp.exp(jnp.sum(lp[2:3] * lp[3:4], axis=-1, keepdims=True)) + lam_init)
    for h in range(n_heads):
        rows = pl.ds(2 * h * tq, 2 * tq)
        o_all = acc_ref[rows, :] / l_ref[rows, :]
        o = o_all[0:tq] - lam * o_all[tq:2 * tq]
        ms = jnp.mean(o * o, axis=-1, keepdims=True)
        o_ref[:, h * LANES:(h + 1) * LANES] = (o * lax.rsqrt(ms + EPS) * sg_ref[...]
                                               * (1.0 - lam_init)).astype(BF16)


def _attention(q, k, v, lamp, sg, *, batch, lp_len, tq, lam_init):
    n, qw = q.shape
    nq = lp_len // tq
    hw = ATTN_HEADS_PER_STEP * LANES
    chains = 2 * ATTN_HEADS_PER_STEP
    kern = functools.partial(_attn_kernel, lam_init=lam_init)
    return pl.pallas_call(
        kern,
        grid=(batch, qw // hw, nq),
        in_specs=[
            pl.BlockSpec((tq, hw), lambda b, h, i: (b * nq + i, h)),
            pl.BlockSpec((lp_len, hw), lambda b, h, i: (b, h)),
            pl.BlockSpec((lp_len, hw), lambda b, h, i: (b, h)),
            pl.BlockSpec(lamp.shape, lambda b, h, i: (0, 0)),
            pl.BlockSpec(sg.shape, lambda b, h, i: (0, 0)),
        ],
        out_specs=pl.BlockSpec((tq, hw), lambda b, h, i: (b * nq + i, h)),
        out_shape=jax.ShapeDtypeStruct((n, v.shape[1]), BF16),
        scratch_shapes=[
            pltpu.VMEM((chains * tq, LANES), BF16),
            pltpu.VMEM((chains * tq, LANES), F32),
            pltpu.VMEM((chains * tq, LANES), F32),
            pltpu.VMEM((chains * tq, LANES), F32),
        ],
        compiler_params=_cparams(3),
        name="diffattn",
    )(q, k, v, lamp, sg)


def _outproj_kernel(hp_ref, cy_ref, o_ref, wout_ref, g2_ref, wrh_ref, wrl_ref, br_ref, upper_ref,
                    h1_ref, xp0_ref, xp1_ref, xp2_ref, xp3_ref, route_ref, cnt_ref, run_ref,
                    *, tiles_per_seq, seq_len):
    i = pl.program_id(0)
    tm = upper_ref.shape[0]
    n_units = hp_ref.shape[0] // tm

    @pl.when(i == 0)
    def _():
        run_ref[...] = jnp.zeros_like(run_ref)

    def project(rows):
        mix = jnp.concatenate([cy_ref[rows, :], o_ref[rows, :]], axis=1)
        h1 = hp_ref[rows, :] + jnp.dot(mix, wout_ref[...], preferred_element_type=F32)
        h1_ref[rows, :] = h1
        ms = jnp.mean(h1 * h1, axis=-1, keepdims=True)
        xn = h1 * lax.rsqrt(ms + EPS) * g2_ref[...]
        xw = _pack_rows(xn)
        for c, ref in enumerate((xp0_ref, xp1_ref, xp2_ref, xp3_ref)):
            ref[rows, :] = xw[:, c * LANES:(c + 1) * LANES]
        x_hi = xn.astype(BF16)
        x_lo = (xn - x_hi.astype(F32)).astype(BF16)
        hi_both = jnp.dot(x_hi, jnp.concatenate([wrh_ref[...], wrl_ref[...]], axis=1),
                          preferred_element_type=F32)
        return (hi_both[:, :LANES] + hi_both[:, LANES:]
                + jnp.dot(x_lo, wrh_ref[...], preferred_element_type=F32) + br_ref[...])

    def route(u, logits, run):
        lt = logits.T[0:ROUTE_ROWS, :]
        row = lax.broadcasted_iota(jnp.int32, lt.shape, 0)
        big = jnp.int32(4 * LANES)

        def first_argmax(vals, vmax):
            return jnp.min(jnp.where(vals == vmax, row, big), axis=0, keepdims=True)

        gl = jnp.where(row < N_GROUPS, lt, NEG_BIG)
        gmax = jnp.max(gl, axis=0, keepdims=True)
        g_val = 1.0 / jnp.sum(jnp.exp(gl - gmax), axis=0, keepdims=True)
        g_idx = first_argmax(gl, gmax)
        lo = EXPERT_LANE0 + EXPERTS_PER_GROUP * g_idx
        el = jnp.where((row >= lo) & (row < lo + EXPERTS_PER_GROUP), lt, NEG_BIG)
        m1 = jnp.max(el, axis=0, keepdims=True)
        i1 = first_argmax(el, m1)
        el2 = jnp.where(row == i1, NEG_BIG, el)
        m2 = jnp.max(el2, axis=0, keepdims=True)
        i2 = first_argmax(el2, m2)
        r = jnp.exp(m2 - m1)
        gate1 = g_val / (1.0 + r)
        gate2 = g_val * r / (1.0 + r)

        pos = ((i * n_units + u) % tiles_per_seq) * tm + lax.broadcasted_iota(jnp.int32, (1, tm), 1)
        valid = pos < seq_len
        oh1 = jnp.where(valid & (row == i1), 1.0, 0.0)
        oh2 = jnp.where(valid & (row == i2), 1.0, 0.0)
        pre = jnp.dot(jnp.concatenate([oh1, oh2], axis=0).astype(BF16), upper_ref[...],
                      preferred_element_type=F32)
        tot1 = jnp.sum(oh1, axis=1, keepdims=True)
        tot2 = jnp.sum(oh2, axis=1, keepdims=True)
        run_t = jnp.tile(run, (1, tm // LANES))
        rank1 = jnp.sum(oh1 * (pre[:ROUTE_ROWS] + run_t), axis=0, keepdims=True)
        rank2 = jnp.sum(oh2 * (pre[ROUTE_ROWS:] + run_t + tot1), axis=0, keepdims=True)

        e1 = (i1 - EXPERT_LANE0).astype(F32)
        e2 = (i2 - EXPERT_LANE0).astype(F32)
        r8 = lax.broadcasted_iota(jnp.int32, (ROUTE_COLS, tm), 0)
        route_ref[:, u * tm:(u + 1) * tm] = jnp.where(r8 == 0, e1, jnp.where(r8 == 1, e2, jnp.where(
            r8 == 2, gate1, jnp.where(r8 == 3, gate2, jnp.where(r8 == 4, rank1, jnp.where(r8 == 5, rank2, 0.0))))))
        return run + tot1 + tot2

    all_logits = [project(pl.ds(u * tm, tm)) for u in range(n_units)]
    run = run_ref[...]
    for u, logits in enumerate(all_logits):
        run = route(u, logits, run)
    run_ref[...] = run
    cnt_ref[...] = run


def _outproj(hp, convy, o, w_out, g2, wr_hi, wr_lo, br, upper, *, tiles_per_seq, seq_len):
    n, d = hp.shape
    tm = OUTPROJ_UNITS * upper.shape[0]
    assert n % tm == 0
    const = lambda i: (0, 0)
    tile = lambda i: (i, 0)
    kern = functools.partial(_outproj_kernel, tiles_per_seq=tiles_per_seq, seq_len=seq_len)
    return pl.pallas_call(
        kern,
        grid=(n // tm,),
        in_specs=[
            pl.BlockSpec((tm, d), tile),
            pl.BlockSpec((tm, convy.shape[1]), tile),
            pl.BlockSpec((tm, o.shape[1]), tile),
            pl.BlockSpec(w_out.shape, const),
            pl.BlockSpec((1, d), const),
            pl.BlockSpec(wr_hi.shape, const),
            pl.BlockSpec(wr_lo.shape, const),
            pl.BlockSpec((1, LANES), const),
            pl.BlockSpec(upper.shape, const),
        ],
        out_specs=[
            pl.BlockSpec((tm, d), tile),
            *[pl.BlockSpec((tm, LANES), tile)] * ROW_PLANES,
            pl.BlockSpec((ROUTE_COLS, tm), lambda i: (0, i)),
            pl.BlockSpec((ROUTE_ROWS, LANES), const),
        ],
        out_shape=[
            jax.ShapeDtypeStruct((n, d), F32),
            *[jax.ShapeDtypeStruct((n, LANES), jnp.uint32)] * ROW_PLANES,
            jax.ShapeDtypeStruct((ROUTE_COLS, n), F32),
            jax.ShapeDtypeStruct((ROUTE_ROWS, LANES), F32),
        ],
        scratch_shapes=[pltpu.VMEM((ROUTE_ROWS, LANES), F32)],
        compiler_params=_cparams(1),
        name="outproj_router",
    )(hp, convy, o, w_out, g2, wr_hi, wr_lo, br, upper)


def _slots_kernel(pstart_ref, route_ref, dest_ref, *, seq_len, p_rows):
    b = pl.program_id(0)
    route = route_ref[...]
    eid = route.astype(jnp.int32)
    start = jnp.zeros_like(eid)
    for e in range(N_EXPERTS):
        start = jnp.where(eid == e, pstart_ref[e], start)
    rank = pltpu.roll(route, ROUTE_COLS - 2 * TOP_K, axis=0).astype(jnp.int32)
    k = lax.broadcasted_iota(jnp.int32, route.shape, 0)
    pos = lax.broadcasted_iota(jnp.int32, route.shape, 1)
    n_pad = route.shape[1] - seq_len
    spare = p_rows + (b * n_pad + (pos - seq_len)) * TOP_K + k
    dest_ref[...] = jnp.where(pos < seq_len, start + rank, spare)


def _slots(pstarts, route, *, batch, lp_len, seq_len, p_rows):
    kern = functools.partial(_slots_kernel, seq_len=seq_len, p_rows=p_rows)
    grid_spec = pltpu.PrefetchScalarGridSpec(
        num_scalar_prefetch=1,
        grid=(batch,),
        in_specs=[pl.BlockSpec((ROUTE_COLS, lp_len), lambda b, ps: (0, b))],
        out_specs=pl.BlockSpec((ROUTE_COLS, lp_len), lambda b, ps: (0, b)),
    )
    return pl.pallas_call(
        kern,
        grid_spec=grid_spec,
        out_shape=jax.ShapeDtypeStruct(route.shape, jnp.int32),
        compiler_params=_cparams(1),
        name="moe_slots",
    )(pstarts, route)


def _sc_workers():
    info = plsc.get_sparse_core_info()
    return info.num_cores, info.num_cores * info.num_subcores


def _sc_scatter_rows(planes, idx_a, idx_b, out_rows):
    n_win = idx_a.shape[0]
    n_cores, n_workers = _sc_workers()
    trips = -(-n_win // n_workers)
    mesh = plsc.VectorSubcoreMesh(core_axis_name="c", subcore_axis_name="s")

    def body(*refs):
        xs = refs[0:ROW_PLANES]
        ia_hbm, ib_hbm = refs[ROW_PLANES:ROW_PLANES + 2]
        outs = refs[ROW_PLANES + 2:2 * ROW_PLANES + 2]
        ia_v, ib_v, buf, sem = refs[2 * ROW_PLANES + 2:]
        wid = lax.axis_index("s") * n_cores + lax.axis_index("c")

        def step(t, carry):
            g = wid + t * n_workers

            @pl.when(g < n_win)
            def _():
                row0 = pl.multiple_of(g * SC_WINDOW, SC_WINDOW)
                loads = [pltpu.async_copy(ia_hbm.at[g], ia_v, sem), pltpu.async_copy(ib_hbm.at[g], ib_v, sem)]
                loads += [pltpu.async_copy(xs[c].at[pl.ds(row0, SC_WINDOW)], buf.at[c], sem)
                          for c in range(ROW_PLANES)]
                for cp in loads:
                    cp.wait()
                stores = [pltpu.async_copy(buf.at[c], outs[c].at[iv], sem)
                          for c in range(ROW_PLANES) for iv in (ia_v, ib_v)]
                for cp in stores:
                    cp.wait()

            return carry

        lax.fori_loop(0, trips, step, 0)

    kern = pl.kernel(
        body,
        out_type=[jax.ShapeDtypeStruct((out_rows, LANES), jnp.uint32)] * ROW_PLANES,
        mesh=mesh,
        scratch_types=[
            pltpu.VMEM((SC_WINDOW,), jnp.int32),
            pltpu.VMEM((SC_WINDOW,), jnp.int32),
            pltpu.VMEM((ROW_PLANES, SC_WINDOW, LANES), jnp.uint32),
            pltpu.SemaphoreType.DMA,
        ],
        name="moe_dispatch_sc",
    )
    return kern(*planes, idx_a, idx_b)


def _sc_gather_rows(planes, idx_a, idx_b):
    n_win = idx_a.shape[0]
    n_cores, n_workers = _sc_workers()
    trips = -(-n_win // n_workers)
    mesh = plsc.VectorSubcoreMesh(core_axis_name="c", subcore_axis_name="s")

    def body(*refs):
        ys = refs[0:ROW_PLANES]
        ia_hbm, ib_hbm = refs[ROW_PLANES:ROW_PLANES + 2]
        outs_a = refs[ROW_PLANES + 2:2 * ROW_PLANES + 2]
        outs_b = refs[2 * ROW_PLANES + 2:3 * ROW_PLANES + 2]
        ia_v, ib_v, buf, sem = refs[3 * ROW_PLANES + 2:]
        wid = lax.axis_index("s") * n_cores + lax.axis_index("c")

        def step(t, carry):
            g = wid + t * n_workers

            @pl.when(g < n_win)
            def _():
                row0 = pl.multiple_of(g * SC_WINDOW, SC_WINDOW)
                idx_loads = [pltpu.async_copy(ia_hbm.at[g], ia_v, sem), pltpu.async_copy(ib_hbm.at[g], ib_v, sem)]
                for cp in idx_loads:
                    cp.wait()
                for iv, outs in ((ia_v, outs_a), (ib_v, outs_b)):
                    loads = [pltpu.async_copy(ys[c].at[iv], buf.at[c], sem) for c in range(ROW_PLANES)]
                    for cp in loads:
                        cp.wait()
                    stores = [pltpu.async_copy(buf.at[c], outs[c].at[pl.ds(row0, SC_WINDOW)], sem)
                              for c in range(ROW_PLANES)]
                    for cp in stores:
                        cp.wait()

            return carry

        lax.fori_loop(0, trips, step, 0)

    n_rows = n_win * SC_WINDOW
    kern = pl.kernel(
        body,
        out_type=[jax.ShapeDtypeStruct((n_rows, LANES), jnp.uint32)] * (2 * ROW_PLANES),
        mesh=mesh,
        scratch_types=[
            pltpu.VMEM((SC_WINDOW,), jnp.int32),
            pltpu.VMEM((SC_WINDOW,), jnp.int32),
            pltpu.VMEM((ROW_PLANES, SC_WINDOW, LANES), jnp.uint32),
            pltpu.SemaphoreType.DMA,
        ],
        name="moe_gather_sc",
    )
    res = kern(*planes, idx_a, idx_b)
    return res[:ROW_PLANES], res[ROW_PLANES:]


def _experts_kernel(be_ref, nv_ref, first_ref, slot_ref, nxt_ref, x0_ref, x1_ref, x2_ref, x3_ref,
                    wg_hbm, wu_hbm, wd_hbm, y0_ref, y1_ref, y2_ref, y3_ref,
                    wgf_ref, wuf_ref, wdf_ref, wgb_ref, wub_ref, wdb_ref, sem):
    i = pl.program_id(0)
    e = be_ref[i]
    slot = slot_ref[i]
    y_refs = (y0_ref, y1_ref, y2_ref, y3_ref)

    def weight_copies(expert, s):
        return [pltpu.make_async_copy(hbm.at[expert], stage.at[s], sem.at[s, j])
                for j, (hbm, stage) in enumerate(((wg_hbm, wgf_ref), (wu_hbm, wuf_ref), (wd_hbm, wdf_ref)))]

    @pl.when(i == 0)
    def _():
        for cp in weight_copies(e, slot):
            cp.start()

    @pl.when(first_ref[i] == 1)
    def _():
        for cp in weight_copies(e, slot):
            cp.wait()
        nxt = nxt_ref[i]

        @pl.when(nxt >= 0)
        def _():
            for cp in weight_copies(nxt, 1 - slot):
                cp.start()

        wgb_ref[...] = wgf_ref[slot].astype(BF16)
        wub_ref[...] = wuf_ref[slot].astype(BF16)
        wdb_ref[...] = wdf_ref[slot].astype(BF16)

    nv = nv_ref[i]

    def mlp(rows):
        xs = _unpack_rows([r[rows, :] for r in (x0_ref, x1_ref, x2_ref, x3_ref)])
        row = lax.broadcasted_iota(jnp.int32, xs.shape, 0)
        x = jnp.where(row < nv, xs, jnp.zeros_like(xs))
        hg = jnp.dot(x, wgb_ref[...], preferred_element_type=F32)
        hu = jnp.dot(x, wub_ref[...], preferred_element_type=F32)
        hid = (hg / (1.0 + jnp.exp(-hg)) * hu).astype(BF16)
        yw = _pack_rows(jnp.dot(hid, wdb_ref[...], preferred_element_type=F32))
        for c, ref in enumerate(y_refs):
            ref[rows, :] = yw[:, c * LANES:(c + 1) * LANES]

    def clear(rows):
        for ref in y_refs:
            ref[rows, :] = jnp.zeros((rows.size, LANES), ref.dtype)

    for piece in range(MOE_BLOCK // MOE_TAIL_STEP + 1):
        used = piece * MOE_TAIL_STEP

        @pl.when((nv > used - MOE_TAIL_STEP) & (nv <= used))
        def _():
            if used > 0:
                mlp(pl.ds(0, used))
            if used < MOE_BLOCK:
                clear(pl.ds(used, MOE_BLOCK - used))


def _experts(block_e, nvalid, xs_planes, w_gate, w_up, w_down):
    n_blocks = block_e.shape[0]
    _, d, ff = w_gate.shape
    first = jnp.concatenate([jnp.ones((1,), jnp.int32), (block_e[1:] != block_e[:-1]).astype(jnp.int32)])
    slot = (jnp.cumsum(first) - 1) % 2
    later = jnp.where(block_e[None, :] > block_e[:, None], block_e[None, :], N_EXPERTS)
    nxt = jnp.min(later, axis=1)
    nxt = jnp.where(nxt == N_EXPERTS, -1, nxt).astype(jnp.int32)
    blk = lambda i, *_: (i, 0)
    grid_spec = pltpu.PrefetchScalarGridSpec(
        num_scalar_prefetch=5,
        grid=(n_blocks,),
        in_specs=[
            *[pl.BlockSpec((MOE_BLOCK, LANES), blk)] * ROW_PLANES,
            pl.BlockSpec(memory_space=pl.ANY),
            pl.BlockSpec(memory_space=pl.ANY),
            pl.BlockSpec(memory_space=pl.ANY),
        ],
        out_specs=[pl.BlockSpec((MOE_BLOCK, LANES), blk)] * ROW_PLANES,
        scratch_shapes=[
            pltpu.VMEM((2, d, ff), F32),
            pltpu.VMEM((2, d, ff), F32),
            pltpu.VMEM((2, ff, d), F32),
            pltpu.VMEM((d, ff), BF16),
            pltpu.VMEM((d, ff), BF16),
            pltpu.VMEM((ff, d), BF16),
            pltpu.SemaphoreType.DMA((2, 3)),
        ],
    )
    return pl.pallas_call(
        _experts_kernel,
        grid_spec=grid_spec,
        out_shape=[jax.ShapeDtypeStruct((n_blocks * MOE_BLOCK, LANES), jnp.uint32)] * ROW_PLANES,
        compiler_params=_cparams(1),
        name="moe_experts",
    )(block_e, nvalid, first, slot.astype(jnp.int32), nxt, *xs_planes, w_gate, w_up, w_down)


def _combine_kernel(route_ref, h1_hbm, *refs, lp_len, tile):
    a_refs = refs[0:ROW_PLANES]
    b_refs = refs[ROW_PLANES:2 * ROW_PLANES]
    out_ref, hbuf, sem_h = refs[2 * ROW_PLANES:]
    nt = pl.num_programs(1)
    step = pl.program_id(0) * nt + pl.program_id(1)
    last = pl.num_programs(0) * nt - 1

    def h_copy(s, slot):
        start = (s // nt) * lp_len + N_META + (s % nt) * tile
        return pltpu.make_async_copy(h1_hbm.at[pl.ds(start, tile), :], hbuf.at[slot], sem_h.at[slot])

    slot = step % 2

    @pl.when(step == 0)
    def _():
        h_copy(step, slot).start()

    @pl.when(step < last)
    def _():
        h_copy(step + 1, 1 - slot).start()

    ya = _unpack_rows([r[...] for r in a_refs]).astype(F32)
    yb = _unpack_rows([r[...] for r in b_refs]).astype(F32)
    g = jnp.concatenate([route_ref[...]] * (LANES // ROUTE_COLS), axis=0).T
    moe = g[:, TOP_K:TOP_K + 1] * ya + g[:, TOP_K + 1:TOP_K + 2] * yb
    h_copy(step, slot).wait()
    out_ref[0] = hbuf[slot] + moe


def _combine(gates, h1, a_planes, b_planes, *, batch, seq, lp_len, tile):
    d = h1.shape[1]
    nt = seq // tile
    kern = functools.partial(_combine_kernel, lp_len=lp_len, tile=tile)
    rows = lambda b, i: (b * nt + i, 0)
    return pl.pallas_call(
        kern,
        grid=(batch, nt),
        in_specs=[
            pl.BlockSpec((ROUTE_COLS, tile), lambda b, i: (0, b * nt + i)),
            pl.BlockSpec(memory_space=pl.ANY),
            *[pl.BlockSpec((tile, LANES), rows)] * (2 * ROW_PLANES),
        ],
        out_specs=pl.BlockSpec((1, tile, d), lambda b, i: (b, i, 0)),
        out_shape=jax.ShapeDtypeStruct((batch, seq, d), F32),
        scratch_shapes=[pltpu.VMEM((2, tile, d), F32), pltpu.SemaphoreType.DMA((2,))],
        compiler_params=_cparams(2),
        name="moe_combine",
    )(gates, h1, *a_planes, *b_planes)


def _rope_tables(lp_len):
    half = ROPE_DIM // 2
    pos = jnp.arange(lp_len, dtype=F32)
    inv_freq = ROPE_THETA ** (-jnp.arange(0, ROPE_DIM, 2, dtype=F32) / ROPE_DIM)
    ang = pos[:, None] * inv_freq[None, :]
    lane = jnp.arange(LANES) % HEAD_DIM
    cos = jnp.tile(jnp.cos(ang), (1, LANES // half))
    sin = jnp.tile(jnp.sin(ang), (1, LANES // half))
    c = jnp.where(lane < ROPE_DIM, cos, 1.0)
    s1 = jnp.where((lane >= half) & (lane < ROPE_DIM), sin, 0.0)
    s2 = jnp.where(lane < half, -sin, 0.0)
    return c, s1, s2


def kernel(x, meta_tokens, norm1_g, w_in, conv_w, q_norm_g, k_norm_g, lambda_q1, lambda_k1, lambda_q2, lambda_k2,
           subln_g, w_out, norm2_g, w_router_group, b_router_group, w_router_expert, b_router_expert, w_gate,
           w_up, w_down):
    batch, seq, _ = x.shape
    assert w_in.shape[0] == 1, "a single layer is supported"
    l = 0
    length = seq + N_META
    tm = TOKEN_TILE
    lp_len = -(-length // tm) * tm
    tiles_per_seq = lp_len // tm
    assert tiles_per_seq >= 2 and (length - (tiles_per_seq - 1) * tm) % 8 == 0
    qw = N_HEADS * 2 * HEAD_DIM
    lam_init = 0.8 - 0.6 * math.exp(-0.3 * l)

    reps = qw // HEAD_DIM
    gq = jnp.tile(q_norm_g[l] * (HEAD_DIM ** -0.5 * LOG2E), reps)[None, :]
    gk = jnp.tile(k_norm_g[l], reps)[None, :]
    seg = jnp.arange(qw) // HEAD_DIM
    bd = (seg[:, None] == seg[None, :]).astype(BF16)
    rope = _rope_tables(lp_len)
    hp, convy, q, k, v = _inproj(x, meta_tokens.astype(x.dtype), norm1_g[l][None, :], w_in[l].astype(BF16),
                                 conv_w[l], gq, gk, bd, *rope, tiles_per_seq=tiles_per_seq, tm=tm)

    lamp = jnp.stack([lambda_q1[l], lambda_k1[l], lambda_q2[l], lambda_k2[l]]).astype(F32)
    o = _attention(q, k, v, lamp, subln_g[l][None, :], batch=batch, lp_len=lp_len, tq=tm, lam_init=lam_init)

    lane_pad = LANES - N_GROUPS - N_EXPERTS
    wr = jnp.pad(jnp.concatenate([w_router_group[l], w_router_expert[l]], axis=1), ((0, 0), (0, lane_pad)))
    wr_hi = wr.astype(BF16)
    wr_lo = (wr - wr_hi.astype(F32)).astype(BF16)
    br = jnp.pad(jnp.concatenate([b_router_group[l], b_router_expert[l]]), (0, lane_pad))[None, :]
    ridx = jnp.arange(tm)
    upper = (ridx[:, None] < ridx[None, :]).astype(BF16)
    h1, *rest = _outproj(hp, convy, o, w_out[l].astype(BF16), norm2_g[l][None, :], wr_hi, wr_lo, br, upper,
                         tiles_per_seq=tiles_per_seq, seq_len=length)
    x_planes = rest[:ROW_PLANES]
    route, cnt = rest[ROW_PLANES:]

    counts = cnt[EXPERT_LANE0:EXPERT_LANE0 + N_EXPERTS, 0].astype(jnp.int32)
    n_blocks = -(-(batch * length * TOP_K) // MOE_BLOCK) + N_EXPERTS
    p_rows = n_blocks * MOE_BLOCK
    padded = (counts + MOE_BLOCK - 1) // MOE_BLOCK * MOE_BLOCK
    pends = jnp.cumsum(padded)
    pstarts = pends - padded

    def lookup(table, idx):
        hit = idx[:, None] == jnp.arange(N_EXPERTS, dtype=jnp.int32)[None, :]
        return jnp.sum(jnp.where(hit, table[None, :], 0), axis=1)

    blk0 = jnp.arange(n_blocks, dtype=jnp.int32) * MOE_BLOCK
    block_e = jnp.minimum(jnp.sum((pends[None, :] <= blk0[:, None]).astype(jnp.int32), axis=1), N_EXPERTS - 1)
    nvalid = jnp.clip(lookup(counts, block_e) - (blk0 - lookup(pstarts, block_e)), 0, MOE_BLOCK)

    spare_rows = -(-(batch * (lp_len - length) * TOP_K) // MOE_BLOCK) * MOE_BLOCK
    dest = _slots(pstarts.astype(jnp.int32), route, batch=batch, lp_len=lp_len, seq_len=length, p_rows=p_rows)
    dest = dest.reshape(ROUTE_COLS, batch, lp_len)

    assert (batch * lp_len) % SC_WINDOW == 0 and (batch * seq) % SC_WINDOW == 0
    xs_planes = _sc_scatter_rows(x_planes, dest[0].reshape(-1, SC_WINDOW), dest[1].reshape(-1, SC_WINDOW),
                                 p_rows + spare_rows)
    y_planes = _experts(block_e, nvalid, xs_planes, w_gate[l], w_up[l], w_down[l])

    dest_x = dest[0:TOP_K, :, N_META:length]
    a_planes, b_planes = _sc_gather_rows(y_planes, dest_x[0].reshape(-1, SC_WINDOW),
                                         dest_x[1].reshape(-1, SC_WINDOW))
    route_x = route.reshape(ROUTE_COLS, batch, lp_len)[:, :, N_META:length].reshape(ROUTE_COLS, batch * seq)
    return _combine(route_x, h1, a_planes, b_planes, batch=batch, seq=seq, lp_len=lp_len,
                    tile=_largest_tile(seq, COMBINE_TILE, LANES))
```

```python
import functools
import math

import jax
import jax.numpy as jnp
from jax import lax
from jax.experimental import pallas as pl
from jax.experimental.pallas import tpu as pltpu
from jax.experimental.pallas import tpu_sc as plsc

F32 = jnp.float32
BF16 = jnp.bfloat16

N_META = 16
N_HEADS = 4
HEAD_DIM = 64
ROPE_DIM = HEAD_DIM // 4
ROPE_THETA = 500000.0
N_GROUPS = 4
EXPERTS_PER_GROUP = 8
N_EXPERTS = N_GROUPS * EXPERTS_PER_GROUP
TOP_K = 2
EPS = 1e-6
LOG2E = 1.4426950408889634

LANES = 128
TOKEN_TILE = 640
INPROJ_UNITS = 2
OUTPROJ_UNITS = 2
ATTN_TAIL_ROWS = ((0, 256), (256, 384))
ATTN_TAIL_ROWS_WIDE = ((0, 384), (384, 256))
ATTN_HEADS_PER_STEP = 4
MOE_BLOCK = 1024
MOE_TAIL_STEP = 256
COMBINE_TILE = 1024
ROUTE_COLS = 8
ROUTE_ROWS = 64
ROW_PLANES = 4
SC_WINDOW = 128
EXPERT_LANE0 = N_GROUPS
NEG_BIG = -1e30
VMEM_LIMIT = 56 * 1024 * 1024


def _largest_tile(n, cap, mult):
    for t in range(min(cap, n), 0, -1):
        if n % t == 0 and t % mult == 0:
            return t
    raise ValueError(f"no tile for {n}")


def _cparams(n_axes):
    return pltpu.CompilerParams(dimension_semantics=("arbitrary",) * n_axes, vmem_limit_bytes=VMEM_LIMIT)


def _pack_rows(x):
    w = x.shape[1] // 2
    lo = lax.bitcast_convert_type(x[:, :w].astype(BF16).astype(F32), jnp.uint32)
    hi = lax.bitcast_convert_type(x[:, w:].astype(BF16).astype(F32), jnp.uint32)
    return lax.shift_right_logical(lo, jnp.uint32(16)) | (hi & jnp.uint32(0xFFFF0000))


def _unpack_rows(planes):
    w = jnp.concatenate(planes, axis=1)
    lo = lax.bitcast_convert_type(lax.shift_left(w, jnp.uint32(16)), F32)
    hi = lax.bitcast_convert_type(w & jnp.uint32(0xFFFF0000), F32)
    return jnp.concatenate([lo, hi], axis=1).astype(BF16)


def _inproj_kernel(x_hbm, meta_hbm, g1_ref, win_ref, convw_ref, gq_ref, gk_ref, bd_ref, *refs,
                   tiles_per_seq, seq, cw, qw, units):
    rope_refs = refs[0:3 * units]
    hp_hbm, convy_ref, q_ref, k_ref, v_ref, carry_ref, xbuf, sem, sem_out = refs[3 * units:]
    i = pl.program_id(0)
    n_steps = pl.num_programs(0)
    tm = xbuf.shape[1]
    q0 = 3 * cw
    w = convw_ref[...]
    last_rows = seq + N_META - (tiles_per_seq - 1) * tm

    def fetch(tile, slot, start):
        b = tile // tiles_per_seq
        t = tile % tiles_per_seq

        def go(src, dst):
            cp = pltpu.make_async_copy(src, dst, sem.at[slot])
            if start:
                cp.start()
            else:
                cp.wait()

        @pl.when(t == 0)
        def _():
            go(meta_hbm, xbuf.at[slot, pl.ds(0, N_META)])
            go(x_hbm.at[pl.ds(b * seq, tm - N_META)], xbuf.at[slot, pl.ds(N_META, tm - N_META)])

        @pl.when((t > 0) & (t < tiles_per_seq - 1))
        def _():
            go(x_hbm.at[pl.ds(b * seq + t * tm - N_META, tm)], xbuf.at[slot])

        @pl.when(t == tiles_per_seq - 1)
        def _():
            go(x_hbm.at[pl.ds(b * seq + t * tm - N_META, last_rows)], xbuf.at[slot, pl.ds(0, last_rows)])

    def hp_store(tile, slot):
        return pltpu.make_async_copy(xbuf.at[slot], hp_hbm.at[pl.ds(tile * tm, tm)], sem_out.at[slot])

    mine = (i % 2) * units
    other = units - mine

    @pl.when(i == 0)
    def _():
        for u in range(units):
            fetch(u, mine + u, True)

    @pl.when(i > 0)
    def _():
        for u in range(units):
            hp_store((i - 1) * units + u, other + u).wait()

    @pl.when(i + 1 < n_steps)
    def _():
        for u in range(units):
            fetch((i + 1) * units + u, other + u, True)

    for u in range(units):
        tile = i * units + u
        fetch(tile, mine + u, False)

        @pl.when(tile % tiles_per_seq == tiles_per_seq - 1)
        def _():
            xbuf[mine + u, pl.ds(last_rows, tm - last_rows), :] = jnp.zeros((tm - last_rows, xbuf.shape[2]),
                                                                           xbuf.dtype)

        hp_store(tile, mine + u).start()

    prev = carry_ref[...]

    for u in range(units):
        rows = pl.ds(u * tm, tm)
        x = xbuf[mine + u]
        ms = jnp.mean(x * x, axis=-1, keepdims=True)
        xn = (x * lax.rsqrt(ms + EPS) * g1_ref[...]).astype(BF16)

        def proj(lo, hi):
            return jnp.dot(xn, win_ref[:, lo:hi], preferred_element_type=F32)

        u_conv = proj(0, q0)
        u_q = proj(q0, q0 + qw)

        z = u_conv[:, cw:2 * cw] * u_conv[:, 2 * cw:3 * cw]
        prev = jnp.where((i * units + u) % tiles_per_seq == 0, 0.0, prev)
        p1 = prev[7:8]
        p2 = prev[6:7]
        row = lax.broadcasted_iota(jnp.int32, z.shape, 0)
        z1 = jnp.where(row == 0, p1, pltpu.roll(z, 1, axis=0))
        z2 = jnp.where(row == 0, p2, jnp.where(row == 1, p1, pltpu.roll(z, 2, axis=0)))
        prev = z[tm - 8:tm]
        conv = w[0:1] * z2 + w[1:2] * z1 + w[2:3] * z
        convy_ref[rows, :] = (u_conv[:, 0:cw] * conv).astype(BF16)

        rc, rs1, rs2 = (ref[...] for ref in rope_refs[3 * u:3 * u + 3])

        def norm_rope(t, g_ref):
            ss = jnp.dot((t * t).astype(BF16), bd_ref[...], preferred_element_type=F32)
            tn = t * lax.rsqrt(ss * (1.0 / HEAD_DIM) + EPS) * g_ref[...]
            outs = []
            for c in range(qw // LANES):
                ch = tn[:, c * LANES:(c + 1) * LANES]
                outs.append(ch * rc + pltpu.roll(ch, ROPE_DIM // 2, axis=1) * rs1
                            + pltpu.roll(ch, LANES - ROPE_DIM // 2, axis=1) * rs2)
            return jnp.concatenate(outs, axis=1).astype(BF16)

        u_k = proj(q0 + qw, q0 + 2 * qw)
        q_ref[rows, :] = norm_rope(u_q, gq_ref)
        u_v = proj(q0 + 2 * qw, win_ref.shape[1])
        k_ref[rows, :] = norm_rope(u_k, gk_ref)
        v_ref[rows, :] = u_v.astype(BF16)

    carry_ref[...] = prev

    @pl.when(i == n_steps - 1)
    def _():
        for u in range(units):
            hp_store(i * units + u, mine + u).wait()


def _inproj(x, meta, g1, w_in, conv_w, gq, gk, bd, rc, rs1, rs2, *, tiles_per_seq, tm):
    batch, seq, d = x.shape
    units = INPROJ_UNITS
    n = batch * tiles_per_seq * tm
    assert (batch * tiles_per_seq) % units == 0
    cw = conv_w.shape[1]
    qw = gq.shape[1]
    aw = w_in.shape[1] - 3 * cw - 2 * qw
    const = lambda i: (0, 0)
    step = lambda i: (i, 0)
    rope_specs = [pl.BlockSpec((tm, LANES), lambda i, u=u: ((i * units + u) % tiles_per_seq, 0))
                  for u in range(units) for _ in range(3)]
    kern = functools.partial(_inproj_kernel, tiles_per_seq=tiles_per_seq, seq=seq, cw=cw, qw=qw, units=units)
    return pl.pallas_call(
        kern,
        grid=(n // (units * tm),),
        in_specs=[
            pl.BlockSpec(memory_space=pl.ANY),
            pl.BlockSpec(memory_space=pl.ANY),
            pl.BlockSpec((1, d), const),
            pl.BlockSpec(w_in.shape, const),
            pl.BlockSpec(conv_w.shape, const),
            pl.BlockSpec((1, qw), const),
            pl.BlockSpec((1, qw), const),
            pl.BlockSpec(bd.shape, const),
            *rope_specs,
        ],
        out_specs=[
            pl.BlockSpec(memory_space=pl.ANY),
            pl.BlockSpec((units * tm, cw), step),
            pl.BlockSpec((units * tm, qw), step),
            pl.BlockSpec((units * tm, qw), step),
            pl.BlockSpec((units * tm, aw), step),
        ],
        out_shape=[
            jax.ShapeDtypeStruct((n, d), x.dtype),
            jax.ShapeDtypeStruct((n, cw), BF16),
            jax.ShapeDtypeStruct((n, qw), BF16),
            jax.ShapeDtypeStruct((n, qw), BF16),
            jax.ShapeDtypeStruct((n, aw), BF16),
        ],
        scratch_shapes=[pltpu.VMEM((8, cw), F32), pltpu.VMEM((2 * units, tm, d), x.dtype),
                        pltpu.SemaphoreType.DMA((2 * units,)), pltpu.SemaphoreType.DMA((2 * units,))],
        compiler_params=_cparams(1),
        name="inproj",
    )(x.reshape(batch * seq, d), meta, g1, w_in, conv_w, gq, gk, bd, *([rc, rs1, rs2] * units))


def _attn_kernel(q_ref, k_ref, v_ref, lamp_ref, sg_ref, o_ref, qs_ref, m_ref, l_ref, acc_ref, *, lam_init):
    qi = pl.program_id(2)
    tq = q_ref.shape[0]
    n_heads = q_ref.shape[1] // LANES
    n_chains = 2 * n_heads
    lane = lax.broadcasted_iota(jnp.int32, (tq, LANES), 1)
    for h in range(n_heads):
        q = q_ref[:, h * LANES:(h + 1) * LANES]
        zero = jnp.zeros_like(q)
        qs_ref[pl.ds(2 * h * tq, tq), :] = jnp.where(lane < HEAD_DIM, q, zero)
        qs_ref[pl.ds((2 * h + 1) * tq, tq), :] = jnp.where(lane >= HEAD_DIM, q, zero)

    def scores(off, width, which, r0=0, nr=None):
        h = which // 2
        nr = tq if nr is None else nr
        kc = k_ref[pl.ds(off, width), h * LANES:(h + 1) * LANES]
        return lax.dot_general(qs_ref[pl.ds(which * tq + r0, nr), :], kc, (((1,), (1,)), ((), ())),
                               preferred_element_type=F32)

    def update(off, width, which, s, first, r0=0, nr=None, diag=None):
        h = which // 2
        nr = tq if nr is None else nr
        vc = jnp.concatenate([v_ref[pl.ds(off, width), h * LANES:(h + 1) * LANES],
                              jnp.ones((width, LANES), BF16)], axis=1)
        rows = pl.ds(which * tq + r0, nr)
        if diag is not None:
            r = lax.broadcasted_iota(jnp.int32, s.shape, 0)
            c = lax.broadcasted_iota(jnp.int32, s.shape, 1)
            s = jnp.where(c <= r + (r0 + diag), s, NEG_BIG)
        m_prev = jnp.where(first, NEG_BIG, m_ref[rows, :])
        l_prev = jnp.where(first, 0.0, l_ref[rows, :])
        acc_prev = jnp.where(first, 0.0, acc_ref[rows, :])
        m_new = jnp.maximum(m_prev, jnp.max(s, axis=-1, keepdims=True))
        alpha = jnp.exp2(m_prev - m_new)
        p = jnp.exp2((s - jnp.tile(m_new, (1, width // LANES))).astype(BF16))
        pv = jnp.dot(p, vc, preferred_element_type=F32)
        l_ref[rows, :] = alpha * l_prev + pv[:, LANES:]
        acc_ref[rows, :] = alpha * acc_prev + pv[:, :LANES]
        m_ref[rows, :] = m_new

    def run_pieces(pieces, first):
        s_next = scores(pieces[0][1], pieces[0][2], pieces[0][0], pieces[0][3], pieces[0][4])
        for n, (c, off, width, r0, nr, diag) in enumerate(pieces):
            s = s_next
            if n + 1 < len(pieces):
                c2, off2, width2, r02, nr2, _ = pieces[n + 1]
                s_next = scores(off2, width2, c2, r02, nr2)
            update(off, width, c, s, first, r0, nr, diag)

    def chunk(off, width, first):
        run_pieces([(c, off, width, 0, tq, None) for c in range(n_chains)], first)

    def tail_chunk(off, width, row_groups, first):
        pieces = [(c, off, width - tq + r0 + nr, r0, nr, width - tq)
                  for c in range(n_chains) for r0, nr in row_groups]
        run_pieces(pieces, first)

    wide = 2 * tq

    def body(j, carry):
        chunk(pl.multiple_of(j * wide, wide), wide, j == 0)
        return carry

    lax.fori_loop(0, qi // 2, body, 0)
    odd = qi % 2 == 1

    @pl.when(odd)
    def _():
        tail_chunk(pl.multiple_of((qi - 1) * tq, tq), wide, ATTN_TAIL_ROWS_WIDE, qi < 2)

    @pl.when(jnp.logical_not(odd))
    def _():
        tail_chunk(pl.multiple_of(qi * tq, tq), tq, ATTN_TAIL_ROWS, qi < 2)

    lp = lamp_ref[...]
    lam = (jnp.exp(jnp.sum(lp[0:1] * lp[1:2], axis=-1, keepdims=True))
           - jnp.exp(jnp.sum(lp[2:3] * lp[3:4], axis=-1, keepdims=True)) + lam_init)
    for h in range(n_heads):
        rows = pl.ds(2 * h * tq, 2 * tq)
        o_all = acc_ref[rows, :] / l_ref[rows, :]
        o = o_all[0:tq] - lam * o_all[tq:2 * tq]
        ms = jnp.mean(o * o, axis=-1, keepdims=True)
        o_ref[:, h * LANES:(h + 1) * LANES] = (o * lax.rsqrt(ms + EPS) * sg_ref[...]
                                               * (1.0 - lam_init)).astype(BF16)


def _attention(q, k, v, lamp, sg, *, batch, lp_len, tq, lam_init):
    n, qw = q.shape
    nq = lp_len // tq
    hw = ATTN_HEADS_PER_STEP * LANES
    chains = 2 * ATTN_HEADS_PER_STEP
    kern = functools.partial(_attn_kernel, lam_init=lam_init)
    return pl.pallas_call(
        kern,
        grid=(batch, qw // hw, nq),
        in_specs=[
            pl.BlockSpec((tq, hw), lambda b, h, i: (b * nq + i, h)),
            pl.BlockSpec((lp_len, hw), lambda b, h, i: (b, h)),
            pl.BlockSpec((lp_len, hw), lambda b, h, i: (b, h)),
            pl.BlockSpec(lamp.shape, lambda b, h, i: (0, 0)),
            pl.BlockSpec(sg.shape, lambda b, h, i: (0, 0)),
        ],
        out_specs=pl.BlockSpec((tq, hw), lambda b, h, i: (b * nq + i, h)),
        out_shape=jax.ShapeDtypeStruct((n, v.shape[1]), BF16),
        scratch_shapes=[
            pltpu.VMEM((chains * tq, LANES), BF16),
            pltpu.VMEM((chains * tq, LANES), F32),
            pltpu.VMEM((chains * tq, LANES), F32),
            pltpu.VMEM((chains * tq, LANES), F32),
        ],
        compiler_params=_cparams(3),
        name="diffattn",
    )(q, k, v, lamp, sg)


def _outproj_kernel(hp_ref, cy_ref, o_ref, wout_ref, g2_ref, wrh_ref, wrl_ref, br_ref, upper_ref,
                    h1_ref, xp0_ref, xp1_ref, xp2_ref, xp3_ref, route_ref, cnt_ref, run_ref,
                    *, tiles_per_seq, seq_len):
    i = pl.program_id(0)
    tm = upper_ref.shape[0]
    n_units = hp_ref.shape[0] // tm

    @pl.when(i == 0)
    def _():
        run_ref[...] = jnp.zeros_like(run_ref)

    def project(rows):
        mix = jnp.concatenate([cy_ref[rows, :], o_ref[rows, :]], axis=1)
        h1 = hp_ref[rows, :] + jnp.dot(mix, wout_ref[...], preferred_element_type=F32)
        h1_ref[rows, :] = h1
        ms = jnp.mean(h1 * h1, axis=-1, keepdims=True)
        xn = h1 * lax.rsqrt(ms + EPS) * g2_ref[...]
        xw = _pack_rows(xn)
        for c, ref in enumerate((xp0_ref, xp1_ref, xp2_ref, xp3_ref)):
            ref[rows, :] = xw[:, c * LANES:(c + 1) * LANES]
        x_hi = xn.astype(BF16)
        x_lo = (xn - x_hi.astype(F32)).astype(BF16)
        hi_both = jnp.dot(x_hi, jnp.concatenate([wrh_ref[...], wrl_ref[...]], axis=1),
                          preferred_element_type=F32)
        return (hi_both[:, :LANES] + hi_both[:, LANES:]
                + jnp.dot(x_lo, wrh_ref[...], preferred_element_type=F32) + br_ref[...])

    def route(u, logits, run):
        lt = logits.T[0:ROUTE_ROWS, :]
        row = lax.broadcasted_iota(jnp.int32, lt.shape, 0)
        big = jnp.int32(4 * LANES)

        def first_argmax(vals, vmax):
            return jnp.min(jnp.where(vals == vmax, row, big), axis=0, keepdims=True)

        gl = jnp.where(row < N_GROUPS, lt, NEG_BIG)
        gmax = jnp.max(gl, axis=0, keepdims=True)
        g_val = 1.0 / jnp.sum(jnp.exp(gl - gmax), axis=0, keepdims=True)
        g_idx = first_argmax(gl, gmax)
        lo = EXPERT_LANE0 + EXPERTS_PER_GROUP * g_idx
        el = jnp.where((row >= lo) & (row < lo + EXPERTS_PER_GROUP), lt, NEG_BIG)
        m1 = jnp.max(el, axis=0, keepdims=True)
        i1 = first_argmax(el, m1)
        el2 = jnp.where(row == i1, NEG_BIG, el)
        m2 = jnp.max(el2, axis=0, keepdims=True)
        i2 = first_argmax(el2, m2)
        r = jnp.exp(m2 - m1)
        gate1 = g_val / (1.0 + r)
        gate2 = g_val * r / (1.0 + r)

        pos = ((i * n_units + u) % tiles_per_seq) * tm + lax.broadcasted_iota(jnp.int32, (1, tm), 1)
        valid = pos < seq_len
        oh1 = jnp.where(valid & (row == i1), 1.0, 0.0)
        oh2 = jnp.where(valid & (row == i2), 1.0, 0.0)
        pre = jnp.dot(jnp.concatenate([oh1, oh2], axis=0).astype(BF16), upper_ref[...],
                      preferred_element_type=F32)
        tot1 = jnp.sum(oh1, axis=1, keepdims=True)
        tot2 = jnp.sum(oh2, axis=1, keepdims=True)
        run_t = jnp.tile(run, (1, tm // LANES))
        rank1 = jnp.sum(oh1 * (pre[:ROUTE_ROWS] + run_t), axis=0, keepdims=True)
        rank2 = jnp.sum(oh2 * (pre[ROUTE_ROWS:] + run_t + tot1), axis=0, keepdims=True)

        e1 = (i1 - EXPERT_LANE0).astype(F32)
        e2 = (i2 - EXPERT_LANE0).astype(F32)
        r8 = lax.broadcasted_iota(jnp.int32, (ROUTE_COLS, tm), 0)
        route_ref[:, u * tm:(u + 1) * tm] = jnp.where(r8 == 0, e1, jnp.where(r8 == 1, e2, jnp.where(
            r8 == 2, gate1, jnp.where(r8 == 3, gate2, jnp.where(r8 == 4, rank1, jnp.where(r8 == 5, rank2, 0.0))))))
        return run + tot1 + tot2

    all_logits = [project(pl.ds(u * tm, tm)) for u in range(n_units)]
    run = run_ref[...]
    for u, logits in enumerate(all_logits):
        run = route(u, logits, run)
    run_ref[...] = run
    cnt_ref[...] = run


def _outproj(hp, convy, o, w_out, g2, wr_hi, wr_lo, br, upper, *, tiles_per_seq, seq_len):
    n, d = hp.shape
    tm = OUTPROJ_UNITS * upper.shape[0]
    assert n % tm == 0
    const = lambda i: (0, 0)
    tile = lambda i: (i, 0)
    kern = functools.partial(_outproj_kernel, tiles_per_seq=tiles_per_seq, seq_len=seq_len)
    return pl.pallas_call(
        kern,
        grid=(n // tm,),
        in_specs=[
            pl.BlockSpec((tm, d), tile),
            pl.BlockSpec((tm, convy.shape[1]), tile),
            pl.BlockSpec((tm, o.shape[1]), tile),
            pl.BlockSpec(w_out.shape, const),
            pl.BlockSpec((1, d), const),
            pl.BlockSpec(wr_hi.shape, const),
            pl.BlockSpec(wr_lo.shape, const),
            pl.BlockSpec((1, LANES), const),
            pl.BlockSpec(upper.shape, const),
        ],
        out_specs=[
            pl.BlockSpec((tm, d), tile),
            *[pl.BlockSpec((tm, LANES), tile)] * ROW_PLANES,
            pl.BlockSpec((ROUTE_COLS, tm), lambda i: (0, i)),
            pl.BlockSpec((ROUTE_ROWS, LANES), const),
        ],
        out_shape=[
            jax.ShapeDtypeStruct((n, d), F32),
            *[jax.ShapeDtypeStruct((n, LANES), jnp.uint32)] * ROW_PLANES,
            jax.ShapeDtypeStruct((ROUTE_COLS, n), F32),
            jax.ShapeDtypeStruct((ROUTE_ROWS, LANES), F32),
        ],
        scratch_shapes=[pltpu.VMEM((ROUTE_ROWS, LANES), F32)],
        compiler_params=_cparams(1),
        name="outproj_router",
    )(hp, convy, o, w_out, g2, wr_hi, wr_lo, br, upper)


def _slots_kernel(pstart_ref, route_ref, dest_ref, *, seq_len, p_rows):
    b = pl.program_id(0)
    route = route_ref[...]
    eid = route.astype(jnp.int32)
    start = jnp.zeros_like(eid)
    for e in range(N_EXPERTS):
        start = jnp.where(eid == e, pstart_ref[e], start)
    rank = pltpu.roll(route, ROUTE_COLS - 2 * TOP_K, axis=0).astype(jnp.int32)
    k = lax.broadcasted_iota(jnp.int32, route.shape, 0)
    pos = lax.broadcasted_iota(jnp.int32, route.shape, 1)
    n_pad = route.shape[1] - seq_len
    spare = p_rows + (b * n_pad + (pos - seq_len)) * TOP_K + k
    dest_ref[...] = jnp.where(pos < seq_len, start + rank, spare)


def _slots(pstarts, route, *, batch, lp_len, seq_len, p_rows):
    kern = functools.partial(_slots_kernel, seq_len=seq_len, p_rows=p_rows)
    grid_spec = pltpu.PrefetchScalarGridSpec(
        num_scalar_prefetch=1,
        grid=(batch,),
        in_specs=[pl.BlockSpec((ROUTE_COLS, lp_len), lambda b, ps: (0, b))],
        out_specs=pl.BlockSpec((ROUTE_COLS, lp_len), lambda b, ps: (0, b)),
    )
    return pl.pallas_call(
        kern,
        grid_spec=grid_spec,
        out_shape=jax.ShapeDtypeStruct(route.shape, jnp.int32),
        compiler_params=_cparams(1),
        name="moe_slots",
    )(pstarts, route)


def _sc_workers():
    info = plsc.get_sparse_core_info()
    return info.num_cores, info.num_cores * info.num_subcores


def _sc_scatter_rows(planes, idx_a, idx_b, out_rows):
    n_win = idx_a.shape[0]
    n_cores, n_workers = _sc_workers()
    trips = -(-n_win // n_workers)
    mesh = plsc.VectorSubcoreMesh(core_axis_name="c", subcore_axis_name="s")

    def body(*refs):
        xs = refs[0:ROW_PLANES]
        ia_hbm, ib_hbm = refs[ROW_PLANES:ROW_PLANES + 2]
        outs = refs[ROW_PLANES + 2:2 * ROW_PLANES + 2]
        ia_v, ib_v, buf, sem = refs[2 * ROW_PLANES + 2:]
        wid = lax.axis_index("s") * n_cores + lax.axis_index("c")

        def step(t, carry):
            g = wid + t * n_workers

            @pl.when(g < n_win)
            def _():
                row0 = pl.multiple_of(g * SC_WINDOW, SC_WINDOW)
                loads = [pltpu.async_copy(ia_hbm.at[g], ia_v, sem), pltpu.async_copy(ib_hbm.at[g], ib_v, sem)]
                loads += [pltpu.async_copy(xs[c].at[pl.ds(row0, SC_WINDOW)], buf.at[c], sem)
                          for c in range(ROW_PLANES)]
                for cp in loads:
                    cp.wait()
                stores = [pltpu.async_copy(buf.at[c], outs[c].at[iv], sem)
                          for c in range(ROW_PLANES) for iv in (ia_v, ib_v)]
                for cp in stores:
                    cp.wait()

            return carry

        lax.fori_loop(0, trips, step, 0)

    kern = pl.kernel(
        body,
        out_type=[jax.ShapeDtypeStruct((out_rows, LANES), jnp.uint32)] * ROW_PLANES,
        mesh=mesh,
        scratch_types=[
            pltpu.VMEM((SC_WINDOW,), jnp.int32),
            pltpu.VMEM((SC_WINDOW,), jnp.int32),
            pltpu.VMEM((ROW_PLANES, SC_WINDOW, LANES), jnp.uint32),
            pltpu.SemaphoreType.DMA,
        ],
        name="moe_dispatch_sc",
    )
    return kern(*planes, idx_a, idx_b)


def _sc_gather_rows(planes, idx_a, idx_b):
    n_win = idx_a.shape[0]
    n_cores, n_workers = _sc_workers()
    trips = -(-n_win // n_workers)
    mesh = plsc.VectorSubcoreMesh(core_axis_name="c", subcore_axis_name="s")

    def body(*refs):
        ys = refs[0:ROW_PLANES]
        ia_hbm, ib_hbm = refs[ROW_PLANES:ROW_PLANES + 2]
        outs_a = refs[ROW_PLANES + 2:2 * ROW_PLANES + 2]
        outs_b = refs[2 * ROW_PLANES + 2:3 * ROW_PLANES + 2]
        ia_v, ib_v, buf, sem = refs[3 * ROW_PLANES + 2:]
        wid = lax.axis_index("s") * n_cores + lax.axis_index("c")

        def step(t, carry):
            g = wid + t * n_workers

            @pl.when(g < n_win)
            def _():
                row0 = pl.multiple_of(g * SC_WINDOW, SC_WINDOW)
                idx_loads = [pltpu.async_copy(ia_hbm.at[g], ia_v, sem), pltpu.async_copy(ib_hbm.at[g], ib_v, sem)]
                for cp in idx_loads:
                    cp.wait()
                for iv, outs in ((ia_v, outs_a), (ib_v, outs_b)):
                    loads = [pltpu.async_copy(ys[c].at[iv], buf.at[c], sem) for c in range(ROW_PLANES)]
                    for cp in loads:
                        cp.wait()
                    stores = [pltpu.async_copy(buf.at[c], outs[c].at[pl.ds(row0, SC_WINDOW)], sem)
                              for c in range(ROW_PLANES)]
                    for cp in stores:
                        cp.wait()

            return carry

        lax.fori_loop(0, trips, step, 0)

    n_rows = n_win * SC_WINDOW
    kern = pl.kernel(
        body,
        out_type=[jax.ShapeDtypeStruct((n_rows, LANES), jnp.uint32)] * (2 * ROW_PLANES),
        mesh=mesh,
        scratch_types=[
            pltpu.VMEM((SC_WINDOW,), jnp.int32),
            pltpu.VMEM((SC_WINDOW,), jnp.int32),
            pltpu.VMEM((ROW_PLANES, SC_WINDOW, LANES), jnp.uint32),
            pltpu.SemaphoreType.DMA,
        ],
        name="moe_gather_sc",
    )
    res = kern(*planes, idx_a, idx_b)
    return res[:ROW_PLANES], res[ROW_PLANES:]


def _experts_kernel(be_ref, nv_ref, first_ref, slot_ref, nxt_ref, x0_ref, x1_ref, x2_ref, x3_ref,
                    wg_hbm, wu_hbm, wd_hbm, y0_ref, y1_ref, y2_ref, y3_ref,
                    wgf_ref, wuf_ref, wdf_ref, wgb_ref, wub_ref, wdb_ref, sem):
    i = pl.program_id(0)
    e = be_ref[i]
    slot = slot_ref[i]
    y_refs = (y0_ref, y1_ref, y2_ref, y3_ref)

    def weight_copies(expert, s):
        return [pltpu.make_async_copy(hbm.at[expert], stage.at[s], sem.at[s, j])
                for j, (hbm, stage) in enumerate(((wg_hbm, wgf_ref), (wu_hbm, wuf_ref), (wd_hbm, wdf_ref)))]

    @pl.when(i == 0)
    def _():
        for cp in weight_copies(e, slot):
            cp.start()

    @pl.when(first_ref[i] == 1)
    def _():
        for cp in weight_copies(e, slot):
            cp.wait()
        nxt = nxt_ref[i]

        @pl.when(nxt >= 0)
        def _():
            for cp in weight_copies(nxt, 1 - slot):
                cp.start()

        wgb_ref[...] = wgf_ref[slot].astype(BF16)
        wub_ref[...] = wuf_ref[slot].astype(BF16)
        wdb_ref[...] = wdf_ref[slot].astype(BF16)

    nv = nv_ref[i]

    def mlp(rows):
        xs = _unpack_rows([r[rows, :] for r in (x0_ref, x1_ref, x2_ref, x3_ref)])
        row = lax.broadcasted_iota(jnp.int32, xs.shape, 0)
        x = jnp.where(row < nv, xs, jnp.zeros_like(xs))
        hg = jnp.dot(x, wgb_ref[...], preferred_element_type=F32)
        hu = jnp.dot(x, wub_ref[...], preferred_element_type=F32)
        hid = (hg / (1.0 + jnp.exp(-hg)) * hu).astype(BF16)
        yw = _pack_rows(jnp.dot(hid, wdb_ref[...], preferred_element_type=F32))
        for c, ref in enumerate(y_refs):
            ref[rows, :] = yw[:, c * LANES:(c + 1) * LANES]

    def clear(rows):
        for ref in y_refs:
            ref[rows, :] = jnp.zeros((rows.size, LANES), ref.dtype)

    for piece in range(MOE_BLOCK // MOE_TAIL_STEP + 1):
        used = piece * MOE_TAIL_STEP

        @pl.when((nv > used - MOE_TAIL_STEP) & (nv <= used))
        def _():
            if used > 0:
                mlp(pl.ds(0, used))
            if used < MOE_BLOCK:
                clear(pl.ds(used, MOE_BLOCK - used))


def _experts(block_e, nvalid, xs_planes, w_gate, w_up, w_down):
    n_blocks = block_e.shape[0]
    _, d, ff = w_gate.shape
    first = jnp.concatenate([jnp.ones((1,), jnp.int32), (block_e[1:] != block_e[:-1]).astype(jnp.int32)])
    slot = (jnp.cumsum(first) - 1) % 2
    later = jnp.where(block_e[None, :] > block_e[:, None], block_e[None, :], N_EXPERTS)
    nxt = jnp.min(later, axis=1)
    nxt = jnp.where(nxt == N_EXPERTS, -1, nxt).astype(jnp.int32)
    blk = lambda i, *_: (i, 0)
    grid_spec = pltpu.PrefetchScalarGridSpec(
        num_scalar_prefetch=5,
        grid=(n_blocks,),
        in_specs=[
            *[pl.BlockSpec((MOE_BLOCK, LANES), blk)] * ROW_PLANES,
            pl.BlockSpec(memory_space=pl.ANY),
            pl.BlockSpec(memory_space=pl.ANY),
            pl.BlockSpec(memory_space=pl.ANY),
        ],
        out_specs=[pl.BlockSpec((MOE_BLOCK, LANES), blk)] * ROW_PLANES,
        scratch_shapes=[
            pltpu.VMEM((2, d, ff), F32),
            pltpu.VMEM((2, d, ff), F32),
            pltpu.VMEM((2, ff, d), F32),
            pltpu.VMEM((d, ff), BF16),
            pltpu.VMEM((d, ff), BF16),
            pltpu.VMEM((ff, d), BF16),
            pltpu.SemaphoreType.DMA((2, 3)),
        ],
    )
    return pl.pallas_call(
        _experts_kernel,
        grid_spec=grid_spec,
        out_shape=[jax.ShapeDtypeStruct((n_blocks * MOE_BLOCK, LANES), jnp.uint32)] * ROW_PLANES,
        compiler_params=_cparams(1),
        name="moe_experts",
    )(block_e, nvalid, first, slot.astype(jnp.int32), nxt, *xs_planes, w_gate, w_up, w_down)


def _combine_kernel(route_ref, h1_hbm, *refs, lp_len, tile):
    a_refs = refs[0:ROW_PLANES]
    b_refs = refs[ROW_PLANES:2 * ROW_PLANES]
    out_ref, hbuf, sem_h = refs[2 * ROW_PLANES:]
    nt = pl.num_programs(1)
    step = pl.program_id(0) * nt + pl.program_id(1)
    last = pl.num_programs(0) * nt - 1

    def h_copy(s, slot):
        start = (s // nt) * lp_len + N_META + (s % nt) * tile
        return pltpu.make_async_copy(h1_hbm.at[pl.ds(start, tile), :], hbuf.at[slot], sem_h.at[slot])

    slot = step % 2

    @pl.when(step == 0)
    def _():
        h_copy(step, slot).start()

    @pl.when(step < last)
    def _():
        h_copy(step + 1, 1 - slot).start()

    ya = _unpack_rows([r[...] for r in a_refs]).astype(F32)
    yb = _unpack_rows([r[...] for r in b_refs]).astype(F32)
    g = jnp.concatenate([route_ref[...]] * (LANES // ROUTE_COLS), axis=0).T
    moe = g[:, TOP_K:TOP_K + 1] * ya + g[:, TOP_K + 1:TOP_K + 2] * yb
    h_copy(step, slot).wait()
    out_ref[0] = hbuf[slot] + moe


def _combine(gates, h1, a_planes, b_planes, *, batch, seq, lp_len, tile):
    d = h1.shape[1]
    nt = seq // tile
    kern = functools.partial(_combine_kernel, lp_len=lp_len, tile=tile)
    rows = lambda b, i: (b * nt + i, 0)
    return pl.pallas_call(
        kern,
        grid=(batch, nt),
        in_specs=[
            pl.BlockSpec((ROUTE_COLS, tile), lambda b, i: (0, b * nt + i)),
            pl.BlockSpec(memory_space=pl.ANY),
            *[pl.BlockSpec((tile, LANES), rows)] * (2 * ROW_PLANES),
        ],
        out_specs=pl.BlockSpec((1, tile, d), lambda b, i: (b, i, 0)),
        out_shape=jax.ShapeDtypeStruct((batch, seq, d), F32),
        scratch_shapes=[pltpu.VMEM((2, tile, d), F32), pltpu.SemaphoreType.DMA((2,))],
        compiler_params=_cparams(2),
        name="moe_combine",
    )(gates, h1, *a_planes, *b_planes)


def _rope_tables(lp_len):
    half = ROPE_DIM // 2
    pos = jnp.arange(lp_len, dtype=F32)
    inv_freq = ROPE_THETA ** (-jnp.arange(0, ROPE_DIM, 2, dtype=F32) / ROPE_DIM)
    ang = pos[:, None] * inv_freq[None, :]
    lane = jnp.arange(LANES) % HEAD_DIM
    cos = jnp.tile(jnp.cos(ang), (1, LANES // half))
    sin = jnp.tile(jnp.sin(ang), (1, LANES // half))
    c = jnp.where(lane < ROPE_DIM, cos, 1.0)
    s1 = jnp.where((lane >= half) & (lane < ROPE_DIM), sin, 0.0)
    s2 = jnp.where(lane < half, -sin, 0.0)
    return c, s1, s2


def kernel(x, meta_tokens, norm1_g, w_in, conv_w, q_norm_g, k_norm_g, lambda_q1, lambda_k1, lambda_q2, lambda_k2,
           subln_g, w_out, norm2_g, w_router_group, b_router_group, w_router_expert, b_router_expert, w_gate,
           w_up, w_down):
    batch, seq, _ = x.shape
    assert w_in.shape[0] == 1, "a single layer is supported"
    l = 0
    length = seq + N_META
    tm = TOKEN_TILE
    lp_len = -(-length // tm) * tm
    tiles_per_seq = lp_len // tm
    assert tiles_per_seq >= 2 and (length - (tiles_per_seq - 1) * tm) % 8 == 0
    qw = N_HEADS * 2 * HEAD_DIM
    lam_init = 0.8 - 0.6 * math.exp(-0.3 * l)

    reps = qw // HEAD_DIM
    gq = jnp.tile(q_norm_g[l] * (HEAD_DIM ** -0.5 * LOG2E), reps)[None, :]
    gk = jnp.tile(k_norm_g[l], reps)[None, :]
    seg = jnp.arange(qw) // HEAD_DIM
    bd = (seg[:, None] == seg[None, :]).astype(BF16)
    rope = _rope_tables(lp_len)
    hp, convy, q, k, v = _inproj(x, meta_tokens.astype(x.dtype), norm1_g[l][None, :], w_in[l].astype(BF16),
                                 conv_w[l], gq, gk, bd, *rope, tiles_per_seq=tiles_per_seq, tm=tm)

    lamp = jnp.stack([lambda_q1[l], lambda_k1[l], lambda_q2[l], lambda_k2[l]]).astype(F32)
    o = _attention(q, k, v, lamp, subln_g[l][None, :], batch=batch, lp_len=lp_len, tq=tm, lam_init=lam_init)

    lane_pad = LANES - N_GROUPS - N_EXPERTS
    wr = jnp.pad(jnp.concatenate([w_router_group[l], w_router_expert[l]], axis=1), ((0, 0), (0, lane_pad)))
    wr_hi = wr.astype(BF16)
    wr_lo = (wr - wr_hi.astype(F32)).astype(BF16)
    br = jnp.pad(jnp.concatenate([b_router_group[l], b_router_expert[l]]), (0, lane_pad))[None, :]
    ridx = jnp.arange(tm)
    upper = (ridx[:, None] < ridx[None, :]).astype(BF16)
    h1, *rest = _outproj(hp, convy, o, w_out[l].astype(BF16), norm2_g[l][None, :], wr_hi, wr_lo, br, upper,
                         tiles_per_seq=tiles_per_seq, seq_len=length)
    x_planes = rest[:ROW_PLANES]
    route, cnt = rest[ROW_PLANES:]

    counts = cnt[EXPERT_LANE0:EXPERT_LANE0 + N_EXPERTS, 0].astype(jnp.int32)
    n_blocks = -(-(batch * length * TOP_K) // MOE_BLOCK) + N_EXPERTS
    p_rows = n_blocks * MOE_BLOCK
    padded = (counts + MOE_BLOCK - 1) // MOE_BLOCK * MOE_BLOCK
    pends = jnp.cumsum(padded)
    pstarts = pends - padded

    def lookup(table, idx):
        hit = idx[:, None] == jnp.arange(N_EXPERTS, dtype=jnp.int32)[None, :]
        return jnp.sum(jnp.where(hit, table[None, :], 0), axis=1)

    blk0 = jnp.arange(n_blocks, dtype=jnp.int32) * MOE_BLOCK
    block_e = jnp.minimum(jnp.sum((pends[None, :] <= blk0[:, None]).astype(jnp.int32), axis=1), N_EXPERTS - 1)
    nvalid = jnp.clip(lookup(counts, block_e) - (blk0 - lookup(pstarts, block_e)), 0, MOE_BLOCK)

    spare_rows = -(-(batch * (lp_len - length) * TOP_K) // MOE_BLOCK) * MOE_BLOCK
    dest = _slots(pstarts.astype(jnp.int32), route, batch=batch, lp_len=lp_len, seq_len=length, p_rows=p_rows)
    dest = dest.reshape(ROUTE_COLS, batch, lp_len)

    assert (batch * lp_len) % SC_WINDOW == 0 and (batch * seq) % SC_WINDOW == 0
    xs_planes = _sc_scatter_rows(x_planes, dest[0].reshape(-1, SC_WINDOW), dest[1].reshape(-1, SC_WINDOW),
                                 p_rows + spare_rows)
    y_planes = _experts(block_e, nvalid, xs_planes, w_gate[l], w_up[l], w_down[l])

    dest_x = dest[0:TOP_K, :, N_META:length]
    a_planes, b_planes = _sc_gather_rows(y_planes, dest_x[0].reshape(-1, SC_WINDOW),
                                         dest_x[1].reshape(-1, SC_WINDOW))
    route_x = route.reshape(ROUTE_COLS, batch, lp_len)[:, :, N_META:length].reshape(ROUTE_COLS, batch * seq)
    return _combine(route_x, h1, a_planes, b_planes, batch=batch, seq=seq, lp_len=lp_len,
                    tile=_largest_tile(seq, COMBINE_TILE, LANES))
```

```python
import functools
import math

import jax
import jax.numpy as jnp
from jax import lax
from jax.experimental import pallas as pl
from jax.experimental.pallas import tpu as pltpu
from jax.experimental.pallas import tpu_sc as plsc

F32 = jnp.float32
BF16 = jnp.bfloat16

N_META = 16
N_HEADS = 4
HEAD_DIM = 64
ROPE_DIM = HEAD_DIM // 4
ROPE_THETA = 500000.0
N_GROUPS = 4
EXPERTS_PER_GROUP = 8
N_EXPERTS = N_GROUPS * EXPERTS_PER_GROUP
TOP_K = 2
EPS = 1e-6
LOG2E = 1.4426950408889634

LANES = 128
TOKEN_TILE = 640
INPROJ_UNITS = 2
OUTPROJ_UNITS = 2
ATTN_TAIL_ROWS = ((0, 256), (256, 384))
ATTN_TAIL_ROWS_WIDE = ((0, 384), (384, 256))
ATTN_HEADS_PER_STEP = 4
MOE_BLOCK = 1024
MOE_TAIL_STEP = 256
COMBINE_TILE = 1024
ROUTE_COLS = 8
ROUTE_ROWS = 64
ROW_PLANES = 4
SC_WINDOW = 128
EXPERT_LANE0 = N_GROUPS
NEG_BIG = -1e30
VMEM_LIMIT = 56 * 1024 * 1024


def _largest_tile(n, cap, mult):
    for t in range(min(cap, n), 0, -1):
        if n % t == 0 and t % mult == 0:
            return t
    raise ValueError(f"no tile for {n}")


def _cparams(n_axes):
    return pltpu.CompilerParams(dimension_semantics=("arbitrary",) * n_axes, vmem_limit_bytes=VMEM_LIMIT)


def _pack_rows(x):
    w = x.shape[1] // 2
    lo = lax.bitcast_convert_type(x[:, :w].astype(BF16).astype(F32), jnp.uint32)
    hi = lax.bitcast_convert_type(x[:, w:].astype(BF16).astype(F32), jnp.uint32)
    return lax.shift_right_logical(lo, jnp.uint32(16)) | (hi & jnp.uint32(0xFFFF0000))


def _unpack_rows(planes):
    w = jnp.concatenate(planes, axis=1)
    lo = lax.bitcast_convert_type(lax.shift_left(w, jnp.uint32(16)), F32)
    hi = lax.bitcast_convert_type(w & jnp.uint32(0xFFFF0000), F32)
    return jnp.concatenate([lo, hi], axis=1).astype(BF16)


def _inproj_kernel(x_hbm, meta_hbm, g1_ref, win_ref, convw_ref, gq_ref, gk_ref, bd_ref, *refs,
                   tiles_per_seq, seq, cw, qw, units):
    rope_refs = refs[0:3 * units]
    hp_hbm, convy_ref, q_ref, k_ref, v_ref, carry_ref, xbuf, sem, sem_out = refs[3 * units:]
    i = pl.program_id(0)
    n_steps = pl.num_programs(0)
    tm = xbuf.shape[1]
    q0 = 3 * cw
    w = convw_ref[...]
    last_rows = seq + N_META - (tiles_per_seq - 1) * tm

    def fetch(tile, slot, start):
        b = tile // tiles_per_seq
        t = tile % tiles_per_seq

        def go(src, dst):
            cp = pltpu.make_async_copy(src, dst, sem.at[slot])
            if start:
                cp.start()
            else:
                cp.wait()

        @pl.when(t == 0)
        def _():
            go(meta_hbm, xbuf.at[slot, pl.ds(0, N_META)])
            go(x_hbm.at[pl.ds(b * seq, tm - N_META)], xbuf.at[slot, pl.ds(N_META, tm - N_META)])

        @pl.when((t > 0) & (t < tiles_per_seq - 1))
        def _():
            go(x_hbm.at[pl.ds(b * seq + t * tm - N_META, tm)], xbuf.at[slot])

        @pl.when(t == tiles_per_seq - 1)
        def _():
            go(x_hbm.at[pl.ds(b * seq + t * tm - N_META, last_rows)], xbuf.at[slot, pl.ds(0, last_rows)])

    def hp_store(tile, slot):
        return pltpu.make_async_copy(xbuf.at[slot], hp_hbm.at[pl.ds(tile * tm, tm)], sem_out.at[slot])

    mine = (i % 2) * units
    other = units - mine

    @pl.when(i == 0)
    def _():
        for u in range(units):
            fetch(u, mine + u, True)

    @pl.when(i > 0)
    def _():
        for u in range(units):
            hp_store((i - 1) * units + u, other + u).wait()

    @pl.when(i + 1 < n_steps)
    def _():
        for u in range(units):
            fetch((i + 1) * units + u, other + u, True)

    for u in range(units):
        tile = i * units + u
        fetch(tile, mine + u, False)

        @pl.when(tile % tiles_per_seq == tiles_per_seq - 1)
        def _():
            xbuf[mine + u, pl.ds(last_rows, tm - last_rows), :] = jnp.zeros((tm - last_rows, xbuf.shape[2]),
                                                                           xbuf.dtype)

        hp_store(tile, mine + u).start()

    prev = carry_ref[...]

    for u in range(units):
        rows = pl.ds(u * tm, tm)
        x = xbuf[mine + u]
        ms = jnp.mean(x * x, axis=-1, keepdims=True)
        xn = (x * lax.rsqrt(ms + EPS) * g1_ref[...]).astype(BF16)

        def proj(lo, hi):
            return jnp.dot(xn, win_ref[:, lo:hi], preferred_element_type=F32)

        rc, rs1, rs2 = (ref[...] for ref in rope_refs[3 * u:3 * u + 3])

        def norm_rope(t, g_ref):
            ss = jnp.dot((t * t).astype(BF16), bd_ref[...], preferred_element_type=F32)
            tn = t * lax.rsqrt(ss * (1.0 / HEAD_DIM) + EPS) * g_ref[...]
            outs = []
            for c in range(qw // LANES):
                ch = tn[:, c * LANES:(c + 1) * LANES]
                outs.append(ch * rc + pltpu.roll(ch, ROPE_DIM // 2, axis=1) * rs1
                            + pltpu.roll(ch, LANES - ROPE_DIM // 2, axis=1) * rs2)
            return jnp.concatenate(outs, axis=1).astype(BF16)

        u_q = proj(q0, q0 + qw)
        u_k = proj(q0 + qw, q0 + 2 * qw)
        q_ref[rows, :] = norm_rope(u_q, gq_ref)
        u_cc = proj(cw, 2 * cw)
        u_cx = proj(2 * cw, 3 * cw)
        k_ref[rows, :] = norm_rope(u_k, gk_ref)
        u_v = proj(q0 + 2 * qw, win_ref.shape[1])

        z = u_cc * u_cx
        prev = jnp.where((i * units + u) % tiles_per_seq == 0, 0.0, prev)
        p1 = prev[7:8]
        p2 = prev[6:7]
        row = lax.broadcasted_iota(jnp.int32, z.shape, 0)
        z1 = jnp.where(row == 0, p1, pltpu.roll(z, 1, axis=0))
        z2 = jnp.where(row == 0, p2, jnp.where(row == 1, p1, pltpu.roll(z, 2, axis=0)))
        prev = z[tm - 8:tm]
        conv = w[0:1] * z2 + w[1:2] * z1 + w[2:3] * z
        v_ref[rows, :] = u_v.astype(BF16)
        u_cb = proj(0, cw)
        convy_ref[rows, :] = (u_cb * conv).astype(BF16)

    carry_ref[...] = prev

    @pl.when(i == n_steps - 1)
    def _():
        for u in range(units):
            hp_store(i * units + u, mine + u).wait()


def _inproj(x, meta, g1, w_in, conv_w, gq, gk, bd, rc, rs1, rs2, *, tiles_per_seq, tm):
    batch, seq, d = x.shape
    units = INPROJ_UNITS
    n = batch * tiles_per_seq * tm
    assert (batch * tiles_per_seq) % units == 0
    cw = conv_w.shape[1]
    qw = gq.shape[1]
    aw = w_in.shape[1] - 3 * cw - 2 * qw
    const = lambda i: (0, 0)
    step = lambda i: (i, 0)
    rope_specs = [pl.BlockSpec((tm, LANES), lambda i, u=u: ((i * units + u) % tiles_per_seq, 0))
                  for u in range(units) for _ in range(3)]
    kern = functools.partial(_inproj_kernel, tiles_per_seq=tiles_per_seq, seq=seq, cw=cw, qw=qw, units=units)
    return pl.pallas_call(
        kern,
        grid=(n // (units * tm),),
        in_specs=[
            pl.BlockSpec(memory_space=pl.ANY),
            pl.BlockSpec(memory_space=pl.ANY),
            pl.BlockSpec((1, d), const),
            pl.BlockSpec(w_in.shape, const),
            pl.BlockSpec(conv_w.shape, const),
            pl.BlockSpec((1, qw), const),
            pl.BlockSpec((1, qw), const),
            pl.BlockSpec(bd.shape, const),
            *rope_specs,
        ],
        out_specs=[
            pl.BlockSpec(memory_space=pl.ANY),
            pl.BlockSpec((units * tm, cw), step),
            pl.BlockSpec((units * tm, qw), step),
            pl.BlockSpec((units * tm, qw), step),
            pl.BlockSpec((units * tm, aw), step),
        ],
        out_shape=[
            jax.ShapeDtypeStruct((n, d), x.dtype),
            jax.ShapeDtypeStruct((n, cw), BF16),
            jax.ShapeDtypeStruct((n, qw), BF16),
            jax.ShapeDtypeStruct((n, qw), BF16),
            jax.ShapeDtypeStruct((n, aw), BF16),
        ],
        scratch_shapes=[pltpu.VMEM((8, cw), F32), pltpu.VMEM((2 * units, tm, d), x.dtype),
                        pltpu.SemaphoreType.DMA((2 * units,)), pltpu.SemaphoreType.DMA((2 * units,))],
        compiler_params=_cparams(1),
        name="inproj",
    )(x.reshape(batch * seq, d), meta, g1, w_in, conv_w, gq, gk, bd, *([rc, rs1, rs2] * units))


def _attn_kernel(q_ref, k_ref, v_ref, lamp_ref, sg_ref, o_ref, qs_ref, m_ref, l_ref, acc_ref, *, lam_init):
    qi = pl.program_id(2)
    tq = q_ref.shape[0]
    n_heads = q_ref.shape[1] // LANES
    n_chains = 2 * n_heads
    lane = lax.broadcasted_iota(jnp.int32, (tq, LANES), 1)
    for h in range(n_heads):
        q = q_ref[:, h * LANES:(h + 1) * LANES]
        zero = jnp.zeros_like(q)
        qs_ref[pl.ds(2 * h * tq, tq), :] = jnp.where(lane < HEAD_DIM, q, zero)
        qs_ref[pl.ds((2 * h + 1) * tq, tq), :] = jnp.where(lane >= HEAD_DIM, q, zero)

    def scores(off, width, which, r0=0, nr=None):
        h = which // 2
        nr = tq if nr is None else nr
        kc = k_ref[pl.ds(off, width), h * LANES:(h + 1) * LANES]
        return lax.dot_general(qs_ref[pl.ds(which * tq + r0, nr), :], kc, (((1,), (1,)), ((), ())),
                               preferred_element_type=F32)

    def update(off, width, which, s, first, r0=0, nr=None, diag=None):
        h = which // 2
        nr = tq if nr is None else nr
        vc = jnp.concatenate([v_ref[pl.ds(off, width), h * LANES:(h + 1) * LANES],
                              jnp.ones((width, LANES), BF16)], axis=1)
        rows = pl.ds(which * tq + r0, nr)
        if diag is not None:
            r = lax.broadcasted_iota(jnp.int32, s.shape, 0)
            c = lax.broadcasted_iota(jnp.int32, s.shape, 1)
            s = jnp.where(c <= r + (r0 + diag), s, NEG_BIG)
        m_prev = jnp.where(first, NEG_BIG, m_ref[rows, :])
        l_prev = jnp.where(first, 0.0, l_ref[rows, :])
        acc_prev = jnp.where(first, 0.0, acc_ref[rows, :])
        m_new = jnp.maximum(m_prev, jnp.max(s, axis=-1, keepdims=True))
        alpha = jnp.exp2(m_prev - m_new)
        p = jnp.exp2((s - jnp.tile(m_new, (1, width // LANES))).astype(BF16))
        pv = jnp.dot(p, vc, preferred_element_type=F32)
        l_ref[rows, :] = alpha * l_prev + pv[:, LANES:]
        acc_ref[rows, :] = alpha * acc_prev + pv[:, :LANES]
        m_ref[rows, :] = m_new

    def run_pieces(pieces, first):
        s_next = scores(pieces[0][1], pieces[0][2], pieces[0][0], pieces[0][3], pieces[0][4])
        for n, (c, off, width, r0, nr, diag) in enumerate(pieces):
            s = s_next
            if n + 1 < len(pieces):
                c2, off2, width2, r02, nr2, _ = pieces[n + 1]
                s_next = scores(off2, width2, c2, r02, nr2)
            update(off, width, c, s, first, r0, nr, diag)

    def chunk(off, width, first):
        run_pieces([(c, off, width, 0, tq, None) for c in range(n_chains)], first)

    def tail_chunk(off, width, row_groups, first):
        pieces = [(c, off, width - tq + r0 + nr, r0, nr, width - tq)
                  for c in range(n_chains) for r0, nr in row_groups]
        run_pieces(pieces, first)

    wide = 2 * tq

    def body(j, carry):
        chunk(pl.multiple_of(j * wide, wide), wide, j == 0)
        return carry

    lax.fori_loop(0, qi // 2, body, 0)
    odd = qi % 2 == 1

    @pl.when(odd)
    def _():
        tail_chunk(pl.multiple_of((qi - 1) * tq, tq), wide, ATTN_TAIL_ROWS_WIDE, qi < 2)

    @pl.when(jnp.logical_not(odd))
    def _():
        tail_chunk(pl.multiple_of(qi * tq, tq), tq, ATTN_TAIL_ROWS, qi < 2)

    lp = lamp_ref[...]
    lam = (jnp.exp(jnp.sum(lp[0:1] * lp[1:2], axis=-1, keepdims=True))
           - jnp.exp(jnp.sum(lp[2:3] * lp[3:4], axis=-1, keepdims=True)) + lam_init)
    for h in range(n_heads):
        rows = pl.ds(2 * h * tq, 2 * tq)
        o_all = acc_ref[rows, :] / l_ref[rows, :]
        o = o_all[0:tq] - lam * o_all[tq:2 * tq]
        ms = jnp.mean(o * o, axis=-1, keepdims=True)
        o_ref[:, h * LANES:(h + 1) * LANES] = (o * lax.rsqrt(ms + EPS) * sg_ref[...]
                                               * (1.0 - lam_init)).astype(BF16)


def _attention(q, k, v, lamp, sg, *, batch, lp_len, tq, lam_init):
    n, qw = q.shape
    nq = lp_len // tq
    hw = ATTN_HEADS_PER_STEP * LANES
    chains = 2 * ATTN_HEADS_PER_STEP
    kern = functools.partial(_attn_kernel, lam_init=lam_init)
    return pl.pallas_call(
        kern,
        grid=(batch, qw // hw, nq),
        in_specs=[
            pl.BlockSpec((tq, hw), lambda b, h, i: (b * nq + i, h)),
            pl.BlockSpec((lp_len, hw), lambda b, h, i: (b, h)),
            pl.BlockSpec((lp_len, hw), lambda b, h, i: (b, h)),
            pl.BlockSpec(lamp.shape, lambda b, h, i: (0, 0)),
            pl.BlockSpec(sg.shape, lambda b, h, i: (0, 0)),
        ],
        out_specs=pl.BlockSpec((tq, hw), lambda b, h, i: (b * nq + i, h)),
        out_shape=jax.ShapeDtypeStruct((n, v.shape[1]), BF16),
        scratch_shapes=[
            pltpu.VMEM((chains * tq, LANES), BF16),
            pltpu.VMEM((chains * tq, LANES), F32),
            pltpu.VMEM((chains * tq, LANES), F32),
            pltpu.VMEM((chains * tq, LANES), F32),
        ],
        compiler_params=_cparams(3),
        name="diffattn",
    )(q, k, v, lamp, sg)


def _outproj_kernel(hp_ref, cy_ref, o_ref, wout_ref, g2_ref, wrh_ref, wrl_ref, br_ref, upper_ref,
                    h1_ref, xp0_ref, xp1_ref, xp2_ref, xp3_ref, route_ref, cnt_ref, run_ref,
                    *, tiles_per_seq, seq_len):
    i = pl.program_id(0)
    tm = upper_ref.shape[0]
    n_units = hp_ref.shape[0] // tm

    @pl.when(i == 0)
    def _():
        run_ref[...] = jnp.zeros_like(run_ref)

    def project(rows):
        mix = jnp.concatenate([cy_ref[rows, :], o_ref[rows, :]], axis=1)
        h1 = hp_ref[rows, :] + jnp.dot(mix, wout_ref[...], preferred_element_type=F32)
        h1_ref[rows, :] = h1
        ms = jnp.mean(h1 * h1, axis=-1, keepdims=True)
        xn = h1 * lax.rsqrt(ms + EPS) * g2_ref[...]
        xw = _pack_rows(xn)
        for c, ref in enumerate((xp0_ref, xp1_ref, xp2_ref, xp3_ref)):
            ref[rows, :] = xw[:, c * LANES:(c + 1) * LANES]
        x_hi = xn.astype(BF16)
        x_lo = (xn - x_hi.astype(F32)).astype(BF16)
        hi_both = jnp.dot(x_hi, jnp.concatenate([wrh_ref[...], wrl_ref[...]], axis=1),
                          preferred_element_type=F32)
        return (hi_both[:, :LANES] + hi_both[:, LANES:]
                + jnp.dot(x_lo, wrh_ref[...], preferred_element_type=F32) + br_ref[...])

    def route(u, logits, run):
        lt = logits.T[0:ROUTE_ROWS, :]
        row = lax.broadcasted_iota(jnp.int32, lt.shape, 0)
        big = jnp.int32(4 * LANES)

        def first_argmax(vals, vmax):
            return jnp.min(jnp.where(vals == vmax, row, big), axis=0, keepdims=True)

        gl = jnp.where(row < N_GROUPS, lt, NEG_BIG)
        gmax = jnp.max(gl, axis=0, keepdims=True)
        g_val = 1.0 / jnp.sum(jnp.exp(gl - gmax), axis=0, keepdims=True)
        g_idx = first_argmax(gl, gmax)
        lo = EXPERT_LANE0 + EXPERTS_PER_GROUP * g_idx
        el = jnp.where((row >= lo) & (row < lo + EXPERTS_PER_GROUP), lt, NEG_BIG)
        m1 = jnp.max(el, axis=0, keepdims=True)
        i1 = first_argmax(el, m1)
        el2 = jnp.where(row == i1, NEG_BIG, el)
        m2 = jnp.max(el2, axis=0, keepdims=True)
        i2 = first_argmax(el2, m2)
        r = jnp.exp(m2 - m1)
        gate1 = g_val / (1.0 + r)
        gate2 = g_val * r / (1.0 + r)

        pos = ((i * n_units + u) % tiles_per_seq) * tm + lax.broadcasted_iota(jnp.int32, (1, tm), 1)
        valid = pos < seq_len
        oh1 = jnp.where(valid & (row == i1), 1.0, 0.0)
        oh2 = jnp.where(valid & (row == i2), 1.0, 0.0)
        pre = jnp.dot(jnp.concatenate([oh1, oh2], axis=0).astype(BF16), upper_ref[...],
                      preferred_element_type=F32)
        tot1 = jnp.sum(oh1, axis=1, keepdims=True)
        tot2 = jnp.sum(oh2, axis=1, keepdims=True)
        run_t = jnp.tile(run, (1, tm // LANES))
        rank1 = jnp.sum(oh1 * (pre[:ROUTE_ROWS] + run_t), axis=0, keepdims=True)
        rank2 = jnp.sum(oh2 * (pre[ROUTE_ROWS:] + run_t + tot1), axis=0, keepdims=True)

        e1 = (i1 - EXPERT_LANE0).astype(F32)
        e2 = (i2 - EXPERT_LANE0).astype(F32)
        r8 = lax.broadcasted_iota(jnp.int32, (ROUTE_COLS, tm), 0)
        route_ref[:, u * tm:(u + 1) * tm] = jnp.where(r8 == 0, e1, jnp.where(r8 == 1, e2, jnp.where(
            r8 == 2, gate1, jnp.where(r8 == 3, gate2, jnp.where(r8 == 4, rank1, jnp.where(r8 == 5, rank2, 0.0))))))
        return run + tot1 + tot2

    all_logits = [project(pl.ds(u * tm, tm)) for u in range(n_units)]
    run = run_ref[...]
    for u, logits in enumerate(all_logits):
        run = route(u, logits, run)
    run_ref[...] = run
    cnt_ref[...] = run


def _outproj(hp, convy, o, w_out, g2, wr_hi, wr_lo, br, upper, *, tiles_per_seq, seq_len):
    n, d = hp.shape
    tm = OUTPROJ_UNITS * upper.shape[0]
    assert n % tm == 0
    const = lambda i: (0, 0)
    tile = lambda i: (i, 0)
    kern = functools.partial(_outproj_kernel, tiles_per_seq=tiles_per_seq, seq_len=seq_len)
    return pl.pallas_call(
        kern,
        grid=(n // tm,),
        in_specs=[
            pl.BlockSpec((tm, d), tile),
            pl.BlockSpec((tm, convy.shape[1]), tile),
            pl.BlockSpec((tm, o.shape[1]), tile),
            pl.BlockSpec(w_out.shape, const),
            pl.BlockSpec((1, d), const),
            pl.BlockSpec(wr_hi.shape, const),
            pl.BlockSpec(wr_lo.shape, const),
            pl.BlockSpec((1, LANES), const),
            pl.BlockSpec(upper.shape, const),
        ],
        out_specs=[
            pl.BlockSpec((tm, d), tile),
            *[pl.BlockSpec((tm, LANES), tile)] * ROW_PLANES,
            pl.BlockSpec((ROUTE_COLS, tm), lambda i: (0, i)),
            pl.BlockSpec((ROUTE_ROWS, LANES), const),
        ],
        out_shape=[
            jax.ShapeDtypeStruct((n, d), F32),
            *[jax.ShapeDtypeStruct((n, LANES), jnp.uint32)] * ROW_PLANES,
            jax.ShapeDtypeStruct((ROUTE_COLS, n), F32),
            jax.ShapeDtypeStruct((ROUTE_ROWS, LANES), F32),
        ],
        scratch_shapes=[pltpu.VMEM((ROUTE_ROWS, LANES), F32)],
        compiler_params=_cparams(1),
        name="outproj_router",
    )(hp, convy, o, w_out, g2, wr_hi, wr_lo, br, upper)


def _slots_kernel(pstart_ref, route_ref, dest_ref, *, seq_len, p_rows):
    b = pl.program_id(0)
    route = route_ref[...]
    eid = route.astype(jnp.int32)
    start = jnp.zeros_like(eid)
    for e in range(N_EXPERTS):
        start = jnp.where(eid == e, pstart_ref[e], start)
    rank = pltpu.roll(route, ROUTE_COLS - 2 * TOP_K, axis=0).astype(jnp.int32)
    k = lax.broadcasted_iota(jnp.int32, route.shape, 0)
    pos = lax.broadcasted_iota(jnp.int32, route.shape, 1)
    n_pad = route.shape[1] - seq_len
    spare = p_rows + (b * n_pad + (pos - seq_len)) * TOP_K + k
    dest_ref[...] = jnp.where(pos < seq_len, start + rank, spare)


def _slots(pstarts, route, *, batch, lp_len, seq_len, p_rows):
    kern = functools.partial(_slots_kernel, seq_len=seq_len, p_rows=p_rows)
    grid_spec = pltpu.PrefetchScalarGridSpec(
        num_scalar_prefetch=1,
        grid=(batch,),
        in_specs=[pl.BlockSpec((ROUTE_COLS, lp_len), lambda b, ps: (0, b))],
        out_specs=pl.BlockSpec((ROUTE_COLS, lp_len), lambda b, ps: (0, b)),
    )
    return pl.pallas_call(
        kern,
        grid_spec=grid_spec,
        out_shape=jax.ShapeDtypeStruct(route.shape, jnp.int32),
        compiler_params=_cparams(1),
        name="moe_slots",
    )(pstarts, route)


def _sc_workers():
    info = plsc.get_sparse_core_info()
    return info.num_cores, info.num_cores * info.num_subcores


def _sc_scatter_rows(planes, idx_a, idx_b, out_rows):
    n_win = idx_a.shape[0]
    n_cores, n_workers = _sc_workers()
    trips = -(-n_win // n_workers)
    mesh = plsc.VectorSubcoreMesh(core_axis_name="c", subcore_axis_name="s")

    def body(*refs):
        xs = refs[0:ROW_PLANES]
        ia_hbm, ib_hbm = refs[ROW_PLANES:ROW_PLANES + 2]
        outs = refs[ROW_PLANES + 2:2 * ROW_PLANES + 2]
        ia_v, ib_v, buf, sem = refs[2 * ROW_PLANES + 2:]
        wid = lax.axis_index("s") * n_cores + lax.axis_index("c")

        def step(t, carry):
            g = wid + t * n_workers

            @pl.when(g < n_win)
            def _():
                row0 = pl.multiple_of(g * SC_WINDOW, SC_WINDOW)
                loads = [pltpu.async_copy(ia_hbm.at[g], ia_v, sem), pltpu.async_copy(ib_hbm.at[g], ib_v, sem)]
                loads += [pltpu.async_copy(xs[c].at[pl.ds(row0, SC_WINDOW)], buf.at[c], sem)
                          for c in range(ROW_PLANES)]
                for cp in loads:
                    cp.wait()
                stores = [pltpu.async_copy(buf.at[c], outs[c].at[iv], sem)
                          for c in range(ROW_PLANES) for iv in (ia_v, ib_v)]
                for cp in stores:
                    cp.wait()

            return carry

        lax.fori_loop(0, trips, step, 0)

    kern = pl.kernel(
        body,
        out_type=[jax.ShapeDtypeStruct((out_rows, LANES), jnp.uint32)] * ROW_PLANES,
        mesh=mesh,
        scratch_types=[
            pltpu.VMEM((SC_WINDOW,), jnp.int32),
            pltpu.VMEM((SC_WINDOW,), jnp.int32),
            pltpu.VMEM((ROW_PLANES, SC_WINDOW, LANES), jnp.uint32),
            pltpu.SemaphoreType.DMA,
        ],
        name="moe_dispatch_sc",
    )
    return kern(*planes, idx_a, idx_b)


def _sc_gather_rows(planes, idx_a, idx_b):
    n_win = idx_a.shape[0]
    n_cores, n_workers = _sc_workers()
    trips = -(-n_win // n_workers)
    mesh = plsc.VectorSubcoreMesh(core_axis_name="c", subcore_axis_name="s")

    def body(*refs):
        ys = refs[0:ROW_PLANES]
        ia_hbm, ib_hbm = refs[ROW_PLANES:ROW_PLANES + 2]
        outs_a = refs[ROW_PLANES + 2:2 * ROW_PLANES + 2]
        outs_b = refs[2 * ROW_PLANES + 2:3 * ROW_PLANES + 2]
        ia_v, ib_v, buf, sem = refs[3 * ROW_PLANES + 2:]
        wid = lax.axis_index("s") * n_cores + lax.axis_index("c")

        def step(t, carry):
            g = wid + t * n_workers

            @pl.when(g < n_win)
            def _():
                row0 = pl.multiple_of(g * SC_WINDOW, SC_WINDOW)
                idx_loads = [pltpu.async_copy(ia_hbm.at[g], ia_v, sem), pltpu.async_copy(ib_hbm.at[g], ib_v, sem)]
                for cp in idx_loads:
                    cp.wait()
                for iv, outs in ((ia_v, outs_a), (ib_v, outs_b)):
                    loads = [pltpu.async_copy(ys[c].at[iv], buf.at[c], sem) for c in range(ROW_PLANES)]
                    for cp in loads:
                        cp.wait()
                    stores = [pltpu.async_copy(buf.at[c], outs[c].at[pl.ds(row0, SC_WINDOW)], sem)
                              for c in range(ROW_PLANES)]
                    for cp in stores:
                        cp.wait()

            return carry

        lax.fori_loop(0, trips, step, 0)

    n_rows = n_win * SC_WINDOW
    kern = pl.kernel(
        body,
        out_type=[jax.ShapeDtypeStruct((n_rows, LANES), jnp.uint32)] * (2 * ROW_PLANES),
        mesh=mesh,
        scratch_types=[
            pltpu.VMEM((SC_WINDOW,), jnp.int32),
            pltpu.VMEM((SC_WINDOW,), jnp.int32),
            pltpu.VMEM((ROW_PLANES, SC_WINDOW, LANES), jnp.uint32),
            pltpu.SemaphoreType.DMA,
        ],
        name="moe_gather_sc",
    )
    res = kern(*planes, idx_a, idx_b)
    return res[:ROW_PLANES], res[ROW_PLANES:]


def _experts_kernel(be_ref, nv_ref, first_ref, slot_ref, nxt_ref, x0_ref, x1_ref, x2_ref, x3_ref,
                    wg_hbm, wu_hbm, wd_hbm, y0_ref, y1_ref, y2_ref, y3_ref,
                    wgf_ref, wuf_ref, wdf_ref, wgb_ref, wub_ref, wdb_ref, sem):
    i = pl.program_id(0)
    e = be_ref[i]
    slot = slot_ref[i]
    y_refs = (y0_ref, y1_ref, y2_ref, y3_ref)

    def weight_copies(expert, s):
        return [pltpu.make_async_copy(hbm.at[expert], stage.at[s], sem.at[s, j])
                for j, (hbm, stage) in enumerate(((wg_hbm, wgf_ref), (wu_hbm, wuf_ref), (wd_hbm, wdf_ref)))]

    @pl.when(i == 0)
    def _():
        for cp in weight_copies(e, slot):
            cp.start()

    @pl.when(first_ref[i] == 1)
    def _():
        for cp in weight_copies(e, slot):
            cp.wait()
        nxt = nxt_ref[i]

        @pl.when(nxt >= 0)
        def _():
            for cp in weight_copies(nxt, 1 - slot):
                cp.start()

        wgb_ref[...] = wgf_ref[slot].astype(BF16)
        wub_ref[...] = wuf_ref[slot].astype(BF16)
        wdb_ref[...] = wdf_ref[slot].astype(BF16)

    nv = nv_ref[i]

    def mlp(rows):
        xs = _unpack_rows([r[rows, :] for r in (x0_ref, x1_ref, x2_ref, x3_ref)])
        row = lax.broadcasted_iota(jnp.int32, xs.shape, 0)
        x = jnp.where(row < nv, xs, jnp.zeros_like(xs))
        hg = jnp.dot(x, wgb_ref[...], preferred_element_type=F32)
        hu = jnp.dot(x, wub_ref[...], preferred_element_type=F32)
        hid = (hg / (1.0 + jnp.exp(-hg)) * hu).astype(BF16)
        yw = _pack_rows(jnp.dot(hid, wdb_ref[...], preferred_element_type=F32))
        for c, ref in enumerate(y_refs):
            ref[rows, :] = yw[:, c * LANES:(c + 1) * LANES]

    def clear(rows):
        for ref in y_refs:
            ref[rows, :] = jnp.zeros((rows.size, LANES), ref.dtype)

    for piece in range(MOE_BLOCK // MOE_TAIL_STEP + 1):
        used = piece * MOE_TAIL_STEP

        @pl.when((nv > used - MOE_TAIL_STEP) & (nv <= used))
        def _():
            if used > 0:
                mlp(pl.ds(0, used))
            if used < MOE_BLOCK:
                clear(pl.ds(used, MOE_BLOCK - used))


def _experts(block_e, nvalid, xs_planes, w_gate, w_up, w_down):
    n_blocks = block_e.shape[0]
    _, d, ff = w_gate.shape
    first = jnp.concatenate([jnp.ones((1,), jnp.int32), (block_e[1:] != block_e[:-1]).astype(jnp.int32)])
    slot = (jnp.cumsum(first) - 1) % 2
    later = jnp.where(block_e[None, :] > block_e[:, None], block_e[None, :], N_EXPERTS)
    nxt = jnp.min(later, axis=1)
    nxt = jnp.where(nxt == N_EXPERTS, -1, nxt).astype(jnp.int32)
    blk = lambda i, *_: (i, 0)
    grid_spec = pltpu.PrefetchScalarGridSpec(
        num_scalar_prefetch=5,
        grid=(n_blocks,),
        in_specs=[
            *[pl.BlockSpec((MOE_BLOCK, LANES), blk)] * ROW_PLANES,
            pl.BlockSpec(memory_space=pl.ANY),
            pl.BlockSpec(memory_space=pl.ANY),
            pl.BlockSpec(memory_space=pl.ANY),
        ],
        out_specs=[pl.BlockSpec((MOE_BLOCK, LANES), blk)] * ROW_PLANES,
        scratch_shapes=[
            pltpu.VMEM((2, d, ff), F32),
            pltpu.VMEM((2, d, ff), F32),
            pltpu.VMEM((2, ff, d), F32),
            pltpu.VMEM((d, ff), BF16),
            pltpu.VMEM((d, ff), BF16),
            pltpu.VMEM((ff, d), BF16),
            pltpu.SemaphoreType.DMA((2, 3)),
        ],
    )
    return pl.pallas_call(
        _experts_kernel,
        grid_spec=grid_spec,
        out_shape=[jax.ShapeDtypeStruct((n_blocks * MOE_BLOCK, LANES), jnp.uint32)] * ROW_PLANES,
        compiler_params=_cparams(1),
        name="moe_experts",
    )(block_e, nvalid, first, slot.astype(jnp.int32), nxt, *xs_planes, w_gate, w_up, w_down)


def _combine_kernel(route_ref, h1_hbm, *refs, lp_len, tile):
    a_refs = refs[0:ROW_PLANES]
    b_refs = refs[ROW_PLANES:2 * ROW_PLANES]
    out_ref, hbuf, sem_h = refs[2 * ROW_PLANES:]
    nt = pl.num_programs(1)
    step = pl.program_id(0) * nt + pl.program_id(1)
    last = pl.num_programs(0) * nt - 1

    def h_copy(s, slot):
        start = (s // nt) * lp_len + N_META + (s % nt) * tile
        return pltpu.make_async_copy(h1_hbm.at[pl.ds(start, tile), :], hbuf.at[slot], sem_h.at[slot])

    slot = step % 2

    @pl.when(step == 0)
    def _():
        h_copy(step, slot).start()

    @pl.when(step < last)
    def _():
        h_copy(step + 1, 1 - slot).start()

    ya = _unpack_rows([r[...] for r in a_refs]).astype(F32)
    yb = _unpack_rows([r[...] for r in b_refs]).astype(F32)
    g = jnp.concatenate([route_ref[...]] * (LANES // ROUTE_COLS), axis=0).T
    moe = g[:, TOP_K:TOP_K + 1] * ya + g[:, TOP_K + 1:TOP_K + 2] * yb
    h_copy(step, slot).wait()
    out_ref[0] = hbuf[slot] + moe


def _combine(gates, h1, a_planes, b_planes, *, batch, seq, lp_len, tile):
    d = h1.shape[1]
    nt = seq // tile
    kern = functools.partial(_combine_kernel, lp_len=lp_len, tile=tile)
    rows = lambda b, i: (b * nt + i, 0)
    return pl.pallas_call(
        kern,
        grid=(batch, nt),
        in_specs=[
            pl.BlockSpec((ROUTE_COLS, tile), lambda b, i: (0, b * nt + i)),
            pl.BlockSpec(memory_space=pl.ANY),
            *[pl.BlockSpec((tile, LANES), rows)] * (2 * ROW_PLANES),
        ],
        out_specs=pl.BlockSpec((1, tile, d), lambda b, i: (b, i, 0)),
        out_shape=jax.ShapeDtypeStruct((batch, seq, d), F32),
        scratch_shapes=[pltpu.VMEM((2, tile, d), F32), pltpu.SemaphoreType.DMA((2,))],
        compiler_params=_cparams(2),
        name="moe_combine",
    )(gates, h1, *a_planes, *b_planes)


def _rope_tables(lp_len):
    half = ROPE_DIM // 2
    pos = jnp.arange(lp_len, dtype=F32)
    inv_freq = ROPE_THETA ** (-jnp.arange(0, ROPE_DIM, 2, dtype=F32) / ROPE_DIM)
    ang = pos[:, None] * inv_freq[None, :]
    lane = jnp.arange(LANES) % HEAD_DIM
    cos = jnp.tile(jnp.cos(ang), (1, LANES // half))
    sin = jnp.tile(jnp.sin(ang), (1, LANES // half))
    c = jnp.where(lane < ROPE_DIM, cos, 1.0)
    s1 = jnp.where((lane >= half) & (lane < ROPE_DIM), sin, 0.0)
    s2 = jnp.where(lane < half, -sin, 0.0)
    return c, s1, s2


def kernel(x, meta_tokens, norm1_g, w_in, conv_w, q_norm_g, k_norm_g, lambda_q1, lambda_k1, lambda_q2, lambda_k2,
           subln_g, w_out, norm2_g, w_router_group, b_router_group, w_router_expert, b_router_expert, w_gate,
           w_up, w_down):
    batch, seq, _ = x.shape
    assert w_in.shape[0] == 1, "a single layer is supported"
    l = 0
    length = seq + N_META
    tm = TOKEN_TILE
    lp_len = -(-length // tm) * tm
    tiles_per_seq = lp_len // tm
    assert tiles_per_seq >= 2 and (length - (tiles_per_seq - 1) * tm) % 8 == 0
    qw = N_HEADS * 2 * HEAD_DIM
    lam_init = 0.8 - 0.6 * math.exp(-0.3 * l)

    reps = qw // HEAD_DIM
    gq = jnp.tile(q_norm_g[l] * (HEAD_DIM ** -0.5 * LOG2E), reps)[None, :]
    gk = jnp.tile(k_norm_g[l], reps)[None, :]
    seg = jnp.arange(qw) // HEAD_DIM
    bd = (seg[:, None] == seg[None, :]).astype(BF16)
    rope = _rope_tables(lp_len)
    hp, convy, q, k, v = _inproj(x, meta_tokens.astype(x.dtype), norm1_g[l][None, :], w_in[l].astype(BF16),
                                 conv_w[l], gq, gk, bd, *rope, tiles_per_seq=tiles_per_seq, tm=tm)

    lamp = jnp.stack([lambda_q1[l], lambda_k1[l], lambda_q2[l], lambda_k2[l]]).astype(F32)
    o = _attention(q, k, v, lamp, subln_g[l][None, :], batch=batch, lp_len=lp_len, tq=tm, lam_init=lam_init)

    lane_pad = LANES - N_GROUPS - N_EXPERTS
    wr = jnp.pad(jnp.concatenate([w_router_group[l], w_router_expert[l]], axis=1), ((0, 0), (0, lane_pad)))
    wr_hi = wr.astype(BF16)
    wr_lo = (wr - wr_hi.astype(F32)).astype(BF16)
    br = jnp.pad(jnp.concatenate([b_router_group[l], b_router_expert[l]]), (0, lane_pad))[None, :]
    ridx = jnp.arange(tm)
    upper = (ridx[:, None] < ridx[None, :]).astype(BF16)
    h1, *rest = _outproj(hp, convy, o, w_out[l].astype(BF16), norm2_g[l][None, :], wr_hi, wr_lo, br, upper,
                         tiles_per_seq=tiles_per_seq, seq_len=length)
    x_planes = rest[:ROW_PLANES]
    route, cnt = rest[ROW_PLANES:]

    counts = cnt[EXPERT_LANE0:EXPERT_LANE0 + N_EXPERTS, 0].astype(jnp.int32)
    n_blocks = -(-(batch * length * TOP_K) // MOE_BLOCK) + N_EXPERTS
    p_rows = n_blocks * MOE_BLOCK
    padded = (counts + MOE_BLOCK - 1) // MOE_BLOCK * MOE_BLOCK
    pends = jnp.cumsum(padded)
    pstarts = pends - padded

    def lookup(table, idx):
        hit = idx[:, None] == jnp.arange(N_EXPERTS, dtype=jnp.int32)[None, :]
        return jnp.sum(jnp.where(hit, table[None, :], 0), axis=1)

    blk0 = jnp.arange(n_blocks, dtype=jnp.int32) * MOE_BLOCK
    block_e = jnp.minimum(jnp.sum((pends[None, :] <= blk0[:, None]).astype(jnp.int32), axis=1), N_EXPERTS - 1)
    nvalid = jnp.clip(lookup(counts, block_e) - (blk0 - lookup(pstarts, block_e)), 0, MOE_BLOCK)

    spare_rows = -(-(batch * (lp_len - length) * TOP_K) // MOE_BLOCK) * MOE_BLOCK
    dest = _slots(pstarts.astype(jnp.int32), route, batch=batch, lp_len=lp_len, seq_len=length, p_rows=p_rows)
    dest = dest.reshape(ROUTE_COLS, batch, lp_len)

    assert (batch * lp_len) % SC_WINDOW == 0 and (batch * seq) % SC_WINDOW == 0
    xs_planes = _sc_scatter_rows(x_planes, dest[0].reshape(-1, SC_WINDOW), dest[1].reshape(-1, SC_WINDOW),
                                 p_rows + spare_rows)
    y_planes = _experts(block_e, nvalid, xs_planes, w_gate[l], w_up[l], w_down[l])

    dest_x = dest[0:TOP_K, :, N_META:length]
    a_planes, b_planes = _sc_gather_rows(y_planes, dest_x[0].reshape(-1, SC_WINDOW),
                                         dest_x[1].reshape(-1, SC_WINDOW))
    route_x = route.reshape(ROUTE_COLS, batch, lp_len)[:, :, N_META:length].reshape(ROUTE_COLS, batch * seq)
    return _combine(route_x, h1, a_planes, b_planes, batch=batch, seq=seq, lp_len=lp_len,
                    tile=_largest_tile(seq, COMBINE_TILE, LANES))
```

```python
import functools
import math

import jax
import jax.numpy as jnp
from jax import lax
from jax.experimental import pallas as pl
from jax.experimental.pallas import tpu as pltpu
from jax.experimental.pallas import tpu_sc as plsc

F32 = jnp.float32
BF16 = jnp.bfloat16

N_META = 16
N_HEADS = 4
HEAD_DIM = 64
ROPE_DIM = HEAD_DIM // 4
ROPE_THETA = 500000.0
N_GROUPS = 4
EXPERTS_PER_GROUP = 8
N_EXPERTS = N_GROUPS * EXPERTS_PER_GROUP
TOP_K = 2
EPS = 1e-6
LOG2E = 1.4426950408889634

LANES = 128
TOKEN_TILE = 640
INPROJ_UNITS = 2
OUTPROJ_UNITS = 2
ATTN_TAIL_ROWS = ((0, 256), (256, 384))
ATTN_TAIL_ROWS_WIDE = ((0, 384), (384, 256))
ATTN_HEADS_PER_STEP = 4
MOE_BLOCK = 1024
MOE_TAIL_STEP = 256
COMBINE_TILE = 1024
ROUTE_COLS = 8
ROUTE_ROWS = 64
ROW_PLANES = 4
SC_WINDOW = 128
EXPERT_LANE0 = N_GROUPS
NEG_BIG = -1e30
VMEM_LIMIT = 56 * 1024 * 1024


def _largest_tile(n, cap, mult):
    for t in range(min(cap, n), 0, -1):
        if n % t == 0 and t % mult == 0:
            return t
    raise ValueError(f"no tile for {n}")


def _cparams(n_axes):
    return pltpu.CompilerParams(dimension_semantics=("arbitrary",) * n_axes, vmem_limit_bytes=VMEM_LIMIT)


def _pack_rows(x):
    w = x.shape[1] // 2
    lo = lax.bitcast_convert_type(x[:, :w].astype(BF16).astype(F32), jnp.uint32)
    hi = lax.bitcast_convert_type(x[:, w:].astype(BF16).astype(F32), jnp.uint32)
    return lax.shift_right_logical(lo, jnp.uint32(16)) | (hi & jnp.uint32(0xFFFF0000))


def _unpack_rows(planes):
    w = jnp.concatenate(planes, axis=1)
    lo = lax.bitcast_convert_type(lax.shift_left(w, jnp.uint32(16)), F32)
    hi = lax.bitcast_convert_type(w & jnp.uint32(0xFFFF0000), F32)
    return jnp.concatenate([lo, hi], axis=1).astype(BF16)


def _inproj_kernel(x_hbm, meta_hbm, g1_ref, win_ref, convw_ref, gq_ref, gk_ref, bd_ref, *refs,
                   tiles_per_seq, seq, cw, qw, units):
    rope_refs = refs[0:3 * units]
    hp_hbm, convy_ref, q_ref, k_ref, v_ref, carry_ref, xbuf, sem, sem_out = refs[3 * units:]
    i = pl.program_id(0)
    n_steps = pl.num_programs(0)
    tm = xbuf.shape[1]
    q0 = 3 * cw
    w = convw_ref[...]
    last_rows = seq + N_META - (tiles_per_seq - 1) * tm

    def fetch(tile, slot, start):
        b = tile // tiles_per_seq
        t = tile % tiles_per_seq

        def go(src, dst):
            cp = pltpu.make_async_copy(src, dst, sem.at[slot])
            if start:
                cp.start()
            else:
                cp.wait()

        @pl.when(t == 0)
        def _():
            go(meta_hbm, xbuf.at[slot, pl.ds(0, N_META)])
            go(x_hbm.at[pl.ds(b * seq, tm - N_META)], xbuf.at[slot, pl.ds(N_META, tm - N_META)])

        @pl.when((t > 0) & (t < tiles_per_seq - 1))
        def _():
            go(x_hbm.at[pl.ds(b * seq + t * tm - N_META, tm)], xbuf.at[slot])

        @pl.when(t == tiles_per_seq - 1)
        def _():
            go(x_hbm.at[pl.ds(b * seq + t * tm - N_META, last_rows)], xbuf.at[slot, pl.ds(0, last_rows)])

    def hp_store(tile, slot):
        return pltpu.make_async_copy(xbuf.at[slot], hp_hbm.at[pl.ds(tile * tm, tm)], sem_out.at[slot])

    mine = (i % 2) * units
    other = units - mine

    @pl.when(i == 0)
    def _():
        for u in range(units):
            fetch(u, mine + u, True)

    @pl.when(i > 0)
    def _():
        for u in range(units):
            hp_store((i - 1) * units + u, other + u).wait()

    @pl.when(i + 1 < n_steps)
    def _():
        for u in range(units):
            fetch((i + 1) * units + u, other + u, True)

    for u in range(units):
        tile = i * units + u
        fetch(tile, mine + u, False)

        @pl.when(tile % tiles_per_seq == tiles_per_seq - 1)
        def _():
            xbuf[mine + u, pl.ds(last_rows, tm - last_rows), :] = jnp.zeros((tm - last_rows, xbuf.shape[2]),
                                                                           xbuf.dtype)

        hp_store(tile, mine + u).start()

    prev = carry_ref[...]

    for u in range(units):
        rows = pl.ds(u * tm, tm)
        x = xbuf[mine + u]
        ms = jnp.mean(x * x, axis=-1, keepdims=True)
        xn = (x * lax.rsqrt(ms + EPS) * g1_ref[...]).astype(BF16)

        def proj(lo, hi):
            return jnp.dot(xn, win_ref[:, lo:hi], preferred_element_type=F32)

        rc, rs1, rs2 = (ref[...] for ref in rope_refs[3 * u:3 * u + 3])

        def norm_rope(t, g_ref):
            ss = jnp.dot((t * t).astype(BF16), bd_ref[...], preferred_element_type=F32)
            tn = t * lax.rsqrt(ss * (1.0 / HEAD_DIM) + EPS) * g_ref[...]
            outs = []
            for c in range(qw // LANES):
                ch = tn[:, c * LANES:(c + 1) * LANES]
                outs.append(ch * rc + pltpu.roll(ch, ROPE_DIM // 2, axis=1) * rs1
                            + pltpu.roll(ch, LANES - ROPE_DIM // 2, axis=1) * rs2)
            return jnp.concatenate(outs, axis=1).astype(BF16)

        u_q = proj(q0, q0 + qw)
        u_k = proj(q0 + qw, q0 + 2 * qw)
        q_ref[rows, :] = norm_rope(u_q, gq_ref)
        u_cc = proj(cw, 2 * cw)
        u_cx = proj(2 * cw, 3 * cw)
        k_ref[rows, :] = norm_rope(u_k, gk_ref)
        u_v = proj(q0 + 2 * qw, win_ref.shape[1])

        z = u_cc * u_cx
        prev = jnp.where((i * units + u) % tiles_per_seq == 0, 0.0, prev)
        p1 = prev[7:8]
        p2 = prev[6:7]
        row = lax.broadcasted_iota(jnp.int32, z.shape, 0)
        z1 = jnp.where(row == 0, p1, pltpu.roll(z, 1, axis=0))
        z2 = jnp.where(row == 0, p2, jnp.where(row == 1, p1, pltpu.roll(z, 2, axis=0)))
        prev = z[tm - 8:tm]
        conv = w[0:1] * z2 + w[1:2] * z1 + w[2:3] * z
        v_ref[rows, :] = u_v.astype(BF16)
        u_cb = proj(0, cw)
        convy_ref[rows, :] = (u_cb * conv).astype(BF16)

    carry_ref[...] = prev

    @pl.when(i == n_steps - 1)
    def _():
        for u in range(units):
            hp_store(i * units + u, mine + u).wait()


def _inproj(x, meta, g1, w_in, conv_w, gq, gk, bd, rc, rs1, rs2, *, tiles_per_seq, tm):
    batch, seq, d = x.shape
    units = INPROJ_UNITS
    n = batch * tiles_per_seq * tm
    assert (batch * tiles_per_seq) % units == 0
    cw = conv_w.shape[1]
    qw = gq.shape[1]
    aw = w_in.shape[1] - 3 * cw - 2 * qw
    const = lambda i: (0, 0)
    step = lambda i: (i, 0)
    rope_specs = [pl.BlockSpec((tm, LANES), lambda i, u=u: ((i * units + u) % tiles_per_seq, 0))
                  for u in range(units) for _ in range(3)]
    kern = functools.partial(_inproj_kernel, tiles_per_seq=tiles_per_seq, seq=seq, cw=cw, qw=qw, units=units)
    return pl.pallas_call(
        kern,
        grid=(n // (units * tm),),
        in_specs=[
            pl.BlockSpec(memory_space=pl.ANY),
            pl.BlockSpec(memory_space=pl.ANY),
            pl.BlockSpec((1, d), const),
            pl.BlockSpec(w_in.shape, const),
            pl.BlockSpec(conv_w.shape, const),
            pl.BlockSpec((1, qw), const),
            pl.BlockSpec((1, qw), const),
            pl.BlockSpec(bd.shape, const),
            *rope_specs,
        ],
        out_specs=[
            pl.BlockSpec(memory_space=pl.ANY),
            pl.BlockSpec((units * tm, cw), step),
            pl.BlockSpec((units * tm, qw), step),
            pl.BlockSpec((units * tm, qw), step),
            pl.BlockSpec((units * tm, aw), step),
        ],
        out_shape=[
            jax.ShapeDtypeStruct((n, d), x.dtype),
            jax.ShapeDtypeStruct((n, cw), BF16),
            jax.ShapeDtypeStruct((n, qw), BF16),
            jax.ShapeDtypeStruct((n, qw), BF16),
            jax.ShapeDtypeStruct((n, aw), BF16),
        ],
        scratch_shapes=[pltpu.VMEM((8, cw), F32), pltpu.VMEM((2 * units, tm, d), x.dtype),
                        pltpu.SemaphoreType.DMA((2 * units,)), pltpu.SemaphoreType.DMA((2 * units,))],
        compiler_params=_cparams(1),
        name="inproj",
    )(x.reshape(batch * seq, d), meta, g1, w_in, conv_w, gq, gk, bd, *([rc, rs1, rs2] * units))


def _attn_kernel(q_ref, k_ref, v_ref, lamp_ref, sg_ref, o_ref, qs_ref, m_ref, l_ref, acc_ref, *, lam_init):
    qi = pl.program_id(2)
    tq = q_ref.shape[0]
    n_heads = q_ref.shape[1] // LANES
    n_chains = 2 * n_heads
    lane = lax.broadcasted_iota(jnp.int32, (tq, LANES), 1)
    for h in range(n_heads):
        q = q_ref[:, h * LANES:(h + 1) * LANES]
        zero = jnp.zeros_like(q)
        qs_ref[pl.ds(2 * h * tq, tq), :] = jnp.where(lane < HEAD_DIM, q, zero)
        qs_ref[pl.ds((2 * h + 1) * tq, tq), :] = jnp.where(lane >= HEAD_DIM, q, zero)

    def scores(off, width, which, r0=0, nr=None):
        h = which // 2
        nr = tq if nr is None else nr
        kc = k_ref[pl.ds(off, width), h * LANES:(h + 1) * LANES]
        return lax.dot_general(qs_ref[pl.ds(which * tq + r0, nr), :], kc, (((1,), (1,)), ((), ())),
                               preferred_element_type=F32)

    def update(off, width, which, s, first, r0=0, nr=None, diag=None):
        h = which // 2
        nr = tq if nr is None else nr
        vc = jnp.concatenate([v_ref[pl.ds(off, width), h * LANES:(h + 1) * LANES],
                              jnp.ones((width, LANES), BF16)], axis=1)
        rows = pl.ds(which * tq + r0, nr)
        if diag is not None:
            r = lax.broadcasted_iota(jnp.int32, s.shape, 0)
            c = lax.broadcasted_iota(jnp.int32, s.shape, 1)
            s = jnp.where(c <= r + (r0 + diag), s, NEG_BIG)
        m_prev = jnp.where(first, NEG_BIG, m_ref[rows, :])
        l_prev = jnp.where(first, 0.0, l_ref[rows, :])
        acc_prev = jnp.where(first, 0.0, acc_ref[rows, :])
        m_new = jnp.maximum(m_prev, jnp.max(s, axis=-1, keepdims=True))
        alpha = jnp.exp2(m_prev - m_new)
        p = jnp.exp2((s - jnp.tile(m_new, (1, width // LANES))).astype(BF16))
        pv = jnp.dot(p, vc, preferred_element_type=F32)
        l_ref[rows, :] = alpha * l_prev + pv[:, LANES:]
        acc_ref[rows, :] = alpha * acc_prev + pv[:, :LANES]
        m_ref[rows, :] = m_new

    def run_pieces(pieces, first):
        s_next = scores(pieces[0][1], pieces[0][2], pieces[0][0], pieces[0][3], pieces[0][4])
        for n, (c, off, width, r0, nr, diag) in enumerate(pieces):
            s = s_next
            if n + 1 < len(pieces):
                c2, off2, width2, r02, nr2, _ = pieces[n + 1]
                s_next = scores(off2, width2, c2, r02, nr2)
            update(off, width, c, s, first, r0, nr, diag)

    def chunk(off, width, first):
        run_pieces([(c, off, width, 0, tq, None) for c in range(n_chains)], first)

    def tail_chunk(off, width, row_groups, first):
        pieces = [(c, off, width - tq + r0 + nr, r0, nr, width - tq)
                  for c in range(n_chains) for r0, nr in row_groups]
        run_pieces(pieces, first)

    wide = 2 * tq

    def body(j, carry):
        chunk(pl.multiple_of(j * wide, wide), wide, j == 0)
        return carry

    lax.fori_loop(0, qi // 2, body, 0)
    odd = qi % 2 == 1

    @pl.when(odd)
    def _():
        tail_chunk(pl.multiple_of((qi - 1) * tq, tq), wide, ATTN_TAIL_ROWS_WIDE, qi < 2)

    @pl.when(jnp.logical_not(odd))
    def _():
        tail_chunk(pl.multiple_of(qi * tq, tq), tq, ATTN_TAIL_ROWS, qi < 2)

    lp = lamp_ref[...]
    lam = (jnp.exp(jnp.sum(lp[0:1] * lp[1:2], axis=-1, keepdims=True))
           - jnp.exp(jnp.sum(lp[2:3] * lp[3:4], axis=-1, keepdims=True)) + lam_init)
    for h in range(n_heads):
        rows = pl.ds(2 * h * tq, 2 * tq)
        o_all = acc_ref[rows, :] / l_ref[rows, :]
        o = o_all[0:tq] - lam * o_all[tq:2 * tq]
        ms = jnp.mean(o * o, axis=-1, keepdims=True)
        o_ref[:, h * LANES:(h + 1) * LANES] = (o * lax.rsqrt(ms + EPS) * sg_ref[...]
                                               * (1.0 - lam_init)).astype(BF16)


def _attention(q, k, v, lamp, sg, *, batch, lp_len, tq, lam_init):
    n, qw = q.shape
    nq = lp_len // tq
    hw = ATTN_HEADS_PER_STEP * LANES
    chains = 2 * ATTN_HEADS_PER_STEP
    kern = functools.partial(_attn_kernel, lam_init=lam_init)
    return pl.pallas_call(
        kern,
        grid=(batch, qw // hw, nq),
        in_specs=[
            pl.BlockSpec((tq, hw), lambda b, h, i: (b * nq + i, h)),
            pl.BlockSpec((lp_len, hw), lambda b, h, i: (b, h)),
            pl.BlockSpec((lp_len, hw), lambda b, h, i: (b, h)),
            pl.BlockSpec(lamp.shape, lambda b, h, i: (0, 0)),
            pl.BlockSpec(sg.shape, lambda b, h, i: (0, 0)),
        ],
        out_specs=pl.BlockSpec((tq, hw), lambda b, h, i: (b * nq + i, h)),
        out_shape=jax.ShapeDtypeStruct((n, v.shape[1]), BF16),
        scratch_shapes=[
            pltpu.VMEM((chains * tq, LANES), BF16),
            pltpu.VMEM((chains * tq, LANES), F32),
            pltpu.VMEM((chains * tq, LANES), F32),
            pltpu.VMEM((chains * tq, LANES), F32),
        ],
        compiler_params=_cparams(3),
        name="diffattn",
    )(q, k, v, lamp, sg)


def _outproj_kernel(hp_ref, cy_ref, o_ref, wout_ref, g2_ref, wrh_ref, wrl_ref, br_ref, upper_ref,
                    h1_ref, xp0_ref, xp1_ref, xp2_ref, xp3_ref, route_ref, cnt_ref, run_ref,
                    *, tiles_per_seq, seq_len):
    i = pl.program_id(0)
    tm = upper_ref.shape[0]
    n_units = hp_ref.shape[0] // tm

    @pl.when(i == 0)
    def _():
        run_ref[...] = jnp.zeros_like(run_ref)

    def project(rows):
        mix = jnp.concatenate([cy_ref[rows, :], o_ref[rows, :]], axis=1)
        h1 = hp_ref[rows, :] + jnp.dot(mix, wout_ref[...], preferred_element_type=F32)
        h1_ref[rows, :] = h1
        ms = jnp.mean(h1 * h1, axis=-1, keepdims=True)
        xn = h1 * lax.rsqrt(ms + EPS) * g2_ref[...]
        xw = _pack_rows(xn)
        for c, ref in enumerate((xp0_ref, xp1_ref, xp2_ref, xp3_ref)):
            ref[rows, :] = xw[:, c * LANES:(c + 1) * LANES]
        return xn

    def router_logits(xn):
        x_hi = xn.astype(BF16)
        x_lo = (xn - x_hi.astype(F32)).astype(BF16)
        hi_both = jnp.dot(x_hi, jnp.concatenate([wrh_ref[...], wrl_ref[...]], axis=1),
                          preferred_element_type=F32)
        return (hi_both[:, :LANES] + hi_both[:, LANES:]
                + jnp.dot(x_lo, wrh_ref[...], preferred_element_type=F32) + br_ref[...])

    def route(u, logits, run):
        lt = logits.T[0:ROUTE_ROWS, :]
        row = lax.broadcasted_iota(jnp.int32, lt.shape, 0)
        big = jnp.int32(4 * LANES)

        def first_argmax(vals, vmax):
            return jnp.min(jnp.where(vals == vmax, row, big), axis=0, keepdims=True)

        gl = jnp.where(row < N_GROUPS, lt, NEG_BIG)
        gmax = jnp.max(gl, axis=0, keepdims=True)
        g_val = 1.0 / jnp.sum(jnp.exp(gl - gmax), axis=0, keepdims=True)
        g_idx = first_argmax(gl, gmax)
        lo = EXPERT_LANE0 + EXPERTS_PER_GROUP * g_idx
        el = jnp.where((row >= lo) & (row < lo + EXPERTS_PER_GROUP), lt, NEG_BIG)
        m1 = jnp.max(el, axis=0, keepdims=True)
        i1 = first_argmax(el, m1)
        el2 = jnp.where(row == i1, NEG_BIG, el)
        m2 = jnp.max(el2, axis=0, keepdims=True)
        i2 = first_argmax(el2, m2)
        r = jnp.exp(m2 - m1)
        gate1 = g_val / (1.0 + r)
        gate2 = g_val * r / (1.0 + r)

        pos = ((i * n_units + u) % tiles_per_seq) * tm + lax.broadcasted_iota(jnp.int32, (1, tm), 1)
        valid = pos < seq_len
        oh1 = jnp.where(valid & (row == i1), 1.0, 0.0)
        oh2 = jnp.where(valid & (row == i2), 1.0, 0.0)
        pre = jnp.dot(jnp.concatenate([oh1, oh2], axis=0).astype(BF16), upper_ref[...],
                      preferred_element_type=F32)
        tot1 = jnp.sum(oh1, axis=1, keepdims=True)
        tot2 = jnp.sum(oh2, axis=1, keepdims=True)
        run_t = jnp.tile(run, (1, tm // LANES))
        rank1 = jnp.sum(oh1 * (pre[:ROUTE_ROWS] + run_t), axis=0, keepdims=True)
        rank2 = jnp.sum(oh2 * (pre[ROUTE_ROWS:] + run_t + tot1), axis=0, keepdims=True)

        e1 = (i1 - EXPERT_LANE0).astype(F32)
        e2 = (i2 - EXPERT_LANE0).astype(F32)
        r8 = lax.broadcasted_iota(jnp.int32, (ROUTE_COLS, tm), 0)
        route_ref[:, u * tm:(u + 1) * tm] = jnp.where(r8 == 0, e1, jnp.where(r8 == 1, e2, jnp.where(
            r8 == 2, gate1, jnp.where(r8 == 3, gate2, jnp.where(r8 == 4, rank1, jnp.where(r8 == 5, rank2, 0.0))))))
        return run + tot1 + tot2

    normed = [project(pl.ds(u * tm, tm)) for u in range(n_units)]
    all_logits = [router_logits(xn) for xn in normed]
    run = run_ref[...]
    for u, logits in enumerate(all_logits):
        run = route(u, logits, run)
    run_ref[...] = run
    cnt_ref[...] = run


def _outproj(hp, convy, o, w_out, g2, wr_hi, wr_lo, br, upper, *, tiles_per_seq, seq_len):
    n, d = hp.shape
    tm = OUTPROJ_UNITS * upper.shape[0]
    assert n % tm == 0
    const = lambda i: (0, 0)
    tile = lambda i: (i, 0)
    kern = functools.partial(_outproj_kernel, tiles_per_seq=tiles_per_seq, seq_len=seq_len)
    return pl.pallas_call(
        kern,
        grid=(n // tm,),
        in_specs=[
            pl.BlockSpec((tm, d), tile),
            pl.BlockSpec((tm, convy.shape[1]), tile),
            pl.BlockSpec((tm, o.shape[1]), tile),
            pl.BlockSpec(w_out.shape, const),
            pl.BlockSpec((1, d), const),
            pl.BlockSpec(wr_hi.shape, const),
            pl.BlockSpec(wr_lo.shape, const),
            pl.BlockSpec((1, LANES), const),
            pl.BlockSpec(upper.shape, const),
        ],
        out_specs=[
            pl.BlockSpec((tm, d), tile),
            *[pl.BlockSpec((tm, LANES), tile)] * ROW_PLANES,
            pl.BlockSpec((ROUTE_COLS, tm), lambda i: (0, i)),
            pl.BlockSpec((ROUTE_ROWS, LANES), const),
        ],
        out_shape=[
            jax.ShapeDtypeStruct((n, d), F32),
            *[jax.ShapeDtypeStruct((n, LANES), jnp.uint32)] * ROW_PLANES,
            jax.ShapeDtypeStruct((ROUTE_COLS, n), F32),
            jax.ShapeDtypeStruct((ROUTE_ROWS, LANES), F32),
        ],
        scratch_shapes=[pltpu.VMEM((ROUTE_ROWS, LANES), F32)],
        compiler_params=_cparams(1),
        name="outproj_router",
    )(hp, convy, o, w_out, g2, wr_hi, wr_lo, br, upper)


def _slots_kernel(pstart_ref, route_ref, dest_ref, *, seq_len, p_rows):
    b = pl.program_id(0)
    route = route_ref[...]
    eid = route.astype(jnp.int32)
    start = jnp.zeros_like(eid)
    for e in range(N_EXPERTS):
        start = jnp.where(eid == e, pstart_ref[e], start)
    rank = pltpu.roll(route, ROUTE_COLS - 2 * TOP_K, axis=0).astype(jnp.int32)
    k = lax.broadcasted_iota(jnp.int32, route.shape, 0)
    pos = lax.broadcasted_iota(jnp.int32, route.shape, 1)
    n_pad = route.shape[1] - seq_len
    spare = p_rows + (b * n_pad + (pos - seq_len)) * TOP_K + k
    dest_ref[...] = jnp.where(pos < seq_len, start + rank, spare)


def _slots(pstarts, route, *, batch, lp_len, seq_len, p_rows):
    kern = functools.partial(_slots_kernel, seq_len=seq_len, p_rows=p_rows)
    grid_spec = pltpu.PrefetchScalarGridSpec(
        num_scalar_prefetch=1,
        grid=(batch,),
        in_specs=[pl.BlockSpec((ROUTE_COLS, lp_len), lambda b, ps: (0, b))],
        out_specs=pl.BlockSpec((ROUTE_COLS, lp_len), lambda b, ps: (0, b)),
    )
    return pl.pallas_call(
        kern,
        grid_spec=grid_spec,
        out_shape=jax.ShapeDtypeStruct(route.shape, jnp.int32),
        compiler_params=_cparams(1),
        name="moe_slots",
    )(pstarts, route)


def _sc_workers():
    info = plsc.get_sparse_core_info()
    return info.num_cores, info.num_cores * info.num_subcores


def _sc_scatter_rows(planes, idx_a, idx_b, out_rows):
    n_win = idx_a.shape[0]
    n_cores, n_workers = _sc_workers()
    trips = -(-n_win // n_workers)
    mesh = plsc.VectorSubcoreMesh(core_axis_name="c", subcore_axis_name="s")

    def body(*refs):
        xs = refs[0:ROW_PLANES]
        ia_hbm, ib_hbm = refs[ROW_PLANES:ROW_PLANES + 2]
        outs = refs[ROW_PLANES + 2:2 * ROW_PLANES + 2]
        ia_v, ib_v, buf, sem = refs[2 * ROW_PLANES + 2:]
        wid = lax.axis_index("s") * n_cores + lax.axis_index("c")

        def step(t, carry):
            g = wid + t * n_workers

            @pl.when(g < n_win)
            def _():
                row0 = pl.multiple_of(g * SC_WINDOW, SC_WINDOW)
                loads = [pltpu.async_copy(ia_hbm.at[g], ia_v, sem), pltpu.async_copy(ib_hbm.at[g], ib_v, sem)]
                loads += [pltpu.async_copy(xs[c].at[pl.ds(row0, SC_WINDOW)], buf.at[c], sem)
                          for c in range(ROW_PLANES)]
                for cp in loads:
                    cp.wait()
                stores = [pltpu.async_copy(buf.at[c], outs[c].at[iv], sem)
                          for c in range(ROW_PLANES) for iv in (ia_v, ib_v)]
                for cp in stores:
                    cp.wait()

            return carry

        lax.fori_loop(0, trips, step, 0)

    kern = pl.kernel(
        body,
        out_type=[jax.ShapeDtypeStruct((out_rows, LANES), jnp.uint32)] * ROW_PLANES,
        mesh=mesh,
        scratch_types=[
            pltpu.VMEM((SC_WINDOW,), jnp.int32),
            pltpu.VMEM((SC_WINDOW,), jnp.int32),
            pltpu.VMEM((ROW_PLANES, SC_WINDOW, LANES), jnp.uint32),
            pltpu.SemaphoreType.DMA,
        ],
        name="moe_dispatch_sc",
    )
    return kern(*planes, idx_a, idx_b)


def _sc_gather_rows(planes, idx_a, idx_b):
    n_win = idx_a.shape[0]
    n_cores, n_workers = _sc_workers()
    trips = -(-n_win // n_workers)
    mesh = plsc.VectorSubcoreMesh(core_axis_name="c", subcore_axis_name="s")

    def body(*refs):
        ys = refs[0:ROW_PLANES]
        ia_hbm, ib_hbm = refs[ROW_PLANES:ROW_PLANES + 2]
        outs_a = refs[ROW_PLANES + 2:2 * ROW_PLANES + 2]
        outs_b = refs[2 * ROW_PLANES + 2:3 * ROW_PLANES + 2]
        ia_v, ib_v, buf, sem = refs[3 * ROW_PLANES + 2:]
        wid = lax.axis_index("s") * n_cores + lax.axis_index("c")

        def step(t, carry):
            g = wid + t * n_workers

            @pl.when(g < n_win)
            def _():
                row0 = pl.multiple_of(g * SC_WINDOW, SC_WINDOW)
                idx_loads = [pltpu.async_copy(ia_hbm.at[g], ia_v, sem), pltpu.async_copy(ib_hbm.at[g], ib_v, sem)]
                for cp in idx_loads:
                    cp.wait()
                for iv, outs in ((ia_v, outs_a), (ib_v, outs_b)):
                    loads = [pltpu.async_copy(ys[c].at[iv], buf.at[c], sem) for c in range(ROW_PLANES)]
                    for cp in loads:
                        cp.wait()
                    stores = [pltpu.async_copy(buf.at[c], outs[c].at[pl.ds(row0, SC_WINDOW)], sem)
                              for c in range(ROW_PLANES)]
                    for cp in stores:
                        cp.wait()

            return carry

        lax.fori_loop(0, trips, step, 0)

    n_rows = n_win * SC_WINDOW
    kern = pl.kernel(
        body,
        out_type=[jax.ShapeDtypeStruct((n_rows, LANES), jnp.uint32)] * (2 * ROW_PLANES),
        mesh=mesh,
        scratch_types=[
            pltpu.VMEM((SC_WINDOW,), jnp.int32),
            pltpu.VMEM((SC_WINDOW,), jnp.int32),
            pltpu.VMEM((ROW_PLANES, SC_WINDOW, LANES), jnp.uint32),
            pltpu.SemaphoreType.DMA,
        ],
        name="moe_gather_sc",
    )
    res = kern(*planes, idx_a, idx_b)
    return res[:ROW_PLANES], res[ROW_PLANES:]


def _experts_kernel(be_ref, nv_ref, first_ref, slot_ref, nxt_ref, x0_ref, x1_ref, x2_ref, x3_ref,
                    wg_hbm, wu_hbm, wd_hbm, y0_ref, y1_ref, y2_ref, y3_ref,
                    wgf_ref, wuf_ref, wdf_ref, wgb_ref, wub_ref, wdb_ref, sem):
    i = pl.program_id(0)
    e = be_ref[i]
    slot = slot_ref[i]
    y_refs = (y0_ref, y1_ref, y2_ref, y3_ref)

    def weight_copies(expert, s):
        return [pltpu.make_async_copy(hbm.at[expert], stage.at[s], sem.at[s, j])
                for j, (hbm, stage) in enumerate(((wg_hbm, wgf_ref), (wu_hbm, wuf_ref), (wd_hbm, wdf_ref)))]

    @pl.when(i == 0)
    def _():
        for cp in weight_copies(e, slot):
            cp.start()

    @pl.when(first_ref[i] == 1)
    def _():
        for cp in weight_copies(e, slot):
            cp.wait()
        nxt = nxt_ref[i]

        @pl.when(nxt >= 0)
        def _():
            for cp in weight_copies(nxt, 1 - slot):
                cp.start()

        wgb_ref[...] = wgf_ref[slot].astype(BF16)
        wub_ref[...] = wuf_ref[slot].astype(BF16)
        wdb_ref[...] = wdf_ref[slot].astype(BF16)

    nv = nv_ref[i]

    def mlp(rows):
        xs = _unpack_rows([r[rows, :] for r in (x0_ref, x1_ref, x2_ref, x3_ref)])
        row = lax.broadcasted_iota(jnp.int32, xs.shape, 0)
        x = jnp.where(row < nv, xs, jnp.zeros_like(xs))
        hg = jnp.dot(x, wgb_ref[...], preferred_element_type=F32)
        hu = jnp.dot(x, wub_ref[...], preferred_element_type=F32)
        hid = (hg / (1.0 + jnp.exp(-hg)) * hu).astype(BF16)
        yw = _pack_rows(jnp.dot(hid, wdb_ref[...], preferred_element_type=F32))
        for c, ref in enumerate(y_refs):
            ref[rows, :] = yw[:, c * LANES:(c + 1) * LANES]

    def clear(rows):
        for ref in y_refs:
            ref[rows, :] = jnp.zeros((rows.size, LANES), ref.dtype)

    for piece in range(MOE_BLOCK // MOE_TAIL_STEP + 1):
        used = piece * MOE_TAIL_STEP

        @pl.when((nv > used - MOE_TAIL_STEP) & (nv <= used))
        def _():
            if used > 0:
                mlp(pl.ds(0, used))
            if used < MOE_BLOCK:
                clear(pl.ds(used, MOE_BLOCK - used))


def _experts(block_e, nvalid, xs_planes, w_gate, w_up, w_down):
    n_blocks = block_e.shape[0]
    _, d, ff = w_gate.shape
    first = jnp.concatenate([jnp.ones((1,), jnp.int32), (block_e[1:] != block_e[:-1]).astype(jnp.int32)])
    slot = (jnp.cumsum(first) - 1) % 2
    later = jnp.where(block_e[None, :] > block_e[:, None], block_e[None, :], N_EXPERTS)
    nxt = jnp.min(later, axis=1)
    nxt = jnp.where(nxt == N_EXPERTS, -1, nxt).astype(jnp.int32)
    blk = lambda i, *_: (i, 0)
    grid_spec = pltpu.PrefetchScalarGridSpec(
        num_scalar_prefetch=5,
        grid=(n_blocks,),
        in_specs=[
            *[pl.BlockSpec((MOE_BLOCK, LANES), blk)] * ROW_PLANES,
            pl.BlockSpec(memory_space=pl.ANY),
            pl.BlockSpec(memory_space=pl.ANY),
            pl.BlockSpec(memory_space=pl.ANY),
        ],
        out_specs=[pl.BlockSpec((MOE_BLOCK, LANES), blk)] * ROW_PLANES,
        scratch_shapes=[
            pltpu.VMEM((2, d, ff), F32),
            pltpu.VMEM((2, d, ff), F32),
            pltpu.VMEM((2, ff, d), F32),
            pltpu.VMEM((d, ff), BF16),
            pltpu.VMEM((d, ff), BF16),
            pltpu.VMEM((ff, d), BF16),
            pltpu.SemaphoreType.DMA((2, 3)),
        ],
    )
    return pl.pallas_call(
        _experts_kernel,
        grid_spec=grid_spec,
        out_shape=[jax.ShapeDtypeStruct((n_blocks * MOE_BLOCK, LANES), jnp.uint32)] * ROW_PLANES,
        compiler_params=_cparams(1),
        name="moe_experts",
    )(block_e, nvalid, first, slot.astype(jnp.int32), nxt, *xs_planes, w_gate, w_up, w_down)


def _combine_kernel(route_ref, h1_hbm, *refs, lp_len, tile):
    a_refs = refs[0:ROW_PLANES]
    b_refs = refs[ROW_PLANES:2 * ROW_PLANES]
    out_ref, hbuf, sem_h = refs[2 * ROW_PLANES:]
    nt = pl.num_programs(1)
    step = pl.program_id(0) * nt + pl.program_id(1)
    last = pl.num_programs(0) * nt - 1

    def h_copy(s, slot):
        start = (s // nt) * lp_len + N_META + (s % nt) * tile
        return pltpu.make_async_copy(h1_hbm.at[pl.ds(start, tile), :], hbuf.at[slot], sem_h.at[slot])

    slot = step % 2

    @pl.when(step == 0)
    def _():
        h_copy(step, slot).start()

    @pl.when(step < last)
    def _():
        h_copy(step + 1, 1 - slot).start()

    ya = _unpack_rows([r[...] for r in a_refs]).astype(F32)
    yb = _unpack_rows([r[...] for r in b_refs]).astype(F32)
    g = jnp.concatenate([route_ref[...]] * (LANES // ROUTE_COLS), axis=0).T
    moe = g[:, TOP_K:TOP_K + 1] * ya + g[:, TOP_K + 1:TOP_K + 2] * yb
    h_copy(step, slot).wait()
    out_ref[0] = hbuf[slot] + moe


def _combine(gates, h1, a_planes, b_planes, *, batch, seq, lp_len, tile):
    d = h1.shape[1]
    nt = seq // tile
    kern = functools.partial(_combine_kernel, lp_len=lp_len, tile=tile)
    rows = lambda b, i: (b * nt + i, 0)
    return pl.pallas_call(
        kern,
        grid=(batch, nt),
        in_specs=[
            pl.BlockSpec((ROUTE_COLS, tile), lambda b, i: (0, b * nt + i)),
            pl.BlockSpec(memory_space=pl.ANY),
            *[pl.BlockSpec((tile, LANES), rows)] * (2 * ROW_PLANES),
        ],
        out_specs=pl.BlockSpec((1, tile, d), lambda b, i: (b, i, 0)),
        out_shape=jax.ShapeDtypeStruct((batch, seq, d), F32),
        scratch_shapes=[pltpu.VMEM((2, tile, d), F32), pltpu.SemaphoreType.DMA((2,))],
        compiler_params=_cparams(2),
        name="moe_combine",
    )(gates, h1, *a_planes, *b_planes)


def _rope_tables(lp_len):
    half = ROPE_DIM // 2
    pos = jnp.arange(lp_len, dtype=F32)
    inv_freq = ROPE_THETA ** (-jnp.arange(0, ROPE_DIM, 2, dtype=F32) / ROPE_DIM)
    ang = pos[:, None] * inv_freq[None, :]
    lane = jnp.arange(LANES) % HEAD_DIM
    cos = jnp.tile(jnp.cos(ang), (1, LANES // half))
    sin = jnp.tile(jnp.sin(ang), (1, LANES // half))
    c = jnp.where(lane < ROPE_DIM, cos, 1.0)
    s1 = jnp.where((lane >= half) & (lane < ROPE_DIM), sin, 0.0)
    s2 = jnp.where(lane < half, -sin, 0.0)
    return c, s1, s2


def kernel(x, meta_tokens, norm1_g, w_in, conv_w, q_norm_g, k_norm_g, lambda_q1, lambda_k1, lambda_q2, lambda_k2,
           subln_g, w_out, norm2_g, w_router_group, b_router_group, w_router_expert, b_router_expert, w_gate,
           w_up, w_down):
    batch, seq, _ = x.shape
    assert w_in.shape[0] == 1, "a single layer is supported"
    l = 0
    length = seq + N_META
    tm = TOKEN_TILE
    lp_len = -(-length // tm) * tm
    tiles_per_seq = lp_len // tm
    assert tiles_per_seq >= 2 and (length - (tiles_per_seq - 1) * tm) % 8 == 0
    qw = N_HEADS * 2 * HEAD_DIM
    lam_init = 0.8 - 0.6 * math.exp(-0.3 * l)

    reps = qw // HEAD_DIM
    gq = jnp.tile(q_norm_g[l] * (HEAD_DIM ** -0.5 * LOG2E), reps)[None, :]
    gk = jnp.tile(k_norm_g[l], reps)[None, :]
    seg = jnp.arange(qw) // HEAD_DIM
    bd = (seg[:, None] == seg[None, :]).astype(BF16)
    rope = _rope_tables(lp_len)
    hp, convy, q, k, v = _inproj(x, meta_tokens.astype(x.dtype), norm1_g[l][None, :], w_in[l].astype(BF16),
                                 conv_w[l], gq, gk, bd, *rope, tiles_per_seq=tiles_per_seq, tm=tm)

    lamp = jnp.stack([lambda_q1[l], lambda_k1[l], lambda_q2[l], lambda_k2[l]]).astype(F32)
    o = _attention(q, k, v, lamp, subln_g[l][None, :], batch=batch, lp_len=lp_len, tq=tm, lam_init=lam_init)

    lane_pad = LANES - N_GROUPS - N_EXPERTS
    wr = jnp.pad(jnp.concatenate([w_router_group[l], w_router_expert[l]], axis=1), ((0, 0), (0, lane_pad)))
    wr_hi = wr.astype(BF16)
    wr_lo = (wr - wr_hi.astype(F32)).astype(BF16)
    br = jnp.pad(jnp.concatenate([b_router_group[l], b_router_expert[l]]), (0, lane_pad))[None, :]
    ridx = jnp.arange(tm)
    upper = (ridx[:, None] < ridx[None, :]).astype(BF16)
    h1, *rest = _outproj(hp, convy, o, w_out[l].astype(BF16), norm2_g[l][None, :], wr_hi, wr_lo, br, upper,
                         tiles_per_seq=tiles_per_seq, seq_len=length)
    x_planes = rest[:ROW_PLANES]
    route, cnt = rest[ROW_PLANES:]

    counts = cnt[EXPERT_LANE0:EXPERT_LANE0 + N_EXPERTS, 0].astype(jnp.int32)
    n_blocks = -(-(batch * length * TOP_K) // MOE_BLOCK) + N_EXPERTS
    p_rows = n_blocks * MOE_BLOCK
    padded = (counts + MOE_BLOCK - 1) // MOE_BLOCK * MOE_BLOCK
    pends = jnp.cumsum(padded)
    pstarts = pends - padded

    def lookup(table, idx):
        hit = idx[:, None] == jnp.arange(N_EXPERTS, dtype=jnp.int32)[None, :]
        return jnp.sum(jnp.where(hit, table[None, :], 0), axis=1)

    blk0 = jnp.arange(n_blocks, dtype=jnp.int32) * MOE_BLOCK
    block_e = jnp.minimum(jnp.sum((pends[None, :] <= blk0[:, None]).astype(jnp.int32), axis=1), N_EXPERTS - 1)
    nvalid = jnp.clip(lookup(counts, block_e) - (blk0 - lookup(pstarts, block_e)), 0, MOE_BLOCK)

    spare_rows = -(-(batch * (lp_len - length) * TOP_K) // MOE_BLOCK) * MOE_BLOCK
    dest = _slots(pstarts.astype(jnp.int32), route, batch=batch, lp_len=lp_len, seq_len=length, p_rows=p_rows)
    dest = dest.reshape(ROUTE_COLS, batch, lp_len)

    assert (batch * lp_len) % SC_WINDOW == 0 and (batch * seq) % SC_WINDOW == 0
    xs_planes = _sc_scatter_rows(x_planes, dest[0].reshape(-1, SC_WINDOW), dest[1].reshape(-1, SC_WINDOW),
                                 p_rows + spare_rows)
    y_planes = _experts(block_e, nvalid, xs_planes, w_gate[l], w_up[l], w_down[l])

    dest_x = dest[0:TOP_K, :, N_META:length]
    a_planes, b_planes = _sc_gather_rows(y_planes, dest_x[0].reshape(-1, SC_WINDOW),
                                         dest_x[1].reshape(-1, SC_WINDOW))
    route_x = route.reshape(ROUTE_COLS, batch, lp_len)[:, :, N_META:length].reshape(ROUTE_COLS, batch * seq)
    return _combine(route_x, h1, a_planes, b_planes, batch=batch, seq=seq, lp_len=lp_len,
                    tile=_largest_tile(seq, COMBINE_TILE, LANES))
```

```python
import functools
import math

import jax
import jax.numpy as jnp
from jax import lax
from jax.experimental import pallas as pl
from jax.experimental.pallas import tpu as pltpu
from jax.experimental.pallas import tpu_sc as plsc

F32 = jnp.float32
BF16 = jnp.bfloat16

N_META = 16
N_HEADS = 4
HEAD_DIM = 64
ROPE_DIM = HEAD_DIM // 4
ROPE_THETA = 500000.0
N_GROUPS = 4
EXPERTS_PER_GROUP = 8
N_EXPERTS = N_GROUPS * EXPERTS_PER_GROUP
TOP_K = 2
EPS = 1e-6
LOG2E = 1.4426950408889634

LANES = 128
TOKEN_TILE = 640
INPROJ_UNITS = 2
OUTPROJ_UNITS = 2
ATTN_TAIL_ROWS = ((0, 256), (256, 384))
ATTN_TAIL_ROWS_WIDE = ((0, 384), (384, 256))
ATTN_HEADS_PER_STEP = 4
MOE_BLOCK = 1024
MOE_TAIL_STEP = 256
COMBINE_TILE = 1024
ROUTE_COLS = 8
ROUTE_ROWS = 64
ROW_PLANES = 4
SC_WINDOW = 128
EXPERT_LANE0 = N_GROUPS
NEG_BIG = -1e30
VMEM_LIMIT = 56 * 1024 * 1024


def _largest_tile(n, cap, mult):
    for t in range(min(cap, n), 0, -1):
        if n % t == 0 and t % mult == 0:
            return t
    raise ValueError(f"no tile for {n}")


def _cparams(n_axes):
    return pltpu.CompilerParams(dimension_semantics=("arbitrary",) * n_axes, vmem_limit_bytes=VMEM_LIMIT)


def _pack_rows(x):
    w = x.shape[1] // 2
    lo = lax.bitcast_convert_type(x[:, :w].astype(BF16).astype(F32), jnp.uint32)
    hi = lax.bitcast_convert_type(x[:, w:].astype(BF16).astype(F32), jnp.uint32)
    return lax.shift_right_logical(lo, jnp.uint32(16)) | (hi & jnp.uint32(0xFFFF0000))


def _unpack_rows(planes):
    w = jnp.concatenate(planes, axis=1)
    lo = lax.bitcast_convert_type(lax.shift_left(w, jnp.uint32(16)), F32)
    hi = lax.bitcast_convert_type(w & jnp.uint32(0xFFFF0000), F32)
    return jnp.concatenate([lo, hi], axis=1).astype(BF16)


def _inproj_kernel(x_hbm, meta_hbm, g1_ref, win_ref, convw_ref, gq_ref, gk_ref, bd_ref, *refs,
                   tiles_per_seq, seq, cw, qw, units):
    rope_refs = refs[0:3 * units]
    hp_hbm, convy_ref, q_ref, k_ref, v_ref, carry_ref, xbuf, sem, sem_out = refs[3 * units:]
    i = pl.program_id(0)
    n_steps = pl.num_programs(0)
    tm = xbuf.shape[1]
    q0 = 3 * cw
    w = convw_ref[...]
    last_rows = seq + N_META - (tiles_per_seq - 1) * tm

    def fetch(tile, slot, start):
        b = tile // tiles_per_seq
        t = tile % tiles_per_seq

        def go(src, dst):
            cp = pltpu.make_async_copy(src, dst, sem.at[slot])
            if start:
                cp.start()
            else:
                cp.wait()

        @pl.when(t == 0)
        def _():
            go(meta_hbm, xbuf.at[slot, pl.ds(0, N_META)])
            go(x_hbm.at[pl.ds(b * seq, tm - N_META)], xbuf.at[slot, pl.ds(N_META, tm - N_META)])

        @pl.when((t > 0) & (t < tiles_per_seq - 1))
        def _():
            go(x_hbm.at[pl.ds(b * seq + t * tm - N_META, tm)], xbuf.at[slot])

        @pl.when(t == tiles_per_seq - 1)
        def _():
            go(x_hbm.at[pl.ds(b * seq + t * tm - N_META, last_rows)], xbuf.at[slot, pl.ds(0, last_rows)])

    def hp_store(tile, slot):
        return pltpu.make_async_copy(xbuf.at[slot], hp_hbm.at[pl.ds(tile * tm, tm)], sem_out.at[slot])

    mine = (i % 2) * units
    other = units - mine

    @pl.when(i == 0)
    def _():
        for u in range(units):
            fetch(u, mine + u, True)

    @pl.when(i > 0)
    def _():
        for u in range(units):
            hp_store((i - 1) * units + u, other + u).wait()

    @pl.when(i + 1 < n_steps)
    def _():
        for u in range(units):
            fetch((i + 1) * units + u, other + u, True)

    for u in range(units):
        tile = i * units + u
        fetch(tile, mine + u, False)

        @pl.when(tile % tiles_per_seq == tiles_per_seq - 1)
        def _():
            xbuf[mine + u, pl.ds(last_rows, tm - last_rows), :] = jnp.zeros((tm - last_rows, xbuf.shape[2]),
                                                                           xbuf.dtype)

        hp_store(tile, mine + u).start()

    prev = carry_ref[...]

    for u in range(units):
        rows = pl.ds(u * tm, tm)
        x = xbuf[mine + u]
        ms = jnp.mean(x * x, axis=-1, keepdims=True)
        xn = (x * lax.rsqrt(ms + EPS) * g1_ref[...]).astype(BF16)

        def proj(lo, hi):
            return jnp.dot(xn, win_ref[:, lo:hi], preferred_element_type=F32)

        rc, rs1, rs2 = (ref[...] for ref in rope_refs[3 * u:3 * u + 3])

        def norm_rope(t, g_ref):
            ss = jnp.dot((t * t).astype(BF16), bd_ref[...], preferred_element_type=F32)
            tn = t * lax.rsqrt(ss * (1.0 / HEAD_DIM) + EPS) * g_ref[...]
            outs = []
            for c in range(qw // LANES):
                ch = tn[:, c * LANES:(c + 1) * LANES]
                outs.append(ch * rc + pltpu.roll(ch, ROPE_DIM // 2, axis=1) * rs1
                            + pltpu.roll(ch, LANES - ROPE_DIM // 2, axis=1) * rs2)
            return jnp.concatenate(outs, axis=1).astype(BF16)

        u_q = proj(q0, q0 + qw)
        u_k = proj(q0 + qw, q0 + 2 * qw)
        q_ref[rows, :] = norm_rope(u_q, gq_ref)
        u_cc = proj(cw, 2 * cw)
        u_cx = proj(2 * cw, 3 * cw)
        k_ref[rows, :] = norm_rope(u_k, gk_ref)
        u_v = proj(q0 + 2 * qw, win_ref.shape[1])

        z = u_cc * u_cx
        prev = jnp.where((i * units + u) % tiles_per_seq == 0, 0.0, prev)
        p1 = prev[7:8]
        p2 = prev[6:7]
        row = lax.broadcasted_iota(jnp.int32, z.shape, 0)
        z1 = jnp.where(row == 0, p1, pltpu.roll(z, 1, axis=0))
        z2 = jnp.where(row == 0, p2, jnp.where(row == 1, p1, pltpu.roll(z, 2, axis=0)))
        prev = z[tm - 8:tm]
        conv = w[0:1] * z2 + w[1:2] * z1 + w[2:3] * z
        v_ref[rows, :] = u_v.astype(BF16)
        u_cb = proj(0, cw)
        convy_ref[rows, :] = (u_cb * conv).astype(BF16)

    carry_ref[...] = prev

    @pl.when(i == n_steps - 1)
    def _():
        for u in range(units):
            hp_store(i * units + u, mine + u).wait()


def _inproj(x, meta, g1, w_in, conv_w, gq, gk, bd, rc, rs1, rs2, *, tiles_per_seq, tm):
    batch, seq, d = x.shape
    units = INPROJ_UNITS
    n = batch * tiles_per_seq * tm
    assert (batch * tiles_per_seq) % units == 0
    cw = conv_w.shape[1]
    qw = gq.shape[1]
    aw = w_in.shape[1] - 3 * cw - 2 * qw
    const = lambda i: (0, 0)
    step = lambda i: (i, 0)
    rope_specs = [pl.BlockSpec((tm, LANES), lambda i, u=u: ((i * units + u) % tiles_per_seq, 0))
                  for u in range(units) for _ in range(3)]
    kern = functools.partial(_inproj_kernel, tiles_per_seq=tiles_per_seq, seq=seq, cw=cw, qw=qw, units=units)
    return pl.pallas_call(
        kern,
        grid=(n // (units * tm),),
        in_specs=[
            pl.BlockSpec(memory_space=pl.ANY),
            pl.BlockSpec(memory_space=pl.ANY),
            pl.BlockSpec((1, d), const),
            pl.BlockSpec(w_in.shape, const),
            pl.BlockSpec(conv_w.shape, const),
            pl.BlockSpec((1, qw), const),
            pl.BlockSpec((1, qw), const),
            pl.BlockSpec(bd.shape, const),
            *rope_specs,
        ],
        out_specs=[
            pl.BlockSpec(memory_space=pl.ANY),
            pl.BlockSpec((units * tm, cw), step),
            pl.BlockSpec((units * tm, qw), step),
            pl.BlockSpec((units * tm, qw), step),
            pl.BlockSpec((units * tm, aw), step),
        ],
        out_shape=[
            jax.ShapeDtypeStruct((n, d), x.dtype),
            jax.ShapeDtypeStruct((n, cw), BF16),
            jax.ShapeDtypeStruct((n, qw), BF16),
            jax.ShapeDtypeStruct((n, qw), BF16),
            jax.ShapeDtypeStruct((n, aw), BF16),
        ],
        scratch_shapes=[pltpu.VMEM((8, cw), F32), pltpu.VMEM((2 * units, tm, d), x.dtype),
                        pltpu.SemaphoreType.DMA((2 * units,)), pltpu.SemaphoreType.DMA((2 * units,))],
        compiler_params=_cparams(1),
        name="inproj",
    )(x.reshape(batch * seq, d), meta, g1, w_in, conv_w, gq, gk, bd, *([rc, rs1, rs2] * units))


def _attn_kernel(q_ref, k_ref, v_ref, lamp_ref, sg_ref, o_ref, qs_ref, m_ref, l_ref, acc_ref, *, lam_init):
    qi = pl.program_id(2)
    tq = q_ref.shape[0]
    n_heads = q_ref.shape[1] // LANES
    n_chains = 2 * n_heads
    lane = lax.broadcasted_iota(jnp.int32, (tq, LANES), 1)
    for h in range(n_heads):
        q = q_ref[:, h * LANES:(h + 1) * LANES]
        zero = jnp.zeros_like(q)
        qs_ref[pl.ds(2 * h * tq, tq), :] = jnp.where(lane < HEAD_DIM, q, zero)
        qs_ref[pl.ds((2 * h + 1) * tq, tq), :] = jnp.where(lane >= HEAD_DIM, q, zero)

    def scores(off, width, which, r0=0, nr=None):
        h = which // 2
        nr = tq if nr is None else nr
        kc = k_ref[pl.ds(off, width), h * LANES:(h + 1) * LANES]
        return lax.dot_general(qs_ref[pl.ds(which * tq + r0, nr), :], kc, (((1,), (1,)), ((), ())),
                               preferred_element_type=F32)

    def update(off, width, which, s, first, r0=0, nr=None, diag=None):
        h = which // 2
        nr = tq if nr is None else nr
        vc = jnp.concatenate([v_ref[pl.ds(off, width), h * LANES:(h + 1) * LANES],
                              jnp.ones((width, LANES), BF16)], axis=1)
        rows = pl.ds(which * tq + r0, nr)
        if diag is not None:
            r = lax.broadcasted_iota(jnp.int32, s.shape, 0)
            c = lax.broadcasted_iota(jnp.int32, s.shape, 1)
            s = jnp.where(c <= r + (r0 + diag), s, NEG_BIG)
        m_prev = jnp.where(first, NEG_BIG, m_ref[rows, :])
        l_prev = jnp.where(first, 0.0, l_ref[rows, :])
        acc_prev = jnp.where(first, 0.0, acc_ref[rows, :])
        m_new = jnp.maximum(m_prev, jnp.max(s, axis=-1, keepdims=True))
        alpha = jnp.exp2(m_prev - m_new)
        p = jnp.exp2((s - jnp.tile(m_new, (1, width // LANES))).astype(BF16))
        pv = jnp.dot(p, vc, preferred_element_type=F32)
        l_ref[rows, :] = alpha * l_prev + pv[:, LANES:]
        acc_ref[rows, :] = alpha * acc_prev + pv[:, :LANES]
        m_ref[rows, :] = m_new

    def run_pieces(pieces, first):
        s_next = scores(pieces[0][1], pieces[0][2], pieces[0][0], pieces[0][3], pieces[0][4])
        for n, (c, off, width, r0, nr, diag) in enumerate(pieces):
            s = s_next
            if n + 1 < len(pieces):
                c2, off2, width2, r02, nr2, _ = pieces[n + 1]
                s_next = scores(off2, width2, c2, r02, nr2)
            update(off, width, c, s, first, r0, nr, diag)

    def chunk(off, width, first):
        run_pieces([(c, off, width, 0, tq, None) for c in range(n_chains)], first)

    def tail_chunk(off, width, row_groups, first):
        pieces = [(c, off, width - tq + r0 + nr, r0, nr, width - tq)
                  for c in range(n_chains) for r0, nr in row_groups]
        run_pieces(pieces, first)

    wide = 2 * tq

    def body(j, carry):
        chunk(pl.multiple_of(j * wide, wide), wide, j == 0)
        return carry

    lax.fori_loop(0, qi // 2, body, 0)
    odd = qi % 2 == 1

    @pl.when(odd)
    def _():
        tail_chunk(pl.multiple_of((qi - 1) * tq, tq), wide, ATTN_TAIL_ROWS_WIDE, qi < 2)

    @pl.when(jnp.logical_not(odd))
    def _():
        tail_chunk(pl.multiple_of(qi * tq, tq), tq, ATTN_TAIL_ROWS, qi < 2)

    lp = lamp_ref[...]
    lam = (jnp.exp(jnp.sum(lp[0:1] * lp[1:2], axis=-1, keepdims=True))
           - jnp.exp(jnp.sum(lp[2:3] * lp[3:4], axis=-1, keepdims=True)) + lam_init)
    for h in range(n_heads):
        rows = pl.ds(2 * h * tq, 2 * tq)
        o_all = acc_ref[rows, :] / l_ref[rows, :]
        o = o_all[0:tq] - lam * o_all[tq:2 * tq]
        ms = jnp.mean(o * o, axis=-1, keepdims=True)
        o_ref[:, h * LANES:(h + 1) * LANES] = (o * lax.rsqrt(ms + EPS) * sg_ref[...]
                                               * (1.0 - lam_init)).astype(BF16)


def _attention(q, k, v, lamp, sg, *, batch, lp_len, tq, lam_init):
    n, qw = q.shape
    nq = lp_len // tq
    hw = ATTN_HEADS_PER_STEP * LANES
    chains = 2 * ATTN_HEADS_PER_STEP
    kern = functools.partial(_attn_kernel, lam_init=lam_init)
    return pl.pallas_call(
        kern,
        grid=(batch, qw // hw, nq),
        in_specs=[
            pl.BlockSpec((tq, hw), lambda b, h, i: (b * nq + i, h)),
            pl.BlockSpec((lp_len, hw), lambda b, h, i: (b, h)),
            pl.BlockSpec((lp_len, hw), lambda b, h, i: (b, h)),
            pl.BlockSpec(lamp.shape, lambda b, h, i: (0, 0)),
            pl.BlockSpec(sg.shape, lambda b, h, i: (0, 0)),
        ],
        out_specs=pl.BlockSpec((tq, hw), lambda b, h, i: (b * nq + i, h)),
        out_shape=jax.ShapeDtypeStruct((n, v.shape[1]), BF16),
        scratch_shapes=[
            pltpu.VMEM((chains * tq, LANES), BF16),
            pltpu.VMEM((chains * tq, LANES), F32),
            pltpu.VMEM((chains * tq, LANES), F32),
            pltpu.VMEM((chains * tq, LANES), F32),
        ],
        compiler_params=_cparams(3),
        name="diffattn",
    )(q, k, v, lamp, sg)


def _outproj_kernel(hp_ref, cy_ref, o_ref, wout_ref, g2_ref, wrh_ref, wrl_ref, br_ref, upper_ref,
                    h1_ref, xp0_ref, xp1_ref, xp2_ref, xp3_ref, route_ref, cnt_ref, run_ref,
                    *, tiles_per_seq, seq_len):
    i = pl.program_id(0)
    tm = upper_ref.shape[0]
    n_units = hp_ref.shape[0] // tm

    @pl.when(i == 0)
    def _():
        run_ref[...] = jnp.zeros_like(run_ref)

    def project(rows):
        mix = jnp.concatenate([cy_ref[rows, :], o_ref[rows, :]], axis=1)
        h1 = hp_ref[rows, :] + jnp.dot(mix, wout_ref[...], preferred_element_type=F32)
        h1_ref[rows, :] = h1
        ms = jnp.mean(h1 * h1, axis=-1, keepdims=True)
        xn = h1 * lax.rsqrt(ms + EPS) * g2_ref[...]
        xw = _pack_rows(xn)
        for c, ref in enumerate((xp0_ref, xp1_ref, xp2_ref, xp3_ref)):
            ref[rows, :] = xw[:, c * LANES:(c + 1) * LANES]
        return xn

    def router_logits(xn):
        x_hi = xn.astype(BF16)
        x_lo = (xn - x_hi.astype(F32)).astype(BF16)
        hi_both = jnp.dot(x_hi, jnp.concatenate([wrh_ref[...], wrl_ref[...]], axis=1),
                          preferred_element_type=F32)
        return (hi_both[:, :LANES] + hi_both[:, LANES:]
                + jnp.dot(x_lo, wrh_ref[...], preferred_element_type=F32) + br_ref[...])

    def route(u, logits, run):
        lt = logits.T[0:ROUTE_ROWS, :]
        row = lax.broadcasted_iota(jnp.int32, lt.shape, 0)
        big = jnp.int32(4 * LANES)

        def first_argmax(vals, vmax):
            return jnp.min(jnp.where(vals == vmax, row, big), axis=0, keepdims=True)

        gl = jnp.where(row < N_GROUPS, lt, NEG_BIG)
        gmax = jnp.max(gl, axis=0, keepdims=True)
        g_val = 1.0 / jnp.sum(jnp.exp(gl - gmax), axis=0, keepdims=True)
        g_idx = first_argmax(gl, gmax)
        lo = EXPERT_LANE0 + EXPERTS_PER_GROUP * g_idx
        el = jnp.where((row >= lo) & (row < lo + EXPERTS_PER_GROUP), lt, NEG_BIG)
        m1 = jnp.max(el, axis=0, keepdims=True)
        i1 = first_argmax(el, m1)
        el2 = jnp.where(row == i1, NEG_BIG, el)
        m2 = jnp.max(el2, axis=0, keepdims=True)
        i2 = first_argmax(el2, m2)
        r = jnp.exp(m2 - m1)
        gate1 = g_val / (1.0 + r)
        gate2 = g_val * r / (1.0 + r)

        pos = ((i * n_units + u) % tiles_per_seq) * tm + lax.broadcasted_iota(jnp.int32, (1, tm), 1)
        valid = pos < seq_len
        oh1 = jnp.where(valid & (row == i1), 1.0, 0.0)
        oh2 = jnp.where(valid & (row == i2), 1.0, 0.0)
        pre = jnp.dot(jnp.concatenate([oh1, oh2], axis=0).astype(BF16), upper_ref[...],
                      preferred_element_type=F32)
        tot1 = jnp.sum(oh1, axis=1, keepdims=True)
        tot2 = jnp.sum(oh2, axis=1, keepdims=True)
        run_t = jnp.tile(run, (1, tm // LANES))
        rank1 = jnp.sum(oh1 * (pre[:ROUTE_ROWS] + run_t), axis=0, keepdims=True)
        rank2 = jnp.sum(oh2 * (pre[ROUTE_ROWS:] + run_t + tot1), axis=0, keepdims=True)

        e1 = (i1 - EXPERT_LANE0).astype(F32)
        e2 = (i2 - EXPERT_LANE0).astype(F32)
        r8 = lax.broadcasted_iota(jnp.int32, (ROUTE_COLS, tm), 0)
        route_ref[:, u * tm:(u + 1) * tm] = jnp.where(r8 == 0, e1, jnp.where(r8 == 1, e2, jnp.where(
            r8 == 2, gate1, jnp.where(r8 == 3, gate2, jnp.where(r8 == 4, rank1, jnp.where(r8 == 5, rank2, 0.0))))))
        return run + tot1 + tot2

    normed = [project(pl.ds(u * tm, tm)) for u in range(n_units)]
    all_logits = [router_logits(xn) for xn in normed]
    run = run_ref[...]
    for u, logits in enumerate(all_logits):
        run = route(u, logits, run)
    run_ref[...] = run
    cnt_ref[...] = run


def _outproj(hp, convy, o, w_out, g2, wr_hi, wr_lo, br, upper, *, tiles_per_seq, seq_len):
    n, d = hp.shape
    tm = OUTPROJ_UNITS * upper.shape[0]
    assert n % tm == 0
    const = lambda i: (0, 0)
    tile = lambda i: (i, 0)
    kern = functools.partial(_outproj_kernel, tiles_per_seq=tiles_per_seq, seq_len=seq_len)
    return pl.pallas_call(
        kern,
        grid=(n // tm,),
        in_specs=[
            pl.BlockSpec((tm, d), tile),
            pl.BlockSpec((tm, convy.shape[1]), tile),
            pl.BlockSpec((tm, o.shape[1]), tile),
            pl.BlockSpec(w_out.shape, const),
            pl.BlockSpec((1, d), const),
            pl.BlockSpec(wr_hi.shape, const),
            pl.BlockSpec(wr_lo.shape, const),
            pl.BlockSpec((1, LANES), const),
            pl.BlockSpec(upper.shape, const),
        ],
        out_specs=[
            pl.BlockSpec((tm, d), tile),
            *[pl.BlockSpec((tm, LANES), tile)] * ROW_PLANES,
            pl.BlockSpec((ROUTE_COLS, tm), lambda i: (0, i)),
            pl.BlockSpec((ROUTE_ROWS, LANES), const),
        ],
        out_shape=[
            jax.ShapeDtypeStruct((n, d), F32),
            *[jax.ShapeDtypeStruct((n, LANES), jnp.uint32)] * ROW_PLANES,
            jax.ShapeDtypeStruct((ROUTE_COLS, n), F32),
            jax.ShapeDtypeStruct((ROUTE_ROWS, LANES), F32),
        ],
        scratch_shapes=[pltpu.VMEM((ROUTE_ROWS, LANES), F32)],
        compiler_params=_cparams(1),
        name="outproj_router",
    )(hp, convy, o, w_out, g2, wr_hi, wr_lo, br, upper)


def _slots_kernel(pstart_ref, route_ref, dest_ref, *, seq_len, p_rows):
    b = pl.program_id(0)
    route = route_ref[...]
    eid = route.astype(jnp.int32)
    start = jnp.zeros_like(eid)
    for e in range(N_EXPERTS):
        start = jnp.where(eid == e, pstart_ref[e], start)
    rank = pltpu.roll(route, ROUTE_COLS - 2 * TOP_K, axis=0).astype(jnp.int32)
    k = lax.broadcasted_iota(jnp.int32, route.shape, 0)
    pos = lax.broadcasted_iota(jnp.int32, route.shape, 1)
    n_pad = route.shape[1] - seq_len
    spare = p_rows + (b * n_pad + (pos - seq_len)) * TOP_K + k
    dest_ref[...] = jnp.where(pos < seq_len, start + rank, spare)


def _slots(pstarts, route, *, batch, lp_len, seq_len, p_rows):
    kern = functools.partial(_slots_kernel, seq_len=seq_len, p_rows=p_rows)
    grid_spec = pltpu.PrefetchScalarGridSpec(
        num_scalar_prefetch=1,
        grid=(batch,),
        in_specs=[pl.BlockSpec((ROUTE_COLS, lp_len), lambda b, ps: (0, b))],
        out_specs=pl.BlockSpec((ROUTE_COLS, lp_len), lambda b, ps: (0, b)),
    )
    return pl.pallas_call(
        kern,
        grid_spec=grid_spec,
        out_shape=jax.ShapeDtypeStruct(route.shape, jnp.int32),
        compiler_params=_cparams(1),
        name="moe_slots",
    )(pstarts, route)


def _sc_workers():
    info = plsc.get_sparse_core_info()
    return info.num_cores, info.num_cores * info.num_subcores


def _sc_scatter_rows(planes, idx_a, idx_b, out_rows):
    n_win = idx_a.shape[0]
    n_cores, n_workers = _sc_workers()
    trips = -(-n_win // n_workers)
    mesh = plsc.VectorSubcoreMesh(core_axis_name="c", subcore_axis_name="s")

    def body(*refs):
        xs = refs[0:ROW_PLANES]
        ia_hbm, ib_hbm = refs[ROW_PLANES:ROW_PLANES + 2]
        outs = refs[ROW_PLANES + 2:2 * ROW_PLANES + 2]
        ia_v, ib_v, buf, sem = refs[2 * ROW_PLANES + 2:]
        wid = lax.axis_index("s") * n_cores + lax.axis_index("c")

        def step(t, carry):
            g = wid + t * n_workers

            @pl.when(g < n_win)
            def _():
                row0 = pl.multiple_of(g * SC_WINDOW, SC_WINDOW)
                loads = [pltpu.async_copy(ia_hbm.at[g], ia_v, sem), pltpu.async_copy(ib_hbm.at[g], ib_v, sem)]
                loads += [pltpu.async_copy(xs[c].at[pl.ds(row0, SC_WINDOW)], buf.at[c], sem)
                          for c in range(ROW_PLANES)]
                for cp in loads:
                    cp.wait()
                stores = [pltpu.async_copy(buf.at[c], outs[c].at[iv], sem)
                          for c in range(ROW_PLANES) for iv in (ia_v, ib_v)]
                for cp in stores:
                    cp.wait()

            return carry

        lax.fori_loop(0, trips, step, 0)

    kern = pl.kernel(
        body,
        out_type=[jax.ShapeDtypeStruct((out_rows, LANES), jnp.uint32)] * ROW_PLANES,
        mesh=mesh,
        scratch_types=[
            pltpu.VMEM((SC_WINDOW,), jnp.int32),
            pltpu.VMEM((SC_WINDOW,), jnp.int32),
            pltpu.VMEM((ROW_PLANES, SC_WINDOW, LANES), jnp.uint32),
            pltpu.SemaphoreType.DMA,
        ],
        name="moe_dispatch_sc",
    )
    return kern(*planes, idx_a, idx_b)


def _sc_gather_rows(planes, idx_a, idx_b):
    n_win = idx_a.shape[0]
    n_cores, n_workers = _sc_workers()
    trips = -(-n_win // n_workers)
    mesh = plsc.VectorSubcoreMesh(core_axis_name="c", subcore_axis_name="s")

    def body(*refs):
        ys = refs[0:ROW_PLANES]
        ia_hbm, ib_hbm = refs[ROW_PLANES:ROW_PLANES + 2]
        outs_a = refs[ROW_PLANES + 2:2 * ROW_PLANES + 2]
        outs_b = refs[2 * ROW_PLANES + 2:3 * ROW_PLANES + 2]
        ia_v, ib_v, buf, sem = refs[3 * ROW_PLANES + 2:]
        wid = lax.axis_index("s") * n_cores + lax.axis_index("c")

        def step(t, carry):
            g = wid + t * n_workers

            @pl.when(g < n_win)
            def _():
                row0 = pl.multiple_of(g * SC_WINDOW, SC_WINDOW)
                idx_loads = [pltpu.async_copy(ia_hbm.at[g], ia_v, sem), pltpu.async_copy(ib_hbm.at[g], ib_v, sem)]
                for cp in idx_loads:
                    cp.wait()
                for iv, outs in ((ia_v, outs_a), (ib_v, outs_b)):
                    loads = [pltpu.async_copy(ys[c].at[iv], buf.at[c], sem) for c in range(ROW_PLANES)]
                    for cp in loads:
                        cp.wait()
                    stores = [pltpu.async_copy(buf.at[c], outs[c].at[pl.ds(row0, SC_WINDOW)], sem)
                              for c in range(ROW_PLANES)]
                    for cp in stores:
                        cp.wait()

            return carry

        lax.fori_loop(0, trips, step, 0)

    n_rows = n_win * SC_WINDOW
    kern = pl.kernel(
        body,
        out_type=[jax.ShapeDtypeStruct((n_rows, LANES), jnp.uint32)] * (2 * ROW_PLANES),
        mesh=mesh,
        scratch_types=[
            pltpu.VMEM((SC_WINDOW,), jnp.int32),
            pltpu.VMEM((SC_WINDOW,), jnp.int32),
            pltpu.VMEM((ROW_PLANES, SC_WINDOW, LANES), jnp.uint32),
            pltpu.SemaphoreType.DMA,
        ],
        name="moe_gather_sc",
    )
    res = kern(*planes, idx_a, idx_b)
    return res[:ROW_PLANES], res[ROW_PLANES:]


def _experts_kernel(be_ref, nv_ref, first_ref, slot_ref, nxt_ref, x0_ref, x1_ref, x2_ref, x3_ref,
                    wg_hbm, wu_hbm, wd_hbm, y0_ref, y1_ref, y2_ref, y3_ref,
                    wgf_ref, wuf_ref, wdf_ref, wgb_ref, wub_ref, wdb_ref, sem):
    i = pl.program_id(0)
    e = be_ref[i]
    slot = slot_ref[i]
    y_refs = (y0_ref, y1_ref, y2_ref, y3_ref)

    def weight_copies(expert, s):
        return [pltpu.make_async_copy(hbm.at[expert], stage.at[s], sem.at[s, j])
                for j, (hbm, stage) in enumerate(((wg_hbm, wgf_ref), (wu_hbm, wuf_ref), (wd_hbm, wdf_ref)))]

    @pl.when(i == 0)
    def _():
        for cp in weight_copies(e, slot):
            cp.start()

    is_first = first_ref[i] == 1

    @pl.when(is_first)
    def _():
        for cp in weight_copies(e, slot):
            cp.wait()
        nxt = nxt_ref[i]

        @pl.when(nxt >= 0)
        def _():
            for cp in weight_copies(nxt, 1 - slot):
                cp.start()

    nv = nv_ref[i]

    def weights(stage_ref, bf_ref, fresh):
        if fresh:
            bf_ref[...] = stage_ref[slot].astype(BF16)
        return bf_ref[...]

    def mlp(rows, fresh):
        xs = _unpack_rows([r[rows, :] for r in (x0_ref, x1_ref, x2_ref, x3_ref)])
        row = lax.broadcasted_iota(jnp.int32, xs.shape, 0)
        x = jnp.where(row < nv, xs, jnp.zeros_like(xs))
        hg = jnp.dot(x, weights(wgf_ref, wgb_ref, fresh), preferred_element_type=F32)
        hu = jnp.dot(x, weights(wuf_ref, wub_ref, fresh), preferred_element_type=F32)
        hid = (hg / (1.0 + jnp.exp(-hg)) * hu).astype(BF16)
        yw = _pack_rows(jnp.dot(hid, weights(wdf_ref, wdb_ref, fresh), preferred_element_type=F32))
        for c, ref in enumerate(y_refs):
            ref[rows, :] = yw[:, c * LANES:(c + 1) * LANES]

    def clear(rows):
        for ref in y_refs:
            ref[rows, :] = jnp.zeros((rows.size, LANES), ref.dtype)

    for piece in range(MOE_BLOCK // MOE_TAIL_STEP + 1):
        used = piece * MOE_TAIL_STEP
        in_piece = (nv > used - MOE_TAIL_STEP) & (nv <= used)

        for fresh in (False, True):
            if fresh and used == 0:
                continue

            @pl.when(in_piece & (is_first if fresh else jnp.logical_not(is_first)) if used > 0 else in_piece)
            def _():
                if used > 0:
                    mlp(pl.ds(0, used), fresh)
                if used < MOE_BLOCK:
                    clear(pl.ds(used, MOE_BLOCK - used))


def _experts(block_e, nvalid, xs_planes, w_gate, w_up, w_down):
    n_blocks = block_e.shape[0]
    _, d, ff = w_gate.shape
    first = jnp.concatenate([jnp.ones((1,), jnp.int32), (block_e[1:] != block_e[:-1]).astype(jnp.int32)])
    slot = (jnp.cumsum(first) - 1) % 2
    later = jnp.where(block_e[None, :] > block_e[:, None], block_e[None, :], N_EXPERTS)
    nxt = jnp.min(later, axis=1)
    nxt = jnp.where(nxt == N_EXPERTS, -1, nxt).astype(jnp.int32)
    blk = lambda i, *_: (i, 0)
    grid_spec = pltpu.PrefetchScalarGridSpec(
        num_scalar_prefetch=5,
        grid=(n_blocks,),
        in_specs=[
            *[pl.BlockSpec((MOE_BLOCK, LANES), blk)] * ROW_PLANES,
            pl.BlockSpec(memory_space=pl.ANY),
            pl.BlockSpec(memory_space=pl.ANY),
            pl.BlockSpec(memory_space=pl.ANY),
        ],
        out_specs=[pl.BlockSpec((MOE_BLOCK, LANES), blk)] * ROW_PLANES,
        scratch_shapes=[
            pltpu.VMEM((2, d, ff), F32),
            pltpu.VMEM((2, d, ff), F32),
            pltpu.VMEM((2, ff, d), F32),
            pltpu.VMEM((d, ff), BF16),
            pltpu.VMEM((d, ff), BF16),
            pltpu.VMEM((ff, d), BF16),
            pltpu.SemaphoreType.DMA((2, 3)),
        ],
    )
    return pl.pallas_call(
        _experts_kernel,
        grid_spec=grid_spec,
        out_shape=[jax.ShapeDtypeStruct((n_blocks * MOE_BLOCK, LANES), jnp.uint32)] * ROW_PLANES,
        compiler_params=_cparams(1),
        name="moe_experts",
    )(block_e, nvalid, first, slot.astype(jnp.int32), nxt, *xs_planes, w_gate, w_up, w_down)


def _combine_kernel(route_ref, h1_hbm, *refs, lp_len, tile):
    a_refs = refs[0:ROW_PLANES]
    b_refs = refs[ROW_PLANES:2 * ROW_PLANES]
    out_ref, hbuf, sem_h = refs[2 * ROW_PLANES:]
    nt = pl.num_programs(1)
    step = pl.program_id(0) * nt + pl.program_id(1)
    last = pl.num_programs(0) * nt - 1

    def h_copy(s, slot):
        start = (s // nt) * lp_len + N_META + (s % nt) * tile
        return pltpu.make_async_copy(h1_hbm.at[pl.ds(start, tile), :], hbuf.at[slot], sem_h.at[slot])

    slot = step % 2

    @pl.when(step == 0)
    def _():
        h_copy(step, slot).start()

    @pl.when(step < last)
    def _():
        h_copy(step + 1, 1 - slot).start()

    ya = _unpack_rows([r[...] for r in a_refs]).astype(F32)
    yb = _unpack_rows([r[...] for r in b_refs]).astype(F32)
    g = jnp.concatenate([route_ref[...]] * (LANES // ROUTE_COLS), axis=0).T
    moe = g[:, TOP_K:TOP_K + 1] * ya + g[:, TOP_K + 1:TOP_K + 2] * yb
    h_copy(step, slot).wait()
    out_ref[0] = hbuf[slot] + moe


def _combine(gates, h1, a_planes, b_planes, *, batch, seq, lp_len, tile):
    d = h1.shape[1]
    nt = seq // tile
    kern = functools.partial(_combine_kernel, lp_len=lp_len, tile=tile)
    rows = lambda b, i: (b * nt + i, 0)
    return pl.pallas_call(
        kern,
        grid=(batch, nt),
        in_specs=[
            pl.BlockSpec((ROUTE_COLS, tile), lambda b, i: (0, b * nt + i)),
            pl.BlockSpec(memory_space=pl.ANY),
            *[pl.BlockSpec((tile, LANES), rows)] * (2 * ROW_PLANES),
        ],
        out_specs=pl.BlockSpec((1, tile, d), lambda b, i: (b, i, 0)),
        out_shape=jax.ShapeDtypeStruct((batch, seq, d), F32),
        scratch_shapes=[pltpu.VMEM((2, tile, d), F32), pltpu.SemaphoreType.DMA((2,))],
        compiler_params=_cparams(2),
        name="moe_combine",
    )(gates, h1, *a_planes, *b_planes)


def _rope_tables(lp_len):
    half = ROPE_DIM // 2
    pos = jnp.arange(lp_len, dtype=F32)
    inv_freq = ROPE_THETA ** (-jnp.arange(0, ROPE_DIM, 2, dtype=F32) / ROPE_DIM)
    ang = pos[:, None] * inv_freq[None, :]
    lane = jnp.arange(LANES) % HEAD_DIM
    cos = jnp.tile(jnp.cos(ang), (1, LANES // half))
    sin = jnp.tile(jnp.sin(ang), (1, LANES // half))
    c = jnp.where(lane < ROPE_DIM, cos, 1.0)
    s1 = jnp.where((lane >= half) & (lane < ROPE_DIM), sin, 0.0)
    s2 = jnp.where(lane < half, -sin, 0.0)
    return c, s1, s2


def kernel(x, meta_tokens, norm1_g, w_in, conv_w, q_norm_g, k_norm_g, lambda_q1, lambda_k1, lambda_q2, lambda_k2,
           subln_g, w_out, norm2_g, w_router_group, b_router_group, w_router_expert, b_router_expert, w_gate,
           w_up, w_down):
    batch, seq, _ = x.shape
    assert w_in.shape[0] == 1, "a single layer is supported"
    l = 0
    length = seq + N_META
    tm = TOKEN_TILE
    lp_len = -(-length // tm) * tm
    tiles_per_seq = lp_len // tm
    assert tiles_per_seq >= 2 and (length - (tiles_per_seq - 1) * tm) % 8 == 0
    qw = N_HEADS * 2 * HEAD_DIM
    lam_init = 0.8 - 0.6 * math.exp(-0.3 * l)

    reps = qw // HEAD_DIM
    gq = jnp.tile(q_norm_g[l] * (HEAD_DIM ** -0.5 * LOG2E), reps)[None, :]
    gk = jnp.tile(k_norm_g[l], reps)[None, :]
    seg = jnp.arange(qw) // HEAD_DIM
    bd = (seg[:, None] == seg[None, :]).astype(BF16)
    rope = _rope_tables(lp_len)
    hp, convy, q, k, v = _inproj(x, meta_tokens.astype(x.dtype), norm1_g[l][None, :], w_in[l].astype(BF16),
                                 conv_w[l], gq, gk, bd, *rope, tiles_per_seq=tiles_per_seq, tm=tm)

    lamp = jnp.stack([lambda_q1[l], lambda_k1[l], lambda_q2[l], lambda_k2[l]]).astype(F32)
    o = _attention(q, k, v, lamp, subln_g[l][None, :], batch=batch, lp_len=lp_len, tq=tm, lam_init=lam_init)

    lane_pad = LANES - N_GROUPS - N_EXPERTS
    wr = jnp.pad(jnp.concatenate([w_router_group[l], w_router_expert[l]], axis=1), ((0, 0), (0, lane_pad)))
    wr_hi = wr.astype(BF16)
    wr_lo = (wr - wr_hi.astype(F32)).astype(BF16)
    br = jnp.pad(jnp.concatenate([b_router_group[l], b_router_expert[l]]), (0, lane_pad))[None, :]
    ridx = jnp.arange(tm)
    upper = (ridx[:, None] < ridx[None, :]).astype(BF16)
    h1, *rest = _outproj(hp, convy, o, w_out[l].astype(BF16), norm2_g[l][None, :], wr_hi, wr_lo, br, upper,
                         tiles_per_seq=tiles_per_seq, seq_len=length)
    x_planes = rest[:ROW_PLANES]
    route, cnt = rest[ROW_PLANES:]

    counts = cnt[EXPERT_LANE0:EXPERT_LANE0 + N_EXPERTS, 0].astype(jnp.int32)
    n_blocks = -(-(batch * length * TOP_K) // MOE_BLOCK) + N_EXPERTS
    p_rows = n_blocks * MOE_BLOCK
    padded = (counts + MOE_BLOCK - 1) // MOE_BLOCK * MOE_BLOCK
    pends = jnp.cumsum(padded)
    pstarts = pends - padded

    def lookup(table, idx):
        hit = idx[:, None] == jnp.arange(N_EXPERTS, dtype=jnp.int32)[None, :]
        return jnp.sum(jnp.where(hit, table[None, :], 0), axis=1)

    blk0 = jnp.arange(n_blocks, dtype=jnp.int32) * MOE_BLOCK
    block_e = jnp.minimum(jnp.sum((pends[None, :] <= blk0[:, None]).astype(jnp.int32), axis=1), N_EXPERTS - 1)
    nvalid = jnp.clip(lookup(counts, block_e) - (blk0 - lookup(pstarts, block_e)), 0, MOE_BLOCK)

    spare_rows = -(-(batch * (lp_len - length) * TOP_K) // MOE_BLOCK) * MOE_BLOCK
    dest = _slots(pstarts.astype(jnp.int32), route, batch=batch, lp_len=lp_len, seq_len=length, p_rows=p_rows)
    dest = dest.reshape(ROUTE_COLS, batch, lp_len)

    assert (batch * lp_len) % SC_WINDOW == 0 and (batch * seq) % SC_WINDOW == 0
    xs_planes = _sc_scatter_rows(x_planes, dest[0].reshape(-1, SC_WINDOW), dest[1].reshape(-1, SC_WINDOW),
                                 p_rows + spare_rows)
    y_planes = _experts(block_e, nvalid, xs_planes, w_gate[l], w_up[l], w_down[l])

    dest_x = dest[0:TOP_K, :, N_META:length]
    a_planes, b_planes = _sc_gather_rows(y_planes, dest_x[0].reshape(-1, SC_WINDOW),
                                         dest_x[1].reshape(-1, SC_WINDOW))
    route_x = route.reshape(ROUTE_COLS, batch, lp_len)[:, :, N_META:length].reshape(ROUTE_COLS, batch * seq)
    return _combine(route_x, h1, a_planes, b_planes, batch=batch, seq=seq, lp_len=lp_len,
                    tile=_largest_tile(seq, COMBINE_TILE, LANES))
```

```python
import functools
import math

import jax
import jax.numpy as jnp
from jax import lax
from jax.experimental import pallas as pl
from jax.experimental.pallas import tpu as pltpu
from jax.experimental.pallas import tpu_sc as plsc

F32 = jnp.float32
BF16 = jnp.bfloat16

N_META = 16
N_HEADS = 4
HEAD_DIM = 64
ROPE_DIM = HEAD_DIM // 4
ROPE_THETA = 500000.0
N_GROUPS = 4
EXPERTS_PER_GROUP = 8
N_EXPERTS = N_GROUPS * EXPERTS_PER_GROUP
TOP_K = 2
EPS = 1e-6
LOG2E = 1.4426950408889634

LANES = 128
TOKEN_TILE = 640
INPROJ_UNITS = 2
OUTPROJ_UNITS = 2
ATTN_TAIL_ROWS = ((0, 256), (256, 384))
ATTN_TAIL_ROWS_WIDE = ((0, 384), (384, 256))
ATTN_HEADS_PER_STEP = 4
MOE_BLOCK = 1024
MOE_TAIL_STEP = 256
COMBINE_TILE = 1024
ROUTE_COLS = 8
ROUTE_ROWS = 64
ROW_PLANES = 4
SC_WINDOW = 128
EXPERT_LANE0 = N_GROUPS
NEG_BIG = -1e30
VMEM_LIMIT = 56 * 1024 * 1024


def _largest_tile(n, cap, mult):
    for t in range(min(cap, n), 0, -1):
        if n % t == 0 and t % mult == 0:
            return t
    raise ValueError(f"no tile for {n}")


def _cparams(n_axes):
    return pltpu.CompilerParams(dimension_semantics=("arbitrary",) * n_axes, vmem_limit_bytes=VMEM_LIMIT)


def _pack_rows(x):
    w = x.shape[1] // 2
    lo = lax.bitcast_convert_type(x[:, :w].astype(BF16).astype(F32), jnp.uint32)
    hi = lax.bitcast_convert_type(x[:, w:].astype(BF16).astype(F32), jnp.uint32)
    return lax.shift_right_logical(lo, jnp.uint32(16)) | (hi & jnp.uint32(0xFFFF0000))


def _unpack_rows(planes):
    w = jnp.concatenate(planes, axis=1)
    lo = lax.bitcast_convert_type(lax.shift_left(w, jnp.uint32(16)), F32)
    hi = lax.bitcast_convert_type(w & jnp.uint32(0xFFFF0000), F32)
    return jnp.concatenate([lo, hi], axis=1).astype(BF16)


def _inproj_kernel(x_hbm, meta_hbm, g1_ref, win_ref, convw_ref, gq_ref, gk_ref, bd_ref, *refs,
                   tiles_per_seq, seq, cw, qw, units):
    rope_refs = refs[0:3 * units]
    hp_hbm, convy_ref, q_ref, k_ref, v_ref, carry_ref, xbuf, sem, sem_out = refs[3 * units:]
    i = pl.program_id(0)
    n_steps = pl.num_programs(0)
    tm = xbuf.shape[1]
    q0 = 3 * cw
    w = convw_ref[...]
    last_rows = seq + N_META - (tiles_per_seq - 1) * tm

    def fetch(tile, slot, start):
        b = tile // tiles_per_seq
        t = tile % tiles_per_seq

        def go(src, dst):
            cp = pltpu.make_async_copy(src, dst, sem.at[slot])
            if start:
                cp.start()
            else:
                cp.wait()

        @pl.when(t == 0)
        def _():
            go(meta_hbm, xbuf.at[slot, pl.ds(0, N_META)])
            go(x_hbm.at[pl.ds(b * seq, tm - N_META)], xbuf.at[slot, pl.ds(N_META, tm - N_META)])

        @pl.when((t > 0) & (t < tiles_per_seq - 1))
        def _():
            go(x_hbm.at[pl.ds(b * seq + t * tm - N_META, tm)], xbuf.at[slot])

        @pl.when(t == tiles_per_seq - 1)
        def _():
            go(x_hbm.at[pl.ds(b * seq + t * tm - N_META, last_rows)], xbuf.at[slot, pl.ds(0, last_rows)])

    def hp_store(tile, slot):
        return pltpu.make_async_copy(xbuf.at[slot], hp_hbm.at[pl.ds(tile * tm, tm)], sem_out.at[slot])

    mine = (i % 2) * units
    other = units - mine

    @pl.when(i == 0)
    def _():
        for u in range(units):
            fetch(u, mine + u, True)

    @pl.when(i > 0)
    def _():
        for u in range(units):
            hp_store((i - 1) * units + u, other + u).wait()

    @pl.when(i + 1 < n_steps)
    def _():
        for u in range(units):
            fetch((i + 1) * units + u, other + u, True)

    for u in range(units):
        tile = i * units + u
        fetch(tile, mine + u, False)

        @pl.when(tile % tiles_per_seq == tiles_per_seq - 1)
        def _():
            xbuf[mine + u, pl.ds(last_rows, tm - last_rows), :] = jnp.zeros((tm - last_rows, xbuf.shape[2]),
                                                                           xbuf.dtype)

        hp_store(tile, mine + u).start()

    prev = carry_ref[...]

    for u in range(units):
        rows = pl.ds(u * tm, tm)
        x = xbuf[mine + u]
        ms = jnp.mean(x * x, axis=-1, keepdims=True)
        xn = (x * lax.rsqrt(ms + EPS) * g1_ref[...]).astype(BF16)

        def proj(lo, hi):
            return jnp.dot(xn, win_ref[:, lo:hi], preferred_element_type=F32)

        rc, rs1, rs2 = (ref[...] for ref in rope_refs[3 * u:3 * u + 3])

        def norm_rope(t, g_ref):
            ss = jnp.dot((t * t).astype(BF16), bd_ref[...], preferred_element_type=F32)
            tn = t * lax.rsqrt(ss * (1.0 / HEAD_DIM) + EPS) * g_ref[...]
            outs = []
            for c in range(qw // LANES):
                ch = tn[:, c * LANES:(c + 1) * LANES]
                outs.append(ch * rc + pltpu.roll(ch, ROPE_DIM // 2, axis=1) * rs1
                            + pltpu.roll(ch, LANES - ROPE_DIM // 2, axis=1) * rs2)
            return jnp.concatenate(outs, axis=1).astype(BF16)

        u_q = proj(q0, q0 + qw)
        u_k = proj(q0 + qw, q0 + 2 * qw)
        q_ref[rows, :] = norm_rope(u_q, gq_ref)
        u_cc = proj(cw, 2 * cw)
        u_cx = proj(2 * cw, 3 * cw)
        k_ref[rows, :] = norm_rope(u_k, gk_ref)
        u_v = proj(q0 + 2 * qw, win_ref.shape[1])

        z = u_cc * u_cx
        prev = jnp.where((i * units + u) % tiles_per_seq == 0, 0.0, prev)
        p1 = prev[7:8]
        p2 = prev[6:7]
        row = lax.broadcasted_iota(jnp.int32, z.shape, 0)
        z1 = jnp.where(row == 0, p1, pltpu.roll(z, 1, axis=0))
        z2 = jnp.where(row == 0, p2, jnp.where(row == 1, p1, pltpu.roll(z, 2, axis=0)))
        prev = z[tm - 8:tm]
        conv = w[0:1] * z2 + w[1:2] * z1 + w[2:3] * z
        v_ref[rows, :] = u_v.astype(BF16)
        u_cb = proj(0, cw)
        convy_ref[rows, :] = (u_cb * conv).astype(BF16)

    carry_ref[...] = prev

    @pl.when(i == n_steps - 1)
    def _():
        for u in range(units):
            hp_store(i * units + u, mine + u).wait()


def _inproj(x, meta, g1, w_in, conv_w, gq, gk, bd, rc, rs1, rs2, *, tiles_per_seq, tm):
    batch, seq, d = x.shape
    units = INPROJ_UNITS
    n = batch * tiles_per_seq * tm
    assert (batch * tiles_per_seq) % units == 0
    cw = conv_w.shape[1]
    qw = gq.shape[1]
    aw = w_in.shape[1] - 3 * cw - 2 * qw
    const = lambda i: (0, 0)
    step = lambda i: (i, 0)
    rope_specs = [pl.BlockSpec((tm, LANES), lambda i, u=u: ((i * units + u) % tiles_per_seq, 0))
                  for u in range(units) for _ in range(3)]
    kern = functools.partial(_inproj_kernel, tiles_per_seq=tiles_per_seq, seq=seq, cw=cw, qw=qw, units=units)
    return pl.pallas_call(
        kern,
        grid=(n // (units * tm),),
        in_specs=[
            pl.BlockSpec(memory_space=pl.ANY),
            pl.BlockSpec(memory_space=pl.ANY),
            pl.BlockSpec((1, d), const),
            pl.BlockSpec(w_in.shape, const),
            pl.BlockSpec(conv_w.shape, const),
            pl.BlockSpec((1, qw), const),
            pl.BlockSpec((1, qw), const),
            pl.BlockSpec(bd.shape, const),
            *rope_specs,
        ],
        out_specs=[
            pl.BlockSpec(memory_space=pl.ANY),
            pl.BlockSpec((units * tm, cw), step),
            pl.BlockSpec((units * tm, qw), step),
            pl.BlockSpec((units * tm, qw), step),
            pl.BlockSpec((units * tm, aw), step),
        ],
        out_shape=[
            jax.ShapeDtypeStruct((n, d), x.dtype),
            jax.ShapeDtypeStruct((n, cw), BF16),
            jax.ShapeDtypeStruct((n, qw), BF16),
            jax.ShapeDtypeStruct((n, qw), BF16),
            jax.ShapeDtypeStruct((n, aw), BF16),
        ],
        scratch_shapes=[pltpu.VMEM((8, cw), F32), pltpu.VMEM((2 * units, tm, d), x.dtype),
                        pltpu.SemaphoreType.DMA((2 * units,)), pltpu.SemaphoreType.DMA((2 * units,))],
        compiler_params=_cparams(1),
        name="inproj",
    )(x.reshape(batch * seq, d), meta, g1, w_in, conv_w, gq, gk, bd, *([rc, rs1, rs2] * units))


def _attn_kernel(q_ref, k_ref, v_ref, lamp_ref, sg_ref, o_ref, qs_ref, m_ref, l_ref, acc_ref, *, lam_init):
    qi = pl.program_id(2)
    tq = q_ref.shape[0]
    n_heads = q_ref.shape[1] // LANES
    n_chains = 2 * n_heads
    lane = lax.broadcasted_iota(jnp.int32, (tq, LANES), 1)
    for h in range(n_heads):
        q = q_ref[:, h * LANES:(h + 1) * LANES]
        zero = jnp.zeros_like(q)
        qs_ref[pl.ds(2 * h * tq, tq), :] = jnp.where(lane < HEAD_DIM, q, zero)
        qs_ref[pl.ds((2 * h + 1) * tq, tq), :] = jnp.where(lane >= HEAD_DIM, q, zero)

    def scores(off, width, which, r0=0, nr=None):
        h = which // 2
        nr = tq if nr is None else nr
        kc = k_ref[pl.ds(off, width), h * LANES:(h + 1) * LANES]
        return lax.dot_general(qs_ref[pl.ds(which * tq + r0, nr), :], kc, (((1,), (1,)), ((), ())),
                               preferred_element_type=F32)

    def update(off, width, which, s, first, r0=0, nr=None, diag=None):
        h = which // 2
        nr = tq if nr is None else nr
        vc = jnp.concatenate([v_ref[pl.ds(off, width), h * LANES:(h + 1) * LANES],
                              jnp.ones((width, LANES), BF16)], axis=1)
        rows = pl.ds(which * tq + r0, nr)
        if diag is not None:
            r = lax.broadcasted_iota(jnp.int32, s.shape, 0)
            c = lax.broadcasted_iota(jnp.int32, s.shape, 1)
            s = jnp.where(c <= r + (r0 + diag), s, NEG_BIG)
        m_prev = jnp.where(first, NEG_BIG, m_ref[rows, :])
        l_prev = jnp.where(first, 0.0, l_ref[rows, :])
        acc_prev = jnp.where(first, 0.0, acc_ref[rows, :])
        m_new = jnp.maximum(m_prev, jnp.max(s, axis=-1, keepdims=True))
        alpha = jnp.exp2(m_prev - m_new)
        p = jnp.exp2((s - jnp.tile(m_new, (1, width // LANES))).astype(BF16))
        pv = jnp.dot(p, vc, preferred_element_type=F32)
        l_ref[rows, :] = alpha * l_prev + pv[:, LANES:]
        acc_ref[rows, :] = alpha * acc_prev + pv[:, :LANES]
        m_ref[rows, :] = m_new

    def run_pieces(pieces, first, chain_done=None):
        s_next = scores(pieces[0][1], pieces[0][2], pieces[0][0], pieces[0][3], pieces[0][4])
        for n, (c, off, width, r0, nr, diag) in enumerate(pieces):
            s = s_next
            if n + 1 < len(pieces):
                c2, off2, width2, r02, nr2, _ = pieces[n + 1]
                s_next = scores(off2, width2, c2, r02, nr2)
            update(off, width, c, s, first, r0, nr, diag)
            if chain_done is not None and (n + 1 == len(pieces) or pieces[n + 1][0] != c):
                chain_done(c)

    def chunk(off, width, first):
        run_pieces([(c, off, width, 0, tq, None) for c in range(n_chains)], first)

    def tail_chunk(off, width, row_groups, first):
        pieces = [(c, off, width - tq + r0 + nr, r0, nr, width - tq)
                  for c in range(n_chains) for r0, nr in row_groups]
        run_pieces(pieces, first, chain_done)

    def chain_done(c):
        if c % 2 == 0:
            return
        h = c // 2
        lp = lamp_ref[...]
        lam = (jnp.exp(jnp.sum(lp[0:1] * lp[1:2], axis=-1, keepdims=True))
               - jnp.exp(jnp.sum(lp[2:3] * lp[3:4], axis=-1, keepdims=True)) + lam_init)
        rows = pl.ds(2 * h * tq, 2 * tq)
        o_all = acc_ref[rows, :] / l_ref[rows, :]
        o = o_all[0:tq] - lam * o_all[tq:2 * tq]
        ms = jnp.mean(o * o, axis=-1, keepdims=True)
        o_ref[:, h * LANES:(h + 1) * LANES] = (o * lax.rsqrt(ms + EPS) * sg_ref[...]
                                               * (1.0 - lam_init)).astype(BF16)

    wide = 2 * tq

    def body(j, carry):
        chunk(pl.multiple_of(j * wide, wide), wide, j == 0)
        return carry

    lax.fori_loop(0, qi // 2, body, 0)
    odd = qi % 2 == 1

    @pl.when(odd)
    def _():
        tail_chunk(pl.multiple_of((qi - 1) * tq, tq), wide, ATTN_TAIL_ROWS_WIDE, qi < 2)

    @pl.when(jnp.logical_not(odd))
    def _():
        tail_chunk(pl.multiple_of(qi * tq, tq), tq, ATTN_TAIL_ROWS, qi < 2)


def _attention(q, k, v, lamp, sg, *, batch, lp_len, tq, lam_init):
    n, qw = q.shape
    nq = lp_len // tq
    hw = ATTN_HEADS_PER_STEP * LANES
    chains = 2 * ATTN_HEADS_PER_STEP
    kern = functools.partial(_attn_kernel, lam_init=lam_init)
    return pl.pallas_call(
        kern,
        grid=(batch, qw // hw, nq),
        in_specs=[
            pl.BlockSpec((tq, hw), lambda b, h, i: (b * nq + i, h)),
            pl.BlockSpec((lp_len, hw), lambda b, h, i: (b, h)),
            pl.BlockSpec((lp_len, hw), lambda b, h, i: (b, h)),
            pl.BlockSpec(lamp.shape, lambda b, h, i: (0, 0)),
            pl.BlockSpec(sg.shape, lambda b, h, i: (0, 0)),
        ],
        out_specs=pl.BlockSpec((tq, hw), lambda b, h, i: (b * nq + i, h)),
        out_shape=jax.ShapeDtypeStruct((n, v.shape[1]), BF16),
        scratch_shapes=[
            pltpu.VMEM((chains * tq, LANES), BF16),
            pltpu.VMEM((chains * tq, LANES), F32),
            pltpu.VMEM((chains * tq, LANES), F32),
            pltpu.VMEM((chains * tq, LANES), F32),
        ],
        compiler_params=_cparams(3),
        name="diffattn",
    )(q, k, v, lamp, sg)


def _outproj_kernel(hp_ref, cy_ref, o_ref, wout_ref, g2_ref, wrh_ref, wrl_ref, br_ref, upper_ref,
                    h1_ref, xp0_ref, xp1_ref, xp2_ref, xp3_ref, route_ref, cnt_ref, run_ref,
                    *, tiles_per_seq, seq_len):
    i = pl.program_id(0)
    tm = upper_ref.shape[0]
    n_units = hp_ref.shape[0] // tm

    @pl.when(i == 0)
    def _():
        run_ref[...] = jnp.zeros_like(run_ref)

    def project(rows):
        mix = jnp.concatenate([cy_ref[rows, :], o_ref[rows, :]], axis=1)
        h1 = hp_ref[rows, :] + jnp.dot(mix, wout_ref[...], preferred_element_type=F32)
        h1_ref[rows, :] = h1
        ms = jnp.mean(h1 * h1, axis=-1, keepdims=True)
        xn = h1 * lax.rsqrt(ms + EPS) * g2_ref[...]
        xw = _pack_rows(xn)
        for c, ref in enumerate((xp0_ref, xp1_ref, xp2_ref, xp3_ref)):
            ref[rows, :] = xw[:, c * LANES:(c + 1) * LANES]
        return xn

    def router_logits(xn):
        x_hi = xn.astype(BF16)
        x_lo = (xn - x_hi.astype(F32)).astype(BF16)
        hi_both = jnp.dot(x_hi, jnp.concatenate([wrh_ref[...], wrl_ref[...]], axis=1),
                          preferred_element_type=F32)
        return (hi_both[:, :LANES] + hi_both[:, LANES:]
                + jnp.dot(x_lo, wrh_ref[...], preferred_element_type=F32) + br_ref[...])

    def route(u, logits, run):
        lt = logits.T[0:ROUTE_ROWS, :]
        row = lax.broadcasted_iota(jnp.int32, lt.shape, 0)
        big = jnp.int32(4 * LANES)

        def first_argmax(vals, vmax):
            return jnp.min(jnp.where(vals == vmax, row, big), axis=0, keepdims=True)

        gl = jnp.where(row < N_GROUPS, lt, NEG_BIG)
        gmax = jnp.max(gl, axis=0, keepdims=True)
        g_val = 1.0 / jnp.sum(jnp.exp(gl - gmax), axis=0, keepdims=True)
        g_idx = first_argmax(gl, gmax)
        lo = EXPERT_LANE0 + EXPERTS_PER_GROUP * g_idx
        el = jnp.where((row >= lo) & (row < lo + EXPERTS_PER_GROUP), lt, NEG_BIG)
        m1 = jnp.max(el, axis=0, keepdims=True)
        i1 = first_argmax(el, m1)
        el2 = jnp.where(row == i1, NEG_BIG, el)
        m2 = jnp.max(el2, axis=0, keepdims=True)
        i2 = first_argmax(el2, m2)
        r = jnp.exp(m2 - m1)
        gate1 = g_val / (1.0 + r)
        gate2 = g_val * r / (1.0 + r)

        pos = ((i * n_units + u) % tiles_per_seq) * tm + lax.broadcasted_iota(jnp.int32, (1, tm), 1)
        valid = pos < seq_len
        oh1 = jnp.where(valid & (row == i1), 1.0, 0.0)
        oh2 = jnp.where(valid & (row == i2), 1.0, 0.0)
        pre = jnp.dot(jnp.concatenate([oh1, oh2], axis=0).astype(BF16), upper_ref[...],
                      preferred_element_type=F32)
        tot1 = jnp.sum(oh1, axis=1, keepdims=True)
        tot2 = jnp.sum(oh2, axis=1, keepdims=True)
        run_t = jnp.tile(run, (1, tm // LANES))
        rank1 = jnp.sum(oh1 * (pre[:ROUTE_ROWS] + run_t), axis=0, keepdims=True)
        rank2 = jnp.sum(oh2 * (pre[ROUTE_ROWS:] + run_t + tot1), axis=0, keepdims=True)

        e1 = (i1 - EXPERT_LANE0).astype(F32)
        e2 = (i2 - EXPERT_LANE0).astype(F32)
        r8 = lax.broadcasted_iota(jnp.int32, (ROUTE_COLS, tm), 0)
        route_ref[:, u * tm:(u + 1) * tm] = jnp.where(r8 == 0, e1, jnp.where(r8 == 1, e2, jnp.where(
            r8 == 2, gate1, jnp.where(r8 == 3, gate2, jnp.where(r8 == 4, rank1, jnp.where(r8 == 5, rank2, 0.0))))))
        return run + tot1 + tot2

    normed = [project(pl.ds(u * tm, tm)) for u in range(n_units)]
    all_logits = [router_logits(xn) for xn in normed]
    run = run_ref[...]
    for u, logits in enumerate(all_logits):
        run = route(u, logits, run)
    run_ref[...] = run
    cnt_ref[...] = run


def _outproj(hp, convy, o, w_out, g2, wr_hi, wr_lo, br, upper, *, tiles_per_seq, seq_len):
    n, d = hp.shape
    tm = OUTPROJ_UNITS * upper.shape[0]
    assert n % tm == 0
    const = lambda i: (0, 0)
    tile = lambda i: (i, 0)
    kern = functools.partial(_outproj_kernel, tiles_per_seq=tiles_per_seq, seq_len=seq_len)
    return pl.pallas_call(
        kern,
        grid=(n // tm,),
        in_specs=[
            pl.BlockSpec((tm, d), tile),
            pl.BlockSpec((tm, convy.shape[1]), tile),
            pl.BlockSpec((tm, o.shape[1]), tile),
            pl.BlockSpec(w_out.shape, const),
            pl.BlockSpec((1, d), const),
            pl.BlockSpec(wr_hi.shape, const),
            pl.BlockSpec(wr_lo.shape, const),
            pl.BlockSpec((1, LANES), const),
            pl.BlockSpec(upper.shape, const),
        ],
        out_specs=[
            pl.BlockSpec((tm, d), tile),
            *[pl.BlockSpec((tm, LANES), tile)] * ROW_PLANES,
            pl.BlockSpec((ROUTE_COLS, tm), lambda i: (0, i)),
            pl.BlockSpec((ROUTE_ROWS, LANES), const),
        ],
        out_shape=[
            jax.ShapeDtypeStruct((n, d), F32),
            *[jax.ShapeDtypeStruct((n, LANES), jnp.uint32)] * ROW_PLANES,
            jax.ShapeDtypeStruct((ROUTE_COLS, n), F32),
            jax.ShapeDtypeStruct((ROUTE_ROWS, LANES), F32),
        ],
        scratch_shapes=[pltpu.VMEM((ROUTE_ROWS, LANES), F32)],
        compiler_params=_cparams(1),
        name="outproj_router",
    )(hp, convy, o, w_out, g2, wr_hi, wr_lo, br, upper)


def _slots_kernel(pstart_ref, route_ref, dest_ref, *, seq_len, p_rows):
    b = pl.program_id(0)
    route = route_ref[...]
    eid = route.astype(jnp.int32)
    start = jnp.zeros_like(eid)
    for e in range(N_EXPERTS):
        start = jnp.where(eid == e, pstart_ref[e], start)
    rank = pltpu.roll(route, ROUTE_COLS - 2 * TOP_K, axis=0).astype(jnp.int32)
    k = lax.broadcasted_iota(jnp.int32, route.shape, 0)
    pos = lax.broadcasted_iota(jnp.int32, route.shape, 1)
    n_pad = route.shape[1] - seq_len
    spare = p_rows + (b * n_pad + (pos - seq_len)) * TOP_K + k
    dest_ref[...] = jnp.where(pos < seq_len, start + rank, spare)


def _slots(pstarts, route, *, batch, lp_len, seq_len, p_rows):
    kern = functools.partial(_slots_kernel, seq_len=seq_len, p_rows=p_rows)
    grid_spec = pltpu.PrefetchScalarGridSpec(
        num_scalar_prefetch=1,
        grid=(batch,),
        in_specs=[pl.BlockSpec((ROUTE_COLS, lp_len), lambda b, ps: (0, b))],
        out_specs=pl.BlockSpec((ROUTE_COLS, lp_len), lambda b, ps: (0, b)),
    )
    return pl.pallas_call(
        kern,
        grid_spec=grid_spec,
        out_shape=jax.ShapeDtypeStruct(route.shape, jnp.int32),
        compiler_params=_cparams(1),
        name="moe_slots",
    )(pstarts, route)


def _sc_workers():
    info = plsc.get_sparse_core_info()
    return info.num_cores, info.num_cores * info.num_subcores


def _sc_scatter_rows(planes, idx_a, idx_b, out_rows):
    n_win = idx_a.shape[0]
    n_cores, n_workers = _sc_workers()
    trips = -(-n_win // n_workers)
    mesh = plsc.VectorSubcoreMesh(core_axis_name="c", subcore_axis_name="s")

    def body(*refs):
        xs = refs[0:ROW_PLANES]
        ia_hbm, ib_hbm = refs[ROW_PLANES:ROW_PLANES + 2]
        outs = refs[ROW_PLANES + 2:2 * ROW_PLANES + 2]
        ia_v, ib_v, buf, sem = refs[2 * ROW_PLANES + 2:]
        wid = lax.axis_index("s") * n_cores + lax.axis_index("c")

        def step(t, carry):
            g = wid + t * n_workers

            @pl.when(g < n_win)
            def _():
                row0 = pl.multiple_of(g * SC_WINDOW, SC_WINDOW)
                loads = [pltpu.async_copy(ia_hbm.at[g], ia_v, sem), pltpu.async_copy(ib_hbm.at[g], ib_v, sem)]
                loads += [pltpu.async_copy(xs[c].at[pl.ds(row0, SC_WINDOW)], buf.at[c], sem)
                          for c in range(ROW_PLANES)]
                for cp in loads:
                    cp.wait()
                stores = [pltpu.async_copy(buf.at[c], outs[c].at[iv], sem)
                          for c in range(ROW_PLANES) for iv in (ia_v, ib_v)]
                for cp in stores:
                    cp.wait()

            return carry

        lax.fori_loop(0, trips, step, 0)

    kern = pl.kernel(
        body,
        out_type=[jax.ShapeDtypeStruct((out_rows, LANES), jnp.uint32)] * ROW_PLANES,
        mesh=mesh,
        scratch_types=[
            pltpu.VMEM((SC_WINDOW,), jnp.int32),
            pltpu.VMEM((SC_WINDOW,), jnp.int32),
            pltpu.VMEM((ROW_PLANES, SC_WINDOW, LANES), jnp.uint32),
            pltpu.SemaphoreType.DMA,
        ],
        name="moe_dispatch_sc",
    )
    return kern(*planes, idx_a, idx_b)


def _sc_gather_rows(planes, idx_a, idx_b):
    n_win = idx_a.shape[0]
    n_cores, n_workers = _sc_workers()
    trips = -(-n_win // n_workers)
    mesh = plsc.VectorSubcoreMesh(core_axis_name="c", subcore_axis_name="s")

    def body(*refs):
        ys = refs[0:ROW_PLANES]
        ia_hbm, ib_hbm = refs[ROW_PLANES:ROW_PLANES + 2]
        outs_a = refs[ROW_PLANES + 2:2 * ROW_PLANES + 2]
        outs_b = refs[2 * ROW_PLANES + 2:3 * ROW_PLANES + 2]
        ia_v, ib_v, buf, sem = refs[3 * ROW_PLANES + 2:]
        wid = lax.axis_index("s") * n_cores + lax.axis_index("c")

        def step(t, carry):
            g = wid + t * n_workers

            @pl.when(g < n_win)
            def _():
                row0 = pl.multiple_of(g * SC_WINDOW, SC_WINDOW)
                idx_loads = [pltpu.async_copy(ia_hbm.at[g], ia_v, sem), pltpu.async_copy(ib_hbm.at[g], ib_v, sem)]
                for cp in idx_loads:
                    cp.wait()
                for iv, outs in ((ia_v, outs_a), (ib_v, outs_b)):
                    loads = [pltpu.async_copy(ys[c].at[iv], buf.at[c], sem) for c in range(ROW_PLANES)]
                    for cp in loads:
                        cp.wait()
                    stores = [pltpu.async_copy(buf.at[c], outs[c].at[pl.ds(row0, SC_WINDOW)], sem)
                              for c in range(ROW_PLANES)]
                    for cp in stores:
                        cp.wait()

            return carry

        lax.fori_loop(0, trips, step, 0)

    n_rows = n_win * SC_WINDOW
    kern = pl.kernel(
        body,
        out_type=[jax.ShapeDtypeStruct((n_rows, LANES), jnp.uint32)] * (2 * ROW_PLANES),
        mesh=mesh,
        scratch_types=[
            pltpu.VMEM((SC_WINDOW,), jnp.int32),
            pltpu.VMEM((SC_WINDOW,), jnp.int32),
            pltpu.VMEM((ROW_PLANES, SC_WINDOW, LANES), jnp.uint32),
            pltpu.SemaphoreType.DMA,
        ],
        name="moe_gather_sc",
    )
    res = kern(*planes, idx_a, idx_b)
    return res[:ROW_PLANES], res[ROW_PLANES:]


def _experts_kernel(be_ref, nv_ref, first_ref, slot_ref, nxt_ref, x0_ref, x1_ref, x2_ref, x3_ref,
                    wg_hbm, wu_hbm, wd_hbm, y0_ref, y1_ref, y2_ref, y3_ref,
                    wgf_ref, wuf_ref, wdf_ref, wgb_ref, wub_ref, wdb_ref, sem):
    i = pl.program_id(0)
    e = be_ref[i]
    slot = slot_ref[i]
    y_refs = (y0_ref, y1_ref, y2_ref, y3_ref)

    def weight_copies(expert, s):
        return [pltpu.make_async_copy(hbm.at[expert], stage.at[s], sem.at[s, j])
                for j, (hbm, stage) in enumerate(((wg_hbm, wgf_ref), (wu_hbm, wuf_ref), (wd_hbm, wdf_ref)))]

    @pl.when(i == 0)
    def _():
        for cp in weight_copies(e, slot):
            cp.start()

    is_first = first_ref[i] == 1

    @pl.when(is_first)
    def _():
        for cp in weight_copies(e, slot):
            cp.wait()
        nxt = nxt_ref[i]

        @pl.when(nxt >= 0)
        def _():
            for cp in weight_copies(nxt, 1 - slot):
                cp.start()

    nv = nv_ref[i]

    def weights(stage_ref, bf_ref, fresh):
        if fresh:
            bf_ref[...] = stage_ref[slot].astype(BF16)
        return bf_ref[...]

    def mlp(rows, fresh):
        xs = _unpack_rows([r[rows, :] for r in (x0_ref, x1_ref, x2_ref, x3_ref)])
        row = lax.broadcasted_iota(jnp.int32, xs.shape, 0)
        x = jnp.where(row < nv, xs, jnp.zeros_like(xs))
        hg = jnp.dot(x, weights(wgf_ref, wgb_ref, fresh), preferred_element_type=F32)
        hu = jnp.dot(x, weights(wuf_ref, wub_ref, fresh), preferred_element_type=F32)
        hid = (hg / (1.0 + jnp.exp(-hg)) * hu).astype(BF16)
        yw = _pack_rows(jnp.dot(hid, weights(wdf_ref, wdb_ref, fresh), preferred_element_type=F32))
        for c, ref in enumerate(y_refs):
            ref[rows, :] = yw[:, c * LANES:(c + 1) * LANES]

    def clear(rows):
        for ref in y_refs:
            ref[rows, :] = jnp.zeros((rows.size, LANES), ref.dtype)

    for piece in range(MOE_BLOCK // MOE_TAIL_STEP + 1):
        used = piece * MOE_TAIL_STEP
        in_piece = (nv > used - MOE_TAIL_STEP) & (nv <= used)

        for fresh in (False, True):
            if fresh and used == 0:
                continue

            @pl.when(in_piece & (is_first if fresh else jnp.logical_not(is_first)) if used > 0 else in_piece)
            def _():
                if used > 0:
                    mlp(pl.ds(0, used), fresh)
                if used < MOE_BLOCK:
                    clear(pl.ds(used, MOE_BLOCK - used))


def _experts(block_e, nvalid, xs_planes, w_gate, w_up, w_down):
    n_blocks = block_e.shape[0]
    _, d, ff = w_gate.shape
    first = jnp.concatenate([jnp.ones((1,), jnp.int32), (block_e[1:] != block_e[:-1]).astype(jnp.int32)])
    slot = (jnp.cumsum(first) - 1) % 2
    later = jnp.where(block_e[None, :] > block_e[:, None], block_e[None, :], N_EXPERTS)
    nxt = jnp.min(later, axis=1)
    nxt = jnp.where(nxt == N_EXPERTS, -1, nxt).astype(jnp.int32)
    blk = lambda i, *_: (i, 0)
    grid_spec = pltpu.PrefetchScalarGridSpec(
        num_scalar_prefetch=5,
        grid=(n_blocks,),
        in_specs=[
            *[pl.BlockSpec((MOE_BLOCK, LANES), blk)] * ROW_PLANES,
            pl.BlockSpec(memory_space=pl.ANY),
            pl.BlockSpec(memory_space=pl.ANY),
            pl.BlockSpec(memory_space=pl.ANY),
        ],
        out_specs=[pl.BlockSpec((MOE_BLOCK, LANES), blk)] * ROW_PLANES,
        scratch_shapes=[
            pltpu.VMEM((2, d, ff), F32),
            pltpu.VMEM((2, d, ff), F32),
            pltpu.VMEM((2, ff, d), F32),
            pltpu.VMEM((d, ff), BF16),
            pltpu.VMEM((d, ff), BF16),
            pltpu.VMEM((ff, d), BF16),
            pltpu.SemaphoreType.DMA((2, 3)),
        ],
    )
    return pl.pallas_call(
        _experts_kernel,
        grid_spec=grid_spec,
        out_shape=[jax.ShapeDtypeStruct((n_blocks * MOE_BLOCK, LANES), jnp.uint32)] * ROW_PLANES,
        compiler_params=_cparams(1),
        name="moe_experts",
    )(block_e, nvalid, first, slot.astype(jnp.int32), nxt, *xs_planes, w_gate, w_up, w_down)


def _combine_kernel(route_ref, h1_hbm, *refs, lp_len, tile):
    a_refs = refs[0:ROW_PLANES]
    b_refs = refs[ROW_PLANES:2 * ROW_PLANES]
    out_ref, hbuf, sem_h = refs[2 * ROW_PLANES:]
    nt = pl.num_programs(1)
    step = pl.program_id(0) * nt + pl.program_id(1)
    last = pl.num_programs(0) * nt - 1

    def h_copy(s, slot):
        start = (s // nt) * lp_len + N_META + (s % nt) * tile
        return pltpu.make_async_copy(h1_hbm.at[pl.ds(start, tile), :], hbuf.at[slot], sem_h.at[slot])

    slot = step % 2

    @pl.when(step == 0)
    def _():
        h_copy(step, slot).start()

    @pl.when(step < last)
    def _():
        h_copy(step + 1, 1 - slot).start()

    ya = _unpack_rows([r[...] for r in a_refs]).astype(F32)
    yb = _unpack_rows([r[...] for r in b_refs]).astype(F32)
    g = jnp.concatenate([route_ref[...]] * (LANES // ROUTE_COLS), axis=0).T
    moe = g[:, TOP_K:TOP_K + 1] * ya + g[:, TOP_K + 1:TOP_K + 2] * yb
    h_copy(step, slot).wait()
    out_ref[0] = hbuf[slot] + moe


def _combine(gates, h1, a_planes, b_planes, *, batch, seq, lp_len, tile):
    d = h1.shape[1]
    nt = seq // tile
    kern = functools.partial(_combine_kernel, lp_len=lp_len, tile=tile)
    rows = lambda b, i: (b * nt + i, 0)
    return pl.pallas_call(
        kern,
        grid=(batch, nt),
        in_specs=[
            pl.BlockSpec((ROUTE_COLS, tile), lambda b, i: (0, b * nt + i)),
            pl.BlockSpec(memory_space=pl.ANY),
            *[pl.BlockSpec((tile, LANES), rows)] * (2 * ROW_PLANES),
        ],
        out_specs=pl.BlockSpec((1, tile, d), lambda b, i: (b, i, 0)),
        out_shape=jax.ShapeDtypeStruct((batch, seq, d), F32),
        scratch_shapes=[pltpu.VMEM((2, tile, d), F32), pltpu.SemaphoreType.DMA((2,))],
        compiler_params=_cparams(2),
        name="moe_combine",
    )(gates, h1, *a_planes, *b_planes)


def _rope_tables(lp_len):
    half = ROPE_DIM // 2
    pos = jnp.arange(lp_len, dtype=F32)
    inv_freq = ROPE_THETA ** (-jnp.arange(0, ROPE_DIM, 2, dtype=F32) / ROPE_DIM)
    ang = pos[:, None] * inv_freq[None, :]
    lane = jnp.arange(LANES) % HEAD_DIM
    cos = jnp.tile(jnp.cos(ang), (1, LANES // half))
    sin = jnp.tile(jnp.sin(ang), (1, LANES // half))
    c = jnp.where(lane < ROPE_DIM, cos, 1.0)
    s1 = jnp.where((lane >= half) & (lane < ROPE_DIM), sin, 0.0)
    s2 = jnp.where(lane < half, -sin, 0.0)
    return c, s1, s2


def kernel(x, meta_tokens, norm1_g, w_in, conv_w, q_norm_g, k_norm_g, lambda_q1, lambda_k1, lambda_q2, lambda_k2,
           subln_g, w_out, norm2_g, w_router_group, b_router_group, w_router_expert, b_router_expert, w_gate,
           w_up, w_down):
    batch, seq, _ = x.shape
    assert w_in.shape[0] == 1, "a single layer is supported"
    l = 0
    length = seq + N_META
    tm = TOKEN_TILE
    lp_len = -(-length // tm) * tm
    tiles_per_seq = lp_len // tm
    assert tiles_per_seq >= 2 and (length - (tiles_per_seq - 1) * tm) % 8 == 0
    qw = N_HEADS * 2 * HEAD_DIM
    lam_init = 0.8 - 0.6 * math.exp(-0.3 * l)

    reps = qw // HEAD_DIM
    gq = jnp.tile(q_norm_g[l] * (HEAD_DIM ** -0.5 * LOG2E), reps)[None, :]
    gk = jnp.tile(k_norm_g[l], reps)[None, :]
    seg = jnp.arange(qw) // HEAD_DIM
    bd = (seg[:, None] == seg[None, :]).astype(BF16)
    rope = _rope_tables(lp_len)
    hp, convy, q, k, v = _inproj(x, meta_tokens.astype(x.dtype), norm1_g[l][None, :], w_in[l].astype(BF16),
                                 conv_w[l], gq, gk, bd, *rope, tiles_per_seq=tiles_per_seq, tm=tm)

    lamp = jnp.stack([lambda_q1[l], lambda_k1[l], lambda_q2[l], lambda_k2[l]]).astype(F32)
    o = _attention(q, k, v, lamp, subln_g[l][None, :], batch=batch, lp_len=lp_len, tq=tm, lam_init=lam_init)

    lane_pad = LANES - N_GROUPS - N_EXPERTS
    wr = jnp.pad(jnp.concatenate([w_router_group[l], w_router_expert[l]], axis=1), ((0, 0), (0, lane_pad)))
    wr_hi = wr.astype(BF16)
    wr_lo = (wr - wr_hi.astype(F32)).astype(BF16)
    br = jnp.pad(jnp.concatenate([b_router_group[l], b_router_expert[l]]), (0, lane_pad))[None, :]
    ridx = jnp.arange(tm)
    upper = (ridx[:, None] < ridx[None, :]).astype(BF16)
    h1, *rest = _outproj(hp, convy, o, w_out[l].astype(BF16), norm2_g[l][None, :], wr_hi, wr_lo, br, upper,
                         tiles_per_seq=tiles_per_seq, seq_len=length)
    x_planes = rest[:ROW_PLANES]
    route, cnt = rest[ROW_PLANES:]

    counts = cnt[EXPERT_LANE0:EXPERT_LANE0 + N_EXPERTS, 0].astype(jnp.int32)
    n_blocks = -(-(batch * length * TOP_K) // MOE_BLOCK) + N_EXPERTS
    p_rows = n_blocks * MOE_BLOCK
    padded = (counts + MOE_BLOCK - 1) // MOE_BLOCK * MOE_BLOCK
    pends = jnp.cumsum(padded)
    pstarts = pends - padded

    def lookup(table, idx):
        hit = idx[:, None] == jnp.arange(N_EXPERTS, dtype=jnp.int32)[None, :]
        return jnp.sum(jnp.where(hit, table[None, :], 0), axis=1)

    blk0 = jnp.arange(n_blocks, dtype=jnp.int32) * MOE_BLOCK
    block_e = jnp.minimum(jnp.sum((pends[None, :] <= blk0[:, None]).astype(jnp.int32), axis=1), N_EXPERTS - 1)
    nvalid = jnp.clip(lookup(counts, block_e) - (blk0 - lookup(pstarts, block_e)), 0, MOE_BLOCK)

    spare_rows = -(-(batch * (lp_len - length) * TOP_K) // MOE_BLOCK) * MOE_BLOCK
    dest = _slots(pstarts.astype(jnp.int32), route, batch=batch, lp_len=lp_len, seq_len=length, p_rows=p_rows)
    dest = dest.reshape(ROUTE_COLS, batch, lp_len)

    assert (batch * lp_len) % SC_WINDOW == 0 and (batch * seq) % SC_WINDOW == 0
    xs_planes = _sc_scatter_rows(x_planes, dest[0].reshape(-1, SC_WINDOW), dest[1].reshape(-1, SC_WINDOW),
                                 p_rows + spare_rows)
    y_planes = _experts(block_e, nvalid, xs_planes, w_gate[l], w_up[l], w_down[l])

    dest_x = dest[0:TOP_K, :, N_META:length]
    a_planes, b_planes = _sc_gather_rows(y_planes, dest_x[0].reshape(-1, SC_WINDOW),
                                         dest_x[1].reshape(-1, SC_WINDOW))
    route_x = route.reshape(ROUTE_COLS, batch, lp_len)[:, :, N_META:length].reshape(ROUTE_COLS, batch * seq)
    return _combine(route_x, h1, a_planes, b_planes, batch=batch, seq=seq, lp_len=lp_len,
                    tile=_largest_tile(seq, COMBINE_TILE, LANES))
```
